```python
import jax, jax.numpy as jnp
from jax import lax
import numpy as np

D_MODEL = 1024
BATCH = 8
SEQ = 4096
DEPTH = 4

N_MIXERS = 2
N_A = (DEPTH + 1) // 2
N_B = DEPTH // 2

D_RNN = 3 * D_MODEL // 2
LRU_HEADS = 12
LRU_BW = D_RNN // LRU_HEADS
CONV_WIDTH = 4
LRU_C = 8.0

D_POOL = D_MODEL
POOL_WINDOWS = (2, 4, 8, 16)
POOL_GROUPS = len(POOL_WINDOWS)
POOL_GW = D_POOL // POOL_GROUPS

D_FF = 4 * D_MODEL
PLE_DIM = 256
ALPHA = (2 * DEPTH) ** 0.25
BETA = (8 * DEPTH) ** (-0.25)
LN_EPS = 1e-5

kernel_name = "hybrid_rglru_pool_deepnorm_trunk"


def layer_norm(x, g, b):
    xf = x.astype(jnp.float32)
    mu = jnp.mean(xf, axis=-1, keepdims=True)
    var = jnp.mean(jnp.square(xf - mu), axis=-1, keepdims=True)
    y = (xf - mu) * lax.rsqrt(var + LN_EPS)
    return (y * g.astype(jnp.float32) + b.astype(jnp.float32)).astype(x.dtype)


def causal_depthwise_conv(u, w, b):
    s = u.shape[1]
    up = jnp.pad(u, ((0, 0), (CONV_WIDTH - 1, 0), (0, 0)))
    out = b
    for k in range(CONV_WIDTH):
        out = out + up[:, k:k + s] * w[k]
    return out


def _lin_rec_combine(left, right):
    a1, b1 = left
    a2, b2 = right
    return a1 * a2, a2 * b1 + b2


def rg_lru(u, wa, ba, wx, bx, lam):
    bsz, s, _ = u.shape
    uh = u.reshape(bsz, s, LRU_HEADS, LRU_BW)
    r = jax.nn.sigmoid(jnp.einsum('bshi,hij->bshj', uh, wa).reshape(bsz, s, D_RNN) + ba)
    ig = jax.nn.sigmoid(jnp.einsum('bshi,hij->bshj', uh, wx).reshape(bsz, s, D_RNN) + bx)
    log_a = -LRU_C * r.astype(jnp.float32) * jax.nn.softplus(-lam.astype(jnp.float32))
    a = jnp.exp(log_a)
    mult = jnp.sqrt(-jnp.expm1(2.0 * log_a))
    mult = mult.at[:, 0].set(1.0)
    bterm = mult * (ig * u).astype(jnp.float32)
    _, h = lax.associative_scan(_lin_rec_combine, (a, bterm), axis=1)
    return h.astype(u.dtype)


def recurrent_mixer(x, w_in, conv_w, conv_b, wa, ba, wx, bx, lam, w_out):
    proj = x @ w_in
    u, y = proj[..., :D_RNN], proj[..., D_RNN:]
    u = causal_depthwise_conv(u, conv_w, conv_b)
    h = rg_lru(u, wa, ba, wx, bx, lam)
    return (h * jax.nn.gelu(y)) @ w_out


def pooling_mixer(x, w_in, w_grp, b_grp, scale, w_out):
    u = x @ w_in
    s = u.shape[1]
    pos = jnp.arange(s, dtype=jnp.int32)
    outs = []
    for g, w in enumerate(POOL_WINDOWS):
        ug = u[..., g * POOL_GW:(g + 1) * POOL_GW].astype(jnp.float32)
        cs = jnp.cumsum(ug, axis=1)
        cs_prev = jnp.pad(cs, ((0, 0), (w, 0), (0, 0)))[:, :s]
        cnt = jnp.minimum(pos + 1, w).astype(jnp.float32)[None, :, None]
        outs.append((cs - cs_prev) / cnt - ug)
    z = jnp.stack(outs, axis=2).astype(u.dtype)
    z = jnp.einsum('bsgi,gij->bsgj', z, w_grp).reshape(u.shape) + b_grp
    return (z * scale) @ w_out


def sq_relu_mlp(x, w1, w2):
    h = jax.nn.relu(x @ w1)
    return (h * h) @ w2


def _fwd_setup_inputs(seed: int = 0) -> dict:
    key = jax.random.key(seed)
    ks = jax.random.split(key, 26)
    f32 = jnp.float32

    def nrm(k, shape, scale):
        return jax.random.normal(k, shape, f32) * scale

    a_c = jax.random.uniform(ks[8], (N_A, D_RNN), f32, minval=0.9, maxval=0.999)
    a0 = a_c ** (1.0 / LRU_C)
    lam = jnp.log(a0) - jnp.log1p(-a0)
    return {
        "x": nrm(ks[0], (BATCH, SEQ, D_MODEL), 1.0),
        "p": nrm(ks[1], (DEPTH, BATCH, SEQ, PLE_DIM), 1.0),
        "lru_w_in": nrm(ks[2], (N_A, D_MODEL, 2 * D_RNN), D_MODEL ** -0.5),
        "lru_conv_w": nrm(ks[3], (N_A, CONV_WIDTH, D_RNN), CONV_WIDTH ** -0.5),
        "lru_conv_b": nrm(ks[4], (N_A, D_RNN), 0.01),
        "lru_wa": nrm(ks[5], (N_A, LRU_HEADS, LRU_BW, LRU_BW), LRU_BW ** -0.5),
        "lru_ba": nrm(ks[6], (N_A, D_RNN), 0.01),
        "lru_wx": nrm(ks[7], (N_A, LRU_HEADS, LRU_BW, LRU_BW), LRU_BW ** -0.5),
        "lru_bx": nrm(ks[9], (N_A, D_RNN), 0.01),
        "lru_lambda": lam,
        "lru_w_out": nrm(ks[10], (N_A, D_RNN, D_MODEL), BETA * D_RNN ** -0.5),
        "pool_w_in": nrm(ks[11], (N_B, D_MODEL, D_POOL), D_MODEL ** -0.5),
        "pool_w_grp": nrm(ks[12], (N_B, POOL_GROUPS, POOL_GW, POOL_GW), POOL_GW ** -0.5),
        "pool_b_grp": nrm(ks[13], (N_B, D_POOL), 0.01),
        "pool_scale": 1.0 + nrm(ks[14], (N_B, D_POOL), 0.1),
        "pool_w_out": nrm(ks[15], (N_B, D_POOL, D_MODEL), BETA * D_POOL ** -0.5),
        "ln_mix_g": 1.0 + nrm(ks[16], (DEPTH, D_MODEL), 0.05),
        "ln_mix_b": nrm(ks[17], (DEPTH, D_MODEL), 0.01),
        "mlp_w1": nrm(ks[18], (DEPTH, D_MODEL, D_FF), D_MODEL ** -0.5),
        "mlp_w2": nrm(ks[19], (DEPTH, D_FF, D_MODEL), BETA * D_FF ** -0.5),
        "ln_mlp_g": 1.0 + nrm(ks[20], (DEPTH, D_MODEL), 0.05),
        "ln_mlp_b": nrm(ks[21], (DEPTH, D_MODEL), 0.01),
        "ple_w": nrm(ks[22], (DEPTH, PLE_DIM, D_MODEL), PLE_DIM ** -0.5),
        "ple_gate_w": nrm(ks[23], (DEPTH, D_MODEL, D_MODEL), D_MODEL ** -0.5),
        "ple_gate_b": nrm(ks[24], (DEPTH, D_MODEL), 0.01),
    }


def _fwd_reference(x, p, lru_w_in, lru_conv_w, lru_conv_b, lru_wa, lru_ba, lru_wx, lru_bx,
              lru_lambda, lru_w_out, pool_w_in, pool_w_grp, pool_b_grp, pool_scale,
              pool_w_out, ln_mix_g, ln_mix_b, mlp_w1, mlp_w2, ln_mlp_g, ln_mlp_b,
              ple_w, ple_gate_w, ple_gate_b):
    for i in range(DEPTH):
        slot = i // N_MIXERS
        if i % N_MIXERS == 0:
            m = recurrent_mixer(x, lru_w_in[slot], lru_conv_w[slot], lru_conv_b[slot],
                                lru_wa[slot], lru_ba[slot], lru_wx[slot], lru_bx[slot],
                                lru_lambda[slot], lru_w_out[slot])
        else:
            m = pooling_mixer(x, pool_w_in[slot], pool_w_grp[slot], pool_b_grp[slot],
                              pool_scale[slot], pool_w_out[slot])
        x = layer_norm(ALPHA * x + m, ln_mix_g[i], ln_mix_b[i])
        x = layer_norm(ALPHA * x + sq_relu_mlp(x, mlp_w1[i], mlp_w2[i]), ln_mlp_g[i], ln_mlp_b[i])
        gate = jax.nn.sigmoid(x @ ple_gate_w[i] + ple_gate_b[i])
        x = x + (p[i] @ ple_w[i]) * gate
    return x


import jax as _jax
import jax.numpy as _jnp

TWIN_FORMAT = 'train_step'
FWD_PARAMS = ['x', 'p', 'lru_w_in', 'lru_conv_w', 'lru_conv_b', 'lru_wa', 'lru_ba', 'lru_wx', 'lru_bx', 'lru_lambda', 'lru_w_out', 'pool_w_in', 'pool_w_grp', 'pool_b_grp', 'pool_scale', 'pool_w_out', 'ln_mix_g', 'ln_mix_b', 'mlp_w1', 'mlp_w2', 'ln_mlp_g', 'ln_mlp_b', 'ple_w', 'ple_gate_w', 'ple_gate_b']
TWIN_WEIGHTS = ['lru_w_in', 'lru_conv_w', 'lru_conv_b', 'lru_wa', 'lru_ba', 'lru_wx', 'lru_bx', 'lru_lambda', 'lru_w_out', 'pool_w_in', 'pool_w_grp', 'pool_b_grp', 'pool_scale', 'pool_w_out', 'ln_mix_g', 'ln_mix_b', 'mlp_w1', 'mlp_w2', 'ln_mlp_g', 'ln_mlp_b', 'ple_w', 'ple_gate_w', 'ple_gate_b']
TWIN_DIFF_INPUT = 'x'
TWIN_INPUTS = ['x', 'p', 'lru_w_in', 'lru_conv_w', 'lru_conv_b', 'lru_wa', 'lru_ba', 'lru_wx', 'lru_bx', 'lru_lambda', 'lru_w_out', 'pool_w_in', 'pool_w_grp', 'pool_b_grp', 'pool_scale', 'pool_w_out', 'ln_mix_g', 'ln_mix_b', 'mlp_w1', 'mlp_w2', 'ln_mlp_g', 'ln_mlp_b', 'ple_w', 'ple_gate_w', 'ple_gate_b', 'loss_target', 'm_lru_w_in', 'm_lru_conv_w', 'm_lru_conv_b', 'm_lru_wa', 'm_lru_ba', 'm_lru_wx', 'm_lru_bx', 'm_lru_lambda', 'm_lru_w_out', 'm_pool_w_in', 'm_pool_w_grp', 'm_pool_b_grp', 'm_pool_scale', 'm_pool_w_out', 'm_ln_mix_g', 'm_ln_mix_b', 'm_mlp_w1', 'm_mlp_w2', 'm_ln_mlp_g', 'm_ln_mlp_b', 'm_ple_w', 'm_ple_gate_w', 'm_ple_gate_b', 'v_lru_w_in', 'v_lru_conv_w', 'v_lru_conv_b', 'v_lru_wa', 'v_lru_ba', 'v_lru_wx', 'v_lru_bx', 'v_lru_lambda', 'v_lru_w_out', 'v_pool_w_in', 'v_pool_w_grp', 'v_pool_b_grp', 'v_pool_scale', 'v_pool_w_out', 'v_ln_mix_g', 'v_ln_mix_b', 'v_mlp_w1', 'v_mlp_w2', 'v_ln_mlp_g', 'v_ln_mlp_b', 'v_ple_w', 'v_ple_gate_w', 'v_ple_gate_b']
TWIN_OUTPUTS = ['loss', 'grad_x', 'grad_lru_w_in', 'grad_lru_conv_w', 'grad_lru_conv_b', 'grad_lru_wa', 'grad_lru_ba', 'grad_lru_wx', 'grad_lru_bx', 'grad_lru_lambda', 'grad_lru_w_out', 'grad_pool_w_in', 'grad_pool_w_grp', 'grad_pool_b_grp', 'grad_pool_scale', 'grad_pool_w_out', 'grad_ln_mix_g', 'grad_ln_mix_b', 'grad_mlp_w1', 'grad_mlp_w2', 'grad_ln_mlp_g', 'grad_ln_mlp_b', 'grad_ple_w', 'grad_ple_gate_w', 'grad_ple_gate_b', 'delta_lru_w_in', 'delta_lru_conv_w', 'delta_lru_conv_b', 'delta_lru_wa', 'delta_lru_ba', 'delta_lru_wx', 'delta_lru_bx', 'delta_lru_lambda', 'delta_lru_w_out', 'delta_pool_w_in', 'delta_pool_w_grp', 'delta_pool_b_grp', 'delta_pool_scale', 'delta_pool_w_out', 'delta_ln_mix_g', 'delta_ln_mix_b', 'delta_mlp_w1', 'delta_mlp_w2', 'delta_ln_mlp_g', 'delta_ln_mlp_b', 'delta_ple_w', 'delta_ple_gate_w', 'delta_ple_gate_b', 'new_m_lru_w_in', 'new_m_lru_conv_w', 'new_m_lru_conv_b', 'new_m_lru_wa', 'new_m_lru_ba', 'new_m_lru_wx', 'new_m_lru_bx', 'new_m_lru_lambda', 'new_m_lru_w_out', 'new_m_pool_w_in', 'new_m_pool_w_grp', 'new_m_pool_b_grp', 'new_m_pool_scale', 'new_m_pool_w_out', 'new_m_ln_mix_g', 'new_m_ln_mix_b', 'new_m_mlp_w1', 'new_m_mlp_w2', 'new_m_ln_mlp_g', 'new_m_ln_mlp_b', 'new_m_ple_w', 'new_m_ple_gate_w', 'new_m_ple_gate_b', 'new_v_lru_w_in', 'new_v_lru_conv_w', 'new_v_lru_conv_b', 'new_v_lru_wa', 'new_v_lru_ba', 'new_v_lru_wx', 'new_v_lru_bx', 'new_v_lru_lambda', 'new_v_lru_w_out', 'new_v_pool_w_in', 'new_v_pool_w_grp', 'new_v_pool_b_grp', 'new_v_pool_scale', 'new_v_pool_w_out', 'new_v_ln_mix_g', 'new_v_ln_mix_b', 'new_v_mlp_w1', 'new_v_mlp_w2', 'new_v_ln_mlp_g', 'new_v_ln_mlp_b', 'new_v_ple_w', 'new_v_ple_gate_w', 'new_v_ple_gate_b']
TWIN_LEAF_KINDS = {'loss': 'loss', 'grad_x': 'grad_x', 'grad_lru_w_in': 'grad_w', 'grad_lru_conv_w': 'grad_w', 'grad_lru_conv_b': 'grad_w', 'grad_lru_wa': 'grad_w', 'grad_lru_ba': 'grad_w', 'grad_lru_wx': 'grad_w', 'grad_lru_bx': 'grad_w', 'grad_lru_lambda': 'grad_w', 'grad_lru_w_out': 'grad_w', 'grad_pool_w_in': 'grad_w', 'grad_pool_w_grp': 'grad_w', 'grad_pool_b_grp': 'grad_w', 'grad_pool_scale': 'grad_w', 'grad_pool_w_out': 'grad_w', 'grad_ln_mix_g': 'grad_w', 'grad_ln_mix_b': 'grad_w', 'grad_mlp_w1': 'grad_w', 'grad_mlp_w2': 'grad_w', 'grad_ln_mlp_g': 'grad_w', 'grad_ln_mlp_b': 'grad_w', 'grad_ple_w': 'grad_w', 'grad_ple_gate_w': 'grad_w', 'grad_ple_gate_b': 'grad_w', 'delta_lru_w_in': 'delta_w', 'delta_lru_conv_w': 'delta_w', 'delta_lru_conv_b': 'delta_w', 'delta_lru_wa': 'delta_w', 'delta_lru_ba': 'delta_w', 'delta_lru_wx': 'delta_w', 'delta_lru_bx': 'delta_w', 'delta_lru_lambda': 'delta_w', 'delta_lru_w_out': 'delta_w', 'delta_pool_w_in': 'delta_w', 'delta_pool_w_grp': 'delta_w', 'delta_pool_b_grp': 'delta_w', 'delta_pool_scale': 'delta_w', 'delta_pool_w_out': 'delta_w', 'delta_ln_mix_g': 'delta_w', 'delta_ln_mix_b': 'delta_w', 'delta_mlp_w1': 'delta_w', 'delta_mlp_w2': 'delta_w', 'delta_ln_mlp_g': 'delta_w', 'delta_ln_mlp_b': 'delta_w', 'delta_ple_w': 'delta_w', 'delta_ple_gate_w': 'delta_w', 'delta_ple_gate_b': 'delta_w', 'new_m_lru_w_in': 'new_m', 'new_m_lru_conv_w': 'new_m', 'new_m_lru_conv_b': 'new_m', 'new_m_lru_wa': 'new_m', 'new_m_lru_ba': 'new_m', 'new_m_lru_wx': 'new_m', 'new_m_lru_bx': 'new_m', 'new_m_lru_lambda': 'new_m', 'new_m_lru_w_out': 'new_m', 'new_m_pool_w_in': 'new_m', 'new_m_pool_w_grp': 'new_m', 'new_m_pool_b_grp': 'new_m', 'new_m_pool_scale': 'new_m', 'new_m_pool_w_out': 'new_m', 'new_m_ln_mix_g': 'new_m', 'new_m_ln_mix_b': 'new_m', 'new_m_mlp_w1': 'new_m', 'new_m_mlp_w2': 'new_m', 'new_m_ln_mlp_g': 'new_m', 'new_m_ln_mlp_b': 'new_m', 'new_m_ple_w': 'new_m', 'new_m_ple_gate_w': 'new_m', 'new_m_ple_gate_b': 'new_m', 'new_v_lru_w_in': 'new_v', 'new_v_lru_conv_w': 'new_v', 'new_v_lru_conv_b': 'new_v', 'new_v_lru_wa': 'new_v', 'new_v_lru_ba': 'new_v', 'new_v_lru_wx': 'new_v', 'new_v_lru_bx': 'new_v', 'new_v_lru_lambda': 'new_v', 'new_v_lru_w_out': 'new_v', 'new_v_pool_w_in': 'new_v', 'new_v_pool_w_grp': 'new_v', 'new_v_pool_b_grp': 'new_v', 'new_v_pool_scale': 'new_v', 'new_v_pool_w_out': 'new_v', 'new_v_ln_mix_g': 'new_v', 'new_v_ln_mix_b': 'new_v', 'new_v_mlp_w1': 'new_v', 'new_v_mlp_w2': 'new_v', 'new_v_ln_mlp_g': 'new_v', 'new_v_ln_mlp_b': 'new_v', 'new_v_ple_w': 'new_v', 'new_v_ple_gate_w': 'new_v', 'new_v_ple_gate_b': 'new_v'}


def _forward(args):
    return _fwd_reference(*[args[k] for k in FWD_PARAMS])


def _output_shape():
    out = _jax.eval_shape(lambda: _forward(_fwd_setup_inputs(0)))
    return out.shape, out.dtype

N_MICROBATCH = 1
ADAM_LR = 0.001
ADAM_B1 = 0.9
ADAM_B2 = 0.999
ADAM_EPS = 1e-08
ADAM_WD = 0.01
ADAM_STEP = 10
PER_EXAMPLE_BATCH_AXIS = {'x': 0, 'p': 1, 'loss_target': 0}
SHARED_INPUTS = []
_WEIGHT_DTYPES = {'lru_w_in': _jnp.float32, 'lru_conv_w': _jnp.float32, 'lru_conv_b': _jnp.float32, 'lru_wa': _jnp.float32, 'lru_ba': _jnp.float32, 'lru_wx': _jnp.float32, 'lru_bx': _jnp.float32, 'lru_lambda': _jnp.float32, 'lru_w_out': _jnp.float32, 'pool_w_in': _jnp.float32, 'pool_w_grp': _jnp.float32, 'pool_b_grp': _jnp.float32, 'pool_scale': _jnp.float32, 'pool_w_out': _jnp.float32, 'ln_mix_g': _jnp.float32, 'ln_mix_b': _jnp.float32, 'mlp_w1': _jnp.float32, 'mlp_w2': _jnp.float32, 'ln_mlp_g': _jnp.float32, 'ln_mlp_b': _jnp.float32, 'ple_w': _jnp.float32, 'ple_gate_w': _jnp.float32, 'ple_gate_b': _jnp.float32}
MOMENT_SCALE = {'lru_w_in': 6.894127e-02, 'lru_conv_w': 1.094727e-01, 'lru_conv_b': 7.656125e-01, 'lru_wa': 2.376464e-02, 'lru_ba': 1.872944e-02, 'lru_wx': 4.460308e-02, 'lru_bx': 4.120348e-02, 'lru_lambda': 4.379107e-02, 'lru_w_out': 3.289859e-01, 'pool_w_in': 4.274954e-02, 'pool_w_grp': 4.319595e-02, 'pool_b_grp': 6.325797e-01, 'pool_scale': 4.429986e-02, 'pool_w_out': 1.028938e-01, 'ln_mix_g': 1.864319e+00, 'ln_mix_b': 2.612202e+00, 'mlp_w1': 3.789141e-02, 'mlp_w2': 6.595033e-01, 'ln_mlp_g': 1.676300e+01, 'ln_mlp_b': 4.235259e+00, 'ple_w': 2.116860e-01, 'ple_gate_w': 2.848355e-01, 'ple_gate_b': 1.638287e+00}


def _to_microbatches(a, axis):
    t = _jnp.moveaxis(a, axis, 0)
    t = t.reshape((N_MICROBATCH, t.shape[0] // N_MICROBATCH) + t.shape[1:])
    return _jnp.moveaxis(t, 1, axis + 1)


def setup_inputs(seed: int = 0) -> dict:
    inp = _fwd_setup_inputs(seed)
    key = _jax.random.fold_in(_jax.random.key(seed), 7919)
    shape, _ = _output_shape()
    out = dict(inp)
    out["loss_target"] = _jax.random.normal(_jax.random.fold_in(key, 0), shape, _jnp.float32)
    for i, name in enumerate(TWIN_WEIGHTS):
        w = inp[name].astype(_jnp.float32)
        if MOMENT_SCALE is None:
            s = _jnp.sqrt(_jnp.mean(_jnp.square(w)) + 1e-30)
        else:
            s = MOMENT_SCALE[name]
        km, kv = _jax.random.split(_jax.random.fold_in(key, i + 1))
        out[name] = w
        out["m_" + name] = s * _jax.random.normal(km, w.shape, _jnp.float32)
        out["v_" + name] = (s * s) * _jax.random.uniform(kv, w.shape, _jnp.float32, 0.5, 1.5)
    if N_MICROBATCH > 1:
        for name, axis in PER_EXAMPLE_BATCH_AXIS.items():
            out[name] = _to_microbatches(out[name], axis)
    return {'x': out['x'], 'p': out['p'], 'lru_w_in': out['lru_w_in'], 'lru_conv_w': out['lru_conv_w'], 'lru_conv_b': out['lru_conv_b'], 'lru_wa': out['lru_wa'], 'lru_ba': out['lru_ba'], 'lru_wx': out['lru_wx'], 'lru_bx': out['lru_bx'], 'lru_lambda': out['lru_lambda'], 'lru_w_out': out['lru_w_out'], 'pool_w_in': out['pool_w_in'], 'pool_w_grp': out['pool_w_grp'], 'pool_b_grp': out['pool_b_grp'], 'pool_scale': out['pool_scale'], 'pool_w_out': out['pool_w_out'], 'ln_mix_g': out['ln_mix_g'], 'ln_mix_b': out['ln_mix_b'], 'mlp_w1': out['mlp_w1'], 'mlp_w2': out['mlp_w2'], 'ln_mlp_g': out['ln_mlp_g'], 'ln_mlp_b': out['ln_mlp_b'], 'ple_w': out['ple_w'], 'ple_gate_w': out['ple_gate_w'], 'ple_gate_b': out['ple_gate_b'], 'loss_target': out['loss_target'], 'm_lru_w_in': out['m_lru_w_in'], 'm_lru_conv_w': out['m_lru_conv_w'], 'm_lru_conv_b': out['m_lru_conv_b'], 'm_lru_wa': out['m_lru_wa'], 'm_lru_ba': out['m_lru_ba'], 'm_lru_wx': out['m_lru_wx'], 'm_lru_bx': out['m_lru_bx'], 'm_lru_lambda': out['m_lru_lambda'], 'm_lru_w_out': out['m_lru_w_out'], 'm_pool_w_in': out['m_pool_w_in'], 'm_pool_w_grp': out['m_pool_w_grp'], 'm_pool_b_grp': out['m_pool_b_grp'], 'm_pool_scale': out['m_pool_scale'], 'm_pool_w_out': out['m_pool_w_out'], 'm_ln_mix_g': out['m_ln_mix_g'], 'm_ln_mix_b': out['m_ln_mix_b'], 'm_mlp_w1': out['m_mlp_w1'], 'm_mlp_w2': out['m_mlp_w2'], 'm_ln_mlp_g': out['m_ln_mlp_g'], 'm_ln_mlp_b': out['m_ln_mlp_b'], 'm_ple_w': out['m_ple_w'], 'm_ple_gate_w': out['m_ple_gate_w'], 'm_ple_gate_b': out['m_ple_gate_b'], 'v_lru_w_in': out['v_lru_w_in'], 'v_lru_conv_w': out['v_lru_conv_w'], 'v_lru_conv_b': out['v_lru_conv_b'], 'v_lru_wa': out['v_lru_wa'], 'v_lru_ba': out['v_lru_ba'], 'v_lru_wx': out['v_lru_wx'], 'v_lru_bx': out['v_lru_bx'], 'v_lru_lambda': out['v_lru_lambda'], 'v_lru_w_out': out['v_lru_w_out'], 'v_pool_w_in': out['v_pool_w_in'], 'v_pool_w_grp': out['v_pool_w_grp'], 'v_pool_b_grp': out['v_pool_b_grp'], 'v_pool_scale': out['v_pool_scale'], 'v_pool_w_out': out['v_pool_w_out'], 'v_ln_mix_g': out['v_ln_mix_g'], 'v_ln_mix_b': out['v_ln_mix_b'], 'v_mlp_w1': out['v_mlp_w1'], 'v_mlp_w2': out['v_mlp_w2'], 'v_ln_mlp_g': out['v_ln_mlp_g'], 'v_ln_mlp_b': out['v_ln_mlp_b'], 'v_ple_w': out['v_ple_w'], 'v_ple_gate_w': out['v_ple_gate_w'], 'v_ple_gate_b': out['v_ple_gate_b']}


def _loss(weights, diff, rest, loss_target):
    with _jax.named_scope("forward"):
        args = {**rest, TWIN_DIFF_INPUT: diff, **{k: w.astype(_WEIGHT_DTYPES[k]) for k, w in weights.items()}}
        y = _forward(args)
    with _jax.named_scope("loss_head"):
        err = _jnp.square(y.astype(_jnp.float32) - loss_target)
        return 0.5 * _jnp.sum(_jnp.mean(err, axis=-1)) if err.ndim else 0.5 * err


def _adamw(w, g, m, v):
    m = ADAM_B1 * m + (1.0 - ADAM_B1) * g
    v = ADAM_B2 * v + (1.0 - ADAM_B2) * _jnp.square(g)
    m_hat = m / (1.0 - ADAM_B1 ** ADAM_STEP)
    v_hat = v / (1.0 - ADAM_B2 ** ADAM_STEP)
    delta = -ADAM_LR * (m_hat / (_jnp.sqrt(v_hat) + ADAM_EPS) + ADAM_WD * w)
    return delta, m, v


def reference(x, p, lru_w_in, lru_conv_w, lru_conv_b, lru_wa, lru_ba, lru_wx, lru_bx, lru_lambda, lru_w_out, pool_w_in, pool_w_grp, pool_b_grp, pool_scale, pool_w_out, ln_mix_g, ln_mix_b, mlp_w1, mlp_w2, ln_mlp_g, ln_mlp_b, ple_w, ple_gate_w, ple_gate_b, loss_target, m_lru_w_in, m_lru_conv_w, m_lru_conv_b, m_lru_wa, m_lru_ba, m_lru_wx, m_lru_bx, m_lru_lambda, m_lru_w_out, m_pool_w_in, m_pool_w_grp, m_pool_b_grp, m_pool_scale, m_pool_w_out, m_ln_mix_g, m_ln_mix_b, m_mlp_w1, m_mlp_w2, m_ln_mlp_g, m_ln_mlp_b, m_ple_w, m_ple_gate_w, m_ple_gate_b, v_lru_w_in, v_lru_conv_w, v_lru_conv_b, v_lru_wa, v_lru_ba, v_lru_wx, v_lru_bx, v_lru_lambda, v_lru_w_out, v_pool_w_in, v_pool_w_grp, v_pool_b_grp, v_pool_scale, v_pool_w_out, v_ln_mix_g, v_ln_mix_b, v_mlp_w1, v_mlp_w2, v_ln_mlp_g, v_ln_mlp_b, v_ple_w, v_ple_gate_w, v_ple_gate_b):
    given = dict(x=x, p=p, lru_w_in=lru_w_in, lru_conv_w=lru_conv_w, lru_conv_b=lru_conv_b, lru_wa=lru_wa, lru_ba=lru_ba, lru_wx=lru_wx, lru_bx=lru_bx, lru_lambda=lru_lambda, lru_w_out=lru_w_out, pool_w_in=pool_w_in, pool_w_grp=pool_w_grp, pool_b_grp=pool_b_grp, pool_scale=pool_scale, pool_w_out=pool_w_out, ln_mix_g=ln_mix_g, ln_mix_b=ln_mix_b, mlp_w1=mlp_w1, mlp_w2=mlp_w2, ln_mlp_g=ln_mlp_g, ln_mlp_b=ln_mlp_b, ple_w=ple_w, ple_gate_w=ple_gate_w, ple_gate_b=ple_gate_b, loss_target=loss_target, m_lru_w_in=m_lru_w_in, m_lru_conv_w=m_lru_conv_w, m_lru_conv_b=m_lru_conv_b, m_lru_wa=m_lru_wa, m_lru_ba=m_lru_ba, m_lru_wx=m_lru_wx, m_lru_bx=m_lru_bx, m_lru_lambda=m_lru_lambda, m_lru_w_out=m_lru_w_out, m_pool_w_in=m_pool_w_in, m_pool_w_grp=m_pool_w_grp, m_pool_b_grp=m_pool_b_grp, m_pool_scale=m_pool_scale, m_pool_w_out=m_pool_w_out, m_ln_mix_g=m_ln_mix_g, m_ln_mix_b=m_ln_mix_b, m_mlp_w1=m_mlp_w1, m_mlp_w2=m_mlp_w2, m_ln_mlp_g=m_ln_mlp_g, m_ln_mlp_b=m_ln_mlp_b, m_ple_w=m_ple_w, m_ple_gate_w=m_ple_gate_w, m_ple_gate_b=m_ple_gate_b, v_lru_w_in=v_lru_w_in, v_lru_conv_w=v_lru_conv_w, v_lru_conv_b=v_lru_conv_b, v_lru_wa=v_lru_wa, v_lru_ba=v_lru_ba, v_lru_wx=v_lru_wx, v_lru_bx=v_lru_bx, v_lru_lambda=v_lru_lambda, v_lru_w_out=v_lru_w_out, v_pool_w_in=v_pool_w_in, v_pool_w_grp=v_pool_w_grp, v_pool_b_grp=v_pool_b_grp, v_pool_scale=v_pool_scale, v_pool_w_out=v_pool_w_out, v_ln_mix_g=v_ln_mix_g, v_ln_mix_b=v_ln_mix_b, v_mlp_w1=v_mlp_w1, v_mlp_w2=v_mlp_w2, v_ln_mlp_g=v_ln_mlp_g, v_ln_mlp_b=v_ln_mlp_b, v_ple_w=v_ple_w, v_ple_gate_w=v_ple_gate_w, v_ple_gate_b=v_ple_gate_b)
    weights = {n: given[n] for n in TWIN_WEIGHTS}
    shared = {n: given[n] for n in SHARED_INPUTS}
    per_example = {n: given[n] for n in ['x', 'p']}
    grad_fn = _jax.value_and_grad(_loss, argnums=(0, 1))

    def one_microbatch(ex, loss_target):
        ex = dict(ex)
        diff = ex.pop(TWIN_DIFF_INPUT)
        return grad_fn(weights, diff, {**shared, **ex}, loss_target)

    if N_MICROBATCH == 1:
        loss, (grad_w, grad_x) = one_microbatch(per_example, given["loss_target"])
    else:
        def body(carry, xs):
            loss_sum, grad_sum = carry
            l_k, (gw_k, gx_k) = one_microbatch(xs[0], xs[1])
            with _jax.named_scope("update"):
                return (loss_sum + l_k, _jax.tree.map(_jnp.add, grad_sum, gw_k)), gx_k

        init = (_jnp.zeros((), _jnp.float32), _jax.tree.map(_jnp.zeros_like, weights))
        (loss, grad_w), grad_x = _jax.lax.scan(body, init, (per_example, given["loss_target"]))
    with _jax.named_scope("update"):
        delta_w, new_m, new_v = {}, {}, {}
        for n in TWIN_WEIGHTS:
            delta_w[n], new_m[n], new_v[n] = _adamw(weights[n], grad_w[n], given["m_" + n], given["v_" + n])
    return (loss, grad_x, *[grad_w[n] for n in TWIN_WEIGHTS], *[delta_w[n] for n in TWIN_WEIGHTS],
            *[new_m[n] for n in TWIN_WEIGHTS], *[new_v[n] for n in TWIN_WEIGHTS])
```

```python
import functools
import math

import jax
import jax.numpy as jnp
from jax import lax
from jax.experimental import pallas as pl
from jax.experimental.pallas import tpu as pltpu

F32 = jnp.float32
BF16 = jnp.bfloat16

N_CHIPS = 4
LRU_BW = 128
LRU_C = 8.0
CONV_WIDTH = 4
POOL_WINDOWS = (2, 4, 8, 16)
POOL_HALO = 16
CONV_HALO = 8
LN_EPS = 1e-5
ADAM_LR = 0.001
ADAM_B1 = 0.9
ADAM_B2 = 0.999
ADAM_EPS = 1e-08
ADAM_WD = 0.01
ADAM_STEP = 10
GELU_C = math.sqrt(2.0 / math.pi)
GELU_K = 0.044715
VMEM_LIMIT_BYTES = 56 * 1024 * 1024
MESH = pl.DeviceIdType.MESH
BLOB_COLS = 1024


def _params(*sem):
    return pltpu.CompilerParams(dimension_semantics=tuple(sem), vmem_limit_bytes=VMEM_LIMIT_BYTES)


def _tile(unit, pref, align=128):
    if unit <= pref:
        return unit
    for d in range(2, unit + 1):
        if unit % d == 0 and unit // d <= pref and (unit // d) % align == 0:
            return unit // d
    raise ValueError((unit, pref, align))


class View:
    def __init__(self, arr, shape, row_unit, col_unit, block_fn, full=None, dtype=None):
        self.arr, self.shape, self.row_unit, self.col_unit, self.block_fn = arr, shape, row_unit, col_unit, block_fn
        self.full = full if full is not None else arr.shape
        self.dtype = dtype if dtype is not None else arr.dtype

    def spec(self, tr, tc, f):
        block, idx = self.block_fn(tr, tc)
        return pl.BlockSpec(block, lambda *g: idx(*f(*g)))


def plain(arr=None, shape=None, dtype=None):
    shape = arr.shape if arr is not None else shape
    return View(arr, shape, shape[0], shape[1], lambda tr, tc: ((tr, tc), lambda rt, ct: (rt, ct)), full=shape, dtype=dtype)


def colsplit(arr, layer, rows, n=N_CHIPS, full=None, dtype=None):
    full = arr.shape if arr is not None else full
    c = full[2]

    def block_fn(tr, tc):
        assert rows % tr == 0 and c % tc == 0, (rows, tr, c, tc)
        per, rpl = c // tc, rows // tr
        return (None, tr, tc), lambda rt, ct: (ct // per, layer * rpl + rt, ct % per)

    return View(arr, (rows, n * c), rows, c, block_fn, full=full, dtype=dtype)


def rowsplit(arr, layer, rows, n=N_CHIPS, full=None, dtype=None):
    full = arr.shape if arr is not None else full
    c = full[2]

    def block_fn(tr, tc):
        assert rows % tr == 0 and c % tc == 0, (rows, tr, c, tc)
        per = rows // tr
        return (None, tr, tc), lambda rt, ct: (rt // per, layer * per + rt % per, ct)

    return View(arr, (n * rows, c), rows, c, block_fn, full=full, dtype=dtype)


def matmul(name, a, b, mode, outs, epilogue=None, tiles=(), rows=(), pm=1024, pn=1024, pk=1024):
    if mode == "nn":
        (m, k), (k2, n) = a.shape, b.shape
        um, uk, un = a.row_unit, min(a.col_unit, b.row_unit), b.col_unit
        dims = (((1,), (0,)), ((), ()))
    elif mode == "nt":
        (m, k), (n, k2) = a.shape, b.shape
        um, uk, un = a.row_unit, min(a.col_unit, b.col_unit), b.row_unit
        dims = (((1,), (1,)), ((), ()))
    else:
        (k, m), (k2, n) = a.shape, b.shape
        um, uk, un = a.col_unit, min(a.row_unit, b.row_unit), b.col_unit
        dims = (((0,), (0,)), ((), ()))
    assert k == k2, (name, a.shape, b.shape)
    for o in list(outs) + list(tiles):
        assert o.shape == (m, n), (name, o.shape, m, n)
        um, un = min(um, o.row_unit), min(un, o.col_unit)
    tm, tn, tk = _tile(um, pm), _tile(un, pn), _tile(uk, pk)
    assert m % tm == 0 and n % tn == 0 and k % tk == 0, (name, m, n, k, tm, tn, tk)
    gm, gn, gk = m // tm, n // tn, k // tk

    if mode == "nn":
        a_spec = a.spec(tm, tk, lambda i, j, kk: (i, kk))
        b_spec = b.spec(tk, tn, lambda i, j, kk: (kk, j))
    elif mode == "nt":
        a_spec = a.spec(tm, tk, lambda i, j, kk: (i, kk))
        b_spec = b.spec(tn, tk, lambda i, j, kk: (j, kk))
    else:
        a_spec = a.spec(tk, tm, lambda i, j, kk: (kk, i))
        b_spec = b.spec(tk, tn, lambda i, j, kk: (kk, j))
    tile_specs = [t.spec(tm, tn, lambda i, j, kk: (i, j)) for t in tiles]
    row_specs = [pl.BlockSpec((1, tn), lambda i, j, kk: (0, j)) for _ in rows]
    in_place = [o for o in outs if o.arr is not None]
    alias_specs = [pl.BlockSpec(memory_space=pl.ANY) for _ in in_place]
    out_specs = [o.spec(tm, tn, lambda i, j, kk: (i, j)) for o in outs]
    n_in = 2 + len(tiles) + len(rows)
    aliases = {}
    for o_idx, o in enumerate(outs):
        if o.arr is not None:
            aliases[n_in + in_place.index(o)] = o_idx
    n_t, n_r, n_a, n_o = len(tiles), len(rows), len(in_place), len(outs)

    def body(*refs):
        a_ref, b_ref = refs[0], refs[1]
        tile_refs = refs[2:2 + n_t]
        row_refs = refs[2 + n_t:2 + n_t + n_r]
        out_refs = refs[2 + n_t + n_r + n_a:2 + n_t + n_r + n_a + n_o]
        acc_ref = refs[-1] if gk > 1 else None

        def finish(acc):
            extra = [t[...] for t in tile_refs] + [r[...] for r in row_refs]
            res = epilogue(acc, *extra) if epilogue is not None else (acc,)
            for o_ref, r in zip(out_refs, res):
                o_ref[...] = r.astype(o_ref.dtype)

        prod = lax.dot_general(a_ref[...].astype(BF16), b_ref[...].astype(BF16), dims, preferred_element_type=F32)
        if gk == 1:
            finish(prod)
        else:
            kk = pl.program_id(2)

            @pl.when(kk == 0)
            def _():
                acc_ref[...] = prod

            @pl.when(kk > 0)
            def _():
                acc_ref[...] += prod

            @pl.when(kk == gk - 1)
            def _():
                finish(acc_ref[...])

    res = pl.pallas_call(
        body,
        name=name,
        grid=(gm, gn, gk),
        in_specs=[a_spec, b_spec] + tile_specs + row_specs + alias_specs,
        out_specs=out_specs,
        out_shape=[jax.ShapeDtypeStruct(o.full, o.dtype) for o in outs],
        scratch_shapes=[pltpu.VMEM((tm, tn), F32)] if gk > 1 else [],
        input_output_aliases=aliases,
        compiler_params=_params("parallel", "parallel", "arbitrary"),
    )(a.arr, b.arr, *[t.arr for t in tiles], *rows, *[o.arr for o in in_place])
    return res


def rows_call(name, fn, tiled, vecs, tiled_out, acc_out, tr=256):
    t = tiled[0].shape[0]
    tr = min(tr, t)
    assert t % tr == 0
    n1, n2, n3 = len(tiled), len(vecs), len(tiled_out)

    def body(*refs):
        fn(pl.program_id(0), refs[:n1], refs[n1:n1 + n2], refs[n1 + n2:n1 + n2 + n3], refs[n1 + n2 + n3:])

    return pl.pallas_call(
        body,
        name=name,
        grid=(t // tr,),
        in_specs=[pl.BlockSpec((tr, x.shape[1]), lambda i: (i, 0)) for x in tiled]
        + [pl.BlockSpec(v.shape, lambda i: (0, 0)) for v in vecs],
        out_specs=[pl.BlockSpec((tr, c), lambda i: (i, 0)) for c, _ in tiled_out]
        + [pl.BlockSpec(s, lambda i: (0, 0)) for s, _ in acc_out],
        out_shape=[jax.ShapeDtypeStruct((t, c), d) for c, d in tiled_out] + [jax.ShapeDtypeStruct(s, d) for s, d in acc_out],
        compiler_params=_params("arbitrary" if acc_out else "parallel"),
    )(*tiled, *vecs)


def _accumulate(step, ref, val):
    @pl.when(step == 0)
    def _():
        ref[...] = val

    @pl.when(step > 0)
    def _():
        ref[...] += val


def _ln_stats(s):
    mu = jnp.mean(s, axis=-1, keepdims=True)
    d = s - mu
    var = jnp.mean(d * d, axis=-1, keepdims=True)
    rstd = lax.rsqrt(var + LN_EPS)
    return d * rstd, rstd


def ln_fwd(name, alpha, x_in, m, g, b):
    d = x_in.shape[1]

    def fn(step, tiled, vecs, outs, accs):
        s = alpha * tiled[0][...] + tiled[1][...]
        xhat, _ = _ln_stats(s)
        y = xhat * vecs[0][...] + vecs[1][...]
        outs[0][...] = y
        outs[1][...] = y.astype(BF16)
        outs[2][...] = s

    return rows_call(name, fn, [x_in, m], [g, b], [(d, F32), (d, BF16), (d, F32)], [])


def ln_bwd(name, ca, da, db, s, g):
    d = s.shape[1]

    def fn(step, tiled, vecs, outs, accs):
        dx = ca * tiled[0][...] + tiled[1][...]
        xhat, rstd = _ln_stats(tiled[2][...])
        dxh = dx * vecs[0][...]
        ds = rstd * (dxh - jnp.mean(dxh, axis=-1, keepdims=True) - xhat * jnp.mean(dxh * xhat, axis=-1, keepdims=True))
        outs[0][...] = ds
        outs[1][...] = ds.astype(BF16)
        _accumulate(step, accs[0], jnp.sum(dx * xhat, axis=0, keepdims=True))
        _accumulate(step, accs[1], jnp.sum(dx, axis=0, keepdims=True))

    return rows_call(name, fn, [da, db, s], [g], [(d, F32), (d, BF16)], [((1, d), F32), ((1, d), F32)])


def ple_bwd(name, dx3, gate, e):
    d = dx3.shape[1]

    def fn(step, tiled, vecs, outs, accs):
        dx, gt, ev = tiled[0][...], tiled[1][...], tiled[2][...]
        dpre = dx * ev * gt * (1.0 - gt)
        outs[0][...] = (dx * gt).astype(BF16)
        outs[1][...] = dpre.astype(BF16)
        _accumulate(step, accs[0], jnp.sum(dpre, axis=0, keepdims=True))

    return rows_call(name, fn, [dx3, gate, e], [], [(d, BF16), (d, BF16)], [((1, d), F32)])


def loss_head(name, y, target):
    t, d = y.shape

    def fn(step, tiled, vecs, outs, accs):
        err = tiled[0][...] - tiled[1][...]
        outs[0][...] = err * (1.0 / d)
        part = jnp.sum(jnp.sum(err * err, axis=1, keepdims=True), axis=0, keepdims=True) * (0.5 / d)
        _accumulate(step, accs[0], part)

    return rows_call(name, fn, [y, target], [], [(d, F32)], [((1, 1), F32)])


def _softplus(z):
    return jnp.maximum(z, 0.0) + jnp.log1p(jnp.exp(-jnp.abs(z)))


def _gelu(y):
    th = jnp.tanh(GELU_C * (y + GELU_K * (y * y * y)))
    cdf = 0.5 * (1.0 + th)
    return y * cdf, cdf + 0.5 * y * (1.0 - th * th) * (GELU_C * (1.0 + 3.0 * GELU_K * y * y))


def _up(win, k):
    return pltpu.roll(win, win.shape[0] - k, 0)


def _down(win, k):
    return pltpu.roll(win, k, 0)


def _lru_gates(win, row0, cw_ref, cb, wa, ba, wx, bx, sp):
    h = CONV_HALO
    u = (cb + cw_ref[3:4, :] * win[h:] + cw_ref[2:3, :] * _down(win, 1)[h:]
         + cw_ref[1:2, :] * _down(win, 2)[h:] + cw_ref[0:1, :] * _down(win, 3)[h:])
    ub = u.astype(BF16)
    r = jax.nn.sigmoid(jnp.dot(ub, wa, preferred_element_type=F32) + ba)
    ig = jax.nn.sigmoid(jnp.dot(ub, wx, preferred_element_type=F32) + bx)
    log_a = (-LRU_C) * r * sp
    a = jnp.exp(log_a)
    mult = jnp.sqrt(-jnp.tanh(log_a) * (a * a + 1.0))
    first = (row0 + lax.broadcasted_iota(jnp.int32, u.shape, 0)) == 0
    mult = jnp.where(first, 1.0, mult)
    return u, r, ig, a, mult, first


def _scan_forward(a_ref, b_ref, out_ref, out_off, t):
    row = lax.broadcasted_iota(jnp.int32, (8, LRU_BW), 0)
    last = row == 7

    def block(j, hprev):
        r0 = pl.multiple_of(j * 8, 8)
        a = a_ref[pl.ds(r0, 8), :]
        b = b_ref[pl.ds(r0, 8), :]
        for s in (1, 2, 4):
            keep = row < s
            a_sh, b_sh = _down(a, s), _down(b, s)
            b = jnp.where(keep, b, a * b_sh + b)
            a = jnp.where(keep, a, a * a_sh)
        a_tot = jnp.sum(jnp.where(last, a, 0.0), axis=0, keepdims=True)
        b_tot = jnp.sum(jnp.where(last, b, 0.0), axis=0, keepdims=True)
        out_ref[pl.ds(pl.multiple_of(out_off + r0, 8), 8), :] = a * hprev + b
        return a_tot * hprev + b_tot

    lax.fori_loop(0, t // 8, block, jnp.zeros((1, LRU_BW), F32))


def _scan_backward(c_ref, d_ref, t):
    row = lax.broadcasted_iota(jnp.int32, (8, LRU_BW), 0)
    first = row == 0

    def block(jj, gnext):
        r0 = pl.multiple_of((t // 8 - 1 - jj) * 8, 8)
        c = c_ref[pl.ds(r0, 8), :]
        d = d_ref[pl.ds(r0, 8), :]
        for s in (1, 2, 4):
            keep = row >= 8 - s
            c_sh, d_sh = _up(c, s), _up(d, s)
            d = jnp.where(keep, d, c * d_sh + d)
            c = jnp.where(keep, c, c * c_sh)
        c_tot = jnp.sum(jnp.where(first, c, 0.0), axis=0, keepdims=True)
        d_tot = jnp.sum(jnp.where(first, d, 0.0), axis=0, keepdims=True)
        d_ref[pl.ds(r0, 8), :] = c * gnext + d
        return c_tot * gnext + d_tot

    lax.fori_loop(0, t // 8, block, jnp.zeros((1, LRU_BW), F32))


def _lru_in_specs(t, heads):
    blk = lambda i: (0, i)
    return [
        pl.BlockSpec((2, t, LRU_BW), lambda i: (0, 0, i)),
        pl.BlockSpec((CONV_WIDTH, LRU_BW), blk),
        pl.BlockSpec((1, LRU_BW), blk),
        pl.BlockSpec((None, LRU_BW, LRU_BW), lambda i: (i, 0, 0)),
        pl.BlockSpec((1, LRU_BW), blk),
        pl.BlockSpec((None, LRU_BW, LRU_BW), lambda i: (i, 0, 0)),
        pl.BlockSpec((1, LRU_BW), blk),
        pl.BlockSpec((1, LRU_BW), blk),
    ]


def lru_fwd(name, proj, conv_w, conv_b, wa, ba, wx, bx, lam):
    _, t, c = proj.shape
    heads = c // LRU_BW
    rc = min(256, t)

    def body(proj_ref, cw_ref, cb_ref, wa_ref, ba_ref, wx_ref, bx_ref, lam_ref, out_ref, upad, a_s, b_s):
        upad[0:CONV_HALO, :] = jnp.zeros((CONV_HALO, LRU_BW), F32)
        upad[CONV_HALO:, :] = proj_ref[0]
        sp = _softplus(-lam_ref[...])
        cb, ba, bx, wa, wx = cb_ref[...], ba_ref[...], bx_ref[...], wa_ref[...], wx_ref[...]

        def gates(i, carry):
            r0 = pl.multiple_of(i * rc, rc)
            win = upad[pl.ds(r0, rc + CONV_HALO), :]
            u, r, ig, a, mult, _ = _lru_gates(win, r0, cw_ref, cb, wa, ba, wx, bx, sp)
            a_s[pl.ds(r0, rc), :] = a
            b_s[pl.ds(r0, rc), :] = mult * (ig * u)
            return carry

        lax.fori_loop(0, t // rc, gates, 0)
        _scan_forward(a_s, b_s, b_s, 0, t)

        def gate_out(i, carry):
            r0 = pl.multiple_of(i * rc, rc)
            gy, _ = _gelu(proj_ref[1, pl.ds(r0, rc), :])
            out_ref[pl.ds(r0, rc), :] = (b_s[pl.ds(r0, rc), :] * gy).astype(BF16)
            return carry

        lax.fori_loop(0, t // rc, gate_out, 0)

    return pl.pallas_call(
        body,
        name=name,
        grid=(heads,),
        in_specs=_lru_in_specs(t, heads),
        out_specs=pl.BlockSpec((t, LRU_BW), lambda i: (0, i)),
        out_shape=jax.ShapeDtypeStruct((t, c), BF16),
        scratch_shapes=[pltpu.VMEM((t + CONV_HALO, LRU_BW), F32), pltpu.VMEM((t, LRU_BW), F32), pltpu.VMEM((t, LRU_BW), F32)],
        compiler_params=_params("parallel"),
    )(proj, conv_w, conv_b, wa, ba, wx, bx, lam)


def lru_bwd(name, proj, dhg, conv_w, conv_b, wa, ba, wx, bx, lam):
    _, t, c = proj.shape
    heads = c // LRU_BW
    rc = min(256, t)
    h8 = CONV_HALO

    def body(proj_ref, dhg_ref, cw_ref, cb_ref, wa_ref, ba_ref, wx_ref, bx_ref, lam_ref,
             dproj_ref, dcw_ref, dcb_ref, dba_ref, dbx_ref, dlam_ref, dwa_ref, dwx_ref,
             upad, u_s, r_s, ig_s, apad, hpad, g_s, anext_s, dupad):
        zeros8 = jnp.zeros((h8, LRU_BW), F32)
        upad[0:h8, :] = zeros8
        upad[h8:, :] = proj_ref[0]
        hpad[0:h8, :] = zeros8
        apad[t:, :] = zeros8
        dupad[t:, :] = zeros8
        lam = lam_ref[...]
        sp = _softplus(-lam)
        cb, ba, bx, wa, wx = cb_ref[...], ba_ref[...], bx_ref[...], wa_ref[...], wx_ref[...]

        def gates(i, carry):
            r0 = pl.multiple_of(i * rc, rc)
            win = upad[pl.ds(r0, rc + h8), :]
            u, r, ig, a, mult, _ = _lru_gates(win, r0, cw_ref, cb, wa, ba, wx, bx, sp)
            u_s[pl.ds(r0, rc), :] = u
            r_s[pl.ds(r0, rc), :] = r
            ig_s[pl.ds(r0, rc), :] = ig
            apad[pl.ds(r0, rc), :] = a
            g_s[pl.ds(r0, rc), :] = mult * (ig * u)
            return carry

        lax.fori_loop(0, t // rc, gates, 0)
        _scan_forward(apad, g_s, hpad, h8, t)

        def out_gate(i, carry):
            r0 = pl.multiple_of(i * rc, rc)
            gy, dgy = _gelu(proj_ref[1, pl.ds(r0, rc), :])
            dh = dhg_ref[pl.ds(r0, rc), :]
            hh = hpad[pl.ds(pl.multiple_of(r0 + h8, 8), rc), :]
            dproj_ref[1, pl.ds(r0, rc), :] = (dh * hh * dgy).astype(BF16)
            g_s[pl.ds(r0, rc), :] = dh * gy
            anext_s[pl.ds(r0, rc), :] = _up(apad[pl.ds(r0, rc + h8), :], 1)[:rc]
            return carry

        lax.fori_loop(0, t // rc, out_gate, 0)
        _scan_backward(anext_s, g_s, t)

        zrow = jnp.zeros((1, LRU_BW), F32)
        zmat = jnp.zeros((LRU_BW, LRU_BW), F32)

        def grads(i, carry):
            dsp, dba, dbx, dwa, dwx = carry
            r0 = pl.multiple_of(i * rc, rc)
            g = g_s[pl.ds(r0, rc), :]
            u, r, ig, a = u_s[pl.ds(r0, rc), :], r_s[pl.ds(r0, rc), :], ig_s[pl.ds(r0, rc), :], apad[pl.ds(r0, rc), :]
            hprev = _down(hpad[pl.ds(r0, rc + h8), :], 1)[h8:]
            first = (r0 + lax.broadcasted_iota(jnp.int32, u.shape, 0)) == 0
            log_a = (-LRU_C) * r * sp
            mult = jnp.where(first, 1.0, jnp.sqrt(-jnp.tanh(log_a) * (a * a + 1.0)))
            dmult = jnp.where(first, 0.0, g * (ig * u))
            dlog_a = g * hprev * a - dmult * (a * a) / mult
            dr = dlog_a * ((-LRU_C) * sp)
            dpre_r = dr * r * (1.0 - r)
            dpre_i = (g * mult * u) * ig * (1.0 - ig)
            pr, pi, ub = dpre_r.astype(BF16), dpre_i.astype(BF16), u.astype(BF16)
            nt = (((1,), (1,)), ((), ()))
            tn = (((0,), (0,)), ((), ()))
            du = (g * mult * ig + lax.dot_general(pr, wa, nt, preferred_element_type=F32)
                  + lax.dot_general(pi, wx, nt, preferred_element_type=F32))
            dupad[pl.ds(r0, rc), :] = du
            return (dsp + jnp.sum(dlog_a * ((-LRU_C) * r), axis=0, keepdims=True),
                    dba + jnp.sum(dpre_r, axis=0, keepdims=True),
                    dbx + jnp.sum(dpre_i, axis=0, keepdims=True),
                    dwa + lax.dot_general(ub, pr, tn, preferred_element_type=F32),
                    dwx + lax.dot_general(ub, pi, tn, preferred_element_type=F32))

        dsp, dba, dbx, dwa, dwx = lax.fori_loop(0, t // rc, grads, (zrow, zrow, zrow, zmat, zmat))
        dba_ref[...] = dba
        dbx_ref[...] = dbx
        dwa_ref[...] = dwa
        dwx_ref[...] = dwx
        dlam_ref[...] = -dsp * jax.nn.sigmoid(-lam)

        def conv_back(i, carry):
            dcb, d0, d1, d2, d3 = carry
            r0 = pl.multiple_of(i * rc, rc)
            dwin = dupad[pl.ds(r0, rc + h8), :]
            du = dwin[:rc]
            du0 = (cw_ref[3:4, :] * du + cw_ref[2:3, :] * _up(dwin, 1)[:rc]
                   + cw_ref[1:2, :] * _up(dwin, 2)[:rc] + cw_ref[0:1, :] * _up(dwin, 3)[:rc])
            dproj_ref[0, pl.ds(r0, rc), :] = du0.astype(BF16)
            win = upad[pl.ds(r0, rc + h8), :]
            red = lambda v: jnp.sum(v, axis=0, keepdims=True)
            return (dcb + red(du), d0 + red(du * _down(win, 3)[h8:]), d1 + red(du * _down(win, 2)[h8:]),
                    d2 + red(du * _down(win, 1)[h8:]), d3 + red(du * win[h8:]))

        dcb, d0, d1, d2, d3 = lax.fori_loop(0, t // rc, conv_back, (zrow,) * 5)
        dcb_ref[...] = dcb
        dcw_ref[0:1, :] = d0
        dcw_ref[1:2, :] = d1
        dcw_ref[2:3, :] = d2
        dcw_ref[3:4, :] = d3

    blk = lambda i: (0, i)
    vec = jax.ShapeDtypeStruct((1, c), F32)
    mat = jax.ShapeDtypeStruct((heads, LRU_BW, LRU_BW), F32)
    full = lambda: pltpu.VMEM((t, LRU_BW), F32)
    padded = lambda: pltpu.VMEM((t + h8, LRU_BW), F32)
    return pl.pallas_call(
        body,
        name=name,
        grid=(heads,),
        in_specs=_lru_in_specs(t, heads)[:1] + [pl.BlockSpec((t, LRU_BW), blk)] + _lru_in_specs(t, heads)[1:],
        out_specs=[pl.BlockSpec((2, t, LRU_BW), lambda i: (0, 0, i)), pl.BlockSpec((CONV_WIDTH, LRU_BW), blk)]
        + [pl.BlockSpec((1, LRU_BW), blk)] * 4 + [pl.BlockSpec((None, LRU_BW, LRU_BW), lambda i: (i, 0, 0))] * 2,
        out_shape=[jax.ShapeDtypeStruct((2, t, c), BF16), jax.ShapeDtypeStruct((CONV_WIDTH, c), F32), vec, vec, vec, vec, mat, mat],
        scratch_shapes=[padded(), full(), full(), full(), padded(), padded(), full(), full(), padded()],
        compiler_params=_params("parallel"),
    )(proj, dhg, conv_w, conv_b, wa, ba, wx, bx, lam)


def _pick_level(g, levels):
    out = levels[-1]
    for k in range(len(levels) - 2, -1, -1):
        out = jnp.where(g == k, levels[k], out)
    return out


def _pool_z(win, g, row0, rc):
    levels, cur = [], win
    for k in range(len(POOL_WINDOWS)):
        cur = cur + _down(cur, 1 << k)
        levels.append(cur[POOL_HALO:])
    tot = _pick_level(g, levels)
    width = jnp.left_shift(2, g)
    row = row0 + lax.broadcasted_iota(jnp.int32, tot.shape, 0)
    cnt = jnp.minimum(row + 1, width).astype(F32)
    return tot / cnt - win[POOL_HALO:], cnt


def _pool_specs(t, gw):
    blk = lambda g: (0, g)
    return [pl.BlockSpec((t, gw), blk), pl.BlockSpec((None, gw, gw), lambda g: (g, 0, 0)),
            pl.BlockSpec((1, gw), blk), pl.BlockSpec((1, gw), blk)]


def pool_fwd(name, u, w_grp, b_grp, scale):
    t, d = u.shape
    gw = d // len(POOL_WINDOWS)
    rc = min(256, t)

    def body(u_ref, wg_ref, bg_ref, sc_ref, out_ref, upad):
        g = pl.program_id(0)
        upad[0:POOL_HALO, :] = jnp.zeros((POOL_HALO, gw), F32)
        upad[POOL_HALO:, :] = u_ref[...]
        wg, bg, sc = wg_ref[...], bg_ref[...], sc_ref[...]

        def chunk(i, carry):
            r0 = pl.multiple_of(i * rc, rc)
            z, _ = _pool_z(upad[pl.ds(r0, rc + POOL_HALO), :], g, r0, rc)
            z2 = jnp.dot(z.astype(BF16), wg, preferred_element_type=F32) + bg
            out_ref[pl.ds(r0, rc), :] = (z2 * sc).astype(BF16)
            return carry

        lax.fori_loop(0, t // rc, chunk, 0)

    return pl.pallas_call(
        body,
        name=name,
        grid=(len(POOL_WINDOWS),),
        in_specs=_pool_specs(t, gw),
        out_specs=pl.BlockSpec((t, gw), lambda g: (0, g)),
        out_shape=jax.ShapeDtypeStruct((t, d), BF16),
        scratch_shapes=[pltpu.VMEM((t + POOL_HALO, gw), F32)],
        compiler_params=_params("parallel"),
    )(u, w_grp, b_grp, scale)


def pool_bwd(name, u, dzs, w_grp, b_grp, scale):
    t, d = u.shape
    gw = d // len(POOL_WINDOWS)
    rc = min(256, t)

    def body(u_ref, dzs_ref, wg_ref, bg_ref, sc_ref, du_ref, dwg_ref, dbg_ref, dsc_ref, upad, qpad, dz_s):
        g = pl.program_id(0)
        upad[0:POOL_HALO, :] = jnp.zeros((POOL_HALO, gw), F32)
        upad[POOL_HALO:, :] = u_ref[...]
        qpad[t:, :] = jnp.zeros((POOL_HALO, gw), F32)
        wg, bg, sc = wg_ref[...], bg_ref[...], sc_ref[...]
        zrow = jnp.zeros((1, gw), F32)

        def chunk(i, carry):
            dsc, dbg, dwg = carry
            r0 = pl.multiple_of(i * rc, rc)
            z, cnt = _pool_z(upad[pl.ds(r0, rc + POOL_HALO), :], g, r0, rc)
            zb = z.astype(BF16)
            z2 = jnp.dot(zb, wg, preferred_element_type=F32) + bg
            dzs = dzs_ref[pl.ds(r0, rc), :]
            dz2 = dzs * sc
            d2b = dz2.astype(BF16)
            dz = lax.dot_general(d2b, wg, (((1,), (1,)), ((), ())), preferred_element_type=F32)
            dz_s[pl.ds(r0, rc), :] = dz
            qpad[pl.ds(r0, rc), :] = dz / cnt
            return (dsc + jnp.sum(dzs * z2, axis=0, keepdims=True), dbg + jnp.sum(dz2, axis=0, keepdims=True),
                    dwg + lax.dot_general(zb, d2b, (((0,), (0,)), ((), ())), preferred_element_type=F32))

        dsc, dbg, dwg = lax.fori_loop(0, t // rc, chunk, (zrow, zrow, jnp.zeros((gw, gw), F32)))
        dsc_ref[...] = dsc
        dbg_ref[...] = dbg
        dwg_ref[...] = dwg

        def spread(i, carry):
            r0 = pl.multiple_of(i * rc, rc)
            levels, cur = [], qpad[pl.ds(r0, rc + POOL_HALO), :]
            for k in range(len(POOL_WINDOWS)):
                cur = cur + _up(cur, 1 << k)
                levels.append(cur[:rc])
            du_ref[pl.ds(r0, rc), :] = (_pick_level(g, levels) - dz_s[pl.ds(r0, rc), :]).astype(BF16)
            return carry

        lax.fori_loop(0, t // rc, spread, 0)

    blk = lambda g: (0, g)
    vec = jax.ShapeDtypeStruct((1, d), F32)
    return pl.pallas_call(
        body,
        name=name,
        grid=(len(POOL_WINDOWS),),
        in_specs=_pool_specs(t, gw)[:1] + [pl.BlockSpec((t, gw), blk)] + _pool_specs(t, gw)[1:],
        out_specs=[pl.BlockSpec((t, gw), blk), pl.BlockSpec((None, gw, gw), lambda g: (g, 0, 0)),
                   pl.BlockSpec((1, gw), blk), pl.BlockSpec((1, gw), blk)],
        out_shape=[jax.ShapeDtypeStruct((t, d), BF16), jax.ShapeDtypeStruct((len(POOL_WINDOWS), gw, gw), F32), vec, vec],
        scratch_shapes=[pltpu.VMEM((t + POOL_HALO, gw), F32), pltpu.VMEM((t + POOL_HALO, gw), F32), pltpu.VMEM((t, gw), F32)],
        compiler_params=_params("parallel"),
    )(u, dzs, w_grp, b_grp, scale)


def _place():
    return lax.axis_index("x"), lax.axis_index("y"), lax.axis_index("c")


def _other_chips(x, y):
    return [(1 - x, y), (x, 1 - y), (1 - x, 1 - y)]


def _half(c, rows):
    h = rows // 2
    return pl.ds(pl.multiple_of(c * h, 8), h)


_ANY = pl.BlockSpec(memory_space=pl.ANY)


def all_gather_chips(name, shards):
    n = len(shards)

    def body(*refs):
        ins, outs = refs[:n], refs[n:2 * n]
        send_sems, recv_sems, local_sems = refs[2 * n:]
        x, y, c = _place()
        me, sibling = 2 * x + y, (x, y, 1 - c)
        chips = _other_chips(x, y)

        def copy(i, slot, block, half, to, src=None):
            rows = _half(half, ins[i].shape[0])
            dst = outs[i].at[block, rows, :]
            return pltpu.make_async_remote_copy(
                src_ref=dst if src is None else src, dst_ref=dst, send_sem=send_sems.at[i * 6 + slot],
                recv_sem=recv_sems.at[i * 6 + slot], device_id=to, device_id_type=MESH)

        local = [pltpu.make_async_copy(ins[i], outs[i].at[me], local_sems.at[i]) for i in range(n)]
        for cp in local:
            cp.start()
        first = []
        for i in range(n):
            for j, chip in enumerate(chips):
                first.append(copy(i, j, me, c, (*chip, c), src=ins[i].at[_half(c, ins[i].shape[0]), :]))
        for cp in first:
            cp.start()
        passed = []
        for i in range(n):
            for j, (cx, cy) in enumerate(chips):
                copy(i, j, 2 * cx + cy, c, (x, y, c)).wait_recv()
                fwd = copy(i, 3 + j, 2 * cx + cy, c, sibling)
                fwd.start()
                passed.append(fwd)
        for i in range(n):
            for j, (cx, cy) in enumerate(chips):
                copy(i, 3 + j, 2 * cx + cy, 1 - c, (x, y, c)).wait_recv()
        for cp in first + passed:
            cp.wait_send()
        for cp in local:
            cp.wait()

    return pl.pallas_call(
        body,
        name=name,
        in_specs=[_ANY] * n,
        out_specs=[_ANY] * n,
        out_shape=[jax.ShapeDtypeStruct((N_CHIPS,) + s.shape, s.dtype) for s in shards],
        scratch_shapes=[pltpu.SemaphoreType.DMA((6 * n,)), pltpu.SemaphoreType.DMA((6 * n,)), pltpu.SemaphoreType.DMA((n,))],
    )(*shards)


def pair_exchange(name, grads):
    n = len(grads)

    def body(*refs):
        ins, outs = refs[:n], refs[n:2 * n]
        send_sems, recv_sems = refs[2 * n:]
        x, y, c = _place()
        copies = [pltpu.make_async_remote_copy(
            src_ref=ins[i].at[:, _half(1 - c, ins[i].shape[1]), :], dst_ref=outs[i], send_sem=send_sems.at[i],
            recv_sem=recv_sems.at[i], device_id=(x, y, 1 - c), device_id_type=MESH) for i in range(n)]
        for cp in copies:
            cp.start()
        for cp in copies:
            cp.wait()

    return pl.pallas_call(
        body,
        name=name,
        in_specs=[_ANY] * n,
        out_specs=[_ANY] * n,
        out_shape=[jax.ShapeDtypeStruct((g.shape[0], g.shape[1] // 2, g.shape[2]), g.dtype) for g in grads],
        scratch_shapes=[pltpu.SemaphoreType.DMA((n,)), pltpu.SemaphoreType.DMA((n,))],
    )(*grads)


def chip_exchange(name, parts):
    n = len(parts)

    def body(*refs):
        ins, outs = refs[:n], refs[n:2 * n]
        send_sems, recv_sems, local_sems = refs[2 * n:]
        x, y, c = _place()
        me = 2 * x + y
        chips = _other_chips(x, y)
        local = [pltpu.make_async_copy(ins[i].at[me], outs[i].at[me], local_sems.at[i]) for i in range(n)]
        for cp in local:
            cp.start()
        copies = []
        for i in range(n):
            for j, (cx, cy) in enumerate(chips):
                copies.append(pltpu.make_async_remote_copy(
                    src_ref=ins[i].at[2 * cx + cy], dst_ref=outs[i].at[me], send_sem=send_sems.at[3 * i + j],
                    recv_sem=recv_sems.at[3 * i + j], device_id=(cx, cy, c), device_id_type=MESH))
        for cp in copies:
            cp.start()
        for cp in copies:
            cp.wait()
        for cp in local:
            cp.wait()

    return pl.pallas_call(
        body,
        name=name,
        in_specs=[_ANY] * n,
        out_specs=[_ANY] * n,
        out_shape=[jax.ShapeDtypeStruct(p.shape, p.dtype) for p in parts],
        scratch_shapes=[pltpu.SemaphoreType.DMA((3 * n,)), pltpu.SemaphoreType.DMA((3 * n,)), pltpu.SemaphoreType.DMA((n,))],
    )(*parts)


def pair_gather(name, halves):
    n = len(halves)

    def body(*refs):
        ins, outs = refs[:n], refs[n:2 * n]
        send_sems, recv_sems, local_sems = refs[2 * n:]
        x, y, c = _place()
        local, copies = [], []
        for i in range(n):
            mine = outs[i].at[_half(c, outs[i].shape[0]), :]
            local.append(pltpu.make_async_copy(ins[i], mine, local_sems.at[i]))
            copies.append(pltpu.make_async_remote_copy(
                src_ref=ins[i], dst_ref=mine, send_sem=send_sems.at[i], recv_sem=recv_sems.at[i],
                device_id=(x, y, 1 - c), device_id_type=MESH))
        for cp in local + copies:
            cp.start()
        for cp in copies:
            cp.wait()
        for cp in local:
            cp.wait()

    return pl.pallas_call(
        body,
        name=name,
        in_specs=[_ANY] * n,
        out_specs=[_ANY] * n,
        out_shape=[jax.ShapeDtypeStruct((2 * h.shape[0], h.shape[1]), h.dtype) for h in halves],
        scratch_shapes=[pltpu.SemaphoreType.DMA((n,)), pltpu.SemaphoreType.DMA((n,)), pltpu.SemaphoreType.DMA((n,))],
    )(*halves)


def pair_sum(name, grad, recv, core):
    _, r, c = grad.shape
    h = r // 2
    th = _tile(h, 512, 8)
    per = h // th

    def body(core_ref, g_ref, r_ref, o_ref):
        o_ref[...] = g_ref[...] + r_ref[...]

    return pl.pallas_call(
        body,
        name=name,
        grid_spec=pltpu.PrefetchScalarGridSpec(
            num_scalar_prefetch=1,
            grid=(N_CHIPS, per),
            in_specs=[pl.BlockSpec((None, th, c), lambda k, i, core_ref: (k, core_ref[0] * per + i, 0)),
                      pl.BlockSpec((None, th, c), lambda k, i, core_ref: (k, i, 0))],
            out_specs=pl.BlockSpec((None, th, c), lambda k, i, core_ref: (k, i, 0)),
        ),
        out_shape=jax.ShapeDtypeStruct((N_CHIPS, h, c), F32),
        compiler_params=_params("parallel", "parallel"),
    )(core, grad, recv)


def chip_sum(name, parts):
    _, h, c = parts.shape
    th = _tile(h, 512, 8)

    def body(p_ref, o_ref):
        o_ref[...] = ((p_ref[0] + p_ref[1]) + p_ref[2]) + p_ref[3]

    return pl.pallas_call(
        body,
        name=name,
        grid=(h // th,),
        in_specs=[pl.BlockSpec((N_CHIPS, th, c), lambda i: (0, i, 0))],
        out_specs=pl.BlockSpec((th, c), lambda i: (i, 0)),
        out_shape=jax.ShapeDtypeStruct((h, c), F32),
        compiler_params=_params("parallel"),
    )(parts)


def adamw(name, w, g, m, v):
    r, c = w.shape
    tr = _tile(r, 512, 8)
    c1 = 1.0 - ADAM_B1 ** ADAM_STEP
    c2 = 1.0 - ADAM_B2 ** ADAM_STEP

    def body(w_ref, g_ref, m_ref, v_ref, d_ref, nm_ref, nv_ref):
        gv = g_ref[...]
        nm = ADAM_B1 * m_ref[...] + (1.0 - ADAM_B1) * gv
        nv = ADAM_B2 * v_ref[...] + (1.0 - ADAM_B2) * (gv * gv)
        d_ref[...] = -ADAM_LR * ((nm / c1) / (jnp.sqrt(nv / c2) + ADAM_EPS) + ADAM_WD * w_ref[...])
        nm_ref[...] = nm
        nv_ref[...] = nv

    spec = pl.BlockSpec((tr, c), lambda i: (i, 0))
    return pl.pallas_call(
        body,
        name=name,
        grid=(r // tr,),
        in_specs=[spec] * 4,
        out_specs=[spec] * 3,
        out_shape=[jax.ShapeDtypeStruct((r, c), F32)] * 3,
        compiler_params=_params("parallel"),
    )(w, g, m, v)


def reduce_to_shards(tag, grads, core):
    recv = pair_exchange(tag + "_pair_exchange", grads)
    parts = [pair_sum(f"{tag}_pair_sum_{i}", g, r, core) for i, (g, r) in enumerate(zip(grads, recv))]
    got = chip_exchange(tag + "_chip_exchange", parts)
    halves = [chip_sum(f"{tag}_chip_sum_{i}", q) for i, q in enumerate(got)]
    return pair_gather(tag + "_pair_gather", halves)


def _pack(arrays, row_multiple, cols=BLOB_COLS):
    flat = jnp.concatenate([a.reshape(-1).astype(F32) for a in arrays])
    rows = -(-flat.shape[0] // cols)
    rows = -(-rows // row_multiple) * row_multiple
    return jnp.pad(flat, (0, rows * cols - flat.shape[0])).reshape(rows, cols)


def _unpack(blob, shapes):
    flat, out, off = blob.reshape(-1), [], 0
    for s in shapes:
        size = math.prod(s)
        out.append(flat[off:off + size].reshape(s))
        off += size
    return out


def _unpack_rows(blobs, shapes):
    out, off = [], 0
    for s in shapes:
        size = math.prod(s)
        out.append(blobs[:, off:off + size].reshape((blobs.shape[0],) + tuple(s)))
        off += size
    return out


def kernel(x, p, lru_w_in, lru_conv_w, lru_conv_b, lru_wa, lru_ba, lru_wx, lru_bx, lru_lambda, lru_w_out, pool_w_in, pool_w_grp, pool_b_grp, pool_scale, pool_w_out, ln_mix_g, ln_mix_b, mlp_w1, mlp_w2, ln_mlp_g, ln_mlp_b, ple_w, ple_gate_w, ple_gate_b, loss_target, m_lru_w_in, m_lru_conv_w, m_lru_conv_b, m_lru_wa, m_lru_ba, m_lru_wx, m_lru_bx, m_lru_lambda, m_lru_w_out, m_pool_w_in, m_pool_w_grp, m_pool_b_grp, m_pool_scale, m_pool_w_out, m_ln_mix_g, m_ln_mix_b, m_mlp_w1, m_mlp_w2, m_ln_mlp_g, m_ln_mlp_b, m_ple_w, m_ple_gate_w, m_ple_gate_b, v_lru_w_in, v_lru_conv_w, v_lru_conv_b, v_lru_wa, v_lru_ba, v_lru_wx, v_lru_bx, v_lru_lambda, v_lru_w_out, v_pool_w_in, v_pool_w_grp, v_pool_b_grp, v_pool_scale, v_pool_w_out, v_ln_mix_g, v_ln_mix_b, v_mlp_w1, v_mlp_w2, v_ln_mlp_g, v_ln_mlp_b, v_ple_w, v_ple_gate_w, v_ple_gate_b):
    weights = dict(lru_w_in=lru_w_in, lru_conv_w=lru_conv_w, lru_conv_b=lru_conv_b, lru_wa=lru_wa, lru_ba=lru_ba, lru_wx=lru_wx, lru_bx=lru_bx, lru_lambda=lru_lambda, lru_w_out=lru_w_out, pool_w_in=pool_w_in, pool_w_grp=pool_w_grp, pool_b_grp=pool_b_grp, pool_scale=pool_scale, pool_w_out=pool_w_out, ln_mix_g=ln_mix_g, ln_mix_b=ln_mix_b, mlp_w1=mlp_w1, mlp_w2=mlp_w2, ln_mlp_g=ln_mlp_g, ln_mlp_b=ln_mlp_b, ple_w=ple_w, ple_gate_w=ple_gate_w, ple_gate_b=ple_gate_b)
    mom_m = dict(lru_w_in=m_lru_w_in, lru_conv_w=m_lru_conv_w, lru_conv_b=m_lru_conv_b, lru_wa=m_lru_wa, lru_ba=m_lru_ba, lru_wx=m_lru_wx, lru_bx=m_lru_bx, lru_lambda=m_lru_lambda, lru_w_out=m_lru_w_out, pool_w_in=m_pool_w_in, pool_w_grp=m_pool_w_grp, pool_b_grp=m_pool_b_grp, pool_scale=m_pool_scale, pool_w_out=m_pool_w_out, ln_mix_g=m_ln_mix_g, ln_mix_b=m_ln_mix_b, mlp_w1=m_mlp_w1, mlp_w2=m_mlp_w2, ln_mlp_g=m_ln_mlp_g, ln_mlp_b=m_ln_mlp_b, ple_w=m_ple_w, ple_gate_w=m_ple_gate_w, ple_gate_b=m_ple_gate_b)
    mom_v = dict(lru_w_in=v_lru_w_in, lru_conv_w=v_lru_conv_w, lru_conv_b=v_lru_conv_b, lru_wa=v_lru_wa, lru_ba=v_lru_ba, lru_wx=v_lru_wx, lru_bx=v_lru_bx, lru_lambda=v_lru_lambda, lru_w_out=v_lru_w_out, pool_w_in=v_pool_w_in, pool_w_grp=v_pool_w_grp, pool_b_grp=v_pool_b_grp, pool_scale=v_pool_scale, pool_w_out=v_pool_w_out, ln_mix_g=v_ln_mix_g, ln_mix_b=v_ln_mix_b, mlp_w1=v_mlp_w1, mlp_w2=v_mlp_w2, ln_mlp_g=v_ln_mlp_g, ln_mlp_b=v_ln_mlp_b, ple_w=v_ple_w, ple_gate_w=v_ple_gate_w, ple_gate_b=v_ple_gate_b)
    names = list(weights)

    depth, d = ln_mix_g.shape
    t = x.shape[1]
    n_a, n_b = lru_w_in.shape[0], pool_w_in.shape[0]
    d_rnn = lru_w_out.shape[1] * N_CHIPS
    heads = d_rnn // LRU_BW
    d_ff = mlp_w1.shape[2] * N_CHIPS
    ple_dim = ple_w.shape[1]
    n_grp = len(POOL_WINDOWS)
    gw = d // n_grp
    alpha = (2 * depth) ** 0.25
    chip = 2 * lax.axis_index("x") + lax.axis_index("y")
    core = lax.axis_index("c").astype(jnp.int32).reshape(1)

    x2d = x.reshape(t, d)
    target = loss_target.reshape(t, d)
    p3 = p.reshape(depth, t, ple_dim)

    big = ["lru_w_in", "lru_w_out", "pool_w_in", "pool_w_out", "mlp_w1", "mlp_w2", "ple_w", "ple_gate_w", "pool_w_grp"]
    flat2 = lambda a: a.reshape(-1, a.shape[-1])
    small_sharded = ["lru_conv_w", "pool_b_grp", "pool_scale"]
    small_blob = _pack([weights[k] for k in small_sharded], 16, cols=256)
    gathered = all_gather_chips("gather_weights", [flat2(weights[k]).astype(BF16) for k in big] + [small_blob])
    wg = dict(zip(big, gathered[:-1]))
    conv_w_sh, b_grp_sh, scale_sh = _unpack_rows(gathered[-1].reshape(N_CHIPS, -1), [weights[k].shape for k in small_sharded])
    conv_w_full = jnp.moveaxis(conv_w_sh, 0, 2).reshape(n_a, CONV_WIDTH, d_rnn)
    b_grp_full = jnp.moveaxis(b_grp_sh, 0, 1).reshape(n_b, 1, d)
    scale_full = jnp.moveaxis(scale_sh, 0, 1).reshape(n_b, 1, d)
    rows_grp = gw // N_CHIPS
    w_grp_full = jnp.moveaxis(wg["pool_w_grp"].reshape(N_CHIPS, n_b, n_grp, rows_grp, gw), 0, 2).reshape(n_b, n_grp, gw, gw)
    wa_bf, wx_bf = lru_wa.astype(BF16), lru_wx.astype(BF16)
    row = lambda a, i: a[i].reshape(1, -1)

    saved = []
    cur, cur_bf = x2d, x2d.astype(BF16)
    for i in range(depth):
        slot = i // 2
        sv = dict(x_bf=cur_bf)
        if i % 2 == 0:
            (proj,) = matmul(f"l{i}_lru_in", plain(cur_bf), colsplit(wg["lru_w_in"], slot, d), "nn",
                             [colsplit(None, 0, t, n=2, full=(2, t, d_rnn), dtype=F32)])
            hg = lru_fwd(f"l{i}_lru", proj, conv_w_full[slot], row(lru_conv_b, slot), wa_bf[slot], row(lru_ba, slot),
                         wx_bf[slot], row(lru_bx, slot), row(lru_lambda, slot))
            (mix,) = matmul(f"l{i}_lru_out", plain(hg), rowsplit(wg["lru_w_out"], slot, d_rnn // N_CHIPS), "nn",
                            [plain(shape=(t, d), dtype=F32)])
            sv.update(proj=proj, act=hg)
        else:
            (u,) = matmul(f"l{i}_pool_in", plain(cur_bf), rowsplit(wg["pool_w_in"], slot, d // N_CHIPS), "nn",
                          [plain(shape=(t, d), dtype=F32)])
            zs = pool_fwd(f"l{i}_pool", u, w_grp_full[slot], b_grp_full[slot], scale_full[slot])
            (mix,) = matmul(f"l{i}_pool_out", plain(zs), rowsplit(wg["pool_w_out"], slot, d // N_CHIPS), "nn",
                            [plain(shape=(t, d), dtype=F32)])
            sv.update(u=u, act=zs)
        x1, x1_bf, s1 = ln_fwd(f"l{i}_ln_mix", alpha, cur, mix, row(ln_mix_g, i), row(ln_mix_b, i))

        def relu2(acc):
            hr = jnp.maximum(acc, 0.0)
            return hr, hr * hr

        hr, hh = matmul(f"l{i}_mlp_up", plain(x1_bf), colsplit(wg["mlp_w1"], i, d), "nn",
                        [plain(shape=(t, d_ff), dtype=F32), plain(shape=(t, d_ff), dtype=BF16)], epilogue=relu2)
        (mlp,) = matmul(f"l{i}_mlp_down", plain(hh), rowsplit(wg["mlp_w2"], i, d_ff // N_CHIPS), "nn",
                        [plain(shape=(t, d), dtype=F32)])
        x2, x2_bf, s2 = ln_fwd(f"l{i}_ln_mlp", alpha, x1, mlp, row(ln_mlp_g, i), row(ln_mlp_b, i))
        (e,) = matmul(f"l{i}_ple", plain(p3[i]), colsplit(wg["ple_w"], i, ple_dim), "nn", [plain(shape=(t, d), dtype=F32)])

        def ple_out(acc, x2_t, e_t, gb):
            gate = jax.nn.sigmoid(acc + gb)
            x3 = x2_t + e_t * gate
            return x3, x3, gate

        cur, cur_bf, gate = matmul(f"l{i}_ple_gate", plain(x2_bf), rowsplit(wg["ple_gate_w"], i, d // N_CHIPS), "nn",
                                   [plain(shape=(t, d), dtype=F32), plain(shape=(t, d), dtype=BF16), plain(shape=(t, d), dtype=F32)],
                                   epilogue=ple_out, tiles=[plain(x2), plain(e)], rows=[row(ple_gate_b, i)])
        sv.update(s1=s1, x1_bf=x1_bf, hr=hr, hh=hh, s2=s2, x2_bf=x2_bf, gate=gate, e=e)
        saved.append(sv)

    dy, loss_part = loss_head("loss", cur, target)
    loss = lax.psum(loss_part.reshape(()), ("x", "y", "c"))

    part = {}

    def grad_view(key, layer, split):
        w = weights[key]
        full = (N_CHIPS, w.shape[0] * w.shape[1], w.shape[2])
        return split(part.get(key), layer, w.shape[1], full=full, dtype=F32)

    small = {k: [None] * weights[k].shape[0] for k in names if k not in big or k == "pool_w_grp"}
    dcur = dy
    for i in reversed(range(depth)):
        slot = i // 2
        sv = saved[i]
        de, dpre, dgb = ple_bwd(f"l{i}_ple_bwd", dcur, sv["gate"], sv["e"])
        small["ple_gate_b"][i] = dgb
        (part["ple_w"],) = matmul(f"l{i}_d_ple_w", plain(p3[i]), plain(de), "tn", [grad_view("ple_w", i, colsplit)])
        (part["ple_gate_w"],) = matmul(f"l{i}_d_ple_gate_w", plain(sv["x2_bf"]), plain(dpre), "tn",
                                       [grad_view("ple_gate_w", i, rowsplit)])
        (dx2b,) = matmul(f"l{i}_dx2", plain(dpre), rowsplit(wg["ple_gate_w"], i, d // N_CHIPS), "nt",
                         [plain(shape=(t, d), dtype=F32)])
        ds2, ds2_bf, dg, db = ln_bwd(f"l{i}_ln_mlp_bwd", 1.0, dcur, dx2b, sv["s2"], row(ln_mlp_g, i))
        small["ln_mlp_g"][i], small["ln_mlp_b"][i] = dg, db
        (part["mlp_w2"],) = matmul(f"l{i}_d_mlp_w2", plain(sv["hh"]), plain(ds2_bf), "tn", [grad_view("mlp_w2", i, rowsplit)])
        (dhpre,) = matmul(f"l{i}_dh", plain(ds2_bf), rowsplit(wg["mlp_w2"], i, d_ff // N_CHIPS), "nt",
                          [plain(shape=(t, d_ff), dtype=BF16)], epilogue=lambda acc, hr_t: (acc * (2.0 * hr_t),),
                          tiles=[plain(sv["hr"])])
        (part["mlp_w1"],) = matmul(f"l{i}_d_mlp_w1", plain(sv["x1_bf"]), plain(dhpre), "tn", [grad_view("mlp_w1", i, colsplit)])
        (dx1b,) = matmul(f"l{i}_dx1", plain(dhpre), colsplit(wg["mlp_w1"], i, d), "nt", [plain(shape=(t, d), dtype=F32)])
        ds1, ds1_bf, dg, db = ln_bwd(f"l{i}_ln_mix_bwd", alpha, ds2, dx1b, sv["s1"], row(ln_mix_g, i))
        small["ln_mix_g"][i], small["ln_mix_b"][i] = dg, db
        residual = lambda acc, ds_t: (alpha * ds_t + acc,)
        if i % 2 == 0:
            (part["lru_w_out"],) = matmul(f"l{i}_d_lru_out", plain(sv["act"]), plain(ds1_bf), "tn",
                                          [grad_view("lru_w_out", slot, rowsplit)])
            (dhg,) = matmul(f"l{i}_dhg", plain(ds1_bf), rowsplit(wg["lru_w_out"], slot, d_rnn // N_CHIPS), "nt",
                            [plain(shape=(t, d_rnn), dtype=F32)])
            dproj, dcw, dcb, dba, dbx, dlam, dwa, dwx = lru_bwd(
                f"l{i}_lru_bwd", sv["proj"], dhg, conv_w_full[slot], row(lru_conv_b, slot), wa_bf[slot], row(lru_ba, slot),
                wx_bf[slot], row(lru_bx, slot), row(lru_lambda, slot))
            for key, val in (("lru_conv_w", dcw), ("lru_conv_b", dcb), ("lru_ba", dba), ("lru_bx", dbx),
                             ("lru_lambda", dlam), ("lru_wa", dwa), ("lru_wx", dwx)):
                small[key][slot] = val
            dproj_v = colsplit(dproj, 0, t, n=2)
            (part["lru_w_in"],) = matmul(f"l{i}_d_lru_in", plain(sv["x_bf"]), dproj_v, "tn", [grad_view("lru_w_in", slot, colsplit)])
            (dcur,) = matmul(f"l{i}_dx", dproj_v, colsplit(wg["lru_w_in"], slot, d), "nt",
                             [plain(shape=(t, d), dtype=F32)], epilogue=residual, tiles=[plain(ds1)])
        else:
            (part["pool_w_out"],) = matmul(f"l{i}_d_pool_out", plain(sv["act"]), plain(ds1_bf), "tn",
                                           [grad_view("pool_w_out", slot, rowsplit)])
            (dzs,) = matmul(f"l{i}_dzs", plain(ds1_bf), rowsplit(wg["pool_w_out"], slot, d // N_CHIPS), "nt",
                            [plain(shape=(t, d), dtype=F32)])
            du, dwg, dbg, dsc = pool_bwd(f"l{i}_pool_bwd", sv["u"], dzs, w_grp_full[slot], b_grp_full[slot], scale_full[slot])
            small["pool_w_grp"][slot], small["pool_b_grp"][slot], small["pool_scale"][slot] = dwg, dbg, dsc
            (part["pool_w_in"],) = matmul(f"l{i}_d_pool_in", plain(sv["x_bf"]), plain(du), "tn", [grad_view("pool_w_in", slot, rowsplit)])
            (dcur,) = matmul(f"l{i}_dx", plain(du), rowsplit(wg["pool_w_in"], slot, d // N_CHIPS), "nt",
                             [plain(shape=(t, d), dtype=F32)], epilogue=residual, tiles=[plain(ds1)])
    grad_x = dcur.reshape(x.shape)

    big_w = [k for k in big if k != "pool_w_grp"]
    small_keys = [k for k in names if k not in big_w]
    small_full = [jnp.stack(small[k]).reshape((weights[k].shape[0],) + tuple(
        s * (N_CHIPS if ax in _sharded_axis(k) else 1) for ax, s in enumerate(weights[k].shape[1:], 1))) for k in small_keys]
    full_shapes = [a.shape for a in small_full]
    blob = _pack(small_full, 64)
    blob4 = blob.reshape(N_CHIPS, blob.shape[0] // N_CHIPS, BLOB_COLS)
    reduced = reduce_to_shards("grads", [part[k] for k in big_w] + [blob4], core)
    (blob_all,) = all_gather_chips("gather_small_grads", [reduced[-1]])
    small_grads = dict(zip(small_keys, _unpack(blob_all.reshape(blob.shape), full_shapes)))
    for k in small_keys:
        for ax in _sharded_axis(k):
            n = weights[k].shape[ax]
            small_grads[k] = lax.dynamic_slice_in_dim(small_grads[k], chip * n, n, axis=ax)
    grads = {k: reduced[j].reshape(weights[k].shape) for j, k in enumerate(big_w)}
    grads.update(small_grads)

    delta, new_m, new_v = {}, {}, {}
    for k in big_w:
        dl, nm, nv = adamw("adamw_" + k, flat2(weights[k]), flat2(grads[k]), flat2(mom_m[k]), flat2(mom_v[k]))
        delta[k], new_m[k], new_v[k] = (a.reshape(weights[k].shape) for a in (dl, nm, nv))
    shapes = [weights[k].shape for k in small_keys]
    dl, nm, nv = adamw("adamw_small", *[_pack([src[k] for k in small_keys], 8) for src in (weights, grads, mom_m, mom_v)])
    for out, blob_out in ((delta, dl), (new_m, nm), (new_v, nv)):
        out.update(zip(small_keys, _unpack(blob_out, shapes)))

    return (loss, grad_x, *[grads[k] for k in names], *[delta[k] for k in names],
            *[new_m[k] for k in names], *[new_v[k] for k in names])


def _sharded_axis(key):
    return {"lru_conv_w": (2,), "pool_w_grp": (2,), "pool_b_grp": (1,), "pool_scale": (1,)}.get(key, ())
```

```python
import functools
import math

import jax
import jax.numpy as jnp
from jax import lax
from jax.experimental import pallas as pl
from jax.experimental.pallas import tpu as pltpu

F32 = jnp.float32
BF16 = jnp.bfloat16

N_CHIPS = 4
LRU_BW = 128
LRU_C = 8.0
CONV_WIDTH = 4
POOL_WINDOWS = (2, 4, 8, 16)
POOL_HALO = 16
CONV_HALO = 8
LN_EPS = 1e-5
ADAM_LR = 0.001
ADAM_B1 = 0.9
ADAM_B2 = 0.999
ADAM_EPS = 1e-08
ADAM_WD = 0.01
ADAM_STEP = 10
GELU_C = math.sqrt(2.0 / math.pi)
GELU_K = 0.044715
VMEM_LIMIT_BYTES = 56 * 1024 * 1024
MESH = pl.DeviceIdType.MESH
BLOB_COLS = 1024


def _params(*sem):
    return pltpu.CompilerParams(dimension_semantics=tuple(sem), vmem_limit_bytes=VMEM_LIMIT_BYTES)


def _tile(unit, pref, align=128):
    if unit <= pref:
        return unit
    for d in range(2, unit + 1):
        if unit % d == 0 and unit // d <= pref and (unit // d) % align == 0:
            return unit // d
    raise ValueError((unit, pref, align))


class View:
    def __init__(self, arr, shape, row_unit, col_unit, block_fn, full=None, dtype=None):
        self.arr, self.shape, self.row_unit, self.col_unit, self.block_fn = arr, shape, row_unit, col_unit, block_fn
        self.full = full if full is not None else arr.shape
        self.dtype = dtype if dtype is not None else arr.dtype

    def spec(self, tr, tc, f):
        block, idx = self.block_fn(tr, tc)
        return pl.BlockSpec(block, lambda *g: idx(*f(*g)))


def plain(arr=None, shape=None, dtype=None):
    shape = arr.shape if arr is not None else shape
    return View(arr, shape, shape[0], shape[1], lambda tr, tc: ((tr, tc), lambda rt, ct: (rt, ct)), full=shape, dtype=dtype)


def colsplit(arr, layer, rows, n=N_CHIPS, full=None, dtype=None):
    full = arr.shape if arr is not None else full
    c = full[2]

    def block_fn(tr, tc):
        assert rows % tr == 0 and c % tc == 0, (rows, tr, c, tc)
        per, rpl = c // tc, rows // tr
        return (None, tr, tc), lambda rt, ct: (ct // per, layer * rpl + rt, ct % per)

    return View(arr, (rows, n * c), rows, c, block_fn, full=full, dtype=dtype)


def rowsplit(arr, layer, rows, n=N_CHIPS, full=None, dtype=None):
    full = arr.shape if arr is not None else full
    c = full[2]

    def block_fn(tr, tc):
        assert rows % tr == 0 and c % tc == 0, (rows, tr, c, tc)
        per = rows // tr
        return (None, tr, tc), lambda rt, ct: (rt // per, layer * per + rt % per, ct)

    return View(arr, (n * rows, c), rows, c, block_fn, full=full, dtype=dtype)


def matmul(name, a, b, mode, outs, epilogue=None, tiles=(), rows=(), pm=1024, pn=1024, pk=1024):
    if mode == "nn":
        (m, k), (k2, n) = a.shape, b.shape
        um, uk, un = a.row_unit, min(a.col_unit, b.row_unit), b.col_unit
        dims = (((1,), (0,)), ((), ()))
    elif mode == "nt":
        (m, k), (n, k2) = a.shape, b.shape
        um, uk, un = a.row_unit, min(a.col_unit, b.col_unit), b.row_unit
        dims = (((1,), (1,)), ((), ()))
    else:
        (k, m), (k2, n) = a.shape, b.shape
        um, uk, un = a.col_unit, min(a.row_unit, b.row_unit), b.col_unit
        dims = (((0,), (0,)), ((), ()))
    assert k == k2, (name, a.shape, b.shape)
    for o in list(outs) + list(tiles):
        assert o.shape == (m, n), (name, o.shape, m, n)
        um, un = min(um, o.row_unit), min(un, o.col_unit)
    tm, tn, tk = _tile(um, pm), _tile(un, pn), _tile(uk, pk)
    assert m % tm == 0 and n % tn == 0 and k % tk == 0, (name, m, n, k, tm, tn, tk)
    gm, gn, gk = m // tm, n // tn, k // tk

    if mode == "nn":
        a_spec = a.spec(tm, tk, lambda i, j, kk: (i, kk))
        b_spec = b.spec(tk, tn, lambda i, j, kk: (kk, j))
    elif mode == "nt":
        a_spec = a.spec(tm, tk, lambda i, j, kk: (i, kk))
        b_spec = b.spec(tn, tk, lambda i, j, kk: (j, kk))
    else:
        a_spec = a.spec(tk, tm, lambda i, j, kk: (kk, i))
        b_spec = b.spec(tk, tn, lambda i, j, kk: (kk, j))
    tile_specs = [t.spec(tm, tn, lambda i, j, kk: (i, j)) for t in tiles]
    row_specs = [pl.BlockSpec((1, tn), lambda i, j, kk: (0, j)) for _ in rows]
    in_place = [o for o in outs if o.arr is not None]
    alias_specs = [pl.BlockSpec(memory_space=pl.ANY) for _ in in_place]
    out_specs = [o.spec(tm, tn, lambda i, j, kk: (i, j)) for o in outs]
    n_in = 2 + len(tiles) + len(rows)
    aliases = {}
    for o_idx, o in enumerate(outs):
        if o.arr is not None:
            aliases[n_in + in_place.index(o)] = o_idx
    n_t, n_r, n_a, n_o = len(tiles), len(rows), len(in_place), len(outs)

    def body(*refs):
        a_ref, b_ref = refs[0], refs[1]
        tile_refs = refs[2:2 + n_t]
        row_refs = refs[2 + n_t:2 + n_t + n_r]
        out_refs = refs[2 + n_t + n_r + n_a:2 + n_t + n_r + n_a + n_o]
        acc_ref = refs[-1] if gk > 1 else None

        def finish(acc):
            extra = [t[...] for t in tile_refs] + [r[...] for r in row_refs]
            res = epilogue(acc, *extra) if epilogue is not None else (acc,)
            for o_ref, r in zip(out_refs, res):
                o_ref[...] = r.astype(o_ref.dtype)

        prod = lax.dot_general(a_ref[...].astype(BF16), b_ref[...].astype(BF16), dims, preferred_element_type=F32)
        if gk == 1:
            finish(prod)
        else:
            kk = pl.program_id(2)

            @pl.when(kk == 0)
            def _():
                acc_ref[...] = prod

            @pl.when(kk > 0)
            def _():
                acc_ref[...] += prod

            @pl.when(kk == gk - 1)
            def _():
                finish(acc_ref[...])

    res = pl.pallas_call(
        body,
        name=name,
        grid=(gm, gn, gk),
        in_specs=[a_spec, b_spec] + tile_specs + row_specs + alias_specs,
        out_specs=out_specs,
        out_shape=[jax.ShapeDtypeStruct(o.full, o.dtype) for o in outs],
        scratch_shapes=[pltpu.VMEM((tm, tn), F32)] if gk > 1 else [],
        input_output_aliases=aliases,
        compiler_params=_params("parallel", "parallel", "arbitrary"),
    )(a.arr, b.arr, *[t.arr for t in tiles], *rows, *[o.arr for o in in_place])
    return res


def rows_call(name, fn, tiled, vecs, tiled_out, acc_out, tr=256):
    t = tiled[0].shape[0]
    tr = min(tr, t)
    assert t % tr == 0
    n1, n2, n3 = len(tiled), len(vecs), len(tiled_out)

    def body(*refs):
        fn(pl.program_id(0), refs[:n1], refs[n1:n1 + n2], refs[n1 + n2:n1 + n2 + n3], refs[n1 + n2 + n3:])

    return pl.pallas_call(
        body,
        name=name,
        grid=(t // tr,),
        in_specs=[pl.BlockSpec((tr, x.shape[1]), lambda i: (i, 0)) for x in tiled]
        + [pl.BlockSpec(v.shape, lambda i: (0, 0)) for v in vecs],
        out_specs=[pl.BlockSpec((tr, c), lambda i: (i, 0)) for c, _ in tiled_out]
        + [pl.BlockSpec(s, lambda i: (0, 0)) for s, _ in acc_out],
        out_shape=[jax.ShapeDtypeStruct((t, c), d) for c, d in tiled_out] + [jax.ShapeDtypeStruct(s, d) for s, d in acc_out],
        compiler_params=_params("arbitrary" if acc_out else "parallel"),
    )(*tiled, *vecs)


def _accumulate(step, ref, val):
    @pl.when(step == 0)
    def _():
        ref[...] = val

    @pl.when(step > 0)
    def _():
        ref[...] += val


def _ln_stats(s):
    mu = jnp.mean(s, axis=-1, keepdims=True)
    d = s - mu
    var = jnp.mean(d * d, axis=-1, keepdims=True)
    rstd = lax.rsqrt(var + LN_EPS)
    return d * rstd, rstd


def ln_fwd(name, alpha, x_in, m, g, b):
    d = x_in.shape[1]

    def fn(step, tiled, vecs, outs, accs):
        s = alpha * tiled[0][...] + tiled[1][...]
        xhat, _ = _ln_stats(s)
        y = xhat * vecs[0][...] + vecs[1][...]
        outs[0][...] = y
        outs[1][...] = y.astype(BF16)
        outs[2][...] = s

    return rows_call(name, fn, [x_in, m], [g, b], [(d, F32), (d, BF16), (d, F32)], [])


def ln_bwd(name, ca, da, db, s, g):
    d = s.shape[1]

    def fn(step, tiled, vecs, outs, accs):
        dx = ca * tiled[0][...] + tiled[1][...]
        xhat, rstd = _ln_stats(tiled[2][...])
        dxh = dx * vecs[0][...]
        ds = rstd * (dxh - jnp.mean(dxh, axis=-1, keepdims=True) - xhat * jnp.mean(dxh * xhat, axis=-1, keepdims=True))
        outs[0][...] = ds
        outs[1][...] = ds.astype(BF16)
        _accumulate(step, accs[0], jnp.sum(dx * xhat, axis=0, keepdims=True))
        _accumulate(step, accs[1], jnp.sum(dx, axis=0, keepdims=True))

    return rows_call(name, fn, [da, db, s], [g], [(d, F32), (d, BF16)], [((1, d), F32), ((1, d), F32)])


def ple_bwd(name, dx3, gate, e):
    d = dx3.shape[1]

    def fn(step, tiled, vecs, outs, accs):
        dx, gt, ev = tiled[0][...], tiled[1][...], tiled[2][...]
        dpre = dx * ev * gt * (1.0 - gt)
        outs[0][...] = (dx * gt).astype(BF16)
        outs[1][...] = dpre.astype(BF16)
        _accumulate(step, accs[0], jnp.sum(dpre, axis=0, keepdims=True))

    return rows_call(name, fn, [dx3, gate, e], [], [(d, BF16), (d, BF16)], [((1, d), F32)])


def loss_head(name, y, target):
    t, d = y.shape

    def fn(step, tiled, vecs, outs, accs):
        err = tiled[0][...] - tiled[1][...]
        outs[0][...] = err * (1.0 / d)
        part = jnp.sum(jnp.sum(err * err, axis=1, keepdims=True), axis=0, keepdims=True) * (0.5 / d)
        _accumulate(step, accs[0], part)

    return rows_call(name, fn, [y, target], [], [(d, F32)], [((1, 1), F32)])


def _softplus(z):
    return jnp.maximum(z, 0.0) + jnp.log1p(jnp.exp(-jnp.abs(z)))


def _gelu(y):
    th = jnp.tanh(GELU_C * (y + GELU_K * (y * y * y)))
    cdf = 0.5 * (1.0 + th)
    return y * cdf, cdf + 0.5 * y * (1.0 - th * th) * (GELU_C * (1.0 + 3.0 * GELU_K * y * y))


def _up(win, k):
    return pltpu.roll(win, win.shape[0] - k, 0)


def _down(win, k):
    return pltpu.roll(win, k, 0)


def _lru_gates(win, row0, cw_ref, cb, wa, ba, wx, bx, sp):
    h = CONV_HALO
    u = (cb + cw_ref[3:4, :] * win[h:] + cw_ref[2:3, :] * _down(win, 1)[h:]
         + cw_ref[1:2, :] * _down(win, 2)[h:] + cw_ref[0:1, :] * _down(win, 3)[h:])
    ub = u.astype(BF16)
    r = jax.nn.sigmoid(jnp.dot(ub, wa, preferred_element_type=F32) + ba)
    ig = jax.nn.sigmoid(jnp.dot(ub, wx, preferred_element_type=F32) + bx)
    log_a = (-LRU_C) * r * sp
    a = jnp.exp(log_a)
    mult = jnp.sqrt(-jnp.tanh(log_a) * (a * a + 1.0))
    first = (row0 + lax.broadcasted_iota(jnp.int32, u.shape, 0)) == 0
    mult = jnp.where(first, 1.0, mult)
    return u, r, ig, a, mult, first


def _block_scan(a, b, reverse):
    pos = lax.broadcasted_iota(jnp.int32, a.shape, 0) & 7
    shift, spread = (_up, _down) if reverse else (_down, _up)
    for s in (1, 2, 4):
        keep = (pos >= 8 - s) if reverse else (pos < s)
        b = jnp.where(keep, b, a * shift(b, s) + b)
        a = jnp.where(keep, a, a * shift(a, s))
    at, bt = a, b
    for s in (1, 2, 4):
        keep = (pos < s) if reverse else (pos >= 8 - s)
        at = jnp.where(keep, at, spread(at, s))
        bt = jnp.where(keep, bt, spread(bt, s))
    return a, b, at, bt


def _carry_scan(a_ref, b_ref, at_ref, bt_ref, out_ref, out_off, t, reverse):
    groups = t // 8

    def block(j, h_in):
        r0 = pl.multiple_of((groups - 1 - j if reverse else j) * 8, 8)
        out_ref[pl.ds(pl.multiple_of(out_off + r0, 8), 8), :] = a_ref[pl.ds(r0, 8), :] * h_in + b_ref[pl.ds(r0, 8), :]
        return at_ref[pl.ds(r0, 8), :] * h_in + bt_ref[pl.ds(r0, 8), :]

    lax.fori_loop(0, groups, block, jnp.zeros((8, LRU_BW), F32), unroll=8)


def _lru_in_specs(t, heads):
    blk = lambda i: (0, i)
    return [
        pl.BlockSpec((2, t, LRU_BW), lambda i: (0, 0, i)),
        pl.BlockSpec((CONV_WIDTH, LRU_BW), blk),
        pl.BlockSpec((1, LRU_BW), blk),
        pl.BlockSpec((None, LRU_BW, LRU_BW), lambda i: (i, 0, 0)),
        pl.BlockSpec((1, LRU_BW), blk),
        pl.BlockSpec((None, LRU_BW, LRU_BW), lambda i: (i, 0, 0)),
        pl.BlockSpec((1, LRU_BW), blk),
        pl.BlockSpec((1, LRU_BW), blk),
    ]


def lru_fwd(name, proj, conv_w, conv_b, wa, ba, wx, bx, lam):
    _, t, c = proj.shape
    heads = c // LRU_BW
    rc = min(256, t)

    def body(proj_ref, cw_ref, cb_ref, wa_ref, ba_ref, wx_ref, bx_ref, lam_ref, out_ref, upad, a_s, b_s, at_s, bt_s):
        upad[0:CONV_HALO, :] = jnp.zeros((CONV_HALO, LRU_BW), F32)
        upad[CONV_HALO:, :] = proj_ref[0]
        sp = _softplus(-lam_ref[...])
        cb, ba, bx, wa, wx = cb_ref[...], ba_ref[...], bx_ref[...], wa_ref[...], wx_ref[...]

        def gates(i, carry):
            r0 = pl.multiple_of(i * rc, rc)
            win = upad[pl.ds(r0, rc + CONV_HALO), :]
            u, r, ig, a, mult, _ = _lru_gates(win, r0, cw_ref, cb, wa, ba, wx, bx, sp)
            rows = pl.ds(r0, rc)
            a_s[rows, :], b_s[rows, :], at_s[rows, :], bt_s[rows, :] = _block_scan(a, mult * (ig * u), False)
            return carry

        lax.fori_loop(0, t // rc, gates, 0)
        _carry_scan(a_s, b_s, at_s, bt_s, b_s, 0, t, False)

        def gate_out(i, carry):
            r0 = pl.multiple_of(i * rc, rc)
            gy, _ = _gelu(proj_ref[1, pl.ds(r0, rc), :])
            out_ref[pl.ds(r0, rc), :] = (b_s[pl.ds(r0, rc), :] * gy).astype(BF16)
            return carry

        lax.fori_loop(0, t // rc, gate_out, 0)

    return pl.pallas_call(
        body,
        name=name,
        grid=(heads,),
        in_specs=_lru_in_specs(t, heads),
        out_specs=pl.BlockSpec((t, LRU_BW), lambda i: (0, i)),
        out_shape=jax.ShapeDtypeStruct((t, c), BF16),
        scratch_shapes=[pltpu.VMEM((t + CONV_HALO, LRU_BW), F32)] + [pltpu.VMEM((t, LRU_BW), F32)] * 4,
        compiler_params=_params("parallel"),
    )(proj, conv_w, conv_b, wa, ba, wx, bx, lam)


def lru_bwd(name, proj, dhg, conv_w, conv_b, wa, ba, wx, bx, lam):
    _, t, c = proj.shape
    heads = c // LRU_BW
    rc = min(256, t)
    h8 = CONV_HALO

    def body(proj_ref, dhg_ref, cw_ref, cb_ref, wa_ref, ba_ref, wx_ref, bx_ref, lam_ref,
             dproj_ref, dcw_ref, dcb_ref, dba_ref, dbx_ref, dlam_ref, dwa_ref, dwx_ref,
             upad, u_s, r_s, ig_s, apad, hpad, g_s, dupad, sa_s, sb_s, at_s, bt_s):
        zeros8 = jnp.zeros((h8, LRU_BW), F32)
        upad[0:h8, :] = zeros8
        upad[h8:, :] = proj_ref[0]
        hpad[0:h8, :] = zeros8
        apad[t:, :] = zeros8
        dupad[t:, :] = zeros8
        lam = lam_ref[...]
        sp = _softplus(-lam)
        cb, ba, bx, wa, wx = cb_ref[...], ba_ref[...], bx_ref[...], wa_ref[...], wx_ref[...]

        def gates(i, carry):
            r0 = pl.multiple_of(i * rc, rc)
            win = upad[pl.ds(r0, rc + h8), :]
            u, r, ig, a, mult, _ = _lru_gates(win, r0, cw_ref, cb, wa, ba, wx, bx, sp)
            u_s[pl.ds(r0, rc), :] = u
            r_s[pl.ds(r0, rc), :] = r
            ig_s[pl.ds(r0, rc), :] = ig
            rows = pl.ds(r0, rc)
            apad[rows, :] = a
            sa_s[rows, :], sb_s[rows, :], at_s[rows, :], bt_s[rows, :] = _block_scan(a, mult * (ig * u), False)
            return carry

        lax.fori_loop(0, t // rc, gates, 0)
        _carry_scan(sa_s, sb_s, at_s, bt_s, hpad, h8, t, False)

        def out_gate(i, carry):
            r0 = pl.multiple_of(i * rc, rc)
            gy, dgy = _gelu(proj_ref[1, pl.ds(r0, rc), :])
            dh = dhg_ref[pl.ds(r0, rc), :]
            hh = hpad[pl.ds(pl.multiple_of(r0 + h8, 8), rc), :]
            dproj_ref[1, pl.ds(r0, rc), :] = (dh * hh * dgy).astype(BF16)
            rows = pl.ds(r0, rc)
            a_next = _up(apad[pl.ds(r0, rc + h8), :], 1)[:rc]
            sa_s[rows, :], sb_s[rows, :], at_s[rows, :], bt_s[rows, :] = _block_scan(a_next, dh * gy, True)
            return carry

        lax.fori_loop(0, t // rc, out_gate, 0)
        _carry_scan(sa_s, sb_s, at_s, bt_s, g_s, 0, t, True)

        zrow = jnp.zeros((1, LRU_BW), F32)
        zmat = jnp.zeros((LRU_BW, LRU_BW), F32)

        def grads(i, carry):
            dsp, dba, dbx, dwa, dwx = carry
            r0 = pl.multiple_of(i * rc, rc)
            g = g_s[pl.ds(r0, rc), :]
            u, r, ig, a = u_s[pl.ds(r0, rc), :], r_s[pl.ds(r0, rc), :], ig_s[pl.ds(r0, rc), :], apad[pl.ds(r0, rc), :]
            hprev = _down(hpad[pl.ds(r0, rc + h8), :], 1)[h8:]
            first = (r0 + lax.broadcasted_iota(jnp.int32, u.shape, 0)) == 0
            log_a = (-LRU_C) * r * sp
            mult = jnp.where(first, 1.0, jnp.sqrt(-jnp.tanh(log_a) * (a * a + 1.0)))
            dmult = jnp.where(first, 0.0, g * (ig * u))
            dlog_a = g * hprev * a - dmult * (a * a) / mult
            dr = dlog_a * ((-LRU_C) * sp)
            dpre_r = dr * r * (1.0 - r)
            dpre_i = (g * mult * u) * ig * (1.0 - ig)
            pr, pi, ub = dpre_r.astype(BF16), dpre_i.astype(BF16), u.astype(BF16)
            nt = (((1,), (1,)), ((), ()))
            tn = (((0,), (0,)), ((), ()))
            du = (g * mult * ig + lax.dot_general(pr, wa, nt, preferred_element_type=F32)
                  + lax.dot_general(pi, wx, nt, preferred_element_type=F32))
            dupad[pl.ds(r0, rc), :] = du
            return (dsp + jnp.sum(dlog_a * ((-LRU_C) * r), axis=0, keepdims=True),
                    dba + jnp.sum(dpre_r, axis=0, keepdims=True),
                    dbx + jnp.sum(dpre_i, axis=0, keepdims=True),
                    dwa + lax.dot_general(ub, pr, tn, preferred_element_type=F32),
                    dwx + lax.dot_general(ub, pi, tn, preferred_element_type=F32))

        dsp, dba, dbx, dwa, dwx = lax.fori_loop(0, t // rc, grads, (zrow, zrow, zrow, zmat, zmat))
        dba_ref[...] = dba
        dbx_ref[...] = dbx
        dwa_ref[...] = dwa
        dwx_ref[...] = dwx
        dlam_ref[...] = -dsp * jax.nn.sigmoid(-lam)

        def conv_back(i, carry):
            dcb, d0, d1, d2, d3 = carry
            r0 = pl.multiple_of(i * rc, rc)
            dwin = dupad[pl.ds(r0, rc + h8), :]
            du = dwin[:rc]
            du0 = (cw_ref[3:4, :] * du + cw_ref[2:3, :] * _up(dwin, 1)[:rc]
                   + cw_ref[1:2, :] * _up(dwin, 2)[:rc] + cw_ref[0:1, :] * _up(dwin, 3)[:rc])
            dproj_ref[0, pl.ds(r0, rc), :] = du0.astype(BF16)
            win = upad[pl.ds(r0, rc + h8), :]
            red = lambda v: jnp.sum(v, axis=0, keepdims=True)
            return (dcb + red(du), d0 + red(du * _down(win, 3)[h8:]), d1 + red(du * _down(win, 2)[h8:]),
                    d2 + red(du * _down(win, 1)[h8:]), d3 + red(du * win[h8:]))

        dcb, d0, d1, d2, d3 = lax.fori_loop(0, t // rc, conv_back, (zrow,) * 5)
        dcb_ref[...] = dcb
        dcw_ref[0:1, :] = d0
        dcw_ref[1:2, :] = d1
        dcw_ref[2:3, :] = d2
        dcw_ref[3:4, :] = d3

    blk = lambda i: (0, i)
    vec = jax.ShapeDtypeStruct((1, c), F32)
    mat = jax.ShapeDtypeStruct((heads, LRU_BW, LRU_BW), F32)
    full = lambda: pltpu.VMEM((t, LRU_BW), F32)
    padded = lambda: pltpu.VMEM((t + h8, LRU_BW), F32)
    return pl.pallas_call(
        body,
        name=name,
        grid=(heads,),
        in_specs=_lru_in_specs(t, heads)[:1] + [pl.BlockSpec((t, LRU_BW), blk)] + _lru_in_specs(t, heads)[1:],
        out_specs=[pl.BlockSpec((2, t, LRU_BW), lambda i: (0, 0, i)), pl.BlockSpec((CONV_WIDTH, LRU_BW), blk)]
        + [pl.BlockSpec((1, LRU_BW), blk)] * 4 + [pl.BlockSpec((None, LRU_BW, LRU_BW), lambda i: (i, 0, 0))] * 2,
        out_shape=[jax.ShapeDtypeStruct((2, t, c), BF16), jax.ShapeDtypeStruct((CONV_WIDTH, c), F32), vec, vec, vec, vec, mat, mat],
        scratch_shapes=[padded(), full(), full(), full(), padded(), padded(), full(), padded()] + [full()] * 4,
        compiler_params=_params("parallel"),
    )(proj, dhg, conv_w, conv_b, wa, ba, wx, bx, lam)


def _pick_level(g, levels):
    out = levels[-1]
    for k in range(len(levels) - 2, -1, -1):
        out = jnp.where(g == k, levels[k], out)
    return out


def _pool_z(win, g, row0, rc):
    levels, cur = [], win
    for k in range(len(POOL_WINDOWS)):
        cur = cur + _down(cur, 1 << k)
        levels.append(cur[POOL_HALO:])
    tot = _pick_level(g, levels)
    width = jnp.left_shift(2, g)
    row = row0 + lax.broadcasted_iota(jnp.int32, tot.shape, 0)
    cnt = jnp.minimum(row + 1, width).astype(F32)
    return tot / cnt - win[POOL_HALO:], cnt


def _pool_specs(t, gw):
    blk = lambda g: (0, g)
    return [pl.BlockSpec((t, gw), blk), pl.BlockSpec((None, gw, gw), lambda g: (g, 0, 0)),
            pl.BlockSpec((1, gw), blk), pl.BlockSpec((1, gw), blk)]


def pool_fwd(name, u, w_grp, b_grp, scale):
    t, d = u.shape
    gw = d // len(POOL_WINDOWS)
    rc = min(256, t)

    def body(u_ref, wg_ref, bg_ref, sc_ref, out_ref, upad):
        g = pl.program_id(0)
        upad[0:POOL_HALO, :] = jnp.zeros((POOL_HALO, gw), F32)
        upad[POOL_HALO:, :] = u_ref[...]
        wg, bg, sc = wg_ref[...], bg_ref[...], sc_ref[...]

        def chunk(i, carry):
            r0 = pl.multiple_of(i * rc, rc)
            z, _ = _pool_z(upad[pl.ds(r0, rc + POOL_HALO), :], g, r0, rc)
            z2 = jnp.dot(z.astype(BF16), wg, preferred_element_type=F32) + bg
            out_ref[pl.ds(r0, rc), :] = (z2 * sc).astype(BF16)
            return carry

        lax.fori_loop(0, t // rc, chunk, 0)

    return pl.pallas_call(
        body,
        name=name,
        grid=(len(POOL_WINDOWS),),
        in_specs=_pool_specs(t, gw),
        out_specs=pl.BlockSpec((t, gw), lambda g: (0, g)),
        out_shape=jax.ShapeDtypeStruct((t, d), BF16),
        scratch_shapes=[pltpu.VMEM((t + POOL_HALO, gw), F32)],
        compiler_params=_params("parallel"),
    )(u, w_grp, b_grp, scale)


def pool_bwd(name, u, dzs, w_grp, b_grp, scale):
    t, d = u.shape
    gw = d // len(POOL_WINDOWS)
    rc = min(256, t)

    def body(u_ref, dzs_ref, wg_ref, bg_ref, sc_ref, du_ref, dwg_ref, dbg_ref, dsc_ref, upad, qpad, dz_s):
        g = pl.program_id(0)
        upad[0:POOL_HALO, :] = jnp.zeros((POOL_HALO, gw), F32)
        upad[POOL_HALO:, :] = u_ref[...]
        qpad[t:, :] = jnp.zeros((POOL_HALO, gw), F32)
        wg, bg, sc = wg_ref[...], bg_ref[...], sc_ref[...]
        zrow = jnp.zeros((1, gw), F32)

        def chunk(i, carry):
            dsc, dbg, dwg = carry
            r0 = pl.multiple_of(i * rc, rc)
            z, cnt = _pool_z(upad[pl.ds(r0, rc + POOL_HALO), :], g, r0, rc)
            zb = z.astype(BF16)
            z2 = jnp.dot(zb, wg, preferred_element_type=F32) + bg
            dzs = dzs_ref[pl.ds(r0, rc), :]
            dz2 = dzs * sc
            d2b = dz2.astype(BF16)
            dz = lax.dot_general(d2b, wg, (((1,), (1,)), ((), ())), preferred_element_type=F32)
            dz_s[pl.ds(r0, rc), :] = dz
            qpad[pl.ds(r0, rc), :] = dz / cnt
            return (dsc + jnp.sum(dzs * z2, axis=0, keepdims=True), dbg + jnp.sum(dz2, axis=0, keepdims=True),
                    dwg + lax.dot_general(zb, d2b, (((0,), (0,)), ((), ())), preferred_element_type=F32))

        dsc, dbg, dwg = lax.fori_loop(0, t // rc, chunk, (zrow, zrow, jnp.zeros((gw, gw), F32)))
        dsc_ref[...] = dsc
        dbg_ref[...] = dbg
        dwg_ref[...] = dwg

        def spread(i, carry):
            r0 = pl.multiple_of(i * rc, rc)
            levels, cur = [], qpad[pl.ds(r0, rc + POOL_HALO), :]
            for k in range(len(POOL_WINDOWS)):
                cur = cur + _up(cur, 1 << k)
                levels.append(cur[:rc])
            du_ref[pl.ds(r0, rc), :] = (_pick_level(g, levels) - dz_s[pl.ds(r0, rc), :]).astype(BF16)
            return carry

        lax.fori_loop(0, t // rc, spread, 0)

    blk = lambda g: (0, g)
    vec = jax.ShapeDtypeStruct((1, d), F32)
    return pl.pallas_call(
        body,
        name=name,
        grid=(len(POOL_WINDOWS),),
        in_specs=_pool_specs(t, gw)[:1] + [pl.BlockSpec((t, gw), blk)] + _pool_specs(t, gw)[1:],
        out_specs=[pl.BlockSpec((t, gw), blk), pl.BlockSpec((None, gw, gw), lambda g: (g, 0, 0)),
                   pl.BlockSpec((1, gw), blk), pl.BlockSpec((1, gw), blk)],
        out_shape=[jax.ShapeDtypeStruct((t, d), BF16), jax.ShapeDtypeStruct((len(POOL_WINDOWS), gw, gw), F32), vec, vec],
        scratch_shapes=[pltpu.VMEM((t + POOL_HALO, gw), F32), pltpu.VMEM((t + POOL_HALO, gw), F32), pltpu.VMEM((t, gw), F32)],
        compiler_params=_params("parallel"),
    )(u, dzs, w_grp, b_grp, scale)


def _place():
    return lax.axis_index("x"), lax.axis_index("y"), lax.axis_index("c")


def _other_chips(x, y):
    return [(1 - x, y), (x, 1 - y), (1 - x, 1 - y)]


def _half(c, rows):
    h = rows // 2
    return pl.ds(pl.multiple_of(c * h, 8), h)


_ANY = pl.BlockSpec(memory_space=pl.ANY)


def into_block(name, shard, me, dtype):
    r, c = shard.shape
    tr = _tile(r, 512, 16)

    def body(me_ref, s_ref, o_ref):
        o_ref[...] = s_ref[...].astype(o_ref.dtype)

    return pl.pallas_call(
        body,
        name=name,
        grid_spec=pltpu.PrefetchScalarGridSpec(
            num_scalar_prefetch=1,
            grid=(r // tr,),
            in_specs=[pl.BlockSpec((tr, c), lambda i, me_ref: (i, 0))],
            out_specs=pl.BlockSpec((None, tr, c), lambda i, me_ref: (me_ref[0], i, 0)),
        ),
        out_shape=jax.ShapeDtypeStruct((N_CHIPS, r, c), dtype),
        compiler_params=_params("parallel"),
    )(me, shard)


def all_gather_chips(name, bufs):
    n = len(bufs)

    def body(*refs):
        outs = refs[n:2 * n]
        send_sems, recv_sems = refs[2 * n:]
        x, y, c = _place()
        me, sibling = 2 * x + y, (x, y, 1 - c)
        chips = _other_chips(x, y)

        def copy(i, slot, block, half, to):
            blk = outs[i].at[block, _half(half, outs[i].shape[1]), :]
            return pltpu.make_async_remote_copy(
                src_ref=blk, dst_ref=blk, send_sem=send_sems.at[i * 6 + slot], recv_sem=recv_sems.at[i * 6 + slot],
                device_id=to, device_id_type=MESH)

        first = [copy(i, j, me, c, (*chip, c)) for i in range(n) for j, chip in enumerate(chips)]
        for cp in first:
            cp.start()
        passed = []
        for i in range(n):
            for j, (cx, cy) in enumerate(chips):
                copy(i, j, 2 * cx + cy, c, (x, y, c)).wait_recv()
                fwd = copy(i, 3 + j, 2 * cx + cy, c, sibling)
                fwd.start()
                passed.append(fwd)
        for i in range(n):
            for j, (cx, cy) in enumerate(chips):
                copy(i, 3 + j, 2 * cx + cy, 1 - c, (x, y, c)).wait_recv()
        for cp in first + passed:
            cp.wait_send()

    return pl.pallas_call(
        body,
        name=name,
        in_specs=[_ANY] * n,
        out_specs=[_ANY] * n,
        out_shape=[jax.ShapeDtypeStruct(b.shape, b.dtype) for b in bufs],
        input_output_aliases={i: i for i in range(n)},
        scratch_shapes=[pltpu.SemaphoreType.DMA((6 * n,)), pltpu.SemaphoreType.DMA((6 * n,))],
    )(*bufs)


def pair_exchange(name, grads):
    n = len(grads)

    def body(*refs):
        ins, outs = refs[:n], refs[n:2 * n]
        send_sems, recv_sems = refs[2 * n:]
        x, y, c = _place()
        copies = [pltpu.make_async_remote_copy(
            src_ref=ins[i].at[:, _half(1 - c, ins[i].shape[1]), :], dst_ref=outs[i], send_sem=send_sems.at[i],
            recv_sem=recv_sems.at[i], device_id=(x, y, 1 - c), device_id_type=MESH) for i in range(n)]
        for cp in copies:
            cp.start()
        for cp in copies:
            cp.wait()

    return pl.pallas_call(
        body,
        name=name,
        in_specs=[_ANY] * n,
        out_specs=[_ANY] * n,
        out_shape=[jax.ShapeDtypeStruct((g.shape[0], g.shape[1] // 2, g.shape[2]), g.dtype) for g in grads],
        scratch_shapes=[pltpu.SemaphoreType.DMA((n,)), pltpu.SemaphoreType.DMA((n,))],
    )(*grads)


def chip_exchange(name, parts):
    n = len(parts)

    def body(*refs):
        ins, outs = refs[:n], refs[n:2 * n]
        send_sems, recv_sems = refs[2 * n:]
        x, y, c = _place()
        me = 2 * x + y
        chips = _other_chips(x, y)
        copies = []
        for i in range(n):
            for j, (cx, cy) in enumerate(chips):
                copies.append(pltpu.make_async_remote_copy(
                    src_ref=ins[i].at[2 * cx + cy], dst_ref=outs[i].at[me], send_sem=send_sems.at[3 * i + j],
                    recv_sem=recv_sems.at[3 * i + j], device_id=(cx, cy, c), device_id_type=MESH))
        for cp in copies:
            cp.start()
        for cp in copies:
            cp.wait()

    return pl.pallas_call(
        body,
        name=name,
        in_specs=[_ANY] * n,
        out_specs=[_ANY] * n,
        out_shape=[jax.ShapeDtypeStruct(p.shape, p.dtype) for p in parts],
        scratch_shapes=[pltpu.SemaphoreType.DMA((3 * n,)), pltpu.SemaphoreType.DMA((3 * n,))],
    )(*parts)


def pair_gather(name, bufs, blocked):
    n = len(bufs)

    def body(*refs):
        outs = refs[n:2 * n]
        send_sems, recv_sems = refs[2 * n:]
        x, y, c = _place()
        copies = []
        for i in range(n):
            buf = outs[i].at[2 * x + y] if blocked[i] else outs[i]
            mine = buf.at[_half(c, buf.shape[0]), :]
            copies.append(pltpu.make_async_remote_copy(
                src_ref=mine, dst_ref=mine, send_sem=send_sems.at[i], recv_sem=recv_sems.at[i],
                device_id=(x, y, 1 - c), device_id_type=MESH))
        for cp in copies:
            cp.start()
        for cp in copies:
            cp.wait()

    return pl.pallas_call(
        body,
        name=name,
        in_specs=[_ANY] * n,
        out_specs=[_ANY] * n,
        out_shape=[jax.ShapeDtypeStruct(b.shape, b.dtype) for b in bufs],
        input_output_aliases={i: i for i in range(n)},
        scratch_shapes=[pltpu.SemaphoreType.DMA((n,)), pltpu.SemaphoreType.DMA((n,))],
    )(*bufs)


def pair_sum(name, grad, recv, core, dtype):
    _, r, c = grad.shape
    h = r // 2
    th = _tile(h, 512, 16)
    per = h // th

    def body(core_ref, g_ref, r_ref, o_ref):
        o_ref[...] = (g_ref[...] + r_ref[...]).astype(o_ref.dtype)

    return pl.pallas_call(
        body,
        name=name,
        grid_spec=pltpu.PrefetchScalarGridSpec(
            num_scalar_prefetch=1,
            grid=(N_CHIPS, per),
            in_specs=[pl.BlockSpec((None, th, c), lambda k, i, core_ref: (k, core_ref[0] * per + i, 0)),
                      pl.BlockSpec((None, th, c), lambda k, i, core_ref: (k, i, 0))],
            out_specs=pl.BlockSpec((None, th, c), lambda k, i, core_ref: (k, i, 0)),
        ),
        out_shape=jax.ShapeDtypeStruct((N_CHIPS, h, c), dtype),
        compiler_params=_params("parallel", "parallel"),
    )(core, grad, recv)


def chip_sum(name, got, parts, place, blocked):
    _, h, c = parts.shape
    th = _tile(h, 256, 16)
    per = h // th

    def body(place_ref, q0, q1, q2, q3, p_ref, o_ref):
        me = place_ref[0]
        own = p_ref[...].astype(F32)
        v = [jnp.where(me == k, own, q[...].astype(F32)) for k, q in enumerate((q0, q1, q2, q3))]
        o_ref[...] = ((v[0] + v[1]) + v[2]) + v[3]

    def got_spec(k):
        return pl.BlockSpec((None, th, c), lambda i, pr: (jnp.where(pr[0] == k, (k + 1) % N_CHIPS, k), i, 0))

    if blocked:
        out_spec = pl.BlockSpec((None, th, c), lambda i, pr: (pr[0], pr[1] * per + i, 0))
        out_shape = jax.ShapeDtypeStruct((N_CHIPS, 2 * h, c), F32)
    else:
        out_spec = pl.BlockSpec((th, c), lambda i, pr: (pr[1] * per + i, 0))
        out_shape = jax.ShapeDtypeStruct((2 * h, c), F32)
    return pl.pallas_call(
        body,
        name=name,
        grid_spec=pltpu.PrefetchScalarGridSpec(
            num_scalar_prefetch=1,
            grid=(per,),
            in_specs=[got_spec(k) for k in range(N_CHIPS)] + [pl.BlockSpec((None, th, c), lambda i, pr: (pr[0], i, 0))],
            out_specs=out_spec,
        ),
        out_shape=out_shape,
        compiler_params=_params("parallel"),
    )(place, got, got, got, got, parts)


def adamw(name, w, g, m, v):
    r, c = w.shape
    tr = _tile(r, 512, 8)
    c1 = 1.0 - ADAM_B1 ** ADAM_STEP
    c2 = 1.0 - ADAM_B2 ** ADAM_STEP

    def body(w_ref, g_ref, m_ref, v_ref, d_ref, nm_ref, nv_ref):
        gv = g_ref[...]
        nm = ADAM_B1 * m_ref[...] + (1.0 - ADAM_B1) * gv
        nv = ADAM_B2 * v_ref[...] + (1.0 - ADAM_B2) * (gv * gv)
        d_ref[...] = -ADAM_LR * ((nm / c1) / (jnp.sqrt(nv / c2) + ADAM_EPS) + ADAM_WD * w_ref[...])
        nm_ref[...] = nm
        nv_ref[...] = nv

    spec = pl.BlockSpec((tr, c), lambda i: (i, 0))
    return pl.pallas_call(
        body,
        name=name,
        grid=(r // tr,),
        in_specs=[spec] * 4,
        out_specs=[spec] * 3,
        out_shape=[jax.ShapeDtypeStruct((r, c), F32)] * 3,
        compiler_params=_params("parallel"),
    )(w, g, m, v)


def reduce_to_shards(tag, grads, wire, blocked, place):
    recv = pair_exchange(tag + "_pair_exchange", grads)
    parts = [pair_sum(f"{tag}_pair_sum_{i}", g, r, place[1:], w) for i, (g, r, w) in enumerate(zip(grads, recv, wire))]
    got = chip_exchange(tag + "_chip_exchange", parts)
    sums = [chip_sum(f"{tag}_chip_sum_{i}", q, p, place, b) for i, (q, p, b) in enumerate(zip(got, parts, blocked))]
    return pair_gather(tag + "_pair_gather", sums, blocked)


def _pack(arrays, row_multiple, cols=BLOB_COLS):
    flat = jnp.concatenate([a.reshape(-1).astype(F32) for a in arrays])
    rows = -(-flat.shape[0] // cols)
    rows = -(-rows // row_multiple) * row_multiple
    return jnp.pad(flat, (0, rows * cols - flat.shape[0])).reshape(rows, cols)


def _unpack(blob, shapes):
    flat, out, off = blob.reshape(-1), [], 0
    for s in shapes:
        size = math.prod(s)
        out.append(flat[off:off + size].reshape(s))
        off += size
    return out


def _unpack_rows(blobs, shapes):
    out, off = [], 0
    for s in shapes:
        size = math.prod(s)
        out.append(blobs[:, off:off + size].reshape((blobs.shape[0],) + tuple(s)))
        off += size
    return out


def kernel(x, p, lru_w_in, lru_conv_w, lru_conv_b, lru_wa, lru_ba, lru_wx, lru_bx, lru_lambda, lru_w_out, pool_w_in, pool_w_grp, pool_b_grp, pool_scale, pool_w_out, ln_mix_g, ln_mix_b, mlp_w1, mlp_w2, ln_mlp_g, ln_mlp_b, ple_w, ple_gate_w, ple_gate_b, loss_target, m_lru_w_in, m_lru_conv_w, m_lru_conv_b, m_lru_wa, m_lru_ba, m_lru_wx, m_lru_bx, m_lru_lambda, m_lru_w_out, m_pool_w_in, m_pool_w_grp, m_pool_b_grp, m_pool_scale, m_pool_w_out, m_ln_mix_g, m_ln_mix_b, m_mlp_w1, m_mlp_w2, m_ln_mlp_g, m_ln_mlp_b, m_ple_w, m_ple_gate_w, m_ple_gate_b, v_lru_w_in, v_lru_conv_w, v_lru_conv_b, v_lru_wa, v_lru_ba, v_lru_wx, v_lru_bx, v_lru_lambda, v_lru_w_out, v_pool_w_in, v_pool_w_grp, v_pool_b_grp, v_pool_scale, v_pool_w_out, v_ln_mix_g, v_ln_mix_b, v_mlp_w1, v_mlp_w2, v_ln_mlp_g, v_ln_mlp_b, v_ple_w, v_ple_gate_w, v_ple_gate_b):
    weights = dict(lru_w_in=lru_w_in, lru_conv_w=lru_conv_w, lru_conv_b=lru_conv_b, lru_wa=lru_wa, lru_ba=lru_ba, lru_wx=lru_wx, lru_bx=lru_bx, lru_lambda=lru_lambda, lru_w_out=lru_w_out, pool_w_in=pool_w_in, pool_w_grp=pool_w_grp, pool_b_grp=pool_b_grp, pool_scale=pool_scale, pool_w_out=pool_w_out, ln_mix_g=ln_mix_g, ln_mix_b=ln_mix_b, mlp_w1=mlp_w1, mlp_w2=mlp_w2, ln_mlp_g=ln_mlp_g, ln_mlp_b=ln_mlp_b, ple_w=ple_w, ple_gate_w=ple_gate_w, ple_gate_b=ple_gate_b)
    mom_m = dict(lru_w_in=m_lru_w_in, lru_conv_w=m_lru_conv_w, lru_conv_b=m_lru_conv_b, lru_wa=m_lru_wa, lru_ba=m_lru_ba, lru_wx=m_lru_wx, lru_bx=m_lru_bx, lru_lambda=m_lru_lambda, lru_w_out=m_lru_w_out, pool_w_in=m_pool_w_in, pool_w_grp=m_pool_w_grp, pool_b_grp=m_pool_b_grp, pool_scale=m_pool_scale, pool_w_out=m_pool_w_out, ln_mix_g=m_ln_mix_g, ln_mix_b=m_ln_mix_b, mlp_w1=m_mlp_w1, mlp_w2=m_mlp_w2, ln_mlp_g=m_ln_mlp_g, ln_mlp_b=m_ln_mlp_b, ple_w=m_ple_w, ple_gate_w=m_ple_gate_w, ple_gate_b=m_ple_gate_b)
    mom_v = dict(lru_w_in=v_lru_w_in, lru_conv_w=v_lru_conv_w, lru_conv_b=v_lru_conv_b, lru_wa=v_lru_wa, lru_ba=v_lru_ba, lru_wx=v_lru_wx, lru_bx=v_lru_bx, lru_lambda=v_lru_lambda, lru_w_out=v_lru_w_out, pool_w_in=v_pool_w_in, pool_w_grp=v_pool_w_grp, pool_b_grp=v_pool_b_grp, pool_scale=v_pool_scale, pool_w_out=v_pool_w_out, ln_mix_g=v_ln_mix_g, ln_mix_b=v_ln_mix_b, mlp_w1=v_mlp_w1, mlp_w2=v_mlp_w2, ln_mlp_g=v_ln_mlp_g, ln_mlp_b=v_ln_mlp_b, ple_w=v_ple_w, ple_gate_w=v_ple_gate_w, ple_gate_b=v_ple_gate_b)
    names = list(weights)

    depth, d = ln_mix_g.shape
    t = x.shape[1]
    n_a, n_b = lru_w_in.shape[0], pool_w_in.shape[0]
    d_rnn = lru_w_out.shape[1] * N_CHIPS
    heads = d_rnn // LRU_BW
    d_ff = mlp_w1.shape[2] * N_CHIPS
    ple_dim = ple_w.shape[1]
    n_grp = len(POOL_WINDOWS)
    gw = d // n_grp
    alpha = (2 * depth) ** 0.25
    chip = 2 * lax.axis_index("x") + lax.axis_index("y")
    place = jnp.stack([chip, lax.axis_index("c")]).astype(jnp.int32)

    x2d = x.reshape(t, d)
    target = loss_target.reshape(t, d)
    p3 = p.reshape(depth, t, ple_dim)

    big = ["lru_w_in", "lru_w_out", "pool_w_in", "pool_w_out", "mlp_w1", "mlp_w2", "ple_w", "ple_gate_w", "pool_w_grp"]
    flat2 = lambda a: a.reshape(-1, a.shape[-1])
    small_sharded = ["lru_conv_w", "pool_b_grp", "pool_scale"]
    small_blob = _pack([weights[k] for k in small_sharded], 16, cols=256)
    gathered = all_gather_chips(
        "gather_weights", [into_block("stage_" + k, flat2(weights[k]), place[:1], BF16) for k in big]
        + [into_block("stage_small", small_blob, place[:1], F32)])
    wg = dict(zip(big, gathered[:-1]))
    conv_w_sh, b_grp_sh, scale_sh = _unpack_rows(gathered[-1].reshape(N_CHIPS, -1), [weights[k].shape for k in small_sharded])
    conv_w_full = jnp.moveaxis(conv_w_sh, 0, 2).reshape(n_a, CONV_WIDTH, d_rnn)
    b_grp_full = jnp.moveaxis(b_grp_sh, 0, 1).reshape(n_b, 1, d)
    scale_full = jnp.moveaxis(scale_sh, 0, 1).reshape(n_b, 1, d)
    rows_grp = gw // N_CHIPS
    w_grp_full = jnp.moveaxis(wg["pool_w_grp"].reshape(N_CHIPS, n_b, n_grp, rows_grp, gw), 0, 2).reshape(n_b, n_grp, gw, gw)
    wa_bf, wx_bf = lru_wa.astype(BF16), lru_wx.astype(BF16)
    row = lambda a, i: a[i].reshape(1, -1)

    saved = []
    cur, cur_bf = x2d, x2d.astype(BF16)
    for i in range(depth):
        slot = i // 2
        sv = dict(x_bf=cur_bf)
        if i % 2 == 0:
            (proj,) = matmul(f"l{i}_lru_in", plain(cur_bf), colsplit(wg["lru_w_in"], slot, d), "nn",
                             [colsplit(None, 0, t, n=2, full=(2, t, d_rnn), dtype=F32)])
            hg = lru_fwd(f"l{i}_lru", proj, conv_w_full[slot], row(lru_conv_b, slot), wa_bf[slot], row(lru_ba, slot),
                         wx_bf[slot], row(lru_bx, slot), row(lru_lambda, slot))
            (mix,) = matmul(f"l{i}_lru_out", plain(hg), rowsplit(wg["lru_w_out"], slot, d_rnn // N_CHIPS), "nn",
                            [plain(shape=(t, d), dtype=F32)])
            sv.update(proj=proj, act=hg)
        else:
            (u,) = matmul(f"l{i}_pool_in", plain(cur_bf), rowsplit(wg["pool_w_in"], slot, d // N_CHIPS), "nn",
                          [plain(shape=(t, d), dtype=F32)])
            zs = pool_fwd(f"l{i}_pool", u, w_grp_full[slot], b_grp_full[slot], scale_full[slot])
            (mix,) = matmul(f"l{i}_pool_out", plain(zs), rowsplit(wg["pool_w_out"], slot, d // N_CHIPS), "nn",
                            [plain(shape=(t, d), dtype=F32)])
            sv.update(u=u, act=zs)
        x1, x1_bf, s1 = ln_fwd(f"l{i}_ln_mix", alpha, cur, mix, row(ln_mix_g, i), row(ln_mix_b, i))

        def relu2(acc):
            hr = jnp.maximum(acc, 0.0)
            return hr, hr * hr

        hr, hh = matmul(f"l{i}_mlp_up", plain(x1_bf), colsplit(wg["mlp_w1"], i, d), "nn",
                        [plain(shape=(t, d_ff), dtype=F32), plain(shape=(t, d_ff), dtype=BF16)], epilogue=relu2)
        (mlp,) = matmul(f"l{i}_mlp_down", plain(hh), rowsplit(wg["mlp_w2"], i, d_ff // N_CHIPS), "nn",
                        [plain(shape=(t, d), dtype=F32)])
        x2, x2_bf, s2 = ln_fwd(f"l{i}_ln_mlp", alpha, x1, mlp, row(ln_mlp_g, i), row(ln_mlp_b, i))
        (e,) = matmul(f"l{i}_ple", plain(p3[i]), colsplit(wg["ple_w"], i, ple_dim), "nn", [plain(shape=(t, d), dtype=F32)])

        def ple_out(acc, x2_t, e_t, gb):
            gate = jax.nn.sigmoid(acc + gb)
            x3 = x2_t + e_t * gate
            return x3, x3, gate

        cur, cur_bf, gate = matmul(f"l{i}_ple_gate", plain(x2_bf), rowsplit(wg["ple_gate_w"], i, d // N_CHIPS), "nn",
                                   [plain(shape=(t, d), dtype=F32), plain(shape=(t, d), dtype=BF16), plain(shape=(t, d), dtype=F32)],
                                   epilogue=ple_out, tiles=[plain(x2), plain(e)], rows=[row(ple_gate_b, i)])
        sv.update(s1=s1, x1_bf=x1_bf, hr=hr, hh=hh, s2=s2, x2_bf=x2_bf, gate=gate, e=e)
        saved.append(sv)

    dy, loss_part = loss_head("loss", cur, target)
    loss = lax.psum(loss_part.reshape(()), ("x", "y", "c"))

    part = {}

    def grad_view(key, layer, split):
        w = weights[key]
        full = (N_CHIPS, w.shape[0] * w.shape[1], w.shape[2])
        return split(part.get(key), layer, w.shape[1], full=full, dtype=F32)

    small = {k: [None] * weights[k].shape[0] for k in names if k not in big or k == "pool_w_grp"}
    dcur = dy
    for i in reversed(range(depth)):
        slot = i // 2
        sv = saved[i]
        de, dpre, dgb = ple_bwd(f"l{i}_ple_bwd", dcur, sv["gate"], sv["e"])
        small["ple_gate_b"][i] = dgb
        (part["ple_w"],) = matmul(f"l{i}_d_ple_w", plain(p3[i]), plain(de), "tn", [grad_view("ple_w", i, colsplit)])
        (part["ple_gate_w"],) = matmul(f"l{i}_d_ple_gate_w", plain(sv["x2_bf"]), plain(dpre), "tn",
                                       [grad_view("ple_gate_w", i, rowsplit)])
        (dx2b,) = matmul(f"l{i}_dx2", plain(dpre), rowsplit(wg["ple_gate_w"], i, d // N_CHIPS), "nt",
                         [plain(shape=(t, d), dtype=F32)])
        ds2, ds2_bf, dg, db = ln_bwd(f"l{i}_ln_mlp_bwd", 1.0, dcur, dx2b, sv["s2"], row(ln_mlp_g, i))
        small["ln_mlp_g"][i], small["ln_mlp_b"][i] = dg, db
        (part["mlp_w2"],) = matmul(f"l{i}_d_mlp_w2", plain(sv["hh"]), plain(ds2_bf), "tn", [grad_view("mlp_w2", i, rowsplit)])
        (dhpre,) = matmul(f"l{i}_dh", plain(ds2_bf), rowsplit(wg["mlp_w2"], i, d_ff // N_CHIPS), "nt",
                          [plain(shape=(t, d_ff), dtype=BF16)], epilogue=lambda acc, hr_t: (acc * (2.0 * hr_t),),
                          tiles=[plain(sv["hr"])])
        (part["mlp_w1"],) = matmul(f"l{i}_d_mlp_w1", plain(sv["x1_bf"]), plain(dhpre), "tn", [grad_view("mlp_w1", i, colsplit)])
        (dx1b,) = matmul(f"l{i}_dx1", plain(dhpre), colsplit(wg["mlp_w1"], i, d), "nt", [plain(shape=(t, d), dtype=F32)])
        ds1, ds1_bf, dg, db = ln_bwd(f"l{i}_ln_mix_bwd", alpha, ds2, dx1b, sv["s1"], row(ln_mix_g, i))
        small["ln_mix_g"][i], small["ln_mix_b"][i] = dg, db
        residual = lambda acc, ds_t: (alpha * ds_t + acc,)
        if i % 2 == 0:
            (part["lru_w_out"],) = matmul(f"l{i}_d_lru_out", plain(sv["act"]), plain(ds1_bf), "tn",
                                          [grad_view("lru_w_out", slot, rowsplit)])
            (dhg,) = matmul(f"l{i}_dhg", plain(ds1_bf), rowsplit(wg["lru_w_out"], slot, d_rnn // N_CHIPS), "nt",
                            [plain(shape=(t, d_rnn), dtype=F32)])
            dproj, dcw, dcb, dba, dbx, dlam, dwa, dwx = lru_bwd(
                f"l{i}_lru_bwd", sv["proj"], dhg, conv_w_full[slot], row(lru_conv_b, slot), wa_bf[slot], row(lru_ba, slot),
                wx_bf[slot], row(lru_bx, slot), row(lru_lambda, slot))
            for key, val in (("lru_conv_w", dcw), ("lru_conv_b", dcb), ("lru_ba", dba), ("lru_bx", dbx),
                             ("lru_lambda", dlam), ("lru_wa", dwa), ("lru_wx", dwx)):
                small[key][slot] = val
            dproj_v = colsplit(dproj, 0, t, n=2)
            (part["lru_w_in"],) = matmul(f"l{i}_d_lru_in", plain(sv["x_bf"]), dproj_v, "tn", [grad_view("lru_w_in", slot, colsplit)])
            (dcur,) = matmul(f"l{i}_dx", dproj_v, colsplit(wg["lru_w_in"], slot, d), "nt",
                             [plain(shape=(t, d), dtype=F32)], epilogue=residual, tiles=[plain(ds1)])
        else:
            (part["pool_w_out"],) = matmul(f"l{i}_d_pool_out", plain(sv["act"]), plain(ds1_bf), "tn",
                                           [grad_view("pool_w_out", slot, rowsplit)])
            (dzs,) = matmul(f"l{i}_dzs", plain(ds1_bf), rowsplit(wg["pool_w_out"], slot, d // N_CHIPS), "nt",
                            [plain(shape=(t, d), dtype=F32)])
            du, dwg, dbg, dsc = pool_bwd(f"l{i}_pool_bwd", sv["u"], dzs, w_grp_full[slot], b_grp_full[slot], scale_full[slot])
            small["pool_w_grp"][slot], small["pool_b_grp"][slot], small["pool_scale"][slot] = dwg, dbg, dsc
            (part["pool_w_in"],) = matmul(f"l{i}_d_pool_in", plain(sv["x_bf"]), plain(du), "tn", [grad_view("pool_w_in", slot, rowsplit)])
            (dcur,) = matmul(f"l{i}_dx", plain(du), rowsplit(wg["pool_w_in"], slot, d // N_CHIPS), "nt",
                             [plain(shape=(t, d), dtype=F32)], epilogue=residual, tiles=[plain(ds1)])
    grad_x = dcur.reshape(x.shape)

    big_w = [k for k in big if k != "pool_w_grp"]
    small_keys = [k for k in names if k not in big_w]
    small_full = [jnp.stack(small[k]).reshape((weights[k].shape[0],) + tuple(
        s * (N_CHIPS if ax in _sharded_axis(k) else 1) for ax, s in enumerate(weights[k].shape[1:], 1))) for k in small_keys]
    full_shapes = [a.shape for a in small_full]
    blob = _pack(small_full, 64)
    blob4 = blob.reshape(N_CHIPS, blob.shape[0] // N_CHIPS, BLOB_COLS)
    reduced = reduce_to_shards("grads", [part[k] for k in big_w] + [blob4], [BF16] * len(big_w) + [F32],
                               [False] * len(big_w) + [True], place)
    (blob_all,) = all_gather_chips("gather_small_grads", [reduced[-1]])
    small_grads = dict(zip(small_keys, _unpack(blob_all.reshape(blob.shape), full_shapes)))
    for k in small_keys:
        for ax in _sharded_axis(k):
            n = weights[k].shape[ax]
            small_grads[k] = lax.dynamic_slice_in_dim(small_grads[k], chip * n, n, axis=ax)
    grads = {k: reduced[j].reshape(weights[k].shape) for j, k in enumerate(big_w)}
    grads.update(small_grads)

    delta, new_m, new_v = {}, {}, {}
    for k in big_w:
        dl, nm, nv = adamw("adamw_" + k, flat2(weights[k]), flat2(grads[k]), flat2(mom_m[k]), flat2(mom_v[k]))
        delta[k], new_m[k], new_v[k] = (a.reshape(weights[k].shape) for a in (dl, nm, nv))
    shapes = [weights[k].shape for k in small_keys]
    dl, nm, nv = adamw("adamw_small", *[_pack([src[k] for k in small_keys], 8) for src in (weights, grads, mom_m, mom_v)])
    for out, blob_out in ((delta, dl), (new_m, nm), (new_v, nv)):
        out.update(zip(small_keys, _unpack(blob_out, shapes)))

    return (loss, grad_x, *[grads[k] for k in names], *[delta[k] for k in names],
            *[new_m[k] for k in names], *[new_v[k] for k in names])


def _sharded_axis(key):
    return {"lru_conv_w": (2,), "pool_w_grp": (2,), "pool_b_grp": (1,), "pool_scale": (1,)}.get(key, ())
```

```python
import functools
import math

import jax
import jax.numpy as jnp
from jax import lax
from jax.experimental import pallas as pl
from jax.experimental.pallas import tpu as pltpu

F32 = jnp.float32
BF16 = jnp.bfloat16

N_CHIPS = 4
LRU_BW = 128
LRU_C = 8.0
CONV_WIDTH = 4
POOL_WINDOWS = (2, 4, 8, 16)
POOL_HALO = 16
CONV_HALO = 8
LN_EPS = 1e-5
ADAM_LR = 0.001
ADAM_B1 = 0.9
ADAM_B2 = 0.999
ADAM_EPS = 1e-08
ADAM_WD = 0.01
ADAM_STEP = 10
GELU_C = math.sqrt(2.0 / math.pi)
GELU_K = 0.044715
VMEM_LIMIT_BYTES = 56 * 1024 * 1024
MESH = pl.DeviceIdType.MESH
BLOB_COLS = 1024


def _params(*sem):
    return pltpu.CompilerParams(dimension_semantics=tuple(sem), vmem_limit_bytes=VMEM_LIMIT_BYTES)


def _tile(unit, pref, align=128):
    if unit <= pref:
        return unit
    for d in range(2, unit + 1):
        if unit % d == 0 and unit // d <= pref and (unit // d) % align == 0:
            return unit // d
    raise ValueError((unit, pref, align))


class View:
    def __init__(self, arr, shape, row_unit, col_unit, block_fn, full=None, dtype=None):
        self.arr, self.shape, self.row_unit, self.col_unit, self.block_fn = arr, shape, row_unit, col_unit, block_fn
        self.full = full if full is not None else arr.shape
        self.dtype = dtype if dtype is not None else arr.dtype

    def spec(self, tr, tc, f):
        block, idx = self.block_fn(tr, tc)
        return pl.BlockSpec(block, lambda *g: idx(*f(*g)))


def plain(arr=None, shape=None, dtype=None):
    shape = arr.shape if arr is not None else shape
    return View(arr, shape, shape[0], shape[1], lambda tr, tc: ((tr, tc), lambda rt, ct: (rt, ct)), full=shape, dtype=dtype)


def colsplit(arr, layer, rows, n=N_CHIPS, full=None, dtype=None):
    full = arr.shape if arr is not None else full
    c = full[2]

    def block_fn(tr, tc):
        assert rows % tr == 0 and c % tc == 0, (rows, tr, c, tc)
        per, rpl = c // tc, rows // tr
        return (None, tr, tc), lambda rt, ct: (ct // per, layer * rpl + rt, ct % per)

    return View(arr, (rows, n * c), rows, c, block_fn, full=full, dtype=dtype)


def rowsplit(arr, layer, rows, n=N_CHIPS, full=None, dtype=None):
    full = arr.shape if arr is not None else full
    c = full[2]

    def block_fn(tr, tc):
        assert rows % tr == 0 and c % tc == 0, (rows, tr, c, tc)
        per = rows // tr
        return (None, tr, tc), lambda rt, ct: (rt // per, layer * per + rt % per, ct)

    return View(arr, (n * rows, c), rows, c, block_fn, full=full, dtype=dtype)


def matmul(name, a, b, mode, outs, epilogue=None, tiles=(), rows=(), deps=(), pm=1024, pn=1024, pk=1024):
    if mode == "nn":
        (m, k), (k2, n) = a.shape, b.shape
        um, uk, un = a.row_unit, min(a.col_unit, b.row_unit), b.col_unit
        dims = (((1,), (0,)), ((), ()))
    elif mode == "nt":
        (m, k), (n, k2) = a.shape, b.shape
        um, uk, un = a.row_unit, min(a.col_unit, b.col_unit), b.row_unit
        dims = (((1,), (1,)), ((), ()))
    else:
        (k, m), (k2, n) = a.shape, b.shape
        um, uk, un = a.col_unit, min(a.row_unit, b.row_unit), b.col_unit
        dims = (((0,), (0,)), ((), ()))
    assert k == k2, (name, a.shape, b.shape)
    for o in list(outs) + list(tiles):
        assert o.shape == (m, n), (name, o.shape, m, n)
        um, un = min(um, o.row_unit), min(un, o.col_unit)
    tm, tn, tk = _tile(um, pm), _tile(un, pn), _tile(uk, pk)
    assert m % tm == 0 and n % tn == 0 and k % tk == 0, (name, m, n, k, tm, tn, tk)
    gm, gn, gk = m // tm, n // tn, k // tk

    if mode == "nn":
        a_spec = a.spec(tm, tk, lambda i, j, kk: (i, kk))
        b_spec = b.spec(tk, tn, lambda i, j, kk: (kk, j))
    elif mode == "nt":
        a_spec = a.spec(tm, tk, lambda i, j, kk: (i, kk))
        b_spec = b.spec(tn, tk, lambda i, j, kk: (j, kk))
    else:
        a_spec = a.spec(tk, tm, lambda i, j, kk: (kk, i))
        b_spec = b.spec(tk, tn, lambda i, j, kk: (kk, j))
    tile_specs = [t.spec(tm, tn, lambda i, j, kk: (i, j)) for t in tiles]
    row_specs = [pl.BlockSpec((1, tn), lambda i, j, kk: (0, j)) for _ in rows]
    in_place = [o for o in outs if o.arr is not None]
    alias_specs = [pl.BlockSpec(memory_space=pl.ANY) for _ in in_place]
    out_specs = [o.spec(tm, tn, lambda i, j, kk: (i, j)) for o in outs]
    n_in = 2 + len(tiles) + len(rows)
    aliases = {}
    for o_idx, o in enumerate(outs):
        if o.arr is not None:
            aliases[n_in + in_place.index(o)] = o_idx
    n_t, n_r, n_a, n_o = len(tiles), len(rows), len(in_place) + len(deps), len(outs)
    dep_specs = [pl.BlockSpec(memory_space=pl.ANY) for _ in deps]

    def body(*refs):
        a_ref, b_ref = refs[0], refs[1]
        tile_refs = refs[2:2 + n_t]
        row_refs = refs[2 + n_t:2 + n_t + n_r]
        out_refs = refs[2 + n_t + n_r + n_a:2 + n_t + n_r + n_a + n_o]
        acc_ref = refs[-1] if gk > 1 else None

        def finish(acc):
            extra = [t[...] for t in tile_refs] + [r[...] for r in row_refs]
            res = epilogue(acc, *extra) if epilogue is not None else (acc,)
            for o_ref, r in zip(out_refs, res):
                o_ref[...] = r.astype(o_ref.dtype)

        prod = lax.dot_general(a_ref[...].astype(BF16), b_ref[...].astype(BF16), dims, preferred_element_type=F32)
        if gk == 1:
            finish(prod)
        else:
            kk = pl.program_id(2)

            @pl.when(kk == 0)
            def _():
                acc_ref[...] = prod

            @pl.when(kk > 0)
            def _():
                acc_ref[...] += prod

            @pl.when(kk == gk - 1)
            def _():
                finish(acc_ref[...])

    res = pl.pallas_call(
        body,
        name=name,
        grid=(gm, gn, gk),
        in_specs=[a_spec, b_spec] + tile_specs + row_specs + alias_specs + dep_specs,
        out_specs=out_specs,
        out_shape=[jax.ShapeDtypeStruct(o.full, o.dtype) for o in outs],
        scratch_shapes=[pltpu.VMEM((tm, tn), F32)] if gk > 1 else [],
        input_output_aliases=aliases,
        compiler_params=_params("parallel", "parallel", "arbitrary"),
    )(a.arr, b.arr, *[t.arr for t in tiles], *rows, *[o.arr for o in in_place], *deps)
    return res


def rows_call(name, fn, tiled, vecs, tiled_out, acc_out, tr=256):
    t = tiled[0].shape[0]
    tr = min(tr, t)
    assert t % tr == 0
    n1, n2, n3 = len(tiled), len(vecs), len(tiled_out)

    def body(*refs):
        fn(pl.program_id(0), refs[:n1], refs[n1:n1 + n2], refs[n1 + n2:n1 + n2 + n3], refs[n1 + n2 + n3:])

    return pl.pallas_call(
        body,
        name=name,
        grid=(t // tr,),
        in_specs=[pl.BlockSpec((tr, x.shape[1]), lambda i: (i, 0)) for x in tiled]
        + [pl.BlockSpec(v.shape, lambda i: (0, 0)) for v in vecs],
        out_specs=[pl.BlockSpec((tr, c), lambda i: (i, 0)) for c, _ in tiled_out]
        + [pl.BlockSpec(s, lambda i: (0, 0)) for s, _ in acc_out],
        out_shape=[jax.ShapeDtypeStruct((t, c), d) for c, d in tiled_out] + [jax.ShapeDtypeStruct(s, d) for s, d in acc_out],
        compiler_params=_params("arbitrary" if acc_out else "parallel"),
    )(*tiled, *vecs)


def _accumulate(step, ref, val):
    @pl.when(step == 0)
    def _():
        ref[...] = val

    @pl.when(step > 0)
    def _():
        ref[...] += val


def _ln_stats(s):
    mu = jnp.mean(s, axis=-1, keepdims=True)
    d = s - mu
    var = jnp.mean(d * d, axis=-1, keepdims=True)
    rstd = lax.rsqrt(var + LN_EPS)
    return d * rstd, rstd


def ln_fwd(name, alpha, x_in, m, g, b):
    d = x_in.shape[1]

    def fn(step, tiled, vecs, outs, accs):
        s = alpha * tiled[0][...] + tiled[1][...]
        xhat, _ = _ln_stats(s)
        y = xhat * vecs[0][...] + vecs[1][...]
        outs[0][...] = y
        outs[1][...] = y.astype(BF16)
        outs[2][...] = s

    return rows_call(name, fn, [x_in, m], [g, b], [(d, F32), (d, BF16), (d, F32)], [])


def ln_bwd(name, ca, da, db, s, g):
    d = s.shape[1]

    def fn(step, tiled, vecs, outs, accs):
        dx = ca * tiled[0][...] + tiled[1][...]
        xhat, rstd = _ln_stats(tiled[2][...])
        dxh = dx * vecs[0][...]
        ds = rstd * (dxh - jnp.mean(dxh, axis=-1, keepdims=True) - xhat * jnp.mean(dxh * xhat, axis=-1, keepdims=True))
        outs[0][...] = ds
        outs[1][...] = ds.astype(BF16)
        _accumulate(step, accs[0], jnp.sum(dx * xhat, axis=0, keepdims=True))
        _accumulate(step, accs[1], jnp.sum(dx, axis=0, keepdims=True))

    return rows_call(name, fn, [da, db, s], [g], [(d, F32), (d, BF16)], [((1, d), F32), ((1, d), F32)])


def ple_bwd(name, dx3, gate, e):
    d = dx3.shape[1]

    def fn(step, tiled, vecs, outs, accs):
        dx, gt, ev = tiled[0][...], tiled[1][...], tiled[2][...]
        dpre = dx * ev * gt * (1.0 - gt)
        outs[0][...] = (dx * gt).astype(BF16)
        outs[1][...] = dpre.astype(BF16)
        _accumulate(step, accs[0], jnp.sum(dpre, axis=0, keepdims=True))

    return rows_call(name, fn, [dx3, gate, e], [], [(d, BF16), (d, BF16)], [((1, d), F32)])


def loss_head(name, y, target):
    t, d = y.shape

    def fn(step, tiled, vecs, outs, accs):
        err = tiled[0][...] - tiled[1][...]
        outs[0][...] = err * (1.0 / d)
        part = jnp.sum(jnp.sum(err * err, axis=1, keepdims=True), axis=0, keepdims=True) * (0.5 / d)
        _accumulate(step, accs[0], part)

    return rows_call(name, fn, [y, target], [], [(d, F32)], [((1, 1), F32)])


def _softplus(z):
    return jnp.maximum(z, 0.0) + jnp.log1p(jnp.exp(-jnp.abs(z)))


def _gelu(y):
    th = jnp.tanh(GELU_C * (y + GELU_K * (y * y * y)))
    cdf = 0.5 * (1.0 + th)
    return y * cdf, cdf + 0.5 * y * (1.0 - th * th) * (GELU_C * (1.0 + 3.0 * GELU_K * y * y))


def _up(win, k):
    return pltpu.roll(win, win.shape[0] - k, 0)


def _down(win, k):
    return pltpu.roll(win, k, 0)


def _lru_gates(win, row0, cw_ref, cb, wa, ba, wx, bx, sp):
    h = CONV_HALO
    u = (cb + cw_ref[3:4, :] * win[h:] + cw_ref[2:3, :] * _down(win, 1)[h:]
         + cw_ref[1:2, :] * _down(win, 2)[h:] + cw_ref[0:1, :] * _down(win, 3)[h:])
    ub = u.astype(BF16)
    r = jax.nn.sigmoid(jnp.dot(ub, wa, preferred_element_type=F32) + ba)
    ig = jax.nn.sigmoid(jnp.dot(ub, wx, preferred_element_type=F32) + bx)
    log_a = (-LRU_C) * r * sp
    a = jnp.exp(log_a)
    mult = jnp.sqrt(-jnp.tanh(log_a) * (a * a + 1.0))
    first = (row0 + lax.broadcasted_iota(jnp.int32, u.shape, 0)) == 0
    mult = jnp.where(first, 1.0, mult)
    return u, r, ig, a, mult, first


def _block_scan(a, b, reverse):
    pos = lax.broadcasted_iota(jnp.int32, a.shape, 0) & 7
    shift, spread = (_up, _down) if reverse else (_down, _up)
    for s in (1, 2, 4):
        keep = (pos >= 8 - s) if reverse else (pos < s)
        b = jnp.where(keep, b, a * shift(b, s) + b)
        a = jnp.where(keep, a, a * shift(a, s))
    at, bt = a, b
    for s in (1, 2, 4):
        keep = (pos < s) if reverse else (pos >= 8 - s)
        at = jnp.where(keep, at, spread(at, s))
        bt = jnp.where(keep, bt, spread(bt, s))
    return a, b, at, bt


def _carry_scan(a_ref, b_ref, at_ref, bt_ref, out_ref, out_off, t, reverse):
    groups = t // 8

    def block(j, h_in):
        r0 = pl.multiple_of((groups - 1 - j if reverse else j) * 8, 8)
        out_ref[pl.ds(pl.multiple_of(out_off + r0, 8), 8), :] = a_ref[pl.ds(r0, 8), :] * h_in + b_ref[pl.ds(r0, 8), :]
        return at_ref[pl.ds(r0, 8), :] * h_in + bt_ref[pl.ds(r0, 8), :]

    lax.fori_loop(0, groups, block, jnp.zeros((8, LRU_BW), F32), unroll=8)


def _lru_in_specs(t, heads):
    blk = lambda i: (0, i)
    return [
        pl.BlockSpec((2, t, LRU_BW), lambda i: (0, 0, i)),
        pl.BlockSpec((CONV_WIDTH, LRU_BW), blk),
        pl.BlockSpec((1, LRU_BW), blk),
        pl.BlockSpec((None, LRU_BW, LRU_BW), lambda i: (i, 0, 0)),
        pl.BlockSpec((1, LRU_BW), blk),
        pl.BlockSpec((None, LRU_BW, LRU_BW), lambda i: (i, 0, 0)),
        pl.BlockSpec((1, LRU_BW), blk),
        pl.BlockSpec((1, LRU_BW), blk),
    ]


def lru_fwd(name, proj, conv_w, conv_b, wa, ba, wx, bx, lam):
    _, t, c = proj.shape
    heads = c // LRU_BW
    rc = min(256, t)

    def body(proj_ref, cw_ref, cb_ref, wa_ref, ba_ref, wx_ref, bx_ref, lam_ref, out_ref, upad, a_s, b_s, at_s, bt_s):
        upad[0:CONV_HALO, :] = jnp.zeros((CONV_HALO, LRU_BW), F32)
        upad[CONV_HALO:, :] = proj_ref[0]
        sp = _softplus(-lam_ref[...])
        cb, ba, bx, wa, wx = cb_ref[...], ba_ref[...], bx_ref[...], wa_ref[...], wx_ref[...]

        def gates(i, carry):
            r0 = pl.multiple_of(i * rc, rc)
            win = upad[pl.ds(r0, rc + CONV_HALO), :]
            u, r, ig, a, mult, _ = _lru_gates(win, r0, cw_ref, cb, wa, ba, wx, bx, sp)
            rows = pl.ds(r0, rc)
            a_s[rows, :], b_s[rows, :], at_s[rows, :], bt_s[rows, :] = _block_scan(a, mult * (ig * u), False)
            return carry

        lax.fori_loop(0, t // rc, gates, 0)
        _carry_scan(a_s, b_s, at_s, bt_s, b_s, 0, t, False)

        def gate_out(i, carry):
            r0 = pl.multiple_of(i * rc, rc)
            gy, _ = _gelu(proj_ref[1, pl.ds(r0, rc), :])
            out_ref[pl.ds(r0, rc), :] = (b_s[pl.ds(r0, rc), :] * gy).astype(BF16)
            return carry

        lax.fori_loop(0, t // rc, gate_out, 0)

    return pl.pallas_call(
        body,
        name=name,
        grid=(heads,),
        in_specs=_lru_in_specs(t, heads),
        out_specs=pl.BlockSpec((t, LRU_BW), lambda i: (0, i)),
        out_shape=jax.ShapeDtypeStruct((t, c), BF16),
        scratch_shapes=[pltpu.VMEM((t + CONV_HALO, LRU_BW), F32)] + [pltpu.VMEM((t, LRU_BW), F32)] * 4,
        compiler_params=_params("parallel"),
    )(proj, conv_w, conv_b, wa, ba, wx, bx, lam)


def lru_bwd(name, proj, dhg, conv_w, conv_b, wa, ba, wx, bx, lam):
    _, t, c = proj.shape
    heads = c // LRU_BW
    rc = min(256, t)
    h8 = CONV_HALO

    def body(proj_ref, dhg_ref, cw_ref, cb_ref, wa_ref, ba_ref, wx_ref, bx_ref, lam_ref,
             dproj_ref, dcw_ref, dcb_ref, dba_ref, dbx_ref, dlam_ref, dwa_ref, dwx_ref,
             upad, u_s, r_s, ig_s, apad, hpad, g_s, dupad, sa_s, sb_s, at_s, bt_s):
        zeros8 = jnp.zeros((h8, LRU_BW), F32)
        upad[0:h8, :] = zeros8
        upad[h8:, :] = proj_ref[0]
        hpad[0:h8, :] = zeros8
        apad[t:, :] = zeros8
        dupad[t:, :] = zeros8
        lam = lam_ref[...]
        sp = _softplus(-lam)
        cb, ba, bx, wa, wx = cb_ref[...], ba_ref[...], bx_ref[...], wa_ref[...], wx_ref[...]

        def gates(i, carry):
            r0 = pl.multiple_of(i * rc, rc)
            win = upad[pl.ds(r0, rc + h8), :]
            u, r, ig, a, mult, _ = _lru_gates(win, r0, cw_ref, cb, wa, ba, wx, bx, sp)
            u_s[pl.ds(r0, rc), :] = u
            r_s[pl.ds(r0, rc), :] = r
            ig_s[pl.ds(r0, rc), :] = ig
            rows = pl.ds(r0, rc)
            apad[rows, :] = a
            sa_s[rows, :], sb_s[rows, :], at_s[rows, :], bt_s[rows, :] = _block_scan(a, mult * (ig * u), False)
            return carry

        lax.fori_loop(0, t // rc, gates, 0)
        _carry_scan(sa_s, sb_s, at_s, bt_s, hpad, h8, t, False)

        def out_gate(i, carry):
            r0 = pl.multiple_of(i * rc, rc)
            gy, dgy = _gelu(proj_ref[1, pl.ds(r0, rc), :])
            dh = dhg_ref[pl.ds(r0, rc), :]
            hh = hpad[pl.ds(pl.multiple_of(r0 + h8, 8), rc), :]
            dproj_ref[1, pl.ds(r0, rc), :] = (dh * hh * dgy).astype(BF16)
            rows = pl.ds(r0, rc)
            a_next = _up(apad[pl.ds(r0, rc + h8), :], 1)[:rc]
            sa_s[rows, :], sb_s[rows, :], at_s[rows, :], bt_s[rows, :] = _block_scan(a_next, dh * gy, True)
            return carry

        lax.fori_loop(0, t // rc, out_gate, 0)
        _carry_scan(sa_s, sb_s, at_s, bt_s, g_s, 0, t, True)

        zrow = jnp.zeros((1, LRU_BW), F32)
        zmat = jnp.zeros((LRU_BW, LRU_BW), F32)

        def grads(i, carry):
            dsp, dba, dbx, dwa, dwx = carry
            r0 = pl.multiple_of(i * rc, rc)
            g = g_s[pl.ds(r0, rc), :]
            u, r, ig, a = u_s[pl.ds(r0, rc), :], r_s[pl.ds(r0, rc), :], ig_s[pl.ds(r0, rc), :], apad[pl.ds(r0, rc), :]
            hprev = _down(hpad[pl.ds(r0, rc + h8), :], 1)[h8:]
            first = (r0 + lax.broadcasted_iota(jnp.int32, u.shape, 0)) == 0
            log_a = (-LRU_C) * r * sp
            mult = jnp.where(first, 1.0, jnp.sqrt(-jnp.tanh(log_a) * (a * a + 1.0)))
            dmult = jnp.where(first, 0.0, g * (ig * u))
            dlog_a = g * hprev * a - dmult * (a * a) / mult
            dr = dlog_a * ((-LRU_C) * sp)
            dpre_r = dr * r * (1.0 - r)
            dpre_i = (g * mult * u) * ig * (1.0 - ig)
            pr, pi, ub = dpre_r.astype(BF16), dpre_i.astype(BF16), u.astype(BF16)
            nt = (((1,), (1,)), ((), ()))
            tn = (((0,), (0,)), ((), ()))
            du = (g * mult * ig + lax.dot_general(pr, wa, nt, preferred_element_type=F32)
                  + lax.dot_general(pi, wx, nt, preferred_element_type=F32))
            dupad[pl.ds(r0, rc), :] = du
            return (dsp + jnp.sum(dlog_a * ((-LRU_C) * r), axis=0, keepdims=True),
                    dba + jnp.sum(dpre_r, axis=0, keepdims=True),
                    dbx + jnp.sum(dpre_i, axis=0, keepdims=True),
                    dwa + lax.dot_general(ub, pr, tn, preferred_element_type=F32),
                    dwx + lax.dot_general(ub, pi, tn, preferred_element_type=F32))

        dsp, dba, dbx, dwa, dwx = lax.fori_loop(0, t // rc, grads, (zrow, zrow, zrow, zmat, zmat))
        dba_ref[...] = dba
        dbx_ref[...] = dbx
        dwa_ref[...] = dwa
        dwx_ref[...] = dwx
        dlam_ref[...] = -dsp * jax.nn.sigmoid(-lam)

        def conv_back(i, carry):
            dcb, d0, d1, d2, d3 = carry
            r0 = pl.multiple_of(i * rc, rc)
            dwin = dupad[pl.ds(r0, rc + h8), :]
            du = dwin[:rc]
            du0 = (cw_ref[3:4, :] * du + cw_ref[2:3, :] * _up(dwin, 1)[:rc]
                   + cw_ref[1:2, :] * _up(dwin, 2)[:rc] + cw_ref[0:1, :] * _up(dwin, 3)[:rc])
            dproj_ref[0, pl.ds(r0, rc), :] = du0.astype(BF16)
            win = upad[pl.ds(r0, rc + h8), :]
            red = lambda v: jnp.sum(v, axis=0, keepdims=True)
            return (dcb + red(du), d0 + red(du * _down(win, 3)[h8:]), d1 + red(du * _down(win, 2)[h8:]),
                    d2 + red(du * _down(win, 1)[h8:]), d3 + red(du * win[h8:]))

        dcb, d0, d1, d2, d3 = lax.fori_loop(0, t // rc, conv_back, (zrow,) * 5)
        dcb_ref[...] = dcb
        dcw_ref[0:1, :] = d0
        dcw_ref[1:2, :] = d1
        dcw_ref[2:3, :] = d2
        dcw_ref[3:4, :] = d3

    blk = lambda i: (0, i)
    vec = jax.ShapeDtypeStruct((1, c), F32)
    mat = jax.ShapeDtypeStruct((heads, LRU_BW, LRU_BW), F32)
    full = lambda: pltpu.VMEM((t, LRU_BW), F32)
    padded = lambda: pltpu.VMEM((t + h8, LRU_BW), F32)
    return pl.pallas_call(
        body,
        name=name,
        grid=(heads,),
        in_specs=_lru_in_specs(t, heads)[:1] + [pl.BlockSpec((t, LRU_BW), blk)] + _lru_in_specs(t, heads)[1:],
        out_specs=[pl.BlockSpec((2, t, LRU_BW), lambda i: (0, 0, i)), pl.BlockSpec((CONV_WIDTH, LRU_BW), blk)]
        + [pl.BlockSpec((1, LRU_BW), blk)] * 4 + [pl.BlockSpec((None, LRU_BW, LRU_BW), lambda i: (i, 0, 0))] * 2,
        out_shape=[jax.ShapeDtypeStruct((2, t, c), BF16), jax.ShapeDtypeStruct((CONV_WIDTH, c), F32), vec, vec, vec, vec, mat, mat],
        scratch_shapes=[padded(), full(), full(), full(), padded(), padded(), full(), padded()] + [full()] * 4,
        compiler_params=_params("parallel"),
    )(proj, dhg, conv_w, conv_b, wa, ba, wx, bx, lam)


def _pick_level(g, levels):
    out = levels[-1]
    for k in range(len(levels) - 2, -1, -1):
        out = jnp.where(g == k, levels[k], out)
    return out


def _pool_z(win, g, row0, rc):
    levels, cur = [], win
    for k in range(len(POOL_WINDOWS)):
        cur = cur + _down(cur, 1 << k)
        levels.append(cur[POOL_HALO:])
    tot = _pick_level(g, levels)
    width = jnp.left_shift(2, g)
    row = row0 + lax.broadcasted_iota(jnp.int32, tot.shape, 0)
    cnt = jnp.minimum(row + 1, width).astype(F32)
    return tot / cnt - win[POOL_HALO:], cnt


def _pool_specs(t, gw):
    blk = lambda g: (0, g)
    return [pl.BlockSpec((t, gw), blk), pl.BlockSpec((None, gw, gw), lambda g: (g, 0, 0)),
            pl.BlockSpec((1, gw), blk), pl.BlockSpec((1, gw), blk)]


def pool_fwd(name, u, w_grp, b_grp, scale):
    t, d = u.shape
    gw = d // len(POOL_WINDOWS)
    rc = min(256, t)

    def body(u_ref, wg_ref, bg_ref, sc_ref, out_ref, upad):
        g = pl.program_id(0)
        upad[0:POOL_HALO, :] = jnp.zeros((POOL_HALO, gw), F32)
        upad[POOL_HALO:, :] = u_ref[...]
        wg, bg, sc = wg_ref[...], bg_ref[...], sc_ref[...]

        def chunk(i, carry):
            r0 = pl.multiple_of(i * rc, rc)
            z, _ = _pool_z(upad[pl.ds(r0, rc + POOL_HALO), :], g, r0, rc)
            z2 = jnp.dot(z.astype(BF16), wg, preferred_element_type=F32) + bg
            out_ref[pl.ds(r0, rc), :] = (z2 * sc).astype(BF16)
            return carry

        lax.fori_loop(0, t // rc, chunk, 0)

    return pl.pallas_call(
        body,
        name=name,
        grid=(len(POOL_WINDOWS),),
        in_specs=_pool_specs(t, gw),
        out_specs=pl.BlockSpec((t, gw), lambda g: (0, g)),
        out_shape=jax.ShapeDtypeStruct((t, d), BF16),
        scratch_shapes=[pltpu.VMEM((t + POOL_HALO, gw), F32)],
        compiler_params=_params("parallel"),
    )(u, w_grp, b_grp, scale)


def pool_bwd(name, u, dzs, w_grp, b_grp, scale):
    t, d = u.shape
    gw = d // len(POOL_WINDOWS)
    rc = min(256, t)

    def body(u_ref, dzs_ref, wg_ref, bg_ref, sc_ref, du_ref, dwg_ref, dbg_ref, dsc_ref, upad, qpad, dz_s):
        g = pl.program_id(0)
        upad[0:POOL_HALO, :] = jnp.zeros((POOL_HALO, gw), F32)
        upad[POOL_HALO:, :] = u_ref[...]
        qpad[t:, :] = jnp.zeros((POOL_HALO, gw), F32)
        wg, bg, sc = wg_ref[...], bg_ref[...], sc_ref[...]
        zrow = jnp.zeros((1, gw), F32)

        def chunk(i, carry):
            dsc, dbg, dwg = carry
            r0 = pl.multiple_of(i * rc, rc)
            z, cnt = _pool_z(upad[pl.ds(r0, rc + POOL_HALO), :], g, r0, rc)
            zb = z.astype(BF16)
            z2 = jnp.dot(zb, wg, preferred_element_type=F32) + bg
            dzs = dzs_ref[pl.ds(r0, rc), :]
            dz2 = dzs * sc
            d2b = dz2.astype(BF16)
            dz = lax.dot_general(d2b, wg, (((1,), (1,)), ((), ())), preferred_element_type=F32)
            dz_s[pl.ds(r0, rc), :] = dz
            qpad[pl.ds(r0, rc), :] = dz / cnt
            return (dsc + jnp.sum(dzs * z2, axis=0, keepdims=True), dbg + jnp.sum(dz2, axis=0, keepdims=True),
                    dwg + lax.dot_general(zb, d2b, (((0,), (0,)), ((), ())), preferred_element_type=F32))

        dsc, dbg, dwg = lax.fori_loop(0, t // rc, chunk, (zrow, zrow, jnp.zeros((gw, gw), F32)))
        dsc_ref[...] = dsc
        dbg_ref[...] = dbg
        dwg_ref[...] = dwg

        def spread(i, carry):
            r0 = pl.multiple_of(i * rc, rc)
            levels, cur = [], qpad[pl.ds(r0, rc + POOL_HALO), :]
            for k in range(len(POOL_WINDOWS)):
                cur = cur + _up(cur, 1 << k)
                levels.append(cur[:rc])
            du_ref[pl.ds(r0, rc), :] = (_pick_level(g, levels) - dz_s[pl.ds(r0, rc), :]).astype(BF16)
            return carry

        lax.fori_loop(0, t // rc, spread, 0)

    blk = lambda g: (0, g)
    vec = jax.ShapeDtypeStruct((1, d), F32)
    return pl.pallas_call(
        body,
        name=name,
        grid=(len(POOL_WINDOWS),),
        in_specs=_pool_specs(t, gw)[:1] + [pl.BlockSpec((t, gw), blk)] + _pool_specs(t, gw)[1:],
        out_specs=[pl.BlockSpec((t, gw), blk), pl.BlockSpec((None, gw, gw), lambda g: (g, 0, 0)),
                   pl.BlockSpec((1, gw), blk), pl.BlockSpec((1, gw), blk)],
        out_shape=[jax.ShapeDtypeStruct((t, d), BF16), jax.ShapeDtypeStruct((len(POOL_WINDOWS), gw, gw), F32), vec, vec],
        scratch_shapes=[pltpu.VMEM((t + POOL_HALO, gw), F32), pltpu.VMEM((t + POOL_HALO, gw), F32), pltpu.VMEM((t, gw), F32)],
        compiler_params=_params("parallel"),
    )(u, dzs, w_grp, b_grp, scale)


def _place():
    return lax.axis_index("x"), lax.axis_index("y"), lax.axis_index("c")


def _other_chips(x, y):
    return [(1 - x, y), (x, 1 - y), (1 - x, 1 - y)]


def _half(c, rows):
    h = rows // 2
    return pl.ds(pl.multiple_of(c * h, 8), h)


_ANY = pl.BlockSpec(memory_space=pl.ANY)


def into_block(name, shards, layer, r, me, dtype):
    c = shards.shape[1]
    tr = _tile(r, 512, 16)
    per = r // tr

    def body(me_ref, s_ref, o_ref):
        o_ref[...] = s_ref[...].astype(o_ref.dtype)

    return pl.pallas_call(
        body,
        name=name,
        grid_spec=pltpu.PrefetchScalarGridSpec(
            num_scalar_prefetch=1,
            grid=(per,),
            in_specs=[pl.BlockSpec((tr, c), lambda i, me_ref: (layer * per + i, 0))],
            out_specs=pl.BlockSpec((None, tr, c), lambda i, me_ref: (me_ref[0], i, 0)),
        ),
        out_shape=jax.ShapeDtypeStruct((N_CHIPS, r, c), dtype),
        compiler_params=_params("parallel"),
    )(me, shards)


_HBM = pl.BlockSpec(memory_space=pltpu.HBM)
_SEM = pl.BlockSpec(memory_space=pltpu.SEMAPHORE)


def _in_hbm(a):
    return pltpu.with_memory_space_constraint(a, pltpu.HBM)


def split_start(name, plan, n_copies, bufs, dep):
    n = len(bufs)

    def body(*refs):
        for cp in plan(refs[:n], refs[n + 1], refs[n + 2]):
            cp.start()
        refs[-1][...] = jnp.zeros_like(refs[-1])

    res = pl.pallas_call(
        body,
        name=name,
        in_specs=[_HBM] * n + [_ANY],
        out_specs=[_SEM, _SEM] + [_HBM] * n + [pl.BlockSpec(memory_space=pltpu.VMEM)],
        out_shape=[pltpu.SemaphoreType.DMA((n_copies,)), pltpu.SemaphoreType.DMA((n_copies,))]
        + [pltpu.HBM(b.shape, b.dtype) for b in bufs] + [jax.ShapeDtypeStruct((8, 128), F32)],
        input_output_aliases={i: 2 + i for i in range(n)},
        compiler_params=pltpu.CompilerParams(has_side_effects=pltpu.SideEffectType.DATAFLOW_SIDE_EFFECTING),
    )(*[_in_hbm(b) for b in bufs], dep)
    return res[0], res[1], list(res[2:2 + n]), res[-1]


def split_wait(name, plan, send_sems, recv_sems, bufs, after):
    n = len(bufs)

    def body(*refs):
        copies = plan(refs[:n], refs[n], refs[n + 1])
        for cp in copies:
            cp.wait_send()
        for cp in copies:
            cp.wait_recv()

    return pl.pallas_call(
        body,
        name=name,
        in_specs=[_HBM] * n + [_SEM, _SEM, _ANY],
        out_specs=[_HBM] * n,
        out_shape=[pltpu.HBM(b.shape, b.dtype) for b in bufs],
        input_output_aliases={i: i for i in range(n)},
        compiler_params=pltpu.CompilerParams(has_side_effects=pltpu.SideEffectType.DATAFLOW_SIDE_EFFECTING),
    )(*bufs, send_sems, recv_sems, after)


def gather_plan(n):
    def plan(bufs, send_sems, recv_sems):
        x, y, c = _place()
        copies = []
        for i in range(n):
            blk = bufs[i].at[2 * x + y, _half(c, bufs[i].shape[1]), :]
            for j, chip in enumerate(_other_chips(x, y)):
                copies.append(pltpu.make_async_remote_copy(
                    src_ref=blk, dst_ref=blk, send_sem=send_sems.at[3 * i + j], recv_sem=recv_sems.at[3 * i + j],
                    device_id=(*chip, c), device_id_type=MESH))
        return copies

    return plan


def pair_forward(name, bufs):
    n = len(bufs)

    def body(*refs):
        outs = refs[n:2 * n]
        send_sems, recv_sems = refs[2 * n:]
        x, y, c = _place()
        copies = []
        for i in range(n):
            for j, (cx, cy) in enumerate(_other_chips(x, y)):
                blk = outs[i].at[2 * cx + cy, _half(c, outs[i].shape[1]), :]
                copies.append(pltpu.make_async_remote_copy(
                    src_ref=blk, dst_ref=blk, send_sem=send_sems.at[3 * i + j], recv_sem=recv_sems.at[3 * i + j],
                    device_id=(x, y, 1 - c), device_id_type=MESH))
        for cp in copies:
            cp.start()
        for cp in copies:
            cp.wait()

    return pl.pallas_call(
        body,
        name=name,
        in_specs=[_ANY] * n,
        out_specs=[_ANY] * n,
        out_shape=[jax.ShapeDtypeStruct(b.shape, b.dtype) for b in bufs],
        input_output_aliases={i: i for i in range(n)},
        scratch_shapes=[pltpu.SemaphoreType.DMA((3 * n,)), pltpu.SemaphoreType.DMA((3 * n,))],
    )(*bufs)


def all_gather_chips(name, bufs):
    n = len(bufs)

    def body(*refs):
        outs = refs[n:2 * n]
        send_sems, recv_sems = refs[2 * n:]
        x, y, c = _place()
        me, sibling = 2 * x + y, (x, y, 1 - c)
        chips = _other_chips(x, y)

        def copy(i, slot, block, half, to):
            blk = outs[i].at[block, _half(half, outs[i].shape[1]), :]
            return pltpu.make_async_remote_copy(
                src_ref=blk, dst_ref=blk, send_sem=send_sems.at[i * 6 + slot], recv_sem=recv_sems.at[i * 6 + slot],
                device_id=to, device_id_type=MESH)

        first = [copy(i, j, me, c, (*chip, c)) for i in range(n) for j, chip in enumerate(chips)]
        for cp in first:
            cp.start()
        passed = []
        for i in range(n):
            for j, (cx, cy) in enumerate(chips):
                copy(i, j, 2 * cx + cy, c, (x, y, c)).wait_recv()
                fwd = copy(i, 3 + j, 2 * cx + cy, c, sibling)
                fwd.start()
                passed.append(fwd)
        for i in range(n):
            for j, (cx, cy) in enumerate(chips):
                copy(i, 3 + j, 2 * cx + cy, 1 - c, (x, y, c)).wait_recv()
        for cp in first + passed:
            cp.wait_send()

    return pl.pallas_call(
        body,
        name=name,
        in_specs=[_ANY] * n,
        out_specs=[_ANY] * n,
        out_shape=[jax.ShapeDtypeStruct(b.shape, b.dtype) for b in bufs],
        input_output_aliases={i: i for i in range(n)},
        scratch_shapes=[pltpu.SemaphoreType.DMA((6 * n,)), pltpu.SemaphoreType.DMA((6 * n,))],
    )(*bufs)


def pair_exchange(name, grads):
    n = len(grads)

    def body(*refs):
        ins, outs = refs[:n], refs[n:2 * n]
        send_sems, recv_sems = refs[2 * n:]
        x, y, c = _place()
        copies = [pltpu.make_async_remote_copy(
            src_ref=ins[i].at[:, _half(1 - c, ins[i].shape[1]), :], dst_ref=outs[i], send_sem=send_sems.at[i],
            recv_sem=recv_sems.at[i], device_id=(x, y, 1 - c), device_id_type=MESH) for i in range(n)]
        for cp in copies:
            cp.start()
        for cp in copies:
            cp.wait()

    return pl.pallas_call(
        body,
        name=name,
        in_specs=[_ANY] * n,
        out_specs=[_ANY] * n,
        out_shape=[jax.ShapeDtypeStruct((g.shape[0], g.shape[1] // 2, g.shape[2]), g.dtype) for g in grads],
        scratch_shapes=[pltpu.SemaphoreType.DMA((n,)), pltpu.SemaphoreType.DMA((n,))],
    )(*grads)


def chip_exchange(name, parts):
    n = len(parts)

    def body(*refs):
        ins, outs = refs[:n], refs[n:2 * n]
        send_sems, recv_sems = refs[2 * n:]
        x, y, c = _place()
        me = 2 * x + y
        chips = _other_chips(x, y)
        copies = []
        for i in range(n):
            for j, (cx, cy) in enumerate(chips):
                copies.append(pltpu.make_async_remote_copy(
                    src_ref=ins[i].at[2 * cx + cy], dst_ref=outs[i].at[me], send_sem=send_sems.at[3 * i + j],
                    recv_sem=recv_sems.at[3 * i + j], device_id=(cx, cy, c), device_id_type=MESH))
        for cp in copies:
            cp.start()
        for cp in copies:
            cp.wait()

    return pl.pallas_call(
        body,
        name=name,
        in_specs=[_ANY] * n,
        out_specs=[_ANY] * n,
        out_shape=[jax.ShapeDtypeStruct(p.shape, p.dtype) for p in parts],
        scratch_shapes=[pltpu.SemaphoreType.DMA((3 * n,)), pltpu.SemaphoreType.DMA((3 * n,))],
    )(*parts)


def pair_gather(name, bufs, blocked):
    n = len(bufs)

    def body(*refs):
        outs = refs[n:2 * n]
        send_sems, recv_sems = refs[2 * n:]
        x, y, c = _place()
        copies = []
        for i in range(n):
            buf = outs[i].at[2 * x + y] if blocked[i] else outs[i]
            mine = buf.at[_half(c, buf.shape[0]), :]
            copies.append(pltpu.make_async_remote_copy(
                src_ref=mine, dst_ref=mine, send_sem=send_sems.at[i], recv_sem=recv_sems.at[i],
                device_id=(x, y, 1 - c), device_id_type=MESH))
        for cp in copies:
            cp.start()
        for cp in copies:
            cp.wait()

    return pl.pallas_call(
        body,
        name=name,
        in_specs=[_ANY] * n,
        out_specs=[_ANY] * n,
        out_shape=[jax.ShapeDtypeStruct(b.shape, b.dtype) for b in bufs],
        input_output_aliases={i: i for i in range(n)},
        scratch_shapes=[pltpu.SemaphoreType.DMA((n,)), pltpu.SemaphoreType.DMA((n,))],
    )(*bufs)


def pair_sum(name, grad, recv, core, dtype):
    _, r, c = grad.shape
    h = r // 2
    th = _tile(h, 512, 16)
    per = h // th

    def body(core_ref, g_ref, r_ref, o_ref):
        o_ref[...] = (g_ref[...] + r_ref[...]).astype(o_ref.dtype)

    return pl.pallas_call(
        body,
        name=name,
        grid_spec=pltpu.PrefetchScalarGridSpec(
            num_scalar_prefetch=1,
            grid=(N_CHIPS, per),
            in_specs=[pl.BlockSpec((None, th, c), lambda k, i, core_ref: (k, core_ref[0] * per + i, 0)),
                      pl.BlockSpec((None, th, c), lambda k, i, core_ref: (k, i, 0))],
            out_specs=pl.BlockSpec((None, th, c), lambda k, i, core_ref: (k, i, 0)),
        ),
        out_shape=jax.ShapeDtypeStruct((N_CHIPS, h, c), dtype),
        compiler_params=_params("parallel", "parallel"),
    )(core, grad, recv)


def chip_sum(name, got, parts, place, blocked):
    _, h, c = parts.shape
    th = _tile(h, 256, 16)
    per = h // th

    def body(place_ref, q0, q1, q2, q3, p_ref, o_ref):
        me = place_ref[0]
        own = p_ref[...].astype(F32)
        v = [jnp.where(me == k, own, q[...].astype(F32)) for k, q in enumerate((q0, q1, q2, q3))]
        o_ref[...] = ((v[0] + v[1]) + v[2]) + v[3]

    def got_spec(k):
        return pl.BlockSpec((None, th, c), lambda i, pr: (jnp.where(pr[0] == k, (k + 1) % N_CHIPS, k), i, 0))

    if blocked:
        out_spec = pl.BlockSpec((None, th, c), lambda i, pr: (pr[0], pr[1] * per + i, 0))
        out_shape = jax.ShapeDtypeStruct((N_CHIPS, 2 * h, c), F32)
    else:
        out_spec = pl.BlockSpec((th, c), lambda i, pr: (pr[1] * per + i, 0))
        out_shape = jax.ShapeDtypeStruct((2 * h, c), F32)
    return pl.pallas_call(
        body,
        name=name,
        grid_spec=pltpu.PrefetchScalarGridSpec(
            num_scalar_prefetch=1,
            grid=(per,),
            in_specs=[got_spec(k) for k in range(N_CHIPS)] + [pl.BlockSpec((None, th, c), lambda i, pr: (pr[0], i, 0))],
            out_specs=out_spec,
        ),
        out_shape=out_shape,
        compiler_params=_params("parallel"),
    )(place, got, got, got, got, parts)


def adamw(name, w, g, m, v):
    r, c = w.shape
    tr = _tile(r, 512, 8)
    c1 = 1.0 - ADAM_B1 ** ADAM_STEP
    c2 = 1.0 - ADAM_B2 ** ADAM_STEP

    def body(w_ref, g_ref, m_ref, v_ref, d_ref, nm_ref, nv_ref):
        gv = g_ref[...]
        nm = ADAM_B1 * m_ref[...] + (1.0 - ADAM_B1) * gv
        nv = ADAM_B2 * v_ref[...] + (1.0 - ADAM_B2) * (gv * gv)
        d_ref[...] = -ADAM_LR * ((nm / c1) / (jnp.sqrt(nv / c2) + ADAM_EPS) + ADAM_WD * w_ref[...])
        nm_ref[...] = nm
        nv_ref[...] = nv

    spec = pl.BlockSpec((tr, c), lambda i: (i, 0))
    return pl.pallas_call(
        body,
        name=name,
        grid=(r // tr,),
        in_specs=[spec] * 4,
        out_specs=[spec] * 3,
        out_shape=[jax.ShapeDtypeStruct((r, c), F32)] * 3,
        compiler_params=_params("parallel"),
    )(w, g, m, v)


def reduce_to_shards(tag, grads, wire, blocked, place):
    recv = pair_exchange(tag + "_pair_exchange", grads)
    parts = [pair_sum(f"{tag}_pair_sum_{i}", g, r, place[1:], w) for i, (g, r, w) in enumerate(zip(grads, recv, wire))]
    got = chip_exchange(tag + "_chip_exchange", parts)
    sums = [chip_sum(f"{tag}_chip_sum_{i}", q, p, place, b) for i, (q, p, b) in enumerate(zip(got, parts, blocked))]
    return pair_gather(tag + "_pair_gather", sums, blocked)


def _pack(arrays, row_multiple, cols=BLOB_COLS):
    flat = jnp.concatenate([a.reshape(-1).astype(F32) for a in arrays])
    rows = -(-flat.shape[0] // cols)
    rows = -(-rows // row_multiple) * row_multiple
    return jnp.pad(flat, (0, rows * cols - flat.shape[0])).reshape(rows, cols)


def _unpack(blob, shapes):
    flat, out, off = blob.reshape(-1), [], 0
    for s in shapes:
        size = math.prod(s)
        out.append(flat[off:off + size].reshape(s))
        off += size
    return out


def _unpack_rows(blobs, shapes):
    out, off = [], 0
    for s in shapes:
        size = math.prod(s)
        out.append(blobs[:, off:off + size].reshape((blobs.shape[0],) + tuple(s)))
        off += size
    return out


def kernel(x, p, lru_w_in, lru_conv_w, lru_conv_b, lru_wa, lru_ba, lru_wx, lru_bx, lru_lambda, lru_w_out, pool_w_in, pool_w_grp, pool_b_grp, pool_scale, pool_w_out, ln_mix_g, ln_mix_b, mlp_w1, mlp_w2, ln_mlp_g, ln_mlp_b, ple_w, ple_gate_w, ple_gate_b, loss_target, m_lru_w_in, m_lru_conv_w, m_lru_conv_b, m_lru_wa, m_lru_ba, m_lru_wx, m_lru_bx, m_lru_lambda, m_lru_w_out, m_pool_w_in, m_pool_w_grp, m_pool_b_grp, m_pool_scale, m_pool_w_out, m_ln_mix_g, m_ln_mix_b, m_mlp_w1, m_mlp_w2, m_ln_mlp_g, m_ln_mlp_b, m_ple_w, m_ple_gate_w, m_ple_gate_b, v_lru_w_in, v_lru_conv_w, v_lru_conv_b, v_lru_wa, v_lru_ba, v_lru_wx, v_lru_bx, v_lru_lambda, v_lru_w_out, v_pool_w_in, v_pool_w_grp, v_pool_b_grp, v_pool_scale, v_pool_w_out, v_ln_mix_g, v_ln_mix_b, v_mlp_w1, v_mlp_w2, v_ln_mlp_g, v_ln_mlp_b, v_ple_w, v_ple_gate_w, v_ple_gate_b):
    weights = dict(lru_w_in=lru_w_in, lru_conv_w=lru_conv_w, lru_conv_b=lru_conv_b, lru_wa=lru_wa, lru_ba=lru_ba, lru_wx=lru_wx, lru_bx=lru_bx, lru_lambda=lru_lambda, lru_w_out=lru_w_out, pool_w_in=pool_w_in, pool_w_grp=pool_w_grp, pool_b_grp=pool_b_grp, pool_scale=pool_scale, pool_w_out=pool_w_out, ln_mix_g=ln_mix_g, ln_mix_b=ln_mix_b, mlp_w1=mlp_w1, mlp_w2=mlp_w2, ln_mlp_g=ln_mlp_g, ln_mlp_b=ln_mlp_b, ple_w=ple_w, ple_gate_w=ple_gate_w, ple_gate_b=ple_gate_b)
    mom_m = dict(lru_w_in=m_lru_w_in, lru_conv_w=m_lru_conv_w, lru_conv_b=m_lru_conv_b, lru_wa=m_lru_wa, lru_ba=m_lru_ba, lru_wx=m_lru_wx, lru_bx=m_lru_bx, lru_lambda=m_lru_lambda, lru_w_out=m_lru_w_out, pool_w_in=m_pool_w_in, pool_w_grp=m_pool_w_grp, pool_b_grp=m_pool_b_grp, pool_scale=m_pool_scale, pool_w_out=m_pool_w_out, ln_mix_g=m_ln_mix_g, ln_mix_b=m_ln_mix_b, mlp_w1=m_mlp_w1, mlp_w2=m_mlp_w2, ln_mlp_g=m_ln_mlp_g, ln_mlp_b=m_ln_mlp_b, ple_w=m_ple_w, ple_gate_w=m_ple_gate_w, ple_gate_b=m_ple_gate_b)
    mom_v = dict(lru_w_in=v_lru_w_in, lru_conv_w=v_lru_conv_w, lru_conv_b=v_lru_conv_b, lru_wa=v_lru_wa, lru_ba=v_lru_ba, lru_wx=v_lru_wx, lru_bx=v_lru_bx, lru_lambda=v_lru_lambda, lru_w_out=v_lru_w_out, pool_w_in=v_pool_w_in, pool_w_grp=v_pool_w_grp, pool_b_grp=v_pool_b_grp, pool_scale=v_pool_scale, pool_w_out=v_pool_w_out, ln_mix_g=v_ln_mix_g, ln_mix_b=v_ln_mix_b, mlp_w1=v_mlp_w1, mlp_w2=v_mlp_w2, ln_mlp_g=v_ln_mlp_g, ln_mlp_b=v_ln_mlp_b, ple_w=v_ple_w, ple_gate_w=v_ple_gate_w, ple_gate_b=v_ple_gate_b)
    names = list(weights)

    depth, d = ln_mix_g.shape
    t = x.shape[1]
    n_a, n_b = lru_w_in.shape[0], pool_w_in.shape[0]
    d_rnn = lru_w_out.shape[1] * N_CHIPS
    heads = d_rnn // LRU_BW
    d_ff = mlp_w1.shape[2] * N_CHIPS
    ple_dim = ple_w.shape[1]
    n_grp = len(POOL_WINDOWS)
    gw = d // n_grp
    alpha = (2 * depth) ** 0.25
    chip = 2 * lax.axis_index("x") + lax.axis_index("y")
    place = jnp.stack([chip, lax.axis_index("c")]).astype(jnp.int32)

    x2d = x.reshape(t, d)
    target = loss_target.reshape(t, d)
    p3 = p.reshape(depth, t, ple_dim)

    big = ["lru_w_in", "lru_w_out", "pool_w_in", "pool_w_out", "mlp_w1", "mlp_w2", "ple_w", "ple_gate_w", "pool_w_grp"]
    flat2 = lambda a: a.reshape(-1, a.shape[-1])
    small_sharded = ["lru_conv_w", "pool_b_grp", "pool_scale"]
    small_blob = _pack([weights[k] for k in small_sharded], 16, cols=256)
    every_layer = ("mlp_w1", "mlp_w2", "ple_w", "ple_gate_w")

    def layer_keys(i):
        return (["lru_w_in", "lru_w_out"] if i % 2 == 0 else ["pool_w_in", "pool_w_out", "pool_w_grp"]) + list(every_layer)

    def stage(k, i):
        w = weights[k]
        return into_block(f"stage_l{i}_{k}", flat2(w), i if k in every_layer else i // 2, math.prod(w.shape[1:-1]),
                          place[:1], BF16)

    staged = [[stage(k, i) for k in layer_keys(i)] for i in range(depth)]
    first = all_gather_chips("gather_l0", staged[0] + [into_block("stage_small", small_blob, 0, small_blob.shape[0], place[:1], F32)])
    wg = dict(zip([(k, 0) for k in layer_keys(0)], first[:-1]))
    conv_w_sh, b_grp_sh, scale_sh = _unpack_rows(first[-1].reshape(N_CHIPS, -1), [weights[k].shape for k in small_sharded])
    conv_w_full = jnp.moveaxis(conv_w_sh, 0, 2).reshape(n_a, CONV_WIDTH, d_rnn)
    b_grp_full = jnp.moveaxis(b_grp_sh, 0, 1).reshape(n_b, 1, d)
    scale_full = jnp.moveaxis(scale_sh, 0, 1).reshape(n_b, 1, d)
    rows_grp = gw // N_CHIPS
    w_grp_full = lambda i: jnp.moveaxis(wg["pool_w_grp", i].reshape(N_CHIPS, n_grp, rows_grp, gw), 0, 1).reshape(n_grp, gw, gw)
    wa_bf, wx_bf = lru_wa.astype(BF16), lru_wx.astype(BF16)
    row = lambda a, i: a[i].reshape(1, -1)

    saved = []
    cur, cur_bf = x2d, x2d.astype(BF16)
    for i in range(depth):
        slot = i // 2
        sv = dict(x_bf=cur_bf)
        deps = ()
        if i + 1 < depth:
            plan = gather_plan(len(staged[i + 1]))
            send_sems, recv_sems, flying, token = split_start(f"gather_l{i + 1}_start", plan, 3 * len(staged[i + 1]), staged[i + 1], cur)
            deps = (token,)
        if i % 2 == 0:
            (proj,) = matmul(f"l{i}_lru_in", plain(cur_bf), colsplit(wg["lru_w_in", i], 0, d), "nn",
                             [colsplit(None, 0, t, n=2, full=(2, t, d_rnn), dtype=F32)], deps=deps)
            hg = lru_fwd(f"l{i}_lru", proj, conv_w_full[slot], row(lru_conv_b, slot), wa_bf[slot], row(lru_ba, slot),
                         wx_bf[slot], row(lru_bx, slot), row(lru_lambda, slot))
            (mix,) = matmul(f"l{i}_lru_out", plain(hg), rowsplit(wg["lru_w_out", i], 0, d_rnn // N_CHIPS), "nn",
                            [plain(shape=(t, d), dtype=F32)])
            sv.update(proj=proj, act=hg)
        else:
            (u,) = matmul(f"l{i}_pool_in", plain(cur_bf), rowsplit(wg["pool_w_in", i], 0, d // N_CHIPS), "nn",
                          [plain(shape=(t, d), dtype=F32)], deps=deps)
            zs = pool_fwd(f"l{i}_pool", u, w_grp_full(i), b_grp_full[slot], scale_full[slot])
            (mix,) = matmul(f"l{i}_pool_out", plain(zs), rowsplit(wg["pool_w_out", i], 0, d // N_CHIPS), "nn",
                            [plain(shape=(t, d), dtype=F32)])
            sv.update(u=u, act=zs)
        x1, x1_bf, s1 = ln_fwd(f"l{i}_ln_mix", alpha, cur, mix, row(ln_mix_g, i), row(ln_mix_b, i))

        def relu2(acc):
            hr = jnp.maximum(acc, 0.0)
            return hr, hr * hr

        hr, hh = matmul(f"l{i}_mlp_up", plain(x1_bf), colsplit(wg["mlp_w1", i], 0, d), "nn",
                        [plain(shape=(t, d_ff), dtype=F32), plain(shape=(t, d_ff), dtype=BF16)], epilogue=relu2)
        (mlp,) = matmul(f"l{i}_mlp_down", plain(hh), rowsplit(wg["mlp_w2", i], 0, d_ff // N_CHIPS), "nn",
                        [plain(shape=(t, d), dtype=F32)])
        x2, x2_bf, s2 = ln_fwd(f"l{i}_ln_mlp", alpha, x1, mlp, row(ln_mlp_g, i), row(ln_mlp_b, i))
        (e,) = matmul(f"l{i}_ple", plain(p3[i]), colsplit(wg["ple_w", i], 0, ple_dim), "nn", [plain(shape=(t, d), dtype=F32)])

        def ple_out(acc, x2_t, e_t, gb):
            gate = jax.nn.sigmoid(acc + gb)
            x3 = x2_t + e_t * gate
            return x3, x3, gate

        cur, cur_bf, gate = matmul(f"l{i}_ple_gate", plain(x2_bf), rowsplit(wg["ple_gate_w", i], 0, d // N_CHIPS), "nn",
                                   [plain(shape=(t, d), dtype=F32), plain(shape=(t, d), dtype=BF16), plain(shape=(t, d), dtype=F32)],
                                   epilogue=ple_out, tiles=[plain(x2), plain(e)], rows=[row(ple_gate_b, i)])
        sv.update(s1=s1, x1_bf=x1_bf, hr=hr, hh=hh, s2=s2, x2_bf=x2_bf, gate=gate, e=e)
        saved.append(sv)
        if i + 1 < depth:
            landed = split_wait(f"gather_l{i + 1}_wait", plan, send_sems, recv_sems, flying, cur)
            wg.update(zip([(k, i + 1) for k in layer_keys(i + 1)], pair_forward(f"gather_l{i + 1}_forward", landed)))

    dy, loss_part = loss_head("loss", cur, target)
    loss = lax.psum(loss_part.reshape(()), ("x", "y", "c"))

    part = {}

    def grad_view(key, layer, split):
        w = weights[key]
        full = (N_CHIPS, w.shape[0] * w.shape[1], w.shape[2])
        return split(part.get(key), layer, w.shape[1], full=full, dtype=F32)

    small = {k: [None] * weights[k].shape[0] for k in names if k not in big or k == "pool_w_grp"}
    dcur = dy
    for i in reversed(range(depth)):
        slot = i // 2
        sv = saved[i]
        de, dpre, dgb = ple_bwd(f"l{i}_ple_bwd", dcur, sv["gate"], sv["e"])
        small["ple_gate_b"][i] = dgb
        (part["ple_w"],) = matmul(f"l{i}_d_ple_w", plain(p3[i]), plain(de), "tn", [grad_view("ple_w", i, colsplit)])
        (part["ple_gate_w"],) = matmul(f"l{i}_d_ple_gate_w", plain(sv["x2_bf"]), plain(dpre), "tn",
                                       [grad_view("ple_gate_w", i, rowsplit)])
        (dx2b,) = matmul(f"l{i}_dx2", plain(dpre), rowsplit(wg["ple_gate_w", i], 0, d // N_CHIPS), "nt",
                         [plain(shape=(t, d), dtype=F32)])
        ds2, ds2_bf, dg, db = ln_bwd(f"l{i}_ln_mlp_bwd", 1.0, dcur, dx2b, sv["s2"], row(ln_mlp_g, i))
        small["ln_mlp_g"][i], small["ln_mlp_b"][i] = dg, db
        (part["mlp_w2"],) = matmul(f"l{i}_d_mlp_w2", plain(sv["hh"]), plain(ds2_bf), "tn", [grad_view("mlp_w2", i, rowsplit)])
        (dhpre,) = matmul(f"l{i}_dh", plain(ds2_bf), rowsplit(wg["mlp_w2", i], 0, d_ff // N_CHIPS), "nt",
                          [plain(shape=(t, d_ff), dtype=BF16)], epilogue=lambda acc, hr_t: (acc * (2.0 * hr_t),),
                          tiles=[plain(sv["hr"])])
        (part["mlp_w1"],) = matmul(f"l{i}_d_mlp_w1", plain(sv["x1_bf"]), plain(dhpre), "tn", [grad_view("mlp_w1", i, colsplit)])
        (dx1b,) = matmul(f"l{i}_dx1", plain(dhpre), colsplit(wg["mlp_w1", i], 0, d), "nt", [plain(shape=(t, d), dtype=F32)])
        ds1, ds1_bf, dg, db = ln_bwd(f"l{i}_ln_mix_bwd", alpha, ds2, dx1b, sv["s1"], row(ln_mix_g, i))
        small["ln_mix_g"][i], small["ln_mix_b"][i] = dg, db
        residual = lambda acc, ds_t: (alpha * ds_t + acc,)
        if i % 2 == 0:
            (part["lru_w_out"],) = matmul(f"l{i}_d_lru_out", plain(sv["act"]), plain(ds1_bf), "tn",
                                          [grad_view("lru_w_out", slot, rowsplit)])
            (dhg,) = matmul(f"l{i}_dhg", plain(ds1_bf), rowsplit(wg["lru_w_out", i], 0, d_rnn // N_CHIPS), "nt",
                            [plain(shape=(t, d_rnn), dtype=F32)])
            dproj, dcw, dcb, dba, dbx, dlam, dwa, dwx = lru_bwd(
                f"l{i}_lru_bwd", sv["proj"], dhg, conv_w_full[slot], row(lru_conv_b, slot), wa_bf[slot], row(lru_ba, slot),
                wx_bf[slot], row(lru_bx, slot), row(lru_lambda, slot))
            for key, val in (("lru_conv_w", dcw), ("lru_conv_b", dcb), ("lru_ba", dba), ("lru_bx", dbx),
                             ("lru_lambda", dlam), ("lru_wa", dwa), ("lru_wx", dwx)):
                small[key][slot] = val
            dproj_v = colsplit(dproj, 0, t, n=2)
            (part["lru_w_in"],) = matmul(f"l{i}_d_lru_in", plain(sv["x_bf"]), dproj_v, "tn", [grad_view("lru_w_in", slot, colsplit)])
            (dcur,) = matmul(f"l{i}_dx", dproj_v, colsplit(wg["lru_w_in", i], 0, d), "nt",
                             [plain(shape=(t, d), dtype=F32)], epilogue=residual, tiles=[plain(ds1)])
        else:
            (part["pool_w_out"],) = matmul(f"l{i}_d_pool_out", plain(sv["act"]), plain(ds1_bf), "tn",
                                           [grad_view("pool_w_out", slot, rowsplit)])
            (dzs,) = matmul(f"l{i}_dzs", plain(ds1_bf), rowsplit(wg["pool_w_out", i], 0, d // N_CHIPS), "nt",
                            [plain(shape=(t, d), dtype=F32)])
            du, dwg, dbg, dsc = pool_bwd(f"l{i}_pool_bwd", sv["u"], dzs, w_grp_full(i), b_grp_full[slot], scale_full[slot])
            small["pool_w_grp"][slot], small["pool_b_grp"][slot], small["pool_scale"][slot] = dwg, dbg, dsc
            (part["pool_w_in"],) = matmul(f"l{i}_d_pool_in", plain(sv["x_bf"]), plain(du), "tn", [grad_view("pool_w_in", slot, rowsplit)])
            (dcur,) = matmul(f"l{i}_dx", plain(du), rowsplit(wg["pool_w_in", i], 0, d // N_CHIPS), "nt",
                             [plain(shape=(t, d), dtype=F32)], epilogue=residual, tiles=[plain(ds1)])
    grad_x = dcur.reshape(x.shape)

    big_w = [k for k in big if k != "pool_w_grp"]
    small_keys = [k for k in names if k not in big_w]
    small_full = [jnp.stack(small[k]).reshape((weights[k].shape[0],) + tuple(
        s * (N_CHIPS if ax in _sharded_axis(k) else 1) for ax, s in enumerate(weights[k].shape[1:], 1))) for k in small_keys]
    full_shapes = [a.shape for a in small_full]
    blob = _pack(small_full, 64)
    blob4 = blob.reshape(N_CHIPS, blob.shape[0] // N_CHIPS, BLOB_COLS)
    reduced = reduce_to_shards("grads", [part[k] for k in big_w] + [blob4], [BF16] * len(big_w) + [F32],
                               [False] * len(big_w) + [True], place)
    (blob_all,) = all_gather_chips("gather_small_grads", [reduced[-1]])
    small_grads = dict(zip(small_keys, _unpack(blob_all.reshape(blob.shape), full_shapes)))
    for k in small_keys:
        for ax in _sharded_axis(k):
            n = weights[k].shape[ax]
            small_grads[k] = lax.dynamic_slice_in_dim(small_grads[k], chip * n, n, axis=ax)
    grads = {k: reduced[j].reshape(weights[k].shape) for j, k in enumerate(big_w)}
    grads.update(small_grads)

    delta, new_m, new_v = {}, {}, {}
    for k in big_w:
        dl, nm, nv = adamw("adamw_" + k, flat2(weights[k]), flat2(grads[k]), flat2(mom_m[k]), flat2(mom_v[k]))
        delta[k], new_m[k], new_v[k] = (a.reshape(weights[k].shape) for a in (dl, nm, nv))
    shapes = [weights[k].shape for k in small_keys]
    dl, nm, nv = adamw("adamw_small", *[_pack([src[k] for k in small_keys], 8) for src in (weights, grads, mom_m, mom_v)])
    for out, blob_out in ((delta, dl), (new_m, nm), (new_v, nv)):
        out.update(zip(small_keys, _unpack(blob_out, shapes)))

    return (loss, grad_x, *[grads[k] for k in names], *[delta[k] for k in names],
            *[new_m[k] for k in names], *[new_v[k] for k in names])


def _sharded_axis(key):
    return {"lru_conv_w": (2,), "pool_w_grp": (2,), "pool_b_grp": (1,), "pool_scale": (1,)}.get(key, ())
```

```python
import functools
import math

import jax
import jax.numpy as jnp
from jax import lax
from jax.experimental import pallas as pl
from jax.experimental.pallas import tpu as pltpu

F32 = jnp.float32
BF16 = jnp.bfloat16

N_CHIPS = 4
LRU_BW = 128
LRU_C = 8.0
CONV_WIDTH = 4
POOL_WINDOWS = (2, 4, 8, 16)
POOL_HALO = 16
CONV_HALO = 8
LN_EPS = 1e-5
ADAM_LR = 0.001
ADAM_B1 = 0.9
ADAM_B2 = 0.999
ADAM_EPS = 1e-08
ADAM_WD = 0.01
ADAM_STEP = 10
GELU_C = math.sqrt(2.0 / math.pi)
GELU_K = 0.044715
VMEM_LIMIT_BYTES = 56 * 1024 * 1024
MESH = pl.DeviceIdType.MESH
BLOB_COLS = 1024


def _params(*sem):
    return pltpu.CompilerParams(dimension_semantics=tuple(sem), vmem_limit_bytes=VMEM_LIMIT_BYTES)


def _tile(unit, pref, align=128):
    if unit <= pref:
        return unit
    for d in range(2, unit + 1):
        if unit % d == 0 and unit // d <= pref and (unit // d) % align == 0:
            return unit // d
    raise ValueError((unit, pref, align))


class View:
    def __init__(self, arr, shape, row_unit, col_unit, block_fn, full=None, dtype=None):
        self.arr, self.shape, self.row_unit, self.col_unit, self.block_fn = arr, shape, row_unit, col_unit, block_fn
        self.full = full if full is not None else arr.shape
        self.dtype = dtype if dtype is not None else arr.dtype

    def spec(self, tr, tc, f):
        block, idx = self.block_fn(tr, tc)
        return pl.BlockSpec(block, lambda *g: idx(*f(*g)))


def plain(arr=None, shape=None, dtype=None):
    shape = arr.shape if arr is not None else shape
    return View(arr, shape, shape[0], shape[1], lambda tr, tc: ((tr, tc), lambda rt, ct: (rt, ct)), full=shape, dtype=dtype)


def colsplit(arr, layer, rows, n=N_CHIPS, full=None, dtype=None):
    full = arr.shape if arr is not None else full
    c = full[2]

    def block_fn(tr, tc):
        assert rows % tr == 0 and c % tc == 0, (rows, tr, c, tc)
        per, rpl = c // tc, rows // tr
        return (None, tr, tc), lambda rt, ct: (ct // per, layer * rpl + rt, ct % per)

    return View(arr, (rows, n * c), rows, c, block_fn, full=full, dtype=dtype)


def rowsplit(arr, layer, rows, n=N_CHIPS, full=None, dtype=None):
    full = arr.shape if arr is not None else full
    c = full[2]

    def block_fn(tr, tc):
        assert rows % tr == 0 and c % tc == 0, (rows, tr, c, tc)
        per = rows // tr
        return (None, tr, tc), lambda rt, ct: (rt // per, layer * per + rt % per, ct)

    return View(arr, (n * rows, c), rows, c, block_fn, full=full, dtype=dtype)


def matmul(name, a, b, mode, outs, epilogue=None, tiles=(), rows=(), deps=(), pm=1024, pn=1024, pk=1024):
    if mode == "nn":
        (m, k), (k2, n) = a.shape, b.shape
        um, uk, un = a.row_unit, min(a.col_unit, b.row_unit), b.col_unit
        dims = (((1,), (0,)), ((), ()))
    elif mode == "nt":
        (m, k), (n, k2) = a.shape, b.shape
        um, uk, un = a.row_unit, min(a.col_unit, b.col_unit), b.row_unit
        dims = (((1,), (1,)), ((), ()))
    else:
        (k, m), (k2, n) = a.shape, b.shape
        um, uk, un = a.col_unit, min(a.row_unit, b.row_unit), b.col_unit
        dims = (((0,), (0,)), ((), ()))
    assert k == k2, (name, a.shape, b.shape)
    for o in list(outs) + list(tiles):
        assert o.shape == (m, n), (name, o.shape, m, n)
        um, un = min(um, o.row_unit), min(un, o.col_unit)
    tm, tn, tk = _tile(um, pm), _tile(un, pn), _tile(uk, pk)
    assert m % tm == 0 and n % tn == 0 and k % tk == 0, (name, m, n, k, tm, tn, tk)
    gm, gn, gk = m // tm, n // tn, k // tk

    if mode == "nn":
        a_spec = a.spec(tm, tk, lambda i, j, kk: (i, kk))
        b_spec = b.spec(tk, tn, lambda i, j, kk: (kk, j))
    elif mode == "nt":
        a_spec = a.spec(tm, tk, lambda i, j, kk: (i, kk))
        b_spec = b.spec(tn, tk, lambda i, j, kk: (j, kk))
    else:
        a_spec = a.spec(tk, tm, lambda i, j, kk: (kk, i))
        b_spec = b.spec(tk, tn, lambda i, j, kk: (kk, j))
    tile_specs = [t.spec(tm, tn, lambda i, j, kk: (i, j)) for t in tiles]
    row_specs = [pl.BlockSpec((1, tn), lambda i, j, kk: (0, j)) for _ in rows]
    in_place = [o for o in outs if o.arr is not None]
    alias_specs = [pl.BlockSpec(memory_space=pl.ANY) for _ in in_place]
    out_specs = [o.spec(tm, tn, lambda i, j, kk: (i, j)) for o in outs]
    n_in = 2 + len(tiles) + len(rows)
    aliases = {}
    for o_idx, o in enumerate(outs):
        if o.arr is not None:
            aliases[n_in + in_place.index(o)] = o_idx
    n_t, n_r, n_a, n_o = len(tiles), len(rows), len(in_place) + len(deps), len(outs)
    dep_specs = [pl.BlockSpec(memory_space=pl.ANY) for _ in deps]

    def body(*refs):
        a_ref, b_ref = refs[0], refs[1]
        tile_refs = refs[2:2 + n_t]
        row_refs = refs[2 + n_t:2 + n_t + n_r]
        out_refs = refs[2 + n_t + n_r + n_a:2 + n_t + n_r + n_a + n_o]
        acc_ref = refs[-1] if gk > 1 else None

        def finish(acc):
            extra = [t[...] for t in tile_refs] + [r[...] for r in row_refs]
            res = epilogue(acc, *extra) if epilogue is not None else (acc,)
            for o_ref, r in zip(out_refs, res):
                o_ref[...] = r.astype(o_ref.dtype)

        prod = lax.dot_general(a_ref[...].astype(BF16), b_ref[...].astype(BF16), dims, preferred_element_type=F32)
        if gk == 1:
            finish(prod)
        else:
            kk = pl.program_id(2)

            @pl.when(kk == 0)
            def _():
                acc_ref[...] = prod

            @pl.when(kk > 0)
            def _():
                acc_ref[...] += prod

            @pl.when(kk == gk - 1)
            def _():
                finish(acc_ref[...])

    res = pl.pallas_call(
        body,
        name=name,
        grid=(gm, gn, gk),
        in_specs=[a_spec, b_spec] + tile_specs + row_specs + alias_specs + dep_specs,
        out_specs=out_specs,
        out_shape=[jax.ShapeDtypeStruct(o.full, o.dtype) for o in outs],
        scratch_shapes=[pltpu.VMEM((tm, tn), F32)] if gk > 1 else [],
        input_output_aliases=aliases,
        compiler_params=_params("parallel", "parallel", "arbitrary"),
    )(a.arr, b.arr, *[t.arr for t in tiles], *rows, *[o.arr for o in in_place], *deps)
    return res


def rows_call(name, fn, tiled, vecs, tiled_out, acc_out, tr=256):
    t = tiled[0].shape[0]
    tr = min(tr, t)
    assert t % tr == 0
    n1, n2, n3 = len(tiled), len(vecs), len(tiled_out)

    def body(*refs):
        fn(pl.program_id(0), refs[:n1], refs[n1:n1 + n2], refs[n1 + n2:n1 + n2 + n3], refs[n1 + n2 + n3:])

    return pl.pallas_call(
        body,
        name=name,
        grid=(t // tr,),
        in_specs=[pl.BlockSpec((tr, x.shape[1]), lambda i: (i, 0)) for x in tiled]
        + [pl.BlockSpec(v.shape, lambda i: (0, 0)) for v in vecs],
        out_specs=[pl.BlockSpec((tr, c), lambda i: (i, 0)) for c, _ in tiled_out]
        + [pl.BlockSpec(s, lambda i: (0, 0)) for s, _ in acc_out],
        out_shape=[jax.ShapeDtypeStruct((t, c), d) for c, d in tiled_out] + [jax.ShapeDtypeStruct(s, d) for s, d in acc_out],
        compiler_params=_params("arbitrary" if acc_out else "parallel"),
    )(*tiled, *vecs)


def _accumulate(step, ref, val):
    @pl.when(step == 0)
    def _():
        ref[...] = val

    @pl.when(step > 0)
    def _():
        ref[...] += val


def _ln_stats(s):
    mu = jnp.mean(s, axis=-1, keepdims=True)
    d = s - mu
    var = jnp.mean(d * d, axis=-1, keepdims=True)
    rstd = lax.rsqrt(var + LN_EPS)
    return d * rstd, rstd


def ln_fwd(name, alpha, x_in, m, g, b):
    d = x_in.shape[1]

    def fn(step, tiled, vecs, outs, accs):
        s = alpha * tiled[0][...] + tiled[1][...]
        xhat, _ = _ln_stats(s)
        y = xhat * vecs[0][...] + vecs[1][...]
        outs[0][...] = y
        outs[1][...] = y.astype(BF16)
        outs[2][...] = s

    return rows_call(name, fn, [x_in, m], [g, b], [(d, F32), (d, BF16), (d, F32)], [])


def ln_bwd(name, ca, da, db, s, g):
    d = s.shape[1]

    def fn(step, tiled, vecs, outs, accs):
        dx = ca * tiled[0][...] + tiled[1][...]
        xhat, rstd = _ln_stats(tiled[2][...])
        dxh = dx * vecs[0][...]
        ds = rstd * (dxh - jnp.mean(dxh, axis=-1, keepdims=True) - xhat * jnp.mean(dxh * xhat, axis=-1, keepdims=True))
        outs[0][...] = ds
        outs[1][...] = ds.astype(BF16)
        _accumulate(step, accs[0], jnp.sum(dx * xhat, axis=0, keepdims=True))
        _accumulate(step, accs[1], jnp.sum(dx, axis=0, keepdims=True))

    return rows_call(name, fn, [da, db, s], [g], [(d, F32), (d, BF16)], [((1, d), F32), ((1, d), F32)])


def ple_bwd(name, dx3, gate, e):
    d = dx3.shape[1]

    def fn(step, tiled, vecs, outs, accs):
        dx, gt, ev = tiled[0][...], tiled[1][...], tiled[2][...]
        dpre = dx * ev * gt * (1.0 - gt)
        outs[0][...] = (dx * gt).astype(BF16)
        outs[1][...] = dpre.astype(BF16)
        _accumulate(step, accs[0], jnp.sum(dpre, axis=0, keepdims=True))

    return rows_call(name, fn, [dx3, gate, e], [], [(d, BF16), (d, BF16)], [((1, d), F32)])


def loss_head(name, y, target):
    t, d = y.shape

    def fn(step, tiled, vecs, outs, accs):
        err = tiled[0][...] - tiled[1][...]
        outs[0][...] = err * (1.0 / d)
        part = jnp.sum(jnp.sum(err * err, axis=1, keepdims=True), axis=0, keepdims=True) * (0.5 / d)
        _accumulate(step, accs[0], part)

    return rows_call(name, fn, [y, target], [], [(d, F32)], [((1, 1), F32)])


def _softplus(z):
    return jnp.maximum(z, 0.0) + jnp.log1p(jnp.exp(-jnp.abs(z)))


def _gelu(y):
    th = jnp.tanh(GELU_C * (y + GELU_K * (y * y * y)))
    cdf = 0.5 * (1.0 + th)
    return y * cdf, cdf + 0.5 * y * (1.0 - th * th) * (GELU_C * (1.0 + 3.0 * GELU_K * y * y))


def _up(win, k):
    return pltpu.roll(win, win.shape[0] - k, 0)


def _down(win, k):
    return pltpu.roll(win, k, 0)


def _lru_gates(win, row0, cw_ref, cb, wa, ba, wx, bx, sp):
    h = CONV_HALO
    u = (cb + cw_ref[3:4, :] * win[h:] + cw_ref[2:3, :] * _down(win, 1)[h:]
         + cw_ref[1:2, :] * _down(win, 2)[h:] + cw_ref[0:1, :] * _down(win, 3)[h:])
    ub = u.astype(BF16)
    r = jax.nn.sigmoid(jnp.dot(ub, wa, preferred_element_type=F32) + ba)
    ig = jax.nn.sigmoid(jnp.dot(ub, wx, preferred_element_type=F32) + bx)
    log_a = (-LRU_C) * r * sp
    a = jnp.exp(log_a)
    mult = jnp.sqrt(-jnp.tanh(log_a) * (a * a + 1.0))
    first = (row0 + lax.broadcasted_iota(jnp.int32, u.shape, 0)) == 0
    mult = jnp.where(first, 1.0, mult)
    return u, r, ig, a, mult, first


def _block_scan(a, b, reverse):
    pos = lax.broadcasted_iota(jnp.int32, a.shape, 0) & 7
    shift, spread = (_up, _down) if reverse else (_down, _up)
    for s in (1, 2, 4):
        keep = (pos >= 8 - s) if reverse else (pos < s)
        b = jnp.where(keep, b, a * shift(b, s) + b)
        a = jnp.where(keep, a, a * shift(a, s))
    at, bt = a, b
    for s in (1, 2, 4):
        keep = (pos < s) if reverse else (pos >= 8 - s)
        at = jnp.where(keep, at, spread(at, s))
        bt = jnp.where(keep, bt, spread(bt, s))
    return a, b, at, bt


def _carry_scan(a_ref, b_ref, at_ref, bt_ref, out_ref, out_off, t, reverse):
    groups = t // 8

    def block(j, h_in):
        r0 = pl.multiple_of((groups - 1 - j if reverse else j) * 8, 8)
        out_ref[pl.ds(pl.multiple_of(out_off + r0, 8), 8), :] = a_ref[pl.ds(r0, 8), :] * h_in + b_ref[pl.ds(r0, 8), :]
        return at_ref[pl.ds(r0, 8), :] * h_in + bt_ref[pl.ds(r0, 8), :]

    lax.fori_loop(0, groups, block, jnp.zeros((8, LRU_BW), F32), unroll=8)


def _lru_in_specs(t, heads):
    blk = lambda i: (0, i)
    return [
        pl.BlockSpec((2, t, LRU_BW), lambda i: (0, 0, i)),
        pl.BlockSpec((CONV_WIDTH, LRU_BW), blk),
        pl.BlockSpec((1, LRU_BW), blk),
        pl.BlockSpec((None, LRU_BW, LRU_BW), lambda i: (i, 0, 0)),
        pl.BlockSpec((1, LRU_BW), blk),
        pl.BlockSpec((None, LRU_BW, LRU_BW), lambda i: (i, 0, 0)),
        pl.BlockSpec((1, LRU_BW), blk),
        pl.BlockSpec((1, LRU_BW), blk),
    ]


def lru_fwd(name, proj, conv_w, conv_b, wa, ba, wx, bx, lam):
    _, t, c = proj.shape
    heads = c // LRU_BW
    rc = min(256, t)

    def body(proj_ref, cw_ref, cb_ref, wa_ref, ba_ref, wx_ref, bx_ref, lam_ref, out_ref, upad, a_s, b_s, at_s, bt_s):
        upad[0:CONV_HALO, :] = jnp.zeros((CONV_HALO, LRU_BW), F32)
        upad[CONV_HALO:, :] = proj_ref[0]
        sp = _softplus(-lam_ref[...])
        cb, ba, bx, wa, wx = cb_ref[...], ba_ref[...], bx_ref[...], wa_ref[...], wx_ref[...]

        def gates(i, carry):
            r0 = pl.multiple_of(i * rc, rc)
            win = upad[pl.ds(r0, rc + CONV_HALO), :]
            u, r, ig, a, mult, _ = _lru_gates(win, r0, cw_ref, cb, wa, ba, wx, bx, sp)
            rows = pl.ds(r0, rc)
            a_s[rows, :], b_s[rows, :], at_s[rows, :], bt_s[rows, :] = _block_scan(a, mult * (ig * u), False)
            return carry

        lax.fori_loop(0, t // rc, gates, 0)
        _carry_scan(a_s, b_s, at_s, bt_s, b_s, 0, t, False)

        def gate_out(i, carry):
            r0 = pl.multiple_of(i * rc, rc)
            gy, _ = _gelu(proj_ref[1, pl.ds(r0, rc), :])
            out_ref[pl.ds(r0, rc), :] = (b_s[pl.ds(r0, rc), :] * gy).astype(BF16)
            return carry

        lax.fori_loop(0, t // rc, gate_out, 0)

    return pl.pallas_call(
        body,
        name=name,
        grid=(heads,),
        in_specs=_lru_in_specs(t, heads),
        out_specs=pl.BlockSpec((t, LRU_BW), lambda i: (0, i)),
        out_shape=jax.ShapeDtypeStruct((t, c), BF16),
        scratch_shapes=[pltpu.VMEM((t + CONV_HALO, LRU_BW), F32)] + [pltpu.VMEM((t, LRU_BW), F32)] * 4,
        compiler_params=_params("parallel"),
    )(proj, conv_w, conv_b, wa, ba, wx, bx, lam)


def lru_bwd(name, proj, dhg, conv_w, conv_b, wa, ba, wx, bx, lam):
    _, t, c = proj.shape
    heads = c // LRU_BW
    rc = min(256, t)
    h8 = CONV_HALO

    def body(proj_ref, dhg_ref, cw_ref, cb_ref, wa_ref, ba_ref, wx_ref, bx_ref, lam_ref,
             dproj_ref, dcw_ref, dcb_ref, dba_ref, dbx_ref, dlam_ref, dwa_ref, dwx_ref,
             upad, u_s, r_s, ig_s, apad, hpad, g_s, dupad, sa_s, sb_s, at_s, bt_s):
        zeros8 = jnp.zeros((h8, LRU_BW), F32)
        upad[0:h8, :] = zeros8
        upad[h8:, :] = proj_ref[0]
        hpad[0:h8, :] = zeros8
        apad[t:, :] = zeros8
        dupad[t:, :] = zeros8
        lam = lam_ref[...]
        sp = _softplus(-lam)
        cb, ba, bx, wa, wx = cb_ref[...], ba_ref[...], bx_ref[...], wa_ref[...], wx_ref[...]

        def gates(i, carry):
            r0 = pl.multiple_of(i * rc, rc)
            win = upad[pl.ds(r0, rc + h8), :]
            u, r, ig, a, mult, _ = _lru_gates(win, r0, cw_ref, cb, wa, ba, wx, bx, sp)
            u_s[pl.ds(r0, rc), :] = u
            r_s[pl.ds(r0, rc), :] = r
            ig_s[pl.ds(r0, rc), :] = ig
            rows = pl.ds(r0, rc)
            apad[rows, :] = a
            sa_s[rows, :], sb_s[rows, :], at_s[rows, :], bt_s[rows, :] = _block_scan(a, mult * (ig * u), False)
            return carry

        lax.fori_loop(0, t // rc, gates, 0)
        _carry_scan(sa_s, sb_s, at_s, bt_s, hpad, h8, t, False)

        def out_gate(i, carry):
            r0 = pl.multiple_of(i * rc, rc)
            gy, dgy = _gelu(proj_ref[1, pl.ds(r0, rc), :])
            dh = dhg_ref[pl.ds(r0, rc), :]
            hh = hpad[pl.ds(pl.multiple_of(r0 + h8, 8), rc), :]
            dproj_ref[1, pl.ds(r0, rc), :] = (dh * hh * dgy).astype(BF16)
            rows = pl.ds(r0, rc)
            a_next = _up(apad[pl.ds(r0, rc + h8), :], 1)[:rc]
            sa_s[rows, :], sb_s[rows, :], at_s[rows, :], bt_s[rows, :] = _block_scan(a_next, dh * gy, True)
            return carry

        lax.fori_loop(0, t // rc, out_gate, 0)
        _carry_scan(sa_s, sb_s, at_s, bt_s, g_s, 0, t, True)

        zrow = jnp.zeros((1, LRU_BW), F32)
        zmat = jnp.zeros((LRU_BW, LRU_BW), F32)

        def grads(i, carry):
            dsp, dba, dbx, dwa, dwx = carry
            r0 = pl.multiple_of(i * rc, rc)
            g = g_s[pl.ds(r0, rc), :]
            u, r, ig, a = u_s[pl.ds(r0, rc), :], r_s[pl.ds(r0, rc), :], ig_s[pl.ds(r0, rc), :], apad[pl.ds(r0, rc), :]
            hprev = _down(hpad[pl.ds(r0, rc + h8), :], 1)[h8:]
            first = (r0 + lax.broadcasted_iota(jnp.int32, u.shape, 0)) == 0
            log_a = (-LRU_C) * r * sp
            mult = jnp.where(first, 1.0, jnp.sqrt(-jnp.tanh(log_a) * (a * a + 1.0)))
            dmult = jnp.where(first, 0.0, g * (ig * u))
            dlog_a = g * hprev * a - dmult * (a * a) / mult
            dr = dlog_a * ((-LRU_C) * sp)
            dpre_r = dr * r * (1.0 - r)
            dpre_i = (g * mult * u) * ig * (1.0 - ig)
            pr, pi, ub = dpre_r.astype(BF16), dpre_i.astype(BF16), u.astype(BF16)
            nt = (((1,), (1,)), ((), ()))
            tn = (((0,), (0,)), ((), ()))
            du = (g * mult * ig + lax.dot_general(pr, wa, nt, preferred_element_type=F32)
                  + lax.dot_general(pi, wx, nt, preferred_element_type=F32))
            dupad[pl.ds(r0, rc), :] = du
            return (dsp + jnp.sum(dlog_a * ((-LRU_C) * r), axis=0, keepdims=True),
                    dba + jnp.sum(dpre_r, axis=0, keepdims=True),
                    dbx + jnp.sum(dpre_i, axis=0, keepdims=True),
                    dwa + lax.dot_general(ub, pr, tn, preferred_element_type=F32),
                    dwx + lax.dot_general(ub, pi, tn, preferred_element_type=F32))

        dsp, dba, dbx, dwa, dwx = lax.fori_loop(0, t // rc, grads, (zrow, zrow, zrow, zmat, zmat))
        dba_ref[...] = dba
        dbx_ref[...] = dbx
        dwa_ref[...] = dwa
        dwx_ref[...] = dwx
        dlam_ref[...] = -dsp * jax.nn.sigmoid(-lam)

        def conv_back(i, carry):
            dcb, d0, d1, d2, d3 = carry
            r0 = pl.multiple_of(i * rc, rc)
            dwin = dupad[pl.ds(r0, rc + h8), :]
            du = dwin[:rc]
            du0 = (cw_ref[3:4, :] * du + cw_ref[2:3, :] * _up(dwin, 1)[:rc]
                   + cw_ref[1:2, :] * _up(dwin, 2)[:rc] + cw_ref[0:1, :] * _up(dwin, 3)[:rc])
            dproj_ref[0, pl.ds(r0, rc), :] = du0.astype(BF16)
            win = upad[pl.ds(r0, rc + h8), :]
            red = lambda v: jnp.sum(v, axis=0, keepdims=True)
            return (dcb + red(du), d0 + red(du * _down(win, 3)[h8:]), d1 + red(du * _down(win, 2)[h8:]),
                    d2 + red(du * _down(win, 1)[h8:]), d3 + red(du * win[h8:]))

        dcb, d0, d1, d2, d3 = lax.fori_loop(0, t // rc, conv_back, (zrow,) * 5)
        dcb_ref[...] = dcb
        dcw_ref[0:1, :] = d0
        dcw_ref[1:2, :] = d1
        dcw_ref[2:3, :] = d2
        dcw_ref[3:4, :] = d3

    blk = lambda i: (0, i)
    vec = jax.ShapeDtypeStruct((1, c), F32)
    mat = jax.ShapeDtypeStruct((heads, LRU_BW, LRU_BW), F32)
    full = lambda: pltpu.VMEM((t, LRU_BW), F32)
    padded = lambda: pltpu.VMEM((t + h8, LRU_BW), F32)
    return pl.pallas_call(
        body,
        name=name,
        grid=(heads,),
        in_specs=_lru_in_specs(t, heads)[:1] + [pl.BlockSpec((t, LRU_BW), blk)] + _lru_in_specs(t, heads)[1:],
        out_specs=[pl.BlockSpec((2, t, LRU_BW), lambda i: (0, 0, i)), pl.BlockSpec((CONV_WIDTH, LRU_BW), blk)]
        + [pl.BlockSpec((1, LRU_BW), blk)] * 4 + [pl.BlockSpec((None, LRU_BW, LRU_BW), lambda i: (i, 0, 0))] * 2,
        out_shape=[jax.ShapeDtypeStruct((2, t, c), BF16), jax.ShapeDtypeStruct((CONV_WIDTH, c), F32), vec, vec, vec, vec, mat, mat],
        scratch_shapes=[padded(), full(), full(), full(), padded(), padded(), full(), padded()] + [full()] * 4,
        compiler_params=_params("parallel"),
    )(proj, dhg, conv_w, conv_b, wa, ba, wx, bx, lam)


def _pick_level(g, levels):
    out = levels[-1]
    for k in range(len(levels) - 2, -1, -1):
        out = jnp.where(g == k, levels[k], out)
    return out


def _pool_z(win, g, row0, rc):
    levels, cur = [], win
    for k in range(len(POOL_WINDOWS)):
        cur = cur + _down(cur, 1 << k)
        levels.append(cur[POOL_HALO:])
    tot = _pick_level(g, levels)
    width = jnp.left_shift(2, g)
    row = row0 + lax.broadcasted_iota(jnp.int32, tot.shape, 0)
    cnt = jnp.minimum(row + 1, width).astype(F32)
    return tot / cnt - win[POOL_HALO:], cnt


def _pool_specs(t, gw):
    blk = lambda g: (0, g)
    return [pl.BlockSpec((t, gw), blk), pl.BlockSpec((None, gw, gw), lambda g: (g, 0, 0)),
            pl.BlockSpec((1, gw), blk), pl.BlockSpec((1, gw), blk)]


def pool_fwd(name, u, w_grp, b_grp, scale):
    t, d = u.shape
    gw = d // len(POOL_WINDOWS)
    rc = min(256, t)

    def body(u_ref, wg_ref, bg_ref, sc_ref, out_ref, upad):
        g = pl.program_id(0)
        upad[0:POOL_HALO, :] = jnp.zeros((POOL_HALO, gw), F32)
        upad[POOL_HALO:, :] = u_ref[...]
        wg, bg, sc = wg_ref[...], bg_ref[...], sc_ref[...]

        def chunk(i, carry):
            r0 = pl.multiple_of(i * rc, rc)
            z, _ = _pool_z(upad[pl.ds(r0, rc + POOL_HALO), :], g, r0, rc)
            z2 = jnp.dot(z.astype(BF16), wg, preferred_element_type=F32) + bg
            out_ref[pl.ds(r0, rc), :] = (z2 * sc).astype(BF16)
            return carry

        lax.fori_loop(0, t // rc, chunk, 0)

    return pl.pallas_call(
        body,
        name=name,
        grid=(len(POOL_WINDOWS),),
        in_specs=_pool_specs(t, gw),
        out_specs=pl.BlockSpec((t, gw), lambda g: (0, g)),
        out_shape=jax.ShapeDtypeStruct((t, d), BF16),
        scratch_shapes=[pltpu.VMEM((t + POOL_HALO, gw), F32)],
        compiler_params=_params("parallel"),
    )(u, w_grp, b_grp, scale)


def pool_bwd(name, u, dzs, w_grp, b_grp, scale):
    t, d = u.shape
    gw = d // len(POOL_WINDOWS)
    rc = min(256, t)

    def body(u_ref, dzs_ref, wg_ref, bg_ref, sc_ref, du_ref, dwg_ref, dbg_ref, dsc_ref, upad, qpad, dz_s):
        g = pl.program_id(0)
        upad[0:POOL_HALO, :] = jnp.zeros((POOL_HALO, gw), F32)
        upad[POOL_HALO:, :] = u_ref[...]
        qpad[t:, :] = jnp.zeros((POOL_HALO, gw), F32)
        wg, bg, sc = wg_ref[...], bg_ref[...], sc_ref[...]
        zrow = jnp.zeros((1, gw), F32)

        def chunk(i, carry):
            dsc, dbg, dwg = carry
            r0 = pl.multiple_of(i * rc, rc)
            z, cnt = _pool_z(upad[pl.ds(r0, rc + POOL_HALO), :], g, r0, rc)
            zb = z.astype(BF16)
            z2 = jnp.dot(zb, wg, preferred_element_type=F32) + bg
            dzs = dzs_ref[pl.ds(r0, rc), :]
            dz2 = dzs * sc
            d2b = dz2.astype(BF16)
            dz = lax.dot_general(d2b, wg, (((1,), (1,)), ((), ())), preferred_element_type=F32)
            dz_s[pl.ds(r0, rc), :] = dz
            qpad[pl.ds(r0, rc), :] = dz / cnt
            return (dsc + jnp.sum(dzs * z2, axis=0, keepdims=True), dbg + jnp.sum(dz2, axis=0, keepdims=True),
                    dwg + lax.dot_general(zb, d2b, (((0,), (0,)), ((), ())), preferred_element_type=F32))

        dsc, dbg, dwg = lax.fori_loop(0, t // rc, chunk, (zrow, zrow, jnp.zeros((gw, gw), F32)))
        dsc_ref[...] = dsc
        dbg_ref[...] = dbg
        dwg_ref[...] = dwg

        def spread(i, carry):
            r0 = pl.multiple_of(i * rc, rc)
            levels, cur = [], qpad[pl.ds(r0, rc + POOL_HALO), :]
            for k in range(len(POOL_WINDOWS)):
                cur = cur + _up(cur, 1 << k)
                levels.append(cur[:rc])
            du_ref[pl.ds(r0, rc), :] = (_pick_level(g, levels) - dz_s[pl.ds(r0, rc), :]).astype(BF16)
            return carry

        lax.fori_loop(0, t // rc, spread, 0)

    blk = lambda g: (0, g)
    vec = jax.ShapeDtypeStruct((1, d), F32)
    return pl.pallas_call(
        body,
        name=name,
        grid=(len(POOL_WINDOWS),),
        in_specs=_pool_specs(t, gw)[:1] + [pl.BlockSpec((t, gw), blk)] + _pool_specs(t, gw)[1:],
        out_specs=[pl.BlockSpec((t, gw), blk), pl.BlockSpec((None, gw, gw), lambda g: (g, 0, 0)),
                   pl.BlockSpec((1, gw), blk), pl.BlockSpec((1, gw), blk)],
        out_shape=[jax.ShapeDtypeStruct((t, d), BF16), jax.ShapeDtypeStruct((len(POOL_WINDOWS), gw, gw), F32), vec, vec],
        scratch_shapes=[pltpu.VMEM((t + POOL_HALO, gw), F32), pltpu.VMEM((t + POOL_HALO, gw), F32), pltpu.VMEM((t, gw), F32)],
        compiler_params=_params("parallel"),
    )(u, dzs, w_grp, b_grp, scale)


def _place():
    return lax.axis_index("x"), lax.axis_index("y"), lax.axis_index("c")


def _other_chips(x, y):
    return [(1 - x, y), (x, 1 - y), (1 - x, 1 - y)]


def _half(c, rows):
    h = rows // 2
    return pl.ds(pl.multiple_of(c * h, 8), h)


_ANY = pl.BlockSpec(memory_space=pl.ANY)


def into_block(name, shards, layer, r, me, dtype):
    c = shards.shape[1]
    tr = _tile(r, 512, 16)
    per = r // tr

    def body(me_ref, s_ref, o_ref):
        o_ref[...] = s_ref[...].astype(o_ref.dtype)

    return pl.pallas_call(
        body,
        name=name,
        grid_spec=pltpu.PrefetchScalarGridSpec(
            num_scalar_prefetch=1,
            grid=(per,),
            in_specs=[pl.BlockSpec((tr, c), lambda i, me_ref: (layer * per + i, 0))],
            out_specs=pl.BlockSpec((None, tr, c), lambda i, me_ref: (me_ref[0], i, 0)),
        ),
        out_shape=jax.ShapeDtypeStruct((N_CHIPS, r, c), dtype),
        compiler_params=_params("parallel"),
    )(me, shards)


_HBM = pl.BlockSpec(memory_space=pltpu.HBM)
_SEM = pl.BlockSpec(memory_space=pltpu.SEMAPHORE)


def _in_hbm(a):
    return pltpu.with_memory_space_constraint(a, pltpu.HBM)


def split_start(name, plan, n_copies, bufs, dep):
    n = len(bufs)

    def body(*refs):
        for cp in plan(refs[:n], refs[n + 1], refs[n + 2]):
            cp.start()
        refs[-1][...] = jnp.zeros_like(refs[-1])

    res = pl.pallas_call(
        body,
        name=name,
        in_specs=[_HBM] * n + [_ANY],
        out_specs=[_SEM, _SEM] + [_HBM] * n + [pl.BlockSpec(memory_space=pltpu.VMEM)],
        out_shape=[pltpu.SemaphoreType.DMA((n_copies,)), pltpu.SemaphoreType.DMA((n_copies,))]
        + [pltpu.HBM(b.shape, b.dtype) for b in bufs] + [jax.ShapeDtypeStruct((8, 128), F32)],
        input_output_aliases={i: 2 + i for i in range(n)},
        compiler_params=pltpu.CompilerParams(has_side_effects=pltpu.SideEffectType.DATAFLOW_SIDE_EFFECTING),
    )(*[_in_hbm(b) for b in bufs], dep)
    return res[0], res[1], list(res[2:2 + n]), res[-1]


def split_wait(name, plan, send_sems, recv_sems, bufs, after):
    n = len(bufs)

    def body(*refs):
        copies = plan(refs[:n], refs[n], refs[n + 1])
        for cp in copies:
            cp.wait_send()
        for cp in copies:
            cp.wait_recv()

    return pl.pallas_call(
        body,
        name=name,
        in_specs=[_HBM] * n + [_SEM, _SEM, _ANY],
        out_specs=[_HBM] * n,
        out_shape=[pltpu.HBM(b.shape, b.dtype) for b in bufs],
        input_output_aliases={i: i for i in range(n)},
        compiler_params=pltpu.CompilerParams(has_side_effects=pltpu.SideEffectType.DATAFLOW_SIDE_EFFECTING),
    )(*bufs, send_sems, recv_sems, after)


def gather_plan(n):
    def plan(bufs, send_sems, recv_sems):
        x, y, c = _place()
        copies = []
        for i in range(n):
            blk = bufs[i].at[2 * x + y, _half(c, bufs[i].shape[1]), :]
            for j, chip in enumerate(_other_chips(x, y)):
                copies.append(pltpu.make_async_remote_copy(
                    src_ref=blk, dst_ref=blk, send_sem=send_sems.at[3 * i + j], recv_sem=recv_sems.at[3 * i + j],
                    device_id=(*chip, c), device_id_type=MESH))
        return copies

    return plan


def pair_forward(name, bufs):
    n = len(bufs)

    def body(*refs):
        outs = refs[n:2 * n]
        send_sems, recv_sems = refs[2 * n:]
        x, y, c = _place()
        copies = []
        for i in range(n):
            for j, (cx, cy) in enumerate(_other_chips(x, y)):
                blk = outs[i].at[2 * cx + cy, _half(c, outs[i].shape[1]), :]
                copies.append(pltpu.make_async_remote_copy(
                    src_ref=blk, dst_ref=blk, send_sem=send_sems.at[3 * i + j], recv_sem=recv_sems.at[3 * i + j],
                    device_id=(x, y, 1 - c), device_id_type=MESH))
        for cp in copies:
            cp.start()
        for cp in copies:
            cp.wait()

    return pl.pallas_call(
        body,
        name=name,
        in_specs=[_ANY] * n,
        out_specs=[_ANY] * n,
        out_shape=[jax.ShapeDtypeStruct(b.shape, b.dtype) for b in bufs],
        input_output_aliases={i: i for i in range(n)},
        scratch_shapes=[pltpu.SemaphoreType.DMA((3 * n,)), pltpu.SemaphoreType.DMA((3 * n,))],
    )(*bufs)


def all_gather_chips(name, bufs):
    n = len(bufs)

    def body(*refs):
        outs = refs[n:2 * n]
        send_sems, recv_sems = refs[2 * n:]
        x, y, c = _place()
        me, sibling = 2 * x + y, (x, y, 1 - c)
        chips = _other_chips(x, y)

        def copy(i, slot, block, half, to):
            blk = outs[i].at[block, _half(half, outs[i].shape[1]), :]
            return pltpu.make_async_remote_copy(
                src_ref=blk, dst_ref=blk, send_sem=send_sems.at[i * 6 + slot], recv_sem=recv_sems.at[i * 6 + slot],
                device_id=to, device_id_type=MESH)

        first = [copy(i, j, me, c, (*chip, c)) for i in range(n) for j, chip in enumerate(chips)]
        for cp in first:
            cp.start()
        passed = []
        for i in range(n):
            for j, (cx, cy) in enumerate(chips):
                copy(i, j, 2 * cx + cy, c, (x, y, c)).wait_recv()
                fwd = copy(i, 3 + j, 2 * cx + cy, c, sibling)
                fwd.start()
                passed.append(fwd)
        for i in range(n):
            for j, (cx, cy) in enumerate(chips):
                copy(i, 3 + j, 2 * cx + cy, 1 - c, (x, y, c)).wait_recv()
        for cp in first + passed:
            cp.wait_send()

    return pl.pallas_call(
        body,
        name=name,
        in_specs=[_ANY] * n,
        out_specs=[_ANY] * n,
        out_shape=[jax.ShapeDtypeStruct(b.shape, b.dtype) for b in bufs],
        input_output_aliases={i: i for i in range(n)},
        scratch_shapes=[pltpu.SemaphoreType.DMA((6 * n,)), pltpu.SemaphoreType.DMA((6 * n,))],
    )(*bufs)


def pair_plan(n):
    def plan(bufs, send_sems, recv_sems):
        x, y, c = _place()
        return [pltpu.make_async_remote_copy(
            src_ref=bufs[i].at[:, _half(1 - c, bufs[i].shape[1]), :], dst_ref=bufs[n + i], send_sem=send_sems.at[i],
            recv_sem=recv_sems.at[i], device_id=(x, y, 1 - c), device_id_type=MESH) for i in range(n)]

    return plan


def chip_plan(n):
    def plan(bufs, send_sems, recv_sems):
        x, y, c = _place()
        copies = []
        for i in range(n):
            for j, (cx, cy) in enumerate(_other_chips(x, y)):
                copies.append(pltpu.make_async_remote_copy(
                    src_ref=bufs[i].at[2 * cx + cy], dst_ref=bufs[n + i].at[2 * x + y], send_sem=send_sems.at[3 * i + j],
                    recv_sem=recv_sems.at[3 * i + j], device_id=(cx, cy, c), device_id_type=MESH))
        return copies

    return plan


def exchange(name, plan, n_copies, srcs, land_shapes):
    n = len(srcs)

    def body(*refs):
        copies = plan(refs[:2 * n], refs[2 * n], refs[2 * n + 1])
        for cp in copies:
            cp.start()
        for cp in copies:
            cp.wait()

    return pl.pallas_call(
        body,
        name=name,
        in_specs=[_ANY] * n,
        out_specs=[_ANY] * n,
        out_shape=land_shapes,
        scratch_shapes=[pltpu.SemaphoreType.DMA((n_copies,)), pltpu.SemaphoreType.DMA((n_copies,))],
    )(*srcs)


def pair_lands(grads):
    return [jax.ShapeDtypeStruct((g.shape[0], g.shape[1] // 2, g.shape[2]), g.dtype) for g in grads]


def pair_exchange(name, grads):
    return exchange(name, pair_plan(len(grads)), len(grads), grads, pair_lands(grads))


def chip_exchange(name, parts):
    return exchange(name, chip_plan(len(parts)), 3 * len(parts), parts, [jax.ShapeDtypeStruct(p.shape, p.dtype) for p in parts])


def pair_gather(name, bufs, blocked, layers):
    n = len(bufs)
    n_copies = sum(layers)

    def body(*refs):
        outs = refs[n:2 * n]
        send_sems, recv_sems = refs[2 * n:]
        x, y, c = _place()
        copies = []
        for i in range(n):
            buf = outs[i].at[2 * x + y] if blocked[i] else outs[i]
            r = buf.shape[0] // layers[i]
            for l in range(layers[i]):
                mine = buf.at[pl.ds(pl.multiple_of(l * r + c * (r // 2), 8), r // 2), :]
                copies.append(pltpu.make_async_remote_copy(
                    src_ref=mine, dst_ref=mine, send_sem=send_sems.at[len(copies)], recv_sem=recv_sems.at[len(copies)],
                    device_id=(x, y, 1 - c), device_id_type=MESH))
        for cp in copies:
            cp.start()
        for cp in copies:
            cp.wait()

    return pl.pallas_call(
        body,
        name=name,
        in_specs=[_ANY] * n,
        out_specs=[_ANY] * n,
        out_shape=[jax.ShapeDtypeStruct(b.shape, b.dtype) for b in bufs],
        input_output_aliases={i: i for i in range(n)},
        scratch_shapes=[pltpu.SemaphoreType.DMA((n_copies,)), pltpu.SemaphoreType.DMA((n_copies,))],
    )(*bufs)


def pair_sum(name, grad, recv, core, dtype):
    _, r, c = grad.shape
    h = r // 2
    th = _tile(h, 512, 16)
    per = h // th

    def body(core_ref, g_ref, r_ref, o_ref):
        o_ref[...] = (g_ref[...] + r_ref[...]).astype(o_ref.dtype)

    return pl.pallas_call(
        body,
        name=name,
        grid_spec=pltpu.PrefetchScalarGridSpec(
            num_scalar_prefetch=1,
            grid=(N_CHIPS, per),
            in_specs=[pl.BlockSpec((None, th, c), lambda k, i, core_ref: (k, core_ref[0] * per + i, 0)),
                      pl.BlockSpec((None, th, c), lambda k, i, core_ref: (k, i, 0))],
            out_specs=pl.BlockSpec((None, th, c), lambda k, i, core_ref: (k, i, 0)),
        ),
        out_shape=jax.ShapeDtypeStruct((N_CHIPS, h, c), dtype),
        compiler_params=_params("parallel", "parallel"),
    )(core, grad, recv)


def chip_sum(name, got, parts, place, blocked, into=None, layer=0, n_layers=1):
    _, h, c = parts.shape
    th = _tile(h, 256, 16)
    per = h // th

    def body(place_ref, q0, q1, q2, q3, p_ref, *rest):
        o_ref = rest[-1]
        me = place_ref[0]
        own = p_ref[...].astype(F32)
        v = [jnp.where(me == k, own, q[...].astype(F32)) for k, q in enumerate((q0, q1, q2, q3))]
        o_ref[...] = ((v[0] + v[1]) + v[2]) + v[3]

    def got_spec(k):
        return pl.BlockSpec((None, th, c), lambda i, pr: (jnp.where(pr[0] == k, (k + 1) % N_CHIPS, k), i, 0))

    if blocked:
        out_spec = pl.BlockSpec((None, th, c), lambda i, pr: (pr[0], pr[1] * per + i, 0))
        out_shape = jax.ShapeDtypeStruct((N_CHIPS, 2 * h, c), F32)
    else:
        out_spec = pl.BlockSpec((th, c), lambda i, pr: ((2 * layer + pr[1]) * per + i, 0))
        out_shape = jax.ShapeDtypeStruct((n_layers * 2 * h, c), F32)
    carried = [] if into is None else [into]
    return pl.pallas_call(
        body,
        name=name,
        grid_spec=pltpu.PrefetchScalarGridSpec(
            num_scalar_prefetch=1,
            grid=(per,),
            in_specs=[got_spec(k) for k in range(N_CHIPS)] + [pl.BlockSpec((None, th, c), lambda i, pr: (pr[0], i, 0))]
            + [_ANY] * len(carried),
            out_specs=out_spec,
        ),
        out_shape=out_shape,
        input_output_aliases={6: 0} if carried else {},
        compiler_params=_params("parallel"),
    )(place, got, got, got, got, parts, *carried)


def adamw(name, w, g, m, v):
    r, c = w.shape
    tr = _tile(r, 512, 8)
    c1 = 1.0 - ADAM_B1 ** ADAM_STEP
    c2 = 1.0 - ADAM_B2 ** ADAM_STEP

    def body(w_ref, g_ref, m_ref, v_ref, d_ref, nm_ref, nv_ref):
        gv = g_ref[...]
        nm = ADAM_B1 * m_ref[...] + (1.0 - ADAM_B1) * gv
        nv = ADAM_B2 * v_ref[...] + (1.0 - ADAM_B2) * (gv * gv)
        d_ref[...] = -ADAM_LR * ((nm / c1) / (jnp.sqrt(nv / c2) + ADAM_EPS) + ADAM_WD * w_ref[...])
        nm_ref[...] = nm
        nv_ref[...] = nv

    spec = pl.BlockSpec((tr, c), lambda i: (i, 0))
    return pl.pallas_call(
        body,
        name=name,
        grid=(r // tr,),
        in_specs=[spec] * 4,
        out_specs=[spec] * 3,
        out_shape=[jax.ShapeDtypeStruct((r, c), F32)] * 3,
        compiler_params=_params("parallel"),
    )(w, g, m, v)


def reduce_to_shards(tag, grads, wire, blocked, place):
    recv = pair_exchange(tag + "_pair_exchange", grads)
    parts = [pair_sum(f"{tag}_pair_sum_{i}", g, r, place[1:], w) for i, (g, r, w) in enumerate(zip(grads, recv, wire))]
    got = chip_exchange(tag + "_chip_exchange", parts)
    sums = [chip_sum(f"{tag}_chip_sum_{i}", q, p, place, b) for i, (q, p, b) in enumerate(zip(got, parts, blocked))]
    return pair_gather(tag + "_pair_gather", sums, blocked, [1] * len(sums))


def _pack(arrays, row_multiple, cols=BLOB_COLS):
    flat = jnp.concatenate([a.reshape(-1).astype(F32) for a in arrays])
    rows = -(-flat.shape[0] // cols)
    rows = -(-rows // row_multiple) * row_multiple
    return jnp.pad(flat, (0, rows * cols - flat.shape[0])).reshape(rows, cols)


def _unpack(blob, shapes):
    flat, out, off = blob.reshape(-1), [], 0
    for s in shapes:
        size = math.prod(s)
        out.append(flat[off:off + size].reshape(s))
        off += size
    return out


def _unpack_rows(blobs, shapes):
    out, off = [], 0
    for s in shapes:
        size = math.prod(s)
        out.append(blobs[:, off:off + size].reshape((blobs.shape[0],) + tuple(s)))
        off += size
    return out


def kernel(x, p, lru_w_in, lru_conv_w, lru_conv_b, lru_wa, lru_ba, lru_wx, lru_bx, lru_lambda, lru_w_out, pool_w_in, pool_w_grp, pool_b_grp, pool_scale, pool_w_out, ln_mix_g, ln_mix_b, mlp_w1, mlp_w2, ln_mlp_g, ln_mlp_b, ple_w, ple_gate_w, ple_gate_b, loss_target, m_lru_w_in, m_lru_conv_w, m_lru_conv_b, m_lru_wa, m_lru_ba, m_lru_wx, m_lru_bx, m_lru_lambda, m_lru_w_out, m_pool_w_in, m_pool_w_grp, m_pool_b_grp, m_pool_scale, m_pool_w_out, m_ln_mix_g, m_ln_mix_b, m_mlp_w1, m_mlp_w2, m_ln_mlp_g, m_ln_mlp_b, m_ple_w, m_ple_gate_w, m_ple_gate_b, v_lru_w_in, v_lru_conv_w, v_lru_conv_b, v_lru_wa, v_lru_ba, v_lru_wx, v_lru_bx, v_lru_lambda, v_lru_w_out, v_pool_w_in, v_pool_w_grp, v_pool_b_grp, v_pool_scale, v_pool_w_out, v_ln_mix_g, v_ln_mix_b, v_mlp_w1, v_mlp_w2, v_ln_mlp_g, v_ln_mlp_b, v_ple_w, v_ple_gate_w, v_ple_gate_b):
    weights = dict(lru_w_in=lru_w_in, lru_conv_w=lru_conv_w, lru_conv_b=lru_conv_b, lru_wa=lru_wa, lru_ba=lru_ba, lru_wx=lru_wx, lru_bx=lru_bx, lru_lambda=lru_lambda, lru_w_out=lru_w_out, pool_w_in=pool_w_in, pool_w_grp=pool_w_grp, pool_b_grp=pool_b_grp, pool_scale=pool_scale, pool_w_out=pool_w_out, ln_mix_g=ln_mix_g, ln_mix_b=ln_mix_b, mlp_w1=mlp_w1, mlp_w2=mlp_w2, ln_mlp_g=ln_mlp_g, ln_mlp_b=ln_mlp_b, ple_w=ple_w, ple_gate_w=ple_gate_w, ple_gate_b=ple_gate_b)
    mom_m = dict(lru_w_in=m_lru_w_in, lru_conv_w=m_lru_conv_w, lru_conv_b=m_lru_conv_b, lru_wa=m_lru_wa, lru_ba=m_lru_ba, lru_wx=m_lru_wx, lru_bx=m_lru_bx, lru_lambda=m_lru_lambda, lru_w_out=m_lru_w_out, pool_w_in=m_pool_w_in, pool_w_grp=m_pool_w_grp, pool_b_grp=m_pool_b_grp, pool_scale=m_pool_scale, pool_w_out=m_pool_w_out, ln_mix_g=m_ln_mix_g, ln_mix_b=m_ln_mix_b, mlp_w1=m_mlp_w1, mlp_w2=m_mlp_w2, ln_mlp_g=m_ln_mlp_g, ln_mlp_b=m_ln_mlp_b, ple_w=m_ple_w, ple_gate_w=m_ple_gate_w, ple_gate_b=m_ple_gate_b)
    mom_v = dict(lru_w_in=v_lru_w_in, lru_conv_w=v_lru_conv_w, lru_conv_b=v_lru_conv_b, lru_wa=v_lru_wa, lru_ba=v_lru_ba, lru_wx=v_lru_wx, lru_bx=v_lru_bx, lru_lambda=v_lru_lambda, lru_w_out=v_lru_w_out, pool_w_in=v_pool_w_in, pool_w_grp=v_pool_w_grp, pool_b_grp=v_pool_b_grp, pool_scale=v_pool_scale, pool_w_out=v_pool_w_out, ln_mix_g=v_ln_mix_g, ln_mix_b=v_ln_mix_b, mlp_w1=v_mlp_w1, mlp_w2=v_mlp_w2, ln_mlp_g=v_ln_mlp_g, ln_mlp_b=v_ln_mlp_b, ple_w=v_ple_w, ple_gate_w=v_ple_gate_w, ple_gate_b=v_ple_gate_b)
    names = list(weights)

    depth, d = ln_mix_g.shape
    t = x.shape[1]
    n_a, n_b = lru_w_in.shape[0], pool_w_in.shape[0]
    d_rnn = lru_w_out.shape[1] * N_CHIPS
    heads = d_rnn // LRU_BW
    d_ff = mlp_w1.shape[2] * N_CHIPS
    ple_dim = ple_w.shape[1]
    n_grp = len(POOL_WINDOWS)
    gw = d // n_grp
    alpha = (2 * depth) ** 0.25
    chip = 2 * lax.axis_index("x") + lax.axis_index("y")
    place = jnp.stack([chip, lax.axis_index("c")]).astype(jnp.int32)

    x2d = x.reshape(t, d)
    target = loss_target.reshape(t, d)
    p3 = p.reshape(depth, t, ple_dim)

    big = ["lru_w_in", "lru_w_out", "pool_w_in", "pool_w_out", "mlp_w1", "mlp_w2", "ple_w", "ple_gate_w", "pool_w_grp"]
    flat2 = lambda a: a.reshape(-1, a.shape[-1])
    small_sharded = ["lru_conv_w", "pool_b_grp", "pool_scale"]
    small_blob = _pack([weights[k] for k in small_sharded], 16, cols=256)
    every_layer = ("mlp_w1", "mlp_w2", "ple_w", "ple_gate_w")

    def layer_keys(i):
        return (["lru_w_in", "lru_w_out"] if i % 2 == 0 else ["pool_w_in", "pool_w_out", "pool_w_grp"]) + list(every_layer)

    def stage(k, i):
        w = weights[k]
        return into_block(f"stage_l{i}_{k}", flat2(w), i if k in every_layer else i // 2, math.prod(w.shape[1:-1]),
                          place[:1], BF16)

    staged = [[stage(k, i) for k in layer_keys(i)] for i in range(depth)]
    first = all_gather_chips("gather_l0", staged[0] + [into_block("stage_small", small_blob, 0, small_blob.shape[0], place[:1], F32)])
    wg = dict(zip([(k, 0) for k in layer_keys(0)], first[:-1]))
    conv_w_sh, b_grp_sh, scale_sh = _unpack_rows(first[-1].reshape(N_CHIPS, -1), [weights[k].shape for k in small_sharded])
    conv_w_full = jnp.moveaxis(conv_w_sh, 0, 2).reshape(n_a, CONV_WIDTH, d_rnn)
    b_grp_full = jnp.moveaxis(b_grp_sh, 0, 1).reshape(n_b, 1, d)
    scale_full = jnp.moveaxis(scale_sh, 0, 1).reshape(n_b, 1, d)
    rows_grp = gw // N_CHIPS
    w_grp_full = lambda i: jnp.moveaxis(wg["pool_w_grp", i].reshape(N_CHIPS, n_grp, rows_grp, gw), 0, 1).reshape(n_grp, gw, gw)
    wa_bf, wx_bf = lru_wa.astype(BF16), lru_wx.astype(BF16)
    row = lambda a, i: a[i].reshape(1, -1)

    saved = []
    cur, cur_bf = x2d, x2d.astype(BF16)
    for i in range(depth):
        slot = i // 2
        sv = dict(x_bf=cur_bf)
        deps = ()
        if i + 1 < depth:
            plan = gather_plan(len(staged[i + 1]))
            send_sems, recv_sems, flying, token = split_start(f"gather_l{i + 1}_start", plan, 3 * len(staged[i + 1]), staged[i + 1], cur)
            deps = (token,)
        if i % 2 == 0:
            (proj,) = matmul(f"l{i}_lru_in", plain(cur_bf), colsplit(wg["lru_w_in", i], 0, d), "nn",
                             [colsplit(None, 0, t, n=2, full=(2, t, d_rnn), dtype=F32)], deps=deps)
            hg = lru_fwd(f"l{i}_lru", proj, conv_w_full[slot], row(lru_conv_b, slot), wa_bf[slot], row(lru_ba, slot),
                         wx_bf[slot], row(lru_bx, slot), row(lru_lambda, slot))
            (mix,) = matmul(f"l{i}_lru_out", plain(hg), rowsplit(wg["lru_w_out", i], 0, d_rnn // N_CHIPS), "nn",
                            [plain(shape=(t, d), dtype=F32)])
            sv.update(proj=proj, act=hg)
        else:
            (u,) = matmul(f"l{i}_pool_in", plain(cur_bf), rowsplit(wg["pool_w_in", i], 0, d // N_CHIPS), "nn",
                          [plain(shape=(t, d), dtype=F32)], deps=deps)
            zs = pool_fwd(f"l{i}_pool", u, w_grp_full(i), b_grp_full[slot], scale_full[slot])
            (mix,) = matmul(f"l{i}_pool_out", plain(zs), rowsplit(wg["pool_w_out", i], 0, d // N_CHIPS), "nn",
                            [plain(shape=(t, d), dtype=F32)])
            sv.update(u=u, act=zs)
        x1, x1_bf, s1 = ln_fwd(f"l{i}_ln_mix", alpha, cur, mix, row(ln_mix_g, i), row(ln_mix_b, i))

        def relu2(acc):
            hr = jnp.maximum(acc, 0.0)
            return hr, hr * hr

        hr, hh = matmul(f"l{i}_mlp_up", plain(x1_bf), colsplit(wg["mlp_w1", i], 0, d), "nn",
                        [plain(shape=(t, d_ff), dtype=F32), plain(shape=(t, d_ff), dtype=BF16)], epilogue=relu2)
        (mlp,) = matmul(f"l{i}_mlp_down", plain(hh), rowsplit(wg["mlp_w2", i], 0, d_ff // N_CHIPS), "nn",
                        [plain(shape=(t, d), dtype=F32)])
        x2, x2_bf, s2 = ln_fwd(f"l{i}_ln_mlp", alpha, x1, mlp, row(ln_mlp_g, i), row(ln_mlp_b, i))
        (e,) = matmul(f"l{i}_ple", plain(p3[i]), colsplit(wg["ple_w", i], 0, ple_dim), "nn", [plain(shape=(t, d), dtype=F32)])

        def ple_out(acc, x2_t, e_t, gb):
            gate = jax.nn.sigmoid(acc + gb)
            x3 = x2_t + e_t * gate
            return x3, x3, gate

        cur, cur_bf, gate = matmul(f"l{i}_ple_gate", plain(x2_bf), rowsplit(wg["ple_gate_w", i], 0, d // N_CHIPS), "nn",
                                   [plain(shape=(t, d), dtype=F32), plain(shape=(t, d), dtype=BF16), plain(shape=(t, d), dtype=F32)],
                                   epilogue=ple_out, tiles=[plain(x2), plain(e)], rows=[row(ple_gate_b, i)])
        sv.update(s1=s1, x1_bf=x1_bf, hr=hr, hh=hh, s2=s2, x2_bf=x2_bf, gate=gate, e=e)
        saved.append(sv)
        if i + 1 < depth:
            landed = split_wait(f"gather_l{i + 1}_wait", plan, send_sems, recv_sems, flying, cur)
            wg.update(zip([(k, i + 1) for k in layer_keys(i + 1)], pair_forward(f"gather_l{i + 1}_forward", landed)))

    dy, loss_part = loss_head("loss", cur, target)
    loss = lax.psum(loss_part.reshape(()), ("x", "y", "c"))

    part = {}
    sums = {}

    def grad_view(key, split):
        w = weights[key]
        return split(None, 0, w.shape[1], full=(N_CHIPS, w.shape[1], w.shape[2]), dtype=F32)

    def reduced_keys(i):
        return [k for k in layer_keys(i) if k != "pool_w_grp"]

    def pair_start(i, dep):
        srcs = [part[k, i] for k in reduced_keys(i)]
        plan = pair_plan(len(srcs))
        lands = [lax.empty(s.shape, s.dtype) for s in pair_lands(srcs)]
        return (plan,) + split_start(f"grads_l{i}_pair_start", plan, len(srcs), srcs + lands, dep)

    def pair_end_chip_start(i, flight, after):
        plan, send_sems, recv_sems, bufs, _ = flight
        bufs = split_wait(f"grads_l{i}_pair_wait", plan, send_sems, recv_sems, bufs, after)
        n = len(bufs) // 2
        parts = [pair_sum(f"grads_l{i}_pair_sum_{k}", bufs[j], bufs[n + j], place[1:], BF16) for j, k in enumerate(reduced_keys(i))]
        plan = chip_plan(n)
        return (plan,) + split_start(f"grads_l{i}_chip_start", plan, 3 * n, parts + [lax.empty(q.shape, q.dtype) for q in parts], after)

    def chip_end(i, flight, after):
        plan, send_sems, recv_sems, bufs, _ = flight
        bufs = split_wait(f"grads_l{i}_chip_wait", plan, send_sems, recv_sems, bufs, after)
        n = len(bufs) // 2
        for j, k in enumerate(reduced_keys(i)):
            sums[k] = chip_sum(f"grads_l{i}_chip_sum_{k}", bufs[n + j], bufs[j], place, False, into=sums.get(k),
                               layer=i if k in every_layer else i // 2, n_layers=weights[k].shape[0])

    small = {k: [None] * weights[k].shape[0] for k in names if k not in big or k == "pool_w_grp"}
    dcur = dy
    pair_flight = chip_flight = None
    for i in reversed(range(depth)):
        slot = i // 2
        sv = saved[i]
        de, dpre, dgb = ple_bwd(f"l{i}_ple_bwd", dcur, sv["gate"], sv["e"])
        small["ple_gate_b"][i] = dgb
        (part["ple_w", i],) = matmul(f"l{i}_d_ple_w", plain(p3[i]), plain(de), "tn", [grad_view("ple_w", colsplit)],
                                     deps=() if pair_flight is None else (pair_flight[-1],))
        (part["ple_gate_w", i],) = matmul(f"l{i}_d_ple_gate_w", plain(sv["x2_bf"]), plain(dpre), "tn",
                                          [grad_view("ple_gate_w", rowsplit)])
        (dx2b,) = matmul(f"l{i}_dx2", plain(dpre), rowsplit(wg["ple_gate_w", i], 0, d // N_CHIPS), "nt",
                         [plain(shape=(t, d), dtype=F32)])
        if pair_flight is not None:
            chip_flight = pair_end_chip_start(i + 1, pair_flight, dx2b)
        ds2, ds2_bf, dg, db = ln_bwd(f"l{i}_ln_mlp_bwd", 1.0, dcur, dx2b, sv["s2"], row(ln_mlp_g, i))
        small["ln_mlp_g"][i], small["ln_mlp_b"][i] = dg, db
        (part["mlp_w2", i],) = matmul(f"l{i}_d_mlp_w2", plain(sv["hh"]), plain(ds2_bf), "tn", [grad_view("mlp_w2", rowsplit)])
        (dhpre,) = matmul(f"l{i}_dh", plain(ds2_bf), rowsplit(wg["mlp_w2", i], 0, d_ff // N_CHIPS), "nt",
                          [plain(shape=(t, d_ff), dtype=BF16)], epilogue=lambda acc, hr_t: (acc * (2.0 * hr_t),),
                          tiles=[plain(sv["hr"])], deps=() if chip_flight is None else (chip_flight[-1],))
        (part["mlp_w1", i],) = matmul(f"l{i}_d_mlp_w1", plain(sv["x1_bf"]), plain(dhpre), "tn", [grad_view("mlp_w1", colsplit)])
        (dx1b,) = matmul(f"l{i}_dx1", plain(dhpre), colsplit(wg["mlp_w1", i], 0, d), "nt", [plain(shape=(t, d), dtype=F32)])
        ds1, ds1_bf, dg, db = ln_bwd(f"l{i}_ln_mix_bwd", alpha, ds2, dx1b, sv["s1"], row(ln_mix_g, i))
        small["ln_mix_g"][i], small["ln_mix_b"][i] = dg, db
        residual = lambda acc, ds_t: (alpha * ds_t + acc,)
        if i % 2 == 0:
            (part["lru_w_out", i],) = matmul(f"l{i}_d_lru_out", plain(sv["act"]), plain(ds1_bf), "tn",
                                             [grad_view("lru_w_out", rowsplit)])
            (dhg,) = matmul(f"l{i}_dhg", plain(ds1_bf), rowsplit(wg["lru_w_out", i], 0, d_rnn // N_CHIPS), "nt",
                            [plain(shape=(t, d_rnn), dtype=F32)])
            dproj, dcw, dcb, dba, dbx, dlam, dwa, dwx = lru_bwd(
                f"l{i}_lru_bwd", sv["proj"], dhg, conv_w_full[slot], row(lru_conv_b, slot), wa_bf[slot], row(lru_ba, slot),
                wx_bf[slot], row(lru_bx, slot), row(lru_lambda, slot))
            for key, val in (("lru_conv_w", dcw), ("lru_conv_b", dcb), ("lru_ba", dba), ("lru_bx", dbx),
                             ("lru_lambda", dlam), ("lru_wa", dwa), ("lru_wx", dwx)):
                small[key][slot] = val
            dproj_v = colsplit(dproj, 0, t, n=2)
            (part["lru_w_in", i],) = matmul(f"l{i}_d_lru_in", plain(sv["x_bf"]), dproj_v, "tn", [grad_view("lru_w_in", colsplit)])
            (dcur,) = matmul(f"l{i}_dx", dproj_v, colsplit(wg["lru_w_in", i], 0, d), "nt",
                             [plain(shape=(t, d), dtype=F32)], epilogue=residual, tiles=[plain(ds1)])
        else:
            (part["pool_w_out", i],) = matmul(f"l{i}_d_pool_out", plain(sv["act"]), plain(ds1_bf), "tn",
                                              [grad_view("pool_w_out", rowsplit)])
            (dzs,) = matmul(f"l{i}_dzs", plain(ds1_bf), rowsplit(wg["pool_w_out", i], 0, d // N_CHIPS), "nt",
                            [plain(shape=(t, d), dtype=F32)])
            du, dwg, dbg, dsc = pool_bwd(f"l{i}_pool_bwd", sv["u"], dzs, w_grp_full(i), b_grp_full[slot], scale_full[slot])
            small["pool_w_grp"][slot], small["pool_b_grp"][slot], small["pool_scale"][slot] = dwg, dbg, dsc
            (part["pool_w_in", i],) = matmul(f"l{i}_d_pool_in", plain(sv["x_bf"]), plain(du), "tn", [grad_view("pool_w_in", rowsplit)])
            (dcur,) = matmul(f"l{i}_dx", plain(du), rowsplit(wg["pool_w_in", i], 0, d // N_CHIPS), "nt",
                             [plain(shape=(t, d), dtype=F32)], epilogue=residual, tiles=[plain(ds1)])
        if chip_flight is not None:
            chip_end(i + 1, chip_flight, dcur)
        pair_flight = pair_start(i, dcur)
    grad_x = dcur.reshape(x.shape)
    chip_flight = pair_end_chip_start(0, pair_flight, dcur)

    big_w = [k for k in big if k != "pool_w_grp"]
    small_keys = [k for k in names if k not in big_w]
    small_full = [jnp.stack(small[k]).reshape((weights[k].shape[0],) + tuple(
        s * (N_CHIPS if ax in _sharded_axis(k) else 1) for ax, s in enumerate(weights[k].shape[1:], 1))) for k in small_keys]
    full_shapes = [a.shape for a in small_full]
    blob = _pack(small_full, 64)
    blob4 = blob.reshape(N_CHIPS, blob.shape[0] // N_CHIPS, BLOB_COLS)
    (blob_mine,) = reduce_to_shards("small", [blob4], [F32], [True], place)
    (blob_all,) = all_gather_chips("gather_small_grads", [blob_mine])
    small_grads = dict(zip(small_keys, _unpack(blob_all.reshape(blob.shape), full_shapes)))
    for k in small_keys:
        for ax in _sharded_axis(k):
            n = weights[k].shape[ax]
            small_grads[k] = lax.dynamic_slice_in_dim(small_grads[k], chip * n, n, axis=ax)
    chip_end(0, chip_flight, blob_all)
    reduced = pair_gather("grads_pair_gather", [sums[k] for k in big_w], [False] * len(big_w), [weights[k].shape[0] for k in big_w])
    grads = {k: reduced[j].reshape(weights[k].shape) for j, k in enumerate(big_w)}
    grads.update(small_grads)

    delta, new_m, new_v = {}, {}, {}
    for k in big_w:
        dl, nm, nv = adamw("adamw_" + k, flat2(weights[k]), flat2(grads[k]), flat2(mom_m[k]), flat2(mom_v[k]))
        delta[k], new_m[k], new_v[k] = (a.reshape(weights[k].shape) for a in (dl, nm, nv))
    shapes = [weights[k].shape for k in small_keys]
    dl, nm, nv = adamw("adamw_small", *[_pack([src[k] for k in small_keys], 8) for src in (weights, grads, mom_m, mom_v)])
    for out, blob_out in ((delta, dl), (new_m, nm), (new_v, nv)):
        out.update(zip(small_keys, _unpack(blob_out, shapes)))

    return (loss, grad_x, *[grads[k] for k in names], *[delta[k] for k in names],
            *[new_m[k] for k in names], *[new_v[k] for k in names])


def _sharded_axis(key):
    return {"lru_conv_w": (2,), "pool_w_grp": (2,), "pool_b_grp": (1,), "pool_scale": (1,)}.get(key, ())
```

```python
import functools
import math

import jax
import jax.numpy as jnp
from jax import lax
from jax.experimental import pallas as pl
from jax.experimental.pallas import tpu as pltpu

F32 = jnp.float32
BF16 = jnp.bfloat16

N_CHIPS = 4
LRU_BW = 128
LRU_C = 8.0
CONV_WIDTH = 4
POOL_WINDOWS = (2, 4, 8, 16)
POOL_HALO = 16
CONV_HALO = 8
LN_EPS = 1e-5
ADAM_LR = 0.001
ADAM_B1 = 0.9
ADAM_B2 = 0.999
ADAM_EPS = 1e-08
ADAM_WD = 0.01
ADAM_STEP = 10
GELU_C = math.sqrt(2.0 / math.pi)
GELU_K = 0.044715
VMEM_LIMIT_BYTES = 56 * 1024 * 1024
MESH = pl.DeviceIdType.MESH
BLOB_COLS = 1024


def _params(*sem):
    return pltpu.CompilerParams(dimension_semantics=tuple(sem), vmem_limit_bytes=VMEM_LIMIT_BYTES)


def _tile(unit, pref, align=128):
    if unit <= pref:
        return unit
    for d in range(2, unit + 1):
        if unit % d == 0 and unit // d <= pref and (unit // d) % align == 0:
            return unit // d
    raise ValueError((unit, pref, align))


class View:
    def __init__(self, arr, shape, row_unit, col_unit, block_fn, full=None, dtype=None):
        self.arr, self.shape, self.row_unit, self.col_unit, self.block_fn = arr, shape, row_unit, col_unit, block_fn
        self.full = full if full is not None else arr.shape
        self.dtype = dtype if dtype is not None else arr.dtype

    def spec(self, tr, tc, f):
        block, idx = self.block_fn(tr, tc)
        return pl.BlockSpec(block, lambda *g: idx(*f(*g)))


def plain(arr=None, shape=None, dtype=None):
    shape = arr.shape if arr is not None else shape
    return View(arr, shape, shape[0], shape[1], lambda tr, tc: ((tr, tc), lambda rt, ct: (rt, ct)), full=shape, dtype=dtype)


def colsplit(arr, layer, rows, n=N_CHIPS, full=None, dtype=None):
    full = arr.shape if arr is not None else full
    c = full[2]

    def block_fn(tr, tc):
        assert rows % tr == 0 and c % tc == 0, (rows, tr, c, tc)
        per, rpl = c // tc, rows // tr
        return (None, tr, tc), lambda rt, ct: (ct // per, layer * rpl + rt, ct % per)

    return View(arr, (rows, n * c), rows, c, block_fn, full=full, dtype=dtype)


def rowsplit(arr, layer, rows, n=N_CHIPS, full=None, dtype=None):
    full = arr.shape if arr is not None else full
    c = full[2]

    def block_fn(tr, tc):
        assert rows % tr == 0 and c % tc == 0, (rows, tr, c, tc)
        per = rows // tr
        return (None, tr, tc), lambda rt, ct: (rt // per, layer * per + rt % per, ct)

    return View(arr, (n * rows, c), rows, c, block_fn, full=full, dtype=dtype)


def matmul(name, a, b, mode, outs, epilogue=None, tiles=(), rows=(), deps=(), pm=1024, pn=1024, pk=1024):
    if mode == "nn":
        (m, k), (k2, n) = a.shape, b.shape
        um, uk, un = a.row_unit, min(a.col_unit, b.row_unit), b.col_unit
        dims = (((1,), (0,)), ((), ()))
    elif mode == "nt":
        (m, k), (n, k2) = a.shape, b.shape
        um, uk, un = a.row_unit, min(a.col_unit, b.col_unit), b.row_unit
        dims = (((1,), (1,)), ((), ()))
    else:
        (k, m), (k2, n) = a.shape, b.shape
        um, uk, un = a.col_unit, min(a.row_unit, b.row_unit), b.col_unit
        dims = (((0,), (0,)), ((), ()))
    assert k == k2, (name, a.shape, b.shape)
    for o in list(outs) + list(tiles):
        assert o.shape == (m, n), (name, o.shape, m, n)
        um, un = min(um, o.row_unit), min(un, o.col_unit)
    tm, tn, tk = _tile(um, pm), _tile(un, pn), _tile(uk, pk)
    assert m % tm == 0 and n % tn == 0 and k % tk == 0, (name, m, n, k, tm, tn, tk)
    gm, gn, gk = m // tm, n // tn, k // tk

    if mode == "nn":
        a_spec = a.spec(tm, tk, lambda i, j, kk: (i, kk))
        b_spec = b.spec(tk, tn, lambda i, j, kk: (kk, j))
    elif mode == "nt":
        a_spec = a.spec(tm, tk, lambda i, j, kk: (i, kk))
        b_spec = b.spec(tn, tk, lambda i, j, kk: (j, kk))
    else:
        a_spec = a.spec(tk, tm, lambda i, j, kk: (kk, i))
        b_spec = b.spec(tk, tn, lambda i, j, kk: (kk, j))
    tile_specs = [t.spec(tm, tn, lambda i, j, kk: (i, j)) for t in tiles]
    row_specs = [pl.BlockSpec((1, tn), lambda i, j, kk: (0, j)) for _ in rows]
    in_place = [o for o in outs if o.arr is not None]
    alias_specs = [pl.BlockSpec(memory_space=pl.ANY) for _ in in_place]
    out_specs = [o.spec(tm, tn, lambda i, j, kk: (i, j)) for o in outs]
    n_in = 2 + len(tiles) + len(rows)
    aliases = {}
    for o_idx, o in enumerate(outs):
        if o.arr is not None:
            aliases[n_in + in_place.index(o)] = o_idx
    n_t, n_r, n_a, n_o = len(tiles), len(rows), len(in_place) + len(deps), len(outs)
    dep_specs = [pl.BlockSpec(memory_space=pl.ANY) for _ in deps]

    def body(*refs):
        a_ref, b_ref = refs[0], refs[1]
        tile_refs = refs[2:2 + n_t]
        row_refs = refs[2 + n_t:2 + n_t + n_r]
        out_refs = refs[2 + n_t + n_r + n_a:2 + n_t + n_r + n_a + n_o]
        acc_ref = refs[-1] if gk > 1 else None

        def finish(acc):
            extra = [t[...] for t in tile_refs] + [r[...] for r in row_refs]
            res = epilogue(acc, *extra) if epilogue is not None else (acc,)
            for o_ref, r in zip(out_refs, res):
                o_ref[...] = r.astype(o_ref.dtype)

        prod = lax.dot_general(a_ref[...].astype(BF16), b_ref[...].astype(BF16), dims, preferred_element_type=F32)
        if gk == 1:
            finish(prod)
        else:
            kk = pl.program_id(2)

            @pl.when(kk == 0)
            def _():
                acc_ref[...] = prod

            @pl.when(kk > 0)
            def _():
                acc_ref[...] += prod

            @pl.when(kk == gk - 1)
            def _():
                finish(acc_ref[...])

    res = pl.pallas_call(
        body,
        name=name,
        grid=(gm, gn, gk),
        in_specs=[a_spec, b_spec] + tile_specs + row_specs + alias_specs + dep_specs,
        out_specs=out_specs,
        out_shape=[jax.ShapeDtypeStruct(o.full, o.dtype) for o in outs],
        scratch_shapes=[pltpu.VMEM((tm, tn), F32)] if gk > 1 else [],
        input_output_aliases=aliases,
        compiler_params=_params("parallel", "parallel", "arbitrary"),
    )(a.arr, b.arr, *[t.arr for t in tiles], *rows, *[o.arr for o in in_place], *deps)
    return res


def rows_call(name, fn, tiled, vecs, tiled_out, acc_out, tr=256):
    t = tiled[0].shape[0]
    tr = min(tr, t)
    assert t % tr == 0
    n1, n2, n3 = len(tiled), len(vecs), len(tiled_out)

    def body(*refs):
        fn(pl.program_id(0), refs[:n1], refs[n1:n1 + n2], refs[n1 + n2:n1 + n2 + n3], refs[n1 + n2 + n3:])

    return pl.pallas_call(
        body,
        name=name,
        grid=(t // tr,),
        in_specs=[pl.BlockSpec((tr, x.shape[1]), lambda i: (i, 0)) for x in tiled]
        + [pl.BlockSpec(v.shape, lambda i: (0, 0)) for v in vecs],
        out_specs=[pl.BlockSpec((tr, c), lambda i: (i, 0)) for c, _ in tiled_out]
        + [pl.BlockSpec(s, lambda i: (0, 0)) for s, _ in acc_out],
        out_shape=[jax.ShapeDtypeStruct((t, c), d) for c, d in tiled_out] + [jax.ShapeDtypeStruct(s, d) for s, d in acc_out],
        compiler_params=_params("arbitrary" if acc_out else "parallel"),
    )(*tiled, *vecs)


def _accumulate(step, ref, val):
    @pl.when(step == 0)
    def _():
        ref[...] = val

    @pl.when(step > 0)
    def _():
        ref[...] += val


def _ln_stats(s):
    mu = jnp.mean(s, axis=-1, keepdims=True)
    d = s - mu
    var = jnp.mean(d * d, axis=-1, keepdims=True)
    rstd = lax.rsqrt(var + LN_EPS)
    return d * rstd, rstd


def ln_fwd(name, alpha, x_in, m, g, b):
    d = x_in.shape[1]

    def fn(step, tiled, vecs, outs, accs):
        s = alpha * tiled[0][...] + tiled[1][...]
        xhat, _ = _ln_stats(s)
        y = xhat * vecs[0][...] + vecs[1][...]
        outs[0][...] = y
        outs[1][...] = y.astype(BF16)
        outs[2][...] = s

    return rows_call(name, fn, [x_in, m], [g, b], [(d, F32), (d, BF16), (d, F32)], [])


def ln_bwd(name, ca, da, db, s, g):
    d = s.shape[1]

    def fn(step, tiled, vecs, outs, accs):
        dx = ca * tiled[0][...] + tiled[1][...]
        xhat, rstd = _ln_stats(tiled[2][...])
        dxh = dx * vecs[0][...]
        ds = rstd * (dxh - jnp.mean(dxh, axis=-1, keepdims=True) - xhat * jnp.mean(dxh * xhat, axis=-1, keepdims=True))
        outs[0][...] = ds
        outs[1][...] = ds.astype(BF16)
        _accumulate(step, accs[0], jnp.sum(dx * xhat, axis=0, keepdims=True))
        _accumulate(step, accs[1], jnp.sum(dx, axis=0, keepdims=True))

    return rows_call(name, fn, [da, db, s], [g], [(d, F32), (d, BF16)], [((1, d), F32), ((1, d), F32)])


def ple_bwd(name, dx3, gate, e):
    d = dx3.shape[1]

    def fn(step, tiled, vecs, outs, accs):
        dx, gt, ev = tiled[0][...], tiled[1][...], tiled[2][...]
        dpre = dx * ev * gt * (1.0 - gt)
        outs[0][...] = (dx * gt).astype(BF16)
        outs[1][...] = dpre.astype(BF16)
        _accumulate(step, accs[0], jnp.sum(dpre, axis=0, keepdims=True))

    return rows_call(name, fn, [dx3, gate, e], [], [(d, BF16), (d, BF16)], [((1, d), F32)])


def loss_head(name, y, target):
    t, d = y.shape

    def fn(step, tiled, vecs, outs, accs):
        err = tiled[0][...] - tiled[1][...]
        outs[0][...] = err * (1.0 / d)
        part = jnp.sum(jnp.sum(err * err, axis=1, keepdims=True), axis=0, keepdims=True) * (0.5 / d)
        _accumulate(step, accs[0], part)

    return rows_call(name, fn, [y, target], [], [(d, F32)], [((1, 1), F32)])


def _softplus(z):
    return jnp.maximum(z, 0.0) + jnp.log1p(jnp.exp(-jnp.abs(z)))


def _gelu(y):
    th = jnp.tanh(GELU_C * (y + GELU_K * (y * y * y)))
    cdf = 0.5 * (1.0 + th)
    return y * cdf, cdf + 0.5 * y * (1.0 - th * th) * (GELU_C * (1.0 + 3.0 * GELU_K * y * y))


def _up(win, k):
    return pltpu.roll(win, win.shape[0] - k, 0)


def _down(win, k):
    return pltpu.roll(win, k, 0)


def _lru_gates(win, row0, cw_ref, cb, wa, ba, wx, bx, sp):
    h = CONV_HALO
    u = (cb + cw_ref[3:4, :] * win[h:] + cw_ref[2:3, :] * _down(win, 1)[h:]
         + cw_ref[1:2, :] * _down(win, 2)[h:] + cw_ref[0:1, :] * _down(win, 3)[h:])
    ub = u.astype(BF16)
    r = jax.nn.sigmoid(jnp.dot(ub, wa, preferred_element_type=F32) + ba)
    ig = jax.nn.sigmoid(jnp.dot(ub, wx, preferred_element_type=F32) + bx)
    log_a = (-LRU_C) * r * sp
    a = jnp.exp(log_a)
    mult = jnp.sqrt(-jnp.tanh(log_a) * (a * a + 1.0))
    first = (row0 + lax.broadcasted_iota(jnp.int32, u.shape, 0)) == 0
    mult = jnp.where(first, 1.0, mult)
    return u, r, ig, a, mult, first


def _block_scan(a, b, reverse):
    n = a.shape[0]
    a, b = a.reshape(n // 8, 8, LRU_BW), b.reshape(n // 8, 8, LRU_BW)
    pos = lax.broadcasted_iota(jnp.int32, a.shape, 1)
    for s in (1, 2, 4):
        keep = (pos >= 8 - s) if reverse else (pos < s)
        by = 8 - s if reverse else s
        b = jnp.where(keep, b, a * pltpu.roll(b, by, 1) + b)
        a = jnp.where(keep, a, a * pltpu.roll(a, by, 1))
    return a.reshape(n, LRU_BW), b.reshape(n, LRU_BW)


def _carry_scan(a_ref, b_ref, out_ref, out_off, t, reverse):
    groups = t // 8

    def block(j, h_in):
        r0 = pl.multiple_of((groups - 1 - j if reverse else j) * 8, 8)
        edge = r0 if reverse else r0 + 7
        h_out = a_ref[pl.ds(edge, 1), :] * h_in + b_ref[pl.ds(edge, 1), :]
        out_ref[pl.ds(pl.multiple_of(out_off + r0, 8), 8), :] = a_ref[pl.ds(r0, 8), :] * h_in + b_ref[pl.ds(r0, 8), :]
        return h_out

    lax.fori_loop(0, groups, block, jnp.zeros((1, LRU_BW), F32), unroll=8)


def _lru_in_specs(t, heads):
    blk = lambda i: (0, i)
    return [
        pl.BlockSpec((2, t, LRU_BW), lambda i: (0, 0, i)),
        pl.BlockSpec((CONV_WIDTH, LRU_BW), blk),
        pl.BlockSpec((1, LRU_BW), blk),
        pl.BlockSpec((None, LRU_BW, LRU_BW), lambda i: (i, 0, 0)),
        pl.BlockSpec((1, LRU_BW), blk),
        pl.BlockSpec((None, LRU_BW, LRU_BW), lambda i: (i, 0, 0)),
        pl.BlockSpec((1, LRU_BW), blk),
        pl.BlockSpec((1, LRU_BW), blk),
    ]


def lru_fwd(name, proj, conv_w, conv_b, wa, ba, wx, bx, lam):
    _, t, c = proj.shape
    heads = c // LRU_BW
    rc = min(256, t)

    def body(proj_ref, cw_ref, cb_ref, wa_ref, ba_ref, wx_ref, bx_ref, lam_ref, out_ref, upad, a_s, b_s):
        upad[0:CONV_HALO, :] = jnp.zeros((CONV_HALO, LRU_BW), F32)
        upad[CONV_HALO:, :] = proj_ref[0]
        sp = _softplus(-lam_ref[...])
        cb, ba, bx, wa, wx = cb_ref[...], ba_ref[...], bx_ref[...], wa_ref[...], wx_ref[...]

        def gates(i, carry):
            r0 = pl.multiple_of(i * rc, rc)
            win = upad[pl.ds(r0, rc + CONV_HALO), :]
            u, r, ig, a, mult, _ = _lru_gates(win, r0, cw_ref, cb, wa, ba, wx, bx, sp)
            rows = pl.ds(r0, rc)
            a_s[rows, :], b_s[rows, :] = _block_scan(a, mult * (ig * u), False)
            return carry

        lax.fori_loop(0, t // rc, gates, 0)
        _carry_scan(a_s, b_s, b_s, 0, t, False)

        def gate_out(i, carry):
            r0 = pl.multiple_of(i * rc, rc)
            gy, _ = _gelu(proj_ref[1, pl.ds(r0, rc), :])
            out_ref[pl.ds(r0, rc), :] = (b_s[pl.ds(r0, rc), :] * gy).astype(BF16)
            return carry

        lax.fori_loop(0, t // rc, gate_out, 0)

    return pl.pallas_call(
        body,
        name=name,
        grid=(heads,),
        in_specs=_lru_in_specs(t, heads),
        out_specs=pl.BlockSpec((t, LRU_BW), lambda i: (0, i)),
        out_shape=jax.ShapeDtypeStruct((t, c), BF16),
        scratch_shapes=[pltpu.VMEM((t + CONV_HALO, LRU_BW), F32)] + [pltpu.VMEM((t, LRU_BW), F32)] * 2,
        compiler_params=_params("parallel"),
    )(proj, conv_w, conv_b, wa, ba, wx, bx, lam)


def lru_bwd(name, proj, dhg, conv_w, conv_b, wa, ba, wx, bx, lam):
    _, t, c = proj.shape
    heads = c // LRU_BW
    rc = min(256, t)
    h8 = CONV_HALO

    def body(proj_ref, dhg_ref, cw_ref, cb_ref, wa_ref, ba_ref, wx_ref, bx_ref, lam_ref,
             dproj_ref, dcw_ref, dcb_ref, dba_ref, dbx_ref, dlam_ref, dwa_ref, dwx_ref,
             upad, u_s, r_s, ig_s, apad, hpad, g_s, dupad, sa_s, sb_s):
        zeros8 = jnp.zeros((h8, LRU_BW), F32)
        upad[0:h8, :] = zeros8
        upad[h8:, :] = proj_ref[0]
        hpad[0:h8, :] = zeros8
        apad[t:, :] = zeros8
        dupad[t:, :] = zeros8
        lam = lam_ref[...]
        sp = _softplus(-lam)
        cb, ba, bx, wa, wx = cb_ref[...], ba_ref[...], bx_ref[...], wa_ref[...], wx_ref[...]

        def gates(i, carry):
            r0 = pl.multiple_of(i * rc, rc)
            win = upad[pl.ds(r0, rc + h8), :]
            u, r, ig, a, mult, _ = _lru_gates(win, r0, cw_ref, cb, wa, ba, wx, bx, sp)
            u_s[pl.ds(r0, rc), :] = u
            r_s[pl.ds(r0, rc), :] = r
            ig_s[pl.ds(r0, rc), :] = ig
            rows = pl.ds(r0, rc)
            apad[rows, :] = a
            sa_s[rows, :], sb_s[rows, :] = _block_scan(a, mult * (ig * u), False)
            return carry

        lax.fori_loop(0, t // rc, gates, 0)
        _carry_scan(sa_s, sb_s, hpad, h8, t, False)

        def out_gate(i, carry):
            r0 = pl.multiple_of(i * rc, rc)
            gy, dgy = _gelu(proj_ref[1, pl.ds(r0, rc), :])
            dh = dhg_ref[pl.ds(r0, rc), :]
            hh = hpad[pl.ds(pl.multiple_of(r0 + h8, 8), rc), :]
            dproj_ref[1, pl.ds(r0, rc), :] = (dh * hh * dgy).astype(BF16)
            rows = pl.ds(r0, rc)
            a_next = _up(apad[pl.ds(r0, rc + h8), :], 1)[:rc]
            sa_s[rows, :], sb_s[rows, :] = _block_scan(a_next, dh * gy, True)
            return carry

        lax.fori_loop(0, t // rc, out_gate, 0)
        _carry_scan(sa_s, sb_s, g_s, 0, t, True)

        zrow = jnp.zeros((1, LRU_BW), F32)
        zmat = jnp.zeros((LRU_BW, LRU_BW), F32)

        def grads(i, carry):
            dsp, dba, dbx, dwa, dwx = carry
            r0 = pl.multiple_of(i * rc, rc)
            g = g_s[pl.ds(r0, rc), :]
            u, r, ig, a = u_s[pl.ds(r0, rc), :], r_s[pl.ds(r0, rc), :], ig_s[pl.ds(r0, rc), :], apad[pl.ds(r0, rc), :]
            hprev = _down(hpad[pl.ds(r0, rc + h8), :], 1)[h8:]
            first = (r0 + lax.broadcasted_iota(jnp.int32, u.shape, 0)) == 0
            log_a = (-LRU_C) * r * sp
            mult = jnp.where(first, 1.0, jnp.sqrt(-jnp.tanh(log_a) * (a * a + 1.0)))
            dmult = jnp.where(first, 0.0, g * (ig * u))
            dlog_a = g * hprev * a - dmult * (a * a) / mult
            dr = dlog_a * ((-LRU_C) * sp)
            dpre_r = dr * r * (1.0 - r)
            dpre_i = (g * mult * u) * ig * (1.0 - ig)
            pr, pi, ub = dpre_r.astype(BF16), dpre_i.astype(BF16), u.astype(BF16)
            nt = (((1,), (1,)), ((), ()))
            tn = (((0,), (0,)), ((), ()))
            du = (g * mult * ig + lax.dot_general(pr, wa, nt, preferred_element_type=F32)
                  + lax.dot_general(pi, wx, nt, preferred_element_type=F32))
            dupad[pl.ds(r0, rc), :] = du
            return (dsp + jnp.sum(dlog_a * ((-LRU_C) * r), axis=0, keepdims=True),
                    dba + jnp.sum(dpre_r, axis=0, keepdims=True),
                    dbx + jnp.sum(dpre_i, axis=0, keepdims=True),
                    dwa + lax.dot_general(ub, pr, tn, preferred_element_type=F32),
                    dwx + lax.dot_general(ub, pi, tn, preferred_element_type=F32))

        dsp, dba, dbx, dwa, dwx = lax.fori_loop(0, t // rc, grads, (zrow, zrow, zrow, zmat, zmat))
        dba_ref[...] = dba
        dbx_ref[...] = dbx
        dwa_ref[...] = dwa
        dwx_ref[...] = dwx
        dlam_ref[...] = -dsp * jax.nn.sigmoid(-lam)

        def conv_back(i, carry):
            dcb, d0, d1, d2, d3 = carry
            r0 = pl.multiple_of(i * rc, rc)
            dwin = dupad[pl.ds(r0, rc + h8), :]
            du = dwin[:rc]
            du0 = (cw_ref[3:4, :] * du + cw_ref[2:3, :] * _up(dwin, 1)[:rc]
                   + cw_ref[1:2, :] * _up(dwin, 2)[:rc] + cw_ref[0:1, :] * _up(dwin, 3)[:rc])
            dproj_ref[0, pl.ds(r0, rc), :] = du0.astype(BF16)
            win = upad[pl.ds(r0, rc + h8), :]
            red = lambda v: jnp.sum(v, axis=0, keepdims=True)
            return (dcb + red(du), d0 + red(du * _down(win, 3)[h8:]), d1 + red(du * _down(win, 2)[h8:]),
                    d2 + red(du * _down(win, 1)[h8:]), d3 + red(du * win[h8:]))

        dcb, d0, d1, d2, d3 = lax.fori_loop(0, t // rc, conv_back, (zrow,) * 5)
        dcb_ref[...] = dcb
        dcw_ref[0:1, :] = d0
        dcw_ref[1:2, :] = d1
        dcw_ref[2:3, :] = d2
        dcw_ref[3:4, :] = d3

    blk = lambda i: (0, i)
    vec = jax.ShapeDtypeStruct((1, c), F32)
    mat = jax.ShapeDtypeStruct((heads, LRU_BW, LRU_BW), F32)
    full = lambda: pltpu.VMEM((t, LRU_BW), F32)
    padded = lambda: pltpu.VMEM((t + h8, LRU_BW), F32)
    return pl.pallas_call(
        body,
        name=name,
        grid=(heads,),
        in_specs=_lru_in_specs(t, heads)[:1] + [pl.BlockSpec((t, LRU_BW), blk)] + _lru_in_specs(t, heads)[1:],
        out_specs=[pl.BlockSpec((2, t, LRU_BW), lambda i: (0, 0, i)), pl.BlockSpec((CONV_WIDTH, LRU_BW), blk)]
        + [pl.BlockSpec((1, LRU_BW), blk)] * 4 + [pl.BlockSpec((None, LRU_BW, LRU_BW), lambda i: (i, 0, 0))] * 2,
        out_shape=[jax.ShapeDtypeStruct((2, t, c), BF16), jax.ShapeDtypeStruct((CONV_WIDTH, c), F32), vec, vec, vec, vec, mat, mat],
        scratch_shapes=[padded(), full(), full(), full(), padded(), padded(), full(), padded()] + [full()] * 2,
        compiler_params=_params("parallel"),
    )(proj, dhg, conv_w, conv_b, wa, ba, wx, bx, lam)


def _pick_level(g, levels):
    out = levels[-1]
    for k in range(len(levels) - 2, -1, -1):
        out = jnp.where(g == k, levels[k], out)
    return out


def _pool_z(win, g, row0, rc):
    levels, cur = [], win
    for k in range(len(POOL_WINDOWS)):
        cur = cur + _down(cur, 1 << k)
        levels.append(cur[POOL_HALO:])
    tot = _pick_level(g, levels)
    width = jnp.left_shift(2, g)
    row = row0 + lax.broadcasted_iota(jnp.int32, tot.shape, 0)
    cnt = jnp.minimum(row + 1, width).astype(F32)
    return tot / cnt - win[POOL_HALO:], cnt


def _pool_specs(t, gw):
    blk = lambda g: (0, g)
    return [pl.BlockSpec((t, gw), blk), pl.BlockSpec((None, gw, gw), lambda g: (g, 0, 0)),
            pl.BlockSpec((1, gw), blk), pl.BlockSpec((1, gw), blk)]


def pool_fwd(name, u, w_grp, b_grp, scale):
    t, d = u.shape
    gw = d // len(POOL_WINDOWS)
    rc = min(256, t)

    def body(u_ref, wg_ref, bg_ref, sc_ref, out_ref, upad):
        g = pl.program_id(0)
        upad[0:POOL_HALO, :] = jnp.zeros((POOL_HALO, gw), F32)
        upad[POOL_HALO:, :] = u_ref[...]
        wg, bg, sc = wg_ref[...], bg_ref[...], sc_ref[...]

        def chunk(i, carry):
            r0 = pl.multiple_of(i * rc, rc)
            z, _ = _pool_z(upad[pl.ds(r0, rc + POOL_HALO), :], g, r0, rc)
            z2 = jnp.dot(z.astype(BF16), wg, preferred_element_type=F32) + bg
            out_ref[pl.ds(r0, rc), :] = (z2 * sc).astype(BF16)
            return carry

        lax.fori_loop(0, t // rc, chunk, 0)

    return pl.pallas_call(
        body,
        name=name,
        grid=(len(POOL_WINDOWS),),
        in_specs=_pool_specs(t, gw),
        out_specs=pl.BlockSpec((t, gw), lambda g: (0, g)),
        out_shape=jax.ShapeDtypeStruct((t, d), BF16),
        scratch_shapes=[pltpu.VMEM((t + POOL_HALO, gw), F32)],
        compiler_params=_params("parallel"),
    )(u, w_grp, b_grp, scale)


def pool_bwd(name, u, dzs, w_grp, b_grp, scale):
    t, d = u.shape
    gw = d // len(POOL_WINDOWS)
    rc = min(256, t)

    def body(u_ref, dzs_ref, wg_ref, bg_ref, sc_ref, du_ref, dwg_ref, dbg_ref, dsc_ref, upad, qpad, dz_s):
        g = pl.program_id(0)
        upad[0:POOL_HALO, :] = jnp.zeros((POOL_HALO, gw), F32)
        upad[POOL_HALO:, :] = u_ref[...]
        qpad[t:, :] = jnp.zeros((POOL_HALO, gw), F32)
        wg, bg, sc = wg_ref[...], bg_ref[...], sc_ref[...]
        zrow = jnp.zeros((1, gw), F32)

        def chunk(i, carry):
            dsc, dbg, dwg = carry
            r0 = pl.multiple_of(i * rc, rc)
            z, cnt = _pool_z(upad[pl.ds(r0, rc + POOL_HALO), :], g, r0, rc)
            zb = z.astype(BF16)
            z2 = jnp.dot(zb, wg, preferred_element_type=F32) + bg
            dzs = dzs_ref[pl.ds(r0, rc), :]
            dz2 = dzs * sc
            d2b = dz2.astype(BF16)
            dz = lax.dot_general(d2b, wg, (((1,), (1,)), ((), ())), preferred_element_type=F32)
            dz_s[pl.ds(r0, rc), :] = dz
            qpad[pl.ds(r0, rc), :] = dz / cnt
            return (dsc + jnp.sum(dzs * z2, axis=0, keepdims=True), dbg + jnp.sum(dz2, axis=0, keepdims=True),
                    dwg + lax.dot_general(zb, d2b, (((0,), (0,)), ((), ())), preferred_element_type=F32))

        dsc, dbg, dwg = lax.fori_loop(0, t // rc, chunk, (zrow, zrow, jnp.zeros((gw, gw), F32)))
        dsc_ref[...] = dsc
        dbg_ref[...] = dbg
        dwg_ref[...] = dwg

        def spread(i, carry):
            r0 = pl.multiple_of(i * rc, rc)
            levels, cur = [], qpad[pl.ds(r0, rc + POOL_HALO), :]
            for k in range(len(POOL_WINDOWS)):
                cur = cur + _up(cur, 1 << k)
                levels.append(cur[:rc])
            du_ref[pl.ds(r0, rc), :] = (_pick_level(g, levels) - dz_s[pl.ds(r0, rc), :]).astype(BF16)
            return carry

        lax.fori_loop(0, t // rc, spread, 0)

    blk = lambda g: (0, g)
    vec = jax.ShapeDtypeStruct((1, d), F32)
    return pl.pallas_call(
        body,
        name=name,
        grid=(len(POOL_WINDOWS),),
        in_specs=_pool_specs(t, gw)[:1] + [pl.BlockSpec((t, gw), blk)] + _pool_specs(t, gw)[1:],
        out_specs=[pl.BlockSpec((t, gw), blk), pl.BlockSpec((None, gw, gw), lambda g: (g, 0, 0)),
                   pl.BlockSpec((1, gw), blk), pl.BlockSpec((1, gw), blk)],
        out_shape=[jax.ShapeDtypeStruct((t, d), BF16), jax.ShapeDtypeStruct((len(POOL_WINDOWS), gw, gw), F32), vec, vec],
        scratch_shapes=[pltpu.VMEM((t + POOL_HALO, gw), F32), pltpu.VMEM((t + POOL_HALO, gw), F32), pltpu.VMEM((t, gw), F32)],
        compiler_params=_params("parallel"),
    )(u, dzs, w_grp, b_grp, scale)


def _place():
    return lax.axis_index("x"), lax.axis_index("y"), lax.axis_index("c")


def _other_chips(x, y):
    return [(1 - x, y), (x, 1 - y), (1 - x, 1 - y)]


def _half(c, rows):
    h = rows // 2
    return pl.ds(pl.multiple_of(c * h, 8), h)


_ANY = pl.BlockSpec(memory_space=pl.ANY)


def into_block(name, shards, layer, r, me, dtype):
    c = shards.shape[1]
    tr = _tile(r, 512, 16)
    per = r // tr

    def body(me_ref, s_ref, o_ref):
        o_ref[...] = s_ref[...].astype(o_ref.dtype)

    return pl.pallas_call(
        body,
        name=name,
        grid_spec=pltpu.PrefetchScalarGridSpec(
            num_scalar_prefetch=1,
            grid=(per,),
            in_specs=[pl.BlockSpec((tr, c), lambda i, me_ref: (layer * per + i, 0))],
            out_specs=pl.BlockSpec((None, tr, c), lambda i, me_ref: (me_ref[0], i, 0)),
        ),
        out_shape=jax.ShapeDtypeStruct((N_CHIPS, r, c), dtype),
        compiler_params=_params("parallel"),
    )(me, shards)


_HBM = pl.BlockSpec(memory_space=pltpu.HBM)
_SEM = pl.BlockSpec(memory_space=pltpu.SEMAPHORE)


def _in_hbm(a):
    return pltpu.with_memory_space_constraint(a, pltpu.HBM)


def split_start(name, plan, n_copies, bufs, dep):
    n = len(bufs)

    def body(*refs):
        for cp in plan(refs[:n], refs[n + 1], refs[n + 2]):
            cp.start()
        refs[-1][...] = jnp.zeros_like(refs[-1])

    res = pl.pallas_call(
        body,
        name=name,
        in_specs=[_HBM] * n + [_ANY],
        out_specs=[_SEM, _SEM] + [_HBM] * n + [pl.BlockSpec(memory_space=pltpu.VMEM)],
        out_shape=[pltpu.SemaphoreType.DMA((n_copies,)), pltpu.SemaphoreType.DMA((n_copies,))]
        + [pltpu.HBM(b.shape, b.dtype) for b in bufs] + [jax.ShapeDtypeStruct((8, 128), F32)],
        input_output_aliases={i: 2 + i for i in range(n)},
        compiler_params=pltpu.CompilerParams(has_side_effects=pltpu.SideEffectType.DATAFLOW_SIDE_EFFECTING),
    )(*[_in_hbm(b) for b in bufs], dep)
    return res[0], res[1], list(res[2:2 + n]), res[-1]


def split_wait(name, plan, send_sems, recv_sems, bufs, after):
    n = len(bufs)

    def body(*refs):
        copies = plan(refs[:n], refs[n], refs[n + 1])
        for cp in copies:
            cp.wait_send()
        for cp in copies:
            cp.wait_recv()

    return pl.pallas_call(
        body,
        name=name,
        in_specs=[_HBM] * n + [_SEM, _SEM, _ANY],
        out_specs=[_HBM] * n,
        out_shape=[pltpu.HBM(b.shape, b.dtype) for b in bufs],
        input_output_aliases={i: i for i in range(n)},
        compiler_params=pltpu.CompilerParams(has_side_effects=pltpu.SideEffectType.DATAFLOW_SIDE_EFFECTING),
    )(*bufs, send_sems, recv_sems, after)


def gather_plan(n):
    def plan(bufs, send_sems, recv_sems):
        x, y, c = _place()
        copies = []
        for i in range(n):
            blk = bufs[i].at[2 * x + y, _half(c, bufs[i].shape[1]), :]
            for j, chip in enumerate(_other_chips(x, y)):
                copies.append(pltpu.make_async_remote_copy(
                    src_ref=blk, dst_ref=blk, send_sem=send_sems.at[3 * i + j], recv_sem=recv_sems.at[3 * i + j],
                    device_id=(*chip, c), device_id_type=MESH))
        return copies

    return plan


def pair_forward(name, bufs):
    n = len(bufs)

    def body(*refs):
        outs = refs[n:2 * n]
        send_sems, recv_sems = refs[2 * n:]
        x, y, c = _place()
        copies = []
        for i in range(n):
            for j, (cx, cy) in enumerate(_other_chips(x, y)):
                blk = outs[i].at[2 * cx + cy, _half(c, outs[i].shape[1]), :]
                copies.append(pltpu.make_async_remote_copy(
                    src_ref=blk, dst_ref=blk, send_sem=send_sems.at[3 * i + j], recv_sem=recv_sems.at[3 * i + j],
                    device_id=(x, y, 1 - c), device_id_type=MESH))
        for cp in copies:
            cp.start()
        for cp in copies:
            cp.wait()

    return pl.pallas_call(
        body,
        name=name,
        in_specs=[_ANY] * n,
        out_specs=[_ANY] * n,
        out_shape=[jax.ShapeDtypeStruct(b.shape, b.dtype) for b in bufs],
        input_output_aliases={i: i for i in range(n)},
        scratch_shapes=[pltpu.SemaphoreType.DMA((3 * n,)), pltpu.SemaphoreType.DMA((3 * n,))],
    )(*bufs)


def all_gather_chips(name, bufs):
    n = len(bufs)

    def body(*refs):
        outs = refs[n:2 * n]
        send_sems, recv_sems = refs[2 * n:]
        x, y, c = _place()
        me, sibling = 2 * x + y, (x, y, 1 - c)
        chips = _other_chips(x, y)

        def copy(i, slot, block, half, to):
            blk = outs[i].at[block, _half(half, outs[i].shape[1]), :]
            return pltpu.make_async_remote_copy(
                src_ref=blk, dst_ref=blk, send_sem=send_sems.at[i * 6 + slot], recv_sem=recv_sems.at[i * 6 + slot],
                device_id=to, device_id_type=MESH)

        first = [copy(i, j, me, c, (*chip, c)) for i in range(n) for j, chip in enumerate(chips)]
        for cp in first:
            cp.start()
        passed = []
        for i in range(n):
            for j, (cx, cy) in enumerate(chips):
                copy(i, j, 2 * cx + cy, c, (x, y, c)).wait_recv()
                fwd = copy(i, 3 + j, 2 * cx + cy, c, sibling)
                fwd.start()
                passed.append(fwd)
        for i in range(n):
            for j, (cx, cy) in enumerate(chips):
                copy(i, 3 + j, 2 * cx + cy, 1 - c, (x, y, c)).wait_recv()
        for cp in first + passed:
            cp.wait_send()

    return pl.pallas_call(
        body,
        name=name,
        in_specs=[_ANY] * n,
        out_specs=[_ANY] * n,
        out_shape=[jax.ShapeDtypeStruct(b.shape, b.dtype) for b in bufs],
        input_output_aliases={i: i for i in range(n)},
        scratch_shapes=[pltpu.SemaphoreType.DMA((6 * n,)), pltpu.SemaphoreType.DMA((6 * n,))],
    )(*bufs)


def pair_plan(n):
    def plan(bufs, send_sems, recv_sems):
        x, y, c = _place()
        return [pltpu.make_async_remote_copy(
            src_ref=bufs[i].at[:, _half(1 - c, bufs[i].shape[1]), :], dst_ref=bufs[n + i], send_sem=send_sems.at[i],
            recv_sem=recv_sems.at[i], device_id=(x, y, 1 - c), device_id_type=MESH) for i in range(n)]

    return plan


def chip_plan(n):
    def plan(bufs, send_sems, recv_sems):
        x, y, c = _place()
        copies = []
        for i in range(n):
            for j, (cx, cy) in enumerate(_other_chips(x, y)):
                copies.append(pltpu.make_async_remote_copy(
                    src_ref=bufs[i].at[2 * cx + cy], dst_ref=bufs[n + i].at[2 * x + y], send_sem=send_sems.at[3 * i + j],
                    recv_sem=recv_sems.at[3 * i + j], device_id=(cx, cy, c), device_id_type=MESH))
        return copies

    return plan


def exchange(name, plan, n_copies, srcs, land_shapes, deps=()):
    n, n_d = len(srcs), len(deps)

    def body(*refs):
        copies = plan(refs[:n] + refs[n + n_d:2 * n + n_d], refs[2 * n + n_d], refs[2 * n + n_d + 1])
        for cp in copies:
            cp.start()
        for cp in copies:
            cp.wait()

    return pl.pallas_call(
        body,
        name=name,
        in_specs=[_ANY] * (n + n_d),
        out_specs=[_ANY] * n,
        out_shape=land_shapes,
        scratch_shapes=[pltpu.SemaphoreType.DMA((n_copies,)), pltpu.SemaphoreType.DMA((n_copies,))],
    )(*srcs, *deps)


def pair_lands(grads):
    return [jax.ShapeDtypeStruct((g.shape[0], g.shape[1] // 2, g.shape[2]), g.dtype) for g in grads]


def pair_exchange(name, grads, deps=()):
    return exchange(name, pair_plan(len(grads)), len(grads), grads, pair_lands(grads), deps)


def chip_exchange(name, parts):
    return exchange(name, chip_plan(len(parts)), 3 * len(parts), parts, [jax.ShapeDtypeStruct(p.shape, p.dtype) for p in parts])


def pair_gather(name, bufs, blocked, layers):
    n = len(bufs)
    n_copies = sum(layers)

    def body(*refs):
        outs = refs[n:2 * n]
        send_sems, recv_sems = refs[2 * n:]
        x, y, c = _place()
        copies = []
        for i in range(n):
            buf = outs[i].at[2 * x + y] if blocked[i] else outs[i]
            r = buf.shape[0] // layers[i]
            for l in range(layers[i]):
                mine = buf.at[pl.ds(pl.multiple_of(l * r + c * (r // 2), 8), r // 2), :]
                copies.append(pltpu.make_async_remote_copy(
                    src_ref=mine, dst_ref=mine, send_sem=send_sems.at[len(copies)], recv_sem=recv_sems.at[len(copies)],
                    device_id=(x, y, 1 - c), device_id_type=MESH))
        for cp in copies:
            cp.start()
        for cp in copies:
            cp.wait()

    return pl.pallas_call(
        body,
        name=name,
        in_specs=[_ANY] * n,
        out_specs=[_ANY] * n,
        out_shape=[jax.ShapeDtypeStruct(b.shape, b.dtype) for b in bufs],
        input_output_aliases={i: i for i in range(n)},
        scratch_shapes=[pltpu.SemaphoreType.DMA((n_copies,)), pltpu.SemaphoreType.DMA((n_copies,))],
    )(*bufs)


def pair_sum(name, grad, recv, core, dtype):
    _, r, c = grad.shape
    h = r // 2
    th = _tile(h, 512, 16)
    per = h // th

    def body(core_ref, g_ref, r_ref, o_ref):
        o_ref[...] = (g_ref[...] + r_ref[...]).astype(o_ref.dtype)

    return pl.pallas_call(
        body,
        name=name,
        grid_spec=pltpu.PrefetchScalarGridSpec(
            num_scalar_prefetch=1,
            grid=(N_CHIPS, per),
            in_specs=[pl.BlockSpec((None, th, c), lambda k, i, core_ref: (k, core_ref[0] * per + i, 0)),
                      pl.BlockSpec((None, th, c), lambda k, i, core_ref: (k, i, 0))],
            out_specs=pl.BlockSpec((None, th, c), lambda k, i, core_ref: (k, i, 0)),
        ),
        out_shape=jax.ShapeDtypeStruct((N_CHIPS, h, c), dtype),
        compiler_params=_params("parallel", "parallel"),
    )(core, grad, recv)


def chip_sum(name, got, parts, place, blocked, into=None, layer=0, n_layers=1):
    _, h, c = parts.shape
    th = _tile(h, 256, 16)
    per = h // th

    def body(place_ref, q0, q1, q2, q3, p_ref, *rest):
        o_ref = rest[-1]
        me = place_ref[0]
        own = p_ref[...].astype(F32)
        v = [jnp.where(me == k, own, q[...].astype(F32)) for k, q in enumerate((q0, q1, q2, q3))]
        o_ref[...] = ((v[0] + v[1]) + v[2]) + v[3]

    def got_spec(k):
        return pl.BlockSpec((None, th, c), lambda i, pr: (jnp.where(pr[0] == k, (k + 1) % N_CHIPS, k), i, 0))

    if blocked:
        out_spec = pl.BlockSpec((None, th, c), lambda i, pr: (pr[0], pr[1] * per + i, 0))
        out_shape = jax.ShapeDtypeStruct((N_CHIPS, 2 * h, c), F32)
    else:
        out_spec = pl.BlockSpec((th, c), lambda i, pr: ((2 * layer + pr[1]) * per + i, 0))
        out_shape = jax.ShapeDtypeStruct((n_layers * 2 * h, c), F32)
    carried = [] if into is None else [into]
    return pl.pallas_call(
        body,
        name=name,
        grid_spec=pltpu.PrefetchScalarGridSpec(
            num_scalar_prefetch=1,
            grid=(per,),
            in_specs=[got_spec(k) for k in range(N_CHIPS)] + [pl.BlockSpec((None, th, c), lambda i, pr: (pr[0], i, 0))]
            + [_ANY] * len(carried),
            out_specs=out_spec,
        ),
        out_shape=out_shape,
        input_output_aliases={6: 0} if carried else {},
        compiler_params=_params("parallel"),
    )(place, got, got, got, got, parts, *carried)


def adamw(name, w, g, m, v):
    r, c = w.shape
    tr = _tile(r, 512, 8)
    c1 = 1.0 - ADAM_B1 ** ADAM_STEP
    c2 = 1.0 - ADAM_B2 ** ADAM_STEP

    def body(w_ref, g_ref, m_ref, v_ref, d_ref, nm_ref, nv_ref):
        gv = g_ref[...]
        nm = ADAM_B1 * m_ref[...] + (1.0 - ADAM_B1) * gv
        nv = ADAM_B2 * v_ref[...] + (1.0 - ADAM_B2) * (gv * gv)
        d_ref[...] = -ADAM_LR * ((nm / c1) / (jnp.sqrt(nv / c2) + ADAM_EPS) + ADAM_WD * w_ref[...])
        nm_ref[...] = nm
        nv_ref[...] = nv

    spec = pl.BlockSpec((tr, c), lambda i: (i, 0))
    return pl.pallas_call(
        body,
        name=name,
        grid=(r // tr,),
        in_specs=[spec] * 4,
        out_specs=[spec] * 3,
        out_shape=[jax.ShapeDtypeStruct((r, c), F32)] * 3,
        compiler_params=_params("parallel"),
    )(w, g, m, v)


def reduce_to_shards(tag, grads, wire, blocked, place, deps=()):
    recv = pair_exchange(tag + "_pair_exchange", grads, deps)
    parts = [pair_sum(f"{tag}_pair_sum_{i}", g, r, place[1:], w) for i, (g, r, w) in enumerate(zip(grads, recv, wire))]
    got = chip_exchange(tag + "_chip_exchange", parts)
    sums = [chip_sum(f"{tag}_chip_sum_{i}", q, p, place, b) for i, (q, p, b) in enumerate(zip(got, parts, blocked))]
    return pair_gather(tag + "_pair_gather", sums, blocked, [1] * len(sums))


def _pack(arrays, row_multiple, cols=BLOB_COLS):
    flat = jnp.concatenate([a.reshape(-1).astype(F32) for a in arrays])
    rows = -(-flat.shape[0] // cols)
    rows = -(-rows // row_multiple) * row_multiple
    return jnp.pad(flat, (0, rows * cols - flat.shape[0])).reshape(rows, cols)


def _unpack(blob, shapes):
    flat, out, off = blob.reshape(-1), [], 0
    for s in shapes:
        size = math.prod(s)
        out.append(flat[off:off + size].reshape(s))
        off += size
    return out


def _unpack_rows(blobs, shapes):
    out, off = [], 0
    for s in shapes:
        size = math.prod(s)
        out.append(blobs[:, off:off + size].reshape((blobs.shape[0],) + tuple(s)))
        off += size
    return out


def kernel(x, p, lru_w_in, lru_conv_w, lru_conv_b, lru_wa, lru_ba, lru_wx, lru_bx, lru_lambda, lru_w_out, pool_w_in, pool_w_grp, pool_b_grp, pool_scale, pool_w_out, ln_mix_g, ln_mix_b, mlp_w1, mlp_w2, ln_mlp_g, ln_mlp_b, ple_w, ple_gate_w, ple_gate_b, loss_target, m_lru_w_in, m_lru_conv_w, m_lru_conv_b, m_lru_wa, m_lru_ba, m_lru_wx, m_lru_bx, m_lru_lambda, m_lru_w_out, m_pool_w_in, m_pool_w_grp, m_pool_b_grp, m_pool_scale, m_pool_w_out, m_ln_mix_g, m_ln_mix_b, m_mlp_w1, m_mlp_w2, m_ln_mlp_g, m_ln_mlp_b, m_ple_w, m_ple_gate_w, m_ple_gate_b, v_lru_w_in, v_lru_conv_w, v_lru_conv_b, v_lru_wa, v_lru_ba, v_lru_wx, v_lru_bx, v_lru_lambda, v_lru_w_out, v_pool_w_in, v_pool_w_grp, v_pool_b_grp, v_pool_scale, v_pool_w_out, v_ln_mix_g, v_ln_mix_b, v_mlp_w1, v_mlp_w2, v_ln_mlp_g, v_ln_mlp_b, v_ple_w, v_ple_gate_w, v_ple_gate_b):
    weights = dict(lru_w_in=lru_w_in, lru_conv_w=lru_conv_w, lru_conv_b=lru_conv_b, lru_wa=lru_wa, lru_ba=lru_ba, lru_wx=lru_wx, lru_bx=lru_bx, lru_lambda=lru_lambda, lru_w_out=lru_w_out, pool_w_in=pool_w_in, pool_w_grp=pool_w_grp, pool_b_grp=pool_b_grp, pool_scale=pool_scale, pool_w_out=pool_w_out, ln_mix_g=ln_mix_g, ln_mix_b=ln_mix_b, mlp_w1=mlp_w1, mlp_w2=mlp_w2, ln_mlp_g=ln_mlp_g, ln_mlp_b=ln_mlp_b, ple_w=ple_w, ple_gate_w=ple_gate_w, ple_gate_b=ple_gate_b)
    mom_m = dict(lru_w_in=m_lru_w_in, lru_conv_w=m_lru_conv_w, lru_conv_b=m_lru_conv_b, lru_wa=m_lru_wa, lru_ba=m_lru_ba, lru_wx=m_lru_wx, lru_bx=m_lru_bx, lru_lambda=m_lru_lambda, lru_w_out=m_lru_w_out, pool_w_in=m_pool_w_in, pool_w_grp=m_pool_w_grp, pool_b_grp=m_pool_b_grp, pool_scale=m_pool_scale, pool_w_out=m_pool_w_out, ln_mix_g=m_ln_mix_g, ln_mix_b=m_ln_mix_b, mlp_w1=m_mlp_w1, mlp_w2=m_mlp_w2, ln_mlp_g=m_ln_mlp_g, ln_mlp_b=m_ln_mlp_b, ple_w=m_ple_w, ple_gate_w=m_ple_gate_w, ple_gate_b=m_ple_gate_b)
    mom_v = dict(lru_w_in=v_lru_w_in, lru_conv_w=v_lru_conv_w, lru_conv_b=v_lru_conv_b, lru_wa=v_lru_wa, lru_ba=v_lru_ba, lru_wx=v_lru_wx, lru_bx=v_lru_bx, lru_lambda=v_lru_lambda, lru_w_out=v_lru_w_out, pool_w_in=v_pool_w_in, pool_w_grp=v_pool_w_grp, pool_b_grp=v_pool_b_grp, pool_scale=v_pool_scale, pool_w_out=v_pool_w_out, ln_mix_g=v_ln_mix_g, ln_mix_b=v_ln_mix_b, mlp_w1=v_mlp_w1, mlp_w2=v_mlp_w2, ln_mlp_g=v_ln_mlp_g, ln_mlp_b=v_ln_mlp_b, ple_w=v_ple_w, ple_gate_w=v_ple_gate_w, ple_gate_b=v_ple_gate_b)
    names = list(weights)

    depth, d = ln_mix_g.shape
    t = x.shape[1]
    n_a, n_b = lru_w_in.shape[0], pool_w_in.shape[0]
    d_rnn = lru_w_out.shape[1] * N_CHIPS
    heads = d_rnn // LRU_BW
    d_ff = mlp_w1.shape[2] * N_CHIPS
    ple_dim = ple_w.shape[1]
    n_grp = len(POOL_WINDOWS)
    gw = d // n_grp
    alpha = (2 * depth) ** 0.25
    chip = 2 * lax.axis_index("x") + lax.axis_index("y")
    place = jnp.stack([chip, lax.axis_index("c")]).astype(jnp.int32)

    x2d = x.reshape(t, d)
    target = loss_target.reshape(t, d)
    p3 = p.reshape(depth, t, ple_dim)

    big = ["lru_w_in", "lru_w_out", "pool_w_in", "pool_w_out", "mlp_w1", "mlp_w2", "ple_w", "ple_gate_w", "pool_w_grp"]
    flat2 = lambda a: a.reshape(-1, a.shape[-1])
    small_sharded = ["lru_conv_w", "pool_b_grp", "pool_scale"]
    small_blob = _pack([weights[k] for k in small_sharded], 16, cols=256)
    every_layer = ("mlp_w1", "mlp_w2", "ple_w", "ple_gate_w")
    mlp_tiles = [dict(), dict(pm=512), dict(pn=512), dict(pk=512)]

    def layer_keys(i):
        return (["lru_w_in", "lru_w_out"] if i % 2 == 0 else ["pool_w_in", "pool_w_out", "pool_w_grp"]) + list(every_layer)

    def stage(k, i):
        w = weights[k]
        return into_block(f"stage_l{i}_{k}", flat2(w), i if k in every_layer else i // 2, math.prod(w.shape[1:-1]),
                          place[:1], BF16)

    staged = [[stage(k, i) for k in layer_keys(i)] for i in range(depth)]
    first = all_gather_chips("gather_l0", staged[0][:1] + [into_block("stage_small", small_blob, 0, small_blob.shape[0], place[:1], F32)])
    wg = {(layer_keys(0)[0], 0): first[0]}

    def start_gather(tag, bufs, dep):
        plan = gather_plan(len(bufs))
        return (plan,) + split_start(f"gather_{tag}_start", plan, 3 * len(bufs), bufs, dep)

    def land_gather(tag, flight, keys, layer, after):
        plan, send_sems, recv_sems, bufs, _ = flight
        landed = split_wait(f"gather_{tag}_wait", plan, send_sems, recv_sems, bufs, after)
        wg.update(zip([(k, layer) for k in keys], pair_forward(f"gather_{tag}_forward", landed)))

    conv_w_sh, b_grp_sh, scale_sh = _unpack_rows(first[-1].reshape(N_CHIPS, -1), [weights[k].shape for k in small_sharded])
    conv_w_full = jnp.moveaxis(conv_w_sh, 0, 2).reshape(n_a, CONV_WIDTH, d_rnn)
    b_grp_full = jnp.moveaxis(b_grp_sh, 0, 1).reshape(n_b, 1, d)
    scale_full = jnp.moveaxis(scale_sh, 0, 1).reshape(n_b, 1, d)
    rows_grp = gw // N_CHIPS
    w_grp_full = lambda i: jnp.moveaxis(wg["pool_w_grp", i].reshape(N_CHIPS, n_grp, rows_grp, gw), 0, 1).reshape(n_grp, gw, gw)
    wa_bf, wx_bf = lru_wa.astype(BF16), lru_wx.astype(BF16)
    row = lambda a, i: a[i].reshape(1, -1)

    saved = []
    cur, cur_bf = x2d, x2d.astype(BF16)
    for i in range(depth):
        slot = i // 2
        sv = dict(x_bf=cur_bf)
        if i == 0:
            flight = start_gather("l0_rest", staged[0][1:], cur)
        elif i + 1 < depth:
            flight = start_gather(f"l{i + 1}", staged[i + 1], cur)
        deps = (flight[-1],) if i + 1 < depth else ()
        if i % 2 == 0:
            (proj,) = matmul(f"l{i}_lru_in", plain(cur_bf), colsplit(wg["lru_w_in", i], 0, d), "nn",
                             [colsplit(None, 0, t, n=2, full=(2, t, d_rnn), dtype=F32)], deps=deps)
            hg = lru_fwd(f"l{i}_lru", proj, conv_w_full[slot], row(lru_conv_b, slot), wa_bf[slot], row(lru_ba, slot),
                         wx_bf[slot], row(lru_bx, slot), row(lru_lambda, slot))
            deps = ()
            if i == 0:
                land_gather("l0_rest", flight, layer_keys(0)[1:], 0, hg)
                flight = start_gather("l1", staged[1], hg)
                deps = (flight[-1],)
            (mix,) = matmul(f"l{i}_lru_out", plain(hg), rowsplit(wg["lru_w_out", i], 0, d_rnn // N_CHIPS), "nn",
                            [plain(shape=(t, d), dtype=F32)], deps=deps)
            sv.update(proj=proj, act=hg)
        else:
            (u,) = matmul(f"l{i}_pool_in", plain(cur_bf), rowsplit(wg["pool_w_in", i], 0, d // N_CHIPS), "nn",
                          [plain(shape=(t, d), dtype=F32)], deps=deps)
            zs = pool_fwd(f"l{i}_pool", u, w_grp_full(i), b_grp_full[slot], scale_full[slot])
            (mix,) = matmul(f"l{i}_pool_out", plain(zs), rowsplit(wg["pool_w_out", i], 0, d // N_CHIPS), "nn",
                            [plain(shape=(t, d), dtype=F32)])
            sv.update(u=u, act=zs)
        x1, x1_bf, s1 = ln_fwd(f"l{i}_ln_mix", alpha, cur, mix, row(ln_mix_g, i), row(ln_mix_b, i))

        def relu2(acc):
            hr = jnp.maximum(acc, 0.0)
            return hr, hr * hr

        hr, hh = matmul(f"l{i}_mlp_up", plain(x1_bf), colsplit(wg["mlp_w1", i], 0, d), "nn",
                        [plain(shape=(t, d_ff), dtype=F32), plain(shape=(t, d_ff), dtype=BF16)], epilogue=relu2, **mlp_tiles[i])
        (mlp,) = matmul(f"l{i}_mlp_down", plain(hh), rowsplit(wg["mlp_w2", i], 0, d_ff // N_CHIPS), "nn",
                        [plain(shape=(t, d), dtype=F32)], **mlp_tiles[i])
        x2, x2_bf, s2 = ln_fwd(f"l{i}_ln_mlp", alpha, x1, mlp, row(ln_mlp_g, i), row(ln_mlp_b, i))
        (e,) = matmul(f"l{i}_ple", plain(p3[i]), colsplit(wg["ple_w", i], 0, ple_dim), "nn", [plain(shape=(t, d), dtype=F32)])

        def ple_out(acc, x2_t, e_t, gb):
            gate = jax.nn.sigmoid(acc + gb)
            x3 = x2_t + e_t * gate
            return x3, x3, gate

        cur, cur_bf, gate = matmul(f"l{i}_ple_gate", plain(x2_bf), rowsplit(wg["ple_gate_w", i], 0, d // N_CHIPS), "nn",
                                   [plain(shape=(t, d), dtype=F32), plain(shape=(t, d), dtype=BF16), plain(shape=(t, d), dtype=F32)],
                                   epilogue=ple_out, tiles=[plain(x2), plain(e)], rows=[row(ple_gate_b, i)])
        sv.update(s1=s1, x1_bf=x1_bf, hr=hr, hh=hh, s2=s2, x2_bf=x2_bf, gate=gate, e=e)
        saved.append(sv)
        if i + 1 < depth:
            land_gather(f"l{i + 1}", flight, layer_keys(i + 1), i + 1, cur)

    dy, loss_part = loss_head("loss", cur, target)
    loss = lax.psum(loss_part.reshape(()), ("x", "y", "c"))

    part = {}
    sums = {}

    def grad_view(key, split):
        w = weights[key]
        return split(None, 0, w.shape[1], full=(N_CHIPS, w.shape[1], w.shape[2]), dtype=F32)

    def reduced_keys(i):
        return [k for k in layer_keys(i) if k != "pool_w_grp"]

    def pair_start(i, dep):
        srcs = [part[k, i] for k in reduced_keys(i)]
        plan = pair_plan(len(srcs))
        lands = [lax.empty(s.shape, s.dtype) for s in pair_lands(srcs)]
        return (plan,) + split_start(f"grads_l{i}_pair_start", plan, len(srcs), srcs + lands, dep)

    def pair_end_chip_start(i, flight, after):
        plan, send_sems, recv_sems, bufs, _ = flight
        bufs = split_wait(f"grads_l{i}_pair_wait", plan, send_sems, recv_sems, bufs, after)
        n = len(bufs) // 2
        parts = [pair_sum(f"grads_l{i}_pair_sum_{k}", bufs[j], bufs[n + j], place[1:], BF16) for j, k in enumerate(reduced_keys(i))]
        plan = chip_plan(n)
        return (plan,) + split_start(f"grads_l{i}_chip_start", plan, 3 * n, parts + [lax.empty(q.shape, q.dtype) for q in parts], after)

    def chip_end(i, flight, after):
        plan, send_sems, recv_sems, bufs, _ = flight
        bufs = split_wait(f"grads_l{i}_chip_wait", plan, send_sems, recv_sems, bufs, after)
        n = len(bufs) // 2
        for j, k in enumerate(reduced_keys(i)):
            sums[k] = chip_sum(f"grads_l{i}_chip_sum_{k}", bufs[n + j], bufs[j], place, False, into=sums.get(k),
                               layer=i if k in every_layer else i // 2, n_layers=weights[k].shape[0])

    small = {k: [None] * weights[k].shape[0] for k in names if k not in big or k == "pool_w_grp"}
    dcur = dy
    pair_flight = chip_flight = None
    for i in reversed(range(depth)):
        slot = i // 2
        sv = saved[i]
        de, dpre, dgb = ple_bwd(f"l{i}_ple_bwd", dcur, sv["gate"], sv["e"])
        small["ple_gate_b"][i] = dgb
        (part["ple_w", i],) = matmul(f"l{i}_d_ple_w", plain(p3[i]), plain(de), "tn", [grad_view("ple_w", colsplit)],
                                     deps=() if pair_flight is None else (pair_flight[-1],))
        (part["ple_gate_w", i],) = matmul(f"l{i}_d_ple_gate_w", plain(sv["x2_bf"]), plain(dpre), "tn",
                                          [grad_view("ple_gate_w", rowsplit)])
        (dx2b,) = matmul(f"l{i}_dx2", plain(dpre), rowsplit(wg["ple_gate_w", i], 0, d // N_CHIPS), "nt",
                         [plain(shape=(t, d), dtype=F32)])
        ds2, ds2_bf, dg, db = ln_bwd(f"l{i}_ln_mlp_bwd", 1.0, dcur, dx2b, sv["s2"], row(ln_mlp_g, i))
        small["ln_mlp_g"][i], small["ln_mlp_b"][i] = dg, db
        (part["mlp_w2", i],) = matmul(f"l{i}_d_mlp_w2", plain(sv["hh"]), plain(ds2_bf), "tn", [grad_view("mlp_w2", rowsplit)], **mlp_tiles[i])
        (dhpre,) = matmul(f"l{i}_dh", plain(ds2_bf), rowsplit(wg["mlp_w2", i], 0, d_ff // N_CHIPS), "nt",
                          [plain(shape=(t, d_ff), dtype=BF16)], epilogue=lambda acc, hr_t: (acc * (2.0 * hr_t),),
                          tiles=[plain(sv["hr"])], **mlp_tiles[i])
        if pair_flight is not None:
            chip_flight = pair_end_chip_start(i + 1, pair_flight, dhpre)
        (part["mlp_w1", i],) = matmul(f"l{i}_d_mlp_w1", plain(sv["x1_bf"]), plain(dhpre), "tn", [grad_view("mlp_w1", colsplit)],
                                      deps=() if chip_flight is None else (chip_flight[-1],), **mlp_tiles[i])
        (dx1b,) = matmul(f"l{i}_dx1", plain(dhpre), colsplit(wg["mlp_w1", i], 0, d), "nt", [plain(shape=(t, d), dtype=F32)],
                         **mlp_tiles[i])
        ds1, ds1_bf, dg, db = ln_bwd(f"l{i}_ln_mix_bwd", alpha, ds2, dx1b, sv["s1"], row(ln_mix_g, i))
        small["ln_mix_g"][i], small["ln_mix_b"][i] = dg, db
        residual = lambda acc, ds_t: (alpha * ds_t + acc,)
        if i % 2 == 0:
            (part["lru_w_out", i],) = matmul(f"l{i}_d_lru_out", plain(sv["act"]), plain(ds1_bf), "tn",
                                             [grad_view("lru_w_out", rowsplit)])
            (dhg,) = matmul(f"l{i}_dhg", plain(ds1_bf), rowsplit(wg["lru_w_out", i], 0, d_rnn // N_CHIPS), "nt",
                            [plain(shape=(t, d_rnn), dtype=F32)])
            dproj, dcw, dcb, dba, dbx, dlam, dwa, dwx = lru_bwd(
                f"l{i}_lru_bwd", sv["proj"], dhg, conv_w_full[slot], row(lru_conv_b, slot), wa_bf[slot], row(lru_ba, slot),
                wx_bf[slot], row(lru_bx, slot), row(lru_lambda, slot))
            for key, val in (("lru_conv_w", dcw), ("lru_conv_b", dcb), ("lru_ba", dba), ("lru_bx", dbx),
                             ("lru_lambda", dlam), ("lru_wa", dwa), ("lru_wx", dwx)):
                small[key][slot] = val
            dproj_v = colsplit(dproj, 0, t, n=2)
            (part["lru_w_in", i],) = matmul(f"l{i}_d_lru_in", plain(sv["x_bf"]), dproj_v, "tn", [grad_view("lru_w_in", colsplit)])
            (dcur,) = matmul(f"l{i}_dx", dproj_v, colsplit(wg["lru_w_in", i], 0, d), "nt",
                             [plain(shape=(t, d), dtype=F32)], epilogue=residual, tiles=[plain(ds1)])
        else:
            (part["pool_w_out", i],) = matmul(f"l{i}_d_pool_out", plain(sv["act"]), plain(ds1_bf), "tn",
                                              [grad_view("pool_w_out", rowsplit)])
            (dzs,) = matmul(f"l{i}_dzs", plain(ds1_bf), rowsplit(wg["pool_w_out", i], 0, d // N_CHIPS), "nt",
                            [plain(shape=(t, d), dtype=F32)])
            du, dwg, dbg, dsc = pool_bwd(f"l{i}_pool_bwd", sv["u"], dzs, w_grp_full(i), b_grp_full[slot], scale_full[slot])
            small["pool_w_grp"][slot], small["pool_b_grp"][slot], small["pool_scale"][slot] = dwg, dbg, dsc
            (part["pool_w_in", i],) = matmul(f"l{i}_d_pool_in", plain(sv["x_bf"]), plain(du), "tn", [grad_view("pool_w_in", rowsplit)])
            (dcur,) = matmul(f"l{i}_dx", plain(du), rowsplit(wg["pool_w_in", i], 0, d // N_CHIPS), "nt",
                             [plain(shape=(t, d), dtype=F32)], epilogue=residual, tiles=[plain(ds1)])
        if chip_flight is not None:
            chip_end(i + 1, chip_flight, dcur)
        pair_flight = pair_start(i, dcur)
    grad_x = dcur.reshape(x.shape)
    chip_flight = pair_end_chip_start(0, pair_flight, dcur)

    big_w = [k for k in big if k != "pool_w_grp"]
    small_keys = [k for k in names if k not in big_w]
    small_full = [jnp.stack(small[k]).reshape((weights[k].shape[0],) + tuple(
        s * (N_CHIPS if ax in _sharded_axis(k) else 1) for ax, s in enumerate(weights[k].shape[1:], 1))) for k in small_keys]
    full_shapes = [a.shape for a in small_full]
    blob = _pack(small_full, 64)
    blob4 = blob.reshape(N_CHIPS, blob.shape[0] // N_CHIPS, BLOB_COLS)
    (blob_mine,) = reduce_to_shards("small", [blob4], [F32], [True], place, deps=(chip_flight[-1],))
    (blob_all,) = all_gather_chips("gather_small_grads", [blob_mine])
    small_grads = dict(zip(small_keys, _unpack(blob_all.reshape(blob.shape), full_shapes)))
    for k in small_keys:
        for ax in _sharded_axis(k):
            n = weights[k].shape[ax]
            small_grads[k] = lax.dynamic_slice_in_dim(small_grads[k], chip * n, n, axis=ax)
    chip_end(0, chip_flight, blob_all)
    reduced = pair_gather("grads_pair_gather", [sums[k] for k in big_w], [False] * len(big_w), [weights[k].shape[0] for k in big_w])
    grads = {k: reduced[j].reshape(weights[k].shape) for j, k in enumerate(big_w)}
    grads.update(small_grads)

    delta, new_m, new_v = {}, {}, {}
    for k in big_w:
        dl, nm, nv = adamw("adamw_" + k, flat2(weights[k]), flat2(grads[k]), flat2(mom_m[k]), flat2(mom_v[k]))
        delta[k], new_m[k], new_v[k] = (a.reshape(weights[k].shape) for a in (dl, nm, nv))
    shapes = [weights[k].shape for k in small_keys]
    dl, nm, nv = adamw("adamw_small", *[_pack([src[k] for k in small_keys], 8) for src in (weights, grads, mom_m, mom_v)])
    for out, blob_out in ((delta, dl), (new_m, nm), (new_v, nv)):
        out.update(zip(small_keys, _unpack(blob_out, shapes)))

    return (loss, grad_x, *[grads[k] for k in names], *[delta[k] for k in names],
            *[new_m[k] for k in names], *[new_v[k] for k in names])


def _sharded_axis(key):
    return {"lru_conv_w": (2,), "pool_w_grp": (2,), "pool_b_grp": (1,), "pool_scale": (1,)}.get(key, ())
```

```python
import functools
import math

import jax
import jax.numpy as jnp
from jax import lax
from jax.experimental import pallas as pl
from jax.experimental.pallas import tpu as pltpu

F32 = jnp.float32
BF16 = jnp.bfloat16

N_CHIPS = 4
LRU_BW = 128
LRU_C = 8.0
CONV_WIDTH = 4
POOL_WINDOWS = (2, 4, 8, 16)
POOL_HALO = 16
CONV_HALO = 8
LN_EPS = 1e-5
ADAM_LR = 0.001
ADAM_B1 = 0.9
ADAM_B2 = 0.999
ADAM_EPS = 1e-08
ADAM_WD = 0.01
ADAM_STEP = 10
GELU_C = math.sqrt(2.0 / math.pi)
GELU_K = 0.044715
VMEM_LIMIT_BYTES = 56 * 1024 * 1024
MESH = pl.DeviceIdType.MESH
BLOB_COLS = 1024


def _params(*sem):
    return pltpu.CompilerParams(dimension_semantics=tuple(sem), vmem_limit_bytes=VMEM_LIMIT_BYTES)


def _tile(unit, pref, align=128):
    if unit <= pref:
        return unit
    for d in range(2, unit + 1):
        if unit % d == 0 and unit // d <= pref and (unit // d) % align == 0:
            return unit // d
    raise ValueError((unit, pref, align))


class View:
    def __init__(self, arr, shape, row_unit, col_unit, block_fn, full=None, dtype=None):
        self.arr, self.shape, self.row_unit, self.col_unit, self.block_fn = arr, shape, row_unit, col_unit, block_fn
        self.full = full if full is not None else arr.shape
        self.dtype = dtype if dtype is not None else arr.dtype

    def spec(self, tr, tc, f):
        block, idx = self.block_fn(tr, tc)
        return pl.BlockSpec(block, lambda *g: idx(*f(*g)))


def plain(arr=None, shape=None, dtype=None):
    shape = arr.shape if arr is not None else shape
    return View(arr, shape, shape[0], shape[1], lambda tr, tc: ((tr, tc), lambda rt, ct: (rt, ct)), full=shape, dtype=dtype)


def colsplit(arr, layer, rows, n=N_CHIPS, full=None, dtype=None):
    full = arr.shape if arr is not None else full
    c = full[2]

    def block_fn(tr, tc):
        assert rows % tr == 0 and c % tc == 0, (rows, tr, c, tc)
        per, rpl = c // tc, rows // tr
        return (None, tr, tc), lambda rt, ct: (ct // per, layer * rpl + rt, ct % per)

    return View(arr, (rows, n * c), rows, c, block_fn, full=full, dtype=dtype)


def rowsplit(arr, layer, rows, n=N_CHIPS, full=None, dtype=None):
    full = arr.shape if arr is not None else full
    c = full[2]

    def block_fn(tr, tc):
        assert rows % tr == 0 and c % tc == 0, (rows, tr, c, tc)
        per = rows // tr
        return (None, tr, tc), lambda rt, ct: (rt // per, layer * per + rt % per, ct)

    return View(arr, (n * rows, c), rows, c, block_fn, full=full, dtype=dtype)


def rowsplit_whole(arr):
    n, rows, c = arr.shape

    def block_fn(tr, tc):
        assert tr == n * rows and c % tc == 0, (tr, n, rows, c, tc)
        return (n, rows, tc), lambda rt, ct: (0, 0, ct)

    return View(arr, (n * rows, c), n * rows, c, block_fn)


def matmul(name, a, b, mode, outs, epilogue=None, tiles=(), rows=(), deps=(), pm=1024, pn=1024, pk=1024):
    if mode == "nn":
        (m, k), (k2, n) = a.shape, b.shape
        um, uk, un = a.row_unit, min(a.col_unit, b.row_unit), b.col_unit
        dims = (((1,), (0,)), ((), ()))
    elif mode == "nt":
        (m, k), (n, k2) = a.shape, b.shape
        um, uk, un = a.row_unit, min(a.col_unit, b.col_unit), b.row_unit
        dims = (((1,), (1,)), ((), ()))
    else:
        (k, m), (k2, n) = a.shape, b.shape
        um, uk, un = a.col_unit, min(a.row_unit, b.row_unit), b.col_unit
        dims = (((0,), (0,)), ((), ()))
    assert k == k2, (name, a.shape, b.shape)
    for o in list(outs) + list(tiles):
        assert o.shape == (m, n), (name, o.shape, m, n)
        um, un = min(um, o.row_unit), min(un, o.col_unit)
    tm, tn, tk = _tile(um, pm), _tile(un, pn), _tile(uk, pk)
    assert m % tm == 0 and n % tn == 0 and k % tk == 0, (name, m, n, k, tm, tn, tk)
    gm, gn, gk = m // tm, n // tn, k // tk

    if mode == "nn":
        a_spec = a.spec(tm, tk, lambda i, j, kk: (i, kk))
        b_spec = b.spec(tk, tn, lambda i, j, kk: (kk, j))
    elif mode == "nt":
        a_spec = a.spec(tm, tk, lambda i, j, kk: (i, kk))
        b_spec = b.spec(tn, tk, lambda i, j, kk: (j, kk))
    else:
        a_spec = a.spec(tk, tm, lambda i, j, kk: (kk, i))
        b_spec = b.spec(tk, tn, lambda i, j, kk: (kk, j))
    tile_specs = [t.spec(tm, tn, lambda i, j, kk: (i, j)) for t in tiles]
    row_specs = [pl.BlockSpec((1, tn), lambda i, j, kk: (0, j)) for _ in rows]
    in_place = [o for o in outs if o.arr is not None]
    alias_specs = [pl.BlockSpec(memory_space=pl.ANY) for _ in in_place]
    out_specs = [o.spec(tm, tn, lambda i, j, kk: (i, j)) for o in outs]
    n_in = 2 + len(tiles) + len(rows)
    aliases = {}
    for o_idx, o in enumerate(outs):
        if o.arr is not None:
            aliases[n_in + in_place.index(o)] = o_idx
    n_t, n_r, n_a, n_o = len(tiles), len(rows), len(in_place) + len(deps), len(outs)
    dep_specs = [pl.BlockSpec(memory_space=pl.ANY) for _ in deps]

    def body(*refs):
        a_ref, b_ref = refs[0], refs[1]
        tile_refs = refs[2:2 + n_t]
        row_refs = refs[2 + n_t:2 + n_t + n_r]
        out_refs = refs[2 + n_t + n_r + n_a:2 + n_t + n_r + n_a + n_o]
        acc_ref = refs[-1] if gk > 1 else None

        def finish(acc):
            extra = [t[...] for t in tile_refs] + [r[...] for r in row_refs]
            res = epilogue(acc, *extra) if epilogue is not None else (acc,)
            for o_ref, r in zip(out_refs, res):
                o_ref[...] = r.astype(o_ref.dtype)

        b_tile = b_ref[...]
        b_tile = b_tile.reshape(-1, b_tile.shape[-1])
        prod = lax.dot_general(a_ref[...].astype(BF16), b_tile.astype(BF16), dims, preferred_element_type=F32)
        if gk == 1:
            finish(prod)
        else:
            kk = pl.program_id(2)

            @pl.when(kk == 0)
            def _():
                acc_ref[...] = prod

            @pl.when(kk > 0)
            def _():
                acc_ref[...] += prod

            @pl.when(kk == gk - 1)
            def _():
                finish(acc_ref[...])

    res = pl.pallas_call(
        body,
        name=name,
        grid=(gm, gn, gk),
        in_specs=[a_spec, b_spec] + tile_specs + row_specs + alias_specs + dep_specs,
        out_specs=out_specs,
        out_shape=[jax.ShapeDtypeStruct(o.full, o.dtype) for o in outs],
        scratch_shapes=[pltpu.VMEM((tm, tn), F32)] if gk > 1 else [],
        input_output_aliases=aliases,
        compiler_params=_params("parallel", "parallel", "arbitrary"),
    )(a.arr, b.arr, *[t.arr for t in tiles], *rows, *[o.arr for o in in_place], *deps)
    return res


def rows_call(name, fn, tiled, vecs, tiled_out, acc_out, tr=512):
    t = tiled[0].shape[0]
    tr = min(tr, t)
    assert t % tr == 0
    n1, n2, n3 = len(tiled), len(vecs), len(tiled_out)

    def body(*refs):
        fn(pl.program_id(0), refs[:n1], refs[n1:n1 + n2], refs[n1 + n2:n1 + n2 + n3], refs[n1 + n2 + n3:])

    return pl.pallas_call(
        body,
        name=name,
        grid=(t // tr,),
        in_specs=[pl.BlockSpec((tr, x.shape[1]), lambda i: (i, 0)) for x in tiled]
        + [pl.BlockSpec(v.shape, lambda i: (0, 0)) for v in vecs],
        out_specs=[pl.BlockSpec((tr, c), lambda i: (i, 0)) for c, _ in tiled_out]
        + [pl.BlockSpec(s, lambda i: (0, 0)) for s, _ in acc_out],
        out_shape=[jax.ShapeDtypeStruct((t, c), d) for c, d in tiled_out] + [jax.ShapeDtypeStruct(s, d) for s, d in acc_out],
        compiler_params=_params("arbitrary" if acc_out else "parallel"),
    )(*tiled, *vecs)


def _accumulate(step, ref, val):
    @pl.when(step == 0)
    def _():
        ref[...] = val

    @pl.when(step > 0)
    def _():
        ref[...] += val


def _ln_stats(s):
    mu = jnp.mean(s, axis=-1, keepdims=True)
    d = s - mu
    var = jnp.mean(d * d, axis=-1, keepdims=True)
    rstd = lax.rsqrt(var + LN_EPS)
    return d * rstd, rstd


def ln_fwd(name, alpha, x_in, m, g, b):
    d = x_in.shape[1]

    def fn(step, tiled, vecs, outs, accs):
        s = alpha * tiled[0][...] + tiled[1][...]
        xhat, _ = _ln_stats(s)
        y = xhat * vecs[0][...] + vecs[1][...]
        outs[0][...] = y
        outs[1][...] = y.astype(BF16)
        outs[2][...] = s

    return rows_call(name, fn, [x_in, m], [g, b], [(d, F32), (d, BF16), (d, F32)], [])


def ln_bwd(name, ca, da, db, s, g):
    d = s.shape[1]

    def fn(step, tiled, vecs, outs, accs):
        dx = ca * tiled[0][...] + tiled[1][...]
        xhat, rstd = _ln_stats(tiled[2][...])
        dxh = dx * vecs[0][...]
        ds = rstd * (dxh - jnp.mean(dxh, axis=-1, keepdims=True) - xhat * jnp.mean(dxh * xhat, axis=-1, keepdims=True))
        outs[0][...] = ds
        outs[1][...] = ds.astype(BF16)
        _accumulate(step, accs[0], jnp.sum(dx * xhat, axis=0, keepdims=True))
        _accumulate(step, accs[1], jnp.sum(dx, axis=0, keepdims=True))

    return rows_call(name, fn, [da, db, s], [g], [(d, F32), (d, BF16)], [((1, d), F32), ((1, d), F32)])


def ple_bwd(name, dx3, gate, e):
    d = dx3.shape[1]

    def fn(step, tiled, vecs, outs, accs):
        dx, gt, ev = tiled[0][...], tiled[1][...], tiled[2][...]
        dpre = dx * ev * gt * (1.0 - gt)
        outs[0][...] = (dx * gt).astype(BF16)
        outs[1][...] = dpre.astype(BF16)
        _accumulate(step, accs[0], jnp.sum(dpre, axis=0, keepdims=True))

    return rows_call(name, fn, [dx3, gate, e], [], [(d, BF16), (d, BF16)], [((1, d), F32)])


def loss_head(name, y, target):
    t, d = y.shape

    def fn(step, tiled, vecs, outs, accs):
        err = tiled[0][...] - tiled[1][...]
        outs[0][...] = err * (1.0 / d)
        part = jnp.sum(jnp.sum(err * err, axis=1, keepdims=True), axis=0, keepdims=True) * (0.5 / d)
        _accumulate(step, accs[0], part)

    return rows_call(name, fn, [y, target], [], [(d, F32)], [((1, 1), F32)])


def _softplus(z):
    return jnp.maximum(z, 0.0) + jnp.log1p(jnp.exp(-jnp.abs(z)))


def _gelu(y):
    th = jnp.tanh(GELU_C * (y + GELU_K * (y * y * y)))
    cdf = 0.5 * (1.0 + th)
    return y * cdf, cdf + 0.5 * y * (1.0 - th * th) * (GELU_C * (1.0 + 3.0 * GELU_K * y * y))


def _up(win, k):
    return pltpu.roll(win, win.shape[0] - k, 0)


def _down(win, k):
    return pltpu.roll(win, k, 0)


def _lru_gates(win, row0, cw_ref, cb, wa, ba, wx, bx, sp):
    h = CONV_HALO
    u = (cb + cw_ref[3:4, :] * win[h:] + cw_ref[2:3, :] * _down(win, 1)[h:]
         + cw_ref[1:2, :] * _down(win, 2)[h:] + cw_ref[0:1, :] * _down(win, 3)[h:])
    ub = u.astype(BF16)
    r = jax.nn.sigmoid(jnp.dot(ub, wa, preferred_element_type=F32) + ba)
    ig = jax.nn.sigmoid(jnp.dot(ub, wx, preferred_element_type=F32) + bx)
    log_a = (-LRU_C) * r * sp
    a = jnp.exp(log_a)
    mult = jnp.sqrt(-jnp.tanh(log_a) * (a * a + 1.0))
    first = (row0 + lax.broadcasted_iota(jnp.int32, u.shape, 0)) == 0
    mult = jnp.where(first, 1.0, mult)
    return u, r, ig, a, mult, first


def _block_scan(a, b, reverse):
    n = a.shape[0]
    a, b = a.reshape(n // 8, 8, LRU_BW), b.reshape(n // 8, 8, LRU_BW)
    pos = lax.broadcasted_iota(jnp.int32, a.shape, 1)
    for s in (1, 2, 4):
        keep = (pos >= 8 - s) if reverse else (pos < s)
        by = 8 - s if reverse else s
        b = jnp.where(keep, b, a * pltpu.roll(b, by, 1) + b)
        a = jnp.where(keep, a, a * pltpu.roll(a, by, 1))
    return a.reshape(n, LRU_BW), b.reshape(n, LRU_BW)


def _carry_scan(a_ref, b_ref, out_ref, out_off, t, reverse):
    groups, per_step = t // 8, 8

    def step(j, h):
        for k in range(per_step):
            g = j * per_step + k
            r0 = pl.multiple_of((groups - 1 - g if reverse else g) * 8, 8)
            edge = r0 if reverse else r0 + 7
            h_out = a_ref[pl.ds(edge, 1), :] * h + b_ref[pl.ds(edge, 1), :]
            out_ref[pl.ds(pl.multiple_of(out_off + r0, 8), 8), :] = a_ref[pl.ds(r0, 8), :] * h + b_ref[pl.ds(r0, 8), :]
            h = h_out
        return h

    lax.fori_loop(0, groups // per_step, step, jnp.zeros((1, LRU_BW), F32))


def _lru_in_specs(t, heads):
    blk = lambda i: (0, i)
    return [
        pl.BlockSpec((2, t, LRU_BW), lambda i: (0, 0, i)),
        pl.BlockSpec((CONV_WIDTH, LRU_BW), blk),
        pl.BlockSpec((1, LRU_BW), blk),
        pl.BlockSpec((None, LRU_BW, LRU_BW), lambda i: (i, 0, 0)),
        pl.BlockSpec((1, LRU_BW), blk),
        pl.BlockSpec((None, LRU_BW, LRU_BW), lambda i: (i, 0, 0)),
        pl.BlockSpec((1, LRU_BW), blk),
        pl.BlockSpec((1, LRU_BW), blk),
    ]


def lru_fwd(name, proj, conv_w, conv_b, wa, ba, wx, bx, lam):
    _, t, c = proj.shape
    heads = c // LRU_BW
    rc = min(256, t)

    def body(proj_ref, cw_ref, cb_ref, wa_ref, ba_ref, wx_ref, bx_ref, lam_ref, out_ref, upad, a_s, b_s):
        upad[0:CONV_HALO, :] = jnp.zeros((CONV_HALO, LRU_BW), F32)
        upad[CONV_HALO:, :] = proj_ref[0]
        sp = _softplus(-lam_ref[...])
        cb, ba, bx, wa, wx = cb_ref[...], ba_ref[...], bx_ref[...], wa_ref[...], wx_ref[...]

        def gates(i, carry):
            r0 = pl.multiple_of(i * rc, rc)
            win = upad[pl.ds(r0, rc + CONV_HALO), :]
            u, r, ig, a, mult, _ = _lru_gates(win, r0, cw_ref, cb, wa, ba, wx, bx, sp)
            rows = pl.ds(r0, rc)
            a_s[rows, :], b_s[rows, :] = _block_scan(a, mult * (ig * u), False)
            return carry

        lax.fori_loop(0, t // rc, gates, 0)
        _carry_scan(a_s, b_s, b_s, 0, t, False)

        def gate_out(i, carry):
            r0 = pl.multiple_of(i * rc, rc)
            gy, _ = _gelu(proj_ref[1, pl.ds(r0, rc), :])
            out_ref[pl.ds(r0, rc), :] = (b_s[pl.ds(r0, rc), :] * gy).astype(BF16)
            return carry

        lax.fori_loop(0, t // rc, gate_out, 0)

    return pl.pallas_call(
        body,
        name=name,
        grid=(heads,),
        in_specs=_lru_in_specs(t, heads),
        out_specs=pl.BlockSpec((t, LRU_BW), lambda i: (0, i)),
        out_shape=jax.ShapeDtypeStruct((t, c), BF16),
        scratch_shapes=[pltpu.VMEM((t + CONV_HALO, LRU_BW), F32)] + [pltpu.VMEM((t, LRU_BW), F32)] * 2,
        compiler_params=_params("parallel"),
    )(proj, conv_w, conv_b, wa, ba, wx, bx, lam)


def lru_bwd(name, proj, dhg, conv_w, conv_b, wa, ba, wx, bx, lam, deps=()):
    _, t, c = proj.shape
    heads = c // LRU_BW
    rc = min(256, t)
    h8 = CONV_HALO

    def body(proj_ref, dhg_ref, cw_ref, cb_ref, wa_ref, ba_ref, wx_ref, bx_ref, lam_ref,
             dproj_ref, dcw_ref, dcb_ref, dba_ref, dbx_ref, dlam_ref, dwa_ref, dwx_ref,
             upad, u_s, r_s, ig_s, apad, hpad, g_s, dupad, sa_s, sb_s):
        zeros8 = jnp.zeros((h8, LRU_BW), F32)
        upad[0:h8, :] = zeros8
        upad[h8:, :] = proj_ref[0]
        hpad[0:h8, :] = zeros8
        apad[t:, :] = zeros8
        dupad[t:, :] = zeros8
        lam = lam_ref[...]
        sp = _softplus(-lam)
        cb, ba, bx, wa, wx = cb_ref[...], ba_ref[...], bx_ref[...], wa_ref[...], wx_ref[...]

        def gates(i, carry):
            r0 = pl.multiple_of(i * rc, rc)
            win = upad[pl.ds(r0, rc + h8), :]
            u, r, ig, a, mult, _ = _lru_gates(win, r0, cw_ref, cb, wa, ba, wx, bx, sp)
            u_s[pl.ds(r0, rc), :] = u
            r_s[pl.ds(r0, rc), :] = r
            ig_s[pl.ds(r0, rc), :] = ig
            rows = pl.ds(r0, rc)
            apad[rows, :] = a
            sa_s[rows, :], sb_s[rows, :] = _block_scan(a, mult * (ig * u), False)
            return carry

        lax.fori_loop(0, t // rc, gates, 0)
        _carry_scan(sa_s, sb_s, hpad, h8, t, False)

        def out_gate(i, carry):
            r0 = pl.multiple_of(i * rc, rc)
            gy, dgy = _gelu(proj_ref[1, pl.ds(r0, rc), :])
            dh = dhg_ref[pl.ds(r0, rc), :]
            hh = hpad[pl.ds(pl.multiple_of(r0 + h8, 8), rc), :]
            dproj_ref[1, pl.ds(r0, rc), :] = (dh * hh * dgy).astype(BF16)
            rows = pl.ds(r0, rc)
            a_next = _up(apad[pl.ds(r0, rc + h8), :], 1)[:rc]
            sa_s[rows, :], sb_s[rows, :] = _block_scan(a_next, dh * gy, True)
            return carry

        lax.fori_loop(0, t // rc, out_gate, 0)
        _carry_scan(sa_s, sb_s, g_s, 0, t, True)

        zrow = jnp.zeros((1, LRU_BW), F32)
        zmat = jnp.zeros((LRU_BW, LRU_BW), F32)

        def grads(i, carry):
            dsp, dba, dbx, dwa, dwx = carry
            r0 = pl.multiple_of(i * rc, rc)
            g = g_s[pl.ds(r0, rc), :]
            u, r, ig, a = u_s[pl.ds(r0, rc), :], r_s[pl.ds(r0, rc), :], ig_s[pl.ds(r0, rc), :], apad[pl.ds(r0, rc), :]
            hprev = _down(hpad[pl.ds(r0, rc + h8), :], 1)[h8:]
            first = (r0 + lax.broadcasted_iota(jnp.int32, u.shape, 0)) == 0
            log_a = (-LRU_C) * r * sp
            mult = jnp.where(first, 1.0, jnp.sqrt(-jnp.tanh(log_a) * (a * a + 1.0)))
            dmult = jnp.where(first, 0.0, g * (ig * u))
            dlog_a = g * hprev * a - dmult * (a * a) / mult
            dr = dlog_a * ((-LRU_C) * sp)
            dpre_r = dr * r * (1.0 - r)
            dpre_i = (g * mult * u) * ig * (1.0 - ig)
            pr, pi, ub = dpre_r.astype(BF16), dpre_i.astype(BF16), u.astype(BF16)
            nt = (((1,), (1,)), ((), ()))
            tn = (((0,), (0,)), ((), ()))
            du = (g * mult * ig + lax.dot_general(pr, wa, nt, preferred_element_type=F32)
                  + lax.dot_general(pi, wx, nt, preferred_element_type=F32))
            dupad[pl.ds(r0, rc), :] = du
            return (dsp + jnp.sum(dlog_a * ((-LRU_C) * r), axis=0, keepdims=True),
                    dba + jnp.sum(dpre_r, axis=0, keepdims=True),
                    dbx + jnp.sum(dpre_i, axis=0, keepdims=True),
                    dwa + lax.dot_general(ub, pr, tn, preferred_element_type=F32),
                    dwx + lax.dot_general(ub, pi, tn, preferred_element_type=F32))

        dsp, dba, dbx, dwa, dwx = lax.fori_loop(0, t // rc, grads, (zrow, zrow, zrow, zmat, zmat))
        dba_ref[...] = dba
        dbx_ref[...] = dbx
        dwa_ref[...] = dwa
        dwx_ref[...] = dwx
        dlam_ref[...] = -dsp * jax.nn.sigmoid(-lam)

        def conv_back(i, carry):
            dcb, d0, d1, d2, d3 = carry
            r0 = pl.multiple_of(i * rc, rc)
            dwin = dupad[pl.ds(r0, rc + h8), :]
            du = dwin[:rc]
            du0 = (cw_ref[3:4, :] * du + cw_ref[2:3, :] * _up(dwin, 1)[:rc]
                   + cw_ref[1:2, :] * _up(dwin, 2)[:rc] + cw_ref[0:1, :] * _up(dwin, 3)[:rc])
            dproj_ref[0, pl.ds(r0, rc), :] = du0.astype(BF16)
            win = upad[pl.ds(r0, rc + h8), :]
            red = lambda v: jnp.sum(v, axis=0, keepdims=True)
            return (dcb + red(du), d0 + red(du * _down(win, 3)[h8:]), d1 + red(du * _down(win, 2)[h8:]),
                    d2 + red(du * _down(win, 1)[h8:]), d3 + red(du * win[h8:]))

        dcb, d0, d1, d2, d3 = lax.fori_loop(0, t // rc, conv_back, (zrow,) * 5)
        dcb_ref[...] = dcb
        dcw_ref[0:1, :] = d0
        dcw_ref[1:2, :] = d1
        dcw_ref[2:3, :] = d2
        dcw_ref[3:4, :] = d3

    blk = lambda i: (0, i)
    vec = jax.ShapeDtypeStruct((1, c), F32)
    mat = jax.ShapeDtypeStruct((heads, LRU_BW, LRU_BW), F32)
    full = lambda: pltpu.VMEM((t, LRU_BW), F32)
    padded = lambda: pltpu.VMEM((t + h8, LRU_BW), F32)
    return pl.pallas_call(
        lambda *refs: body(*refs[len(deps):]),
        name=name,
        grid=(heads,),
        in_specs=[_ANY] * len(deps) + _lru_in_specs(t, heads)[:1] + [pl.BlockSpec((t, LRU_BW), blk)] + _lru_in_specs(t, heads)[1:],
        out_specs=[pl.BlockSpec((2, t, LRU_BW), lambda i: (0, 0, i)), pl.BlockSpec((CONV_WIDTH, LRU_BW), blk)]
        + [pl.BlockSpec((1, LRU_BW), blk)] * 4 + [pl.BlockSpec((None, LRU_BW, LRU_BW), lambda i: (i, 0, 0))] * 2,
        out_shape=[jax.ShapeDtypeStruct((2, t, c), BF16), jax.ShapeDtypeStruct((CONV_WIDTH, c), F32), vec, vec, vec, vec, mat, mat],
        scratch_shapes=[padded(), full(), full(), full(), padded(), padded(), full(), padded()] + [full()] * 2,
        compiler_params=_params("parallel"),
    )(*deps, proj, dhg, conv_w, conv_b, wa, ba, wx, bx, lam)


def _pick_level(g, levels):
    out = levels[-1]
    for k in range(len(levels) - 2, -1, -1):
        out = jnp.where(g == k, levels[k], out)
    return out


def _pool_z(win, g, row0, rc):
    levels, cur = [], win
    for k in range(len(POOL_WINDOWS)):
        cur = cur + _down(cur, 1 << k)
        levels.append(cur[POOL_HALO:])
    tot = _pick_level(g, levels)
    width = jnp.left_shift(2, g)
    row = row0 + lax.broadcasted_iota(jnp.int32, tot.shape, 0)
    cnt = jnp.minimum(row + 1, width).astype(F32)
    return tot / cnt - win[POOL_HALO:], cnt


def _pool_specs(t, gw):
    blk = lambda g: (0, g)
    return [pl.BlockSpec((t, gw), blk), pl.BlockSpec((None, gw, gw), lambda g: (g, 0, 0)),
            pl.BlockSpec((1, gw), blk), pl.BlockSpec((1, gw), blk)]


def pool_fwd(name, u, w_grp, b_grp, scale):
    t, d = u.shape
    gw = d // len(POOL_WINDOWS)
    rc = min(256, t)

    def body(u_ref, wg_ref, bg_ref, sc_ref, out_ref, upad):
        g = pl.program_id(0)
        upad[0:POOL_HALO, :] = jnp.zeros((POOL_HALO, gw), F32)
        upad[POOL_HALO:, :] = u_ref[...]
        wg, bg, sc = wg_ref[...], bg_ref[...], sc_ref[...]

        def chunk(i, carry):
            r0 = pl.multiple_of(i * rc, rc)
            z, _ = _pool_z(upad[pl.ds(r0, rc + POOL_HALO), :], g, r0, rc)
            z2 = jnp.dot(z.astype(BF16), wg, preferred_element_type=F32) + bg
            out_ref[pl.ds(r0, rc), :] = (z2 * sc).astype(BF16)
            return carry

        lax.fori_loop(0, t // rc, chunk, 0)

    return pl.pallas_call(
        body,
        name=name,
        grid=(len(POOL_WINDOWS),),
        in_specs=_pool_specs(t, gw),
        out_specs=pl.BlockSpec((t, gw), lambda g: (0, g)),
        out_shape=jax.ShapeDtypeStruct((t, d), BF16),
        scratch_shapes=[pltpu.VMEM((t + POOL_HALO, gw), F32)],
        compiler_params=_params("parallel"),
    )(u, w_grp, b_grp, scale)


def pool_bwd(name, u, dzs, w_grp, b_grp, scale, deps=()):
    t, d = u.shape
    gw = d // len(POOL_WINDOWS)
    rc = min(256, t)

    def body(u_ref, dzs_ref, wg_ref, bg_ref, sc_ref, du_ref, dwg_ref, dbg_ref, dsc_ref, upad, qpad, dz_s):
        g = pl.program_id(0)
        upad[0:POOL_HALO, :] = jnp.zeros((POOL_HALO, gw), F32)
        upad[POOL_HALO:, :] = u_ref[...]
        qpad[t:, :] = jnp.zeros((POOL_HALO, gw), F32)
        wg, bg, sc = wg_ref[...], bg_ref[...], sc_ref[...]
        zrow = jnp.zeros((1, gw), F32)

        def chunk(i, carry):
            dsc, dbg, dwg = carry
            r0 = pl.multiple_of(i * rc, rc)
            z, cnt = _pool_z(upad[pl.ds(r0, rc + POOL_HALO), :], g, r0, rc)
            zb = z.astype(BF16)
            z2 = jnp.dot(zb, wg, preferred_element_type=F32) + bg
            dzs = dzs_ref[pl.ds(r0, rc), :]
            dz2 = dzs * sc
            d2b = dz2.astype(BF16)
            dz = lax.dot_general(d2b, wg, (((1,), (1,)), ((), ())), preferred_element_type=F32)
            dz_s[pl.ds(r0, rc), :] = dz
            qpad[pl.ds(r0, rc), :] = dz / cnt
            return (dsc + jnp.sum(dzs * z2, axis=0, keepdims=True), dbg + jnp.sum(dz2, axis=0, keepdims=True),
                    dwg + lax.dot_general(zb, d2b, (((0,), (0,)), ((), ())), preferred_element_type=F32))

        dsc, dbg, dwg = lax.fori_loop(0, t // rc, chunk, (zrow, zrow, jnp.zeros((gw, gw), F32)))
        dsc_ref[...] = dsc
        dbg_ref[...] = dbg
        dwg_ref[...] = dwg

        def spread(i, carry):
            r0 = pl.multiple_of(i * rc, rc)
            levels, cur = [], qpad[pl.ds(r0, rc + POOL_HALO), :]
            for k in range(len(POOL_WINDOWS)):
                cur = cur + _up(cur, 1 << k)
                levels.append(cur[:rc])
            du_ref[pl.ds(r0, rc), :] = (_pick_level(g, levels) - dz_s[pl.ds(r0, rc), :]).astype(BF16)
            return carry

        lax.fori_loop(0, t // rc, spread, 0)

    blk = lambda g: (0, g)
    vec = jax.ShapeDtypeStruct((1, d), F32)
    return pl.pallas_call(
        lambda *refs: body(*refs[len(deps):]),
        name=name,
        grid=(len(POOL_WINDOWS),),
        in_specs=[_ANY] * len(deps) + _pool_specs(t, gw)[:1] + [pl.BlockSpec((t, gw), blk)] + _pool_specs(t, gw)[1:],
        out_specs=[pl.BlockSpec((t, gw), blk), pl.BlockSpec((None, gw, gw), lambda g: (g, 0, 0)),
                   pl.BlockSpec((1, gw), blk), pl.BlockSpec((1, gw), blk)],
        out_shape=[jax.ShapeDtypeStruct((t, d), BF16), jax.ShapeDtypeStruct((len(POOL_WINDOWS), gw, gw), F32), vec, vec],
        scratch_shapes=[pltpu.VMEM((t + POOL_HALO, gw), F32), pltpu.VMEM((t + POOL_HALO, gw), F32), pltpu.VMEM((t, gw), F32)],
        compiler_params=_params("parallel"),
    )(*deps, u, dzs, w_grp, b_grp, scale)


def _place():
    return lax.axis_index("x"), lax.axis_index("y"), lax.axis_index("c")


def _other_chips(x, y):
    return [(1 - x, y), (x, 1 - y), (1 - x, 1 - y)]


def _half(c, rows):
    h = rows // 2
    return pl.ds(pl.multiple_of(c * h, 8), h)


_ANY = pl.BlockSpec(memory_space=pl.ANY)


def into_block(name, shards, layer, r, me, dtype):
    c = shards.shape[1]
    tr = _tile(r, 512, 16)
    per = r // tr

    def body(me_ref, s_ref, o_ref):
        o_ref[...] = s_ref[...].astype(o_ref.dtype)

    return pl.pallas_call(
        body,
        name=name,
        grid_spec=pltpu.PrefetchScalarGridSpec(
            num_scalar_prefetch=1,
            grid=(per,),
            in_specs=[pl.BlockSpec((tr, c), lambda i, me_ref: (layer * per + i, 0))],
            out_specs=pl.BlockSpec((None, tr, c), lambda i, me_ref: (me_ref[0], i, 0)),
        ),
        out_shape=jax.ShapeDtypeStruct((N_CHIPS, r, c), dtype),
        compiler_params=_params("parallel"),
    )(me, shards)


_HBM = pl.BlockSpec(memory_space=pltpu.HBM)
_SEM = pl.BlockSpec(memory_space=pltpu.SEMAPHORE)


def _in_hbm(a):
    return pltpu.with_memory_space_constraint(a, pltpu.HBM)


def split_start(name, plan, n_copies, bufs, dep):
    n = len(bufs)

    def body(*refs):
        for cp in plan(refs[:n], refs[n + 1], refs[n + 2]):
            cp.start()
        refs[-1][...] = jnp.zeros_like(refs[-1])

    res = pl.pallas_call(
        body,
        name=name,
        in_specs=[_HBM] * n + [_ANY],
        out_specs=[_SEM, _SEM] + [_HBM] * n + [pl.BlockSpec(memory_space=pltpu.VMEM)],
        out_shape=[pltpu.SemaphoreType.DMA((n_copies,)), pltpu.SemaphoreType.DMA((n_copies,))]
        + [pltpu.HBM(b.shape, b.dtype) for b in bufs] + [jax.ShapeDtypeStruct((8, 128), F32)],
        input_output_aliases={i: 2 + i for i in range(n)},
        compiler_params=pltpu.CompilerParams(has_side_effects=pltpu.SideEffectType.DATAFLOW_SIDE_EFFECTING),
    )(*[_in_hbm(b) for b in bufs], dep)
    return res[0], res[1], list(res[2:2 + n]), res[-1]


def split_wait(name, plan, send_sems, recv_sems, bufs, after):
    n = len(bufs)

    def body(*refs):
        copies = plan(refs[:n], refs[n], refs[n + 1])
        for cp in copies:
            cp.wait_send()
        for cp in copies:
            cp.wait_recv()

    return pl.pallas_call(
        body,
        name=name,
        in_specs=[_HBM] * n + [_SEM, _SEM, _ANY],
        out_specs=[_HBM] * n,
        out_shape=[pltpu.HBM(b.shape, b.dtype) for b in bufs],
        input_output_aliases={i: i for i in range(n)},
        compiler_params=pltpu.CompilerParams(has_side_effects=pltpu.SideEffectType.DATAFLOW_SIDE_EFFECTING),
    )(*bufs, send_sems, recv_sems, after)


def gather_plan(n):
    def plan(bufs, send_sems, recv_sems):
        x, y, c = _place()
        copies = []
        for i in range(n):
            blk = bufs[i].at[2 * x + y, _half(c, bufs[i].shape[1]), :]
            for j, chip in enumerate(_other_chips(x, y)):
                copies.append(pltpu.make_async_remote_copy(
                    src_ref=blk, dst_ref=blk, send_sem=send_sems.at[3 * i + j], recv_sem=recv_sems.at[3 * i + j],
                    device_id=(*chip, c), device_id_type=MESH))
        return copies

    return plan


def pair_forward(name, bufs):
    n = len(bufs)

    def body(*refs):
        outs = refs[n:2 * n]
        send_sems, recv_sems = refs[2 * n:]
        x, y, c = _place()
        copies = []
        for i in range(n):
            for j, (cx, cy) in enumerate(_other_chips(x, y)):
                blk = outs[i].at[2 * cx + cy, _half(c, outs[i].shape[1]), :]
                copies.append(pltpu.make_async_remote_copy(
                    src_ref=blk, dst_ref=blk, send_sem=send_sems.at[3 * i + j], recv_sem=recv_sems.at[3 * i + j],
                    device_id=(x, y, 1 - c), device_id_type=MESH))
        for cp in copies:
            cp.start()
        for cp in copies:
            cp.wait()

    return pl.pallas_call(
        body,
        name=name,
        in_specs=[_ANY] * n,
        out_specs=[_ANY] * n,
        out_shape=[jax.ShapeDtypeStruct(b.shape, b.dtype) for b in bufs],
        input_output_aliases={i: i for i in range(n)},
        scratch_shapes=[pltpu.SemaphoreType.DMA((3 * n,)), pltpu.SemaphoreType.DMA((3 * n,))],
    )(*bufs)


def all_gather_chips(name, bufs):
    n = len(bufs)

    def body(*refs):
        outs = refs[n:2 * n]
        send_sems, recv_sems = refs[2 * n:]
        x, y, c = _place()
        me, sibling = 2 * x + y, (x, y, 1 - c)
        chips = _other_chips(x, y)

        def copy(i, slot, block, half, to):
            blk = outs[i].at[block, _half(half, outs[i].shape[1]), :]
            return pltpu.make_async_remote_copy(
                src_ref=blk, dst_ref=blk, send_sem=send_sems.at[i * 6 + slot], recv_sem=recv_sems.at[i * 6 + slot],
                device_id=to, device_id_type=MESH)

        first = [copy(i, j, me, c, (*chip, c)) for i in range(n) for j, chip in enumerate(chips)]
        for cp in first:
            cp.start()
        passed = []
        for i in range(n):
            for j, (cx, cy) in enumerate(chips):
                copy(i, j, 2 * cx + cy, c, (x, y, c)).wait_recv()
                fwd = copy(i, 3 + j, 2 * cx + cy, c, sibling)
                fwd.start()
                passed.append(fwd)
        for i in range(n):
            for j, (cx, cy) in enumerate(chips):
                copy(i, 3 + j, 2 * cx + cy, 1 - c, (x, y, c)).wait_recv()
        for cp in first + passed:
            cp.wait_send()

    return pl.pallas_call(
        body,
        name=name,
        in_specs=[_ANY] * n,
        out_specs=[_ANY] * n,
        out_shape=[jax.ShapeDtypeStruct(b.shape, b.dtype) for b in bufs],
        input_output_aliases={i: i for i in range(n)},
        scratch_shapes=[pltpu.SemaphoreType.DMA((6 * n,)), pltpu.SemaphoreType.DMA((6 * n,))],
    )(*bufs)


def pair_plan(n):
    def plan(bufs, send_sems, recv_sems):
        x, y, c = _place()
        return [pltpu.make_async_remote_copy(
            src_ref=bufs[i].at[:, _half(1 - c, bufs[i].shape[1]), :], dst_ref=bufs[n + i], send_sem=send_sems.at[i],
            recv_sem=recv_sems.at[i], device_id=(x, y, 1 - c), device_id_type=MESH) for i in range(n)]

    return plan


def chip_plan(n):
    def plan(bufs, send_sems, recv_sems):
        x, y, c = _place()
        copies = []
        for i in range(n):
            for j, (cx, cy) in enumerate(_other_chips(x, y)):
                copies.append(pltpu.make_async_remote_copy(
                    src_ref=bufs[i].at[2 * cx + cy], dst_ref=bufs[n + i].at[2 * x + y], send_sem=send_sems.at[3 * i + j],
                    recv_sem=recv_sems.at[3 * i + j], device_id=(cx, cy, c), device_id_type=MESH))
        return copies

    return plan


def exchange(name, plan, n_copies, srcs, land_shapes, deps=()):
    n, n_d = len(srcs), len(deps)

    def body(*refs):
        copies = plan(refs[:n] + refs[n + n_d:2 * n + n_d], refs[2 * n + n_d], refs[2 * n + n_d + 1])
        for cp in copies:
            cp.start()
        for cp in copies:
            cp.wait()

    return pl.pallas_call(
        body,
        name=name,
        in_specs=[_ANY] * (n + n_d),
        out_specs=[_ANY] * n,
        out_shape=land_shapes,
        scratch_shapes=[pltpu.SemaphoreType.DMA((n_copies,)), pltpu.SemaphoreType.DMA((n_copies,))],
    )(*srcs, *deps)


def pair_lands(grads):
    return [jax.ShapeDtypeStruct((g.shape[0], g.shape[1] // 2, g.shape[2]), g.dtype) for g in grads]


def pair_exchange(name, grads, deps=()):
    return exchange(name, pair_plan(len(grads)), len(grads), grads, pair_lands(grads), deps)


def chip_exchange(name, parts):
    return exchange(name, chip_plan(len(parts)), 3 * len(parts), parts, [jax.ShapeDtypeStruct(p.shape, p.dtype) for p in parts])


def pair_gather(name, bufs, blocked, layers):
    n = len(bufs)
    n_copies = sum(layers)

    def body(*refs):
        outs = refs[n:2 * n]
        send_sems, recv_sems = refs[2 * n:]
        x, y, c = _place()
        copies = []
        for i in range(n):
            buf = outs[i].at[2 * x + y] if blocked[i] else outs[i]
            r = buf.shape[0] // layers[i]
            for l in range(layers[i]):
                mine = buf.at[pl.ds(pl.multiple_of(l * r + c * (r // 2), 8), r // 2), :]
                copies.append(pltpu.make_async_remote_copy(
                    src_ref=mine, dst_ref=mine, send_sem=send_sems.at[len(copies)], recv_sem=recv_sems.at[len(copies)],
                    device_id=(x, y, 1 - c), device_id_type=MESH))
        for cp in copies:
            cp.start()
        for cp in copies:
            cp.wait()

    return pl.pallas_call(
        body,
        name=name,
        in_specs=[_ANY] * n,
        out_specs=[_ANY] * n,
        out_shape=[jax.ShapeDtypeStruct(b.shape, b.dtype) for b in bufs],
        input_output_aliases={i: i for i in range(n)},
        scratch_shapes=[pltpu.SemaphoreType.DMA((n_copies,)), pltpu.SemaphoreType.DMA((n_copies,))],
    )(*bufs)


def pair_sum(name, grad, recv, core, dtype):
    _, r, c = grad.shape
    h = r // 2
    th = _tile(h, 512, 16)
    per = h // th

    def body(core_ref, g_ref, r_ref, o_ref):
        o_ref[...] = (g_ref[...] + r_ref[...]).astype(o_ref.dtype)

    return pl.pallas_call(
        body,
        name=name,
        grid_spec=pltpu.PrefetchScalarGridSpec(
            num_scalar_prefetch=1,
            grid=(N_CHIPS, per),
            in_specs=[pl.BlockSpec((None, th, c), lambda k, i, core_ref: (k, core_ref[0] * per + i, 0)),
                      pl.BlockSpec((None, th, c), lambda k, i, core_ref: (k, i, 0))],
            out_specs=pl.BlockSpec((None, th, c), lambda k, i, core_ref: (k, i, 0)),
        ),
        out_shape=jax.ShapeDtypeStruct((N_CHIPS, h, c), dtype),
        compiler_params=_params("parallel", "parallel"),
    )(core, grad, recv)


def chip_sum(name, got, parts, place, blocked, into=None, layer=0, n_layers=1):
    _, h, c = parts.shape
    th = _tile(h, 256, 16)
    per = h // th

    def body(place_ref, q0, q1, q2, q3, p_ref, *rest):
        o_ref = rest[-1]
        me = place_ref[0]
        own = p_ref[...].astype(F32)
        v = [jnp.where(me == k, own, q[...].astype(F32)) for k, q in enumerate((q0, q1, q2, q3))]
        o_ref[...] = ((v[0] + v[1]) + v[2]) + v[3]

    def got_spec(k):
        return pl.BlockSpec((None, th, c), lambda i, pr: (jnp.where(pr[0] == k, (k + 1) % N_CHIPS, k), i, 0))

    if blocked:
        out_spec = pl.BlockSpec((None, th, c), lambda i, pr: (pr[0], pr[1] * per + i, 0))
        out_shape = jax.ShapeDtypeStruct((N_CHIPS, 2 * h, c), F32)
    else:
        out_spec = pl.BlockSpec((th, c), lambda i, pr: ((2 * layer + pr[1]) * per + i, 0))
        out_shape = jax.ShapeDtypeStruct((n_layers * 2 * h, c), F32)
    carried = [] if into is None else [into]
    return pl.pallas_call(
        body,
        name=name,
        grid_spec=pltpu.PrefetchScalarGridSpec(
            num_scalar_prefetch=1,
            grid=(per,),
            in_specs=[got_spec(k) for k in range(N_CHIPS)] + [pl.BlockSpec((None, th, c), lambda i, pr: (pr[0], i, 0))]
            + [_ANY] * len(carried),
            out_specs=out_spec,
        ),
        out_shape=out_shape,
        input_output_aliases={6: 0} if carried else {},
        compiler_params=_params("parallel"),
    )(place, got, got, got, got, parts, *carried)


def adamw(name, w, g, m, v):
    r, c = w.shape
    tr = _tile(r, 512, 8)
    c1 = 1.0 - ADAM_B1 ** ADAM_STEP
    c2 = 1.0 - ADAM_B2 ** ADAM_STEP

    def body(w_ref, g_ref, m_ref, v_ref, d_ref, nm_ref, nv_ref):
        gv = g_ref[...]
        nm = ADAM_B1 * m_ref[...] + (1.0 - ADAM_B1) * gv
        nv = ADAM_B2 * v_ref[...] + (1.0 - ADAM_B2) * (gv * gv)
        d_ref[...] = -ADAM_LR * ((nm / c1) / (jnp.sqrt(nv / c2) + ADAM_EPS) + ADAM_WD * w_ref[...])
        nm_ref[...] = nm
        nv_ref[...] = nv

    spec = pl.BlockSpec((tr, c), lambda i: (i, 0))
    return pl.pallas_call(
        body,
        name=name,
        grid=(r // tr,),
        in_specs=[spec] * 4,
        out_specs=[spec] * 3,
        out_shape=[jax.ShapeDtypeStruct((r, c), F32)] * 3,
        compiler_params=_params("parallel"),
    )(w, g, m, v)


def reduce_to_shards(tag, grads, wire, blocked, place, deps=()):
    recv = pair_exchange(tag + "_pair_exchange", grads, deps)
    parts = [pair_sum(f"{tag}_pair_sum_{i}", g, r, place[1:], w) for i, (g, r, w) in enumerate(zip(grads, recv, wire))]
    got = chip_exchange(tag + "_chip_exchange", parts)
    sums = [chip_sum(f"{tag}_chip_sum_{i}", q, p, place, b) for i, (q, p, b) in enumerate(zip(got, parts, blocked))]
    return pair_gather(tag + "_pair_gather", sums, blocked, [1] * len(sums))


def _pack(arrays, row_multiple, cols=BLOB_COLS):
    flat = jnp.concatenate([a.reshape(-1).astype(F32) for a in arrays])
    rows = -(-flat.shape[0] // cols)
    rows = -(-rows // row_multiple) * row_multiple
    return jnp.pad(flat, (0, rows * cols - flat.shape[0])).reshape(rows, cols)


def _unpack(blob, shapes):
    flat, out, off = blob.reshape(-1), [], 0
    for s in shapes:
        size = math.prod(s)
        out.append(flat[off:off + size].reshape(s))
        off += size
    return out


def _unpack_rows(blobs, shapes):
    out, off = [], 0
    for s in shapes:
        size = math.prod(s)
        out.append(blobs[:, off:off + size].reshape((blobs.shape[0],) + tuple(s)))
        off += size
    return out


def kernel(x, p, lru_w_in, lru_conv_w, lru_conv_b, lru_wa, lru_ba, lru_wx, lru_bx, lru_lambda, lru_w_out, pool_w_in, pool_w_grp, pool_b_grp, pool_scale, pool_w_out, ln_mix_g, ln_mix_b, mlp_w1, mlp_w2, ln_mlp_g, ln_mlp_b, ple_w, ple_gate_w, ple_gate_b, loss_target, m_lru_w_in, m_lru_conv_w, m_lru_conv_b, m_lru_wa, m_lru_ba, m_lru_wx, m_lru_bx, m_lru_lambda, m_lru_w_out, m_pool_w_in, m_pool_w_grp, m_pool_b_grp, m_pool_scale, m_pool_w_out, m_ln_mix_g, m_ln_mix_b, m_mlp_w1, m_mlp_w2, m_ln_mlp_g, m_ln_mlp_b, m_ple_w, m_ple_gate_w, m_ple_gate_b, v_lru_w_in, v_lru_conv_w, v_lru_conv_b, v_lru_wa, v_lru_ba, v_lru_wx, v_lru_bx, v_lru_lambda, v_lru_w_out, v_pool_w_in, v_pool_w_grp, v_pool_b_grp, v_pool_scale, v_pool_w_out, v_ln_mix_g, v_ln_mix_b, v_mlp_w1, v_mlp_w2, v_ln_mlp_g, v_ln_mlp_b, v_ple_w, v_ple_gate_w, v_ple_gate_b):
    weights = dict(lru_w_in=lru_w_in, lru_conv_w=lru_conv_w, lru_conv_b=lru_conv_b, lru_wa=lru_wa, lru_ba=lru_ba, lru_wx=lru_wx, lru_bx=lru_bx, lru_lambda=lru_lambda, lru_w_out=lru_w_out, pool_w_in=pool_w_in, pool_w_grp=pool_w_grp, pool_b_grp=pool_b_grp, pool_scale=pool_scale, pool_w_out=pool_w_out, ln_mix_g=ln_mix_g, ln_mix_b=ln_mix_b, mlp_w1=mlp_w1, mlp_w2=mlp_w2, ln_mlp_g=ln_mlp_g, ln_mlp_b=ln_mlp_b, ple_w=ple_w, ple_gate_w=ple_gate_w, ple_gate_b=ple_gate_b)
    mom_m = dict(lru_w_in=m_lru_w_in, lru_conv_w=m_lru_conv_w, lru_conv_b=m_lru_conv_b, lru_wa=m_lru_wa, lru_ba=m_lru_ba, lru_wx=m_lru_wx, lru_bx=m_lru_bx, lru_lambda=m_lru_lambda, lru_w_out=m_lru_w_out, pool_w_in=m_pool_w_in, pool_w_grp=m_pool_w_grp, pool_b_grp=m_pool_b_grp, pool_scale=m_pool_scale, pool_w_out=m_pool_w_out, ln_mix_g=m_ln_mix_g, ln_mix_b=m_ln_mix_b, mlp_w1=m_mlp_w1, mlp_w2=m_mlp_w2, ln_mlp_g=m_ln_mlp_g, ln_mlp_b=m_ln_mlp_b, ple_w=m_ple_w, ple_gate_w=m_ple_gate_w, ple_gate_b=m_ple_gate_b)
    mom_v = dict(lru_w_in=v_lru_w_in, lru_conv_w=v_lru_conv_w, lru_conv_b=v_lru_conv_b, lru_wa=v_lru_wa, lru_ba=v_lru_ba, lru_wx=v_lru_wx, lru_bx=v_lru_bx, lru_lambda=v_lru_lambda, lru_w_out=v_lru_w_out, pool_w_in=v_pool_w_in, pool_w_grp=v_pool_w_grp, pool_b_grp=v_pool_b_grp, pool_scale=v_pool_scale, pool_w_out=v_pool_w_out, ln_mix_g=v_ln_mix_g, ln_mix_b=v_ln_mix_b, mlp_w1=v_mlp_w1, mlp_w2=v_mlp_w2, ln_mlp_g=v_ln_mlp_g, ln_mlp_b=v_ln_mlp_b, ple_w=v_ple_w, ple_gate_w=v_ple_gate_w, ple_gate_b=v_ple_gate_b)
    names = list(weights)

    depth, d = ln_mix_g.shape
    t = x.shape[1]
    n_a, n_b = lru_w_in.shape[0], pool_w_in.shape[0]
    d_rnn = lru_w_out.shape[1] * N_CHIPS
    heads = d_rnn // LRU_BW
    d_ff = mlp_w1.shape[2] * N_CHIPS
    ple_dim = ple_w.shape[1]
    n_grp = len(POOL_WINDOWS)
    gw = d // n_grp
    alpha = (2 * depth) ** 0.25
    chip = 2 * lax.axis_index("x") + lax.axis_index("y")
    place = jnp.stack([chip, lax.axis_index("c")]).astype(jnp.int32)

    x2d = x.reshape(t, d)
    target = loss_target.reshape(t, d)
    p3 = p.reshape(depth, t, ple_dim)

    big = ["lru_w_in", "lru_w_out", "pool_w_in", "pool_w_out", "mlp_w1", "mlp_w2", "ple_w", "ple_gate_w", "pool_w_grp"]
    flat2 = lambda a: a.reshape(-1, a.shape[-1])
    small_sharded = ["lru_conv_w", "pool_b_grp", "pool_scale"]
    small_blob = _pack([weights[k] for k in small_sharded], 16, cols=256)
    every_layer = ("mlp_w1", "mlp_w2", "ple_w", "ple_gate_w")
    mlp_tiles = [dict(), dict(), dict(), dict(pm=2048)]

    def layer_keys(i):
        return (["lru_w_in", "lru_w_out"] if i % 2 == 0 else ["pool_w_in", "pool_w_out", "pool_w_grp"]) + list(every_layer)

    def stage(k, i):
        w = weights[k]
        return into_block(f"stage_l{i}_{k}", flat2(w), i if k in every_layer else i // 2, math.prod(w.shape[1:-1]),
                          place[:1], BF16)

    staged = [[stage(k, i) for k in layer_keys(i)] for i in range(depth)]
    first = all_gather_chips("gather_l0", staged[0][:1] + [into_block("stage_small", small_blob, 0, small_blob.shape[0], place[:1], F32)])
    wg = {(layer_keys(0)[0], 0): first[0]}

    tokens = []

    def take_tokens():
        deps = tuple(tokens)
        tokens.clear()
        return deps

    def mm(*args, **kwargs):
        return matmul(*args, deps=take_tokens(), **kwargs)

    def start_gather(tag, bufs, dep):
        plan = gather_plan(len(bufs))
        flight = (plan,) + split_start(f"gather_{tag}_start", plan, 3 * len(bufs), bufs, dep)
        tokens.append(flight[-1])
        return flight

    def land_gather(tag, flight, keys, layer, after):
        plan, send_sems, recv_sems, bufs, _ = flight
        landed = split_wait(f"gather_{tag}_wait", plan, send_sems, recv_sems, bufs, after)
        wg.update(zip([(k, layer) for k in keys], pair_forward(f"gather_{tag}_forward", landed)))

    conv_w_sh, b_grp_sh, scale_sh = _unpack_rows(first[-1].reshape(N_CHIPS, -1), [weights[k].shape for k in small_sharded])
    conv_w_full = jnp.moveaxis(conv_w_sh, 0, 2).reshape(n_a, CONV_WIDTH, d_rnn)
    b_grp_full = jnp.moveaxis(b_grp_sh, 0, 1).reshape(n_b, 1, d)
    scale_full = jnp.moveaxis(scale_sh, 0, 1).reshape(n_b, 1, d)
    rows_grp = gw // N_CHIPS
    w_grp_full = lambda i: jnp.moveaxis(wg["pool_w_grp", i].reshape(N_CHIPS, n_grp, rows_grp, gw), 0, 1).reshape(n_grp, gw, gw)
    wa_bf, wx_bf = lru_wa.astype(BF16), lru_wx.astype(BF16)
    row = lambda a, i: a[i].reshape(1, -1)

    saved = []
    cur, cur_bf = x2d, x2d
    for i in range(depth):
        slot = i // 2
        sv = dict(x_bf=cur_bf)
        if i == 0:
            flight = start_gather("l0_rest", staged[0][1:], first[0])
        elif i + 1 < depth:
            flight = start_gather(f"l{i + 1}", staged[i + 1], cur)
        if i % 2 == 0:
            (proj,) = mm(f"l{i}_lru_in", plain(cur_bf), colsplit(wg["lru_w_in", i], 0, d), "nn",
                         [colsplit(None, 0, t, n=2, full=(2, t, d_rnn), dtype=F32)])
            hg = lru_fwd(f"l{i}_lru", proj, conv_w_full[slot], row(lru_conv_b, slot), wa_bf[slot], row(lru_ba, slot),
                         wx_bf[slot], row(lru_bx, slot), row(lru_lambda, slot))
            if i == 0:
                land_gather("l0_rest", flight, layer_keys(0)[1:], 0, hg)
                flight = start_gather("l1", staged[1], hg)
            (mix,) = mm(f"l{i}_lru_out", plain(hg), rowsplit_whole(wg["lru_w_out", i]), "nn",
                        [plain(shape=(t, d), dtype=F32)], pk=2048)
            sv.update(proj=proj, act=hg)
        else:
            (u,) = mm(f"l{i}_pool_in", plain(cur_bf), rowsplit_whole(wg["pool_w_in", i]), "nn",
                          [plain(shape=(t, d), dtype=F32)])
            zs = pool_fwd(f"l{i}_pool", u, w_grp_full(i), b_grp_full[slot], scale_full[slot])
            (mix,) = mm(f"l{i}_pool_out", plain(zs), rowsplit_whole(wg["pool_w_out", i]), "nn",
                            [plain(shape=(t, d), dtype=F32)])
            sv.update(u=u, act=zs)
        x1, x1_bf, s1 = ln_fwd(f"l{i}_ln_mix", alpha, cur, mix, row(ln_mix_g, i), row(ln_mix_b, i))

        def relu2(acc):
            hr = jnp.maximum(acc, 0.0)
            return hr, hr * hr

        hr, hh = mm(f"l{i}_mlp_up", plain(x1_bf), colsplit(wg["mlp_w1", i], 0, d), "nn",
                        [plain(shape=(t, d_ff), dtype=F32), plain(shape=(t, d_ff), dtype=BF16)], epilogue=relu2, **mlp_tiles[i])
        (mlp,) = mm(f"l{i}_mlp_down", plain(hh), rowsplit(wg["mlp_w2", i], 0, d_ff // N_CHIPS), "nn",
                        [plain(shape=(t, d), dtype=F32)], **mlp_tiles[i])
        x2, x2_bf, s2 = ln_fwd(f"l{i}_ln_mlp", alpha, x1, mlp, row(ln_mlp_g, i), row(ln_mlp_b, i))
        (e,) = mm(f"l{i}_ple", plain(p3[i]), colsplit(wg["ple_w", i], 0, ple_dim), "nn", [plain(shape=(t, d), dtype=F32)])

        def ple_out(acc, x2_t, e_t, gb):
            gate = jax.nn.sigmoid(acc + gb)
            x3 = x2_t + e_t * gate
            return x3, x3, gate

        cur, cur_bf, gate = mm(f"l{i}_ple_gate", plain(x2_bf), rowsplit_whole(wg["ple_gate_w", i]), "nn",
                                   [plain(shape=(t, d), dtype=F32), plain(shape=(t, d), dtype=BF16), plain(shape=(t, d), dtype=F32)],
                                   epilogue=ple_out, tiles=[plain(x2), plain(e)], rows=[row(ple_gate_b, i)])
        sv.update(s1=s1, x1_bf=x1_bf, hr=hr, hh=hh, s2=s2, x2_bf=x2_bf, gate=gate, e=e)
        saved.append(sv)
        if i + 1 < depth:
            land_gather(f"l{i + 1}", flight, layer_keys(i + 1), i + 1, cur)

    dy, loss_part = loss_head("loss", cur, target)
    loss = lax.psum(loss_part.reshape(()), ("x", "y", "c"))

    part = {}
    sums = {}

    def grad_view(key, split):
        w = weights[key]
        return split(None, 0, w.shape[1], full=(N_CHIPS, w.shape[1], w.shape[2]), dtype=F32)

    def group_start(tag, items, dep):
        srcs = [part[it] for it in items]
        plan = pair_plan(len(srcs))
        lands = [lax.empty(s.shape, s.dtype) for s in pair_lands(srcs)]
        flight = (tag, items, plan) + split_start(f"grads_{tag}_pair_start", plan, len(srcs), srcs + lands, dep)
        tokens.append(flight[-1])
        return flight

    def group_mid(flight, after):
        tag, items, plan, send_sems, recv_sems, bufs, _ = flight
        bufs = split_wait(f"grads_{tag}_pair_wait", plan, send_sems, recv_sems, bufs, after)
        n = len(items)
        parts = [pair_sum(f"grads_{tag}_pair_sum_{j}", bufs[j], bufs[n + j], place[1:], F32 if it[0] == "blob" else BF16)
                 for j, it in enumerate(items)]
        plan = chip_plan(n)
        flight = (tag, items, plan) + split_start(f"grads_{tag}_chip_start", plan, 3 * n,
                                                  parts + [lax.empty(q.shape, q.dtype) for q in parts], after)
        tokens.append(flight[-1])
        return flight

    def group_end(flight, after):
        tag, items, plan, send_sems, recv_sems, bufs, _ = flight
        bufs = split_wait(f"grads_{tag}_chip_wait", plan, send_sems, recv_sems, bufs, after)
        n = len(items)
        for j, (k, layer) in enumerate(items):
            if k == "blob":
                sums[k] = chip_sum(f"grads_{tag}_chip_sum_{j}", bufs[n + j], bufs[j], place, True)
            else:
                sums[k] = chip_sum(f"grads_{tag}_chip_sum_{j}", bufs[n + j], bufs[j], place, False, into=sums.get(k),
                                   layer=layer if k in every_layer else layer // 2, n_layers=weights[k].shape[0])

    big_w = [k for k in big if k != "pool_w_grp"]
    small_keys = [k for k in names if k not in big_w]

    small = {k: [None] * weights[k].shape[0] for k in names if k not in big or k == "pool_w_grp"}
    dcur = dy
    mlp_pair = mlp_chip = mix_pair = mix_chip = None
    for i in reversed(range(depth)):
        slot = i // 2
        sv = saved[i]
        de, dpre, dgb = ple_bwd(f"l{i}_ple_bwd", dcur, sv["gate"], sv["e"])
        small["ple_gate_b"][i] = dgb
        (part["ple_w", i],) = mm(f"l{i}_d_ple_w", plain(p3[i]), plain(de), "tn", [grad_view("ple_w", colsplit)])
        (part["ple_gate_w", i],) = mm(f"l{i}_d_ple_gate_w", plain(sv["x2_bf"]), plain(dpre), "tn",
                                          [grad_view("ple_gate_w", rowsplit)])
        (dx2b,) = mm(f"l{i}_dx2", plain(dpre), rowsplit_whole(wg["ple_gate_w", i]), "nt",
                         [plain(shape=(t, d), dtype=F32)])
        ds2, ds2_bf, dg, db = ln_bwd(f"l{i}_ln_mlp_bwd", 1.0, dcur, dx2b, sv["s2"], row(ln_mlp_g, i))
        small["ln_mlp_g"][i], small["ln_mlp_b"][i] = dg, db
        (part["mlp_w2", i],) = mm(f"l{i}_d_mlp_w2", plain(sv["hh"]), plain(ds2_bf), "tn", [grad_view("mlp_w2", rowsplit)], **mlp_tiles[i])
        (dhpre,) = mm(f"l{i}_dh", plain(ds2_bf), rowsplit(wg["mlp_w2", i], 0, d_ff // N_CHIPS), "nt",
                          [plain(shape=(t, d_ff), dtype=BF16)], epilogue=lambda acc, hr_t: (acc * (2.0 * hr_t),),
                          tiles=[plain(sv["hr"])], **mlp_tiles[i])
        (part["mlp_w1", i],) = mm(f"l{i}_d_mlp_w1", plain(sv["x1_bf"]), plain(dhpre), "tn", [grad_view("mlp_w1", colsplit)],
                                  **mlp_tiles[i])
        if mlp_chip is not None:
            group_end(mlp_chip, dhpre)
        if mix_pair is not None:
            mix_chip = group_mid(mix_pair, dhpre)
        mlp_pair = group_start(f"l{i}_mlp", [(k, i) for k in every_layer], part["mlp_w1", i])
        (dx1b,) = mm(f"l{i}_dx1", plain(dhpre), colsplit(wg["mlp_w1", i], 0, d), "nt", [plain(shape=(t, d), dtype=F32)],
                     **mlp_tiles[i])
        ds1, ds1_bf, dg, db = ln_bwd(f"l{i}_ln_mix_bwd", alpha, ds2, dx1b, sv["s1"], row(ln_mix_g, i))
        small["ln_mix_g"][i], small["ln_mix_b"][i] = dg, db
        residual = lambda acc, ds_t: (alpha * ds_t + acc,)
        if i % 2 == 0:
            (part["lru_w_out", i],) = mm(f"l{i}_d_lru_out", plain(sv["act"]), plain(ds1_bf), "tn",
                                             [grad_view("lru_w_out", rowsplit)])
            (dhg,) = mm(f"l{i}_dhg", plain(ds1_bf), rowsplit_whole(wg["lru_w_out", i]), "nt",
                            [plain(shape=(t, d_rnn), dtype=F32)], pn=2048)
            mlp_chip = group_mid(mlp_pair, dhg)
            dproj, dcw, dcb, dba, dbx, dlam, dwa, dwx = lru_bwd(
                f"l{i}_lru_bwd", sv["proj"], dhg, conv_w_full[slot], row(lru_conv_b, slot), wa_bf[slot], row(lru_ba, slot),
                wx_bf[slot], row(lru_bx, slot), row(lru_lambda, slot), deps=take_tokens())
            for key, val in (("lru_conv_w", dcw), ("lru_conv_b", dcb), ("lru_ba", dba), ("lru_bx", dbx),
                             ("lru_lambda", dlam), ("lru_wa", dwa), ("lru_wx", dwx)):
                small[key][slot] = val
            dproj_v = colsplit(dproj, 0, t, n=2)
            (part["lru_w_in", i],) = mm(f"l{i}_d_lru_in", plain(sv["x_bf"]), dproj_v, "tn", [grad_view("lru_w_in", colsplit)])
            (dcur,) = mm(f"l{i}_dx", dproj_v, colsplit(wg["lru_w_in", i], 0, d), "nt",
                             [plain(shape=(t, d), dtype=F32)], epilogue=residual, tiles=[plain(ds1)])
        else:
            (part["pool_w_out", i],) = mm(f"l{i}_d_pool_out", plain(sv["act"]), plain(ds1_bf), "tn",
                                              [grad_view("pool_w_out", rowsplit)])
            (dzs,) = mm(f"l{i}_dzs", plain(ds1_bf), rowsplit_whole(wg["pool_w_out", i]), "nt",
                            [plain(shape=(t, d), dtype=F32)])
            mlp_chip = group_mid(mlp_pair, dzs)
            du, dwg, dbg, dsc = pool_bwd(f"l{i}_pool_bwd", sv["u"], dzs, w_grp_full(i), b_grp_full[slot], scale_full[slot],
                                         deps=take_tokens())
            small["pool_w_grp"][slot], small["pool_b_grp"][slot], small["pool_scale"][slot] = dwg, dbg, dsc
            (part["pool_w_in", i],) = mm(f"l{i}_d_pool_in", plain(sv["x_bf"]), plain(du), "tn", [grad_view("pool_w_in", rowsplit)])
            (dcur,) = mm(f"l{i}_dx", plain(du), rowsplit_whole(wg["pool_w_in", i]), "nt",
                             [plain(shape=(t, d), dtype=F32)], epilogue=residual, tiles=[plain(ds1)])
        if mix_chip is not None:
            group_end(mix_chip, dcur)
        mixer = [(k, i) for k in layer_keys(i) if k not in every_layer and k != "pool_w_grp"]
        if i == 0:
            small_full = [jnp.stack(small[k]).reshape((weights[k].shape[0],) + tuple(
                s * (N_CHIPS if ax in _sharded_axis(k) else 1) for ax, s in enumerate(weights[k].shape[1:], 1))) for k in small_keys]
            blob = _pack(small_full, 64)
            part["blob", 0] = blob.reshape(N_CHIPS, blob.shape[0] // N_CHIPS, BLOB_COLS)
            mixer.append(("blob", 0))
        mix_pair = group_start(f"l{i}_mix", mixer, dcur)
    grad_x = dcur.reshape(x.shape)
    mix_chip = group_mid(mix_pair, dcur)
    group_end(mlp_chip, mix_chip[-1])
    group_end(mix_chip, sums["mlp_w1"])
    order = big_w + ["blob"]
    reduced = dict(zip(order, pair_gather("grads_pair_gather", [sums[k] for k in order], [k == "blob" for k in order],
                                          [1 if k == "blob" else weights[k].shape[0] for k in order])))
    (blob_all,) = all_gather_chips("gather_small_grads", [reduced["blob"]])
    small_grads = dict(zip(small_keys, _unpack(blob_all.reshape(blob.shape), [a.shape for a in small_full])))
    for k in small_keys:
        for ax in _sharded_axis(k):
            n = weights[k].shape[ax]
            small_grads[k] = lax.dynamic_slice_in_dim(small_grads[k], chip * n, n, axis=ax)
    grads = {k: reduced[k].reshape(weights[k].shape) for k in big_w}
    grads.update(small_grads)

    delta, new_m, new_v = {}, {}, {}
    for k in big_w:
        dl, nm, nv = adamw("adamw_" + k, flat2(weights[k]), flat2(grads[k]), flat2(mom_m[k]), flat2(mom_v[k]))
        delta[k], new_m[k], new_v[k] = (a.reshape(weights[k].shape) for a in (dl, nm, nv))
    shapes = [weights[k].shape for k in small_keys]
    dl, nm, nv = adamw("adamw_small", *[_pack([src[k] for k in small_keys], 8) for src in (weights, grads, mom_m, mom_v)])
    for out, blob_out in ((delta, dl), (new_m, nm), (new_v, nv)):
        out.update(zip(small_keys, _unpack(blob_out, shapes)))

    return (loss, grad_x, *[grads[k] for k in names], *[delta[k] for k in names],
            *[new_m[k] for k in names], *[new_v[k] for k in names])


def _sharded_axis(key):
    return {"lru_conv_w": (2,), "pool_w_grp": (2,), "pool_b_grp": (1,), "pool_scale": (1,)}.get(key, ())
```

```python
import functools
import math

import jax
import jax.numpy as jnp
from jax import lax
from jax.experimental import pallas as pl
from jax.experimental.pallas import tpu as pltpu

F32 = jnp.float32
BF16 = jnp.bfloat16

N_CHIPS = 4
LRU_BW = 128
LRU_C = 8.0
CONV_WIDTH = 4
POOL_WINDOWS = (2, 4, 8, 16)
POOL_HALO = 16
CONV_HALO = 8
LN_EPS = 1e-5
ADAM_LR = 0.001
ADAM_B1 = 0.9
ADAM_B2 = 0.999
ADAM_EPS = 1e-08
ADAM_WD = 0.01
ADAM_STEP = 10
GELU_C = math.sqrt(2.0 / math.pi)
GELU_K = 0.044715
VMEM_LIMIT_BYTES = 56 * 1024 * 1024
MATMUL_TILE_BYTES = 44 * 1024 * 1024
MESH = pl.DeviceIdType.MESH
BLOB_COLS = 1024


def _params(*sem):
    return pltpu.CompilerParams(dimension_semantics=tuple(sem), vmem_limit_bytes=VMEM_LIMIT_BYTES)


def _tile(unit, pref, align=128):
    if unit <= pref:
        return unit
    for d in range(2, unit + 1):
        if unit % d == 0 and unit // d <= pref and (unit // d) % align == 0:
            return unit // d
    raise ValueError((unit, pref, align))


class View:
    def __init__(self, arr, shape, row_unit, col_unit, block_fn, full=None, dtype=None):
        self.arr, self.shape, self.row_unit, self.col_unit, self.block_fn = arr, shape, row_unit, col_unit, block_fn
        self.full = full if full is not None else arr.shape
        self.dtype = dtype if dtype is not None else arr.dtype

    def spec(self, tr, tc, f):
        block, idx = self.block_fn(tr, tc)
        return pl.BlockSpec(block, lambda *g: idx(*f(*g)))


def plain(arr=None, shape=None, dtype=None):
    shape = arr.shape if arr is not None else shape
    return View(arr, shape, shape[0], shape[1], lambda tr, tc: ((tr, tc), lambda rt, ct: (rt, ct)), full=shape, dtype=dtype)


def colsplit(arr, layer, rows, n=N_CHIPS, full=None, dtype=None):
    full = arr.shape if arr is not None else full
    c = full[2]

    def block_fn(tr, tc):
        assert rows % tr == 0 and c % tc == 0, (rows, tr, c, tc)
        per, rpl = c // tc, rows // tr
        return (None, tr, tc), lambda rt, ct: (ct // per, layer * rpl + rt, ct % per)

    return View(arr, (rows, n * c), rows, c, block_fn, full=full, dtype=dtype)


def rowsplit(arr, layer, rows, n=N_CHIPS, full=None, dtype=None):
    full = arr.shape if arr is not None else full
    c = full[2]

    def block_fn(tr, tc):
        assert rows % tr == 0 and c % tc == 0, (rows, tr, c, tc)
        per = rows // tr
        return (None, tr, tc), lambda rt, ct: (rt // per, layer * per + rt % per, ct)

    return View(arr, (n * rows, c), rows, c, block_fn, full=full, dtype=dtype)


def rowsplit_whole(arr):
    n, rows, c = arr.shape

    def block_fn(tr, tc):
        assert tr == n * rows and c % tc == 0, (tr, n, rows, c, tc)
        return (n, rows, tc), lambda rt, ct: (0, 0, ct)

    return View(arr, (n * rows, c), n * rows, c, block_fn)


def matmul(name, a, b, mode, outs, epilogue=None, tiles=(), rows=(), deps=(), pm=1024, pn=1024, pk=1024):
    if mode == "nn":
        (m, k), (k2, n) = a.shape, b.shape
        um, uk, un = a.row_unit, min(a.col_unit, b.row_unit), b.col_unit
        dims = (((1,), (0,)), ((), ()))
    elif mode == "nt":
        (m, k), (n, k2) = a.shape, b.shape
        um, uk, un = a.row_unit, min(a.col_unit, b.col_unit), b.row_unit
        dims = (((1,), (1,)), ((), ()))
    else:
        (k, m), (k2, n) = a.shape, b.shape
        um, uk, un = a.col_unit, min(a.row_unit, b.row_unit), b.col_unit
        dims = (((0,), (0,)), ((), ()))
    assert k == k2, (name, a.shape, b.shape)
    for o in list(outs) + list(tiles):
        assert o.shape == (m, n), (name, o.shape, m, n)
        um, un = min(um, o.row_unit), min(un, o.col_unit)
    tm, tn, tk = _tile(um, pm), _tile(un, pn), _tile(uk, pk)
    if mode == "tn" and uk == k:
        size = lambda v: jnp.dtype(v.dtype).itemsize
        need = 2 * k * (tm * size(a) + tn * size(b)) + 2 * tm * tn * sum(size(o) for o in outs)
        if need <= MATMUL_TILE_BYTES:
            tk = k
    assert m % tm == 0 and n % tn == 0 and k % tk == 0, (name, m, n, k, tm, tn, tk)
    gm, gn, gk = m // tm, n // tn, k // tk

    if mode == "nn":
        a_spec = a.spec(tm, tk, lambda i, j, kk: (i, kk))
        b_spec = b.spec(tk, tn, lambda i, j, kk: (kk, j))
    elif mode == "nt":
        a_spec = a.spec(tm, tk, lambda i, j, kk: (i, kk))
        b_spec = b.spec(tn, tk, lambda i, j, kk: (j, kk))
    else:
        a_spec = a.spec(tk, tm, lambda i, j, kk: (kk, i))
        b_spec = b.spec(tk, tn, lambda i, j, kk: (kk, j))
    tile_specs = [t.spec(tm, tn, lambda i, j, kk: (i, j)) for t in tiles]
    row_specs = [pl.BlockSpec((1, tn), lambda i, j, kk: (0, j)) for _ in rows]
    in_place = [o for o in outs if o.arr is not None]
    alias_specs = [pl.BlockSpec(memory_space=pl.ANY) for _ in in_place]
    out_specs = [o.spec(tm, tn, lambda i, j, kk: (i, j)) for o in outs]
    n_in = 2 + len(tiles) + len(rows)
    aliases = {}
    for o_idx, o in enumerate(outs):
        if o.arr is not None:
            aliases[n_in + in_place.index(o)] = o_idx
    n_t, n_r, n_a, n_o = len(tiles), len(rows), len(in_place) + len(deps), len(outs)
    dep_specs = [pl.BlockSpec(memory_space=pl.ANY) for _ in deps]

    def body(*refs):
        a_ref, b_ref = refs[0], refs[1]
        tile_refs = refs[2:2 + n_t]
        row_refs = refs[2 + n_t:2 + n_t + n_r]
        out_refs = refs[2 + n_t + n_r + n_a:2 + n_t + n_r + n_a + n_o]
        acc_ref = refs[-1] if gk > 1 else None

        def finish(acc):
            extra = [t[...] for t in tile_refs] + [r[...] for r in row_refs]
            res = epilogue(acc, *extra) if epilogue is not None else (acc,)
            for o_ref, r in zip(out_refs, res):
                o_ref[...] = r.astype(o_ref.dtype)

        b_tile = b_ref[...]
        b_tile = b_tile.reshape(-1, b_tile.shape[-1])
        prod = lax.dot_general(a_ref[...].astype(BF16), b_tile.astype(BF16), dims, preferred_element_type=F32)
        if gk == 1:
            finish(prod)
        else:
            kk = pl.program_id(2)

            @pl.when(kk == 0)
            def _():
                acc_ref[...] = prod

            @pl.when(kk > 0)
            def _():
                acc_ref[...] += prod

            @pl.when(kk == gk - 1)
            def _():
                finish(acc_ref[...])

    res = pl.pallas_call(
        body,
        name=name,
        grid=(gm, gn, gk),
        in_specs=[a_spec, b_spec] + tile_specs + row_specs + alias_specs + dep_specs,
        out_specs=out_specs,
        out_shape=[jax.ShapeDtypeStruct(o.full, o.dtype) for o in outs],
        scratch_shapes=[pltpu.VMEM((tm, tn), F32)] if gk > 1 else [],
        input_output_aliases=aliases,
        compiler_params=_params("parallel", "parallel", "arbitrary"),
    )(a.arr, b.arr, *[t.arr for t in tiles], *rows, *[o.arr for o in in_place], *deps)
    return res


def rows_call(name, fn, tiled, vecs, tiled_out, acc_out, tr=512):
    t = tiled[0].shape[0]
    tr = min(tr, t)
    assert t % tr == 0
    n1, n2, n3 = len(tiled), len(vecs), len(tiled_out)

    def body(*refs):
        fn(pl.program_id(0), refs[:n1], refs[n1:n1 + n2], refs[n1 + n2:n1 + n2 + n3], refs[n1 + n2 + n3:])

    return pl.pallas_call(
        body,
        name=name,
        grid=(t // tr,),
        in_specs=[pl.BlockSpec((tr, x.shape[1]), lambda i: (i, 0)) for x in tiled]
        + [pl.BlockSpec(v.shape, lambda i: (0, 0)) for v in vecs],
        out_specs=[pl.BlockSpec((tr, c), lambda i: (i, 0)) for c, _ in tiled_out]
        + [pl.BlockSpec(s, lambda i: (0, 0)) for s, _ in acc_out],
        out_shape=[jax.ShapeDtypeStruct((t, c), d) for c, d in tiled_out] + [jax.ShapeDtypeStruct(s, d) for s, d in acc_out],
        compiler_params=_params("arbitrary" if acc_out else "parallel"),
    )(*tiled, *vecs)


def _accumulate(step, ref, val):
    @pl.when(step == 0)
    def _():
        ref[...] = val

    @pl.when(step > 0)
    def _():
        ref[...] += val


def _ln_stats(s):
    mu = jnp.mean(s, axis=-1, keepdims=True)
    d = s - mu
    var = jnp.mean(d * d, axis=-1, keepdims=True)
    rstd = lax.rsqrt(var + LN_EPS)
    return d * rstd, rstd


def ln_fwd(name, alpha, x_in, m, g, b):
    d = x_in.shape[1]

    def fn(step, tiled, vecs, outs, accs):
        s = alpha * tiled[0][...] + tiled[1][...]
        xhat, _ = _ln_stats(s)
        y = xhat * vecs[0][...] + vecs[1][...]
        outs[0][...] = y
        outs[1][...] = y.astype(BF16)
        outs[2][...] = s

    return rows_call(name, fn, [x_in, m], [g, b], [(d, F32), (d, BF16), (d, F32)], [])


def ln_bwd(name, ca, da, db, s, g):
    d = s.shape[1]

    def fn(step, tiled, vecs, outs, accs):
        dx = ca * tiled[0][...] + tiled[1][...]
        xhat, rstd = _ln_stats(tiled[2][...])
        dxh = dx * vecs[0][...]
        ds = rstd * (dxh - jnp.mean(dxh, axis=-1, keepdims=True) - xhat * jnp.mean(dxh * xhat, axis=-1, keepdims=True))
        outs[0][...] = ds
        outs[1][...] = ds.astype(BF16)
        _accumulate(step, accs[0], jnp.sum(dx * xhat, axis=0, keepdims=True))
        _accumulate(step, accs[1], jnp.sum(dx, axis=0, keepdims=True))

    return rows_call(name, fn, [da, db, s], [g], [(d, F32), (d, BF16)], [((1, d), F32), ((1, d), F32)])


def ple_bwd(name, dx3, gate, e):
    d = dx3.shape[1]

    def fn(step, tiled, vecs, outs, accs):
        dx, gt, ev = tiled[0][...], tiled[1][...], tiled[2][...]
        dpre = dx * ev * gt * (1.0 - gt)
        outs[0][...] = (dx * gt).astype(BF16)
        outs[1][...] = dpre.astype(BF16)
        _accumulate(step, accs[0], jnp.sum(dpre, axis=0, keepdims=True))

    return rows_call(name, fn, [dx3, gate, e], [], [(d, BF16), (d, BF16)], [((1, d), F32)])


def loss_head(name, y, target):
    t, d = y.shape

    def fn(step, tiled, vecs, outs, accs):
        err = tiled[0][...] - tiled[1][...]
        outs[0][...] = err * (1.0 / d)
        part = jnp.sum(jnp.sum(err * err, axis=1, keepdims=True), axis=0, keepdims=True) * (0.5 / d)
        _accumulate(step, accs[0], part)

    return rows_call(name, fn, [y, target], [], [(d, F32)], [((1, 1), F32)])


def _softplus(z):
    return jnp.maximum(z, 0.0) + jnp.log1p(jnp.exp(-jnp.abs(z)))


def _gelu(y):
    th = jnp.tanh(GELU_C * (y + GELU_K * (y * y * y)))
    cdf = 0.5 * (1.0 + th)
    return y * cdf, cdf + 0.5 * y * (1.0 - th * th) * (GELU_C * (1.0 + 3.0 * GELU_K * y * y))


def _up(win, k):
    return pltpu.roll(win, win.shape[0] - k, 0)


def _down(win, k):
    return pltpu.roll(win, k, 0)


def _lru_gates(win, row0, cw_ref, cb, wa, ba, wx, bx, sp):
    h = CONV_HALO
    u = (cb + cw_ref[3:4, :] * win[h:] + cw_ref[2:3, :] * _down(win, 1)[h:]
         + cw_ref[1:2, :] * _down(win, 2)[h:] + cw_ref[0:1, :] * _down(win, 3)[h:])
    ub = u.astype(BF16)
    r = jax.nn.sigmoid(jnp.dot(ub, wa, preferred_element_type=F32) + ba)
    ig = jax.nn.sigmoid(jnp.dot(ub, wx, preferred_element_type=F32) + bx)
    log_a = (-LRU_C) * r * sp
    a = jnp.exp(log_a)
    mult = jnp.sqrt(-jnp.tanh(log_a) * (a * a + 1.0))
    first = (row0 + lax.broadcasted_iota(jnp.int32, u.shape, 0)) == 0
    mult = jnp.where(first, 1.0, mult)
    return u, r, ig, a, mult, first


def _block_scan(a, b, reverse):
    n = a.shape[0]
    a, b = a.reshape(n // 8, 8, LRU_BW), b.reshape(n // 8, 8, LRU_BW)
    pos = lax.broadcasted_iota(jnp.int32, a.shape, 1)
    for s in (1, 2, 4):
        keep = (pos >= 8 - s) if reverse else (pos < s)
        by = 8 - s if reverse else s
        b = jnp.where(keep, b, a * pltpu.roll(b, by, 1) + b)
        a = jnp.where(keep, a, a * pltpu.roll(a, by, 1))
    return a.reshape(n, LRU_BW), b.reshape(n, LRU_BW)


def _carry_scan(a_ref, b_ref, out_ref, out_off, t, reverse):
    groups, per_step = t // 8, 8

    def step(j, h):
        for k in range(per_step):
            g = j * per_step + k
            r0 = pl.multiple_of((groups - 1 - g if reverse else g) * 8, 8)
            edge = r0 if reverse else r0 + 7
            h_out = a_ref[pl.ds(edge, 1), :] * h + b_ref[pl.ds(edge, 1), :]
            out_ref[pl.ds(pl.multiple_of(out_off + r0, 8), 8), :] = a_ref[pl.ds(r0, 8), :] * h + b_ref[pl.ds(r0, 8), :]
            h = h_out
        return h

    lax.fori_loop(0, groups // per_step, step, jnp.zeros((1, LRU_BW), F32))


def _lru_in_specs(t, heads):
    blk = lambda i: (0, i)
    return [
        pl.BlockSpec((2, t, LRU_BW), lambda i: (0, 0, i)),
        pl.BlockSpec((CONV_WIDTH, LRU_BW), blk),
        pl.BlockSpec((1, LRU_BW), blk),
        pl.BlockSpec((None, LRU_BW, LRU_BW), lambda i: (i, 0, 0)),
        pl.BlockSpec((1, LRU_BW), blk),
        pl.BlockSpec((None, LRU_BW, LRU_BW), lambda i: (i, 0, 0)),
        pl.BlockSpec((1, LRU_BW), blk),
        pl.BlockSpec((1, LRU_BW), blk),
    ]


def lru_fwd(name, proj, conv_w, conv_b, wa, ba, wx, bx, lam):
    _, t, c = proj.shape
    heads = c // LRU_BW
    rc = min(256, t)

    def body(proj_ref, cw_ref, cb_ref, wa_ref, ba_ref, wx_ref, bx_ref, lam_ref, out_ref, upad, a_s, b_s):
        upad[0:CONV_HALO, :] = jnp.zeros((CONV_HALO, LRU_BW), F32)
        upad[CONV_HALO:, :] = proj_ref[0]
        sp = _softplus(-lam_ref[...])
        cb, ba, bx, wa, wx = cb_ref[...], ba_ref[...], bx_ref[...], wa_ref[...], wx_ref[...]

        def gates(i, carry):
            r0 = pl.multiple_of(i * rc, rc)
            win = upad[pl.ds(r0, rc + CONV_HALO), :]
            u, r, ig, a, mult, _ = _lru_gates(win, r0, cw_ref, cb, wa, ba, wx, bx, sp)
            rows = pl.ds(r0, rc)
            a_s[rows, :], b_s[rows, :] = _block_scan(a, mult * (ig * u), False)
            return carry

        lax.fori_loop(0, t // rc, gates, 0)
        _carry_scan(a_s, b_s, b_s, 0, t, False)

        def gate_out(i, carry):
            r0 = pl.multiple_of(i * rc, rc)
            gy, _ = _gelu(proj_ref[1, pl.ds(r0, rc), :])
            out_ref[pl.ds(r0, rc), :] = (b_s[pl.ds(r0, rc), :] * gy).astype(BF16)
            return carry

        lax.fori_loop(0, t // rc, gate_out, 0)

    return pl.pallas_call(
        body,
        name=name,
        grid=(heads,),
        in_specs=_lru_in_specs(t, heads),
        out_specs=pl.BlockSpec((t, LRU_BW), lambda i: (0, i)),
        out_shape=jax.ShapeDtypeStruct((t, c), BF16),
        scratch_shapes=[pltpu.VMEM((t + CONV_HALO, LRU_BW), F32)] + [pltpu.VMEM((t, LRU_BW), F32)] * 2,
        compiler_params=_params("parallel"),
    )(proj, conv_w, conv_b, wa, ba, wx, bx, lam)


def lru_bwd(name, proj, dhg, conv_w, conv_b, wa, ba, wx, bx, lam, deps=()):
    _, t, c = proj.shape
    heads = c // LRU_BW
    rc = min(256, t)
    h8 = CONV_HALO

    def body(proj_ref, dhg_ref, cw_ref, cb_ref, wa_ref, ba_ref, wx_ref, bx_ref, lam_ref,
             dproj_ref, dcw_ref, dcb_ref, dba_ref, dbx_ref, dlam_ref, dwa_ref, dwx_ref,
             upad, u_s, r_s, ig_s, apad, hpad, g_s, dupad, sa_s, sb_s):
        zeros8 = jnp.zeros((h8, LRU_BW), F32)
        upad[0:h8, :] = zeros8
        upad[h8:, :] = proj_ref[0]
        hpad[0:h8, :] = zeros8
        apad[t:, :] = zeros8
        dupad[t:, :] = zeros8
        lam = lam_ref[...]
        sp = _softplus(-lam)
        cb, ba, bx, wa, wx = cb_ref[...], ba_ref[...], bx_ref[...], wa_ref[...], wx_ref[...]

        def gates(i, carry):
            r0 = pl.multiple_of(i * rc, rc)
            win = upad[pl.ds(r0, rc + h8), :]
            u, r, ig, a, mult, _ = _lru_gates(win, r0, cw_ref, cb, wa, ba, wx, bx, sp)
            u_s[pl.ds(r0, rc), :] = u
            r_s[pl.ds(r0, rc), :] = r
            ig_s[pl.ds(r0, rc), :] = ig
            rows = pl.ds(r0, rc)
            apad[rows, :] = a
            sa_s[rows, :], sb_s[rows, :] = _block_scan(a, mult * (ig * u), False)
            return carry

        lax.fori_loop(0, t // rc, gates, 0)
        _carry_scan(sa_s, sb_s, hpad, h8, t, False)

        def out_gate(i, carry):
            r0 = pl.multiple_of(i * rc, rc)
            gy, dgy = _gelu(proj_ref[1, pl.ds(r0, rc), :])
            dh = dhg_ref[pl.ds(r0, rc), :]
            hh = hpad[pl.ds(pl.multiple_of(r0 + h8, 8), rc), :]
            dproj_ref[1, pl.ds(r0, rc), :] = (dh * hh * dgy).astype(BF16)
            rows = pl.ds(r0, rc)
            a_next = _up(apad[pl.ds(r0, rc + h8), :], 1)[:rc]
            sa_s[rows, :], sb_s[rows, :] = _block_scan(a_next, dh * gy, True)
            return carry

        lax.fori_loop(0, t // rc, out_gate, 0)
        _carry_scan(sa_s, sb_s, g_s, 0, t, True)

        zrow = jnp.zeros((1, LRU_BW), F32)
        zmat = jnp.zeros((LRU_BW, LRU_BW), F32)

        def grads(i, carry):
            dsp, dba, dbx, dwa, dwx = carry
            r0 = pl.multiple_of(i * rc, rc)
            g = g_s[pl.ds(r0, rc), :]
            u, r, ig, a = u_s[pl.ds(r0, rc), :], r_s[pl.ds(r0, rc), :], ig_s[pl.ds(r0, rc), :], apad[pl.ds(r0, rc), :]
            hprev = _down(hpad[pl.ds(r0, rc + h8), :], 1)[h8:]
            first = (r0 + lax.broadcasted_iota(jnp.int32, u.shape, 0)) == 0
            log_a = (-LRU_C) * r * sp
            mult = jnp.where(first, 1.0, jnp.sqrt(-jnp.tanh(log_a) * (a * a + 1.0)))
            dmult = jnp.where(first, 0.0, g * (ig * u))
            dlog_a = g * hprev * a - dmult * (a * a) / mult
            dr = dlog_a * ((-LRU_C) * sp)
            dpre_r = dr * r * (1.0 - r)
            dpre_i = (g * mult * u) * ig * (1.0 - ig)
            pr, pi, ub = dpre_r.astype(BF16), dpre_i.astype(BF16), u.astype(BF16)
            nt = (((1,), (1,)), ((), ()))
            tn = (((0,), (0,)), ((), ()))
            du = (g * mult * ig + lax.dot_general(pr, wa, nt, preferred_element_type=F32)
                  + lax.dot_general(pi, wx, nt, preferred_element_type=F32))
            dupad[pl.ds(r0, rc), :] = du
            return (dsp + jnp.sum(dlog_a * ((-LRU_C) * r), axis=0, keepdims=True),
                    dba + jnp.sum(dpre_r, axis=0, keepdims=True),
                    dbx + jnp.sum(dpre_i, axis=0, keepdims=True),
                    dwa + lax.dot_general(ub, pr, tn, preferred_element_type=F32),
                    dwx + lax.dot_general(ub, pi, tn, preferred_element_type=F32))

        dsp, dba, dbx, dwa, dwx = lax.fori_loop(0, t // rc, grads, (zrow, zrow, zrow, zmat, zmat))
        dba_ref[...] = dba
        dbx_ref[...] = dbx
        dwa_ref[...] = dwa
        dwx_ref[...] = dwx
        dlam_ref[...] = -dsp * jax.nn.sigmoid(-lam)

        def conv_back(i, carry):
            dcb, d0, d1, d2, d3 = carry
            r0 = pl.multiple_of(i * rc, rc)
            dwin = dupad[pl.ds(r0, rc + h8), :]
            du = dwin[:rc]
            du0 = (cw_ref[3:4, :] * du + cw_ref[2:3, :] * _up(dwin, 1)[:rc]
                   + cw_ref[1:2, :] * _up(dwin, 2)[:rc] + cw_ref[0:1, :] * _up(dwin, 3)[:rc])
            dproj_ref[0, pl.ds(r0, rc), :] = du0.astype(BF16)
            win = upad[pl.ds(r0, rc + h8), :]
            red = lambda v: jnp.sum(v, axis=0, keepdims=True)
            return (dcb + red(du), d0 + red(du * _down(win, 3)[h8:]), d1 + red(du * _down(win, 2)[h8:]),
                    d2 + red(du * _down(win, 1)[h8:]), d3 + red(du * win[h8:]))

        dcb, d0, d1, d2, d3 = lax.fori_loop(0, t // rc, conv_back, (zrow,) * 5)
        dcb_ref[...] = dcb
        dcw_ref[0:1, :] = d0
        dcw_ref[1:2, :] = d1
        dcw_ref[2:3, :] = d2
        dcw_ref[3:4, :] = d3

    blk = lambda i: (0, i)
    vec = jax.ShapeDtypeStruct((1, c), F32)
    mat = jax.ShapeDtypeStruct((heads, LRU_BW, LRU_BW), F32)
    full = lambda: pltpu.VMEM((t, LRU_BW), F32)
    padded = lambda: pltpu.VMEM((t + h8, LRU_BW), F32)
    return pl.pallas_call(
        lambda *refs: body(*refs[len(deps):]),
        name=name,
        grid=(heads,),
        in_specs=[_ANY] * len(deps) + _lru_in_specs(t, heads)[:1] + [pl.BlockSpec((t, LRU_BW), blk)] + _lru_in_specs(t, heads)[1:],
        out_specs=[pl.BlockSpec((2, t, LRU_BW), lambda i: (0, 0, i)), pl.BlockSpec((CONV_WIDTH, LRU_BW), blk)]
        + [pl.BlockSpec((1, LRU_BW), blk)] * 4 + [pl.BlockSpec((None, LRU_BW, LRU_BW), lambda i: (i, 0, 0))] * 2,
        out_shape=[jax.ShapeDtypeStruct((2, t, c), BF16), jax.ShapeDtypeStruct((CONV_WIDTH, c), F32), vec, vec, vec, vec, mat, mat],
        scratch_shapes=[padded(), full(), full(), full(), padded(), padded(), full(), padded()] + [full()] * 2,
        compiler_params=_params("parallel"),
    )(*deps, proj, dhg, conv_w, conv_b, wa, ba, wx, bx, lam)


def _pick_level(g, levels):
    out = levels[-1]
    for k in range(len(levels) - 2, -1, -1):
        out = jnp.where(g == k, levels[k], out)
    return out


def _pool_z(win, g, row0, rc):
    levels, cur = [], win
    for k in range(len(POOL_WINDOWS)):
        cur = cur + _down(cur, 1 << k)
        levels.append(cur[POOL_HALO:])
    tot = _pick_level(g, levels)
    width = jnp.left_shift(2, g)
    row = row0 + lax.broadcasted_iota(jnp.int32, tot.shape, 0)
    cnt = jnp.minimum(row + 1, width).astype(F32)
    return tot / cnt - win[POOL_HALO:], cnt


def _pool_specs(t, gw):
    blk = lambda g: (0, g)
    return [pl.BlockSpec((t, gw), blk), pl.BlockSpec((None, gw, gw), lambda g: (g, 0, 0)),
            pl.BlockSpec((1, gw), blk), pl.BlockSpec((1, gw), blk)]


def pool_fwd(name, u, w_grp, b_grp, scale):
    t, d = u.shape
    gw = d // len(POOL_WINDOWS)
    rc = min(256, t)

    def body(u_ref, wg_ref, bg_ref, sc_ref, out_ref, upad):
        g = pl.program_id(0)
        upad[0:POOL_HALO, :] = jnp.zeros((POOL_HALO, gw), F32)
        upad[POOL_HALO:, :] = u_ref[...]
        wg, bg, sc = wg_ref[...], bg_ref[...], sc_ref[...]

        def chunk(i, carry):
            r0 = pl.multiple_of(i * rc, rc)
            z, _ = _pool_z(upad[pl.ds(r0, rc + POOL_HALO), :], g, r0, rc)
            z2 = jnp.dot(z.astype(BF16), wg, preferred_element_type=F32) + bg
            out_ref[pl.ds(r0, rc), :] = (z2 * sc).astype(BF16)
            return carry

        lax.fori_loop(0, t // rc, chunk, 0)

    return pl.pallas_call(
        body,
        name=name,
        grid=(len(POOL_WINDOWS),),
        in_specs=_pool_specs(t, gw),
        out_specs=pl.BlockSpec((t, gw), lambda g: (0, g)),
        out_shape=jax.ShapeDtypeStruct((t, d), BF16),
        scratch_shapes=[pltpu.VMEM((t + POOL_HALO, gw), F32)],
        compiler_params=_params("parallel"),
    )(u, w_grp, b_grp, scale)


def pool_bwd(name, u, dzs, w_grp, b_grp, scale, deps=()):
    t, d = u.shape
    gw = d // len(POOL_WINDOWS)
    rc = min(256, t)

    def body(u_ref, dzs_ref, wg_ref, bg_ref, sc_ref, du_ref, dwg_ref, dbg_ref, dsc_ref, upad, qpad, dz_s):
        g = pl.program_id(0)
        upad[0:POOL_HALO, :] = jnp.zeros((POOL_HALO, gw), F32)
        upad[POOL_HALO:, :] = u_ref[...]
        qpad[t:, :] = jnp.zeros((POOL_HALO, gw), F32)
        wg, bg, sc = wg_ref[...], bg_ref[...], sc_ref[...]
        zrow = jnp.zeros((1, gw), F32)

        def chunk(i, carry):
            dsc, dbg, dwg = carry
            r0 = pl.multiple_of(i * rc, rc)
            z, cnt = _pool_z(upad[pl.ds(r0, rc + POOL_HALO), :], g, r0, rc)
            zb = z.astype(BF16)
            z2 = jnp.dot(zb, wg, preferred_element_type=F32) + bg
            dzs = dzs_ref[pl.ds(r0, rc), :]
            dz2 = dzs * sc
            d2b = dz2.astype(BF16)
            dz = lax.dot_general(d2b, wg, (((1,), (1,)), ((), ())), preferred_element_type=F32)
            dz_s[pl.ds(r0, rc), :] = dz
            qpad[pl.ds(r0, rc), :] = dz / cnt
            return (dsc + jnp.sum(dzs * z2, axis=0, keepdims=True), dbg + jnp.sum(dz2, axis=0, keepdims=True),
                    dwg + lax.dot_general(zb, d2b, (((0,), (0,)), ((), ())), preferred_element_type=F32))

        dsc, dbg, dwg = lax.fori_loop(0, t // rc, chunk, (zrow, zrow, jnp.zeros((gw, gw), F32)))
        dsc_ref[...] = dsc
        dbg_ref[...] = dbg
        dwg_ref[...] = dwg

        def spread(i, carry):
            r0 = pl.multiple_of(i * rc, rc)
            levels, cur = [], qpad[pl.ds(r0, rc + POOL_HALO), :]
            for k in range(len(POOL_WINDOWS)):
                cur = cur + _up(cur, 1 << k)
                levels.append(cur[:rc])
            du_ref[pl.ds(r0, rc), :] = (_pick_level(g, levels) - dz_s[pl.ds(r0, rc), :]).astype(BF16)
            return carry

        lax.fori_loop(0, t // rc, spread, 0)

    blk = lambda g: (0, g)
    vec = jax.ShapeDtypeStruct((1, d), F32)
    return pl.pallas_call(
        lambda *refs: body(*refs[len(deps):]),
        name=name,
        grid=(len(POOL_WINDOWS),),
        in_specs=[_ANY] * len(deps) + _pool_specs(t, gw)[:1] + [pl.BlockSpec((t, gw), blk)] + _pool_specs(t, gw)[1:],
        out_specs=[pl.BlockSpec((t, gw), blk), pl.BlockSpec((None, gw, gw), lambda g: (g, 0, 0)),
                   pl.BlockSpec((1, gw), blk), pl.BlockSpec((1, gw), blk)],
        out_shape=[jax.ShapeDtypeStruct((t, d), BF16), jax.ShapeDtypeStruct((len(POOL_WINDOWS), gw, gw), F32), vec, vec],
        scratch_shapes=[pltpu.VMEM((t + POOL_HALO, gw), F32), pltpu.VMEM((t + POOL_HALO, gw), F32), pltpu.VMEM((t, gw), F32)],
        compiler_params=_params("parallel"),
    )(*deps, u, dzs, w_grp, b_grp, scale)


def _place():
    return lax.axis_index("x"), lax.axis_index("y"), lax.axis_index("c")


def _other_chips(x, y):
    return [(1 - x, y), (x, 1 - y), (1 - x, 1 - y)]


def _half(c, rows):
    h = rows // 2
    return pl.ds(pl.multiple_of(c * h, 8), h)


_ANY = pl.BlockSpec(memory_space=pl.ANY)


def into_block(name, shards, layer, r, me, dtype):
    c = shards.shape[1]
    tr = _tile(r, 512, 16)
    per = r // tr

    def body(me_ref, s_ref, o_ref):
        o_ref[...] = s_ref[...].astype(o_ref.dtype)

    return pl.pallas_call(
        body,
        name=name,
        grid_spec=pltpu.PrefetchScalarGridSpec(
            num_scalar_prefetch=1,
            grid=(per,),
            in_specs=[pl.BlockSpec((tr, c), lambda i, me_ref: (layer * per + i, 0))],
            out_specs=pl.BlockSpec((None, tr, c), lambda i, me_ref: (me_ref[0], i, 0)),
        ),
        out_shape=jax.ShapeDtypeStruct((N_CHIPS, r, c), dtype),
        compiler_params=_params("parallel"),
    )(me, shards)


_HBM = pl.BlockSpec(memory_space=pltpu.HBM)
_SEM = pl.BlockSpec(memory_space=pltpu.SEMAPHORE)


def _in_hbm(a):
    return pltpu.with_memory_space_constraint(a, pltpu.HBM)


def split_start(name, plan, n_copies, bufs, dep):
    n = len(bufs)

    def body(*refs):
        for cp in plan(refs[:n], refs[n + 1], refs[n + 2]):
            cp.start()
        refs[-1][...] = jnp.zeros_like(refs[-1])

    res = pl.pallas_call(
        body,
        name=name,
        in_specs=[_HBM] * n + [_ANY],
        out_specs=[_SEM, _SEM] + [_HBM] * n + [pl.BlockSpec(memory_space=pltpu.VMEM)],
        out_shape=[pltpu.SemaphoreType.DMA((n_copies,)), pltpu.SemaphoreType.DMA((n_copies,))]
        + [pltpu.HBM(b.shape, b.dtype) for b in bufs] + [jax.ShapeDtypeStruct((8, 128), F32)],
        input_output_aliases={i: 2 + i for i in range(n)},
        compiler_params=pltpu.CompilerParams(has_side_effects=pltpu.SideEffectType.DATAFLOW_SIDE_EFFECTING),
    )(*[_in_hbm(b) for b in bufs], dep)
    return res[0], res[1], list(res[2:2 + n]), res[-1]


def split_wait(name, plan, send_sems, recv_sems, bufs, after):
    n = len(bufs)

    def body(*refs):
        copies = plan(refs[:n], refs[n], refs[n + 1])
        for cp in copies:
            cp.wait_send()
        for cp in copies:
            cp.wait_recv()

    return pl.pallas_call(
        body,
        name=name,
        in_specs=[_HBM] * n + [_SEM, _SEM, _ANY],
        out_specs=[_HBM] * n,
        out_shape=[pltpu.HBM(b.shape, b.dtype) for b in bufs],
        input_output_aliases={i: i for i in range(n)},
        compiler_params=pltpu.CompilerParams(has_side_effects=pltpu.SideEffectType.DATAFLOW_SIDE_EFFECTING),
    )(*bufs, send_sems, recv_sems, after)


def gather_plan(n):
    def plan(bufs, send_sems, recv_sems):
        x, y, c = _place()
        copies = []
        for i in range(n):
            blk = bufs[i].at[2 * x + y, _half(c, bufs[i].shape[1]), :]
            for j, chip in enumerate(_other_chips(x, y)):
                copies.append(pltpu.make_async_remote_copy(
                    src_ref=blk, dst_ref=blk, send_sem=send_sems.at[3 * i + j], recv_sem=recv_sems.at[3 * i + j],
                    device_id=(*chip, c), device_id_type=MESH))
        return copies

    return plan


def pair_forward(name, bufs):
    n = len(bufs)

    def body(*refs):
        outs = refs[n:2 * n]
        send_sems, recv_sems = refs[2 * n:]
        x, y, c = _place()
        copies = []
        for i in range(n):
            for j, (cx, cy) in enumerate(_other_chips(x, y)):
                blk = outs[i].at[2 * cx + cy, _half(c, outs[i].shape[1]), :]
                copies.append(pltpu.make_async_remote_copy(
                    src_ref=blk, dst_ref=blk, send_sem=send_sems.at[3 * i + j], recv_sem=recv_sems.at[3 * i + j],
                    device_id=(x, y, 1 - c), device_id_type=MESH))
        for cp in copies:
            cp.start()
        for cp in copies:
            cp.wait()

    return pl.pallas_call(
        body,
        name=name,
        in_specs=[_ANY] * n,
        out_specs=[_ANY] * n,
        out_shape=[jax.ShapeDtypeStruct(b.shape, b.dtype) for b in bufs],
        input_output_aliases={i: i for i in range(n)},
        scratch_shapes=[pltpu.SemaphoreType.DMA((3 * n,)), pltpu.SemaphoreType.DMA((3 * n,))],
    )(*bufs)


def all_gather_chips(name, bufs):
    n = len(bufs)

    def body(*refs):
        outs = refs[n:2 * n]
        send_sems, recv_sems = refs[2 * n:]
        x, y, c = _place()
        me, sibling = 2 * x + y, (x, y, 1 - c)
        chips = _other_chips(x, y)

        def copy(i, slot, block, half, to):
            blk = outs[i].at[block, _half(half, outs[i].shape[1]), :]
            return pltpu.make_async_remote_copy(
                src_ref=blk, dst_ref=blk, send_sem=send_sems.at[i * 6 + slot], recv_sem=recv_sems.at[i * 6 + slot],
                device_id=to, device_id_type=MESH)

        first = [copy(i, j, me, c, (*chip, c)) for i in range(n) for j, chip in enumerate(chips)]
        for cp in first:
            cp.start()
        passed = []
        for i in range(n):
            for j, (cx, cy) in enumerate(chips):
                copy(i, j, 2 * cx + cy, c, (x, y, c)).wait_recv()
                fwd = copy(i, 3 + j, 2 * cx + cy, c, sibling)
                fwd.start()
                passed.append(fwd)
        for i in range(n):
            for j, (cx, cy) in enumerate(chips):
                copy(i, 3 + j, 2 * cx + cy, 1 - c, (x, y, c)).wait_recv()
        for cp in first + passed:
            cp.wait_send()

    return pl.pallas_call(
        body,
        name=name,
        in_specs=[_ANY] * n,
        out_specs=[_ANY] * n,
        out_shape=[jax.ShapeDtypeStruct(b.shape, b.dtype) for b in bufs],
        input_output_aliases={i: i for i in range(n)},
        scratch_shapes=[pltpu.SemaphoreType.DMA((6 * n,)), pltpu.SemaphoreType.DMA((6 * n,))],
    )(*bufs)


def pair_plan(n):
    def plan(bufs, send_sems, recv_sems):
        x, y, c = _place()
        return [pltpu.make_async_remote_copy(
            src_ref=bufs[i].at[:, _half(1 - c, bufs[i].shape[1]), :], dst_ref=bufs[n + i], send_sem=send_sems.at[i],
            recv_sem=recv_sems.at[i], device_id=(x, y, 1 - c), device_id_type=MESH) for i in range(n)]

    return plan


def chip_plan(n):
    def plan(bufs, send_sems, recv_sems):
        x, y, c = _place()
        copies = []
        for i in range(n):
            for j, (cx, cy) in enumerate(_other_chips(x, y)):
                copies.append(pltpu.make_async_remote_copy(
                    src_ref=bufs[i].at[2 * cx + cy], dst_ref=bufs[n + i].at[2 * x + y], send_sem=send_sems.at[3 * i + j],
                    recv_sem=recv_sems.at[3 * i + j], device_id=(cx, cy, c), device_id_type=MESH))
        return copies

    return plan


def exchange(name, plan, n_copies, srcs, land_shapes, deps=()):
    n, n_d = len(srcs), len(deps)

    def body(*refs):
        copies = plan(refs[:n] + refs[n + n_d:2 * n + n_d], refs[2 * n + n_d], refs[2 * n + n_d + 1])
        for cp in copies:
            cp.start()
        for cp in copies:
            cp.wait()

    return pl.pallas_call(
        body,
        name=name,
        in_specs=[_ANY] * (n + n_d),
        out_specs=[_ANY] * n,
        out_shape=land_shapes,
        scratch_shapes=[pltpu.SemaphoreType.DMA((n_copies,)), pltpu.SemaphoreType.DMA((n_copies,))],
    )(*srcs, *deps)


def pair_lands(grads):
    return [jax.ShapeDtypeStruct((g.shape[0], g.shape[1] // 2, g.shape[2]), g.dtype) for g in grads]


def pair_exchange(name, grads, deps=()):
    return exchange(name, pair_plan(len(grads)), len(grads), grads, pair_lands(grads), deps)


def chip_exchange(name, parts):
    return exchange(name, chip_plan(len(parts)), 3 * len(parts), parts, [jax.ShapeDtypeStruct(p.shape, p.dtype) for p in parts])


def pair_gather(name, bufs, blocked, layers):
    n = len(bufs)
    n_copies = sum(layers)

    def body(*refs):
        outs = refs[n:2 * n]
        send_sems, recv_sems = refs[2 * n:]
        x, y, c = _place()
        copies = []
        for i in range(n):
            buf = outs[i].at[2 * x + y] if blocked[i] else outs[i]
            r = buf.shape[0] // layers[i]
            for l in range(layers[i]):
                mine = buf.at[pl.ds(pl.multiple_of(l * r + c * (r // 2), 8), r // 2), :]
                copies.append(pltpu.make_async_remote_copy(
                    src_ref=mine, dst_ref=mine, send_sem=send_sems.at[len(copies)], recv_sem=recv_sems.at[len(copies)],
                    device_id=(x, y, 1 - c), device_id_type=MESH))
        for cp in copies:
            cp.start()
        for cp in copies:
            cp.wait()

    return pl.pallas_call(
        body,
        name=name,
        in_specs=[_ANY] * n,
        out_specs=[_ANY] * n,
        out_shape=[jax.ShapeDtypeStruct(b.shape, b.dtype) for b in bufs],
        input_output_aliases={i: i for i in range(n)},
        scratch_shapes=[pltpu.SemaphoreType.DMA((n_copies,)), pltpu.SemaphoreType.DMA((n_copies,))],
    )(*bufs)


def pair_sum(name, grad, recv, core, dtype):
    _, r, c = grad.shape
    h = r // 2
    th = _tile(h, 512, 16)
    per = h // th

    def body(core_ref, g_ref, r_ref, o_ref):
        o_ref[...] = (g_ref[...] + r_ref[...]).astype(o_ref.dtype)

    return pl.pallas_call(
        body,
        name=name,
        grid_spec=pltpu.PrefetchScalarGridSpec(
            num_scalar_prefetch=1,
            grid=(N_CHIPS, per),
            in_specs=[pl.BlockSpec((None, th, c), lambda k, i, core_ref: (k, core_ref[0] * per + i, 0)),
                      pl.BlockSpec((None, th, c), lambda k, i, core_ref: (k, i, 0))],
            out_specs=pl.BlockSpec((None, th, c), lambda k, i, core_ref: (k, i, 0)),
        ),
        out_shape=jax.ShapeDtypeStruct((N_CHIPS, h, c), dtype),
        compiler_params=_params("parallel", "parallel"),
    )(core, grad, recv)


def chip_sum(name, got, parts, place, blocked, into=None, layer=0, n_layers=1):
    _, h, c = parts.shape
    th = _tile(h, 256, 16)
    per = h // th

    def body(place_ref, q0, q1, q2, q3, p_ref, *rest):
        o_ref = rest[-1]
        me = place_ref[0]
        own = p_ref[...].astype(F32)
        v = [jnp.where(me == k, own, q[...].astype(F32)) for k, q in enumerate((q0, q1, q2, q3))]
        o_ref[...] = ((v[0] + v[1]) + v[2]) + v[3]

    def got_spec(k):
        return pl.BlockSpec((None, th, c), lambda i, pr: (jnp.where(pr[0] == k, (k + 1) % N_CHIPS, k), i, 0))

    if blocked:
        out_spec = pl.BlockSpec((None, th, c), lambda i, pr: (pr[0], pr[1] * per + i, 0))
        out_shape = jax.ShapeDtypeStruct((N_CHIPS, 2 * h, c), F32)
    else:
        out_spec = pl.BlockSpec((th, c), lambda i, pr: ((2 * layer + pr[1]) * per + i, 0))
        out_shape = jax.ShapeDtypeStruct((n_layers * 2 * h, c), F32)
    carried = [] if into is None else [into]
    return pl.pallas_call(
        body,
        name=name,
        grid_spec=pltpu.PrefetchScalarGridSpec(
            num_scalar_prefetch=1,
            grid=(per,),
            in_specs=[got_spec(k) for k in range(N_CHIPS)] + [pl.BlockSpec((None, th, c), lambda i, pr: (pr[0], i, 0))]
            + [_ANY] * len(carried),
            out_specs=out_spec,
        ),
        out_shape=out_shape,
        input_output_aliases={6: 0} if carried else {},
        compiler_params=_params("parallel"),
    )(place, got, got, got, got, parts, *carried)


def adamw(name, w, g, m, v):
    r, c = w.shape
    tr = _tile(r, 512, 8)
    c1 = 1.0 - ADAM_B1 ** ADAM_STEP
    c2 = 1.0 - ADAM_B2 ** ADAM_STEP

    def body(w_ref, g_ref, m_ref, v_ref, d_ref, nm_ref, nv_ref):
        gv = g_ref[...]
        nm = ADAM_B1 * m_ref[...] + (1.0 - ADAM_B1) * gv
        nv = ADAM_B2 * v_ref[...] + (1.0 - ADAM_B2) * (gv * gv)
        d_ref[...] = -ADAM_LR * ((nm / c1) / (jnp.sqrt(nv / c2) + ADAM_EPS) + ADAM_WD * w_ref[...])
        nm_ref[...] = nm
        nv_ref[...] = nv

    spec = pl.BlockSpec((tr, c), lambda i: (i, 0))
    return pl.pallas_call(
        body,
        name=name,
        grid=(r // tr,),
        in_specs=[spec] * 4,
        out_specs=[spec] * 3,
        out_shape=[jax.ShapeDtypeStruct((r, c), F32)] * 3,
        compiler_params=_params("parallel"),
    )(w, g, m, v)


def adamw_small(name, ws, gs, ms, vs):
    n = len(ws)
    c1 = 1.0 - ADAM_B1 ** ADAM_STEP
    c2 = 1.0 - ADAM_B2 ** ADAM_STEP

    def body(*refs):
        for i in range(n):
            w_ref, g_ref, m_ref, v_ref = (refs[j * n + i] for j in range(4))
            d_ref, nm_ref, nv_ref = (refs[(4 + j) * n + i] for j in range(3))
            gv = g_ref[...]
            nm = ADAM_B1 * m_ref[...] + (1.0 - ADAM_B1) * gv
            nv = ADAM_B2 * v_ref[...] + (1.0 - ADAM_B2) * (gv * gv)
            d_ref[...] = -ADAM_LR * ((nm / c1) / (jnp.sqrt(nv / c2) + ADAM_EPS) + ADAM_WD * w_ref[...])
            nm_ref[...] = nm
            nv_ref[...] = nv

    whole = pl.BlockSpec(memory_space=pltpu.VMEM)
    res = pl.pallas_call(
        body,
        name=name,
        in_specs=[whole] * (4 * n),
        out_specs=[whole] * (3 * n),
        out_shape=[jax.ShapeDtypeStruct(w.shape, F32) for w in ws] * 3,
        compiler_params=pltpu.CompilerParams(vmem_limit_bytes=VMEM_LIMIT_BYTES),
    )(*ws, *gs, *ms, *vs)
    return res[:n], res[n:2 * n], res[2 * n:]


def reduce_to_shards(tag, grads, wire, blocked, place, deps=()):
    recv = pair_exchange(tag + "_pair_exchange", grads, deps)
    parts = [pair_sum(f"{tag}_pair_sum_{i}", g, r, place[1:], w) for i, (g, r, w) in enumerate(zip(grads, recv, wire))]
    got = chip_exchange(tag + "_chip_exchange", parts)
    sums = [chip_sum(f"{tag}_chip_sum_{i}", q, p, place, b) for i, (q, p, b) in enumerate(zip(got, parts, blocked))]
    return pair_gather(tag + "_pair_gather", sums, blocked, [1] * len(sums))


def _pack(arrays, row_multiple, cols=BLOB_COLS):
    flat = jnp.concatenate([a.reshape(-1).astype(F32) for a in arrays])
    rows = -(-flat.shape[0] // cols)
    rows = -(-rows // row_multiple) * row_multiple
    return jnp.pad(flat, (0, rows * cols - flat.shape[0])).reshape(rows, cols)


def _unpack(blob, shapes):
    flat, out, off = blob.reshape(-1), [], 0
    for s in shapes:
        size = math.prod(s)
        out.append(flat[off:off + size].reshape(s))
        off += size
    return out


def _unpack_rows(blobs, shapes):
    out, off = [], 0
    for s in shapes:
        size = math.prod(s)
        out.append(blobs[:, off:off + size].reshape((blobs.shape[0],) + tuple(s)))
        off += size
    return out


def kernel(x, p, lru_w_in, lru_conv_w, lru_conv_b, lru_wa, lru_ba, lru_wx, lru_bx, lru_lambda, lru_w_out, pool_w_in, pool_w_grp, pool_b_grp, pool_scale, pool_w_out, ln_mix_g, ln_mix_b, mlp_w1, mlp_w2, ln_mlp_g, ln_mlp_b, ple_w, ple_gate_w, ple_gate_b, loss_target, m_lru_w_in, m_lru_conv_w, m_lru_conv_b, m_lru_wa, m_lru_ba, m_lru_wx, m_lru_bx, m_lru_lambda, m_lru_w_out, m_pool_w_in, m_pool_w_grp, m_pool_b_grp, m_pool_scale, m_pool_w_out, m_ln_mix_g, m_ln_mix_b, m_mlp_w1, m_mlp_w2, m_ln_mlp_g, m_ln_mlp_b, m_ple_w, m_ple_gate_w, m_ple_gate_b, v_lru_w_in, v_lru_conv_w, v_lru_conv_b, v_lru_wa, v_lru_ba, v_lru_wx, v_lru_bx, v_lru_lambda, v_lru_w_out, v_pool_w_in, v_pool_w_grp, v_pool_b_grp, v_pool_scale, v_pool_w_out, v_ln_mix_g, v_ln_mix_b, v_mlp_w1, v_mlp_w2, v_ln_mlp_g, v_ln_mlp_b, v_ple_w, v_ple_gate_w, v_ple_gate_b):
    weights = dict(lru_w_in=lru_w_in, lru_conv_w=lru_conv_w, lru_conv_b=lru_conv_b, lru_wa=lru_wa, lru_ba=lru_ba, lru_wx=lru_wx, lru_bx=lru_bx, lru_lambda=lru_lambda, lru_w_out=lru_w_out, pool_w_in=pool_w_in, pool_w_grp=pool_w_grp, pool_b_grp=pool_b_grp, pool_scale=pool_scale, pool_w_out=pool_w_out, ln_mix_g=ln_mix_g, ln_mix_b=ln_mix_b, mlp_w1=mlp_w1, mlp_w2=mlp_w2, ln_mlp_g=ln_mlp_g, ln_mlp_b=ln_mlp_b, ple_w=ple_w, ple_gate_w=ple_gate_w, ple_gate_b=ple_gate_b)
    mom_m = dict(lru_w_in=m_lru_w_in, lru_conv_w=m_lru_conv_w, lru_conv_b=m_lru_conv_b, lru_wa=m_lru_wa, lru_ba=m_lru_ba, lru_wx=m_lru_wx, lru_bx=m_lru_bx, lru_lambda=m_lru_lambda, lru_w_out=m_lru_w_out, pool_w_in=m_pool_w_in, pool_w_grp=m_pool_w_grp, pool_b_grp=m_pool_b_grp, pool_scale=m_pool_scale, pool_w_out=m_pool_w_out, ln_mix_g=m_ln_mix_g, ln_mix_b=m_ln_mix_b, mlp_w1=m_mlp_w1, mlp_w2=m_mlp_w2, ln_mlp_g=m_ln_mlp_g, ln_mlp_b=m_ln_mlp_b, ple_w=m_ple_w, ple_gate_w=m_ple_gate_w, ple_gate_b=m_ple_gate_b)
    mom_v = dict(lru_w_in=v_lru_w_in, lru_conv_w=v_lru_conv_w, lru_conv_b=v_lru_conv_b, lru_wa=v_lru_wa, lru_ba=v_lru_ba, lru_wx=v_lru_wx, lru_bx=v_lru_bx, lru_lambda=v_lru_lambda, lru_w_out=v_lru_w_out, pool_w_in=v_pool_w_in, pool_w_grp=v_pool_w_grp, pool_b_grp=v_pool_b_grp, pool_scale=v_pool_scale, pool_w_out=v_pool_w_out, ln_mix_g=v_ln_mix_g, ln_mix_b=v_ln_mix_b, mlp_w1=v_mlp_w1, mlp_w2=v_mlp_w2, ln_mlp_g=v_ln_mlp_g, ln_mlp_b=v_ln_mlp_b, ple_w=v_ple_w, ple_gate_w=v_ple_gate_w, ple_gate_b=v_ple_gate_b)
    names = list(weights)

    depth, d = ln_mix_g.shape
    t = x.shape[1]
    n_a, n_b = lru_w_in.shape[0], pool_w_in.shape[0]
    d_rnn = lru_w_out.shape[1] * N_CHIPS
    heads = d_rnn // LRU_BW
    d_ff = mlp_w1.shape[2] * N_CHIPS
    ple_dim = ple_w.shape[1]
    n_grp = len(POOL_WINDOWS)
    gw = d // n_grp
    alpha = (2 * depth) ** 0.25
    chip = 2 * lax.axis_index("x") + lax.axis_index("y")
    place = jnp.stack([chip, lax.axis_index("c")]).astype(jnp.int32)

    x2d = x.reshape(t, d)
    target = loss_target.reshape(t, d)
    p3 = p.reshape(depth, t, ple_dim)

    big = ["lru_w_in", "lru_w_out", "pool_w_in", "pool_w_out", "mlp_w1", "mlp_w2", "ple_w", "ple_gate_w", "pool_w_grp"]
    flat2 = lambda a: a.reshape(-1, a.shape[-1])
    small_sharded = ["lru_conv_w", "pool_b_grp", "pool_scale"]
    small_blob = _pack([weights[k] for k in small_sharded], 16, cols=256)
    every_layer = ("mlp_w1", "mlp_w2", "ple_w", "ple_gate_w")

    def layer_keys(i):
        return (["lru_w_in", "lru_w_out"] if i % 2 == 0 else ["pool_w_in", "pool_w_out", "pool_w_grp"]) + list(every_layer)

    def stage(k, i):
        w = weights[k]
        return into_block(f"stage_l{i}_{k}", flat2(w), i if k in every_layer else i // 2, math.prod(w.shape[1:-1]),
                          place[:1], BF16)

    staged = [[stage(k, i) for k in layer_keys(i)] for i in range(depth)]
    first = all_gather_chips("gather_l0", staged[0][:1] + [into_block("stage_small", small_blob, 0, small_blob.shape[0], place[:1], F32)])
    wg = {(layer_keys(0)[0], 0): first[0]}

    tokens = []

    def take_tokens():
        deps = tuple(tokens)
        tokens.clear()
        return deps

    def mm(*args, **kwargs):
        return matmul(*args, deps=take_tokens(), **kwargs)

    def start_gather(tag, bufs, dep):
        plan = gather_plan(len(bufs))
        flight = (plan,) + split_start(f"gather_{tag}_start", plan, 3 * len(bufs), bufs, dep)
        tokens.append(flight[-1])
        return flight

    def land_gather(tag, flight, keys, layer, after):
        plan, send_sems, recv_sems, bufs, _ = flight
        landed = split_wait(f"gather_{tag}_wait", plan, send_sems, recv_sems, bufs, after)
        wg.update(zip([(k, layer) for k in keys], pair_forward(f"gather_{tag}_forward", landed)))

    conv_w_sh, b_grp_sh, scale_sh = _unpack_rows(first[-1].reshape(N_CHIPS, -1), [weights[k].shape for k in small_sharded])
    conv_w_full = jnp.moveaxis(conv_w_sh, 0, 2).reshape(n_a, CONV_WIDTH, d_rnn)
    b_grp_full = jnp.moveaxis(b_grp_sh, 0, 1).reshape(n_b, 1, d)
    scale_full = jnp.moveaxis(scale_sh, 0, 1).reshape(n_b, 1, d)
    rows_grp = gw // N_CHIPS
    w_grp_full = lambda i: jnp.moveaxis(wg["pool_w_grp", i].reshape(N_CHIPS, n_grp, rows_grp, gw), 0, 1).reshape(n_grp, gw, gw)
    wa_bf, wx_bf = lru_wa.astype(BF16), lru_wx.astype(BF16)
    row = lambda a, i: a[i].reshape(1, -1)

    saved = []
    cur, cur_bf = x2d, x2d
    for i in range(depth):
        slot = i // 2
        sv = dict(x_bf=cur_bf)
        if i == 0:
            flight = start_gather("l0_rest", staged[0][1:], first[0])
        elif i + 1 < depth:
            flight = start_gather(f"l{i + 1}", staged[i + 1], cur)
        if i % 2 == 0:
            (proj,) = mm(f"l{i}_lru_in", plain(cur_bf), colsplit(wg["lru_w_in", i], 0, d), "nn",
                         [colsplit(None, 0, t, n=2, full=(2, t, d_rnn), dtype=F32)])
            hg = lru_fwd(f"l{i}_lru", proj, conv_w_full[slot], row(lru_conv_b, slot), wa_bf[slot], row(lru_ba, slot),
                         wx_bf[slot], row(lru_bx, slot), row(lru_lambda, slot))
            if i == 0:
                land_gather("l0_rest", flight, layer_keys(0)[1:], 0, hg)
                flight = start_gather("l1", staged[1], hg)
            (mix,) = mm(f"l{i}_lru_out", plain(hg), rowsplit_whole(wg["lru_w_out", i]), "nn",
                        [plain(shape=(t, d), dtype=F32)], pk=2048)
            sv.update(proj=proj, act=hg)
        else:
            (u,) = mm(f"l{i}_pool_in", plain(cur_bf), rowsplit_whole(wg["pool_w_in", i]), "nn",
                          [plain(shape=(t, d), dtype=F32)])
            zs = pool_fwd(f"l{i}_pool", u, w_grp_full(i), b_grp_full[slot], scale_full[slot])
            (mix,) = mm(f"l{i}_pool_out", plain(zs), rowsplit_whole(wg["pool_w_out", i]), "nn",
                            [plain(shape=(t, d), dtype=F32)])
            sv.update(u=u, act=zs)
        x1, x1_bf, s1 = ln_fwd(f"l{i}_ln_mix", alpha, cur, mix, row(ln_mix_g, i), row(ln_mix_b, i))

        def relu2(acc):
            hr = jnp.maximum(acc, 0.0)
            return (hr * hr,)

        (hh,) = mm(f"l{i}_mlp_up", plain(x1_bf), colsplit(wg["mlp_w1", i], 0, d), "nn",
                   [plain(shape=(t, d_ff), dtype=BF16)], epilogue=relu2, pm=2048)
        (mlp,) = mm(f"l{i}_mlp_down", plain(hh), rowsplit_whole(wg["mlp_w2", i]), "nn",
                    [plain(shape=(t, d), dtype=F32)], pk=d_ff)
        x2, x2_bf, s2 = ln_fwd(f"l{i}_ln_mlp", alpha, x1, mlp, row(ln_mlp_g, i), row(ln_mlp_b, i))
        (e,) = mm(f"l{i}_ple", plain(p3[i]), colsplit(wg["ple_w", i], 0, ple_dim), "nn", [plain(shape=(t, d), dtype=F32)])

        def ple_out(acc, x2_t, e_t, gb):
            gate = jax.nn.sigmoid(acc + gb)
            x3 = x2_t + e_t * gate
            return x3, x3, gate

        cur, cur_bf, gate = mm(f"l{i}_ple_gate", plain(x2_bf), rowsplit_whole(wg["ple_gate_w", i]), "nn",
                                   [plain(shape=(t, d), dtype=F32), plain(shape=(t, d), dtype=BF16), plain(shape=(t, d), dtype=F32)],
                                   epilogue=ple_out, tiles=[plain(x2), plain(e)], rows=[row(ple_gate_b, i)])
        sv.update(s1=s1, x1_bf=x1_bf, hh=hh, s2=s2, x2_bf=x2_bf, gate=gate, e=e)
        saved.append(sv)
        if i + 1 < depth:
            land_gather(f"l{i + 1}", flight, layer_keys(i + 1), i + 1, cur)

    dy, loss_part = loss_head("loss", cur, target)
    loss = lax.psum(loss_part.reshape(()), ("x", "y", "c"))

    part = {}
    sums = {}

    def grad_view(key, split):
        w = weights[key]
        return split(None, 0, w.shape[1], full=(N_CHIPS, w.shape[1], w.shape[2]), dtype=F32)

    def group_start(tag, items, dep):
        srcs = [part[it] for it in items]
        plan = pair_plan(len(srcs))
        lands = [lax.empty(s.shape, s.dtype) for s in pair_lands(srcs)]
        flight = (tag, items, plan) + split_start(f"grads_{tag}_pair_start", plan, len(srcs), srcs + lands, dep)
        tokens.append(flight[-1])
        return flight

    def group_mid(flight, after):
        tag, items, plan, send_sems, recv_sems, bufs, _ = flight
        bufs = split_wait(f"grads_{tag}_pair_wait", plan, send_sems, recv_sems, bufs, after)
        n = len(items)
        parts = [pair_sum(f"grads_{tag}_pair_sum_{j}", bufs[j], bufs[n + j], place[1:], F32 if it[0] == "blob" else BF16)
                 for j, it in enumerate(items)]
        plan = chip_plan(n)
        flight = (tag, items, plan) + split_start(f"grads_{tag}_chip_start", plan, 3 * n,
                                                  parts + [lax.empty(q.shape, q.dtype) for q in parts], after)
        tokens.append(flight[-1])
        return flight

    def group_end(flight, after):
        tag, items, plan, send_sems, recv_sems, bufs, _ = flight
        bufs = split_wait(f"grads_{tag}_chip_wait", plan, send_sems, recv_sems, bufs, after)
        n = len(items)
        for j, (k, layer) in enumerate(items):
            if k == "blob":
                sums[k] = chip_sum(f"grads_{tag}_chip_sum_{j}", bufs[n + j], bufs[j], place, True)
            else:
                sums[k] = chip_sum(f"grads_{tag}_chip_sum_{j}", bufs[n + j], bufs[j], place, False, into=sums.get(k),
                                   layer=layer if k in every_layer else layer // 2, n_layers=weights[k].shape[0])

    big_w = [k for k in big if k != "pool_w_grp"]
    small_keys = [k for k in names if k not in big_w]

    small = {k: [None] * weights[k].shape[0] for k in names if k not in big or k == "pool_w_grp"}
    dcur = dy
    mlp_pair = mlp_chip = mix_pair = mix_chip = None
    for i in reversed(range(depth)):
        slot = i // 2
        sv = saved[i]
        de, dpre, dgb = ple_bwd(f"l{i}_ple_bwd", dcur, sv["gate"], sv["e"])
        small["ple_gate_b"][i] = dgb
        (part["ple_w", i],) = mm(f"l{i}_d_ple_w", plain(p3[i]), plain(de), "tn", [grad_view("ple_w", colsplit)])
        (part["ple_gate_w", i],) = mm(f"l{i}_d_ple_gate_w", plain(sv["x2_bf"]), plain(dpre), "tn",
                                          [grad_view("ple_gate_w", rowsplit)])
        (dx2b,) = mm(f"l{i}_dx2", plain(dpre), rowsplit_whole(wg["ple_gate_w", i]), "nt",
                         [plain(shape=(t, d), dtype=F32)])
        ds2, ds2_bf, dg, db = ln_bwd(f"l{i}_ln_mlp_bwd", 1.0, dcur, dx2b, sv["s2"], row(ln_mlp_g, i))
        small["ln_mlp_g"][i], small["ln_mlp_b"][i] = dg, db
        (part["mlp_w2", i],) = mm(f"l{i}_d_mlp_w2", plain(sv["hh"]), plain(ds2_bf), "tn", [grad_view("mlp_w2", rowsplit)])
        (dhpre,) = mm(f"l{i}_dh", plain(ds2_bf), rowsplit(wg["mlp_w2", i], 0, d_ff // N_CHIPS), "nt",
                      [plain(shape=(t, d_ff), dtype=BF16)], tiles=[plain(sv["hh"])], pm=2048,
                      epilogue=lambda acc, hh_t: (acc * (2.0 * jnp.sqrt(hh_t.astype(F32))),))
        (part["mlp_w1", i],) = mm(f"l{i}_d_mlp_w1", plain(sv["x1_bf"]), plain(dhpre), "tn", [grad_view("mlp_w1", colsplit)])
        if mlp_chip is not None:
            group_end(mlp_chip, dhpre)
        if mix_pair is not None:
            mix_chip = group_mid(mix_pair, dhpre)
        mlp_pair = group_start(f"l{i}_mlp", [(k, i) for k in every_layer], part["mlp_w1", i])
        (dx1b,) = mm(f"l{i}_dx1", plain(dhpre), colsplit(wg["mlp_w1", i], 0, d), "nt", [plain(shape=(t, d), dtype=F32)],
                     pm=2048)
        ds1, ds1_bf, dg, db = ln_bwd(f"l{i}_ln_mix_bwd", alpha, ds2, dx1b, sv["s1"], row(ln_mix_g, i))
        small["ln_mix_g"][i], small["ln_mix_b"][i] = dg, db
        residual = lambda acc, ds_t: (alpha * ds_t + acc,)
        if i % 2 == 0:
            (part["lru_w_out", i],) = mm(f"l{i}_d_lru_out", plain(sv["act"]), plain(ds1_bf), "tn",
                                             [grad_view("lru_w_out", rowsplit)])
            (dhg,) = mm(f"l{i}_dhg", plain(ds1_bf), rowsplit_whole(wg["lru_w_out", i]), "nt",
                            [plain(shape=(t, d_rnn), dtype=F32)], pn=2048)
            mlp_chip = group_mid(mlp_pair, dhg)
            dproj, dcw, dcb, dba, dbx, dlam, dwa, dwx = lru_bwd(
                f"l{i}_lru_bwd", sv["proj"], dhg, conv_w_full[slot], row(lru_conv_b, slot), wa_bf[slot], row(lru_ba, slot),
                wx_bf[slot], row(lru_bx, slot), row(lru_lambda, slot), deps=take_tokens())
            for key, val in (("lru_conv_w", dcw), ("lru_conv_b", dcb), ("lru_ba", dba), ("lru_bx", dbx),
                             ("lru_lambda", dlam), ("lru_wa", dwa), ("lru_wx", dwx)):
                small[key][slot] = val
            dproj_v = colsplit(dproj, 0, t, n=2)
            (part["lru_w_in", i],) = mm(f"l{i}_d_lru_in", plain(sv["x_bf"]), dproj_v, "tn", [grad_view("lru_w_in", colsplit)])
            (dcur,) = mm(f"l{i}_dx", dproj_v, colsplit(wg["lru_w_in", i], 0, d), "nt",
                             [plain(shape=(t, d), dtype=F32)], epilogue=residual, tiles=[plain(ds1)])
        else:
            (part["pool_w_out", i],) = mm(f"l{i}_d_pool_out", plain(sv["act"]), plain(ds1_bf), "tn",
                                              [grad_view("pool_w_out", rowsplit)])
            (dzs,) = mm(f"l{i}_dzs", plain(ds1_bf), rowsplit_whole(wg["pool_w_out", i]), "nt",
                            [plain(shape=(t, d), dtype=F32)])
            mlp_chip = group_mid(mlp_pair, dzs)
            du, dwg, dbg, dsc = pool_bwd(f"l{i}_pool_bwd", sv["u"], dzs, w_grp_full(i), b_grp_full[slot], scale_full[slot],
                                         deps=take_tokens())
            small["pool_w_grp"][slot], small["pool_b_grp"][slot], small["pool_scale"][slot] = dwg, dbg, dsc
            (part["pool_w_in", i],) = mm(f"l{i}_d_pool_in", plain(sv["x_bf"]), plain(du), "tn", [grad_view("pool_w_in", rowsplit)])
            (dcur,) = mm(f"l{i}_dx", plain(du), rowsplit_whole(wg["pool_w_in", i]), "nt",
                             [plain(shape=(t, d), dtype=F32)], epilogue=residual, tiles=[plain(ds1)])
        if mix_chip is not None:
            group_end(mix_chip, dcur)
        mixer = [(k, i) for k in layer_keys(i) if k not in every_layer and k != "pool_w_grp"]
        if i == 0:
            small_full = [jnp.stack(small[k]).reshape((weights[k].shape[0],) + tuple(
                s * (N_CHIPS if ax in _sharded_axis(k) else 1) for ax, s in enumerate(weights[k].shape[1:], 1))) for k in small_keys]
            blob = _pack(small_full, 64)
            part["blob", 0] = blob.reshape(N_CHIPS, blob.shape[0] // N_CHIPS, BLOB_COLS)
            mixer.append(("blob", 0))
        mix_pair = group_start(f"l{i}_mix", mixer, dcur)
    grad_x = dcur.reshape(x.shape)
    mix_chip = group_mid(mix_pair, dcur)
    group_end(mlp_chip, mix_chip[-1])
    group_end(mix_chip, sums["mlp_w1"])
    order = big_w + ["blob"]
    reduced = dict(zip(order, pair_gather("grads_pair_gather", [sums[k] for k in order], [k == "blob" for k in order],
                                          [1 if k == "blob" else weights[k].shape[0] for k in order])))
    (blob_all,) = all_gather_chips("gather_small_grads", [reduced["blob"]])
    small_grads = dict(zip(small_keys, _unpack(blob_all.reshape(blob.shape), [a.shape for a in small_full])))
    for k in small_keys:
        for ax in _sharded_axis(k):
            n = weights[k].shape[ax]
            small_grads[k] = lax.dynamic_slice_in_dim(small_grads[k], chip * n, n, axis=ax)
    grads = {k: reduced[k].reshape(weights[k].shape) for k in big_w}
    grads.update(small_grads)

    delta, new_m, new_v = {}, {}, {}
    for k in big_w:
        dl, nm, nv = adamw("adamw_" + k, flat2(weights[k]), flat2(grads[k]), flat2(mom_m[k]), flat2(mom_v[k]))
        delta[k], new_m[k], new_v[k] = (a.reshape(weights[k].shape) for a in (dl, nm, nv))
    dl, nm, nv = adamw_small("adamw_small", *[[flat2(src[k]) for k in small_keys] for src in (weights, grads, mom_m, mom_v)])
    for out, res in ((delta, dl), (new_m, nm), (new_v, nv)):
        out.update({k: a.reshape(weights[k].shape) for k, a in zip(small_keys, res)})

    return (loss, grad_x, *[grads[k] for k in names], *[delta[k] for k in names],
            *[new_m[k] for k in names], *[new_v[k] for k in names])


def _sharded_axis(key):
    return {"lru_conv_w": (2,), "pool_w_grp": (2,), "pool_b_grp": (1,), "pool_scale": (1,)}.get(key, ())
```

```python
import functools
import math

import jax
import jax.numpy as jnp
from jax import lax
from jax.experimental import pallas as pl
from jax.experimental.pallas import tpu as pltpu

F32 = jnp.float32
BF16 = jnp.bfloat16

N_CHIPS = 4
LRU_BW = 128
LRU_C = 8.0
CONV_WIDTH = 4
POOL_WINDOWS = (2, 4, 8, 16)
POOL_HALO = 16
CONV_HALO = 8
LN_EPS = 1e-5
ADAM_LR = 0.001
ADAM_B1 = 0.9
ADAM_B2 = 0.999
ADAM_EPS = 1e-08
ADAM_WD = 0.01
ADAM_STEP = 10
GELU_C = math.sqrt(2.0 / math.pi)
GELU_K = 0.044715
VMEM_LIMIT_BYTES = 56 * 1024 * 1024
MATMUL_TILE_BYTES = 44 * 1024 * 1024
MESH = pl.DeviceIdType.MESH
BLOB_COLS = 1024


def _params(*sem):
    return pltpu.CompilerParams(dimension_semantics=tuple(sem), vmem_limit_bytes=VMEM_LIMIT_BYTES)


def _tile(unit, pref, align=128):
    if unit <= pref:
        return unit
    for d in range(2, unit + 1):
        if unit % d == 0 and unit // d <= pref and (unit // d) % align == 0:
            return unit // d
    raise ValueError((unit, pref, align))


class View:
    def __init__(self, arr, shape, row_unit, col_unit, block_fn, full=None, dtype=None):
        self.arr, self.shape, self.row_unit, self.col_unit, self.block_fn = arr, shape, row_unit, col_unit, block_fn
        self.full = full if full is not None else arr.shape
        self.dtype = dtype if dtype is not None else arr.dtype

    def spec(self, tr, tc, f):
        block, idx = self.block_fn(tr, tc)
        return pl.BlockSpec(block, lambda *g: idx(*f(*g)))


def plain(arr=None, shape=None, dtype=None):
    shape = arr.shape if arr is not None else shape
    return View(arr, shape, shape[0], shape[1], lambda tr, tc: ((tr, tc), lambda rt, ct: (rt, ct)), full=shape, dtype=dtype)


def colsplit(arr, layer, rows, n=N_CHIPS, full=None, dtype=None):
    full = arr.shape if arr is not None else full
    c = full[2]

    def block_fn(tr, tc):
        assert rows % tr == 0 and c % tc == 0, (rows, tr, c, tc)
        per, rpl = c // tc, rows // tr
        return (None, tr, tc), lambda rt, ct: (ct // per, layer * rpl + rt, ct % per)

    return View(arr, (rows, n * c), rows, c, block_fn, full=full, dtype=dtype)


def rowsplit(arr, layer, rows, n=N_CHIPS, full=None, dtype=None):
    full = arr.shape if arr is not None else full
    c = full[2]

    def block_fn(tr, tc):
        assert rows % tr == 0 and c % tc == 0, (rows, tr, c, tc)
        per = rows // tr
        return (None, tr, tc), lambda rt, ct: (rt // per, layer * per + rt % per, ct)

    return View(arr, (n * rows, c), rows, c, block_fn, full=full, dtype=dtype)


def rowsplit_whole(arr):
    n, rows, c = arr.shape

    def block_fn(tr, tc):
        assert tr == n * rows and c % tc == 0, (tr, n, rows, c, tc)
        return (n, rows, tc), lambda rt, ct: (0, 0, ct)

    return View(arr, (n * rows, c), n * rows, c, block_fn)


def matmul(name, a, b, mode, outs, epilogue=None, tiles=(), rows=(), deps=(), col_sums=0, pm=1024, pn=1024, pk=1024):
    if mode == "nn":
        (m, k), (k2, n) = a.shape, b.shape
        um, uk, un = a.row_unit, min(a.col_unit, b.row_unit), b.col_unit
        dims = (((1,), (0,)), ((), ()))
    elif mode == "nt":
        (m, k), (n, k2) = a.shape, b.shape
        um, uk, un = a.row_unit, min(a.col_unit, b.col_unit), b.row_unit
        dims = (((1,), (1,)), ((), ()))
    else:
        (k, m), (k2, n) = a.shape, b.shape
        um, uk, un = a.col_unit, min(a.row_unit, b.row_unit), b.col_unit
        dims = (((0,), (0,)), ((), ()))
    assert k == k2, (name, a.shape, b.shape)
    for o in list(outs) + list(tiles):
        assert o.shape == (m, n), (name, o.shape, m, n)
        um, un = min(um, o.row_unit), min(un, o.col_unit)
    tm, tn, tk = _tile(um, pm), _tile(un, pn), _tile(uk, pk)
    if mode == "tn" and uk == k:
        size = lambda v: jnp.dtype(v.dtype).itemsize
        need = 2 * k * (tm * size(a) + tn * size(b)) + 2 * tm * tn * sum(size(o) for o in outs)
        if need <= MATMUL_TILE_BYTES:
            tk = k
    assert m % tm == 0 and n % tn == 0 and k % tk == 0, (name, m, n, k, tm, tn, tk)
    gm, gn, gk = m // tm, n // tn, k // tk
    assert not col_sums or gn == 1, (name, gn)

    if mode == "nn":
        a_spec = a.spec(tm, tk, lambda i, j, kk: (i, kk))
        b_spec = b.spec(tk, tn, lambda i, j, kk: (kk, j))
    elif mode == "nt":
        a_spec = a.spec(tm, tk, lambda i, j, kk: (i, kk))
        b_spec = b.spec(tn, tk, lambda i, j, kk: (j, kk))
    else:
        a_spec = a.spec(tk, tm, lambda i, j, kk: (kk, i))
        b_spec = b.spec(tk, tn, lambda i, j, kk: (kk, j))
    tile_specs = [t.spec(tm, tn, lambda i, j, kk: (i, j)) for t in tiles]
    row_specs = [pl.BlockSpec((1, tn), lambda i, j, kk: (0, j)) for _ in rows]
    in_place = [o for o in outs if o.arr is not None]
    alias_specs = [pl.BlockSpec(memory_space=pl.ANY) for _ in in_place]
    out_specs = [o.spec(tm, tn, lambda i, j, kk: (i, j)) for o in outs]
    n_in = 2 + len(tiles) + len(rows)
    aliases = {}
    for o_idx, o in enumerate(outs):
        if o.arr is not None:
            aliases[n_in + in_place.index(o)] = o_idx
    n_t, n_r, n_a, n_o = len(tiles), len(rows), len(in_place) + len(deps), len(outs)
    dep_specs = [pl.BlockSpec(memory_space=pl.ANY) for _ in deps]

    def body(*refs):
        a_ref, b_ref = refs[0], refs[1]
        tile_refs = refs[2:2 + n_t]
        row_refs = refs[2 + n_t:2 + n_t + n_r]
        out_refs = refs[2 + n_t + n_r + n_a:2 + n_t + n_r + n_a + n_o]
        sum_refs = refs[2 + n_t + n_r + n_a + n_o:2 + n_t + n_r + n_a + n_o + col_sums]
        acc_ref = refs[-1] if gk > 1 else None

        def finish(acc):
            extra = [t[...] for t in tile_refs] + [r[...] for r in row_refs]
            res = epilogue(acc, *extra) if epilogue is not None else (acc,)
            for o_ref, r in zip(out_refs, res):
                o_ref[...] = r.astype(o_ref.dtype)
            for s_ref, r in zip(sum_refs, res[n_o:]):
                _accumulate(pl.program_id(0), s_ref, r)

        b_tile = b_ref[...]
        b_tile = b_tile.reshape(-1, b_tile.shape[-1])
        prod = lax.dot_general(a_ref[...].astype(BF16), b_tile.astype(BF16), dims, preferred_element_type=F32)
        if gk == 1:
            finish(prod)
        else:
            kk = pl.program_id(2)

            @pl.when(kk == 0)
            def _():
                acc_ref[...] = prod

            @pl.when(kk > 0)
            def _():
                acc_ref[...] += prod

            @pl.when(kk == gk - 1)
            def _():
                finish(acc_ref[...])

    res = pl.pallas_call(
        body,
        name=name,
        grid=(gm, gn, gk),
        in_specs=[a_spec, b_spec] + tile_specs + row_specs + alias_specs + dep_specs,
        out_specs=out_specs + [pl.BlockSpec((1, tn), lambda i, j, kk: (0, 0))] * col_sums,
        out_shape=[jax.ShapeDtypeStruct(o.full, o.dtype) for o in outs] + [jax.ShapeDtypeStruct((1, n), F32)] * col_sums,
        scratch_shapes=[pltpu.VMEM((tm, tn), F32)] if gk > 1 else [],
        input_output_aliases=aliases,
        compiler_params=_params(*(["arbitrary"] * 3 if col_sums else ["parallel", "parallel", "arbitrary"])),
    )(a.arr, b.arr, *[t.arr for t in tiles], *rows, *[o.arr for o in in_place], *deps)
    return res


def rows_call(name, fn, tiled, vecs, tiled_out, acc_out, tr=512):
    t = tiled[0].shape[0]
    tr = min(tr, t)
    assert t % tr == 0
    n1, n2, n3 = len(tiled), len(vecs), len(tiled_out)

    def body(*refs):
        fn(pl.program_id(0), refs[:n1], refs[n1:n1 + n2], refs[n1 + n2:n1 + n2 + n3], refs[n1 + n2 + n3:])

    return pl.pallas_call(
        body,
        name=name,
        grid=(t // tr,),
        in_specs=[pl.BlockSpec((tr, x.shape[1]), lambda i: (i, 0)) for x in tiled]
        + [pl.BlockSpec(v.shape, lambda i: (0, 0)) for v in vecs],
        out_specs=[pl.BlockSpec((tr, c), lambda i: (i, 0)) for c, _ in tiled_out]
        + [pl.BlockSpec(s, lambda i: (0, 0)) for s, _ in acc_out],
        out_shape=[jax.ShapeDtypeStruct((t, c), d) for c, d in tiled_out] + [jax.ShapeDtypeStruct(s, d) for s, d in acc_out],
        compiler_params=_params("arbitrary" if acc_out else "parallel"),
    )(*tiled, *vecs)


def _accumulate(step, ref, val):
    @pl.when(step == 0)
    def _():
        ref[...] = val

    @pl.when(step > 0)
    def _():
        ref[...] += val


def _ln_stats(s):
    mu = jnp.mean(s, axis=-1, keepdims=True)
    d = s - mu
    var = jnp.mean(d * d, axis=-1, keepdims=True)
    rstd = lax.rsqrt(var + LN_EPS)
    return d * rstd, rstd


def ln_fwd(name, alpha, x_in, m, g, b):
    d = x_in.shape[1]

    def fn(step, tiled, vecs, outs, accs):
        s = alpha * tiled[0][...] + tiled[1][...]
        xhat, _ = _ln_stats(s)
        y = xhat * vecs[0][...] + vecs[1][...]
        outs[0][...] = y
        outs[1][...] = y.astype(BF16)
        outs[2][...] = s

    return rows_call(name, fn, [x_in, m], [g, b], [(d, F32), (d, BF16), (d, F32)], [])


def ln_bwd(name, ca, da, db, s, g):
    d = s.shape[1]

    def fn(step, tiled, vecs, outs, accs):
        dx = ca * tiled[0][...] + tiled[1][...]
        xhat, rstd = _ln_stats(tiled[2][...])
        dxh = dx * vecs[0][...]
        ds = rstd * (dxh - jnp.mean(dxh, axis=-1, keepdims=True) - xhat * jnp.mean(dxh * xhat, axis=-1, keepdims=True))
        outs[0][...] = ds
        outs[1][...] = ds.astype(BF16)
        _accumulate(step, accs[0], jnp.sum(dx * xhat, axis=0, keepdims=True))
        _accumulate(step, accs[1], jnp.sum(dx, axis=0, keepdims=True))

    return rows_call(name, fn, [da, db, s], [g], [(d, F32), (d, BF16)], [((1, d), F32), ((1, d), F32)])


def ple_bwd(name, dx3, gate, e):
    d = dx3.shape[1]

    def fn(step, tiled, vecs, outs, accs):
        dx, gt, ev = tiled[0][...], tiled[1][...], tiled[2][...]
        dpre = dx * ev * gt * (1.0 - gt)
        outs[0][...] = (dx * gt).astype(BF16)
        outs[1][...] = dpre.astype(BF16)
        _accumulate(step, accs[0], jnp.sum(dpre, axis=0, keepdims=True))

    return rows_call(name, fn, [dx3, gate, e], [], [(d, BF16), (d, BF16)], [((1, d), F32)])


def loss_head(name, y, target):
    t, d = y.shape

    def fn(step, tiled, vecs, outs, accs):
        err = tiled[0][...] - tiled[1][...]
        outs[0][...] = err * (1.0 / d)
        part = jnp.sum(jnp.sum(err * err, axis=1, keepdims=True), axis=0, keepdims=True) * (0.5 / d)
        _accumulate(step, accs[0], part)

    return rows_call(name, fn, [y, target], [], [(d, F32)], [((1, 1), F32)])


def _softplus(z):
    return jnp.maximum(z, 0.0) + jnp.log1p(jnp.exp(-jnp.abs(z)))


def _gelu(y):
    th = jnp.tanh(GELU_C * (y + GELU_K * (y * y * y)))
    cdf = 0.5 * (1.0 + th)
    return y * cdf, cdf + 0.5 * y * (1.0 - th * th) * (GELU_C * (1.0 + 3.0 * GELU_K * y * y))


def _up(win, k):
    return pltpu.roll(win, win.shape[0] - k, 0)


def _down(win, k):
    return pltpu.roll(win, k, 0)


def _lru_gates(win, row0, cw_ref, cb, wa, ba, wx, bx, sp):
    h = CONV_HALO
    u = (cb + cw_ref[3:4, :] * win[h:] + cw_ref[2:3, :] * _down(win, 1)[h:]
         + cw_ref[1:2, :] * _down(win, 2)[h:] + cw_ref[0:1, :] * _down(win, 3)[h:])
    ub = u.astype(BF16)
    r = jax.nn.sigmoid(jnp.dot(ub, wa, preferred_element_type=F32) + ba)
    ig = jax.nn.sigmoid(jnp.dot(ub, wx, preferred_element_type=F32) + bx)
    log_a = (-LRU_C) * r * sp
    a = jnp.exp(log_a)
    mult = jnp.sqrt(-jnp.tanh(log_a) * (a * a + 1.0))
    first = (row0 + lax.broadcasted_iota(jnp.int32, u.shape, 0)) == 0
    mult = jnp.where(first, 1.0, mult)
    return u, r, ig, a, mult, first


def _block_scan(a, b, reverse):
    n = a.shape[0]
    a, b = a.reshape(n // 8, 8, LRU_BW), b.reshape(n // 8, 8, LRU_BW)
    pos = lax.broadcasted_iota(jnp.int32, a.shape, 1)
    for s in (1, 2, 4):
        keep = (pos >= 8 - s) if reverse else (pos < s)
        by = 8 - s if reverse else s
        b = jnp.where(keep, b, a * pltpu.roll(b, by, 1) + b)
        a = jnp.where(keep, a, a * pltpu.roll(a, by, 1))
    return a.reshape(n, LRU_BW), b.reshape(n, LRU_BW)


def _carry_scan(a_ref, b_ref, out_ref, out_off, t, reverse):
    groups, per_step = t // 8, 8

    def step(j, h):
        for k in range(per_step):
            g = j * per_step + k
            r0 = pl.multiple_of((groups - 1 - g if reverse else g) * 8, 8)
            edge = r0 if reverse else r0 + 7
            h_out = a_ref[pl.ds(edge, 1), :] * h + b_ref[pl.ds(edge, 1), :]
            out_ref[pl.ds(pl.multiple_of(out_off + r0, 8), 8), :] = a_ref[pl.ds(r0, 8), :] * h + b_ref[pl.ds(r0, 8), :]
            h = h_out
        return h

    lax.fori_loop(0, groups // per_step, step, jnp.zeros((1, LRU_BW), F32))


def _lru_in_specs(t, heads):
    blk = lambda i: (0, i)
    return [
        pl.BlockSpec((2, t, LRU_BW), lambda i: (0, 0, i)),
        pl.BlockSpec((CONV_WIDTH, LRU_BW), blk),
        pl.BlockSpec((1, LRU_BW), blk),
        pl.BlockSpec((None, LRU_BW, LRU_BW), lambda i: (i, 0, 0)),
        pl.BlockSpec((1, LRU_BW), blk),
        pl.BlockSpec((None, LRU_BW, LRU_BW), lambda i: (i, 0, 0)),
        pl.BlockSpec((1, LRU_BW), blk),
        pl.BlockSpec((1, LRU_BW), blk),
    ]


def lru_fwd(name, proj, conv_w, conv_b, wa, ba, wx, bx, lam):
    _, t, c = proj.shape
    heads = c // LRU_BW
    rc = min(256, t)

    def body(proj_ref, cw_ref, cb_ref, wa_ref, ba_ref, wx_ref, bx_ref, lam_ref, out_ref, upad, a_s, b_s):
        upad[0:CONV_HALO, :] = jnp.zeros((CONV_HALO, LRU_BW), F32)
        upad[CONV_HALO:, :] = proj_ref[0]
        sp = _softplus(-lam_ref[...])
        cb, ba, bx, wa, wx = cb_ref[...], ba_ref[...], bx_ref[...], wa_ref[...], wx_ref[...]

        def gates(i, carry):
            r0 = pl.multiple_of(i * rc, rc)
            win = upad[pl.ds(r0, rc + CONV_HALO), :]
            u, r, ig, a, mult, _ = _lru_gates(win, r0, cw_ref, cb, wa, ba, wx, bx, sp)
            rows = pl.ds(r0, rc)
            a_s[rows, :], b_s[rows, :] = _block_scan(a, mult * (ig * u), False)
            return carry

        lax.fori_loop(0, t // rc, gates, 0)
        _carry_scan(a_s, b_s, b_s, 0, t, False)

        def gate_out(i, carry):
            r0 = pl.multiple_of(i * rc, rc)
            gy, _ = _gelu(proj_ref[1, pl.ds(r0, rc), :])
            out_ref[pl.ds(r0, rc), :] = (b_s[pl.ds(r0, rc), :] * gy).astype(BF16)
            return carry

        lax.fori_loop(0, t // rc, gate_out, 0)

    return pl.pallas_call(
        body,
        name=name,
        grid=(heads,),
        in_specs=_lru_in_specs(t, heads),
        out_specs=pl.BlockSpec((t, LRU_BW), lambda i: (0, i)),
        out_shape=jax.ShapeDtypeStruct((t, c), BF16),
        scratch_shapes=[pltpu.VMEM((t + CONV_HALO, LRU_BW), F32)] + [pltpu.VMEM((t, LRU_BW), F32)] * 2,
        compiler_params=_params("parallel"),
    )(proj, conv_w, conv_b, wa, ba, wx, bx, lam)


def lru_bwd(name, proj, dhg, conv_w, conv_b, wa, ba, wx, bx, lam, deps=()):
    _, t, c = proj.shape
    heads = c // LRU_BW
    rc = min(256, t)
    h8 = CONV_HALO

    def body(proj_ref, dhg_ref, cw_ref, cb_ref, wa_ref, ba_ref, wx_ref, bx_ref, lam_ref,
             dproj_ref, dcw_ref, dcb_ref, dba_ref, dbx_ref, dlam_ref, dwa_ref, dwx_ref,
             upad, u_s, r_s, ig_s, apad, hpad, g_s, dupad, sa_s, sb_s):
        zeros8 = jnp.zeros((h8, LRU_BW), F32)
        upad[0:h8, :] = zeros8
        upad[h8:, :] = proj_ref[0]
        hpad[0:h8, :] = zeros8
        apad[t:, :] = zeros8
        dupad[t:, :] = zeros8
        lam = lam_ref[...]
        sp = _softplus(-lam)
        cb, ba, bx, wa, wx = cb_ref[...], ba_ref[...], bx_ref[...], wa_ref[...], wx_ref[...]

        def gates(i, carry):
            r0 = pl.multiple_of(i * rc, rc)
            win = upad[pl.ds(r0, rc + h8), :]
            u, r, ig, a, mult, _ = _lru_gates(win, r0, cw_ref, cb, wa, ba, wx, bx, sp)
            u_s[pl.ds(r0, rc), :] = u
            r_s[pl.ds(r0, rc), :] = r
            ig_s[pl.ds(r0, rc), :] = ig
            rows = pl.ds(r0, rc)
            apad[rows, :] = a
            sa_s[rows, :], sb_s[rows, :] = _block_scan(a, mult * (ig * u), False)
            return carry

        lax.fori_loop(0, t // rc, gates, 0)
        _carry_scan(sa_s, sb_s, hpad, h8, t, False)

        def out_gate(i, carry):
            r0 = pl.multiple_of(i * rc, rc)
            gy, dgy = _gelu(proj_ref[1, pl.ds(r0, rc), :])
            dh = dhg_ref[pl.ds(r0, rc), :]
            hh = hpad[pl.ds(pl.multiple_of(r0 + h8, 8), rc), :]
            dproj_ref[1, pl.ds(r0, rc), :] = (dh * hh * dgy).astype(BF16)
            rows = pl.ds(r0, rc)
            a_next = _up(apad[pl.ds(r0, rc + h8), :], 1)[:rc]
            sa_s[rows, :], sb_s[rows, :] = _block_scan(a_next, dh * gy, True)
            return carry

        lax.fori_loop(0, t // rc, out_gate, 0)
        _carry_scan(sa_s, sb_s, g_s, 0, t, True)

        zrow = jnp.zeros((1, LRU_BW), F32)
        zmat = jnp.zeros((LRU_BW, LRU_BW), F32)

        def grads(i, carry):
            dsp, dba, dbx, dwa, dwx = carry
            r0 = pl.multiple_of(i * rc, rc)
            g = g_s[pl.ds(r0, rc), :]
            u, r, ig, a = u_s[pl.ds(r0, rc), :], r_s[pl.ds(r0, rc), :], ig_s[pl.ds(r0, rc), :], apad[pl.ds(r0, rc), :]
            hprev = _down(hpad[pl.ds(r0, rc + h8), :], 1)[h8:]
            first = (r0 + lax.broadcasted_iota(jnp.int32, u.shape, 0)) == 0
            log_a = (-LRU_C) * r * sp
            mult = jnp.where(first, 1.0, jnp.sqrt(-jnp.tanh(log_a) * (a * a + 1.0)))
            dmult = jnp.where(first, 0.0, g * (ig * u))
            dlog_a = g * hprev * a - dmult * (a * a) / mult
            dr = dlog_a * ((-LRU_C) * sp)
            dpre_r = dr * r * (1.0 - r)
            dpre_i = (g * mult * u) * ig * (1.0 - ig)
            pr, pi, ub = dpre_r.astype(BF16), dpre_i.astype(BF16), u.astype(BF16)
            nt = (((1,), (1,)), ((), ()))
            tn = (((0,), (0,)), ((), ()))
            du = (g * mult * ig + lax.dot_general(pr, wa, nt, preferred_element_type=F32)
                  + lax.dot_general(pi, wx, nt, preferred_element_type=F32))
            dupad[pl.ds(r0, rc), :] = du
            return (dsp + jnp.sum(dlog_a * ((-LRU_C) * r), axis=0, keepdims=True),
                    dba + jnp.sum(dpre_r, axis=0, keepdims=True),
                    dbx + jnp.sum(dpre_i, axis=0, keepdims=True),
                    dwa + lax.dot_general(ub, pr, tn, preferred_element_type=F32),
                    dwx + lax.dot_general(ub, pi, tn, preferred_element_type=F32))

        dsp, dba, dbx, dwa, dwx = lax.fori_loop(0, t // rc, grads, (zrow, zrow, zrow, zmat, zmat))
        dba_ref[...] = dba
        dbx_ref[...] = dbx
        dwa_ref[...] = dwa
        dwx_ref[...] = dwx
        dlam_ref[...] = -dsp * jax.nn.sigmoid(-lam)

        def conv_back(i, carry):
            dcb, d0, d1, d2, d3 = carry
            r0 = pl.multiple_of(i * rc, rc)
            dwin = dupad[pl.ds(r0, rc + h8), :]
            du = dwin[:rc]
            du0 = (cw_ref[3:4, :] * du + cw_ref[2:3, :] * _up(dwin, 1)[:rc]
                   + cw_ref[1:2, :] * _up(dwin, 2)[:rc] + cw_ref[0:1, :] * _up(dwin, 3)[:rc])
            dproj_ref[0, pl.ds(r0, rc), :] = du0.astype(BF16)
            win = upad[pl.ds(r0, rc + h8), :]
            red = lambda v: jnp.sum(v, axis=0, keepdims=True)
            return (dcb + red(du), d0 + red(du * _down(win, 3)[h8:]), d1 + red(du * _down(win, 2)[h8:]),
                    d2 + red(du * _down(win, 1)[h8:]), d3 + red(du * win[h8:]))

        dcb, d0, d1, d2, d3 = lax.fori_loop(0, t // rc, conv_back, (zrow,) * 5)
        dcb_ref[...] = dcb
        dcw_ref[0:1, :] = d0
        dcw_ref[1:2, :] = d1
        dcw_ref[2:3, :] = d2
        dcw_ref[3:4, :] = d3

    blk = lambda i: (0, i)
    vec = jax.ShapeDtypeStruct((1, c), F32)
    mat = jax.ShapeDtypeStruct((heads, LRU_BW, LRU_BW), F32)
    full = lambda: pltpu.VMEM((t, LRU_BW), F32)
    padded = lambda: pltpu.VMEM((t + h8, LRU_BW), F32)
    return pl.pallas_call(
        lambda *refs: body(*refs[len(deps):]),
        name=name,
        grid=(heads,),
        in_specs=[_ANY] * len(deps) + _lru_in_specs(t, heads)[:1] + [pl.BlockSpec((t, LRU_BW), blk)] + _lru_in_specs(t, heads)[1:],
        out_specs=[pl.BlockSpec((2, t, LRU_BW), lambda i: (0, 0, i)), pl.BlockSpec((CONV_WIDTH, LRU_BW), blk)]
        + [pl.BlockSpec((1, LRU_BW), blk)] * 4 + [pl.BlockSpec((None, LRU_BW, LRU_BW), lambda i: (i, 0, 0))] * 2,
        out_shape=[jax.ShapeDtypeStruct((2, t, c), BF16), jax.ShapeDtypeStruct((CONV_WIDTH, c), F32), vec, vec, vec, vec, mat, mat],
        scratch_shapes=[padded(), full(), full(), full(), padded(), padded(), full(), padded()] + [full()] * 2,
        compiler_params=_params("parallel"),
    )(*deps, proj, dhg, conv_w, conv_b, wa, ba, wx, bx, lam)


def _pick_level(g, levels):
    out = levels[-1]
    for k in range(len(levels) - 2, -1, -1):
        out = jnp.where(g == k, levels[k], out)
    return out


def _pool_z(win, g, row0, rc):
    levels, cur = [], win
    for k in range(len(POOL_WINDOWS)):
        cur = cur + _down(cur, 1 << k)
        levels.append(cur[POOL_HALO:])
    tot = _pick_level(g, levels)
    width = jnp.left_shift(2, g)
    row = row0 + lax.broadcasted_iota(jnp.int32, tot.shape, 0)
    cnt = jnp.minimum(row + 1, width).astype(F32)
    return tot / cnt - win[POOL_HALO:], cnt


def _pool_specs(t, gw):
    blk = lambda g: (0, g)
    return [pl.BlockSpec((t, gw), blk), pl.BlockSpec((None, gw, gw), lambda g: (g, 0, 0)),
            pl.BlockSpec((1, gw), blk), pl.BlockSpec((1, gw), blk)]


def pool_fwd(name, u, w_grp, b_grp, scale):
    t, d = u.shape
    gw = d // len(POOL_WINDOWS)
    rc = min(256, t)

    def body(u_ref, wg_ref, bg_ref, sc_ref, out_ref, upad):
        g = pl.program_id(0)
        upad[0:POOL_HALO, :] = jnp.zeros((POOL_HALO, gw), F32)
        upad[POOL_HALO:, :] = u_ref[...]
        wg, bg, sc = wg_ref[...], bg_ref[...], sc_ref[...]

        def chunk(i, carry):
            r0 = pl.multiple_of(i * rc, rc)
            z, _ = _pool_z(upad[pl.ds(r0, rc + POOL_HALO), :], g, r0, rc)
            z2 = jnp.dot(z.astype(BF16), wg, preferred_element_type=F32) + bg
            out_ref[pl.ds(r0, rc), :] = (z2 * sc).astype(BF16)
            return carry

        lax.fori_loop(0, t // rc, chunk, 0)

    return pl.pallas_call(
        body,
        name=name,
        grid=(len(POOL_WINDOWS),),
        in_specs=_pool_specs(t, gw),
        out_specs=pl.BlockSpec((t, gw), lambda g: (0, g)),
        out_shape=jax.ShapeDtypeStruct((t, d), BF16),
        scratch_shapes=[pltpu.VMEM((t + POOL_HALO, gw), F32)],
        compiler_params=_params("parallel"),
    )(u, w_grp, b_grp, scale)


def pool_bwd(name, u, dzs, w_grp, b_grp, scale, deps=()):
    t, d = u.shape
    gw = d // len(POOL_WINDOWS)
    rc = min(256, t)

    def body(u_ref, dzs_ref, wg_ref, bg_ref, sc_ref, du_ref, dwg_ref, dbg_ref, dsc_ref, upad, qpad, dz_s):
        g = pl.program_id(0)
        upad[0:POOL_HALO, :] = jnp.zeros((POOL_HALO, gw), F32)
        upad[POOL_HALO:, :] = u_ref[...]
        qpad[t:, :] = jnp.zeros((POOL_HALO, gw), F32)
        wg, bg, sc = wg_ref[...], bg_ref[...], sc_ref[...]
        zrow = jnp.zeros((1, gw), F32)

        def chunk(i, carry):
            dsc, dbg, dwg = carry
            r0 = pl.multiple_of(i * rc, rc)
            z, cnt = _pool_z(upad[pl.ds(r0, rc + POOL_HALO), :], g, r0, rc)
            zb = z.astype(BF16)
            z2 = jnp.dot(zb, wg, preferred_element_type=F32) + bg
            dzs = dzs_ref[pl.ds(r0, rc), :]
            dz2 = dzs * sc
            d2b = dz2.astype(BF16)
            dz = lax.dot_general(d2b, wg, (((1,), (1,)), ((), ())), preferred_element_type=F32)
            dz_s[pl.ds(r0, rc), :] = dz
            qpad[pl.ds(r0, rc), :] = dz / cnt
            return (dsc + jnp.sum(dzs * z2, axis=0, keepdims=True), dbg + jnp.sum(dz2, axis=0, keepdims=True),
                    dwg + lax.dot_general(zb, d2b, (((0,), (0,)), ((), ())), preferred_element_type=F32))

        dsc, dbg, dwg = lax.fori_loop(0, t // rc, chunk, (zrow, zrow, jnp.zeros((gw, gw), F32)))
        dsc_ref[...] = dsc
        dbg_ref[...] = dbg
        dwg_ref[...] = dwg

        def spread(i, carry):
            r0 = pl.multiple_of(i * rc, rc)
            levels, cur = [], qpad[pl.ds(r0, rc + POOL_HALO), :]
            for k in range(len(POOL_WINDOWS)):
                cur = cur + _up(cur, 1 << k)
                levels.append(cur[:rc])
            du_ref[pl.ds(r0, rc), :] = (_pick_level(g, levels) - dz_s[pl.ds(r0, rc), :]).astype(BF16)
            return carry

        lax.fori_loop(0, t // rc, spread, 0)

    blk = lambda g: (0, g)
    vec = jax.ShapeDtypeStruct((1, d), F32)
    return pl.pallas_call(
        lambda *refs: body(*refs[len(deps):]),
        name=name,
        grid=(len(POOL_WINDOWS),),
        in_specs=[_ANY] * len(deps) + _pool_specs(t, gw)[:1] + [pl.BlockSpec((t, gw), blk)] + _pool_specs(t, gw)[1:],
        out_specs=[pl.BlockSpec((t, gw), blk), pl.BlockSpec((None, gw, gw), lambda g: (g, 0, 0)),
                   pl.BlockSpec((1, gw), blk), pl.BlockSpec((1, gw), blk)],
        out_shape=[jax.ShapeDtypeStruct((t, d), BF16), jax.ShapeDtypeStruct((len(POOL_WINDOWS), gw, gw), F32), vec, vec],
        scratch_shapes=[pltpu.VMEM((t + POOL_HALO, gw), F32), pltpu.VMEM((t + POOL_HALO, gw), F32), pltpu.VMEM((t, gw), F32)],
        compiler_params=_params("parallel"),
    )(*deps, u, dzs, w_grp, b_grp, scale)


def _place():
    return lax.axis_index("x"), lax.axis_index("y"), lax.axis_index("c")


def _other_chips(x, y):
    return [(1 - x, y), (x, 1 - y), (1 - x, 1 - y)]


def _half(c, rows):
    h = rows // 2
    return pl.ds(pl.multiple_of(c * h, 8), h)


_ANY = pl.BlockSpec(memory_space=pl.ANY)


def into_block(name, shards, layer, r, me, dtype):
    c = shards.shape[1]
    tr = _tile(r, 512, 16)
    per = r // tr

    def body(me_ref, s_ref, o_ref):
        o_ref[...] = s_ref[...].astype(o_ref.dtype)

    return pl.pallas_call(
        body,
        name=name,
        grid_spec=pltpu.PrefetchScalarGridSpec(
            num_scalar_prefetch=1,
            grid=(per,),
            in_specs=[pl.BlockSpec((tr, c), lambda i, me_ref: (layer * per + i, 0))],
            out_specs=pl.BlockSpec((None, tr, c), lambda i, me_ref: (me_ref[0], i, 0)),
        ),
        out_shape=jax.ShapeDtypeStruct((N_CHIPS, r, c), dtype),
        compiler_params=_params("parallel"),
    )(me, shards)


_HBM = pl.BlockSpec(memory_space=pltpu.HBM)
_SEM = pl.BlockSpec(memory_space=pltpu.SEMAPHORE)


def _in_hbm(a):
    return pltpu.with_memory_space_constraint(a, pltpu.HBM)


def split_start(name, plan, n_copies, bufs, dep):
    n = len(bufs)

    def body(*refs):
        for cp in plan(refs[:n], refs[n + 1], refs[n + 2]):
            cp.start()
        refs[-1][...] = jnp.zeros_like(refs[-1])

    res = pl.pallas_call(
        body,
        name=name,
        in_specs=[_HBM] * n + [_ANY],
        out_specs=[_SEM, _SEM] + [_HBM] * n + [pl.BlockSpec(memory_space=pltpu.VMEM)],
        out_shape=[pltpu.SemaphoreType.DMA((n_copies,)), pltpu.SemaphoreType.DMA((n_copies,))]
        + [pltpu.HBM(b.shape, b.dtype) for b in bufs] + [jax.ShapeDtypeStruct((8, 128), F32)],
        input_output_aliases={i: 2 + i for i in range(n)},
        compiler_params=pltpu.CompilerParams(has_side_effects=pltpu.SideEffectType.DATAFLOW_SIDE_EFFECTING),
    )(*[_in_hbm(b) for b in bufs], dep)
    return res[0], res[1], list(res[2:2 + n]), res[-1]


def split_wait(name, plan, send_sems, recv_sems, bufs, after):
    n = len(bufs)

    def body(*refs):
        copies = plan(refs[:n], refs[n], refs[n + 1])
        for cp in copies:
            cp.wait_send()
        for cp in copies:
            cp.wait_recv()

    return pl.pallas_call(
        body,
        name=name,
        in_specs=[_HBM] * n + [_SEM, _SEM, _ANY],
        out_specs=[_HBM] * n,
        out_shape=[pltpu.HBM(b.shape, b.dtype) for b in bufs],
        input_output_aliases={i: i for i in range(n)},
        compiler_params=pltpu.CompilerParams(has_side_effects=pltpu.SideEffectType.DATAFLOW_SIDE_EFFECTING),
    )(*bufs, send_sems, recv_sems, after)


def gather_plan(n):
    def plan(bufs, send_sems, recv_sems):
        x, y, c = _place()
        copies = []
        for i in range(n):
            blk = bufs[i].at[2 * x + y, _half(c, bufs[i].shape[1]), :]
            for j, chip in enumerate(_other_chips(x, y)):
                copies.append(pltpu.make_async_remote_copy(
                    src_ref=blk, dst_ref=blk, send_sem=send_sems.at[3 * i + j], recv_sem=recv_sems.at[3 * i + j],
                    device_id=(*chip, c), device_id_type=MESH))
        return copies

    return plan


def pair_forward(name, bufs):
    n = len(bufs)

    def body(*refs):
        outs = refs[n:2 * n]
        send_sems, recv_sems = refs[2 * n:]
        x, y, c = _place()
        copies = []
        for i in range(n):
            for j, (cx, cy) in enumerate(_other_chips(x, y)):
                blk = outs[i].at[2 * cx + cy, _half(c, outs[i].shape[1]), :]
                copies.append(pltpu.make_async_remote_copy(
                    src_ref=blk, dst_ref=blk, send_sem=send_sems.at[3 * i + j], recv_sem=recv_sems.at[3 * i + j],
                    device_id=(x, y, 1 - c), device_id_type=MESH))
        for cp in copies:
            cp.start()
        for cp in copies:
            cp.wait()

    return pl.pallas_call(
        body,
        name=name,
        in_specs=[_ANY] * n,
        out_specs=[_ANY] * n,
        out_shape=[jax.ShapeDtypeStruct(b.shape, b.dtype) for b in bufs],
        input_output_aliases={i: i for i in range(n)},
        scratch_shapes=[pltpu.SemaphoreType.DMA((3 * n,)), pltpu.SemaphoreType.DMA((3 * n,))],
    )(*bufs)


def all_gather_chips(name, bufs):
    n = len(bufs)

    def body(*refs):
        outs = refs[n:2 * n]
        send_sems, recv_sems = refs[2 * n:]
        x, y, c = _place()
        me, sibling = 2 * x + y, (x, y, 1 - c)
        chips = _other_chips(x, y)

        def copy(i, slot, block, half, to):
            blk = outs[i].at[block, _half(half, outs[i].shape[1]), :]
            return pltpu.make_async_remote_copy(
                src_ref=blk, dst_ref=blk, send_sem=send_sems.at[i * 6 + slot], recv_sem=recv_sems.at[i * 6 + slot],
                device_id=to, device_id_type=MESH)

        first = [copy(i, j, me, c, (*chip, c)) for i in range(n) for j, chip in enumerate(chips)]
        for cp in first:
            cp.start()
        passed = []
        for i in range(n):
            for j, (cx, cy) in enumerate(chips):
                copy(i, j, 2 * cx + cy, c, (x, y, c)).wait_recv()
                fwd = copy(i, 3 + j, 2 * cx + cy, c, sibling)
                fwd.start()
                passed.append(fwd)
        for i in range(n):
            for j, (cx, cy) in enumerate(chips):
                copy(i, 3 + j, 2 * cx + cy, 1 - c, (x, y, c)).wait_recv()
        for cp in first + passed:
            cp.wait_send()

    return pl.pallas_call(
        body,
        name=name,
        in_specs=[_ANY] * n,
        out_specs=[_ANY] * n,
        out_shape=[jax.ShapeDtypeStruct(b.shape, b.dtype) for b in bufs],
        input_output_aliases={i: i for i in range(n)},
        scratch_shapes=[pltpu.SemaphoreType.DMA((6 * n,)), pltpu.SemaphoreType.DMA((6 * n,))],
    )(*bufs)


def pair_plan(n):
    def plan(bufs, send_sems, recv_sems):
        x, y, c = _place()
        return [pltpu.make_async_remote_copy(
            src_ref=bufs[i].at[:, _half(1 - c, bufs[i].shape[1]), :], dst_ref=bufs[n + i], send_sem=send_sems.at[i],
            recv_sem=recv_sems.at[i], device_id=(x, y, 1 - c), device_id_type=MESH) for i in range(n)]

    return plan


def chip_plan(n):
    def plan(bufs, send_sems, recv_sems):
        x, y, c = _place()
        copies = []
        for i in range(n):
            for j, (cx, cy) in enumerate(_other_chips(x, y)):
                copies.append(pltpu.make_async_remote_copy(
                    src_ref=bufs[i].at[2 * cx + cy], dst_ref=bufs[n + i].at[2 * x + y], send_sem=send_sems.at[3 * i + j],
                    recv_sem=recv_sems.at[3 * i + j], device_id=(cx, cy, c), device_id_type=MESH))
        return copies

    return plan


def pair_lands(grads):
    return [jax.ShapeDtypeStruct((g.shape[0], g.shape[1] // 2, g.shape[2]), g.dtype) for g in grads]


def pair_gather(name, bufs, blocked, layers):
    n = len(bufs)
    n_copies = sum(layers)

    def body(*refs):
        outs = refs[n:2 * n]
        send_sems, recv_sems = refs[2 * n:]
        x, y, c = _place()
        copies = []
        for i in range(n):
            buf = outs[i].at[2 * x + y] if blocked[i] else outs[i]
            r = buf.shape[0] // layers[i]
            for l in range(layers[i]):
                mine = buf.at[pl.ds(pl.multiple_of(l * r + c * (r // 2), 8), r // 2), :]
                copies.append(pltpu.make_async_remote_copy(
                    src_ref=mine, dst_ref=mine, send_sem=send_sems.at[len(copies)], recv_sem=recv_sems.at[len(copies)],
                    device_id=(x, y, 1 - c), device_id_type=MESH))
        for cp in copies:
            cp.start()
        for cp in copies:
            cp.wait()

    return pl.pallas_call(
        body,
        name=name,
        in_specs=[_ANY] * n,
        out_specs=[_ANY] * n,
        out_shape=[jax.ShapeDtypeStruct(b.shape, b.dtype) for b in bufs],
        input_output_aliases={i: i for i in range(n)},
        scratch_shapes=[pltpu.SemaphoreType.DMA((n_copies,)), pltpu.SemaphoreType.DMA((n_copies,))],
    )(*bufs)


def pair_sum(name, grad, recv, core, dtype):
    _, r, c = grad.shape
    h = r // 2
    th = _tile(h, 1024, 16)
    per = h // th

    def body(core_ref, g_ref, r_ref, o_ref):
        o_ref[...] = (g_ref[...].astype(F32) + r_ref[...].astype(F32)).astype(o_ref.dtype)

    return pl.pallas_call(
        body,
        name=name,
        grid_spec=pltpu.PrefetchScalarGridSpec(
            num_scalar_prefetch=1,
            grid=(N_CHIPS, per),
            in_specs=[pl.BlockSpec((None, th, c), lambda k, i, core_ref: (k, core_ref[0] * per + i, 0)),
                      pl.BlockSpec((None, th, c), lambda k, i, core_ref: (k, i, 0))],
            out_specs=pl.BlockSpec((None, th, c), lambda k, i, core_ref: (k, i, 0)),
        ),
        out_shape=jax.ShapeDtypeStruct((N_CHIPS, h, c), dtype),
        compiler_params=_params("parallel", "parallel"),
    )(core, grad, recv)


def chip_sum(name, got, parts, place, blocked, into=None, layer=0, n_layers=1):
    _, h, c = parts.shape
    th = _tile(h, 512, 16)
    per = h // th

    def body(place_ref, q0, q1, q2, q3, p_ref, *rest):
        o_ref = rest[-1]
        me = place_ref[0]
        own = p_ref[...].astype(F32)
        v = [jnp.where(me == k, own, q[...].astype(F32)) for k, q in enumerate((q0, q1, q2, q3))]
        o_ref[...] = ((v[0] + v[1]) + v[2]) + v[3]

    def got_spec(k):
        return pl.BlockSpec((None, th, c), lambda i, pr: (jnp.where(pr[0] == k, (k + 1) % N_CHIPS, k), i, 0))

    if blocked:
        out_spec = pl.BlockSpec((None, th, c), lambda i, pr: (pr[0], pr[1] * per + i, 0))
        out_shape = jax.ShapeDtypeStruct((N_CHIPS, 2 * h, c), F32)
    else:
        out_spec = pl.BlockSpec((th, c), lambda i, pr: ((2 * layer + pr[1]) * per + i, 0))
        out_shape = jax.ShapeDtypeStruct((n_layers * 2 * h, c), F32)
    carried = [] if into is None else [into]
    return pl.pallas_call(
        body,
        name=name,
        grid_spec=pltpu.PrefetchScalarGridSpec(
            num_scalar_prefetch=1,
            grid=(per,),
            in_specs=[got_spec(k) for k in range(N_CHIPS)] + [pl.BlockSpec((None, th, c), lambda i, pr: (pr[0], i, 0))]
            + [_ANY] * len(carried),
            out_specs=out_spec,
        ),
        out_shape=out_shape,
        input_output_aliases={6: 0} if carried else {},
        compiler_params=_params("parallel"),
    )(place, got, got, got, got, parts, *carried)


def adamw(name, w, g, m, v):
    r, c = w.shape
    tr = _tile(r, 512, 8)
    c1 = 1.0 - ADAM_B1 ** ADAM_STEP
    c2 = 1.0 - ADAM_B2 ** ADAM_STEP

    def body(w_ref, g_ref, m_ref, v_ref, d_ref, nm_ref, nv_ref):
        gv = g_ref[...]
        nm = ADAM_B1 * m_ref[...] + (1.0 - ADAM_B1) * gv
        nv = ADAM_B2 * v_ref[...] + (1.0 - ADAM_B2) * (gv * gv)
        d_ref[...] = -ADAM_LR * ((nm / c1) / (jnp.sqrt(nv / c2) + ADAM_EPS) + ADAM_WD * w_ref[...])
        nm_ref[...] = nm
        nv_ref[...] = nv

    spec = pl.BlockSpec((tr, c), lambda i: (i, 0))
    return pl.pallas_call(
        body,
        name=name,
        grid=(r // tr,),
        in_specs=[spec] * 4,
        out_specs=[spec] * 3,
        out_shape=[jax.ShapeDtypeStruct((r, c), F32)] * 3,
        compiler_params=_params("parallel"),
    )(w, g, m, v)


def adamw_small(name, ws, gs, ms, vs):
    n = len(ws)
    c1 = 1.0 - ADAM_B1 ** ADAM_STEP
    c2 = 1.0 - ADAM_B2 ** ADAM_STEP

    def body(*refs):
        for i in range(n):
            w_ref, g_ref, m_ref, v_ref = (refs[j * n + i] for j in range(4))
            d_ref, nm_ref, nv_ref = (refs[(4 + j) * n + i] for j in range(3))
            gv = g_ref[...]
            nm = ADAM_B1 * m_ref[...] + (1.0 - ADAM_B1) * gv
            nv = ADAM_B2 * v_ref[...] + (1.0 - ADAM_B2) * (gv * gv)
            d_ref[...] = -ADAM_LR * ((nm / c1) / (jnp.sqrt(nv / c2) + ADAM_EPS) + ADAM_WD * w_ref[...])
            nm_ref[...] = nm
            nv_ref[...] = nv

    whole = pl.BlockSpec(memory_space=pltpu.VMEM)
    res = pl.pallas_call(
        body,
        name=name,
        in_specs=[whole] * (4 * n),
        out_specs=[whole] * (3 * n),
        out_shape=[jax.ShapeDtypeStruct(w.shape, F32) for w in ws] * 3,
        compiler_params=pltpu.CompilerParams(vmem_limit_bytes=VMEM_LIMIT_BYTES),
    )(*ws, *gs, *ms, *vs)
    return res[:n], res[n:2 * n], res[2 * n:]


def _pack(arrays, row_multiple, cols=BLOB_COLS):
    flat = jnp.concatenate([a.reshape(-1).astype(F32) for a in arrays])
    rows = -(-flat.shape[0] // cols)
    rows = -(-rows // row_multiple) * row_multiple
    return jnp.pad(flat, (0, rows * cols - flat.shape[0])).reshape(rows, cols)


def _unpack(blob, shapes):
    flat, out, off = blob.reshape(-1), [], 0
    for s in shapes:
        size = math.prod(s)
        out.append(flat[off:off + size].reshape(s))
        off += size
    return out


def _unpack_rows(blobs, shapes):
    out, off = [], 0
    for s in shapes:
        size = math.prod(s)
        out.append(blobs[:, off:off + size].reshape((blobs.shape[0],) + tuple(s)))
        off += size
    return out


def kernel(x, p, lru_w_in, lru_conv_w, lru_conv_b, lru_wa, lru_ba, lru_wx, lru_bx, lru_lambda, lru_w_out, pool_w_in, pool_w_grp, pool_b_grp, pool_scale, pool_w_out, ln_mix_g, ln_mix_b, mlp_w1, mlp_w2, ln_mlp_g, ln_mlp_b, ple_w, ple_gate_w, ple_gate_b, loss_target, m_lru_w_in, m_lru_conv_w, m_lru_conv_b, m_lru_wa, m_lru_ba, m_lru_wx, m_lru_bx, m_lru_lambda, m_lru_w_out, m_pool_w_in, m_pool_w_grp, m_pool_b_grp, m_pool_scale, m_pool_w_out, m_ln_mix_g, m_ln_mix_b, m_mlp_w1, m_mlp_w2, m_ln_mlp_g, m_ln_mlp_b, m_ple_w, m_ple_gate_w, m_ple_gate_b, v_lru_w_in, v_lru_conv_w, v_lru_conv_b, v_lru_wa, v_lru_ba, v_lru_wx, v_lru_bx, v_lru_lambda, v_lru_w_out, v_pool_w_in, v_pool_w_grp, v_pool_b_grp, v_pool_scale, v_pool_w_out, v_ln_mix_g, v_ln_mix_b, v_mlp_w1, v_mlp_w2, v_ln_mlp_g, v_ln_mlp_b, v_ple_w, v_ple_gate_w, v_ple_gate_b):
    weights = dict(lru_w_in=lru_w_in, lru_conv_w=lru_conv_w, lru_conv_b=lru_conv_b, lru_wa=lru_wa, lru_ba=lru_ba, lru_wx=lru_wx, lru_bx=lru_bx, lru_lambda=lru_lambda, lru_w_out=lru_w_out, pool_w_in=pool_w_in, pool_w_grp=pool_w_grp, pool_b_grp=pool_b_grp, pool_scale=pool_scale, pool_w_out=pool_w_out, ln_mix_g=ln_mix_g, ln_mix_b=ln_mix_b, mlp_w1=mlp_w1, mlp_w2=mlp_w2, ln_mlp_g=ln_mlp_g, ln_mlp_b=ln_mlp_b, ple_w=ple_w, ple_gate_w=ple_gate_w, ple_gate_b=ple_gate_b)
    mom_m = dict(lru_w_in=m_lru_w_in, lru_conv_w=m_lru_conv_w, lru_conv_b=m_lru_conv_b, lru_wa=m_lru_wa, lru_ba=m_lru_ba, lru_wx=m_lru_wx, lru_bx=m_lru_bx, lru_lambda=m_lru_lambda, lru_w_out=m_lru_w_out, pool_w_in=m_pool_w_in, pool_w_grp=m_pool_w_grp, pool_b_grp=m_pool_b_grp, pool_scale=m_pool_scale, pool_w_out=m_pool_w_out, ln_mix_g=m_ln_mix_g, ln_mix_b=m_ln_mix_b, mlp_w1=m_mlp_w1, mlp_w2=m_mlp_w2, ln_mlp_g=m_ln_mlp_g, ln_mlp_b=m_ln_mlp_b, ple_w=m_ple_w, ple_gate_w=m_ple_gate_w, ple_gate_b=m_ple_gate_b)
    mom_v = dict(lru_w_in=v_lru_w_in, lru_conv_w=v_lru_conv_w, lru_conv_b=v_lru_conv_b, lru_wa=v_lru_wa, lru_ba=v_lru_ba, lru_wx=v_lru_wx, lru_bx=v_lru_bx, lru_lambda=v_lru_lambda, lru_w_out=v_lru_w_out, pool_w_in=v_pool_w_in, pool_w_grp=v_pool_w_grp, pool_b_grp=v_pool_b_grp, pool_scale=v_pool_scale, pool_w_out=v_pool_w_out, ln_mix_g=v_ln_mix_g, ln_mix_b=v_ln_mix_b, mlp_w1=v_mlp_w1, mlp_w2=v_mlp_w2, ln_mlp_g=v_ln_mlp_g, ln_mlp_b=v_ln_mlp_b, ple_w=v_ple_w, ple_gate_w=v_ple_gate_w, ple_gate_b=v_ple_gate_b)
    names = list(weights)

    depth, d = ln_mix_g.shape
    t = x.shape[1]
    n_a, n_b = lru_w_in.shape[0], pool_w_in.shape[0]
    d_rnn = lru_w_out.shape[1] * N_CHIPS
    heads = d_rnn // LRU_BW
    d_ff = mlp_w1.shape[2] * N_CHIPS
    ple_dim = ple_w.shape[1]
    n_grp = len(POOL_WINDOWS)
    gw = d // n_grp
    alpha = (2 * depth) ** 0.25
    chip = 2 * lax.axis_index("x") + lax.axis_index("y")
    place = jnp.stack([chip, lax.axis_index("c")]).astype(jnp.int32)

    x2d = x.reshape(t, d)
    target = loss_target.reshape(t, d)
    p3 = p.reshape(depth, t, ple_dim)

    big = ["lru_w_in", "lru_w_out", "pool_w_in", "pool_w_out", "mlp_w1", "mlp_w2", "ple_w", "ple_gate_w", "pool_w_grp"]
    flat2 = lambda a: a.reshape(-1, a.shape[-1])
    small_sharded = ["lru_conv_w", "pool_b_grp", "pool_scale"]
    small_blob = _pack([weights[k] for k in small_sharded], 16, cols=256)
    every_layer = ("mlp_w1", "mlp_w2", "ple_w", "ple_gate_w")

    def layer_keys(i):
        return (["lru_w_in", "lru_w_out"] if i % 2 == 0 else ["pool_w_in", "pool_w_out", "pool_w_grp"]) + list(every_layer)

    def stage(k, i):
        w = weights[k]
        return into_block(f"stage_l{i}_{k}", flat2(w), i if k in every_layer else i // 2, math.prod(w.shape[1:-1]),
                          place[:1], BF16)

    staged = [[stage(k, i) for k in layer_keys(i)] for i in range(depth)]
    first = all_gather_chips("gather_l0", staged[0][:1] + [into_block("stage_small", small_blob, 0, small_blob.shape[0], place[:1], F32)])
    wg = {(layer_keys(0)[0], 0): first[0]}

    tokens = []

    def take_tokens():
        deps = tuple(tokens)
        tokens.clear()
        return deps

    def mm(*args, **kwargs):
        return matmul(*args, deps=take_tokens(), **kwargs)

    def start_gather(tag, bufs, dep):
        plan = gather_plan(len(bufs))
        flight = (plan,) + split_start(f"gather_{tag}_start", plan, 3 * len(bufs), bufs, dep)
        tokens.append(flight[-1])
        return flight

    def land_gather(tag, flight, keys, layer, after):
        plan, send_sems, recv_sems, bufs, _ = flight
        landed = split_wait(f"gather_{tag}_wait", plan, send_sems, recv_sems, bufs, after)
        wg.update(zip([(k, layer) for k in keys], pair_forward(f"gather_{tag}_forward", landed)))

    conv_w_sh, b_grp_sh, scale_sh = _unpack_rows(first[-1].reshape(N_CHIPS, -1), [weights[k].shape for k in small_sharded])
    conv_w_full = jnp.moveaxis(conv_w_sh, 0, 2).reshape(n_a, CONV_WIDTH, d_rnn)
    b_grp_full = jnp.moveaxis(b_grp_sh, 0, 1).reshape(n_b, 1, d)
    scale_full = jnp.moveaxis(scale_sh, 0, 1).reshape(n_b, 1, d)
    rows_grp = gw // N_CHIPS
    w_grp_full = lambda i: jnp.moveaxis(wg["pool_w_grp", i].reshape(N_CHIPS, n_grp, rows_grp, gw), 0, 1).reshape(n_grp, gw, gw)
    wa_bf, wx_bf = lru_wa.astype(BF16), lru_wx.astype(BF16)
    row = lambda a, i: a[i].reshape(1, -1)

    def ln_after(acc, x_in, g, b):
        s = alpha * x_in + acc
        y = _ln_stats(s)[0] * g + b
        return y, y, s

    ln_outs = [plain(shape=(t, d), dtype=F32), plain(shape=(t, d), dtype=BF16), plain(shape=(t, d), dtype=F32)]
    saved = []
    cur, cur_bf = x2d, x2d
    for i in range(depth):
        slot = i // 2
        sv = dict(x_bf=cur_bf)
        if i == 0:
            flight = start_gather("l0_rest", staged[0][1:], first[0])
        elif i + 1 < depth:
            flight = start_gather(f"l{i + 1}", staged[i + 1], cur)
        if i % 2 == 0:
            (proj,) = mm(f"l{i}_lru_in", plain(cur_bf), colsplit(wg["lru_w_in", i], 0, d), "nn",
                         [colsplit(None, 0, t, n=2, full=(2, t, d_rnn), dtype=F32)])
            hg = lru_fwd(f"l{i}_lru", proj, conv_w_full[slot], row(lru_conv_b, slot), wa_bf[slot], row(lru_ba, slot),
                         wx_bf[slot], row(lru_bx, slot), row(lru_lambda, slot))
            if i == 0:
                land_gather("l0_rest", flight, layer_keys(0)[1:], 0, hg)
                flight = start_gather("l1", staged[1], hg)
            x1, x1_bf, s1 = mm(f"l{i}_lru_out", plain(hg), rowsplit_whole(wg["lru_w_out", i]), "nn", ln_outs, pk=2048,
                               epilogue=ln_after, tiles=[plain(cur)], rows=[row(ln_mix_g, i), row(ln_mix_b, i)])
            sv.update(proj=proj, act=hg)
        else:
            (u,) = mm(f"l{i}_pool_in", plain(cur_bf), rowsplit_whole(wg["pool_w_in", i]), "nn",
                          [plain(shape=(t, d), dtype=F32)])
            zs = pool_fwd(f"l{i}_pool", u, w_grp_full(i), b_grp_full[slot], scale_full[slot])
            x1, x1_bf, s1 = mm(f"l{i}_pool_out", plain(zs), rowsplit_whole(wg["pool_w_out", i]), "nn", ln_outs,
                               epilogue=ln_after, tiles=[plain(cur)], rows=[row(ln_mix_g, i), row(ln_mix_b, i)])
            sv.update(u=u, act=zs)

        def relu2(acc):
            hr = jnp.maximum(acc, 0.0)
            return hr, hr * hr

        hr, hh = mm(f"l{i}_mlp_up", plain(x1_bf), colsplit(wg["mlp_w1", i], 0, d), "nn",
                    [plain(shape=(t, d_ff), dtype=BF16), plain(shape=(t, d_ff), dtype=BF16)], epilogue=relu2, pm=2048)
        (mlp,) = mm(f"l{i}_mlp_down", plain(hh), rowsplit_whole(wg["mlp_w2", i]), "nn",
                    [plain(shape=(t, d), dtype=F32)], pk=d_ff)
        x2, x2_bf, s2 = ln_fwd(f"l{i}_ln_mlp", alpha, x1, mlp, row(ln_mlp_g, i), row(ln_mlp_b, i))
        (e,) = mm(f"l{i}_ple", plain(p3[i]), colsplit(wg["ple_w", i], 0, ple_dim), "nn", [plain(shape=(t, d), dtype=F32)])

        def ple_out(acc, x2_t, e_t, gb):
            gate = jax.nn.sigmoid(acc + gb)
            x3 = x2_t + e_t * gate
            return x3, x3, gate

        cur, cur_bf, gate = mm(f"l{i}_ple_gate", plain(x2_bf), rowsplit_whole(wg["ple_gate_w", i]), "nn",
                                   [plain(shape=(t, d), dtype=F32), plain(shape=(t, d), dtype=BF16), plain(shape=(t, d), dtype=F32)],
                                   epilogue=ple_out, tiles=[plain(x2), plain(e)], rows=[row(ple_gate_b, i)])
        sv.update(s1=s1, x1_bf=x1_bf, hr=hr, hh=hh, s2=s2, x2_bf=x2_bf, gate=gate, e=e)
        saved.append(sv)
        if i + 1 < depth:
            land_gather(f"l{i + 1}", flight, layer_keys(i + 1), i + 1, cur)

    dy, loss_part = loss_head("loss", cur, target)
    loss = lax.psum(loss_part.reshape(()), ("x", "y", "c"))

    part = {}
    sums = {}

    def grad_view(key, split):
        w = weights[key]
        return split(None, 0, w.shape[1], full=(N_CHIPS, w.shape[1], w.shape[2]), dtype=BF16)

    def group_start(tag, items, dep):
        srcs = [part[it] for it in items]
        plan = pair_plan(len(srcs))
        lands = [lax.empty(s.shape, s.dtype) for s in pair_lands(srcs)]
        flight = (tag, items, plan) + split_start(f"grads_{tag}_pair_start", plan, len(srcs), srcs + lands, dep)
        tokens.append(flight[-1])
        return flight

    def group_mid(flight, after):
        tag, items, plan, send_sems, recv_sems, bufs, _ = flight
        bufs = split_wait(f"grads_{tag}_pair_wait", plan, send_sems, recv_sems, bufs, after)
        n = len(items)
        parts = [pair_sum(f"grads_{tag}_pair_sum_{j}", bufs[j], bufs[n + j], place[1:], F32 if it[0] == "blob" else BF16)
                 for j, it in enumerate(items)]
        plan = chip_plan(n)
        flight = (tag, items, plan) + split_start(f"grads_{tag}_chip_start", plan, 3 * n,
                                                  parts + [lax.empty(q.shape, q.dtype) for q in parts], after)
        tokens.append(flight[-1])
        return flight

    def group_end(flight, after):
        tag, items, plan, send_sems, recv_sems, bufs, _ = flight
        bufs = split_wait(f"grads_{tag}_chip_wait", plan, send_sems, recv_sems, bufs, after)
        n = len(items)
        for j, (k, layer) in enumerate(items):
            if k == "blob":
                sums[k] = chip_sum(f"grads_{tag}_chip_sum_{j}", bufs[n + j], bufs[j], place, True)
            else:
                sums[k] = chip_sum(f"grads_{tag}_chip_sum_{j}", bufs[n + j], bufs[j], place, False, into=sums.get(k),
                                   layer=layer if k in every_layer else layer // 2, n_layers=weights[k].shape[0])

    big_w = [k for k in big if k != "pool_w_grp"]
    small_keys = [k for k in names if k not in big_w]

    def ln_before(ca):
        def back(acc, upstream, s, g):
            dx = ca * upstream + acc
            xhat, rstd = _ln_stats(s)
            dxh = dx * g
            ds = rstd * (dxh - jnp.mean(dxh, axis=-1, keepdims=True) - xhat * jnp.mean(dxh * xhat, axis=-1, keepdims=True))
            return ds, ds, jnp.sum(dx * xhat, axis=0, keepdims=True), jnp.sum(dx, axis=0, keepdims=True)

        return back

    ds_outs = [plain(shape=(t, d), dtype=F32), plain(shape=(t, d), dtype=BF16)]
    small = {k: [None] * weights[k].shape[0] for k in names if k not in big or k == "pool_w_grp"}
    dcur = dy
    mlp_pair = mlp_chip = mix_pair = mix_chip = None
    for i in reversed(range(depth)):
        slot = i // 2
        sv = saved[i]
        de, dpre, dgb = ple_bwd(f"l{i}_ple_bwd", dcur, sv["gate"], sv["e"])
        small["ple_gate_b"][i] = dgb
        (part["ple_w", i],) = mm(f"l{i}_d_ple_w", plain(p3[i]), plain(de), "tn", [grad_view("ple_w", colsplit)])
        (part["ple_gate_w", i],) = mm(f"l{i}_d_ple_gate_w", plain(sv["x2_bf"]), plain(dpre), "tn",
                                          [grad_view("ple_gate_w", rowsplit)])
        ds2, ds2_bf, dg, db = mm(f"l{i}_dx2", plain(dpre), rowsplit_whole(wg["ple_gate_w", i]), "nt", ds_outs, col_sums=2, pm=512,
                                 epilogue=ln_before(1.0), tiles=[plain(dcur), plain(sv["s2"])], rows=[row(ln_mlp_g, i)])
        small["ln_mlp_g"][i], small["ln_mlp_b"][i] = dg, db
        (part["mlp_w2", i],) = mm(f"l{i}_d_mlp_w2", plain(sv["hh"]), plain(ds2_bf), "tn", [grad_view("mlp_w2", rowsplit)])
        (dhpre,) = mm(f"l{i}_dh", plain(ds2_bf), rowsplit(wg["mlp_w2", i], 0, d_ff // N_CHIPS), "nt",
                      [plain(shape=(t, d_ff), dtype=BF16)], tiles=[plain(sv["hr"])], pm=2048,
                      epilogue=lambda acc, hr_t: (acc * (2.0 * hr_t.astype(F32)),))
        (part["mlp_w1", i],) = mm(f"l{i}_d_mlp_w1", plain(sv["x1_bf"]), plain(dhpre), "tn", [grad_view("mlp_w1", colsplit)])
        if mlp_chip is not None:
            group_end(mlp_chip, dhpre)
        if mix_pair is not None:
            mix_chip = group_mid(mix_pair, dhpre)
        mlp_pair = group_start(f"l{i}_mlp", [(k, i) for k in every_layer], part["mlp_w1", i])
        (dx1b,) = mm(f"l{i}_dx1", plain(dhpre), colsplit(wg["mlp_w1", i], 0, d), "nt", [plain(shape=(t, d), dtype=F32)],
                     pm=2048)
        ds1, ds1_bf, dg, db = ln_bwd(f"l{i}_ln_mix_bwd", alpha, ds2, dx1b, sv["s1"], row(ln_mix_g, i))
        small["ln_mix_g"][i], small["ln_mix_b"][i] = dg, db
        residual = lambda acc, ds_t: (alpha * ds_t + acc,)
        if i % 2 == 0:
            (part["lru_w_out", i],) = mm(f"l{i}_d_lru_out", plain(sv["act"]), plain(ds1_bf), "tn",
                                             [grad_view("lru_w_out", rowsplit)])
            (dhg,) = mm(f"l{i}_dhg", plain(ds1_bf), rowsplit_whole(wg["lru_w_out", i]), "nt",
                            [plain(shape=(t, d_rnn), dtype=F32)], pn=2048)
            mlp_chip = group_mid(mlp_pair, dhg)
            dproj, dcw, dcb, dba, dbx, dlam, dwa, dwx = lru_bwd(
                f"l{i}_lru_bwd", sv["proj"], dhg, conv_w_full[slot], row(lru_conv_b, slot), wa_bf[slot], row(lru_ba, slot),
                wx_bf[slot], row(lru_bx, slot), row(lru_lambda, slot), deps=take_tokens())
            for key, val in (("lru_conv_w", dcw), ("lru_conv_b", dcb), ("lru_ba", dba), ("lru_bx", dbx),
                             ("lru_lambda", dlam), ("lru_wa", dwa), ("lru_wx", dwx)):
                small[key][slot] = val
            dproj_v = colsplit(dproj, 0, t, n=2)
            (part["lru_w_in", i],) = mm(f"l{i}_d_lru_in", plain(sv["x_bf"]), dproj_v, "tn", [grad_view("lru_w_in", colsplit)])
            (dcur,) = mm(f"l{i}_dx", dproj_v, colsplit(wg["lru_w_in", i], 0, d), "nt",
                             [plain(shape=(t, d), dtype=F32)], epilogue=residual, tiles=[plain(ds1)])
        else:
            (part["pool_w_out", i],) = mm(f"l{i}_d_pool_out", plain(sv["act"]), plain(ds1_bf), "tn",
                                              [grad_view("pool_w_out", rowsplit)])
            (dzs,) = mm(f"l{i}_dzs", plain(ds1_bf), rowsplit_whole(wg["pool_w_out", i]), "nt",
                            [plain(shape=(t, d), dtype=F32)])
            mlp_chip = group_mid(mlp_pair, dzs)
            du, dwg, dbg, dsc = pool_bwd(f"l{i}_pool_bwd", sv["u"], dzs, w_grp_full(i), b_grp_full[slot], scale_full[slot],
                                         deps=take_tokens())
            small["pool_w_grp"][slot], small["pool_b_grp"][slot], small["pool_scale"][slot] = dwg, dbg, dsc
            (part["pool_w_in", i],) = mm(f"l{i}_d_pool_in", plain(sv["x_bf"]), plain(du), "tn", [grad_view("pool_w_in", rowsplit)])
            (dcur,) = mm(f"l{i}_dx", plain(du), rowsplit_whole(wg["pool_w_in", i]), "nt",
                             [plain(shape=(t, d), dtype=F32)], epilogue=residual, tiles=[plain(ds1)])
        if mix_chip is not None:
            group_end(mix_chip, dcur)
        mixer = [(k, i) for k in layer_keys(i) if k not in every_layer and k != "pool_w_grp"]
        if i == 0:
            small_full = [jnp.stack(small[k]).reshape((weights[k].shape[0],) + tuple(
                s * (N_CHIPS if ax in _sharded_axis(k) else 1) for ax, s in enumerate(weights[k].shape[1:], 1))) for k in small_keys]
            blob = _pack(small_full, 64)
            part["blob", 0] = blob.reshape(N_CHIPS, blob.shape[0] // N_CHIPS, BLOB_COLS)
            mixer.append(("blob", 0))
        mix_pair = group_start(f"l{i}_mix", mixer, dcur)
    grad_x = dcur.reshape(x.shape)
    mix_chip = group_mid(mix_pair, dcur)
    group_end(mlp_chip, mix_chip[-1])
    group_end(mix_chip, sums["mlp_w1"])
    order = big_w + ["blob"]
    reduced = dict(zip(order, pair_gather("grads_pair_gather", [sums[k] for k in order], [k == "blob" for k in order],
                                          [1 if k == "blob" else weights[k].shape[0] for k in order])))
    (blob_all,) = all_gather_chips("gather_small_grads", [reduced["blob"]])
    small_grads = dict(zip(small_keys, _unpack(blob_all.reshape(blob.shape), [a.shape for a in small_full])))
    for k in small_keys:
        for ax in _sharded_axis(k):
            n = weights[k].shape[ax]
            small_grads[k] = lax.dynamic_slice_in_dim(small_grads[k], chip * n, n, axis=ax)
    grads = {k: reduced[k].reshape(weights[k].shape) for k in big_w}
    grads.update(small_grads)

    delta, new_m, new_v = {}, {}, {}
    for k in big_w:
        dl, nm, nv = adamw("adamw_" + k, flat2(weights[k]), flat2(grads[k]), flat2(mom_m[k]), flat2(mom_v[k]))
        delta[k], new_m[k], new_v[k] = (a.reshape(weights[k].shape) for a in (dl, nm, nv))
    dl, nm, nv = adamw_small("adamw_small", *[[flat2(src[k]) for k in small_keys] for src in (weights, grads, mom_m, mom_v)])
    for out, res in ((delta, dl), (new_m, nm), (new_v, nv)):
        out.update({k: a.reshape(weights[k].shape) for k, a in zip(small_keys, res)})

    return (loss, grad_x, *[grads[k] for k in names], *[delta[k] for k in names],
            *[new_m[k] for k in names], *[new_v[k] for k in names])


def _sharded_axis(key):
    return {"lru_conv_w": (2,), "pool_w_grp": (2,), "pool_b_grp": (1,), "pool_scale": (1,)}.get(key, ())
```

```python
import functools
import math

import jax
import jax.numpy as jnp
from jax import lax
from jax.experimental import pallas as pl
from jax.experimental.pallas import tpu as pltpu

F32 = jnp.float32
BF16 = jnp.bfloat16

N_CHIPS = 4
LRU_BW = 128
LRU_C = 8.0
CONV_WIDTH = 4
POOL_WINDOWS = (2, 4, 8, 16)
POOL_HALO = 16
CONV_HALO = 8
LN_EPS = 1e-5
ADAM_LR = 0.001
ADAM_B1 = 0.9
ADAM_B2 = 0.999
ADAM_EPS = 1e-08
ADAM_WD = 0.01
ADAM_STEP = 10
GELU_C = math.sqrt(2.0 / math.pi)
GELU_K = 0.044715
VMEM_LIMIT_BYTES = 56 * 1024 * 1024
MATMUL_TILE_BYTES = 44 * 1024 * 1024
MESH = pl.DeviceIdType.MESH
BLOB_COLS = 1024


def _params(*sem):
    return pltpu.CompilerParams(dimension_semantics=tuple(sem), vmem_limit_bytes=VMEM_LIMIT_BYTES)


def _tile(unit, pref, align=128):
    if unit <= pref:
        return unit
    for d in range(2, unit + 1):
        if unit % d == 0 and unit // d <= pref and (unit // d) % align == 0:
            return unit // d
    raise ValueError((unit, pref, align))


class View:
    def __init__(self, arr, shape, row_unit, col_unit, block_fn, full=None, dtype=None):
        self.arr, self.shape, self.row_unit, self.col_unit, self.block_fn = arr, shape, row_unit, col_unit, block_fn
        self.full = full if full is not None else arr.shape
        self.dtype = dtype if dtype is not None else arr.dtype

    def spec(self, tr, tc, f):
        block, idx = self.block_fn(tr, tc)
        return pl.BlockSpec(block, lambda *g: idx(*f(*g)))


def plain(arr=None, shape=None, dtype=None):
    shape = arr.shape if arr is not None else shape
    return View(arr, shape, shape[0], shape[1], lambda tr, tc: ((tr, tc), lambda rt, ct: (rt, ct)), full=shape, dtype=dtype)


def colsplit(arr, layer, rows, n=N_CHIPS, full=None, dtype=None):
    full = arr.shape if arr is not None else full
    c = full[2]

    def block_fn(tr, tc):
        assert rows % tr == 0 and c % tc == 0, (rows, tr, c, tc)
        per, rpl = c // tc, rows // tr
        return (None, tr, tc), lambda rt, ct: (ct // per, layer * rpl + rt, ct % per)

    return View(arr, (rows, n * c), rows, c, block_fn, full=full, dtype=dtype)


def rowsplit(arr, layer, rows, n=N_CHIPS, full=None, dtype=None):
    full = arr.shape if arr is not None else full
    c = full[2]

    def block_fn(tr, tc):
        assert rows % tr == 0 and c % tc == 0, (rows, tr, c, tc)
        per = rows // tr
        return (None, tr, tc), lambda rt, ct: (rt // per, layer * per + rt % per, ct)

    return View(arr, (n * rows, c), rows, c, block_fn, full=full, dtype=dtype)


def rowsplit_whole(arr):
    n, rows, c = arr.shape

    def block_fn(tr, tc):
        assert tr == n * rows and c % tc == 0, (tr, n, rows, c, tc)
        return (n, rows, tc), lambda rt, ct: (0, 0, ct)

    return View(arr, (n * rows, c), n * rows, c, block_fn)


def matmul(name, a, b, mode, outs, epilogue=None, tiles=(), rows=(), side=(), deps=(), col_sums=0, pm=1024, pn=1024, pk=1024):
    if mode == "nn":
        (m, k), (k2, n) = a.shape, b.shape
        um, uk, un = a.row_unit, min(a.col_unit, b.row_unit), b.col_unit
        dims = (((1,), (0,)), ((), ()))
    elif mode == "nt":
        (m, k), (n, k2) = a.shape, b.shape
        um, uk, un = a.row_unit, min(a.col_unit, b.col_unit), b.row_unit
        dims = (((1,), (1,)), ((), ()))
    else:
        (k, m), (k2, n) = a.shape, b.shape
        um, uk, un = a.col_unit, min(a.row_unit, b.row_unit), b.col_unit
        dims = (((0,), (0,)), ((), ()))
    assert k == k2, (name, a.shape, b.shape)
    for o in list(outs) + list(tiles):
        assert o.shape == (m, n), (name, o.shape, m, n)
        um, un = min(um, o.row_unit), min(un, o.col_unit)
    tm, tn, tk = _tile(um, pm), _tile(un, pn), _tile(uk, pk)
    if mode == "tn" and uk == k:
        size = lambda v: jnp.dtype(v.dtype).itemsize
        need = 2 * k * (tm * size(a) + tn * size(b)) + 2 * tm * tn * sum(size(o) for o in outs)
        if need <= MATMUL_TILE_BYTES:
            tk = k
    assert m % tm == 0 and n % tn == 0 and k % tk == 0, (name, m, n, k, tm, tn, tk)
    gm, gn, gk = m // tm, n // tn, k // tk
    assert not col_sums or gn == 1, (name, gn)

    if mode == "nn":
        a_spec = a.spec(tm, tk, lambda i, j, kk: (i, kk))
        b_spec = b.spec(tk, tn, lambda i, j, kk: (kk, j))
    elif mode == "nt":
        a_spec = a.spec(tm, tk, lambda i, j, kk: (i, kk))
        b_spec = b.spec(tn, tk, lambda i, j, kk: (j, kk))
    else:
        a_spec = a.spec(tk, tm, lambda i, j, kk: (kk, i))
        b_spec = b.spec(tk, tn, lambda i, j, kk: (kk, j))
    tile_specs = [t.spec(tm, tn, lambda i, j, kk: (i, j)) for t in tiles]
    row_specs = [pl.BlockSpec((1, tn), lambda i, j, kk: (0, j)) for _ in rows] + [spec(tm, tn) for _, spec in side]
    rows = list(rows) + [arr for arr, _ in side]
    in_place = [o for o in outs if o.arr is not None]
    alias_specs = [pl.BlockSpec(memory_space=pl.ANY) for _ in in_place]
    out_specs = [o.spec(tm, tn, lambda i, j, kk: (i, j)) for o in outs]
    n_in = 2 + len(tiles) + len(rows)
    aliases = {}
    for o_idx, o in enumerate(outs):
        if o.arr is not None:
            aliases[n_in + in_place.index(o)] = o_idx
    n_t, n_r, n_a, n_o = len(tiles), len(rows), len(in_place) + len(deps), len(outs)
    dep_specs = [pl.BlockSpec(memory_space=pl.ANY) for _ in deps]

    def body(*refs):
        a_ref, b_ref = refs[0], refs[1]
        tile_refs = refs[2:2 + n_t]
        row_refs = refs[2 + n_t:2 + n_t + n_r]
        out_refs = refs[2 + n_t + n_r + n_a:2 + n_t + n_r + n_a + n_o]
        sum_refs = refs[2 + n_t + n_r + n_a + n_o:2 + n_t + n_r + n_a + n_o + col_sums]
        acc_ref = refs[-1] if gk > 1 else None

        def finish(acc):
            extra = [t[...] for t in tile_refs] + [r[...] for r in row_refs]
            res = epilogue(acc, *extra) if epilogue is not None else (acc,)
            for o_ref, r in zip(out_refs, res):
                o_ref[...] = r.astype(o_ref.dtype)
            for s_ref, r in zip(sum_refs, res[n_o:]):
                _accumulate(pl.program_id(0), s_ref, r)

        b_tile = b_ref[...]
        b_tile = b_tile.reshape(-1, b_tile.shape[-1])
        prod = lax.dot_general(a_ref[...].astype(BF16), b_tile.astype(BF16), dims, preferred_element_type=F32)
        if gk == 1:
            finish(prod)
        else:
            kk = pl.program_id(2)

            @pl.when(kk == 0)
            def _():
                acc_ref[...] = prod

            @pl.when(kk > 0)
            def _():
                acc_ref[...] += prod

            @pl.when(kk == gk - 1)
            def _():
                finish(acc_ref[...])

    res = pl.pallas_call(
        body,
        name=name,
        grid=(gm, gn, gk),
        in_specs=[a_spec, b_spec] + tile_specs + row_specs + alias_specs + dep_specs,
        out_specs=out_specs + [pl.BlockSpec((1, tn), lambda i, j, kk: (0, 0))] * col_sums,
        out_shape=[jax.ShapeDtypeStruct(o.full, o.dtype) for o in outs] + [jax.ShapeDtypeStruct((1, n), F32)] * col_sums,
        scratch_shapes=[pltpu.VMEM((tm, tn), F32)] if gk > 1 else [],
        input_output_aliases=aliases,
        compiler_params=_params(*(["arbitrary"] * 3 if col_sums else ["parallel", "parallel", "arbitrary"])),
    )(a.arr, b.arr, *[t.arr for t in tiles], *rows, *[o.arr for o in in_place], *deps)
    return res


def rows_call(name, fn, tiled, vecs, tiled_out, acc_out, tr=512):
    t = tiled[0].shape[0]
    tr = min(tr, t)
    assert t % tr == 0
    n1, n2, n3 = len(tiled), len(vecs), len(tiled_out)

    def body(*refs):
        fn(pl.program_id(0), refs[:n1], refs[n1:n1 + n2], refs[n1 + n2:n1 + n2 + n3], refs[n1 + n2 + n3:])

    return pl.pallas_call(
        body,
        name=name,
        grid=(t // tr,),
        in_specs=[pl.BlockSpec((tr, x.shape[1]), lambda i: (i, 0)) for x in tiled]
        + [pl.BlockSpec(v.shape, lambda i: (0, 0)) for v in vecs],
        out_specs=[pl.BlockSpec((tr, c), lambda i: (i, 0)) for c, _ in tiled_out]
        + [pl.BlockSpec(s, lambda i: (0, 0)) for s, _ in acc_out],
        out_shape=[jax.ShapeDtypeStruct((t, c), d) for c, d in tiled_out] + [jax.ShapeDtypeStruct(s, d) for s, d in acc_out],
        compiler_params=_params("arbitrary" if acc_out else "parallel"),
    )(*tiled, *vecs)


def _accumulate(step, ref, val):
    @pl.when(step == 0)
    def _():
        ref[...] = val

    @pl.when(step > 0)
    def _():
        ref[...] += val


def _ln_stats(s):
    mu = jnp.mean(s, axis=-1, keepdims=True)
    d = s - mu
    var = jnp.mean(d * d, axis=-1, keepdims=True)
    rstd = lax.rsqrt(var + LN_EPS)
    return d * rstd, rstd


def ln_fwd(name, alpha, x_in, m, g, b):
    d = x_in.shape[1]

    def fn(step, tiled, vecs, outs, accs):
        s = alpha * tiled[0][...] + tiled[1][...]
        xhat, _ = _ln_stats(s)
        y = xhat * vecs[0][...] + vecs[1][...]
        outs[0][...] = y
        outs[1][...] = y.astype(BF16)
        outs[2][...] = s

    return rows_call(name, fn, [x_in, m], [g, b], [(d, F32), (d, BF16), (d, F32)], [])


def ln_bwd(name, ca, da, db, s, g):
    d = s.shape[1]

    def fn(step, tiled, vecs, outs, accs):
        dx = ca * tiled[0][...] + tiled[1][...]
        xhat, rstd = _ln_stats(tiled[2][...])
        dxh = dx * vecs[0][...]
        ds = rstd * (dxh - jnp.mean(dxh, axis=-1, keepdims=True) - xhat * jnp.mean(dxh * xhat, axis=-1, keepdims=True))
        outs[0][...] = ds
        outs[1][...] = ds.astype(BF16)
        _accumulate(step, accs[0], jnp.sum(dx * xhat, axis=0, keepdims=True))
        _accumulate(step, accs[1], jnp.sum(dx, axis=0, keepdims=True))

    return rows_call(name, fn, [da, db, s], [g], [(d, F32), (d, BF16)], [((1, d), F32), ((1, d), F32)])


def ple_bwd(name, dx3, gate, e):
    d = dx3.shape[1]

    def fn(step, tiled, vecs, outs, accs):
        dx, gt, ev = tiled[0][...], tiled[1][...], tiled[2][...]
        dpre = dx * ev * gt * (1.0 - gt)
        outs[0][...] = (dx * gt).astype(BF16)
        outs[1][...] = dpre.astype(BF16)
        _accumulate(step, accs[0], jnp.sum(dpre, axis=0, keepdims=True))

    return rows_call(name, fn, [dx3, gate, e], [], [(d, BF16), (d, BF16)], [((1, d), F32)])


def loss_head(name, y, target):
    t, d = y.shape

    def fn(step, tiled, vecs, outs, accs):
        err = tiled[0][...] - tiled[1][...]
        outs[0][...] = err * (1.0 / d)
        part = jnp.sum(jnp.sum(err * err, axis=1, keepdims=True), axis=0, keepdims=True) * (0.5 / d)
        _accumulate(step, accs[0], part)

    return rows_call(name, fn, [y, target], [], [(d, F32)], [((1, 1), F32)])


def _softplus(z):
    return jnp.maximum(z, 0.0) + jnp.log1p(jnp.exp(-jnp.abs(z)))


def _gelu(y):
    th = jnp.tanh(GELU_C * (y + GELU_K * (y * y * y)))
    cdf = 0.5 * (1.0 + th)
    return y * cdf, cdf + 0.5 * y * (1.0 - th * th) * (GELU_C * (1.0 + 3.0 * GELU_K * y * y))


def _up(win, k):
    return pltpu.roll(win, win.shape[0] - k, 0)


def _down(win, k):
    return pltpu.roll(win, k, 0)


def _lru_gates(win, row0, cw_ref, cb, wa, ba, wx, bx, sp):
    h = CONV_HALO
    u = (cb + cw_ref[3:4, :] * win[h:] + cw_ref[2:3, :] * _down(win, 1)[h:]
         + cw_ref[1:2, :] * _down(win, 2)[h:] + cw_ref[0:1, :] * _down(win, 3)[h:])
    ub = u.astype(BF16)
    r = jax.nn.sigmoid(jnp.dot(ub, wa, preferred_element_type=F32) + ba)
    ig = jax.nn.sigmoid(jnp.dot(ub, wx, preferred_element_type=F32) + bx)
    log_a = (-LRU_C) * r * sp
    a = jnp.exp(log_a)
    mult = jnp.sqrt(-jnp.tanh(log_a) * (a * a + 1.0))
    first = (row0 + lax.broadcasted_iota(jnp.int32, u.shape, 0)) == 0
    mult = jnp.where(first, 1.0, mult)
    return u, r, ig, a, mult, first


def _block_scan(a, b, reverse):
    n = a.shape[0]
    a, b = a.reshape(n // 8, 8, LRU_BW), b.reshape(n // 8, 8, LRU_BW)
    pos = lax.broadcasted_iota(jnp.int32, a.shape, 1)
    for s in (1, 2, 4):
        keep = (pos >= 8 - s) if reverse else (pos < s)
        by = 8 - s if reverse else s
        b = jnp.where(keep, b, a * pltpu.roll(b, by, 1) + b)
        a = jnp.where(keep, a, a * pltpu.roll(a, by, 1))
    return a.reshape(n, LRU_BW), b.reshape(n, LRU_BW)


def _carry_scan(a_ref, b_ref, out_ref, out_off, t, reverse):
    groups, per_step = t // 8, 8

    def step(j, h):
        for k in range(per_step):
            g = j * per_step + k
            r0 = pl.multiple_of((groups - 1 - g if reverse else g) * 8, 8)
            edge = r0 if reverse else r0 + 7
            h_out = a_ref[pl.ds(edge, 1), :] * h + b_ref[pl.ds(edge, 1), :]
            out_ref[pl.ds(pl.multiple_of(out_off + r0, 8), 8), :] = a_ref[pl.ds(r0, 8), :] * h + b_ref[pl.ds(r0, 8), :]
            h = h_out
        return h

    lax.fori_loop(0, groups // per_step, step, jnp.zeros((1, LRU_BW), F32))


def _lru_in_specs(t, heads):
    blk = lambda i: (0, i)
    return [
        pl.BlockSpec((2, t, LRU_BW), lambda i: (0, 0, i)),
        pl.BlockSpec((CONV_WIDTH, LRU_BW), blk),
        pl.BlockSpec((1, LRU_BW), blk),
        pl.BlockSpec((None, LRU_BW, LRU_BW), lambda i: (i, 0, 0)),
        pl.BlockSpec((1, LRU_BW), blk),
        pl.BlockSpec((None, LRU_BW, LRU_BW), lambda i: (i, 0, 0)),
        pl.BlockSpec((1, LRU_BW), blk),
        pl.BlockSpec((1, LRU_BW), blk),
    ]


def lru_fwd(name, proj, conv_w, conv_b, wa, ba, wx, bx, lam):
    _, t, c = proj.shape
    heads = c // LRU_BW
    rc = min(256, t)

    def body(proj_ref, cw_ref, cb_ref, wa_ref, ba_ref, wx_ref, bx_ref, lam_ref, out_ref, upad, a_s, b_s):
        upad[0:CONV_HALO, :] = jnp.zeros((CONV_HALO, LRU_BW), F32)
        upad[CONV_HALO:, :] = proj_ref[0]
        sp = _softplus(-lam_ref[...])
        cb, ba, bx, wa, wx = cb_ref[...], ba_ref[...], bx_ref[...], wa_ref[...], wx_ref[...]

        def gates(i, carry):
            r0 = pl.multiple_of(i * rc, rc)
            win = upad[pl.ds(r0, rc + CONV_HALO), :]
            u, r, ig, a, mult, _ = _lru_gates(win, r0, cw_ref, cb, wa, ba, wx, bx, sp)
            rows = pl.ds(r0, rc)
            a_s[rows, :], b_s[rows, :] = _block_scan(a, mult * (ig * u), False)
            return carry

        lax.fori_loop(0, t // rc, gates, 0)
        _carry_scan(a_s, b_s, b_s, 0, t, False)

        def gate_out(i, carry):
            r0 = pl.multiple_of(i * rc, rc)
            gy, _ = _gelu(proj_ref[1, pl.ds(r0, rc), :])
            out_ref[pl.ds(r0, rc), :] = (b_s[pl.ds(r0, rc), :] * gy).astype(BF16)
            return carry

        lax.fori_loop(0, t // rc, gate_out, 0)

    return pl.pallas_call(
        body,
        name=name,
        grid=(heads,),
        in_specs=_lru_in_specs(t, heads),
        out_specs=pl.BlockSpec((t, LRU_BW), lambda i: (0, i)),
        out_shape=jax.ShapeDtypeStruct((t, c), BF16),
        scratch_shapes=[pltpu.VMEM((t + CONV_HALO, LRU_BW), F32)] + [pltpu.VMEM((t, LRU_BW), F32)] * 2,
        compiler_params=_params("parallel"),
    )(proj, conv_w, conv_b, wa, ba, wx, bx, lam)


def lru_bwd(name, proj, dhg, conv_w, conv_b, wa, ba, wx, bx, lam, deps=()):
    _, t, c = proj.shape
    heads = c // LRU_BW
    rc = min(256, t)
    h8 = CONV_HALO

    def body(proj_ref, dhg_ref, cw_ref, cb_ref, wa_ref, ba_ref, wx_ref, bx_ref, lam_ref,
             dproj_ref, dcw_ref, dcb_ref, dba_ref, dbx_ref, dlam_ref, dwa_ref, dwx_ref,
             upad, u_s, r_s, ig_s, apad, hpad, g_s, dupad, sa_s, sb_s):
        zeros8 = jnp.zeros((h8, LRU_BW), F32)
        upad[0:h8, :] = zeros8
        upad[h8:, :] = proj_ref[0]
        hpad[0:h8, :] = zeros8
        apad[t:, :] = zeros8
        dupad[t:, :] = zeros8
        lam = lam_ref[...]
        sp = _softplus(-lam)
        cb, ba, bx, wa, wx = cb_ref[...], ba_ref[...], bx_ref[...], wa_ref[...], wx_ref[...]

        def gates(i, carry):
            r0 = pl.multiple_of(i * rc, rc)
            win = upad[pl.ds(r0, rc + h8), :]
            u, r, ig, a, mult, _ = _lru_gates(win, r0, cw_ref, cb, wa, ba, wx, bx, sp)
            u_s[pl.ds(r0, rc), :] = u
            r_s[pl.ds(r0, rc), :] = r
            ig_s[pl.ds(r0, rc), :] = ig
            rows = pl.ds(r0, rc)
            apad[rows, :] = a
            sa_s[rows, :], sb_s[rows, :] = _block_scan(a, mult * (ig * u), False)
            return carry

        lax.fori_loop(0, t // rc, gates, 0)
        _carry_scan(sa_s, sb_s, hpad, h8, t, False)

        def out_gate(i, carry):
            r0 = pl.multiple_of(i * rc, rc)
            gy, dgy = _gelu(proj_ref[1, pl.ds(r0, rc), :])
            dh = dhg_ref[pl.ds(r0, rc), :]
            hh = hpad[pl.ds(pl.multiple_of(r0 + h8, 8), rc), :]
            dproj_ref[1, pl.ds(r0, rc), :] = (dh * hh * dgy).astype(BF16)
            rows = pl.ds(r0, rc)
            a_next = _up(apad[pl.ds(r0, rc + h8), :], 1)[:rc]
            sa_s[rows, :], sb_s[rows, :] = _block_scan(a_next, dh * gy, True)
            return carry

        lax.fori_loop(0, t // rc, out_gate, 0)
        _carry_scan(sa_s, sb_s, g_s, 0, t, True)

        zrow = jnp.zeros((1, LRU_BW), F32)
        zmat = jnp.zeros((LRU_BW, LRU_BW), F32)

        def grads(i, carry):
            dsp, dba, dbx, dwa, dwx = carry
            r0 = pl.multiple_of(i * rc, rc)
            g = g_s[pl.ds(r0, rc), :]
            u, r, ig, a = u_s[pl.ds(r0, rc), :], r_s[pl.ds(r0, rc), :], ig_s[pl.ds(r0, rc), :], apad[pl.ds(r0, rc), :]
            hprev = _down(hpad[pl.ds(r0, rc + h8), :], 1)[h8:]
            first = (r0 + lax.broadcasted_iota(jnp.int32, u.shape, 0)) == 0
            log_a = (-LRU_C) * r * sp
            mult = jnp.where(first, 1.0, jnp.sqrt(-jnp.tanh(log_a) * (a * a + 1.0)))
            dmult = jnp.where(first, 0.0, g * (ig * u))
            dlog_a = g * hprev * a - dmult * (a * a) / mult
            dr = dlog_a * ((-LRU_C) * sp)
            dpre_r = dr * r * (1.0 - r)
            dpre_i = (g * mult * u) * ig * (1.0 - ig)
            pr, pi, ub = dpre_r.astype(BF16), dpre_i.astype(BF16), u.astype(BF16)
            nt = (((1,), (1,)), ((), ()))
            tn = (((0,), (0,)), ((), ()))
            du = (g * mult * ig + lax.dot_general(pr, wa, nt, preferred_element_type=F32)
                  + lax.dot_general(pi, wx, nt, preferred_element_type=F32))
            dupad[pl.ds(r0, rc), :] = du
            return (dsp + jnp.sum(dlog_a * ((-LRU_C) * r), axis=0, keepdims=True),
                    dba + jnp.sum(dpre_r, axis=0, keepdims=True),
                    dbx + jnp.sum(dpre_i, axis=0, keepdims=True),
                    dwa + lax.dot_general(ub, pr, tn, preferred_element_type=F32),
                    dwx + lax.dot_general(ub, pi, tn, preferred_element_type=F32))

        dsp, dba, dbx, dwa, dwx = lax.fori_loop(0, t // rc, grads, (zrow, zrow, zrow, zmat, zmat))
        dba_ref[...] = dba
        dbx_ref[...] = dbx
        dwa_ref[...] = dwa
        dwx_ref[...] = dwx
        dlam_ref[...] = -dsp * jax.nn.sigmoid(-lam)

        def conv_back(i, carry):
            dcb, d0, d1, d2, d3 = carry
            r0 = pl.multiple_of(i * rc, rc)
            dwin = dupad[pl.ds(r0, rc + h8), :]
            du = dwin[:rc]
            du0 = (cw_ref[3:4, :] * du + cw_ref[2:3, :] * _up(dwin, 1)[:rc]
                   + cw_ref[1:2, :] * _up(dwin, 2)[:rc] + cw_ref[0:1, :] * _up(dwin, 3)[:rc])
            dproj_ref[0, pl.ds(r0, rc), :] = du0.astype(BF16)
            win = upad[pl.ds(r0, rc + h8), :]
            red = lambda v: jnp.sum(v, axis=0, keepdims=True)
            return (dcb + red(du), d0 + red(du * _down(win, 3)[h8:]), d1 + red(du * _down(win, 2)[h8:]),
                    d2 + red(du * _down(win, 1)[h8:]), d3 + red(du * win[h8:]))

        dcb, d0, d1, d2, d3 = lax.fori_loop(0, t // rc, conv_back, (zrow,) * 5)
        dcb_ref[...] = dcb
        dcw_ref[0:1, :] = d0
        dcw_ref[1:2, :] = d1
        dcw_ref[2:3, :] = d2
        dcw_ref[3:4, :] = d3

    blk = lambda i: (0, i)
    vec = jax.ShapeDtypeStruct((1, c), F32)
    mat = jax.ShapeDtypeStruct((heads, LRU_BW, LRU_BW), F32)
    full = lambda: pltpu.VMEM((t, LRU_BW), F32)
    padded = lambda: pltpu.VMEM((t + h8, LRU_BW), F32)
    return pl.pallas_call(
        lambda *refs: body(*refs[len(deps):]),
        name=name,
        grid=(heads,),
        in_specs=[_ANY] * len(deps) + _lru_in_specs(t, heads)[:1] + [pl.BlockSpec((t, LRU_BW), blk)] + _lru_in_specs(t, heads)[1:],
        out_specs=[pl.BlockSpec((2, t, LRU_BW), lambda i: (0, 0, i)), pl.BlockSpec((CONV_WIDTH, LRU_BW), blk)]
        + [pl.BlockSpec((1, LRU_BW), blk)] * 4 + [pl.BlockSpec((None, LRU_BW, LRU_BW), lambda i: (i, 0, 0))] * 2,
        out_shape=[jax.ShapeDtypeStruct((2, t, c), BF16), jax.ShapeDtypeStruct((CONV_WIDTH, c), F32), vec, vec, vec, vec, mat, mat],
        scratch_shapes=[padded(), full(), full(), full(), padded(), padded(), full(), padded()] + [full()] * 2,
        compiler_params=_params("parallel"),
    )(*deps, proj, dhg, conv_w, conv_b, wa, ba, wx, bx, lam)


def _pick_level(g, levels):
    out = levels[-1]
    for k in range(len(levels) - 2, -1, -1):
        out = jnp.where(g == k, levels[k], out)
    return out


def _pool_z(win, g, row0, rc):
    levels, cur = [], win
    for k in range(len(POOL_WINDOWS)):
        cur = cur + _down(cur, 1 << k)
        levels.append(cur[POOL_HALO:])
    tot = _pick_level(g, levels)
    width = jnp.left_shift(2, g)
    row = row0 + lax.broadcasted_iota(jnp.int32, tot.shape, 0)
    cnt = jnp.minimum(row + 1, width).astype(F32)
    return tot / cnt - win[POOL_HALO:], cnt


def _pool_specs(t, gw):
    blk = lambda g: (0, g)
    return [pl.BlockSpec((t, gw), blk), pl.BlockSpec((None, gw, gw), lambda g: (g, 0, 0)),
            pl.BlockSpec((1, gw), blk), pl.BlockSpec((1, gw), blk)]


def pool_fwd(name, u, w_grp, b_grp, scale):
    t, d = u.shape
    gw = d // len(POOL_WINDOWS)
    rc = min(256, t)

    def body(u_ref, wg_ref, bg_ref, sc_ref, out_ref, upad):
        g = pl.program_id(0)
        upad[0:POOL_HALO, :] = jnp.zeros((POOL_HALO, gw), F32)
        upad[POOL_HALO:, :] = u_ref[...]
        wg, bg, sc = wg_ref[...], bg_ref[...], sc_ref[...]

        def chunk(i, carry):
            r0 = pl.multiple_of(i * rc, rc)
            z, _ = _pool_z(upad[pl.ds(r0, rc + POOL_HALO), :], g, r0, rc)
            z2 = jnp.dot(z.astype(BF16), wg, preferred_element_type=F32) + bg
            out_ref[pl.ds(r0, rc), :] = (z2 * sc).astype(BF16)
            return carry

        lax.fori_loop(0, t // rc, chunk, 0)

    return pl.pallas_call(
        body,
        name=name,
        grid=(len(POOL_WINDOWS),),
        in_specs=_pool_specs(t, gw),
        out_specs=pl.BlockSpec((t, gw), lambda g: (0, g)),
        out_shape=jax.ShapeDtypeStruct((t, d), BF16),
        scratch_shapes=[pltpu.VMEM((t + POOL_HALO, gw), F32)],
        compiler_params=_params("parallel"),
    )(u, w_grp, b_grp, scale)


def pool_bwd(name, u, dzs, w_grp, b_grp, scale, deps=()):
    t, d = u.shape
    gw = d // len(POOL_WINDOWS)
    rc = min(256, t)

    def body(u_ref, dzs_ref, wg_ref, bg_ref, sc_ref, du_ref, dwg_ref, dbg_ref, dsc_ref, upad, qpad, dz_s):
        g = pl.program_id(0)
        upad[0:POOL_HALO, :] = jnp.zeros((POOL_HALO, gw), F32)
        upad[POOL_HALO:, :] = u_ref[...]
        qpad[t:, :] = jnp.zeros((POOL_HALO, gw), F32)
        wg, bg, sc = wg_ref[...], bg_ref[...], sc_ref[...]
        zrow = jnp.zeros((1, gw), F32)

        def chunk(i, carry):
            dsc, dbg, dwg = carry
            r0 = pl.multiple_of(i * rc, rc)
            z, cnt = _pool_z(upad[pl.ds(r0, rc + POOL_HALO), :], g, r0, rc)
            zb = z.astype(BF16)
            z2 = jnp.dot(zb, wg, preferred_element_type=F32) + bg
            dzs = dzs_ref[pl.ds(r0, rc), :]
            dz2 = dzs * sc
            d2b = dz2.astype(BF16)
            dz = lax.dot_general(d2b, wg, (((1,), (1,)), ((), ())), preferred_element_type=F32)
            dz_s[pl.ds(r0, rc), :] = dz
            qpad[pl.ds(r0, rc), :] = dz / cnt
            return (dsc + jnp.sum(dzs * z2, axis=0, keepdims=True), dbg + jnp.sum(dz2, axis=0, keepdims=True),
                    dwg + lax.dot_general(zb, d2b, (((0,), (0,)), ((), ())), preferred_element_type=F32))

        dsc, dbg, dwg = lax.fori_loop(0, t // rc, chunk, (zrow, zrow, jnp.zeros((gw, gw), F32)))
        dsc_ref[...] = dsc
        dbg_ref[...] = dbg
        dwg_ref[...] = dwg

        def spread(i, carry):
            r0 = pl.multiple_of(i * rc, rc)
            levels, cur = [], qpad[pl.ds(r0, rc + POOL_HALO), :]
            for k in range(len(POOL_WINDOWS)):
                cur = cur + _up(cur, 1 << k)
                levels.append(cur[:rc])
            du_ref[pl.ds(r0, rc), :] = (_pick_level(g, levels) - dz_s[pl.ds(r0, rc), :]).astype(BF16)
            return carry

        lax.fori_loop(0, t // rc, spread, 0)

    blk = lambda g: (0, g)
    vec = jax.ShapeDtypeStruct((1, d), F32)
    return pl.pallas_call(
        lambda *refs: body(*refs[len(deps):]),
        name=name,
        grid=(len(POOL_WINDOWS),),
        in_specs=[_ANY] * len(deps) + _pool_specs(t, gw)[:1] + [pl.BlockSpec((t, gw), blk)] + _pool_specs(t, gw)[1:],
        out_specs=[pl.BlockSpec((t, gw), blk), pl.BlockSpec((None, gw, gw), lambda g: (g, 0, 0)),
                   pl.BlockSpec((1, gw), blk), pl.BlockSpec((1, gw), blk)],
        out_shape=[jax.ShapeDtypeStruct((t, d), BF16), jax.ShapeDtypeStruct((len(POOL_WINDOWS), gw, gw), F32), vec, vec],
        scratch_shapes=[pltpu.VMEM((t + POOL_HALO, gw), F32), pltpu.VMEM((t + POOL_HALO, gw), F32), pltpu.VMEM((t, gw), F32)],
        compiler_params=_params("parallel"),
    )(*deps, u, dzs, w_grp, b_grp, scale)


def _place():
    return lax.axis_index("x"), lax.axis_index("y"), lax.axis_index("c")


def _other_chips(x, y):
    return [(1 - x, y), (x, 1 - y), (1 - x, 1 - y)]


def _half(c, rows):
    h = rows // 2
    return pl.ds(pl.multiple_of(c * h, 8), h)


_ANY = pl.BlockSpec(memory_space=pl.ANY)


def into_block(name, shards, layer, r, me, dtype):
    c = shards.shape[1]
    tr = _tile(r, 512, 16)
    per = r // tr

    def body(me_ref, s_ref, o_ref):
        o_ref[...] = s_ref[...].astype(o_ref.dtype)

    return pl.pallas_call(
        body,
        name=name,
        grid_spec=pltpu.PrefetchScalarGridSpec(
            num_scalar_prefetch=1,
            grid=(per,),
            in_specs=[pl.BlockSpec((tr, c), lambda i, me_ref: (layer * per + i, 0))],
            out_specs=pl.BlockSpec((None, tr, c), lambda i, me_ref: (me_ref[0], i, 0)),
        ),
        out_shape=jax.ShapeDtypeStruct((N_CHIPS, r, c), dtype),
        compiler_params=_params("parallel"),
    )(me, shards)


_HBM = pl.BlockSpec(memory_space=pltpu.HBM)
_SEM = pl.BlockSpec(memory_space=pltpu.SEMAPHORE)


def _in_hbm(a):
    return pltpu.with_memory_space_constraint(a, pltpu.HBM)


def split_start(name, plan, n_copies, bufs, dep):
    n = len(bufs)

    def body(*refs):
        for cp in plan(refs[:n], refs[n + 1], refs[n + 2]):
            cp.start()
        refs[-1][...] = jnp.zeros_like(refs[-1])

    res = pl.pallas_call(
        body,
        name=name,
        in_specs=[_HBM] * n + [_ANY],
        out_specs=[_SEM, _SEM] + [_HBM] * n + [pl.BlockSpec(memory_space=pltpu.VMEM)],
        out_shape=[pltpu.SemaphoreType.DMA((n_copies,)), pltpu.SemaphoreType.DMA((n_copies,))]
        + [pltpu.HBM(b.shape, b.dtype) for b in bufs] + [jax.ShapeDtypeStruct((8, 128), F32)],
        input_output_aliases={i: 2 + i for i in range(n)},
        compiler_params=pltpu.CompilerParams(has_side_effects=pltpu.SideEffectType.DATAFLOW_SIDE_EFFECTING),
    )(*[_in_hbm(b) for b in bufs], dep)
    return res[0], res[1], list(res[2:2 + n]), res[-1]


def split_wait(name, plan, send_sems, recv_sems, bufs, after):
    n = len(bufs)

    def body(*refs):
        copies = plan(refs[:n], refs[n], refs[n + 1])
        for cp in copies:
            cp.wait_send()
        for cp in copies:
            cp.wait_recv()

    return pl.pallas_call(
        body,
        name=name,
        in_specs=[_HBM] * n + [_SEM, _SEM, _ANY],
        out_specs=[_HBM] * n,
        out_shape=[pltpu.HBM(b.shape, b.dtype) for b in bufs],
        input_output_aliases={i: i for i in range(n)},
        compiler_params=pltpu.CompilerParams(has_side_effects=pltpu.SideEffectType.DATAFLOW_SIDE_EFFECTING),
    )(*bufs, send_sems, recv_sems, after)


def gather_plan(n):
    def plan(bufs, send_sems, recv_sems):
        x, y, c = _place()
        copies = []
        for i in range(n):
            blk = bufs[i].at[2 * x + y, _half(c, bufs[i].shape[1]), :]
            for j, chip in enumerate(_other_chips(x, y)):
                copies.append(pltpu.make_async_remote_copy(
                    src_ref=blk, dst_ref=blk, send_sem=send_sems.at[3 * i + j], recv_sem=recv_sems.at[3 * i + j],
                    device_id=(*chip, c), device_id_type=MESH))
        return copies

    return plan


def forward_plan(n):
    def plan(bufs, send_sems, recv_sems):
        x, y, c = _place()
        copies = []
        for i in range(n):
            for j, (cx, cy) in enumerate(_other_chips(x, y)):
                blk = bufs[i].at[2 * cx + cy, _half(c, bufs[i].shape[1]), :]
                copies.append(pltpu.make_async_remote_copy(
                    src_ref=blk, dst_ref=blk, send_sem=send_sems.at[3 * i + j], recv_sem=recv_sems.at[3 * i + j],
                    device_id=(x, y, 1 - c), device_id_type=MESH))
        return copies

    return plan


def pair_forward(name, bufs):
    n = len(bufs)

    def body(*refs):
        copies = forward_plan(n)(refs[n:2 * n], refs[2 * n], refs[2 * n + 1])
        for cp in copies:
            cp.start()
        for cp in copies:
            cp.wait()

    return pl.pallas_call(
        body,
        name=name,
        in_specs=[_ANY] * n,
        out_specs=[_ANY] * n,
        out_shape=[jax.ShapeDtypeStruct(b.shape, b.dtype) for b in bufs],
        input_output_aliases={i: i for i in range(n)},
        scratch_shapes=[pltpu.SemaphoreType.DMA((3 * n,)), pltpu.SemaphoreType.DMA((3 * n,))],
    )(*bufs)


def all_gather_chips(name, bufs):
    n = len(bufs)

    def body(*refs):
        outs = refs[n:2 * n]
        send_sems, recv_sems = refs[2 * n:]
        x, y, c = _place()
        me, sibling = 2 * x + y, (x, y, 1 - c)
        chips = _other_chips(x, y)

        def copy(i, slot, block, half, to):
            blk = outs[i].at[block, _half(half, outs[i].shape[1]), :]
            return pltpu.make_async_remote_copy(
                src_ref=blk, dst_ref=blk, send_sem=send_sems.at[i * 6 + slot], recv_sem=recv_sems.at[i * 6 + slot],
                device_id=to, device_id_type=MESH)

        first = [copy(i, j, me, c, (*chip, c)) for i in range(n) for j, chip in enumerate(chips)]
        for cp in first:
            cp.start()
        passed = []
        for i in range(n):
            for j, (cx, cy) in enumerate(chips):
                copy(i, j, 2 * cx + cy, c, (x, y, c)).wait_recv()
                fwd = copy(i, 3 + j, 2 * cx + cy, c, sibling)
                fwd.start()
                passed.append(fwd)
        for i in range(n):
            for j, (cx, cy) in enumerate(chips):
                copy(i, 3 + j, 2 * cx + cy, 1 - c, (x, y, c)).wait_recv()
        for cp in first + passed:
            cp.wait_send()

    return pl.pallas_call(
        body,
        name=name,
        in_specs=[_ANY] * n,
        out_specs=[_ANY] * n,
        out_shape=[jax.ShapeDtypeStruct(b.shape, b.dtype) for b in bufs],
        input_output_aliases={i: i for i in range(n)},
        scratch_shapes=[pltpu.SemaphoreType.DMA((6 * n,)), pltpu.SemaphoreType.DMA((6 * n,))],
    )(*bufs)


def pair_plan(n):
    def plan(bufs, send_sems, recv_sems):
        x, y, c = _place()
        return [pltpu.make_async_remote_copy(
            src_ref=bufs[i].at[:, _half(1 - c, bufs[i].shape[1]), :], dst_ref=bufs[n + i], send_sem=send_sems.at[i],
            recv_sem=recv_sems.at[i], device_id=(x, y, 1 - c), device_id_type=MESH) for i in range(n)]

    return plan


def chip_plan(n):
    def plan(bufs, send_sems, recv_sems):
        x, y, c = _place()
        copies = []
        for i in range(n):
            for j, (cx, cy) in enumerate(_other_chips(x, y)):
                copies.append(pltpu.make_async_remote_copy(
                    src_ref=bufs[i].at[2 * cx + cy], dst_ref=bufs[n + i].at[2 * x + y], send_sem=send_sems.at[3 * i + j],
                    recv_sem=recv_sems.at[3 * i + j], device_id=(cx, cy, c), device_id_type=MESH))
        return copies

    return plan


def pair_lands(grads):
    return [jax.ShapeDtypeStruct((g.shape[0], g.shape[1] // 2, g.shape[2]), g.dtype) for g in grads]


def pair_gather(name, bufs, blocked, layers):
    n = len(bufs)
    n_copies = sum(layers)

    def body(*refs):
        outs = refs[n:2 * n]
        send_sems, recv_sems = refs[2 * n:]
        x, y, c = _place()
        copies = []
        for i in range(n):
            buf = outs[i].at[2 * x + y] if blocked[i] else outs[i]
            r = buf.shape[0] // layers[i]
            for l in range(layers[i]):
                mine = buf.at[pl.ds(pl.multiple_of(l * r + c * (r // 2), 8), r // 2), :]
                copies.append(pltpu.make_async_remote_copy(
                    src_ref=mine, dst_ref=mine, send_sem=send_sems.at[len(copies)], recv_sem=recv_sems.at[len(copies)],
                    device_id=(x, y, 1 - c), device_id_type=MESH))
        for cp in copies:
            cp.start()
        for cp in copies:
            cp.wait()

    return pl.pallas_call(
        body,
        name=name,
        in_specs=[_ANY] * n,
        out_specs=[_ANY] * n,
        out_shape=[jax.ShapeDtypeStruct(b.shape, b.dtype) for b in bufs],
        input_output_aliases={i: i for i in range(n)},
        scratch_shapes=[pltpu.SemaphoreType.DMA((n_copies,)), pltpu.SemaphoreType.DMA((n_copies,))],
    )(*bufs)


def pair_sum(name, grad, recv, core, dtype):
    _, r, c = grad.shape
    h = r // 2
    th = _tile(h, 1024, 16)
    per = h // th

    def body(core_ref, g_ref, r_ref, o_ref):
        o_ref[...] = (g_ref[...].astype(F32) + r_ref[...].astype(F32)).astype(o_ref.dtype)

    return pl.pallas_call(
        body,
        name=name,
        grid_spec=pltpu.PrefetchScalarGridSpec(
            num_scalar_prefetch=1,
            grid=(N_CHIPS, per),
            in_specs=[pl.BlockSpec((None, th, c), lambda k, i, core_ref: (k, core_ref[0] * per + i, 0)),
                      pl.BlockSpec((None, th, c), lambda k, i, core_ref: (k, i, 0))],
            out_specs=pl.BlockSpec((None, th, c), lambda k, i, core_ref: (k, i, 0)),
        ),
        out_shape=jax.ShapeDtypeStruct((N_CHIPS, h, c), dtype),
        compiler_params=_params("parallel", "parallel"),
    )(core, grad, recv)


def chip_sum(name, got, parts, place, blocked, into=None, layer=0, n_layers=1):
    _, h, c = parts.shape
    th = _tile(h, 512, 16)
    per = h // th

    def body(place_ref, q0, q1, q2, q3, p_ref, *rest):
        o_ref = rest[-1]
        me = place_ref[0]
        own = p_ref[...].astype(F32)
        v = [jnp.where(me == k, own, q[...].astype(F32)) for k, q in enumerate((q0, q1, q2, q3))]
        o_ref[...] = ((v[0] + v[1]) + v[2]) + v[3]

    def got_spec(k):
        return pl.BlockSpec((None, th, c), lambda i, pr: (jnp.where(pr[0] == k, (k + 1) % N_CHIPS, k), i, 0))

    if blocked:
        out_spec = pl.BlockSpec((None, th, c), lambda i, pr: (pr[0], pr[1] * per + i, 0))
        out_shape = jax.ShapeDtypeStruct((N_CHIPS, 2 * h, c), F32)
    else:
        out_spec = pl.BlockSpec((th, c), lambda i, pr: ((2 * layer + pr[1]) * per + i, 0))
        out_shape = jax.ShapeDtypeStruct((n_layers * 2 * h, c), F32)
    carried = [] if into is None else [into]
    return pl.pallas_call(
        body,
        name=name,
        grid_spec=pltpu.PrefetchScalarGridSpec(
            num_scalar_prefetch=1,
            grid=(per,),
            in_specs=[got_spec(k) for k in range(N_CHIPS)] + [pl.BlockSpec((None, th, c), lambda i, pr: (pr[0], i, 0))]
            + [_ANY] * len(carried),
            out_specs=out_spec,
        ),
        out_shape=out_shape,
        input_output_aliases={6: 0} if carried else {},
        compiler_params=_params("parallel"),
    )(place, got, got, got, got, parts, *carried)


def adamw(name, w, g, m, v):
    r, c = w.shape
    tr = _tile(r, 512, 8)
    c1 = 1.0 - ADAM_B1 ** ADAM_STEP
    c2 = 1.0 - ADAM_B2 ** ADAM_STEP

    def body(w_ref, g_ref, m_ref, v_ref, d_ref, nm_ref, nv_ref, g_out_ref):
        gv = g_ref[...]
        g_out_ref[...] = gv
        nm = ADAM_B1 * m_ref[...] + (1.0 - ADAM_B1) * gv
        nv = ADAM_B2 * v_ref[...] + (1.0 - ADAM_B2) * (gv * gv)
        d_ref[...] = -ADAM_LR * ((nm / c1) / (jnp.sqrt(nv / c2) + ADAM_EPS) + ADAM_WD * w_ref[...])
        nm_ref[...] = nm
        nv_ref[...] = nv

    spec = pl.BlockSpec((tr, c), lambda i: (i, 0))
    return pl.pallas_call(
        body,
        name=name,
        grid=(r // tr,),
        in_specs=[spec] * 4,
        out_specs=[spec] * 4,
        out_shape=[jax.ShapeDtypeStruct((r, c), F32)] * 4,
        compiler_params=_params("parallel"),
    )(w, g, m, v)


def adamw_small(name, ws, gs, ms, vs):
    n = len(ws)
    c1 = 1.0 - ADAM_B1 ** ADAM_STEP
    c2 = 1.0 - ADAM_B2 ** ADAM_STEP

    def body(*refs):
        for i in range(n):
            w_ref, g_ref, m_ref, v_ref = (refs[j * n + i] for j in range(4))
            d_ref, nm_ref, nv_ref = (refs[(4 + j) * n + i] for j in range(3))
            gv = g_ref[...]
            nm = ADAM_B1 * m_ref[...] + (1.0 - ADAM_B1) * gv
            nv = ADAM_B2 * v_ref[...] + (1.0 - ADAM_B2) * (gv * gv)
            d_ref[...] = -ADAM_LR * ((nm / c1) / (jnp.sqrt(nv / c2) + ADAM_EPS) + ADAM_WD * w_ref[...])
            nm_ref[...] = nm
            nv_ref[...] = nv

    whole = pl.BlockSpec(memory_space=pltpu.VMEM)
    res = pl.pallas_call(
        body,
        name=name,
        in_specs=[whole] * (4 * n),
        out_specs=[whole] * (3 * n),
        out_shape=[jax.ShapeDtypeStruct(w.shape, F32) for w in ws] * 3,
        compiler_params=pltpu.CompilerParams(vmem_limit_bytes=VMEM_LIMIT_BYTES),
    )(*ws, *gs, *ms, *vs)
    return res[:n], res[n:2 * n], res[2 * n:]


def _pack(arrays, row_multiple, cols=BLOB_COLS):
    flat = jnp.concatenate([a.reshape(-1).astype(F32) for a in arrays])
    rows = -(-flat.shape[0] // cols)
    rows = -(-rows // row_multiple) * row_multiple
    return jnp.pad(flat, (0, rows * cols - flat.shape[0])).reshape(rows, cols)


def _unpack(blob, shapes):
    flat, out, off = blob.reshape(-1), [], 0
    for s in shapes:
        size = math.prod(s)
        out.append(flat[off:off + size].reshape(s))
        off += size
    return out


def _unpack_rows(blobs, shapes):
    out, off = [], 0
    for s in shapes:
        size = math.prod(s)
        out.append(blobs[:, off:off + size].reshape((blobs.shape[0],) + tuple(s)))
        off += size
    return out


def kernel(x, p, lru_w_in, lru_conv_w, lru_conv_b, lru_wa, lru_ba, lru_wx, lru_bx, lru_lambda, lru_w_out, pool_w_in, pool_w_grp, pool_b_grp, pool_scale, pool_w_out, ln_mix_g, ln_mix_b, mlp_w1, mlp_w2, ln_mlp_g, ln_mlp_b, ple_w, ple_gate_w, ple_gate_b, loss_target, m_lru_w_in, m_lru_conv_w, m_lru_conv_b, m_lru_wa, m_lru_ba, m_lru_wx, m_lru_bx, m_lru_lambda, m_lru_w_out, m_pool_w_in, m_pool_w_grp, m_pool_b_grp, m_pool_scale, m_pool_w_out, m_ln_mix_g, m_ln_mix_b, m_mlp_w1, m_mlp_w2, m_ln_mlp_g, m_ln_mlp_b, m_ple_w, m_ple_gate_w, m_ple_gate_b, v_lru_w_in, v_lru_conv_w, v_lru_conv_b, v_lru_wa, v_lru_ba, v_lru_wx, v_lru_bx, v_lru_lambda, v_lru_w_out, v_pool_w_in, v_pool_w_grp, v_pool_b_grp, v_pool_scale, v_pool_w_out, v_ln_mix_g, v_ln_mix_b, v_mlp_w1, v_mlp_w2, v_ln_mlp_g, v_ln_mlp_b, v_ple_w, v_ple_gate_w, v_ple_gate_b):
    weights = dict(lru_w_in=lru_w_in, lru_conv_w=lru_conv_w, lru_conv_b=lru_conv_b, lru_wa=lru_wa, lru_ba=lru_ba, lru_wx=lru_wx, lru_bx=lru_bx, lru_lambda=lru_lambda, lru_w_out=lru_w_out, pool_w_in=pool_w_in, pool_w_grp=pool_w_grp, pool_b_grp=pool_b_grp, pool_scale=pool_scale, pool_w_out=pool_w_out, ln_mix_g=ln_mix_g, ln_mix_b=ln_mix_b, mlp_w1=mlp_w1, mlp_w2=mlp_w2, ln_mlp_g=ln_mlp_g, ln_mlp_b=ln_mlp_b, ple_w=ple_w, ple_gate_w=ple_gate_w, ple_gate_b=ple_gate_b)
    mom_m = dict(lru_w_in=m_lru_w_in, lru_conv_w=m_lru_conv_w, lru_conv_b=m_lru_conv_b, lru_wa=m_lru_wa, lru_ba=m_lru_ba, lru_wx=m_lru_wx, lru_bx=m_lru_bx, lru_lambda=m_lru_lambda, lru_w_out=m_lru_w_out, pool_w_in=m_pool_w_in, pool_w_grp=m_pool_w_grp, pool_b_grp=m_pool_b_grp, pool_scale=m_pool_scale, pool_w_out=m_pool_w_out, ln_mix_g=m_ln_mix_g, ln_mix_b=m_ln_mix_b, mlp_w1=m_mlp_w1, mlp_w2=m_mlp_w2, ln_mlp_g=m_ln_mlp_g, ln_mlp_b=m_ln_mlp_b, ple_w=m_ple_w, ple_gate_w=m_ple_gate_w, ple_gate_b=m_ple_gate_b)
    mom_v = dict(lru_w_in=v_lru_w_in, lru_conv_w=v_lru_conv_w, lru_conv_b=v_lru_conv_b, lru_wa=v_lru_wa, lru_ba=v_lru_ba, lru_wx=v_lru_wx, lru_bx=v_lru_bx, lru_lambda=v_lru_lambda, lru_w_out=v_lru_w_out, pool_w_in=v_pool_w_in, pool_w_grp=v_pool_w_grp, pool_b_grp=v_pool_b_grp, pool_scale=v_pool_scale, pool_w_out=v_pool_w_out, ln_mix_g=v_ln_mix_g, ln_mix_b=v_ln_mix_b, mlp_w1=v_mlp_w1, mlp_w2=v_mlp_w2, ln_mlp_g=v_ln_mlp_g, ln_mlp_b=v_ln_mlp_b, ple_w=v_ple_w, ple_gate_w=v_ple_gate_w, ple_gate_b=v_ple_gate_b)
    names = list(weights)

    depth, d = ln_mix_g.shape
    t = x.shape[1]
    n_a, n_b = lru_w_in.shape[0], pool_w_in.shape[0]
    d_rnn = lru_w_out.shape[1] * N_CHIPS
    heads = d_rnn // LRU_BW
    d_ff = mlp_w1.shape[2] * N_CHIPS
    ple_dim = ple_w.shape[1]
    n_grp = len(POOL_WINDOWS)
    gw = d // n_grp
    alpha = (2 * depth) ** 0.25
    chip = 2 * lax.axis_index("x") + lax.axis_index("y")
    place = jnp.stack([chip, lax.axis_index("c")]).astype(jnp.int32)

    x2d = x.reshape(t, d)
    target = loss_target.reshape(t, d)
    p3 = p.reshape(depth, t, ple_dim)

    big = ["lru_w_in", "lru_w_out", "pool_w_in", "pool_w_out", "mlp_w1", "mlp_w2", "ple_w", "ple_gate_w", "pool_w_grp"]
    flat2 = lambda a: a.reshape(-1, a.shape[-1])
    small_sharded = ["lru_conv_w", "pool_b_grp", "pool_scale"]
    small_blob = _pack([weights[k] for k in small_sharded], 16, cols=256)
    every_layer = ("mlp_w1", "mlp_w2", "ple_w", "ple_gate_w")

    def layer_keys(i):
        return (["lru_w_in", "lru_w_out"] if i % 2 == 0 else ["pool_w_in", "pool_w_out", "pool_w_grp"]) + list(every_layer)

    def stage(k, i):
        w = weights[k]
        return into_block(f"stage_l{i}_{k}", flat2(w), i if k in every_layer else i // 2, math.prod(w.shape[1:-1]),
                          place[:1], BF16)

    staged = [[stage(k, i) for k in layer_keys(i)] for i in range(depth)]
    first = all_gather_chips("gather_l0", staged[0][:1] + [into_block("stage_small", small_blob, 0, small_blob.shape[0], place[:1], F32)])
    wg = {(layer_keys(0)[0], 0): first[0]}

    tokens = []

    def take_tokens():
        deps = tuple(tokens)
        tokens.clear()
        return deps

    def mm(*args, **kwargs):
        return matmul(*args, deps=take_tokens(), **kwargs)

    def start_gather(tag, bufs, dep):
        plan = gather_plan(len(bufs))
        flight = (plan,) + split_start(f"gather_{tag}_start", plan, 3 * len(bufs), bufs, dep)
        tokens.append(flight[-1])
        return flight

    def land_gather(tag, flight, keys, layer, after, wait_for=True):
        plan, send_sems, recv_sems, bufs, _ = flight
        landed = split_wait(f"gather_{tag}_wait", plan, send_sems, recv_sems, bufs, after)
        if wait_for:
            wg.update(zip([(k, layer) for k in keys], pair_forward(f"gather_{tag}_forward", landed)))
            return None
        plan = forward_plan(len(landed))
        forwarding = (tag, plan) + split_start(f"gather_{tag}_forward_start", plan, 3 * len(landed), landed, after)
        tokens.append(forwarding[-1])
        return forwarding

    def finish_forward(forwarding, keys, layer, after):
        tag, plan, send_sems, recv_sems, bufs, _ = forwarding
        wg.update(zip([(k, layer) for k in keys], split_wait(f"gather_{tag}_forward_wait", plan, send_sems, recv_sems, bufs, after)))

    conv_w_sh, b_grp_sh, scale_sh = _unpack_rows(first[-1].reshape(N_CHIPS, -1), [weights[k].shape for k in small_sharded])
    conv_w_full = jnp.moveaxis(conv_w_sh, 0, 2).reshape(n_a, CONV_WIDTH, d_rnn)
    b_grp_full = jnp.moveaxis(b_grp_sh, 0, 1).reshape(n_b, 1, d)
    scale_full = jnp.moveaxis(scale_sh, 0, 1).reshape(n_b, 1, d)
    rows_grp = gw // N_CHIPS
    w_grp_full = lambda i: jnp.moveaxis(wg["pool_w_grp", i].reshape(N_CHIPS, n_grp, rows_grp, gw), 0, 1).reshape(n_grp, gw, gw)
    wa_bf, wx_bf = lru_wa.astype(BF16), lru_wx.astype(BF16)
    row = lambda a, i: a[i].reshape(1, -1)

    def ln_after(acc, x_in, g, b):
        s = alpha * x_in + acc
        y = _ln_stats(s)[0] * g + b
        return y, y, s

    ln_outs = [plain(shape=(t, d), dtype=F32), plain(shape=(t, d), dtype=BF16), plain(shape=(t, d), dtype=F32)]
    saved = []
    cur, cur_bf = x2d, x2d
    for i in range(depth):
        slot = i // 2
        sv = dict(x_bf=cur_bf)
        if i == 0:
            flight = start_gather("l0_rest", staged[0][1:], first[0])
        elif i + 1 < depth:
            flight = start_gather(f"l{i + 1}", staged[i + 1], cur)
        if i % 2 == 0:
            (proj,) = mm(f"l{i}_lru_in", plain(cur_bf), colsplit(wg["lru_w_in", i], 0, d), "nn",
                         [colsplit(None, 0, t, n=2, full=(2, t, d_rnn), dtype=F32)])
            hg = lru_fwd(f"l{i}_lru", proj, conv_w_full[slot], row(lru_conv_b, slot), wa_bf[slot], row(lru_ba, slot),
                         wx_bf[slot], row(lru_bx, slot), row(lru_lambda, slot))
            if i == 0:
                land_gather("l0_rest", flight, layer_keys(0)[1:], 0, hg)
                flight = start_gather("l1", staged[1], hg)
            x1, x1_bf, s1 = mm(f"l{i}_lru_out", plain(hg), rowsplit_whole(wg["lru_w_out", i]), "nn", ln_outs, pk=2048,
                               epilogue=ln_after, tiles=[plain(cur)], rows=[row(ln_mix_g, i), row(ln_mix_b, i)])
            sv.update(proj=proj, act=hg)
        else:
            (u,) = mm(f"l{i}_pool_in", plain(cur_bf), rowsplit_whole(wg["pool_w_in", i]), "nn",
                          [plain(shape=(t, d), dtype=F32)])
            zs = pool_fwd(f"l{i}_pool", u, w_grp_full(i), b_grp_full[slot], scale_full[slot])
            x1, x1_bf, s1 = mm(f"l{i}_pool_out", plain(zs), rowsplit_whole(wg["pool_w_out", i]), "nn", ln_outs,
                               epilogue=ln_after, tiles=[plain(cur)], rows=[row(ln_mix_g, i), row(ln_mix_b, i)])
            sv.update(u=u, act=zs)

        def relu2(acc):
            hr = jnp.maximum(acc, 0.0)
            return hr, hr * hr

        hr, hh = mm(f"l{i}_mlp_up", plain(x1_bf), colsplit(wg["mlp_w1", i], 0, d), "nn",
                    [plain(shape=(t, d_ff), dtype=BF16), plain(shape=(t, d_ff), dtype=BF16)], epilogue=relu2, pm=2048)
        (mlp,) = mm(f"l{i}_mlp_down", plain(hh), rowsplit_whole(wg["mlp_w2", i]), "nn",
                    [plain(shape=(t, d), dtype=F32)], pk=d_ff)
        x2, x2_bf, s2 = ln_fwd(f"l{i}_ln_mlp", alpha, x1, mlp, row(ln_mlp_g, i), row(ln_mlp_b, i))
        if i + 1 < depth:
            forwarding = land_gather(f"l{i + 1}", flight, layer_keys(i + 1), i + 1, x2_bf, wait_for=False)

        def ple_out(acc, x2_t, gb, p_t, pw):
            e_t = jnp.concatenate([jnp.dot(p_t.astype(BF16), pw[k], preferred_element_type=F32) for k in range(N_CHIPS)], axis=1)
            gate = jax.nn.sigmoid(acc + gb)
            x3 = x2_t + e_t * gate
            return x3, x3, gate, e_t

        cur, cur_bf, gate, e = mm(
            f"l{i}_ple_gate", plain(x2_bf), rowsplit_whole(wg["ple_gate_w", i]), "nn",
            [plain(shape=(t, d), dtype=F32), plain(shape=(t, d), dtype=BF16), plain(shape=(t, d), dtype=F32), plain(shape=(t, d), dtype=F32)],
            epilogue=ple_out, tiles=[plain(x2)], rows=[row(ple_gate_b, i)], pm=512,
            side=[(p3[i], lambda tm, tn: pl.BlockSpec((tm, ple_dim), lambda r, c, kk: (r, 0))),
                  (wg["ple_w", i], lambda tm, tn: pl.BlockSpec(wg["ple_w", i].shape, lambda r, c, kk: (0, 0, 0)))])
        sv.update(s1=s1, x1_bf=x1_bf, hr=hr, hh=hh, s2=s2, x2_bf=x2_bf, gate=gate, e=e)
        saved.append(sv)
        if i + 1 < depth:
            finish_forward(forwarding, layer_keys(i + 1), i + 1, cur)

    dy, loss_part = loss_head("loss", cur, target)
    loss = lax.psum(loss_part.reshape(()), ("x", "y", "c"))

    part = {}
    sums = {}

    def grad_view(key, split):
        w = weights[key]
        return split(None, 0, w.shape[1], full=(N_CHIPS, w.shape[1], w.shape[2]), dtype=BF16)

    def group_start(tag, items, dep):
        srcs = [part[it] for it in items]
        plan = pair_plan(len(srcs))
        lands = [lax.empty(s.shape, s.dtype) for s in pair_lands(srcs)]
        flight = (tag, items, plan) + split_start(f"grads_{tag}_pair_start", plan, len(srcs), srcs + lands, dep)
        tokens.append(flight[-1])
        return flight

    def group_mid(flight, after):
        tag, items, plan, send_sems, recv_sems, bufs, _ = flight
        bufs = split_wait(f"grads_{tag}_pair_wait", plan, send_sems, recv_sems, bufs, after)
        n = len(items)
        parts = [pair_sum(f"grads_{tag}_pair_sum_{j}", bufs[j], bufs[n + j], place[1:], F32 if it[0] == "blob" else BF16)
                 for j, it in enumerate(items)]
        plan = chip_plan(n)
        flight = (tag, items, plan) + split_start(f"grads_{tag}_chip_start", plan, 3 * n,
                                                  parts + [lax.empty(q.shape, q.dtype) for q in parts], after)
        tokens.append(flight[-1])
        return flight

    def group_end(flight, after):
        tag, items, plan, send_sems, recv_sems, bufs, _ = flight
        bufs = split_wait(f"grads_{tag}_chip_wait", plan, send_sems, recv_sems, bufs, after)
        n = len(items)
        for j, (k, layer) in enumerate(items):
            if k == "blob":
                sums[k] = chip_sum(f"grads_{tag}_chip_sum_{j}", bufs[n + j], bufs[j], place, True)
            else:
                sums[k] = chip_sum(f"grads_{tag}_chip_sum_{j}", bufs[n + j], bufs[j], place, False, into=sums.get(k),
                                   layer=layer if k in every_layer else layer // 2, n_layers=weights[k].shape[0])

    big_w = [k for k in big if k != "pool_w_grp"]
    small_keys = [k for k in names if k not in big_w]

    def ln_before(ca):
        def back(acc, upstream, s, g):
            dx = ca * upstream + acc
            xhat, rstd = _ln_stats(s)
            dxh = dx * g
            ds = rstd * (dxh - jnp.mean(dxh, axis=-1, keepdims=True) - xhat * jnp.mean(dxh * xhat, axis=-1, keepdims=True))
            return ds, ds, jnp.sum(dx * xhat, axis=0, keepdims=True), jnp.sum(dx, axis=0, keepdims=True)

        return back

    ds_outs = [plain(shape=(t, d), dtype=F32), plain(shape=(t, d), dtype=BF16)]
    small = {k: [None] * weights[k].shape[0] for k in names if k not in big or k == "pool_w_grp"}
    dcur = dy
    mlp_pair = mlp_chip = mix_pair = mix_chip = None
    for i in reversed(range(depth)):
        slot = i // 2
        sv = saved[i]
        de, dpre, dgb = ple_bwd(f"l{i}_ple_bwd", dcur, sv["gate"], sv["e"])
        small["ple_gate_b"][i] = dgb
        (part["ple_w", i],) = mm(f"l{i}_d_ple_w", plain(p3[i]), plain(de), "tn", [grad_view("ple_w", colsplit)])
        (part["ple_gate_w", i],) = mm(f"l{i}_d_ple_gate_w", plain(sv["x2_bf"]), plain(dpre), "tn",
                                          [grad_view("ple_gate_w", rowsplit)])
        ds2, ds2_bf, dg, db = mm(f"l{i}_dx2", plain(dpre), rowsplit_whole(wg["ple_gate_w", i]), "nt", ds_outs, col_sums=2, pm=512,
                                 epilogue=ln_before(1.0), tiles=[plain(dcur), plain(sv["s2"])], rows=[row(ln_mlp_g, i)])
        small["ln_mlp_g"][i], small["ln_mlp_b"][i] = dg, db
        (part["mlp_w2", i],) = mm(f"l{i}_d_mlp_w2", plain(sv["hh"]), plain(ds2_bf), "tn", [grad_view("mlp_w2", rowsplit)])
        (dhpre,) = mm(f"l{i}_dh", plain(ds2_bf), rowsplit(wg["mlp_w2", i], 0, d_ff // N_CHIPS), "nt",
                      [plain(shape=(t, d_ff), dtype=BF16)], tiles=[plain(sv["hr"])], pm=2048,
                      epilogue=lambda acc, hr_t: (acc * (2.0 * hr_t.astype(F32)),))
        (part["mlp_w1", i],) = mm(f"l{i}_d_mlp_w1", plain(sv["x1_bf"]), plain(dhpre), "tn", [grad_view("mlp_w1", colsplit)])
        if mlp_chip is not None:
            group_end(mlp_chip, dhpre)
        if mix_pair is not None:
            mix_chip = group_mid(mix_pair, dhpre)
        mlp_pair = group_start(f"l{i}_mlp", [(k, i) for k in every_layer], dhpre)
        (dx1b,) = mm(f"l{i}_dx1", plain(dhpre), colsplit(wg["mlp_w1", i], 0, d), "nt", [plain(shape=(t, d), dtype=F32)],
                     pm=2048)
        ds1, ds1_bf, dg, db = ln_bwd(f"l{i}_ln_mix_bwd", alpha, ds2, dx1b, sv["s1"], row(ln_mix_g, i))
        small["ln_mix_g"][i], small["ln_mix_b"][i] = dg, db
        residual = lambda acc, ds_t: (alpha * ds_t + acc,)
        if i % 2 == 0:
            (part["lru_w_out", i],) = mm(f"l{i}_d_lru_out", plain(sv["act"]), plain(ds1_bf), "tn",
                                             [grad_view("lru_w_out", rowsplit)])
            (dhg,) = mm(f"l{i}_dhg", plain(ds1_bf), rowsplit_whole(wg["lru_w_out", i]), "nt",
                            [plain(shape=(t, d_rnn), dtype=F32)], pn=2048)
            mlp_chip = group_mid(mlp_pair, dhg)
            dproj, dcw, dcb, dba, dbx, dlam, dwa, dwx = lru_bwd(
                f"l{i}_lru_bwd", sv["proj"], dhg, conv_w_full[slot], row(lru_conv_b, slot), wa_bf[slot], row(lru_ba, slot),
                wx_bf[slot], row(lru_bx, slot), row(lru_lambda, slot), deps=take_tokens())
            for key, val in (("lru_conv_w", dcw), ("lru_conv_b", dcb), ("lru_ba", dba), ("lru_bx", dbx),
                             ("lru_lambda", dlam), ("lru_wa", dwa), ("lru_wx", dwx)):
                small[key][slot] = val
            dproj_v = colsplit(dproj, 0, t, n=2)
            (part["lru_w_in", i],) = mm(f"l{i}_d_lru_in", plain(sv["x_bf"]), dproj_v, "tn", [grad_view("lru_w_in", colsplit)])
            (dcur,) = mm(f"l{i}_dx", dproj_v, colsplit(wg["lru_w_in", i], 0, d), "nt",
                             [plain(shape=(t, d), dtype=F32)], epilogue=residual, tiles=[plain(ds1)])
        else:
            (part["pool_w_out", i],) = mm(f"l{i}_d_pool_out", plain(sv["act"]), plain(ds1_bf), "tn",
                                              [grad_view("pool_w_out", rowsplit)])
            (dzs,) = mm(f"l{i}_dzs", plain(ds1_bf), rowsplit_whole(wg["pool_w_out", i]), "nt",
                            [plain(shape=(t, d), dtype=F32)])
            mlp_chip = group_mid(mlp_pair, dzs)
            du, dwg, dbg, dsc = pool_bwd(f"l{i}_pool_bwd", sv["u"], dzs, w_grp_full(i), b_grp_full[slot], scale_full[slot],
                                         deps=take_tokens())
            small["pool_w_grp"][slot], small["pool_b_grp"][slot], small["pool_scale"][slot] = dwg, dbg, dsc
            (part["pool_w_in", i],) = mm(f"l{i}_d_pool_in", plain(sv["x_bf"]), plain(du), "tn", [grad_view("pool_w_in", rowsplit)])
            (dcur,) = mm(f"l{i}_dx", plain(du), rowsplit_whole(wg["pool_w_in", i]), "nt",
                             [plain(shape=(t, d), dtype=F32)], epilogue=residual, tiles=[plain(ds1)])
        if mix_chip is not None:
            group_end(mix_chip, dcur)
        mixer = [(k, i) for k in layer_keys(i) if k not in every_layer and k != "pool_w_grp"]
        if i == 0:
            small_full = [jnp.stack(small[k]).reshape((weights[k].shape[0],) + tuple(
                s * (N_CHIPS if ax in _sharded_axis(k) else 1) for ax, s in enumerate(weights[k].shape[1:], 1))) for k in small_keys]
            blob = _pack(small_full, 64)
            part["blob", 0] = blob.reshape(N_CHIPS, blob.shape[0] // N_CHIPS, BLOB_COLS)
            mixer.append(("blob", 0))
        mix_pair = group_start(f"l{i}_mix", mixer, dcur)
    grad_x = dcur.reshape(x.shape)
    mix_chip = group_mid(mix_pair, dcur)
    group_end(mlp_chip, mix_chip[-1])
    group_end(mix_chip, sums["mlp_w1"])
    order = big_w + ["blob"]
    reduced = dict(zip(order, pair_gather("grads_pair_gather", [sums[k] for k in order], [k == "blob" for k in order],
                                          [1 if k == "blob" else weights[k].shape[0] for k in order])))
    (blob_all,) = all_gather_chips("gather_small_grads", [reduced["blob"]])
    small_grads = dict(zip(small_keys, _unpack(blob_all.reshape(blob.shape), [a.shape for a in small_full])))
    for k in small_keys:
        for ax in _sharded_axis(k):
            n = weights[k].shape[ax]
            small_grads[k] = lax.dynamic_slice_in_dim(small_grads[k], chip * n, n, axis=ax)
    grads = {k: reduced[k].reshape(weights[k].shape) for k in big_w}
    grads.update(small_grads)

    delta, new_m, new_v = {}, {}, {}
    for k in big_w:
        dl, nm, nv, g = adamw("adamw_" + k, flat2(weights[k]), flat2(grads[k]), flat2(mom_m[k]), flat2(mom_v[k]))
        delta[k], new_m[k], new_v[k], grads[k] = (a.reshape(weights[k].shape) for a in (dl, nm, nv, g))
    dl, nm, nv = adamw_small("adamw_small", *[[flat2(src[k]) for k in small_keys] for src in (weights, grads, mom_m, mom_v)])
    for out, res in ((delta, dl), (new_m, nm), (new_v, nv)):
        out.update({k: a.reshape(weights[k].shape) for k, a in zip(small_keys, res)})

    return (loss, grad_x, *[grads[k] for k in names], *[delta[k] for k in names],
            *[new_m[k] for k in names], *[new_v[k] for k in names])


def _sharded_axis(key):
    return {"lru_conv_w": (2,), "pool_w_grp": (2,), "pool_b_grp": (1,), "pool_scale": (1,)}.get(key, ())
```

```python
import functools
import math

import jax
import jax.numpy as jnp
from jax import lax
from jax.experimental import pallas as pl
from jax.experimental.pallas import tpu as pltpu

F32 = jnp.float32
BF16 = jnp.bfloat16

N_CHIPS = 4
LRU_BW = 128
LRU_C = 8.0
CONV_WIDTH = 4
POOL_WINDOWS = (2, 4, 8, 16)
POOL_HALO = 16
CONV_HALO = 8
LN_EPS = 1e-5
ADAM_LR = 0.001
ADAM_B1 = 0.9
ADAM_B2 = 0.999
ADAM_EPS = 1e-08
ADAM_WD = 0.01
ADAM_STEP = 10
GELU_C = math.sqrt(2.0 / math.pi)
GELU_K = 0.044715
VMEM_LIMIT_BYTES = 56 * 1024 * 1024
MATMUL_TILE_BYTES = 44 * 1024 * 1024
MESH = pl.DeviceIdType.MESH
BLOB_COLS = 1024


def _params(*sem):
    return pltpu.CompilerParams(dimension_semantics=tuple(sem), vmem_limit_bytes=VMEM_LIMIT_BYTES)


def _tile(unit, pref, align=128):
    if unit <= pref:
        return unit
    for d in range(2, unit + 1):
        if unit % d == 0 and unit // d <= pref and (unit // d) % align == 0:
            return unit // d
    raise ValueError((unit, pref, align))


class View:
    def __init__(self, arr, shape, row_unit, col_unit, block_fn, full=None, dtype=None):
        self.arr, self.shape, self.row_unit, self.col_unit, self.block_fn = arr, shape, row_unit, col_unit, block_fn
        self.full = full if full is not None else arr.shape
        self.dtype = dtype if dtype is not None else arr.dtype

    def spec(self, tr, tc, f):
        block, idx = self.block_fn(tr, tc)
        return pl.BlockSpec(block, lambda *g: idx(*f(*g)))


def plain(arr=None, shape=None, dtype=None):
    shape = arr.shape if arr is not None else shape
    return View(arr, shape, shape[0], shape[1], lambda tr, tc: ((tr, tc), lambda rt, ct: (rt, ct)), full=shape, dtype=dtype)


def colsplit(arr, layer, rows, n=N_CHIPS, full=None, dtype=None):
    full = arr.shape if arr is not None else full
    c = full[2]

    def block_fn(tr, tc):
        assert rows % tr == 0 and c % tc == 0, (rows, tr, c, tc)
        per, rpl = c // tc, rows // tr
        return (None, tr, tc), lambda rt, ct: (ct // per, layer * rpl + rt, ct % per)

    return View(arr, (rows, n * c), rows, c, block_fn, full=full, dtype=dtype)


def rowsplit(arr, layer, rows, n=N_CHIPS, full=None, dtype=None):
    full = arr.shape if arr is not None else full
    c = full[2]

    def block_fn(tr, tc):
        assert rows % tr == 0 and c % tc == 0, (rows, tr, c, tc)
        per = rows // tr
        return (None, tr, tc), lambda rt, ct: (rt // per, layer * per + rt % per, ct)

    return View(arr, (n * rows, c), rows, c, block_fn, full=full, dtype=dtype)


def rowsplit_whole(arr):
    n, rows, c = arr.shape

    def block_fn(tr, tc):
        assert tr == n * rows and c % tc == 0, (tr, n, rows, c, tc)
        return (n, rows, tc), lambda rt, ct: (0, 0, ct)

    return View(arr, (n * rows, c), n * rows, c, block_fn)


def matmul(name, a, b, mode, outs, epilogue=None, tiles=(), rows=(), side=(), deps=(), col_sums=0, pm=1024, pn=1024, pk=1024):
    if mode == "nn":
        (m, k), (k2, n) = a.shape, b.shape
        um, uk, un = a.row_unit, min(a.col_unit, b.row_unit), b.col_unit
        dims = (((1,), (0,)), ((), ()))
    elif mode == "nt":
        (m, k), (n, k2) = a.shape, b.shape
        um, uk, un = a.row_unit, min(a.col_unit, b.col_unit), b.row_unit
        dims = (((1,), (1,)), ((), ()))
    else:
        (k, m), (k2, n) = a.shape, b.shape
        um, uk, un = a.col_unit, min(a.row_unit, b.row_unit), b.col_unit
        dims = (((0,), (0,)), ((), ()))
    assert k == k2, (name, a.shape, b.shape)
    for o in list(outs) + list(tiles):
        assert o.shape == (m, n), (name, o.shape, m, n)
        um, un = min(um, o.row_unit), min(un, o.col_unit)
    tm, tn, tk = _tile(um, pm), _tile(un, pn), _tile(uk, pk)
    if mode == "tn" and uk == k:
        size = lambda v: jnp.dtype(v.dtype).itemsize
        need = 2 * k * (tm * size(a) + tn * size(b)) + 2 * tm * tn * sum(size(o) for o in outs)
        if need <= MATMUL_TILE_BYTES:
            tk = k
    assert m % tm == 0 and n % tn == 0 and k % tk == 0, (name, m, n, k, tm, tn, tk)
    gm, gn, gk = m // tm, n // tn, k // tk
    assert not col_sums or gn == 1, (name, gn)

    if mode == "nn":
        a_spec = a.spec(tm, tk, lambda i, j, kk: (i, kk))
        b_spec = b.spec(tk, tn, lambda i, j, kk: (kk, j))
    elif mode == "nt":
        a_spec = a.spec(tm, tk, lambda i, j, kk: (i, kk))
        b_spec = b.spec(tn, tk, lambda i, j, kk: (j, kk))
    else:
        a_spec = a.spec(tk, tm, lambda i, j, kk: (kk, i))
        b_spec = b.spec(tk, tn, lambda i, j, kk: (kk, j))
    tile_specs = [t.spec(tm, tn, lambda i, j, kk: (i, j)) for t in tiles]
    row_specs = [pl.BlockSpec((1, tn), lambda i, j, kk: (0, j)) for _ in rows] + [spec(tm, tn) for _, spec in side]
    rows = list(rows) + [arr for arr, _ in side]
    in_place = [o for o in outs if o.arr is not None]
    alias_specs = [pl.BlockSpec(memory_space=pl.ANY) for _ in in_place]
    out_specs = [o.spec(tm, tn, lambda i, j, kk: (i, j)) for o in outs]
    n_in = 2 + len(tiles) + len(rows)
    aliases = {}
    for o_idx, o in enumerate(outs):
        if o.arr is not None:
            aliases[n_in + in_place.index(o)] = o_idx
    n_t, n_r, n_a, n_o = len(tiles), len(rows), len(in_place) + len(deps), len(outs)
    dep_specs = [pl.BlockSpec(memory_space=pl.ANY) for _ in deps]

    def body(*refs):
        a_ref, b_ref = refs[0], refs[1]
        tile_refs = refs[2:2 + n_t]
        row_refs = refs[2 + n_t:2 + n_t + n_r]
        out_refs = refs[2 + n_t + n_r + n_a:2 + n_t + n_r + n_a + n_o]
        sum_refs = refs[2 + n_t + n_r + n_a + n_o:2 + n_t + n_r + n_a + n_o + col_sums]
        acc_ref = refs[-1] if gk > 1 else None

        def finish(acc):
            extra = [t[...] for t in tile_refs] + [r[...] for r in row_refs]
            res = epilogue(acc, *extra) if epilogue is not None else (acc,)
            for o_ref, r in zip(out_refs, res):
                o_ref[...] = r.astype(o_ref.dtype)
            for s_ref, r in zip(sum_refs, res[n_o:]):
                _accumulate(pl.program_id(0), s_ref, r)

        b_tile = b_ref[...]
        b_tile = b_tile.reshape(-1, b_tile.shape[-1])
        prod = lax.dot_general(a_ref[...].astype(BF16), b_tile.astype(BF16), dims, preferred_element_type=F32)
        if gk == 1:
            finish(prod)
        else:
            kk = pl.program_id(2)

            @pl.when(kk == 0)
            def _():
                acc_ref[...] = prod

            @pl.when(kk > 0)
            def _():
                acc_ref[...] += prod

            @pl.when(kk == gk - 1)
            def _():
                finish(acc_ref[...])

    res = pl.pallas_call(
        body,
        name=name,
        grid=(gm, gn, gk),
        in_specs=[a_spec, b_spec] + tile_specs + row_specs + alias_specs + dep_specs,
        out_specs=out_specs + [pl.BlockSpec((1, tn), lambda i, j, kk: (0, 0))] * col_sums,
        out_shape=[jax.ShapeDtypeStruct(o.full, o.dtype) for o in outs] + [jax.ShapeDtypeStruct((1, n), F32)] * col_sums,
        scratch_shapes=[pltpu.VMEM((tm, tn), F32)] if gk > 1 else [],
        input_output_aliases=aliases,
        compiler_params=_params(*(["arbitrary"] * 3 if col_sums else ["parallel", "parallel", "arbitrary"])),
    )(a.arr, b.arr, *[t.arr for t in tiles], *rows, *[o.arr for o in in_place], *deps)
    return res


def rows_call(name, fn, tiled, vecs, tiled_out, acc_out, tr=512):
    t = tiled[0].shape[0]
    tr = min(tr, t)
    assert t % tr == 0
    n1, n2, n3 = len(tiled), len(vecs), len(tiled_out)

    def body(*refs):
        fn(pl.program_id(0), refs[:n1], refs[n1:n1 + n2], refs[n1 + n2:n1 + n2 + n3], refs[n1 + n2 + n3:])

    return pl.pallas_call(
        body,
        name=name,
        grid=(t // tr,),
        in_specs=[pl.BlockSpec((tr, x.shape[1]), lambda i: (i, 0)) for x in tiled]
        + [pl.BlockSpec(v.shape, lambda i: (0, 0)) for v in vecs],
        out_specs=[pl.BlockSpec((tr, c), lambda i: (i, 0)) for c, _ in tiled_out]
        + [pl.BlockSpec(s, lambda i: (0, 0)) for s, _ in acc_out],
        out_shape=[jax.ShapeDtypeStruct((t, c), d) for c, d in tiled_out] + [jax.ShapeDtypeStruct(s, d) for s, d in acc_out],
        compiler_params=_params("arbitrary" if acc_out else "parallel"),
    )(*tiled, *vecs)


def _accumulate(step, ref, val):
    @pl.when(step == 0)
    def _():
        ref[...] = val

    @pl.when(step > 0)
    def _():
        ref[...] += val


def _ln_stats(s):
    mu = jnp.mean(s, axis=-1, keepdims=True)
    d = s - mu
    var = jnp.mean(d * d, axis=-1, keepdims=True)
    rstd = lax.rsqrt(var + LN_EPS)
    return d * rstd, rstd


def ln_fwd(name, alpha, x_in, m, g, b):
    d = x_in.shape[1]

    def fn(step, tiled, vecs, outs, accs):
        s = alpha * tiled[0][...] + tiled[1][...]
        xhat, _ = _ln_stats(s)
        y = xhat * vecs[0][...] + vecs[1][...]
        outs[0][...] = y
        outs[1][...] = y.astype(BF16)
        outs[2][...] = s

    return rows_call(name, fn, [x_in, m], [g, b], [(d, F32), (d, BF16), (d, F32)], [])


def ln_bwd(name, ca, da, db, s, g):
    d = s.shape[1]

    def fn(step, tiled, vecs, outs, accs):
        dx = ca * tiled[0][...] + tiled[1][...]
        xhat, rstd = _ln_stats(tiled[2][...])
        dxh = dx * vecs[0][...]
        ds = rstd * (dxh - jnp.mean(dxh, axis=-1, keepdims=True) - xhat * jnp.mean(dxh * xhat, axis=-1, keepdims=True))
        outs[0][...] = ds
        outs[1][...] = ds.astype(BF16)
        _accumulate(step, accs[0], jnp.sum(dx * xhat, axis=0, keepdims=True))
        _accumulate(step, accs[1], jnp.sum(dx, axis=0, keepdims=True))

    return rows_call(name, fn, [da, db, s], [g], [(d, F32), (d, BF16)], [((1, d), F32), ((1, d), F32)])


def ple_bwd(name, dx3, gate, e):
    d = dx3.shape[1]

    def fn(step, tiled, vecs, outs, accs):
        dx, gt, ev = tiled[0][...], tiled[1][...], tiled[2][...]
        dpre = dx * ev * gt * (1.0 - gt)
        outs[0][...] = (dx * gt).astype(BF16)
        outs[1][...] = dpre.astype(BF16)
        _accumulate(step, accs[0], jnp.sum(dpre, axis=0, keepdims=True))

    return rows_call(name, fn, [dx3, gate, e], [], [(d, BF16), (d, BF16)], [((1, d), F32)])


def loss_head(name, y, target):
    t, d = y.shape

    def fn(step, tiled, vecs, outs, accs):
        err = tiled[0][...] - tiled[1][...]
        outs[0][...] = err * (1.0 / d)
        part = jnp.sum(jnp.sum(err * err, axis=1, keepdims=True), axis=0, keepdims=True) * (0.5 / d)
        _accumulate(step, accs[0], part)

    return rows_call(name, fn, [y, target], [], [(d, F32)], [((1, 1), F32)])


def _softplus(z):
    return jnp.maximum(z, 0.0) + jnp.log1p(jnp.exp(-jnp.abs(z)))


def _sqrt_and_inverse(z):
    inv = lax.rsqrt(jnp.maximum(z, 1e-30))
    return z * inv, inv


def _gelu(y):
    th = jnp.tanh(GELU_C * (y + GELU_K * (y * y * y)))
    cdf = 0.5 * (1.0 + th)
    return y * cdf, cdf + 0.5 * y * (1.0 - th * th) * (GELU_C * (1.0 + 3.0 * GELU_K * y * y))


def _up(win, k):
    return pltpu.roll(win, win.shape[0] - k, 0)


def _down(win, k):
    return pltpu.roll(win, k, 0)


def _lru_gates(win, row0, cw_ref, cb, wa, ba, wx, bx, sp):
    h = CONV_HALO
    u = (cb + cw_ref[3:4, :] * win[h:] + cw_ref[2:3, :] * _down(win, 1)[h:]
         + cw_ref[1:2, :] * _down(win, 2)[h:] + cw_ref[0:1, :] * _down(win, 3)[h:])
    ub = u.astype(BF16)
    r = jax.nn.sigmoid(jnp.dot(ub, wa, preferred_element_type=F32) + ba)
    ig = jax.nn.sigmoid(jnp.dot(ub, wx, preferred_element_type=F32) + bx)
    log_a = (-LRU_C) * r * sp
    a = jnp.exp(log_a)
    mult = _sqrt_and_inverse(-jnp.tanh(log_a) * (a * a + 1.0))[0]
    first = (row0 + lax.broadcasted_iota(jnp.int32, u.shape, 0)) == 0
    mult = jnp.where(first, 1.0, mult)
    return u, r, ig, a, mult, first


def _block_scan(a, b, reverse):
    n = a.shape[0]
    a, b = a.reshape(n // 8, 8, LRU_BW), b.reshape(n // 8, 8, LRU_BW)
    pos = lax.broadcasted_iota(jnp.int32, a.shape, 1)
    for s in (1, 2, 4):
        keep = (pos >= 8 - s) if reverse else (pos < s)
        by = 8 - s if reverse else s
        b = jnp.where(keep, b, a * pltpu.roll(b, by, 1) + b)
        a = jnp.where(keep, a, a * pltpu.roll(a, by, 1))
    return a.reshape(n, LRU_BW), b.reshape(n, LRU_BW)


def _carry_scan(a_ref, b_ref, out_ref, out_off, t, reverse):
    groups, per_step = t // 8, 8

    def step(j, h):
        for k in range(per_step):
            g = j * per_step + k
            r0 = pl.multiple_of((groups - 1 - g if reverse else g) * 8, 8)
            edge = r0 if reverse else r0 + 7
            h_out = a_ref[pl.ds(edge, 1), :] * h + b_ref[pl.ds(edge, 1), :]
            out_ref[pl.ds(pl.multiple_of(out_off + r0, 8), 8), :] = a_ref[pl.ds(r0, 8), :] * h + b_ref[pl.ds(r0, 8), :]
            h = h_out
        return h

    lax.fori_loop(0, groups // per_step, step, jnp.zeros((1, LRU_BW), F32))


def _lru_in_specs(t, heads):
    blk = lambda i: (0, i)
    return [
        pl.BlockSpec((2, t, LRU_BW), lambda i: (0, 0, i)),
        pl.BlockSpec((CONV_WIDTH, LRU_BW), blk),
        pl.BlockSpec((1, LRU_BW), blk),
        pl.BlockSpec((None, LRU_BW, LRU_BW), lambda i: (i, 0, 0)),
        pl.BlockSpec((1, LRU_BW), blk),
        pl.BlockSpec((None, LRU_BW, LRU_BW), lambda i: (i, 0, 0)),
        pl.BlockSpec((1, LRU_BW), blk),
        pl.BlockSpec((1, LRU_BW), blk),
    ]


def lru_fwd(name, proj, conv_w, conv_b, wa, ba, wx, bx, lam):
    _, t, c = proj.shape
    heads = c // LRU_BW
    rc = min(256, t)

    def body(proj_ref, cw_ref, cb_ref, wa_ref, ba_ref, wx_ref, bx_ref, lam_ref, out_ref, upad, a_s, b_s):
        upad[0:CONV_HALO, :] = jnp.zeros((CONV_HALO, LRU_BW), F32)
        upad[CONV_HALO:, :] = proj_ref[0]
        sp = _softplus(-lam_ref[...])
        cb, ba, bx, wa, wx = cb_ref[...], ba_ref[...], bx_ref[...], wa_ref[...], wx_ref[...]

        def gates(i, carry):
            r0 = pl.multiple_of(i * rc, rc)
            win = upad[pl.ds(r0, rc + CONV_HALO), :]
            u, r, ig, a, mult, _ = _lru_gates(win, r0, cw_ref, cb, wa, ba, wx, bx, sp)
            rows = pl.ds(r0, rc)
            a_s[rows, :], b_s[rows, :] = _block_scan(a, mult * (ig * u), False)
            return carry

        lax.fori_loop(0, t // rc, gates, 0)
        _carry_scan(a_s, b_s, b_s, 0, t, False)

        def gate_out(i, carry):
            r0 = pl.multiple_of(i * rc, rc)
            gy, _ = _gelu(proj_ref[1, pl.ds(r0, rc), :])
            out_ref[pl.ds(r0, rc), :] = (b_s[pl.ds(r0, rc), :] * gy).astype(BF16)
            return carry

        lax.fori_loop(0, t // rc, gate_out, 0)

    return pl.pallas_call(
        body,
        name=name,
        grid=(heads,),
        in_specs=_lru_in_specs(t, heads),
        out_specs=pl.BlockSpec((t, LRU_BW), lambda i: (0, i)),
        out_shape=jax.ShapeDtypeStruct((t, c), BF16),
        scratch_shapes=[pltpu.VMEM((t + CONV_HALO, LRU_BW), F32)] + [pltpu.VMEM((t, LRU_BW), F32)] * 2,
        compiler_params=_params("parallel"),
    )(proj, conv_w, conv_b, wa, ba, wx, bx, lam)


def lru_bwd(name, proj, dhg, conv_w, conv_b, wa, ba, wx, bx, lam, deps=()):
    _, t, c = proj.shape
    heads = c // LRU_BW
    rc = min(256, t)
    h8 = CONV_HALO

    def body(proj_ref, dhg_ref, cw_ref, cb_ref, wa_ref, ba_ref, wx_ref, bx_ref, lam_ref,
             dproj_ref, dcw_ref, dcb_ref, dba_ref, dbx_ref, dlam_ref, dwa_ref, dwx_ref,
             upad, u_s, r_s, ig_s, apad, hpad, g_s, dupad, sa_s, sb_s):
        zeros8 = jnp.zeros((h8, LRU_BW), F32)
        upad[0:h8, :] = zeros8
        upad[h8:, :] = proj_ref[0]
        hpad[0:h8, :] = zeros8
        apad[t:, :] = zeros8
        dupad[t:, :] = zeros8
        lam = lam_ref[...]
        sp = _softplus(-lam)
        cb, ba, bx, wa, wx = cb_ref[...], ba_ref[...], bx_ref[...], wa_ref[...], wx_ref[...]

        def gates(i, carry):
            r0 = pl.multiple_of(i * rc, rc)
            win = upad[pl.ds(r0, rc + h8), :]
            u, r, ig, a, mult, _ = _lru_gates(win, r0, cw_ref, cb, wa, ba, wx, bx, sp)
            u_s[pl.ds(r0, rc), :] = u
            r_s[pl.ds(r0, rc), :] = r
            ig_s[pl.ds(r0, rc), :] = ig
            rows = pl.ds(r0, rc)
            apad[rows, :] = a
            sa_s[rows, :], sb_s[rows, :] = _block_scan(a, mult * (ig * u), False)
            return carry

        lax.fori_loop(0, t // rc, gates, 0)
        _carry_scan(sa_s, sb_s, hpad, h8, t, False)

        def out_gate(i, carry):
            r0 = pl.multiple_of(i * rc, rc)
            gy, dgy = _gelu(proj_ref[1, pl.ds(r0, rc), :])
            dh = dhg_ref[pl.ds(r0, rc), :]
            hh = hpad[pl.ds(pl.multiple_of(r0 + h8, 8), rc), :]
            dproj_ref[1, pl.ds(r0, rc), :] = (dh * hh * dgy).astype(BF16)
            rows = pl.ds(r0, rc)
            a_next = _up(apad[pl.ds(r0, rc + h8), :], 1)[:rc]
            sa_s[rows, :], sb_s[rows, :] = _block_scan(a_next, dh * gy, True)
            return carry

        lax.fori_loop(0, t // rc, out_gate, 0)
        _carry_scan(sa_s, sb_s, g_s, 0, t, True)

        zrow = jnp.zeros((1, LRU_BW), F32)
        zmat = jnp.zeros((LRU_BW, LRU_BW), F32)

        def grads(i, carry):
            dsp, dba, dbx, dwa, dwx = carry
            r0 = pl.multiple_of(i * rc, rc)
            g = g_s[pl.ds(r0, rc), :]
            u, r, ig, a = u_s[pl.ds(r0, rc), :], r_s[pl.ds(r0, rc), :], ig_s[pl.ds(r0, rc), :], apad[pl.ds(r0, rc), :]
            hprev = _down(hpad[pl.ds(r0, rc + h8), :], 1)[h8:]
            first = (r0 + lax.broadcasted_iota(jnp.int32, u.shape, 0)) == 0
            log_a = (-LRU_C) * r * sp
            mult, inv_mult = _sqrt_and_inverse(-jnp.tanh(log_a) * (a * a + 1.0))
            mult = jnp.where(first, 1.0, mult)
            dmult = jnp.where(first, 0.0, g * (ig * u))
            dlog_a = g * hprev * a - dmult * (a * a) * inv_mult
            dr = dlog_a * ((-LRU_C) * sp)
            dpre_r = dr * r * (1.0 - r)
            dpre_i = (g * mult * u) * ig * (1.0 - ig)
            pr, pi, ub = dpre_r.astype(BF16), dpre_i.astype(BF16), u.astype(BF16)
            nt = (((1,), (1,)), ((), ()))
            tn = (((0,), (0,)), ((), ()))
            du = (g * mult * ig + lax.dot_general(pr, wa, nt, preferred_element_type=F32)
                  + lax.dot_general(pi, wx, nt, preferred_element_type=F32))
            dupad[pl.ds(r0, rc), :] = du
            return (dsp + jnp.sum(dlog_a * ((-LRU_C) * r), axis=0, keepdims=True),
                    dba + jnp.sum(dpre_r, axis=0, keepdims=True),
                    dbx + jnp.sum(dpre_i, axis=0, keepdims=True),
                    dwa + lax.dot_general(ub, pr, tn, preferred_element_type=F32),
                    dwx + lax.dot_general(ub, pi, tn, preferred_element_type=F32))

        dsp, dba, dbx, dwa, dwx = lax.fori_loop(0, t // rc, grads, (zrow, zrow, zrow, zmat, zmat))
        dba_ref[...] = dba
        dbx_ref[...] = dbx
        dwa_ref[...] = dwa
        dwx_ref[...] = dwx
        dlam_ref[...] = -dsp * jax.nn.sigmoid(-lam)

        def conv_back(i, carry):
            dcb, d0, d1, d2, d3 = carry
            r0 = pl.multiple_of(i * rc, rc)
            dwin = dupad[pl.ds(r0, rc + h8), :]
            du = dwin[:rc]
            du0 = (cw_ref[3:4, :] * du + cw_ref[2:3, :] * _up(dwin, 1)[:rc]
                   + cw_ref[1:2, :] * _up(dwin, 2)[:rc] + cw_ref[0:1, :] * _up(dwin, 3)[:rc])
            dproj_ref[0, pl.ds(r0, rc), :] = du0.astype(BF16)
            win = upad[pl.ds(r0, rc + h8), :]
            red = lambda v: jnp.sum(v, axis=0, keepdims=True)
            return (dcb + red(du), d0 + red(du * _down(win, 3)[h8:]), d1 + red(du * _down(win, 2)[h8:]),
                    d2 + red(du * _down(win, 1)[h8:]), d3 + red(du * win[h8:]))

        dcb, d0, d1, d2, d3 = lax.fori_loop(0, t // rc, conv_back, (zrow,) * 5)
        dcb_ref[...] = dcb
        dcw_ref[0:1, :] = d0
        dcw_ref[1:2, :] = d1
        dcw_ref[2:3, :] = d2
        dcw_ref[3:4, :] = d3

    blk = lambda i: (0, i)
    vec = jax.ShapeDtypeStruct((1, c), F32)
    mat = jax.ShapeDtypeStruct((heads, LRU_BW, LRU_BW), F32)
    full = lambda: pltpu.VMEM((t, LRU_BW), F32)
    padded = lambda: pltpu.VMEM((t + h8, LRU_BW), F32)
    return pl.pallas_call(
        lambda *refs: body(*refs[len(deps):]),
        name=name,
        grid=(heads,),
        in_specs=[_ANY] * len(deps) + _lru_in_specs(t, heads)[:1] + [pl.BlockSpec((t, LRU_BW), blk)] + _lru_in_specs(t, heads)[1:],
        out_specs=[pl.BlockSpec((2, t, LRU_BW), lambda i: (0, 0, i)), pl.BlockSpec((CONV_WIDTH, LRU_BW), blk)]
        + [pl.BlockSpec((1, LRU_BW), blk)] * 4 + [pl.BlockSpec((None, LRU_BW, LRU_BW), lambda i: (i, 0, 0))] * 2,
        out_shape=[jax.ShapeDtypeStruct((2, t, c), BF16), jax.ShapeDtypeStruct((CONV_WIDTH, c), F32), vec, vec, vec, vec, mat, mat],
        scratch_shapes=[padded(), full(), full(), full(), padded(), padded(), full(), padded()] + [full()] * 2,
        compiler_params=_params("parallel"),
    )(*deps, proj, dhg, conv_w, conv_b, wa, ba, wx, bx, lam)


def _pick_level(g, levels):
    out = levels[-1]
    for k in range(len(levels) - 2, -1, -1):
        out = jnp.where(g == k, levels[k], out)
    return out


def _pool_z(win, g, row0, rc):
    levels, cur = [], win
    for k in range(len(POOL_WINDOWS)):
        cur = cur + _down(cur, 1 << k)
        levels.append(cur[POOL_HALO:])
    tot = _pick_level(g, levels)
    width = jnp.left_shift(2, g)
    row = row0 + lax.broadcasted_iota(jnp.int32, tot.shape, 0)
    cnt = jnp.minimum(row + 1, width).astype(F32)
    return tot / cnt - win[POOL_HALO:], cnt


def _pool_specs(t, gw):
    blk = lambda g: (0, g)
    return [pl.BlockSpec((t, gw), blk), pl.BlockSpec((None, gw, gw), lambda g: (g, 0, 0)),
            pl.BlockSpec((1, gw), blk), pl.BlockSpec((1, gw), blk)]


def pool_fwd(name, u, w_grp, b_grp, scale):
    t, d = u.shape
    gw = d // len(POOL_WINDOWS)
    rc = min(256, t)

    def body(u_ref, wg_ref, bg_ref, sc_ref, out_ref, upad):
        g = pl.program_id(0)
        upad[0:POOL_HALO, :] = jnp.zeros((POOL_HALO, gw), F32)
        upad[POOL_HALO:, :] = u_ref[...]
        wg, bg, sc = wg_ref[...], bg_ref[...], sc_ref[...]

        def chunk(i, carry):
            r0 = pl.multiple_of(i * rc, rc)
            z, _ = _pool_z(upad[pl.ds(r0, rc + POOL_HALO), :], g, r0, rc)
            z2 = jnp.dot(z.astype(BF16), wg, preferred_element_type=F32) + bg
            out_ref[pl.ds(r0, rc), :] = (z2 * sc).astype(BF16)
            return carry

        lax.fori_loop(0, t // rc, chunk, 0)

    return pl.pallas_call(
        body,
        name=name,
        grid=(len(POOL_WINDOWS),),
        in_specs=_pool_specs(t, gw),
        out_specs=pl.BlockSpec((t, gw), lambda g: (0, g)),
        out_shape=jax.ShapeDtypeStruct((t, d), BF16),
        scratch_shapes=[pltpu.VMEM((t + POOL_HALO, gw), F32)],
        compiler_params=_params("parallel"),
    )(u, w_grp, b_grp, scale)


def pool_bwd(name, u, dzs, w_grp, b_grp, scale, deps=()):
    t, d = u.shape
    gw = d // len(POOL_WINDOWS)
    rc = min(256, t)

    def body(u_ref, dzs_ref, wg_ref, bg_ref, sc_ref, du_ref, dwg_ref, dbg_ref, dsc_ref, upad, qpad, dz_s):
        g = pl.program_id(0)
        upad[0:POOL_HALO, :] = jnp.zeros((POOL_HALO, gw), F32)
        upad[POOL_HALO:, :] = u_ref[...]
        qpad[t:, :] = jnp.zeros((POOL_HALO, gw), F32)
        wg, bg, sc = wg_ref[...], bg_ref[...], sc_ref[...]
        zrow = jnp.zeros((1, gw), F32)

        def chunk(i, carry):
            dsc, dbg, dwg = carry
            r0 = pl.multiple_of(i * rc, rc)
            z, cnt = _pool_z(upad[pl.ds(r0, rc + POOL_HALO), :], g, r0, rc)
            zb = z.astype(BF16)
            z2 = jnp.dot(zb, wg, preferred_element_type=F32) + bg
            dzs = dzs_ref[pl.ds(r0, rc), :]
            dz2 = dzs * sc
            d2b = dz2.astype(BF16)
            dz = lax.dot_general(d2b, wg, (((1,), (1,)), ((), ())), preferred_element_type=F32)
            dz_s[pl.ds(r0, rc), :] = dz
            qpad[pl.ds(r0, rc), :] = dz / cnt
            return (dsc + jnp.sum(dzs * z2, axis=0, keepdims=True), dbg + jnp.sum(dz2, axis=0, keepdims=True),
                    dwg + lax.dot_general(zb, d2b, (((0,), (0,)), ((), ())), preferred_element_type=F32))

        dsc, dbg, dwg = lax.fori_loop(0, t // rc, chunk, (zrow, zrow, jnp.zeros((gw, gw), F32)))
        dsc_ref[...] = dsc
        dbg_ref[...] = dbg
        dwg_ref[...] = dwg

        def spread(i, carry):
            r0 = pl.multiple_of(i * rc, rc)
            levels, cur = [], qpad[pl.ds(r0, rc + POOL_HALO), :]
            for k in range(len(POOL_WINDOWS)):
                cur = cur + _up(cur, 1 << k)
                levels.append(cur[:rc])
            du_ref[pl.ds(r0, rc), :] = (_pick_level(g, levels) - dz_s[pl.ds(r0, rc), :]).astype(BF16)
            return carry

        lax.fori_loop(0, t // rc, spread, 0)

    blk = lambda g: (0, g)
    vec = jax.ShapeDtypeStruct((1, d), F32)
    return pl.pallas_call(
        lambda *refs: body(*refs[len(deps):]),
        name=name,
        grid=(len(POOL_WINDOWS),),
        in_specs=[_ANY] * len(deps) + _pool_specs(t, gw)[:1] + [pl.BlockSpec((t, gw), blk)] + _pool_specs(t, gw)[1:],
        out_specs=[pl.BlockSpec((t, gw), blk), pl.BlockSpec((None, gw, gw), lambda g: (g, 0, 0)),
                   pl.BlockSpec((1, gw), blk), pl.BlockSpec((1, gw), blk)],
        out_shape=[jax.ShapeDtypeStruct((t, d), BF16), jax.ShapeDtypeStruct((len(POOL_WINDOWS), gw, gw), F32), vec, vec],
        scratch_shapes=[pltpu.VMEM((t + POOL_HALO, gw), F32), pltpu.VMEM((t + POOL_HALO, gw), F32), pltpu.VMEM((t, gw), F32)],
        compiler_params=_params("parallel"),
    )(*deps, u, dzs, w_grp, b_grp, scale)


def _place():
    return lax.axis_index("x"), lax.axis_index("y"), lax.axis_index("c")


def _other_chips(x, y):
    return [(1 - x, y), (x, 1 - y), (1 - x, 1 - y)]


def _half(c, rows):
    h = rows // 2
    return pl.ds(pl.multiple_of(c * h, 8), h)


_ANY = pl.BlockSpec(memory_space=pl.ANY)


def into_block(name, shards, layer, r, me, dtype):
    c = shards.shape[1]
    tr = _tile(r, 512, 16)
    per = r // tr

    def body(me_ref, s_ref, o_ref):
        o_ref[...] = s_ref[...].astype(o_ref.dtype)

    return pl.pallas_call(
        body,
        name=name,
        grid_spec=pltpu.PrefetchScalarGridSpec(
            num_scalar_prefetch=1,
            grid=(per,),
            in_specs=[pl.BlockSpec((tr, c), lambda i, me_ref: (layer * per + i, 0))],
            out_specs=pl.BlockSpec((None, tr, c), lambda i, me_ref: (me_ref[0], i, 0)),
        ),
        out_shape=jax.ShapeDtypeStruct((N_CHIPS, r, c), dtype),
        compiler_params=_params("parallel"),
    )(me, shards)


_HBM = pl.BlockSpec(memory_space=pltpu.HBM)
_SEM = pl.BlockSpec(memory_space=pltpu.SEMAPHORE)


def _in_hbm(a):
    return pltpu.with_memory_space_constraint(a, pltpu.HBM)


def split_start(name, plan, n_copies, bufs, dep):
    n = len(bufs)

    def body(*refs):
        for cp in plan(refs[:n], refs[n + 1], refs[n + 2]):
            cp.start()
        refs[-1][...] = jnp.zeros_like(refs[-1])

    res = pl.pallas_call(
        body,
        name=name,
        in_specs=[_HBM] * n + [_ANY],
        out_specs=[_SEM, _SEM] + [_HBM] * n + [pl.BlockSpec(memory_space=pltpu.VMEM)],
        out_shape=[pltpu.SemaphoreType.DMA((n_copies,)), pltpu.SemaphoreType.DMA((n_copies,))]
        + [pltpu.HBM(b.shape, b.dtype) for b in bufs] + [jax.ShapeDtypeStruct((8, 128), F32)],
        input_output_aliases={i: 2 + i for i in range(n)},
        compiler_params=pltpu.CompilerParams(has_side_effects=pltpu.SideEffectType.DATAFLOW_SIDE_EFFECTING),
    )(*[_in_hbm(b) for b in bufs], dep)
    return res[0], res[1], list(res[2:2 + n]), res[-1]


def split_wait(name, plan, send_sems, recv_sems, bufs, after):
    n = len(bufs)

    def body(*refs):
        copies = plan(refs[:n], refs[n], refs[n + 1])
        for cp in copies:
            cp.wait_send()
        for cp in copies:
            cp.wait_recv()

    return pl.pallas_call(
        body,
        name=name,
        in_specs=[_HBM] * n + [_SEM, _SEM, _ANY],
        out_specs=[_HBM] * n,
        out_shape=[pltpu.HBM(b.shape, b.dtype) for b in bufs],
        input_output_aliases={i: i for i in range(n)},
        compiler_params=pltpu.CompilerParams(has_side_effects=pltpu.SideEffectType.DATAFLOW_SIDE_EFFECTING),
    )(*bufs, send_sems, recv_sems, after)


def gather_plan(n):
    def plan(bufs, send_sems, recv_sems):
        x, y, c = _place()
        copies = []
        for i in range(n):
            blk = bufs[i].at[2 * x + y, _half(c, bufs[i].shape[1]), :]
            for j, chip in enumerate(_other_chips(x, y)):
                copies.append(pltpu.make_async_remote_copy(
                    src_ref=blk, dst_ref=blk, send_sem=send_sems.at[3 * i + j], recv_sem=recv_sems.at[3 * i + j],
                    device_id=(*chip, c), device_id_type=MESH))
        return copies

    return plan


def forward_plan(n):
    def plan(bufs, send_sems, recv_sems):
        x, y, c = _place()
        copies = []
        for i in range(n):
            for j, (cx, cy) in enumerate(_other_chips(x, y)):
                blk = bufs[i].at[2 * cx + cy, _half(c, bufs[i].shape[1]), :]
                copies.append(pltpu.make_async_remote_copy(
                    src_ref=blk, dst_ref=blk, send_sem=send_sems.at[3 * i + j], recv_sem=recv_sems.at[3 * i + j],
                    device_id=(x, y, 1 - c), device_id_type=MESH))
        return copies

    return plan


def pair_forward(name, bufs):
    n = len(bufs)

    def body(*refs):
        copies = forward_plan(n)(refs[n:2 * n], refs[2 * n], refs[2 * n + 1])
        for cp in copies:
            cp.start()
        for cp in copies:
            cp.wait()

    return pl.pallas_call(
        body,
        name=name,
        in_specs=[_ANY] * n,
        out_specs=[_ANY] * n,
        out_shape=[jax.ShapeDtypeStruct(b.shape, b.dtype) for b in bufs],
        input_output_aliases={i: i for i in range(n)},
        scratch_shapes=[pltpu.SemaphoreType.DMA((3 * n,)), pltpu.SemaphoreType.DMA((3 * n,))],
    )(*bufs)


def all_gather_chips(name, bufs):
    n = len(bufs)

    def body(*refs):
        outs = refs[n:2 * n]
        send_sems, recv_sems = refs[2 * n:]
        x, y, c = _place()
        me, sibling = 2 * x + y, (x, y, 1 - c)
        chips = _other_chips(x, y)

        def copy(i, slot, block, half, to):
            blk = outs[i].at[block, _half(half, outs[i].shape[1]), :]
            return pltpu.make_async_remote_copy(
                src_ref=blk, dst_ref=blk, send_sem=send_sems.at[i * 6 + slot], recv_sem=recv_sems.at[i * 6 + slot],
                device_id=to, device_id_type=MESH)

        first = [copy(i, j, me, c, (*chip, c)) for i in range(n) for j, chip in enumerate(chips)]
        for cp in first:
            cp.start()
        passed = []
        for i in range(n):
            for j, (cx, cy) in enumerate(chips):
                copy(i, j, 2 * cx + cy, c, (x, y, c)).wait_recv()
                fwd = copy(i, 3 + j, 2 * cx + cy, c, sibling)
                fwd.start()
                passed.append(fwd)
        for i in range(n):
            for j, (cx, cy) in enumerate(chips):
                copy(i, 3 + j, 2 * cx + cy, 1 - c, (x, y, c)).wait_recv()
        for cp in first + passed:
            cp.wait_send()

    return pl.pallas_call(
        body,
        name=name,
        in_specs=[_ANY] * n,
        out_specs=[_ANY] * n,
        out_shape=[jax.ShapeDtypeStruct(b.shape, b.dtype) for b in bufs],
        input_output_aliases={i: i for i in range(n)},
        scratch_shapes=[pltpu.SemaphoreType.DMA((6 * n,)), pltpu.SemaphoreType.DMA((6 * n,))],
    )(*bufs)


def pair_plan(n):
    def plan(bufs, send_sems, recv_sems):
        x, y, c = _place()
        return [pltpu.make_async_remote_copy(
            src_ref=bufs[i].at[:, _half(1 - c, bufs[i].shape[1]), :], dst_ref=bufs[n + i], send_sem=send_sems.at[i],
            recv_sem=recv_sems.at[i], device_id=(x, y, 1 - c), device_id_type=MESH) for i in range(n)]

    return plan


def chip_plan(n):
    def plan(bufs, send_sems, recv_sems):
        x, y, c = _place()
        copies = []
        for i in range(n):
            for j, (cx, cy) in enumerate(_other_chips(x, y)):
                copies.append(pltpu.make_async_remote_copy(
                    src_ref=bufs[i].at[2 * cx + cy], dst_ref=bufs[n + i].at[2 * x + y], send_sem=send_sems.at[3 * i + j],
                    recv_sem=recv_sems.at[3 * i + j], device_id=(cx, cy, c), device_id_type=MESH))
        return copies

    return plan


def pair_lands(grads):
    return [jax.ShapeDtypeStruct((g.shape[0], g.shape[1] // 2, g.shape[2]), g.dtype) for g in grads]


def pair_gather(name, bufs, blocked, layers):
    n = len(bufs)
    n_copies = sum(layers)

    def body(*refs):
        outs = refs[n:2 * n]
        send_sems, recv_sems = refs[2 * n:]
        x, y, c = _place()
        copies = []
        for i in range(n):
            buf = outs[i].at[2 * x + y] if blocked[i] else outs[i]
            r = buf.shape[0] // layers[i]
            for l in range(layers[i]):
                mine = buf.at[pl.ds(pl.multiple_of(l * r + c * (r // 2), 8), r // 2), :]
                copies.append(pltpu.make_async_remote_copy(
                    src_ref=mine, dst_ref=mine, send_sem=send_sems.at[len(copies)], recv_sem=recv_sems.at[len(copies)],
                    device_id=(x, y, 1 - c), device_id_type=MESH))
        for cp in copies:
            cp.start()
        for cp in copies:
            cp.wait()

    return pl.pallas_call(
        body,
        name=name,
        in_specs=[_ANY] * n,
        out_specs=[_ANY] * n,
        out_shape=[jax.ShapeDtypeStruct(b.shape, b.dtype) for b in bufs],
        input_output_aliases={i: i for i in range(n)},
        scratch_shapes=[pltpu.SemaphoreType.DMA((n_copies,)), pltpu.SemaphoreType.DMA((n_copies,))],
    )(*bufs)


def pair_sum(name, grad, recv, core, dtype):
    _, r, c = grad.shape
    h = r // 2
    th = _tile(h, 1024, 16)
    per = h // th

    def body(core_ref, g_ref, r_ref, o_ref):
        o_ref[...] = (g_ref[...].astype(F32) + r_ref[...].astype(F32)).astype(o_ref.dtype)

    return pl.pallas_call(
        body,
        name=name,
        grid_spec=pltpu.PrefetchScalarGridSpec(
            num_scalar_prefetch=1,
            grid=(N_CHIPS, per),
            in_specs=[pl.BlockSpec((None, th, c), lambda k, i, core_ref: (k, core_ref[0] * per + i, 0)),
                      pl.BlockSpec((None, th, c), lambda k, i, core_ref: (k, i, 0))],
            out_specs=pl.BlockSpec((None, th, c), lambda k, i, core_ref: (k, i, 0)),
        ),
        out_shape=jax.ShapeDtypeStruct((N_CHIPS, h, c), dtype),
        compiler_params=_params("parallel", "parallel"),
    )(core, grad, recv)


def chip_sum(name, got, parts, place, blocked, into=None, layer=0, n_layers=1):
    _, h, c = parts.shape
    th = _tile(h, 512, 16)
    per = h // th

    def body(place_ref, q0, q1, q2, q3, p_ref, *rest):
        o_ref = rest[-1]
        me = place_ref[0]
        own = p_ref[...].astype(F32)
        v = [jnp.where(me == k, own, q[...].astype(F32)) for k, q in enumerate((q0, q1, q2, q3))]
        o_ref[...] = ((v[0] + v[1]) + v[2]) + v[3]

    def got_spec(k):
        return pl.BlockSpec((None, th, c), lambda i, pr: (jnp.where(pr[0] == k, (k + 1) % N_CHIPS, k), i, 0))

    if blocked:
        out_spec = pl.BlockSpec((None, th, c), lambda i, pr: (pr[0], pr[1] * per + i, 0))
        out_shape = jax.ShapeDtypeStruct((N_CHIPS, 2 * h, c), F32)
    else:
        out_spec = pl.BlockSpec((th, c), lambda i, pr: ((2 * layer + pr[1]) * per + i, 0))
        out_shape = jax.ShapeDtypeStruct((n_layers * 2 * h, c), F32)
    carried = [] if into is None else [into]
    return pl.pallas_call(
        body,
        name=name,
        grid_spec=pltpu.PrefetchScalarGridSpec(
            num_scalar_prefetch=1,
            grid=(per,),
            in_specs=[got_spec(k) for k in range(N_CHIPS)] + [pl.BlockSpec((None, th, c), lambda i, pr: (pr[0], i, 0))]
            + [_ANY] * len(carried),
            out_specs=out_spec,
        ),
        out_shape=out_shape,
        input_output_aliases={6: 0} if carried else {},
        compiler_params=_params("parallel"),
    )(place, got, got, got, got, parts, *carried)


def adamw(name, w, g, m, v):
    r, c = w.shape
    tr = _tile(r, 512, 8)
    c1 = 1.0 - ADAM_B1 ** ADAM_STEP
    c2 = 1.0 - ADAM_B2 ** ADAM_STEP

    def body(w_ref, g_ref, m_ref, v_ref, d_ref, nm_ref, nv_ref, g_out_ref):
        gv = g_ref[...]
        g_out_ref[...] = gv
        nm = ADAM_B1 * m_ref[...] + (1.0 - ADAM_B1) * gv
        nv = ADAM_B2 * v_ref[...] + (1.0 - ADAM_B2) * (gv * gv)
        d_ref[...] = -ADAM_LR * ((nm / c1) / (jnp.sqrt(nv / c2) + ADAM_EPS) + ADAM_WD * w_ref[...])
        nm_ref[...] = nm
        nv_ref[...] = nv

    spec = pl.BlockSpec((tr, c), lambda i: (i, 0))
    return pl.pallas_call(
        body,
        name=name,
        grid=(r // tr,),
        in_specs=[spec] * 4,
        out_specs=[spec] * 4,
        out_shape=[jax.ShapeDtypeStruct((r, c), F32)] * 4,
        compiler_params=_params("parallel"),
    )(w, g, m, v)


def adamw_small(name, ws, gs, ms, vs):
    n = len(ws)
    c1 = 1.0 - ADAM_B1 ** ADAM_STEP
    c2 = 1.0 - ADAM_B2 ** ADAM_STEP

    def body(*refs):
        for i in range(n):
            w_ref, g_ref, m_ref, v_ref = (refs[j * n + i] for j in range(4))
            d_ref, nm_ref, nv_ref = (refs[(4 + j) * n + i] for j in range(3))
            gv = g_ref[...]
            nm = ADAM_B1 * m_ref[...] + (1.0 - ADAM_B1) * gv
            nv = ADAM_B2 * v_ref[...] + (1.0 - ADAM_B2) * (gv * gv)
            d_ref[...] = -ADAM_LR * ((nm / c1) / (jnp.sqrt(nv / c2) + ADAM_EPS) + ADAM_WD * w_ref[...])
            nm_ref[...] = nm
            nv_ref[...] = nv

    whole = pl.BlockSpec(memory_space=pltpu.VMEM)
    res = pl.pallas_call(
        body,
        name=name,
        in_specs=[whole] * (4 * n),
        out_specs=[whole] * (3 * n),
        out_shape=[jax.ShapeDtypeStruct(w.shape, F32) for w in ws] * 3,
        compiler_params=pltpu.CompilerParams(vmem_limit_bytes=VMEM_LIMIT_BYTES),
    )(*ws, *gs, *ms, *vs)
    return res[:n], res[n:2 * n], res[2 * n:]


def _pack(arrays, row_multiple, cols=BLOB_COLS):
    flat = jnp.concatenate([a.reshape(-1).astype(F32) for a in arrays])
    rows = -(-flat.shape[0] // cols)
    rows = -(-rows // row_multiple) * row_multiple
    return jnp.pad(flat, (0, rows * cols - flat.shape[0])).reshape(rows, cols)


def _unpack(blob, shapes):
    flat, out, off = blob.reshape(-1), [], 0
    for s in shapes:
        size = math.prod(s)
        out.append(flat[off:off + size].reshape(s))
        off += size
    return out


def _unpack_rows(blobs, shapes):
    out, off = [], 0
    for s in shapes:
        size = math.prod(s)
        out.append(blobs[:, off:off + size].reshape((blobs.shape[0],) + tuple(s)))
        off += size
    return out


def kernel(x, p, lru_w_in, lru_conv_w, lru_conv_b, lru_wa, lru_ba, lru_wx, lru_bx, lru_lambda, lru_w_out, pool_w_in, pool_w_grp, pool_b_grp, pool_scale, pool_w_out, ln_mix_g, ln_mix_b, mlp_w1, mlp_w2, ln_mlp_g, ln_mlp_b, ple_w, ple_gate_w, ple_gate_b, loss_target, m_lru_w_in, m_lru_conv_w, m_lru_conv_b, m_lru_wa, m_lru_ba, m_lru_wx, m_lru_bx, m_lru_lambda, m_lru_w_out, m_pool_w_in, m_pool_w_grp, m_pool_b_grp, m_pool_scale, m_pool_w_out, m_ln_mix_g, m_ln_mix_b, m_mlp_w1, m_mlp_w2, m_ln_mlp_g, m_ln_mlp_b, m_ple_w, m_ple_gate_w, m_ple_gate_b, v_lru_w_in, v_lru_conv_w, v_lru_conv_b, v_lru_wa, v_lru_ba, v_lru_wx, v_lru_bx, v_lru_lambda, v_lru_w_out, v_pool_w_in, v_pool_w_grp, v_pool_b_grp, v_pool_scale, v_pool_w_out, v_ln_mix_g, v_ln_mix_b, v_mlp_w1, v_mlp_w2, v_ln_mlp_g, v_ln_mlp_b, v_ple_w, v_ple_gate_w, v_ple_gate_b):
    weights = dict(lru_w_in=lru_w_in, lru_conv_w=lru_conv_w, lru_conv_b=lru_conv_b, lru_wa=lru_wa, lru_ba=lru_ba, lru_wx=lru_wx, lru_bx=lru_bx, lru_lambda=lru_lambda, lru_w_out=lru_w_out, pool_w_in=pool_w_in, pool_w_grp=pool_w_grp, pool_b_grp=pool_b_grp, pool_scale=pool_scale, pool_w_out=pool_w_out, ln_mix_g=ln_mix_g, ln_mix_b=ln_mix_b, mlp_w1=mlp_w1, mlp_w2=mlp_w2, ln_mlp_g=ln_mlp_g, ln_mlp_b=ln_mlp_b, ple_w=ple_w, ple_gate_w=ple_gate_w, ple_gate_b=ple_gate_b)
    mom_m = dict(lru_w_in=m_lru_w_in, lru_conv_w=m_lru_conv_w, lru_conv_b=m_lru_conv_b, lru_wa=m_lru_wa, lru_ba=m_lru_ba, lru_wx=m_lru_wx, lru_bx=m_lru_bx, lru_lambda=m_lru_lambda, lru_w_out=m_lru_w_out, pool_w_in=m_pool_w_in, pool_w_grp=m_pool_w_grp, pool_b_grp=m_pool_b_grp, pool_scale=m_pool_scale, pool_w_out=m_pool_w_out, ln_mix_g=m_ln_mix_g, ln_mix_b=m_ln_mix_b, mlp_w1=m_mlp_w1, mlp_w2=m_mlp_w2, ln_mlp_g=m_ln_mlp_g, ln_mlp_b=m_ln_mlp_b, ple_w=m_ple_w, ple_gate_w=m_ple_gate_w, ple_gate_b=m_ple_gate_b)
    mom_v = dict(lru_w_in=v_lru_w_in, lru_conv_w=v_lru_conv_w, lru_conv_b=v_lru_conv_b, lru_wa=v_lru_wa, lru_ba=v_lru_ba, lru_wx=v_lru_wx, lru_bx=v_lru_bx, lru_lambda=v_lru_lambda, lru_w_out=v_lru_w_out, pool_w_in=v_pool_w_in, pool_w_grp=v_pool_w_grp, pool_b_grp=v_pool_b_grp, pool_scale=v_pool_scale, pool_w_out=v_pool_w_out, ln_mix_g=v_ln_mix_g, ln_mix_b=v_ln_mix_b, mlp_w1=v_mlp_w1, mlp_w2=v_mlp_w2, ln_mlp_g=v_ln_mlp_g, ln_mlp_b=v_ln_mlp_b, ple_w=v_ple_w, ple_gate_w=v_ple_gate_w, ple_gate_b=v_ple_gate_b)
    names = list(weights)

    depth, d = ln_mix_g.shape
    t = x.shape[1]
    n_a, n_b = lru_w_in.shape[0], pool_w_in.shape[0]
    d_rnn = lru_w_out.shape[1] * N_CHIPS
    heads = d_rnn // LRU_BW
    d_ff = mlp_w1.shape[2] * N_CHIPS
    ple_dim = ple_w.shape[1]
    n_grp = len(POOL_WINDOWS)
    gw = d // n_grp
    alpha = (2 * depth) ** 0.25
    chip = 2 * lax.axis_index("x") + lax.axis_index("y")
    place = jnp.stack([chip, lax.axis_index("c")]).astype(jnp.int32)

    x2d = x.reshape(t, d)
    target = loss_target.reshape(t, d)
    p3 = p.reshape(depth, t, ple_dim)

    big = ["lru_w_in", "lru_w_out", "pool_w_in", "pool_w_out", "mlp_w1", "mlp_w2", "ple_w", "ple_gate_w", "pool_w_grp"]
    flat2 = lambda a: a.reshape(-1, a.shape[-1])
    small_sharded = ["lru_conv_w", "pool_b_grp", "pool_scale"]
    small_blob = _pack([weights[k] for k in small_sharded], 16, cols=256)
    every_layer = ("mlp_w1", "mlp_w2", "ple_w", "ple_gate_w")

    def layer_keys(i):
        return (["lru_w_in", "lru_w_out"] if i % 2 == 0 else ["pool_w_in", "pool_w_out", "pool_w_grp"]) + list(every_layer)

    def stage(k, i):
        w = weights[k]
        return into_block(f"stage_l{i}_{k}", flat2(w), i if k in every_layer else i // 2, math.prod(w.shape[1:-1]),
                          place[:1], BF16)

    staged = [[stage(k, i) for k in layer_keys(i)] for i in range(depth)]
    first = all_gather_chips("gather_l0", staged[0][:1] + [into_block("stage_small", small_blob, 0, small_blob.shape[0], place[:1], F32)])
    wg = {(layer_keys(0)[0], 0): first[0]}

    tokens = []

    def take_tokens():
        deps = tuple(tokens)
        tokens.clear()
        return deps

    def mm(*args, **kwargs):
        return matmul(*args, deps=take_tokens(), **kwargs)

    def start_gather(tag, bufs, dep):
        plan = gather_plan(len(bufs))
        flight = (plan,) + split_start(f"gather_{tag}_start", plan, 3 * len(bufs), bufs, dep)
        tokens.append(flight[-1])
        return flight

    def land_gather(tag, flight, keys, layer, after, wait_for=True):
        plan, send_sems, recv_sems, bufs, _ = flight
        landed = split_wait(f"gather_{tag}_wait", plan, send_sems, recv_sems, bufs, after)
        if wait_for:
            wg.update(zip([(k, layer) for k in keys], pair_forward(f"gather_{tag}_forward", landed)))
            return None
        plan = forward_plan(len(landed))
        forwarding = (tag, plan) + split_start(f"gather_{tag}_forward_start", plan, 3 * len(landed), landed, after)
        tokens.append(forwarding[-1])
        return forwarding

    def finish_forward(forwarding, keys, layer, after):
        tag, plan, send_sems, recv_sems, bufs, _ = forwarding
        wg.update(zip([(k, layer) for k in keys], split_wait(f"gather_{tag}_forward_wait", plan, send_sems, recv_sems, bufs, after)))

    conv_w_sh, b_grp_sh, scale_sh = _unpack_rows(first[-1].reshape(N_CHIPS, -1), [weights[k].shape for k in small_sharded])
    conv_w_full = jnp.moveaxis(conv_w_sh, 0, 2).reshape(n_a, CONV_WIDTH, d_rnn)
    b_grp_full = jnp.moveaxis(b_grp_sh, 0, 1).reshape(n_b, 1, d)
    scale_full = jnp.moveaxis(scale_sh, 0, 1).reshape(n_b, 1, d)
    rows_grp = gw // N_CHIPS
    w_grp_full = lambda i: jnp.moveaxis(wg["pool_w_grp", i].reshape(N_CHIPS, n_grp, rows_grp, gw), 0, 1).reshape(n_grp, gw, gw)
    wa_bf, wx_bf = lru_wa.astype(BF16), lru_wx.astype(BF16)
    row = lambda a, i: a[i].reshape(1, -1)

    def ln_after(acc, x_in, g, b):
        s = alpha * x_in + acc
        y = _ln_stats(s)[0] * g + b
        return y, y, s

    ln_outs = [plain(shape=(t, d), dtype=F32), plain(shape=(t, d), dtype=BF16), plain(shape=(t, d), dtype=F32)]
    saved = []
    cur, cur_bf = x2d, x2d
    for i in range(depth):
        slot = i // 2
        sv = dict(x_bf=cur_bf)
        if i == 0:
            flight = start_gather("l0_rest", staged[0][1:], first[0])
        elif i + 1 < depth:
            flight = start_gather(f"l{i + 1}", staged[i + 1], cur)
        if i % 2 == 0:
            (proj,) = mm(f"l{i}_lru_in", plain(cur_bf), colsplit(wg["lru_w_in", i], 0, d), "nn",
                         [colsplit(None, 0, t, n=2, full=(2, t, d_rnn), dtype=F32)])
            hg = lru_fwd(f"l{i}_lru", proj, conv_w_full[slot], row(lru_conv_b, slot), wa_bf[slot], row(lru_ba, slot),
                         wx_bf[slot], row(lru_bx, slot), row(lru_lambda, slot))
            if i == 0:
                land_gather("l0_rest", flight, layer_keys(0)[1:], 0, hg)
                flight = start_gather("l1", staged[1], hg)
            x1, x1_bf, s1 = mm(f"l{i}_lru_out", plain(hg), rowsplit_whole(wg["lru_w_out", i]), "nn", ln_outs, pk=2048,
                               epilogue=ln_after, tiles=[plain(cur)], rows=[row(ln_mix_g, i), row(ln_mix_b, i)])
            sv.update(proj=proj, act=hg)
        else:
            (u,) = mm(f"l{i}_pool_in", plain(cur_bf), rowsplit_whole(wg["pool_w_in", i]), "nn",
                          [plain(shape=(t, d), dtype=F32)])
            zs = pool_fwd(f"l{i}_pool", u, w_grp_full(i), b_grp_full[slot], scale_full[slot])
            x1, x1_bf, s1 = mm(f"l{i}_pool_out", plain(zs), rowsplit_whole(wg["pool_w_out", i]), "nn", ln_outs,
                               epilogue=ln_after, tiles=[plain(cur)], rows=[row(ln_mix_g, i), row(ln_mix_b, i)])
            sv.update(u=u, act=zs)

        def relu2(acc):
            hr = jnp.maximum(acc, 0.0)
            return hr, hr * hr

        hr, hh = mm(f"l{i}_mlp_up", plain(x1_bf), colsplit(wg["mlp_w1", i], 0, d), "nn",
                    [plain(shape=(t, d_ff), dtype=BF16), plain(shape=(t, d_ff), dtype=BF16)], epilogue=relu2, pm=2048)
        (mlp,) = mm(f"l{i}_mlp_down", plain(hh), rowsplit_whole(wg["mlp_w2", i]), "nn",
                    [plain(shape=(t, d), dtype=F32)], pk=d_ff)
        x2, x2_bf, s2 = ln_fwd(f"l{i}_ln_mlp", alpha, x1, mlp, row(ln_mlp_g, i), row(ln_mlp_b, i))
        if i + 1 < depth:
            forwarding = land_gather(f"l{i + 1}", flight, layer_keys(i + 1), i + 1, x2_bf, wait_for=False)

        def ple_out(acc, x2_t, gb, p_t, pw):
            e_t = jnp.concatenate([jnp.dot(p_t.astype(BF16), pw[k], preferred_element_type=F32) for k in range(N_CHIPS)], axis=1)
            gate = jax.nn.sigmoid(acc + gb)
            x3 = x2_t + e_t * gate
            return x3, x3, gate, e_t

        cur, cur_bf, gate, e = mm(
            f"l{i}_ple_gate", plain(x2_bf), rowsplit_whole(wg["ple_gate_w", i]), "nn",
            [plain(shape=(t, d), dtype=F32), plain(shape=(t, d), dtype=BF16), plain(shape=(t, d), dtype=F32), plain(shape=(t, d), dtype=F32)],
            epilogue=ple_out, tiles=[plain(x2)], rows=[row(ple_gate_b, i)], pm=512,
            side=[(p3[i], lambda tm, tn: pl.BlockSpec((tm, ple_dim), lambda r, c, kk: (r, 0))),
                  (wg["ple_w", i], lambda tm, tn: pl.BlockSpec(wg["ple_w", i].shape, lambda r, c, kk: (0, 0, 0)))])
        sv.update(s1=s1, x1_bf=x1_bf, hr=hr, hh=hh, s2=s2, x2_bf=x2_bf, gate=gate, e=e)
        saved.append(sv)
        if i + 1 < depth:
            finish_forward(forwarding, layer_keys(i + 1), i + 1, cur)

    dy, loss_part = loss_head("loss", cur, target)
    loss = lax.psum(loss_part.reshape(()), ("x", "y", "c"))

    part = {}
    sums = {}

    def grad_view(key, split):
        w = weights[key]
        return split(None, 0, w.shape[1], full=(N_CHIPS, w.shape[1], w.shape[2]), dtype=BF16)

    def group_start(tag, items, dep):
        srcs = [part[it] for it in items]
        plan = pair_plan(len(srcs))
        lands = [lax.empty(s.shape, s.dtype) for s in pair_lands(srcs)]
        flight = (tag, items, plan) + split_start(f"grads_{tag}_pair_start", plan, len(srcs), srcs + lands, dep)
        tokens.append(flight[-1])
        return flight

    def group_mid(flight, after):
        tag, items, plan, send_sems, recv_sems, bufs, _ = flight
        bufs = split_wait(f"grads_{tag}_pair_wait", plan, send_sems, recv_sems, bufs, after)
        n = len(items)
        parts = [pair_sum(f"grads_{tag}_pair_sum_{j}", bufs[j], bufs[n + j], place[1:], F32 if it[0] == "blob" else BF16)
                 for j, it in enumerate(items)]
        plan = chip_plan(n)
        flight = (tag, items, plan) + split_start(f"grads_{tag}_chip_start", plan, 3 * n,
                                                  parts + [lax.empty(q.shape, q.dtype) for q in parts], after)
        tokens.append(flight[-1])
        return flight

    def group_end(flight, after):
        tag, items, plan, send_sems, recv_sems, bufs, _ = flight
        bufs = split_wait(f"grads_{tag}_chip_wait", plan, send_sems, recv_sems, bufs, after)
        n = len(items)
        for j, (k, layer) in enumerate(items):
            if k == "blob":
                sums[k] = chip_sum(f"grads_{tag}_chip_sum_{j}", bufs[n + j], bufs[j], place, True)
            else:
                sums[k] = chip_sum(f"grads_{tag}_chip_sum_{j}", bufs[n + j], bufs[j], place, False, into=sums.get(k),
                                   layer=layer if k in every_layer else layer // 2, n_layers=weights[k].shape[0])

    big_w = [k for k in big if k != "pool_w_grp"]
    small_keys = [k for k in names if k not in big_w]

    def ln_before(ca):
        def back(acc, upstream, s, g):
            dx = ca * upstream + acc
            xhat, rstd = _ln_stats(s)
            dxh = dx * g
            ds = rstd * (dxh - jnp.mean(dxh, axis=-1, keepdims=True) - xhat * jnp.mean(dxh * xhat, axis=-1, keepdims=True))
            return ds, ds, jnp.sum(dx * xhat, axis=0, keepdims=True), jnp.sum(dx, axis=0, keepdims=True)

        return back

    ds_outs = [plain(shape=(t, d), dtype=F32), plain(shape=(t, d), dtype=BF16)]
    small = {k: [None] * weights[k].shape[0] for k in names if k not in big or k == "pool_w_grp"}
    dcur = dy
    mlp_pair = mlp_chip = mix_pair = mix_chip = None
    for i in reversed(range(depth)):
        slot = i // 2
        sv = saved[i]
        de, dpre, dgb = ple_bwd(f"l{i}_ple_bwd", dcur, sv["gate"], sv["e"])
        small["ple_gate_b"][i] = dgb
        (part["ple_w", i],) = mm(f"l{i}_d_ple_w", plain(p3[i]), plain(de), "tn", [grad_view("ple_w", colsplit)])
        (part["ple_gate_w", i],) = mm(f"l{i}_d_ple_gate_w", plain(sv["x2_bf"]), plain(dpre), "tn",
                                          [grad_view("ple_gate_w", rowsplit)])
        ds2, ds2_bf, dg, db = mm(f"l{i}_dx2", plain(dpre), rowsplit_whole(wg["ple_gate_w", i]), "nt", ds_outs, col_sums=2, pm=512,
                                 epilogue=ln_before(1.0), tiles=[plain(dcur), plain(sv["s2"])], rows=[row(ln_mlp_g, i)])
        small["ln_mlp_g"][i], small["ln_mlp_b"][i] = dg, db
        (part["mlp_w2", i],) = mm(f"l{i}_d_mlp_w2", plain(sv["hh"]), plain(ds2_bf), "tn", [grad_view("mlp_w2", rowsplit)])
        (dhpre,) = mm(f"l{i}_dh", plain(ds2_bf), rowsplit(wg["mlp_w2", i], 0, d_ff // N_CHIPS), "nt",
                      [plain(shape=(t, d_ff), dtype=BF16)], tiles=[plain(sv["hr"])], pm=2048,
                      epilogue=lambda acc, hr_t: (acc * (2.0 * hr_t.astype(F32)),))
        (part["mlp_w1", i],) = mm(f"l{i}_d_mlp_w1", plain(sv["x1_bf"]), plain(dhpre), "tn", [grad_view("mlp_w1", colsplit)])
        if mlp_chip is not None:
            group_end(mlp_chip, dhpre)
        if mix_pair is not None:
            mix_chip = group_mid(mix_pair, dhpre)
        mlp_pair = group_start(f"l{i}_mlp", [(k, i) for k in every_layer], dhpre)
        (dx1b,) = mm(f"l{i}_dx1", plain(dhpre), colsplit(wg["mlp_w1", i], 0, d), "nt", [plain(shape=(t, d), dtype=F32)],
                     pm=2048)
        ds1, ds1_bf, dg, db = ln_bwd(f"l{i}_ln_mix_bwd", alpha, ds2, dx1b, sv["s1"], row(ln_mix_g, i))
        small["ln_mix_g"][i], small["ln_mix_b"][i] = dg, db
        residual = lambda acc, ds_t: (alpha * ds_t + acc,)
        if i % 2 == 0:
            (part["lru_w_out", i],) = mm(f"l{i}_d_lru_out", plain(sv["act"]), plain(ds1_bf), "tn",
                                             [grad_view("lru_w_out", rowsplit)])
            (dhg,) = mm(f"l{i}_dhg", plain(ds1_bf), rowsplit_whole(wg["lru_w_out", i]), "nt",
                            [plain(shape=(t, d_rnn), dtype=F32)], pn=2048)
            mlp_chip = group_mid(mlp_pair, dhg)
            dproj, dcw, dcb, dba, dbx, dlam, dwa, dwx = lru_bwd(
                f"l{i}_lru_bwd", sv["proj"], dhg, conv_w_full[slot], row(lru_conv_b, slot), wa_bf[slot], row(lru_ba, slot),
                wx_bf[slot], row(lru_bx, slot), row(lru_lambda, slot), deps=take_tokens())
            for key, val in (("lru_conv_w", dcw), ("lru_conv_b", dcb), ("lru_ba", dba), ("lru_bx", dbx),
                             ("lru_lambda", dlam), ("lru_wa", dwa), ("lru_wx", dwx)):
                small[key][slot] = val
            dproj_v = colsplit(dproj, 0, t, n=2)
            (part["lru_w_in", i],) = mm(f"l{i}_d_lru_in", plain(sv["x_bf"]), dproj_v, "tn", [grad_view("lru_w_in", colsplit)])
            (dcur,) = mm(f"l{i}_dx", dproj_v, colsplit(wg["lru_w_in", i], 0, d), "nt",
                             [plain(shape=(t, d), dtype=F32)], epilogue=residual, tiles=[plain(ds1)])
        else:
            (part["pool_w_out", i],) = mm(f"l{i}_d_pool_out", plain(sv["act"]), plain(ds1_bf), "tn",
                                              [grad_view("pool_w_out", rowsplit)])
            (dzs,) = mm(f"l{i}_dzs", plain(ds1_bf), rowsplit_whole(wg["pool_w_out", i]), "nt",
                            [plain(shape=(t, d), dtype=F32)])
            mlp_chip = group_mid(mlp_pair, dzs)
            du, dwg, dbg, dsc = pool_bwd(f"l{i}_pool_bwd", sv["u"], dzs, w_grp_full(i), b_grp_full[slot], scale_full[slot],
                                         deps=take_tokens())
            small["pool_w_grp"][slot], small["pool_b_grp"][slot], small["pool_scale"][slot] = dwg, dbg, dsc
            (part["pool_w_in", i],) = mm(f"l{i}_d_pool_in", plain(sv["x_bf"]), plain(du), "tn", [grad_view("pool_w_in", rowsplit)])
            (dcur,) = mm(f"l{i}_dx", plain(du), rowsplit_whole(wg["pool_w_in", i]), "nt",
                             [plain(shape=(t, d), dtype=F32)], epilogue=residual, tiles=[plain(ds1)])
        if mix_chip is not None:
            group_end(mix_chip, dcur)
        mixer = [(k, i) for k in layer_keys(i) if k not in every_layer and k != "pool_w_grp"]
        if i == 0:
            small_full = [jnp.stack(small[k]).reshape((weights[k].shape[0],) + tuple(
                s * (N_CHIPS if ax in _sharded_axis(k) else 1) for ax, s in enumerate(weights[k].shape[1:], 1))) for k in small_keys]
            blob = _pack(small_full, 64)
            part["blob", 0] = blob.reshape(N_CHIPS, blob.shape[0] // N_CHIPS, BLOB_COLS)
            mixer.append(("blob", 0))
        mix_pair = group_start(f"l{i}_mix", mixer, dcur)
    grad_x = dcur.reshape(x.shape)
    mix_chip = group_mid(mix_pair, dcur)
    group_end(mlp_chip, mix_chip[-1])

    grads, delta, new_m, new_v = {}, {}, {}, {}

    def finish(tag, keys):
        full = pair_gather(f"grads_pair_gather_{tag}", [sums[k] for k in keys], [k == "blob" for k in keys],
                           [1 if k == "blob" else weights[k].shape[0] for k in keys])
        for k, g in zip(keys, full):
            if k == "blob":
                grads[k] = g
                continue
            dl, nm, nv, g = adamw("adamw_" + k, flat2(weights[k]), g, flat2(mom_m[k]), flat2(mom_v[k]))
            delta[k], new_m[k], new_v[k], grads[k] = (a.reshape(weights[k].shape) for a in (dl, nm, nv, g))

    last = [k for k, _ in mix_pair[1]]
    finish("early", [k for k in big_w if k not in last])
    group_end(mix_chip, delta["mlp_w1"])
    finish("last", last)
    (blob_all,) = all_gather_chips("gather_small_grads", [grads.pop("blob")])
    small_grads = dict(zip(small_keys, _unpack(blob_all.reshape(blob.shape), [a.shape for a in small_full])))
    for k in small_keys:
        for ax in _sharded_axis(k):
            n = weights[k].shape[ax]
            small_grads[k] = lax.dynamic_slice_in_dim(small_grads[k], chip * n, n, axis=ax)
    grads.update(small_grads)
    dl, nm, nv = adamw_small("adamw_small", *[[flat2(src[k]) for k in small_keys] for src in (weights, grads, mom_m, mom_v)])
    for out, res in ((delta, dl), (new_m, nm), (new_v, nv)):
        out.update({k: a.reshape(weights[k].shape) for k, a in zip(small_keys, res)})

    return (loss, grad_x, *[grads[k] for k in names], *[delta[k] for k in names],
            *[new_m[k] for k in names], *[new_v[k] for k in names])


def _sharded_axis(key):
    return {"lru_conv_w": (2,), "pool_w_grp": (2,), "pool_b_grp": (1,), "pool_scale": (1,)}.get(key, ())
```

```python
import functools
import math

import jax
import jax.numpy as jnp
from jax import lax
from jax.experimental import pallas as pl
from jax.experimental.pallas import tpu as pltpu

F32 = jnp.float32
BF16 = jnp.bfloat16

N_CHIPS = 4
LRU_BW = 128
LRU_C = 8.0
CONV_WIDTH = 4
POOL_WINDOWS = (2, 4, 8, 16)
POOL_HALO = 16
CONV_HALO = 8
LN_EPS = 1e-5
ADAM_LR = 0.001
ADAM_B1 = 0.9
ADAM_B2 = 0.999
ADAM_EPS = 1e-08
ADAM_WD = 0.01
ADAM_STEP = 10
GELU_C = math.sqrt(2.0 / math.pi)
GELU_K = 0.044715
VMEM_LIMIT_BYTES = 56 * 1024 * 1024
MATMUL_TILE_BYTES = 44 * 1024 * 1024
MESH = pl.DeviceIdType.MESH
BLOB_COLS = 1024


def _params(*sem):
    return pltpu.CompilerParams(dimension_semantics=tuple(sem), vmem_limit_bytes=VMEM_LIMIT_BYTES)


def _tile(unit, pref, align=128):
    if unit <= pref:
        return unit
    for d in range(2, unit + 1):
        if unit % d == 0 and unit // d <= pref and (unit // d) % align == 0:
            return unit // d
    raise ValueError((unit, pref, align))


class View:
    def __init__(self, arr, shape, row_unit, col_unit, block_fn, full=None, dtype=None):
        self.arr, self.shape, self.row_unit, self.col_unit, self.block_fn = arr, shape, row_unit, col_unit, block_fn
        self.full = full if full is not None else arr.shape
        self.dtype = dtype if dtype is not None else arr.dtype

    def spec(self, tr, tc, f):
        block, idx = self.block_fn(tr, tc)
        return pl.BlockSpec(block, lambda *g: idx(*f(*g)))


def plain(arr=None, shape=None, dtype=None):
    shape = arr.shape if arr is not None else shape
    return View(arr, shape, shape[0], shape[1], lambda tr, tc: ((tr, tc), lambda rt, ct: (rt, ct)), full=shape, dtype=dtype)


def colsplit(arr, layer, rows, n=N_CHIPS, full=None, dtype=None):
    full = arr.shape if arr is not None else full
    c = full[2]

    def block_fn(tr, tc):
        assert rows % tr == 0 and c % tc == 0, (rows, tr, c, tc)
        per, rpl = c // tc, rows // tr
        return (None, tr, tc), lambda rt, ct: (ct // per, layer * rpl + rt, ct % per)

    return View(arr, (rows, n * c), rows, c, block_fn, full=full, dtype=dtype)


def rowsplit(arr, layer, rows, n=N_CHIPS, full=None, dtype=None):
    full = arr.shape if arr is not None else full
    c = full[2]

    def block_fn(tr, tc):
        assert rows % tr == 0 and c % tc == 0, (rows, tr, c, tc)
        per = rows // tr
        return (None, tr, tc), lambda rt, ct: (rt // per, layer * per + rt % per, ct)

    return View(arr, (n * rows, c), rows, c, block_fn, full=full, dtype=dtype)


def rowsplit_whole(arr):
    n, rows, c = arr.shape

    def block_fn(tr, tc):
        assert tr == n * rows and c % tc == 0, (tr, n, rows, c, tc)
        return (n, rows, tc), lambda rt, ct: (0, 0, ct)

    return View(arr, (n * rows, c), n * rows, c, block_fn)


def matmul(name, a, b, mode, outs, epilogue=None, tiles=(), rows=(), side=(), deps=(), col_sums=0, pm=1024, pn=1024, pk=1024):
    if mode == "nn":
        (m, k), (k2, n) = a.shape, b.shape
        um, uk, un = a.row_unit, min(a.col_unit, b.row_unit), b.col_unit
        dims = (((1,), (0,)), ((), ()))
    elif mode == "nt":
        (m, k), (n, k2) = a.shape, b.shape
        um, uk, un = a.row_unit, min(a.col_unit, b.col_unit), b.row_unit
        dims = (((1,), (1,)), ((), ()))
    else:
        (k, m), (k2, n) = a.shape, b.shape
        um, uk, un = a.col_unit, min(a.row_unit, b.row_unit), b.col_unit
        dims = (((0,), (0,)), ((), ()))
    assert k == k2, (name, a.shape, b.shape)
    for o in list(outs) + list(tiles):
        assert o.shape == (m, n), (name, o.shape, m, n)
        um, un = min(um, o.row_unit), min(un, o.col_unit)
    tm, tn, tk = _tile(um, pm), _tile(un, pn), _tile(uk, pk)
    if mode == "tn" and uk == k:
        size = lambda v: jnp.dtype(v.dtype).itemsize
        need = 2 * k * (tm * size(a) + tn * size(b)) + 2 * tm * tn * sum(size(o) for o in outs)
        if need <= MATMUL_TILE_BYTES:
            tk = k
    assert m % tm == 0 and n % tn == 0 and k % tk == 0, (name, m, n, k, tm, tn, tk)
    gm, gn, gk = m // tm, n // tn, k // tk
    assert not col_sums or gn == 1, (name, gn)

    if mode == "nn":
        a_spec = a.spec(tm, tk, lambda i, j, kk: (i, kk))
        b_spec = b.spec(tk, tn, lambda i, j, kk: (kk, j))
    elif mode == "nt":
        a_spec = a.spec(tm, tk, lambda i, j, kk: (i, kk))
        b_spec = b.spec(tn, tk, lambda i, j, kk: (j, kk))
    else:
        a_spec = a.spec(tk, tm, lambda i, j, kk: (kk, i))
        b_spec = b.spec(tk, tn, lambda i, j, kk: (kk, j))
    tile_specs = [t.spec(tm, tn, lambda i, j, kk: (i, j)) for t in tiles]
    row_specs = [pl.BlockSpec((1, tn), lambda i, j, kk: (0, j)) for _ in rows] + [spec(tm, tn) for _, spec in side]
    rows = list(rows) + [arr for arr, _ in side]
    in_place = [o for o in outs if o.arr is not None]
    alias_specs = [pl.BlockSpec(memory_space=pl.ANY) for _ in in_place]
    out_specs = [o.spec(tm, tn, lambda i, j, kk: (i, j)) for o in outs]
    n_in = 2 + len(tiles) + len(rows)
    aliases = {}
    for o_idx, o in enumerate(outs):
        if o.arr is not None:
            aliases[n_in + in_place.index(o)] = o_idx
    n_t, n_r, n_a, n_o = len(tiles), len(rows), len(in_place) + len(deps), len(outs)
    dep_specs = [pl.BlockSpec(memory_space=pl.ANY) for _ in deps]

    def body(*refs):
        a_ref, b_ref = refs[0], refs[1]
        tile_refs = refs[2:2 + n_t]
        row_refs = refs[2 + n_t:2 + n_t + n_r]
        out_refs = refs[2 + n_t + n_r + n_a:2 + n_t + n_r + n_a + n_o]
        sum_refs = refs[2 + n_t + n_r + n_a + n_o:2 + n_t + n_r + n_a + n_o + col_sums]
        acc_ref = refs[-1] if gk > 1 else None

        def finish(acc):
            extra = [t[...] for t in tile_refs] + [r[...] for r in row_refs]
            res = epilogue(acc, *extra) if epilogue is not None else (acc,)
            for o_ref, r in zip(out_refs, res):
                o_ref[...] = r.astype(o_ref.dtype)
            for s_ref, r in zip(sum_refs, res[n_o:]):
                _accumulate(pl.program_id(0), s_ref, r)

        b_tile = b_ref[...]
        b_tile = b_tile.reshape(-1, b_tile.shape[-1])
        prod = lax.dot_general(a_ref[...].astype(BF16), b_tile.astype(BF16), dims, preferred_element_type=F32)
        if gk == 1:
            finish(prod)
        else:
            kk = pl.program_id(2)

            @pl.when(kk == 0)
            def _():
                acc_ref[...] = prod

            @pl.when(kk > 0)
            def _():
                acc_ref[...] += prod

            @pl.when(kk == gk - 1)
            def _():
                finish(acc_ref[...])

    res = pl.pallas_call(
        body,
        name=name,
        grid=(gm, gn, gk),
        in_specs=[a_spec, b_spec] + tile_specs + row_specs + alias_specs + dep_specs,
        out_specs=out_specs + [pl.BlockSpec((1, tn), lambda i, j, kk: (0, 0))] * col_sums,
        out_shape=[jax.ShapeDtypeStruct(o.full, o.dtype) for o in outs] + [jax.ShapeDtypeStruct((1, n), F32)] * col_sums,
        scratch_shapes=[pltpu.VMEM((tm, tn), F32)] if gk > 1 else [],
        input_output_aliases=aliases,
        compiler_params=_params(*(["arbitrary"] * 3 if col_sums else ["parallel", "parallel", "arbitrary"])),
    )(a.arr, b.arr, *[t.arr for t in tiles], *rows, *[o.arr for o in in_place], *deps)
    return res


def rows_call(name, fn, tiled, vecs, tiled_out, acc_out, tr=512):
    t = tiled[0].shape[0]
    tr = min(tr, t)
    assert t % tr == 0
    n1, n2, n3 = len(tiled), len(vecs), len(tiled_out)

    def body(*refs):
        fn(pl.program_id(0), refs[:n1], refs[n1:n1 + n2], refs[n1 + n2:n1 + n2 + n3], refs[n1 + n2 + n3:])

    return pl.pallas_call(
        body,
        name=name,
        grid=(t // tr,),
        in_specs=[pl.BlockSpec((tr, x.shape[1]), lambda i: (i, 0)) for x in tiled]
        + [pl.BlockSpec(v.shape, lambda i: (0, 0)) for v in vecs],
        out_specs=[pl.BlockSpec((tr, c), lambda i: (i, 0)) for c, _ in tiled_out]
        + [pl.BlockSpec(s, lambda i: (0, 0)) for s, _ in acc_out],
        out_shape=[jax.ShapeDtypeStruct((t, c), d) for c, d in tiled_out] + [jax.ShapeDtypeStruct(s, d) for s, d in acc_out],
        compiler_params=_params("arbitrary" if acc_out else "parallel"),
    )(*tiled, *vecs)


def _accumulate(step, ref, val):
    @pl.when(step == 0)
    def _():
        ref[...] = val

    @pl.when(step > 0)
    def _():
        ref[...] += val


def _ln_stats(s):
    mu = jnp.mean(s, axis=-1, keepdims=True)
    d = s - mu
    var = jnp.mean(d * d, axis=-1, keepdims=True)
    rstd = lax.rsqrt(var + LN_EPS)
    return d * rstd, rstd


def ln_fwd(name, alpha, x_in, m, g, b):
    d = x_in.shape[1]

    def fn(step, tiled, vecs, outs, accs):
        s = alpha * tiled[0][...] + tiled[1][...]
        xhat, _ = _ln_stats(s)
        y = xhat * vecs[0][...] + vecs[1][...]
        outs[0][...] = y
        outs[1][...] = y.astype(BF16)
        outs[2][...] = s

    return rows_call(name, fn, [x_in, m], [g, b], [(d, F32), (d, BF16), (d, F32)], [])


def ln_bwd(name, ca, da, db, s, g):
    d = s.shape[1]

    def fn(step, tiled, vecs, outs, accs):
        dx = ca * tiled[0][...] + tiled[1][...]
        xhat, rstd = _ln_stats(tiled[2][...])
        dxh = dx * vecs[0][...]
        ds = rstd * (dxh - jnp.mean(dxh, axis=-1, keepdims=True) - xhat * jnp.mean(dxh * xhat, axis=-1, keepdims=True))
        outs[0][...] = ds
        outs[1][...] = ds.astype(BF16)
        _accumulate(step, accs[0], jnp.sum(dx * xhat, axis=0, keepdims=True))
        _accumulate(step, accs[1], jnp.sum(dx, axis=0, keepdims=True))

    return rows_call(name, fn, [da, db, s], [g], [(d, F32), (d, BF16)], [((1, d), F32), ((1, d), F32)])


def ple_bwd(name, dx3, gate, e):
    d = dx3.shape[1]

    def fn(step, tiled, vecs, outs, accs):
        dx, gt, ev = tiled[0][...], tiled[1][...], tiled[2][...]
        dpre = dx * ev * gt * (1.0 - gt)
        outs[0][...] = (dx * gt).astype(BF16)
        outs[1][...] = dpre.astype(BF16)
        _accumulate(step, accs[0], jnp.sum(dpre, axis=0, keepdims=True))

    return rows_call(name, fn, [dx3, gate, e], [], [(d, BF16), (d, BF16)], [((1, d), F32)])


def loss_head(name, y, target):
    t, d = y.shape

    def fn(step, tiled, vecs, outs, accs):
        err = tiled[0][...] - tiled[1][...]
        outs[0][...] = err * (1.0 / d)
        part = jnp.sum(jnp.sum(err * err, axis=1, keepdims=True), axis=0, keepdims=True) * (0.5 / d)
        _accumulate(step, accs[0], part)

    return rows_call(name, fn, [y, target], [], [(d, F32)], [((1, 1), F32)])


def _softplus(z):
    return jnp.maximum(z, 0.0) + jnp.log1p(jnp.exp(-jnp.abs(z)))


def _sqrt_and_inverse(z):
    inv = lax.rsqrt(jnp.maximum(z, 1e-30))
    return z * inv, inv


def _gelu(y):
    th = jnp.tanh(GELU_C * (y + GELU_K * (y * y * y)))
    cdf = 0.5 * (1.0 + th)
    return y * cdf, cdf + 0.5 * y * (1.0 - th * th) * (GELU_C * (1.0 + 3.0 * GELU_K * y * y))


def _up(win, k):
    return pltpu.roll(win, win.shape[0] - k, 0)


def _down(win, k):
    return pltpu.roll(win, k, 0)


def _lru_gates(win, row0, cw_ref, cb, wa, ba, wx, bx, sp):
    h = CONV_HALO
    u = (cb + cw_ref[3:4, :] * win[h:] + cw_ref[2:3, :] * _down(win, 1)[h:]
         + cw_ref[1:2, :] * _down(win, 2)[h:] + cw_ref[0:1, :] * _down(win, 3)[h:])
    ub = u.astype(BF16)
    r = jax.nn.sigmoid(jnp.dot(ub, wa, preferred_element_type=F32) + ba)
    ig = jax.nn.sigmoid(jnp.dot(ub, wx, preferred_element_type=F32) + bx)
    log_a = (-LRU_C) * r * sp
    a = jnp.exp(log_a)
    mult = _sqrt_and_inverse(-jnp.tanh(log_a) * (a * a + 1.0))[0]
    first = (row0 + lax.broadcasted_iota(jnp.int32, u.shape, 0)) == 0
    mult = jnp.where(first, 1.0, mult)
    return u, r, ig, a, mult, first


def _block_scan(a, b, reverse):
    n = a.shape[0]
    a, b = a.reshape(n // 8, 8, LRU_BW), b.reshape(n // 8, 8, LRU_BW)
    pos = lax.broadcasted_iota(jnp.int32, a.shape, 1)
    for s in (1, 2, 4):
        keep = (pos >= 8 - s) if reverse else (pos < s)
        by = 8 - s if reverse else s
        b = jnp.where(keep, b, a * pltpu.roll(b, by, 1) + b)
        a = jnp.where(keep, a, a * pltpu.roll(a, by, 1))
    return a.reshape(n, LRU_BW), b.reshape(n, LRU_BW)


def _carry_scan(a_ref, b_ref, out_ref, out_off, t, reverse):
    groups, per_step = t // 8, 8

    def step(j, h):
        for k in range(per_step):
            g = j * per_step + k
            r0 = pl.multiple_of((groups - 1 - g if reverse else g) * 8, 8)
            edge = r0 if reverse else r0 + 7
            h_out = a_ref[pl.ds(edge, 1), :] * h + b_ref[pl.ds(edge, 1), :]
            out_ref[pl.ds(pl.multiple_of(out_off + r0, 8), 8), :] = a_ref[pl.ds(r0, 8), :] * h + b_ref[pl.ds(r0, 8), :]
            h = h_out
        return h

    lax.fori_loop(0, groups // per_step, step, jnp.zeros((1, LRU_BW), F32))


def _lru_in_specs(t, heads):
    blk = lambda i: (0, i)
    return [
        pl.BlockSpec((2, t, LRU_BW), lambda i: (0, 0, i)),
        pl.BlockSpec((CONV_WIDTH, LRU_BW), blk),
        pl.BlockSpec((1, LRU_BW), blk),
        pl.BlockSpec((None, LRU_BW, LRU_BW), lambda i: (i, 0, 0)),
        pl.BlockSpec((1, LRU_BW), blk),
        pl.BlockSpec((None, LRU_BW, LRU_BW), lambda i: (i, 0, 0)),
        pl.BlockSpec((1, LRU_BW), blk),
        pl.BlockSpec((1, LRU_BW), blk),
    ]


def lru_fwd(name, proj, conv_w, conv_b, wa, ba, wx, bx, lam):
    _, t, c = proj.shape
    heads = c // LRU_BW
    rc = min(256, t)

    def body(proj_ref, cw_ref, cb_ref, wa_ref, ba_ref, wx_ref, bx_ref, lam_ref, out_ref, upad, a_s, b_s):
        upad[0:CONV_HALO, :] = jnp.zeros((CONV_HALO, LRU_BW), F32)
        upad[CONV_HALO:, :] = proj_ref[0]
        sp = _softplus(-lam_ref[...])
        cb, ba, bx, wa, wx = cb_ref[...], ba_ref[...], bx_ref[...], wa_ref[...], wx_ref[...]

        def gates(i, carry):
            r0 = pl.multiple_of(i * rc, rc)
            win = upad[pl.ds(r0, rc + CONV_HALO), :]
            u, r, ig, a, mult, _ = _lru_gates(win, r0, cw_ref, cb, wa, ba, wx, bx, sp)
            rows = pl.ds(r0, rc)
            a_s[rows, :], b_s[rows, :] = _block_scan(a, mult * (ig * u), False)
            return carry

        lax.fori_loop(0, t // rc, gates, 0)
        _carry_scan(a_s, b_s, b_s, 0, t, False)

        def gate_out(i, carry):
            r0 = pl.multiple_of(i * rc, rc)
            gy, _ = _gelu(proj_ref[1, pl.ds(r0, rc), :])
            out_ref[pl.ds(r0, rc), :] = (b_s[pl.ds(r0, rc), :] * gy).astype(BF16)
            return carry

        lax.fori_loop(0, t // rc, gate_out, 0)

    return pl.pallas_call(
        body,
        name=name,
        grid=(heads,),
        in_specs=_lru_in_specs(t, heads),
        out_specs=pl.BlockSpec((t, LRU_BW), lambda i: (0, i)),
        out_shape=jax.ShapeDtypeStruct((t, c), BF16),
        scratch_shapes=[pltpu.VMEM((t + CONV_HALO, LRU_BW), F32)] + [pltpu.VMEM((t, LRU_BW), F32)] * 2,
        compiler_params=_params("parallel"),
    )(proj, conv_w, conv_b, wa, ba, wx, bx, lam)


def lru_bwd(name, proj, dhg, conv_w, conv_b, wa, ba, wx, bx, lam, deps=()):
    _, t, c = proj.shape
    heads = c // LRU_BW
    rc = min(256, t)
    h8 = CONV_HALO

    def body(proj_ref, dhg_ref, cw_ref, cb_ref, wa_ref, ba_ref, wx_ref, bx_ref, lam_ref,
             dproj_ref, dcw_ref, dcb_ref, dba_ref, dbx_ref, dlam_ref, dwa_ref, dwx_ref,
             upad, u_s, r_s, ig_s, apad, hpad, g_s, dupad, sa_s, sb_s):
        zeros8 = jnp.zeros((h8, LRU_BW), F32)
        upad[0:h8, :] = zeros8
        upad[h8:, :] = proj_ref[0]
        hpad[0:h8, :] = zeros8
        apad[t:, :] = zeros8
        dupad[t:, :] = zeros8
        lam = lam_ref[...]
        sp = _softplus(-lam)
        cb, ba, bx, wa, wx = cb_ref[...], ba_ref[...], bx_ref[...], wa_ref[...], wx_ref[...]

        def gates(i, carry):
            r0 = pl.multiple_of(i * rc, rc)
            win = upad[pl.ds(r0, rc + h8), :]
            u, r, ig, a, mult, _ = _lru_gates(win, r0, cw_ref, cb, wa, ba, wx, bx, sp)
            u_s[pl.ds(r0, rc), :] = u
            r_s[pl.ds(r0, rc), :] = r
            ig_s[pl.ds(r0, rc), :] = ig
            rows = pl.ds(r0, rc)
            apad[rows, :] = a
            sa_s[rows, :], sb_s[rows, :] = _block_scan(a, mult * (ig * u), False)
            return carry

        lax.fori_loop(0, t // rc, gates, 0)
        _carry_scan(sa_s, sb_s, hpad, h8, t, False)

        def out_gate(i, carry):
            r0 = pl.multiple_of(i * rc, rc)
            gy, dgy = _gelu(proj_ref[1, pl.ds(r0, rc), :])
            dh = dhg_ref[pl.ds(r0, rc), :]
            hh = hpad[pl.ds(pl.multiple_of(r0 + h8, 8), rc), :]
            dproj_ref[1, pl.ds(r0, rc), :] = (dh * hh * dgy).astype(BF16)
            rows = pl.ds(r0, rc)
            a_next = _up(apad[pl.ds(r0, rc + h8), :], 1)[:rc]
            sa_s[rows, :], sb_s[rows, :] = _block_scan(a_next, dh * gy, True)
            return carry

        lax.fori_loop(0, t // rc, out_gate, 0)
        _carry_scan(sa_s, sb_s, g_s, 0, t, True)

        zrow = jnp.zeros((1, LRU_BW), F32)
        zmat = jnp.zeros((LRU_BW, LRU_BW), F32)

        def grads(i, carry):
            dsp, dba, dbx, dwa, dwx = carry
            r0 = pl.multiple_of(i * rc, rc)
            g = g_s[pl.ds(r0, rc), :]
            u, r, ig, a = u_s[pl.ds(r0, rc), :], r_s[pl.ds(r0, rc), :], ig_s[pl.ds(r0, rc), :], apad[pl.ds(r0, rc), :]
            hprev = _down(hpad[pl.ds(r0, rc + h8), :], 1)[h8:]
            first = (r0 + lax.broadcasted_iota(jnp.int32, u.shape, 0)) == 0
            log_a = (-LRU_C) * r * sp
            mult, inv_mult = _sqrt_and_inverse(-jnp.tanh(log_a) * (a * a + 1.0))
            mult = jnp.where(first, 1.0, mult)
            dmult = jnp.where(first, 0.0, g * (ig * u))
            dlog_a = g * hprev * a - dmult * (a * a) * inv_mult
            dr = dlog_a * ((-LRU_C) * sp)
            dpre_r = dr * r * (1.0 - r)
            dpre_i = (g * mult * u) * ig * (1.0 - ig)
            pr, pi, ub = dpre_r.astype(BF16), dpre_i.astype(BF16), u.astype(BF16)
            nt = (((1,), (1,)), ((), ()))
            tn = (((0,), (0,)), ((), ()))
            du = (g * mult * ig + lax.dot_general(pr, wa, nt, preferred_element_type=F32)
                  + lax.dot_general(pi, wx, nt, preferred_element_type=F32))
            dupad[pl.ds(r0, rc), :] = du
            return (dsp + jnp.sum(dlog_a * ((-LRU_C) * r), axis=0, keepdims=True),
                    dba + jnp.sum(dpre_r, axis=0, keepdims=True),
                    dbx + jnp.sum(dpre_i, axis=0, keepdims=True),
                    dwa + lax.dot_general(ub, pr, tn, preferred_element_type=F32),
                    dwx + lax.dot_general(ub, pi, tn, preferred_element_type=F32))

        dsp, dba, dbx, dwa, dwx = lax.fori_loop(0, t // rc, grads, (zrow, zrow, zrow, zmat, zmat))
        dba_ref[...] = dba
        dbx_ref[...] = dbx
        dwa_ref[...] = dwa
        dwx_ref[...] = dwx
        dlam_ref[...] = -dsp * jax.nn.sigmoid(-lam)

        def conv_back(i, carry):
            dcb, d0, d1, d2, d3 = carry
            r0 = pl.multiple_of(i * rc, rc)
            dwin = dupad[pl.ds(r0, rc + h8), :]
            du = dwin[:rc]
            du0 = (cw_ref[3:4, :] * du + cw_ref[2:3, :] * _up(dwin, 1)[:rc]
                   + cw_ref[1:2, :] * _up(dwin, 2)[:rc] + cw_ref[0:1, :] * _up(dwin, 3)[:rc])
            dproj_ref[0, pl.ds(r0, rc), :] = du0.astype(BF16)
            win = upad[pl.ds(r0, rc + h8), :]
            red = lambda v: jnp.sum(v, axis=0, keepdims=True)
            return (dcb + red(du), d0 + red(du * _down(win, 3)[h8:]), d1 + red(du * _down(win, 2)[h8:]),
                    d2 + red(du * _down(win, 1)[h8:]), d3 + red(du * win[h8:]))

        dcb, d0, d1, d2, d3 = lax.fori_loop(0, t // rc, conv_back, (zrow,) * 5)
        dcb_ref[...] = dcb
        dcw_ref[0:1, :] = d0
        dcw_ref[1:2, :] = d1
        dcw_ref[2:3, :] = d2
        dcw_ref[3:4, :] = d3

    blk = lambda i: (0, i)
    vec = jax.ShapeDtypeStruct((1, c), F32)
    mat = jax.ShapeDtypeStruct((heads, LRU_BW, LRU_BW), F32)
    full = lambda: pltpu.VMEM((t, LRU_BW), F32)
    padded = lambda: pltpu.VMEM((t + h8, LRU_BW), F32)
    return pl.pallas_call(
        lambda *refs: body(*refs[len(deps):]),
        name=name,
        grid=(heads,),
        in_specs=[_ANY] * len(deps) + _lru_in_specs(t, heads)[:1] + [pl.BlockSpec((t, LRU_BW), blk)] + _lru_in_specs(t, heads)[1:],
        out_specs=[pl.BlockSpec((2, t, LRU_BW), lambda i: (0, 0, i)), pl.BlockSpec((CONV_WIDTH, LRU_BW), blk)]
        + [pl.BlockSpec((1, LRU_BW), blk)] * 4 + [pl.BlockSpec((None, LRU_BW, LRU_BW), lambda i: (i, 0, 0))] * 2,
        out_shape=[jax.ShapeDtypeStruct((2, t, c), BF16), jax.ShapeDtypeStruct((CONV_WIDTH, c), F32), vec, vec, vec, vec, mat, mat],
        scratch_shapes=[padded(), full(), full(), full(), padded(), padded(), full(), padded()] + [full()] * 2,
        compiler_params=_params("parallel"),
    )(*deps, proj, dhg, conv_w, conv_b, wa, ba, wx, bx, lam)


def _pick_level(g, levels):
    out = levels[-1]
    for k in range(len(levels) - 2, -1, -1):
        out = jnp.where(g == k, levels[k], out)
    return out


def _pool_z(win, g, row0, rc):
    levels, cur = [], win
    for k in range(len(POOL_WINDOWS)):
        cur = cur + _down(cur, 1 << k)
        levels.append(cur[POOL_HALO:])
    tot = _pick_level(g, levels)
    width = jnp.left_shift(2, g)
    row = row0 + lax.broadcasted_iota(jnp.int32, tot.shape, 0)
    cnt = jnp.minimum(row + 1, width).astype(F32)
    return tot / cnt - win[POOL_HALO:], cnt


def _pool_specs(t, gw):
    blk = lambda g: (0, g)
    return [pl.BlockSpec((t, gw), blk), pl.BlockSpec((None, gw, gw), lambda g: (g, 0, 0)),
            pl.BlockSpec((1, gw), blk), pl.BlockSpec((1, gw), blk)]


def pool_fwd(name, u, w_grp, b_grp, scale):
    t, d = u.shape
    gw = d // len(POOL_WINDOWS)
    rc = min(256, t)

    def body(u_ref, wg_ref, bg_ref, sc_ref, out_ref, upad):
        g = pl.program_id(0)
        upad[0:POOL_HALO, :] = jnp.zeros((POOL_HALO, gw), F32)
        upad[POOL_HALO:, :] = u_ref[...]
        wg, bg, sc = wg_ref[...], bg_ref[...], sc_ref[...]

        def chunk(i, carry):
            r0 = pl.multiple_of(i * rc, rc)
            z, _ = _pool_z(upad[pl.ds(r0, rc + POOL_HALO), :], g, r0, rc)
            z2 = jnp.dot(z.astype(BF16), wg, preferred_element_type=F32) + bg
            out_ref[pl.ds(r0, rc), :] = (z2 * sc).astype(BF16)
            return carry

        lax.fori_loop(0, t // rc, chunk, 0)

    return pl.pallas_call(
        body,
        name=name,
        grid=(len(POOL_WINDOWS),),
        in_specs=_pool_specs(t, gw),
        out_specs=pl.BlockSpec((t, gw), lambda g: (0, g)),
        out_shape=jax.ShapeDtypeStruct((t, d), BF16),
        scratch_shapes=[pltpu.VMEM((t + POOL_HALO, gw), F32)],
        compiler_params=_params("parallel"),
    )(u, w_grp, b_grp, scale)


def pool_bwd(name, u, dzs, w_grp, b_grp, scale, deps=()):
    t, d = u.shape
    gw = d // len(POOL_WINDOWS)
    rc = min(256, t)

    def body(u_ref, dzs_ref, wg_ref, bg_ref, sc_ref, du_ref, dwg_ref, dbg_ref, dsc_ref, upad, qpad, dz_s):
        g = pl.program_id(0)
        upad[0:POOL_HALO, :] = jnp.zeros((POOL_HALO, gw), F32)
        upad[POOL_HALO:, :] = u_ref[...]
        qpad[t:, :] = jnp.zeros((POOL_HALO, gw), F32)
        wg, bg, sc = wg_ref[...], bg_ref[...], sc_ref[...]
        zrow = jnp.zeros((1, gw), F32)

        def chunk(i, carry):
            dsc, dbg, dwg = carry
            r0 = pl.multiple_of(i * rc, rc)
            z, cnt = _pool_z(upad[pl.ds(r0, rc + POOL_HALO), :], g, r0, rc)
            zb = z.astype(BF16)
            z2 = jnp.dot(zb, wg, preferred_element_type=F32) + bg
            dzs = dzs_ref[pl.ds(r0, rc), :]
            dz2 = dzs * sc
            d2b = dz2.astype(BF16)
            dz = lax.dot_general(d2b, wg, (((1,), (1,)), ((), ())), preferred_element_type=F32)
            dz_s[pl.ds(r0, rc), :] = dz
            qpad[pl.ds(r0, rc), :] = dz / cnt
            return (dsc + jnp.sum(dzs * z2, axis=0, keepdims=True), dbg + jnp.sum(dz2, axis=0, keepdims=True),
                    dwg + lax.dot_general(zb, d2b, (((0,), (0,)), ((), ())), preferred_element_type=F32))

        dsc, dbg, dwg = lax.fori_loop(0, t // rc, chunk, (zrow, zrow, jnp.zeros((gw, gw), F32)))
        dsc_ref[...] = dsc
        dbg_ref[...] = dbg
        dwg_ref[...] = dwg

        def spread(i, carry):
            r0 = pl.multiple_of(i * rc, rc)
            levels, cur = [], qpad[pl.ds(r0, rc + POOL_HALO), :]
            for k in range(len(POOL_WINDOWS)):
                cur = cur + _up(cur, 1 << k)
                levels.append(cur[:rc])
            du_ref[pl.ds(r0, rc), :] = (_pick_level(g, levels) - dz_s[pl.ds(r0, rc), :]).astype(BF16)
            return carry

        lax.fori_loop(0, t // rc, spread, 0)

    blk = lambda g: (0, g)
    vec = jax.ShapeDtypeStruct((1, d), F32)
    return pl.pallas_call(
        lambda *refs: body(*refs[len(deps):]),
        name=name,
        grid=(len(POOL_WINDOWS),),
        in_specs=[_ANY] * len(deps) + _pool_specs(t, gw)[:1] + [pl.BlockSpec((t, gw), blk)] + _pool_specs(t, gw)[1:],
        out_specs=[pl.BlockSpec((t, gw), blk), pl.BlockSpec((None, gw, gw), lambda g: (g, 0, 0)),
                   pl.BlockSpec((1, gw), blk), pl.BlockSpec((1, gw), blk)],
        out_shape=[jax.ShapeDtypeStruct((t, d), BF16), jax.ShapeDtypeStruct((len(POOL_WINDOWS), gw, gw), F32), vec, vec],
        scratch_shapes=[pltpu.VMEM((t + POOL_HALO, gw), F32), pltpu.VMEM((t + POOL_HALO, gw), F32), pltpu.VMEM((t, gw), F32)],
        compiler_params=_params("parallel"),
    )(*deps, u, dzs, w_grp, b_grp, scale)


def _place():
    return lax.axis_index("x"), lax.axis_index("y"), lax.axis_index("c")


def _other_chips(x, y):
    return [(1 - x, y), (x, 1 - y), (1 - x, 1 - y)]


def _half(c, rows):
    h = rows // 2
    return pl.ds(pl.multiple_of(c * h, 8), h)


_ANY = pl.BlockSpec(memory_space=pl.ANY)


def into_block(name, shards, layer, r, me, dtype):
    c = shards.shape[1]
    tr = _tile(r, 512, 16)
    per = r // tr

    def body(me_ref, s_ref, o_ref):
        o_ref[...] = s_ref[...].astype(o_ref.dtype)

    return pl.pallas_call(
        body,
        name=name,
        grid_spec=pltpu.PrefetchScalarGridSpec(
            num_scalar_prefetch=1,
            grid=(per,),
            in_specs=[pl.BlockSpec((tr, c), lambda i, me_ref: (layer * per + i, 0))],
            out_specs=pl.BlockSpec((None, tr, c), lambda i, me_ref: (me_ref[0], i, 0)),
        ),
        out_shape=jax.ShapeDtypeStruct((N_CHIPS, r, c), dtype),
        compiler_params=_params("parallel"),
    )(me, shards)


_HBM = pl.BlockSpec(memory_space=pltpu.HBM)
_SEM = pl.BlockSpec(memory_space=pltpu.SEMAPHORE)


def _in_hbm(a):
    return pltpu.with_memory_space_constraint(a, pltpu.HBM)


def split_start(name, plan, n_copies, bufs, dep):
    n = len(bufs)

    def body(*refs):
        for cp in plan(refs[:n], refs[n + 1], refs[n + 2]):
            cp.start()
        refs[-1][...] = jnp.zeros_like(refs[-1])

    res = pl.pallas_call(
        body,
        name=name,
        in_specs=[_HBM] * n + [_ANY],
        out_specs=[_SEM, _SEM] + [_HBM] * n + [pl.BlockSpec(memory_space=pltpu.VMEM)],
        out_shape=[pltpu.SemaphoreType.DMA((n_copies,)), pltpu.SemaphoreType.DMA((n_copies,))]
        + [pltpu.HBM(b.shape, b.dtype) for b in bufs] + [jax.ShapeDtypeStruct((8, 128), F32)],
        input_output_aliases={i: 2 + i for i in range(n)},
        compiler_params=pltpu.CompilerParams(has_side_effects=pltpu.SideEffectType.DATAFLOW_SIDE_EFFECTING),
    )(*[_in_hbm(b) for b in bufs], dep)
    return res[0], res[1], list(res[2:2 + n]), res[-1]


def split_wait(name, plan, send_sems, recv_sems, bufs, after):
    n = len(bufs)

    def body(*refs):
        copies = plan(refs[:n], refs[n], refs[n + 1])
        for cp in copies:
            cp.wait_send()
        for cp in copies:
            cp.wait_recv()

    return pl.pallas_call(
        body,
        name=name,
        in_specs=[_HBM] * n + [_SEM, _SEM, _ANY],
        out_specs=[_HBM] * n,
        out_shape=[pltpu.HBM(b.shape, b.dtype) for b in bufs],
        input_output_aliases={i: i for i in range(n)},
        compiler_params=pltpu.CompilerParams(has_side_effects=pltpu.SideEffectType.DATAFLOW_SIDE_EFFECTING),
    )(*bufs, send_sems, recv_sems, after)


def gather_plan(n):
    def plan(bufs, send_sems, recv_sems):
        x, y, c = _place()
        copies = []
        for i in range(n):
            blk = bufs[i].at[2 * x + y, _half(c, bufs[i].shape[1]), :]
            for j, chip in enumerate(_other_chips(x, y)):
                copies.append(pltpu.make_async_remote_copy(
                    src_ref=blk, dst_ref=blk, send_sem=send_sems.at[3 * i + j], recv_sem=recv_sems.at[3 * i + j],
                    device_id=(*chip, c), device_id_type=MESH))
        return copies

    return plan


def forward_plan(n):
    def plan(bufs, send_sems, recv_sems):
        x, y, c = _place()
        copies = []
        for i in range(n):
            for j, (cx, cy) in enumerate(_other_chips(x, y)):
                blk = bufs[i].at[2 * cx + cy, _half(c, bufs[i].shape[1]), :]
                copies.append(pltpu.make_async_remote_copy(
                    src_ref=blk, dst_ref=blk, send_sem=send_sems.at[3 * i + j], recv_sem=recv_sems.at[3 * i + j],
                    device_id=(x, y, 1 - c), device_id_type=MESH))
        return copies

    return plan


def pair_forward(name, bufs):
    n = len(bufs)

    def body(*refs):
        copies = forward_plan(n)(refs[n:2 * n], refs[2 * n], refs[2 * n + 1])
        for cp in copies:
            cp.start()
        for cp in copies:
            cp.wait()

    return pl.pallas_call(
        body,
        name=name,
        in_specs=[_ANY] * n,
        out_specs=[_ANY] * n,
        out_shape=[jax.ShapeDtypeStruct(b.shape, b.dtype) for b in bufs],
        input_output_aliases={i: i for i in range(n)},
        scratch_shapes=[pltpu.SemaphoreType.DMA((3 * n,)), pltpu.SemaphoreType.DMA((3 * n,))],
    )(*bufs)


def all_gather_chips(name, bufs):
    n = len(bufs)

    def body(*refs):
        outs = refs[n:2 * n]
        send_sems, recv_sems = refs[2 * n:]
        x, y, c = _place()
        me, sibling = 2 * x + y, (x, y, 1 - c)
        chips = _other_chips(x, y)

        def copy(i, slot, block, half, to):
            blk = outs[i].at[block, _half(half, outs[i].shape[1]), :]
            return pltpu.make_async_remote_copy(
                src_ref=blk, dst_ref=blk, send_sem=send_sems.at[i * 6 + slot], recv_sem=recv_sems.at[i * 6 + slot],
                device_id=to, device_id_type=MESH)

        first = [copy(i, j, me, c, (*chip, c)) for i in range(n) for j, chip in enumerate(chips)]
        for cp in first:
            cp.start()
        passed = []
        for i in range(n):
            for j, (cx, cy) in enumerate(chips):
                copy(i, j, 2 * cx + cy, c, (x, y, c)).wait_recv()
                fwd = copy(i, 3 + j, 2 * cx + cy, c, sibling)
                fwd.start()
                passed.append(fwd)
        for i in range(n):
            for j, (cx, cy) in enumerate(chips):
                copy(i, 3 + j, 2 * cx + cy, 1 - c, (x, y, c)).wait_recv()
        for cp in first + passed:
            cp.wait_send()

    return pl.pallas_call(
        body,
        name=name,
        in_specs=[_ANY] * n,
        out_specs=[_ANY] * n,
        out_shape=[jax.ShapeDtypeStruct(b.shape, b.dtype) for b in bufs],
        input_output_aliases={i: i for i in range(n)},
        scratch_shapes=[pltpu.SemaphoreType.DMA((6 * n,)), pltpu.SemaphoreType.DMA((6 * n,))],
    )(*bufs)


def pair_plan(n):
    def plan(bufs, send_sems, recv_sems):
        x, y, c = _place()
        return [pltpu.make_async_remote_copy(
            src_ref=bufs[i].at[:, _half(1 - c, bufs[i].shape[1]), :], dst_ref=bufs[n + i], send_sem=send_sems.at[i],
            recv_sem=recv_sems.at[i], device_id=(x, y, 1 - c), device_id_type=MESH) for i in range(n)]

    return plan


def chip_plan(n):
    def plan(bufs, send_sems, recv_sems):
        x, y, c = _place()
        copies = []
        for i in range(n):
            for j, (cx, cy) in enumerate(_other_chips(x, y)):
                copies.append(pltpu.make_async_remote_copy(
                    src_ref=bufs[i].at[2 * cx + cy], dst_ref=bufs[n + i].at[2 * x + y], send_sem=send_sems.at[3 * i + j],
                    recv_sem=recv_sems.at[3 * i + j], device_id=(cx, cy, c), device_id_type=MESH))
        return copies

    return plan


def pair_lands(grads):
    return [jax.ShapeDtypeStruct((g.shape[0], g.shape[1] // 2, g.shape[2]), g.dtype) for g in grads]


def pair_gather_plan(blocked, layers):
    def plan(bufs, send_sems, recv_sems):
        x, y, c = _place()
        copies = []
        for i in range(len(bufs)):
            buf = bufs[i].at[2 * x + y] if blocked[i] else bufs[i]
            r = buf.shape[0] // layers[i]
            for l in range(layers[i]):
                mine = buf.at[pl.ds(pl.multiple_of(l * r + c * (r // 2), 8), r // 2), :]
                copies.append(pltpu.make_async_remote_copy(
                    src_ref=mine, dst_ref=mine, send_sem=send_sems.at[len(copies)], recv_sem=recv_sems.at[len(copies)],
                    device_id=(x, y, 1 - c), device_id_type=MESH))
        return copies

    return plan


def spread_plan(n):
    def plan(bufs, send_sems, recv_sems):
        x, y, c = _place()
        copies = []
        for i in range(n):
            blk = bufs[i].at[2 * x + y]
            for j, chip in enumerate(_other_chips(x, y)):
                copies.append(pltpu.make_async_remote_copy(
                    src_ref=blk, dst_ref=blk, send_sem=send_sems.at[3 * i + j], recv_sem=recv_sems.at[3 * i + j],
                    device_id=(*chip, c), device_id_type=MESH))
        return copies

    return plan


def pair_gather(name, bufs, blocked, layers):
    n = len(bufs)
    n_copies = sum(layers)

    def body(*refs):
        copies = pair_gather_plan(blocked, layers)(refs[n:2 * n], refs[2 * n], refs[2 * n + 1])
        for cp in copies:
            cp.start()
        for cp in copies:
            cp.wait()

    return pl.pallas_call(
        body,
        name=name,
        in_specs=[_ANY] * n,
        out_specs=[_ANY] * n,
        out_shape=[jax.ShapeDtypeStruct(b.shape, b.dtype) for b in bufs],
        input_output_aliases={i: i for i in range(n)},
        scratch_shapes=[pltpu.SemaphoreType.DMA((n_copies,)), pltpu.SemaphoreType.DMA((n_copies,))],
    )(*bufs)


def pair_sum(name, grad, recv, core, dtype):
    _, r, c = grad.shape
    h = r // 2
    th = _tile(h, 1024, 16)
    per = h // th

    def body(core_ref, g_ref, r_ref, o_ref):
        o_ref[...] = (g_ref[...].astype(F32) + r_ref[...].astype(F32)).astype(o_ref.dtype)

    return pl.pallas_call(
        body,
        name=name,
        grid_spec=pltpu.PrefetchScalarGridSpec(
            num_scalar_prefetch=1,
            grid=(N_CHIPS, per),
            in_specs=[pl.BlockSpec((None, th, c), lambda k, i, core_ref: (k, core_ref[0] * per + i, 0)),
                      pl.BlockSpec((None, th, c), lambda k, i, core_ref: (k, i, 0))],
            out_specs=pl.BlockSpec((None, th, c), lambda k, i, core_ref: (k, i, 0)),
        ),
        out_shape=jax.ShapeDtypeStruct((N_CHIPS, h, c), dtype),
        compiler_params=_params("parallel", "parallel"),
    )(core, grad, recv)


def chip_sum(name, got, parts, place, blocked, into=None, layer=0, n_layers=1):
    _, h, c = parts.shape
    th = _tile(h, 512, 16)
    per = h // th

    def body(place_ref, q0, q1, q2, q3, p_ref, *rest):
        o_ref = rest[-1]
        me = place_ref[0]
        own = p_ref[...].astype(F32)
        v = [jnp.where(me == k, own, q[...].astype(F32)) for k, q in enumerate((q0, q1, q2, q3))]
        o_ref[...] = ((v[0] + v[1]) + v[2]) + v[3]

    def got_spec(k):
        return pl.BlockSpec((None, th, c), lambda i, pr: (jnp.where(pr[0] == k, (k + 1) % N_CHIPS, k), i, 0))

    if blocked:
        out_spec = pl.BlockSpec((None, th, c), lambda i, pr: (pr[0], pr[1] * per + i, 0))
        out_shape = jax.ShapeDtypeStruct((N_CHIPS, 2 * h, c), F32)
    else:
        out_spec = pl.BlockSpec((th, c), lambda i, pr: ((2 * layer + pr[1]) * per + i, 0))
        out_shape = jax.ShapeDtypeStruct((n_layers * 2 * h, c), F32)
    carried = [] if into is None else [into]
    return pl.pallas_call(
        body,
        name=name,
        grid_spec=pltpu.PrefetchScalarGridSpec(
            num_scalar_prefetch=1,
            grid=(per,),
            in_specs=[got_spec(k) for k in range(N_CHIPS)] + [pl.BlockSpec((None, th, c), lambda i, pr: (pr[0], i, 0))]
            + [_ANY] * len(carried),
            out_specs=out_spec,
        ),
        out_shape=out_shape,
        input_output_aliases={6: 0} if carried else {},
        compiler_params=_params("parallel"),
    )(place, got, got, got, got, parts, *carried)


def adamw(name, w, g, m, v):
    r, c = w.shape
    tr = _tile(r, 512, 8)
    c1 = 1.0 - ADAM_B1 ** ADAM_STEP
    c2 = 1.0 - ADAM_B2 ** ADAM_STEP

    def body(w_ref, g_ref, m_ref, v_ref, d_ref, nm_ref, nv_ref, g_out_ref):
        gv = g_ref[...]
        g_out_ref[...] = gv
        nm = ADAM_B1 * m_ref[...] + (1.0 - ADAM_B1) * gv
        nv = ADAM_B2 * v_ref[...] + (1.0 - ADAM_B2) * (gv * gv)
        d_ref[...] = -ADAM_LR * ((nm / c1) / (jnp.sqrt(nv / c2) + ADAM_EPS) + ADAM_WD * w_ref[...])
        nm_ref[...] = nm
        nv_ref[...] = nv

    spec = pl.BlockSpec((tr, c), lambda i: (i, 0))
    return pl.pallas_call(
        body,
        name=name,
        grid=(r // tr,),
        in_specs=[spec] * 4,
        out_specs=[spec] * 4,
        out_shape=[jax.ShapeDtypeStruct((r, c), F32)] * 4,
        compiler_params=_params("parallel"),
    )(w, g, m, v)


def adamw_small(name, ws, gs, ms, vs):
    n = len(ws)
    c1 = 1.0 - ADAM_B1 ** ADAM_STEP
    c2 = 1.0 - ADAM_B2 ** ADAM_STEP

    def body(*refs):
        for i in range(n):
            w_ref, g_ref, m_ref, v_ref = (refs[j * n + i] for j in range(4))
            d_ref, nm_ref, nv_ref = (refs[(4 + j) * n + i] for j in range(3))
            gv = g_ref[...]
            nm = ADAM_B1 * m_ref[...] + (1.0 - ADAM_B1) * gv
            nv = ADAM_B2 * v_ref[...] + (1.0 - ADAM_B2) * (gv * gv)
            d_ref[...] = -ADAM_LR * ((nm / c1) / (jnp.sqrt(nv / c2) + ADAM_EPS) + ADAM_WD * w_ref[...])
            nm_ref[...] = nm
            nv_ref[...] = nv

    whole = pl.BlockSpec(memory_space=pltpu.VMEM)
    res = pl.pallas_call(
        body,
        name=name,
        in_specs=[whole] * (4 * n),
        out_specs=[whole] * (3 * n),
        out_shape=[jax.ShapeDtypeStruct(w.shape, F32) for w in ws] * 3,
        compiler_params=pltpu.CompilerParams(vmem_limit_bytes=VMEM_LIMIT_BYTES),
    )(*ws, *gs, *ms, *vs)
    return res[:n], res[n:2 * n], res[2 * n:]


def _pack(arrays, row_multiple, cols=BLOB_COLS):
    flat = jnp.concatenate([a.reshape(-1).astype(F32) for a in arrays])
    rows = -(-flat.shape[0] // cols)
    rows = -(-rows // row_multiple) * row_multiple
    return jnp.pad(flat, (0, rows * cols - flat.shape[0])).reshape(rows, cols)


def _unpack(blob, shapes):
    flat, out, off = blob.reshape(-1), [], 0
    for s in shapes:
        size = math.prod(s)
        out.append(flat[off:off + size].reshape(s))
        off += size
    return out


def _unpack_rows(blobs, shapes):
    out, off = [], 0
    for s in shapes:
        size = math.prod(s)
        out.append(blobs[:, off:off + size].reshape((blobs.shape[0],) + tuple(s)))
        off += size
    return out


def kernel(x, p, lru_w_in, lru_conv_w, lru_conv_b, lru_wa, lru_ba, lru_wx, lru_bx, lru_lambda, lru_w_out, pool_w_in, pool_w_grp, pool_b_grp, pool_scale, pool_w_out, ln_mix_g, ln_mix_b, mlp_w1, mlp_w2, ln_mlp_g, ln_mlp_b, ple_w, ple_gate_w, ple_gate_b, loss_target, m_lru_w_in, m_lru_conv_w, m_lru_conv_b, m_lru_wa, m_lru_ba, m_lru_wx, m_lru_bx, m_lru_lambda, m_lru_w_out, m_pool_w_in, m_pool_w_grp, m_pool_b_grp, m_pool_scale, m_pool_w_out, m_ln_mix_g, m_ln_mix_b, m_mlp_w1, m_mlp_w2, m_ln_mlp_g, m_ln_mlp_b, m_ple_w, m_ple_gate_w, m_ple_gate_b, v_lru_w_in, v_lru_conv_w, v_lru_conv_b, v_lru_wa, v_lru_ba, v_lru_wx, v_lru_bx, v_lru_lambda, v_lru_w_out, v_pool_w_in, v_pool_w_grp, v_pool_b_grp, v_pool_scale, v_pool_w_out, v_ln_mix_g, v_ln_mix_b, v_mlp_w1, v_mlp_w2, v_ln_mlp_g, v_ln_mlp_b, v_ple_w, v_ple_gate_w, v_ple_gate_b):
    weights = dict(lru_w_in=lru_w_in, lru_conv_w=lru_conv_w, lru_conv_b=lru_conv_b, lru_wa=lru_wa, lru_ba=lru_ba, lru_wx=lru_wx, lru_bx=lru_bx, lru_lambda=lru_lambda, lru_w_out=lru_w_out, pool_w_in=pool_w_in, pool_w_grp=pool_w_grp, pool_b_grp=pool_b_grp, pool_scale=pool_scale, pool_w_out=pool_w_out, ln_mix_g=ln_mix_g, ln_mix_b=ln_mix_b, mlp_w1=mlp_w1, mlp_w2=mlp_w2, ln_mlp_g=ln_mlp_g, ln_mlp_b=ln_mlp_b, ple_w=ple_w, ple_gate_w=ple_gate_w, ple_gate_b=ple_gate_b)
    mom_m = dict(lru_w_in=m_lru_w_in, lru_conv_w=m_lru_conv_w, lru_conv_b=m_lru_conv_b, lru_wa=m_lru_wa, lru_ba=m_lru_ba, lru_wx=m_lru_wx, lru_bx=m_lru_bx, lru_lambda=m_lru_lambda, lru_w_out=m_lru_w_out, pool_w_in=m_pool_w_in, pool_w_grp=m_pool_w_grp, pool_b_grp=m_pool_b_grp, pool_scale=m_pool_scale, pool_w_out=m_pool_w_out, ln_mix_g=m_ln_mix_g, ln_mix_b=m_ln_mix_b, mlp_w1=m_mlp_w1, mlp_w2=m_mlp_w2, ln_mlp_g=m_ln_mlp_g, ln_mlp_b=m_ln_mlp_b, ple_w=m_ple_w, ple_gate_w=m_ple_gate_w, ple_gate_b=m_ple_gate_b)
    mom_v = dict(lru_w_in=v_lru_w_in, lru_conv_w=v_lru_conv_w, lru_conv_b=v_lru_conv_b, lru_wa=v_lru_wa, lru_ba=v_lru_ba, lru_wx=v_lru_wx, lru_bx=v_lru_bx, lru_lambda=v_lru_lambda, lru_w_out=v_lru_w_out, pool_w_in=v_pool_w_in, pool_w_grp=v_pool_w_grp, pool_b_grp=v_pool_b_grp, pool_scale=v_pool_scale, pool_w_out=v_pool_w_out, ln_mix_g=v_ln_mix_g, ln_mix_b=v_ln_mix_b, mlp_w1=v_mlp_w1, mlp_w2=v_mlp_w2, ln_mlp_g=v_ln_mlp_g, ln_mlp_b=v_ln_mlp_b, ple_w=v_ple_w, ple_gate_w=v_ple_gate_w, ple_gate_b=v_ple_gate_b)
    names = list(weights)

    depth, d = ln_mix_g.shape
    t = x.shape[1]
    n_a, n_b = lru_w_in.shape[0], pool_w_in.shape[0]
    d_rnn = lru_w_out.shape[1] * N_CHIPS
    heads = d_rnn // LRU_BW
    d_ff = mlp_w1.shape[2] * N_CHIPS
    ple_dim = ple_w.shape[1]
    n_grp = len(POOL_WINDOWS)
    gw = d // n_grp
    alpha = (2 * depth) ** 0.25
    chip = 2 * lax.axis_index("x") + lax.axis_index("y")
    place = jnp.stack([chip, lax.axis_index("c")]).astype(jnp.int32)

    x2d = x.reshape(t, d)
    target = loss_target.reshape(t, d)
    p3 = p.reshape(depth, t, ple_dim)

    big = ["lru_w_in", "lru_w_out", "pool_w_in", "pool_w_out", "mlp_w1", "mlp_w2", "ple_w", "ple_gate_w", "pool_w_grp"]
    flat2 = lambda a: a.reshape(-1, a.shape[-1])
    small_sharded = ["lru_conv_w", "pool_b_grp", "pool_scale"]
    small_blob = _pack([weights[k] for k in small_sharded], 16, cols=256)
    every_layer = ("mlp_w1", "mlp_w2", "ple_w", "ple_gate_w")

    def layer_keys(i):
        return (["lru_w_in", "lru_w_out"] if i % 2 == 0 else ["pool_w_in", "pool_w_out", "pool_w_grp"]) + list(every_layer)

    def stage(k, i):
        w = weights[k]
        return into_block(f"stage_l{i}_{k}", flat2(w), i if k in every_layer else i // 2, math.prod(w.shape[1:-1]),
                          place[:1], BF16)

    staged = [[stage(k, i) for k in layer_keys(i)] for i in range(depth)]
    first = all_gather_chips("gather_l0", staged[0][:1] + [into_block("stage_small", small_blob, 0, small_blob.shape[0], place[:1], F32)])
    wg = {(layer_keys(0)[0], 0): first[0]}

    tokens = []

    def take_tokens():
        deps = tuple(tokens)
        tokens.clear()
        return deps

    def mm(*args, **kwargs):
        return matmul(*args, deps=take_tokens(), **kwargs)

    def start_gather(tag, bufs, dep):
        plan = gather_plan(len(bufs))
        flight = (plan,) + split_start(f"gather_{tag}_start", plan, 3 * len(bufs), bufs, dep)
        tokens.append(flight[-1])
        return flight

    def land_gather(tag, flight, keys, layer, after, wait_for=True):
        plan, send_sems, recv_sems, bufs, _ = flight
        landed = split_wait(f"gather_{tag}_wait", plan, send_sems, recv_sems, bufs, after)
        if wait_for:
            wg.update(zip([(k, layer) for k in keys], pair_forward(f"gather_{tag}_forward", landed)))
            return None
        plan = forward_plan(len(landed))
        forwarding = (tag, plan) + split_start(f"gather_{tag}_forward_start", plan, 3 * len(landed), landed, after)
        tokens.append(forwarding[-1])
        return forwarding

    def finish_forward(forwarding, keys, layer, after):
        tag, plan, send_sems, recv_sems, bufs, _ = forwarding
        wg.update(zip([(k, layer) for k in keys], split_wait(f"gather_{tag}_forward_wait", plan, send_sems, recv_sems, bufs, after)))

    conv_w_sh, b_grp_sh, scale_sh = _unpack_rows(first[-1].reshape(N_CHIPS, -1), [weights[k].shape for k in small_sharded])
    conv_w_full = jnp.moveaxis(conv_w_sh, 0, 2).reshape(n_a, CONV_WIDTH, d_rnn)
    b_grp_full = jnp.moveaxis(b_grp_sh, 0, 1).reshape(n_b, 1, d)
    scale_full = jnp.moveaxis(scale_sh, 0, 1).reshape(n_b, 1, d)
    rows_grp = gw // N_CHIPS
    w_grp_full = lambda i: jnp.moveaxis(wg["pool_w_grp", i].reshape(N_CHIPS, n_grp, rows_grp, gw), 0, 1).reshape(n_grp, gw, gw)
    wa_bf, wx_bf = lru_wa.astype(BF16), lru_wx.astype(BF16)
    row = lambda a, i: a[i].reshape(1, -1)

    def ln_after(acc, x_in, g, b):
        s = alpha * x_in + acc
        y = _ln_stats(s)[0] * g + b
        return y, y, s

    ln_outs = [plain(shape=(t, d), dtype=F32), plain(shape=(t, d), dtype=BF16), plain(shape=(t, d), dtype=F32)]
    saved = []
    cur, cur_bf = x2d, x2d
    for i in range(depth):
        slot = i // 2
        sv = dict(x_bf=cur_bf)
        if i == 0:
            flight = start_gather("l0_rest", staged[0][1:], first[0])
        elif i + 1 < depth:
            flight = start_gather(f"l{i + 1}", staged[i + 1], cur)
        if i % 2 == 0:
            (proj,) = mm(f"l{i}_lru_in", plain(cur_bf), colsplit(wg["lru_w_in", i], 0, d), "nn",
                         [colsplit(None, 0, t, n=2, full=(2, t, d_rnn), dtype=F32)])
            hg = lru_fwd(f"l{i}_lru", proj, conv_w_full[slot], row(lru_conv_b, slot), wa_bf[slot], row(lru_ba, slot),
                         wx_bf[slot], row(lru_bx, slot), row(lru_lambda, slot))
            if i == 0:
                land_gather("l0_rest", flight, layer_keys(0)[1:], 0, hg)
                flight = start_gather("l1", staged[1], hg)
            x1, x1_bf, s1 = mm(f"l{i}_lru_out", plain(hg), rowsplit_whole(wg["lru_w_out", i]), "nn", ln_outs, pk=2048,
                               epilogue=ln_after, tiles=[plain(cur)], rows=[row(ln_mix_g, i), row(ln_mix_b, i)])
            sv.update(proj=proj, act=hg)
        else:
            (u,) = mm(f"l{i}_pool_in", plain(cur_bf), rowsplit_whole(wg["pool_w_in", i]), "nn",
                          [plain(shape=(t, d), dtype=F32)])
            zs = pool_fwd(f"l{i}_pool", u, w_grp_full(i), b_grp_full[slot], scale_full[slot])
            x1, x1_bf, s1 = mm(f"l{i}_pool_out", plain(zs), rowsplit_whole(wg["pool_w_out", i]), "nn", ln_outs,
                               epilogue=ln_after, tiles=[plain(cur)], rows=[row(ln_mix_g, i), row(ln_mix_b, i)])
            sv.update(u=u, act=zs)

        def relu2(acc):
            hr = jnp.maximum(acc, 0.0)
            return hr, hr * hr

        hr, hh = mm(f"l{i}_mlp_up", plain(x1_bf), colsplit(wg["mlp_w1", i], 0, d), "nn",
                    [plain(shape=(t, d_ff), dtype=BF16), plain(shape=(t, d_ff), dtype=BF16)], epilogue=relu2, pm=2048)
        (mlp,) = mm(f"l{i}_mlp_down", plain(hh), rowsplit_whole(wg["mlp_w2", i]), "nn",
                    [plain(shape=(t, d), dtype=F32)], pk=d_ff)
        x2, x2_bf, s2 = ln_fwd(f"l{i}_ln_mlp", alpha, x1, mlp, row(ln_mlp_g, i), row(ln_mlp_b, i))
        if i + 1 < depth:
            forwarding = land_gather(f"l{i + 1}", flight, layer_keys(i + 1), i + 1, x2_bf, wait_for=False)

        def ple_out(acc, x2_t, gb, p_t, pw):
            e_t = jnp.concatenate([jnp.dot(p_t.astype(BF16), pw[k], preferred_element_type=F32) for k in range(N_CHIPS)], axis=1)
            gate = jax.nn.sigmoid(acc + gb)
            x3 = x2_t + e_t * gate
            return x3, x3, gate, e_t

        cur, cur_bf, gate, e = mm(
            f"l{i}_ple_gate", plain(x2_bf), rowsplit_whole(wg["ple_gate_w", i]), "nn",
            [plain(shape=(t, d), dtype=F32), plain(shape=(t, d), dtype=BF16), plain(shape=(t, d), dtype=F32), plain(shape=(t, d), dtype=F32)],
            epilogue=ple_out, tiles=[plain(x2)], rows=[row(ple_gate_b, i)], pm=512,
            side=[(p3[i], lambda tm, tn: pl.BlockSpec((tm, ple_dim), lambda r, c, kk: (r, 0))),
                  (wg["ple_w", i], lambda tm, tn: pl.BlockSpec(wg["ple_w", i].shape, lambda r, c, kk: (0, 0, 0)))])
        sv.update(s1=s1, x1_bf=x1_bf, hr=hr, hh=hh, s2=s2, x2_bf=x2_bf, gate=gate, e=e)
        saved.append(sv)
        if i + 1 < depth:
            finish_forward(forwarding, layer_keys(i + 1), i + 1, cur)

    dy, loss_part = loss_head("loss", cur, target)
    loss = lax.psum(loss_part.reshape(()), ("x", "y", "c"))

    part = {}
    sums = {}

    def grad_view(key, split):
        w = weights[key]
        return split(None, 0, w.shape[1], full=(N_CHIPS, w.shape[1], w.shape[2]), dtype=BF16)

    def group_start(tag, items, dep):
        srcs = [part[it] for it in items]
        plan = pair_plan(len(srcs))
        lands = [lax.empty(s.shape, s.dtype) for s in pair_lands(srcs)]
        flight = (tag, items, plan) + split_start(f"grads_{tag}_pair_start", plan, len(srcs), srcs + lands, dep)
        tokens.append(flight[-1])
        return flight

    def group_mid(flight, after):
        tag, items, plan, send_sems, recv_sems, bufs, _ = flight
        bufs = split_wait(f"grads_{tag}_pair_wait", plan, send_sems, recv_sems, bufs, after)
        n = len(items)
        parts = [pair_sum(f"grads_{tag}_pair_sum_{j}", bufs[j], bufs[n + j], place[1:], F32 if it[0] == "blob" else BF16)
                 for j, it in enumerate(items)]
        plan = chip_plan(n)
        flight = (tag, items, plan) + split_start(f"grads_{tag}_chip_start", plan, 3 * n,
                                                  parts + [lax.empty(q.shape, q.dtype) for q in parts], after)
        tokens.append(flight[-1])
        return flight

    def group_end(flight, after):
        tag, items, plan, send_sems, recv_sems, bufs, _ = flight
        bufs = split_wait(f"grads_{tag}_chip_wait", plan, send_sems, recv_sems, bufs, after)
        n = len(items)
        for j, (k, layer) in enumerate(items):
            if k == "blob":
                sums[k] = chip_sum(f"grads_{tag}_chip_sum_{j}", bufs[n + j], bufs[j], place, True)
            else:
                sums[k] = chip_sum(f"grads_{tag}_chip_sum_{j}", bufs[n + j], bufs[j], place, False, into=sums.get(k),
                                   layer=layer if k in every_layer else layer // 2, n_layers=weights[k].shape[0])

    big_w = [k for k in big if k != "pool_w_grp"]
    small_keys = [k for k in names if k not in big_w]

    def ln_before(ca):
        def back(acc, upstream, s, g):
            dx = ca * upstream + acc
            xhat, rstd = _ln_stats(s)
            dxh = dx * g
            ds = rstd * (dxh - jnp.mean(dxh, axis=-1, keepdims=True) - xhat * jnp.mean(dxh * xhat, axis=-1, keepdims=True))
            return ds, ds, jnp.sum(dx * xhat, axis=0, keepdims=True), jnp.sum(dx, axis=0, keepdims=True)

        return back

    ds_outs = [plain(shape=(t, d), dtype=F32), plain(shape=(t, d), dtype=BF16)]
    small = {k: [None] * weights[k].shape[0] for k in names if k not in big or k == "pool_w_grp"}
    dcur = dy
    mlp_pair = mlp_chip = mix_pair = mix_chip = None
    for i in reversed(range(depth)):
        slot = i // 2
        sv = saved[i]
        de, dpre, dgb = ple_bwd(f"l{i}_ple_bwd", dcur, sv["gate"], sv["e"])
        small["ple_gate_b"][i] = dgb
        (part["ple_w", i],) = mm(f"l{i}_d_ple_w", plain(p3[i]), plain(de), "tn", [grad_view("ple_w", colsplit)])
        (part["ple_gate_w", i],) = mm(f"l{i}_d_ple_gate_w", plain(sv["x2_bf"]), plain(dpre), "tn",
                                          [grad_view("ple_gate_w", rowsplit)])
        ds2, ds2_bf, dg, db = mm(f"l{i}_dx2", plain(dpre), rowsplit_whole(wg["ple_gate_w", i]), "nt", ds_outs, col_sums=2, pm=512,
                                 epilogue=ln_before(1.0), tiles=[plain(dcur), plain(sv["s2"])], rows=[row(ln_mlp_g, i)])
        small["ln_mlp_g"][i], small["ln_mlp_b"][i] = dg, db
        (part["mlp_w2", i],) = mm(f"l{i}_d_mlp_w2", plain(sv["hh"]), plain(ds2_bf), "tn", [grad_view("mlp_w2", rowsplit)])
        (dhpre,) = mm(f"l{i}_dh", plain(ds2_bf), rowsplit(wg["mlp_w2", i], 0, d_ff // N_CHIPS), "nt",
                      [plain(shape=(t, d_ff), dtype=BF16)], tiles=[plain(sv["hr"])], pm=2048,
                      epilogue=lambda acc, hr_t: (acc * (2.0 * hr_t.astype(F32)),))
        (part["mlp_w1", i],) = mm(f"l{i}_d_mlp_w1", plain(sv["x1_bf"]), plain(dhpre), "tn", [grad_view("mlp_w1", colsplit)])
        if mlp_chip is not None:
            group_end(mlp_chip, dhpre)
        if mix_pair is not None:
            mix_chip = group_mid(mix_pair, dhpre)
        mlp_pair = group_start(f"l{i}_mlp", [(k, i) for k in every_layer], dhpre)
        (dx1b,) = mm(f"l{i}_dx1", plain(dhpre), colsplit(wg["mlp_w1", i], 0, d), "nt", [plain(shape=(t, d), dtype=F32)],
                     pm=2048)
        ds1, ds1_bf, dg, db = ln_bwd(f"l{i}_ln_mix_bwd", alpha, ds2, dx1b, sv["s1"], row(ln_mix_g, i))
        small["ln_mix_g"][i], small["ln_mix_b"][i] = dg, db
        residual = lambda acc, ds_t: (alpha * ds_t + acc,)
        if i % 2 == 0:
            (part["lru_w_out", i],) = mm(f"l{i}_d_lru_out", plain(sv["act"]), plain(ds1_bf), "tn",
                                             [grad_view("lru_w_out", rowsplit)])
            (dhg,) = mm(f"l{i}_dhg", plain(ds1_bf), rowsplit_whole(wg["lru_w_out", i]), "nt",
                            [plain(shape=(t, d_rnn), dtype=F32)], pn=2048)
            mlp_chip = group_mid(mlp_pair, dhg)
            dproj, dcw, dcb, dba, dbx, dlam, dwa, dwx = lru_bwd(
                f"l{i}_lru_bwd", sv["proj"], dhg, conv_w_full[slot], row(lru_conv_b, slot), wa_bf[slot], row(lru_ba, slot),
                wx_bf[slot], row(lru_bx, slot), row(lru_lambda, slot), deps=take_tokens())
            for key, val in (("lru_conv_w", dcw), ("lru_conv_b", dcb), ("lru_ba", dba), ("lru_bx", dbx),
                             ("lru_lambda", dlam), ("lru_wa", dwa), ("lru_wx", dwx)):
                small[key][slot] = val
            dproj_v = colsplit(dproj, 0, t, n=2)
            (part["lru_w_in", i],) = mm(f"l{i}_d_lru_in", plain(sv["x_bf"]), dproj_v, "tn", [grad_view("lru_w_in", colsplit)])
            (dcur,) = mm(f"l{i}_dx", dproj_v, colsplit(wg["lru_w_in", i], 0, d), "nt",
                             [plain(shape=(t, d), dtype=F32)], epilogue=residual, tiles=[plain(ds1)])
        else:
            (part["pool_w_out", i],) = mm(f"l{i}_d_pool_out", plain(sv["act"]), plain(ds1_bf), "tn",
                                              [grad_view("pool_w_out", rowsplit)])
            (dzs,) = mm(f"l{i}_dzs", plain(ds1_bf), rowsplit_whole(wg["pool_w_out", i]), "nt",
                            [plain(shape=(t, d), dtype=F32)])
            mlp_chip = group_mid(mlp_pair, dzs)
            du, dwg, dbg, dsc = pool_bwd(f"l{i}_pool_bwd", sv["u"], dzs, w_grp_full(i), b_grp_full[slot], scale_full[slot],
                                         deps=take_tokens())
            small["pool_w_grp"][slot], small["pool_b_grp"][slot], small["pool_scale"][slot] = dwg, dbg, dsc
            (part["pool_w_in", i],) = mm(f"l{i}_d_pool_in", plain(sv["x_bf"]), plain(du), "tn", [grad_view("pool_w_in", rowsplit)])
            (dcur,) = mm(f"l{i}_dx", plain(du), rowsplit_whole(wg["pool_w_in", i]), "nt",
                             [plain(shape=(t, d), dtype=F32)], epilogue=residual, tiles=[plain(ds1)])
        if mix_chip is not None:
            group_end(mix_chip, dcur)
        mixer = [(k, i) for k in layer_keys(i) if k not in every_layer and k != "pool_w_grp"]
        if i == 0:
            small_full = [jnp.stack(small[k]).reshape((weights[k].shape[0],) + tuple(
                s * (N_CHIPS if ax in _sharded_axis(k) else 1) for ax, s in enumerate(weights[k].shape[1:], 1))) for k in small_keys]
            blob = _pack(small_full, 64)
            part["blob", 0] = blob.reshape(N_CHIPS, blob.shape[0] // N_CHIPS, BLOB_COLS)
            mixer.append(("blob", 0))
        mix_pair = group_start(f"l{i}_mix", mixer, dcur)
    grad_x = dcur.reshape(x.shape)
    mix_chip = group_mid(mix_pair, dcur)
    group_end(mlp_chip, mix_chip[-1])

    grads, delta, new_m, new_v = {}, {}, {}, {}

    def halves_start(tag, keys, dep):
        layers = [1 if k == "blob" else weights[k].shape[0] for k in keys]
        plan = pair_gather_plan([k == "blob" for k in keys], layers)
        return (keys, plan) + split_start(f"grads_pair_gather_{tag}_start", plan, sum(layers), [sums[k] for k in keys], dep)

    def halves_end(tag, flight, after):
        keys, plan, send_sems, recv_sems, bufs, _ = flight
        return dict(zip(keys, split_wait(f"grads_pair_gather_{tag}_wait", plan, send_sems, recv_sems, bufs, after)))

    def update(k, g):
        dl, nm, nv, g = adamw("adamw_" + k, flat2(weights[k]), g, flat2(mom_m[k]), flat2(mom_v[k]))
        delta[k], new_m[k], new_v[k], grads[k] = (a.reshape(weights[k].shape) for a in (dl, nm, nv, g))

    last = [k for k, _ in mix_pair[1]]
    early = [k for k in big_w if k not in last]
    large = [k for k in early if k in ("mlp_w1", "mlp_w2")]
    little = [k for k in early if k not in large]
    little_flight = halves_start("little", little, mix_chip[-1])
    up_flight = halves_start("large0", large[:1], little_flight[-1])
    down_flight = halves_start("large1", large[1:], up_flight[-1])
    whole = halves_end("little", little_flight, down_flight[-1])
    for k in little:
        update(k, whole[k])
    whole.update(halves_end("large0", up_flight, delta[little[-1]]))
    update(large[0], whole[large[0]])
    group_end(mix_chip, delta[large[0]])
    whole.update(zip(last, pair_gather("grads_pair_gather_last", [sums[k] for k in last], [k == "blob" for k in last],
                                       [1 if k == "blob" else weights[k].shape[0] for k in last])))
    plan = spread_plan(1)
    send_sems, recv_sems, spreading, spread_token = split_start("gather_small_grads_start", plan, 3, [whole["blob"]], delta[large[0]])
    whole.update(halves_end("large1", down_flight, spread_token))
    for k in large[1:] + [k for k in last if k != "blob"]:
        update(k, whole[k])
    (blob_all,) = split_wait("gather_small_grads_wait", plan, send_sems, recv_sems, spreading, delta[last[0]])
    small_grads = dict(zip(small_keys, _unpack(blob_all.reshape(blob.shape), [a.shape for a in small_full])))
    for k in small_keys:
        for ax in _sharded_axis(k):
            n = weights[k].shape[ax]
            small_grads[k] = lax.dynamic_slice_in_dim(small_grads[k], chip * n, n, axis=ax)
    grads.update(small_grads)
    dl, nm, nv = adamw_small("adamw_small", *[[flat2(src[k]) for k in small_keys] for src in (weights, grads, mom_m, mom_v)])
    for out, res in ((delta, dl), (new_m, nm), (new_v, nv)):
        out.update({k: a.reshape(weights[k].shape) for k, a in zip(small_keys, res)})

    return (loss, grad_x, *[grads[k] for k in names], *[delta[k] for k in names],
            *[new_m[k] for k in names], *[new_v[k] for k in names])


def _sharded_axis(key):
    return {"lru_conv_w": (2,), "pool_w_grp": (2,), "pool_b_grp": (1,), "pool_scale": (1,)}.get(key, ())
```

```python
import functools
import math

import jax
import jax.numpy as jnp
from jax import lax
from jax.experimental import pallas as pl
from jax.experimental.pallas import tpu as pltpu

F32 = jnp.float32
BF16 = jnp.bfloat16

N_CHIPS = 4
LRU_BW = 128
LRU_C = 8.0
CONV_WIDTH = 4
POOL_WINDOWS = (2, 4, 8, 16)
POOL_HALO = 16
CONV_HALO = 8
LN_EPS = 1e-5
ADAM_LR = 0.001
ADAM_B1 = 0.9
ADAM_B2 = 0.999
ADAM_EPS = 1e-08
ADAM_WD = 0.01
ADAM_STEP = 10
GELU_C = math.sqrt(2.0 / math.pi)
GELU_K = 0.044715
VMEM_LIMIT_BYTES = 56 * 1024 * 1024
MATMUL_TILE_BYTES = 44 * 1024 * 1024
MESH = pl.DeviceIdType.MESH
BLOB_COLS = 1024


def _params(*sem):
    return pltpu.CompilerParams(dimension_semantics=tuple(sem), vmem_limit_bytes=VMEM_LIMIT_BYTES)


def _tile(unit, pref, align=128):
    if unit <= pref:
        return unit
    for d in range(2, unit + 1):
        if unit % d == 0 and unit // d <= pref and (unit // d) % align == 0:
            return unit // d
    raise ValueError((unit, pref, align))


class View:
    def __init__(self, arr, shape, row_unit, col_unit, block_fn, full=None, dtype=None):
        self.arr, self.shape, self.row_unit, self.col_unit, self.block_fn = arr, shape, row_unit, col_unit, block_fn
        self.full = full if full is not None else arr.shape
        self.dtype = dtype if dtype is not None else arr.dtype

    def spec(self, tr, tc, f):
        block, idx = self.block_fn(tr, tc)
        return pl.BlockSpec(block, lambda *g: idx(*f(*g)))


def plain(arr=None, shape=None, dtype=None):
    shape = arr.shape if arr is not None else shape
    return View(arr, shape, shape[0], shape[1], lambda tr, tc: ((tr, tc), lambda rt, ct: (rt, ct)), full=shape, dtype=dtype)


def colsplit(arr, layer, rows, n=N_CHIPS, full=None, dtype=None):
    full = arr.shape if arr is not None else full
    c = full[2]

    def block_fn(tr, tc):
        assert rows % tr == 0 and c % tc == 0, (rows, tr, c, tc)
        per, rpl = c // tc, rows // tr
        return (None, tr, tc), lambda rt, ct: (ct // per, layer * rpl + rt, ct % per)

    return View(arr, (rows, n * c), rows, c, block_fn, full=full, dtype=dtype)


def rowsplit(arr, layer, rows, n=N_CHIPS, full=None, dtype=None):
    full = arr.shape if arr is not None else full
    c = full[2]

    def block_fn(tr, tc):
        assert rows % tr == 0 and c % tc == 0, (rows, tr, c, tc)
        per = rows // tr
        return (None, tr, tc), lambda rt, ct: (rt // per, layer * per + rt % per, ct)

    return View(arr, (n * rows, c), rows, c, block_fn, full=full, dtype=dtype)


def rowsplit_whole(arr):
    n, rows, c = arr.shape

    def block_fn(tr, tc):
        assert tr == n * rows and c % tc == 0, (tr, n, rows, c, tc)
        return (n, rows, tc), lambda rt, ct: (0, 0, ct)

    return View(arr, (n * rows, c), n * rows, c, block_fn)


def matmul(name, a, b, mode, outs, epilogue=None, tiles=(), rows=(), side=(), deps=(), col_sums=0, pm=1024, pn=1024, pk=1024):
    if mode == "nn":
        (m, k), (k2, n) = a.shape, b.shape
        um, uk, un = a.row_unit, min(a.col_unit, b.row_unit), b.col_unit
        dims = (((1,), (0,)), ((), ()))
    elif mode == "nt":
        (m, k), (n, k2) = a.shape, b.shape
        um, uk, un = a.row_unit, min(a.col_unit, b.col_unit), b.row_unit
        dims = (((1,), (1,)), ((), ()))
    else:
        (k, m), (k2, n) = a.shape, b.shape
        um, uk, un = a.col_unit, min(a.row_unit, b.row_unit), b.col_unit
        dims = (((0,), (0,)), ((), ()))
    assert k == k2, (name, a.shape, b.shape)
    for o in list(outs) + list(tiles):
        assert o.shape == (m, n), (name, o.shape, m, n)
        um, un = min(um, o.row_unit), min(un, o.col_unit)
    tm, tn, tk = _tile(um, pm), _tile(un, pn), _tile(uk, pk)
    if mode == "tn" and uk == k:
        size = lambda v: jnp.dtype(v.dtype).itemsize
        need = 2 * k * (tm * size(a) + tn * size(b)) + 2 * tm * tn * sum(size(o) for o in outs)
        if need <= MATMUL_TILE_BYTES:
            tk = k
    assert m % tm == 0 and n % tn == 0 and k % tk == 0, (name, m, n, k, tm, tn, tk)
    gm, gn, gk = m // tm, n // tn, k // tk
    assert not col_sums or gn == 1, (name, gn)

    if mode == "nn":
        a_spec = a.spec(tm, tk, lambda i, j, kk: (i, kk))
        b_spec = b.spec(tk, tn, lambda i, j, kk: (kk, j))
    elif mode == "nt":
        a_spec = a.spec(tm, tk, lambda i, j, kk: (i, kk))
        b_spec = b.spec(tn, tk, lambda i, j, kk: (j, kk))
    else:
        a_spec = a.spec(tk, tm, lambda i, j, kk: (kk, i))
        b_spec = b.spec(tk, tn, lambda i, j, kk: (kk, j))
    tile_specs = [t.spec(tm, tn, lambda i, j, kk: (i, j)) for t in tiles]
    row_specs = [pl.BlockSpec((1, tn), lambda i, j, kk: (0, j)) for _ in rows] + [spec(tm, tn) for _, spec in side]
    rows = list(rows) + [arr for arr, _ in side]
    in_place = [o for o in outs if o.arr is not None]
    alias_specs = [pl.BlockSpec(memory_space=pl.ANY) for _ in in_place]
    out_specs = [o.spec(tm, tn, lambda i, j, kk: (i, j)) for o in outs]
    n_in = 2 + len(tiles) + len(rows)
    aliases = {}
    for o_idx, o in enumerate(outs):
        if o.arr is not None:
            aliases[n_in + in_place.index(o)] = o_idx
    n_t, n_r, n_a, n_o = len(tiles), len(rows), len(in_place) + len(deps), len(outs)
    dep_specs = [pl.BlockSpec(memory_space=pl.ANY) for _ in deps]

    def body(*refs):
        a_ref, b_ref = refs[0], refs[1]
        tile_refs = refs[2:2 + n_t]
        row_refs = refs[2 + n_t:2 + n_t + n_r]
        out_refs = refs[2 + n_t + n_r + n_a:2 + n_t + n_r + n_a + n_o]
        sum_refs = refs[2 + n_t + n_r + n_a + n_o:2 + n_t + n_r + n_a + n_o + col_sums]
        acc_ref = refs[-1] if gk > 1 else None

        def finish(acc):
            extra = [t[...] for t in tile_refs] + [r[...] for r in row_refs]
            res = epilogue(acc, *extra) if epilogue is not None else (acc,)
            for o_ref, r in zip(out_refs, res):
                o_ref[...] = r.astype(o_ref.dtype)
            for s_ref, r in zip(sum_refs, res[n_o:]):
                _accumulate(pl.program_id(0), s_ref, r)

        b_tile = b_ref[...]
        b_tile = b_tile.reshape(-1, b_tile.shape[-1])
        prod = lax.dot_general(a_ref[...].astype(BF16), b_tile.astype(BF16), dims, preferred_element_type=F32)
        if gk == 1:
            finish(prod)
        else:
            kk = pl.program_id(2)

            @pl.when(kk == 0)
            def _():
                acc_ref[...] = prod

            @pl.when(kk > 0)
            def _():
                acc_ref[...] += prod

            @pl.when(kk == gk - 1)
            def _():
                finish(acc_ref[...])

    res = pl.pallas_call(
        body,
        name=name,
        grid=(gm, gn, gk),
        in_specs=[a_spec, b_spec] + tile_specs + row_specs + alias_specs + dep_specs,
        out_specs=out_specs + [pl.BlockSpec((1, tn), lambda i, j, kk: (0, 0))] * col_sums,
        out_shape=[jax.ShapeDtypeStruct(o.full, o.dtype) for o in outs] + [jax.ShapeDtypeStruct((1, n), F32)] * col_sums,
        scratch_shapes=[pltpu.VMEM((tm, tn), F32)] if gk > 1 else [],
        input_output_aliases=aliases,
        compiler_params=_params(*(["arbitrary"] * 3 if col_sums else ["parallel", "parallel", "arbitrary"])),
    )(a.arr, b.arr, *[t.arr for t in tiles], *rows, *[o.arr for o in in_place], *deps)
    return res


def rows_call(name, fn, tiled, vecs, tiled_out, acc_out, tr=512):
    t = tiled[0].shape[0]
    tr = min(tr, t)
    assert t % tr == 0
    n1, n2, n3 = len(tiled), len(vecs), len(tiled_out)

    def body(*refs):
        fn(pl.program_id(0), refs[:n1], refs[n1:n1 + n2], refs[n1 + n2:n1 + n2 + n3], refs[n1 + n2 + n3:])

    return pl.pallas_call(
        body,
        name=name,
        grid=(t // tr,),
        in_specs=[pl.BlockSpec((tr, x.shape[1]), lambda i: (i, 0)) for x in tiled]
        + [pl.BlockSpec(v.shape, lambda i: (0, 0)) for v in vecs],
        out_specs=[pl.BlockSpec((tr, c), lambda i: (i, 0)) for c, _ in tiled_out]
        + [pl.BlockSpec(s, lambda i: (0, 0)) for s, _ in acc_out],
        out_shape=[jax.ShapeDtypeStruct((t, c), d) for c, d in tiled_out] + [jax.ShapeDtypeStruct(s, d) for s, d in acc_out],
        compiler_params=_params("arbitrary" if acc_out else "parallel"),
    )(*tiled, *vecs)


def _accumulate(step, ref, val):
    @pl.when(step == 0)
    def _():
        ref[...] = val

    @pl.when(step > 0)
    def _():
        ref[...] += val


def _ln_stats(s):
    mu = jnp.mean(s, axis=-1, keepdims=True)
    d = s - mu
    var = jnp.mean(d * d, axis=-1, keepdims=True)
    rstd = lax.rsqrt(var + LN_EPS)
    return d * rstd, rstd


def ln_fwd(name, alpha, x_in, m, g, b):
    d = x_in.shape[1]

    def fn(step, tiled, vecs, outs, accs):
        s = alpha * tiled[0][...] + tiled[1][...]
        xhat, _ = _ln_stats(s)
        y = xhat * vecs[0][...] + vecs[1][...]
        outs[0][...] = y
        outs[1][...] = y.astype(BF16)
        outs[2][...] = s

    return rows_call(name, fn, [x_in, m], [g, b], [(d, F32), (d, BF16), (d, F32)], [])


def ln_bwd(name, ca, da, db, s, g):
    d = s.shape[1]

    def fn(step, tiled, vecs, outs, accs):
        dx = ca * tiled[0][...] + tiled[1][...]
        xhat, rstd = _ln_stats(tiled[2][...])
        dxh = dx * vecs[0][...]
        ds = rstd * (dxh - jnp.mean(dxh, axis=-1, keepdims=True) - xhat * jnp.mean(dxh * xhat, axis=-1, keepdims=True))
        outs[0][...] = ds
        outs[1][...] = ds.astype(BF16)
        _accumulate(step, accs[0], jnp.sum(dx * xhat, axis=0, keepdims=True))
        _accumulate(step, accs[1], jnp.sum(dx, axis=0, keepdims=True))

    return rows_call(name, fn, [da, db, s], [g], [(d, F32), (d, BF16)], [((1, d), F32), ((1, d), F32)])


def ple_bwd(name, dx3, gate, e):
    d = dx3.shape[1]

    def fn(step, tiled, vecs, outs, accs):
        dx, gt, ev = tiled[0][...], tiled[1][...], tiled[2][...]
        dpre = dx * ev * gt * (1.0 - gt)
        outs[0][...] = (dx * gt).astype(BF16)
        outs[1][...] = dpre.astype(BF16)
        _accumulate(step, accs[0], jnp.sum(dpre, axis=0, keepdims=True))

    return rows_call(name, fn, [dx3, gate, e], [], [(d, BF16), (d, BF16)], [((1, d), F32)])


def loss_head(name, y, target):
    t, d = y.shape

    def fn(step, tiled, vecs, outs, accs):
        err = tiled[0][...] - tiled[1][...]
        outs[0][...] = err * (1.0 / d)
        part = jnp.sum(jnp.sum(err * err, axis=1, keepdims=True), axis=0, keepdims=True) * (0.5 / d)
        _accumulate(step, accs[0], part)

    return rows_call(name, fn, [y, target], [], [(d, F32)], [((1, 1), F32)])


def _softplus(z):
    return jnp.maximum(z, 0.0) + jnp.log1p(jnp.exp(-jnp.abs(z)))


def _sqrt_and_inverse(z):
    inv = lax.rsqrt(jnp.maximum(z, 1e-30))
    return z * inv, inv


def _gelu(y):
    th = jnp.tanh(GELU_C * (y + GELU_K * (y * y * y)))
    cdf = 0.5 * (1.0 + th)
    return y * cdf, cdf + 0.5 * y * (1.0 - th * th) * (GELU_C * (1.0 + 3.0 * GELU_K * y * y))


def _up(win, k):
    return pltpu.roll(win, win.shape[0] - k, 0)


def _down(win, k):
    return pltpu.roll(win, k, 0)


def _lru_gates(win, row0, cw_ref, cb, wa, ba, wx, bx, sp):
    h = CONV_HALO
    u = (cb + cw_ref[3:4, :] * win[h:] + cw_ref[2:3, :] * _down(win, 1)[h:]
         + cw_ref[1:2, :] * _down(win, 2)[h:] + cw_ref[0:1, :] * _down(win, 3)[h:])
    ub = u.astype(BF16)
    r = jax.nn.sigmoid(jnp.dot(ub, wa, preferred_element_type=F32) + ba)
    ig = jax.nn.sigmoid(jnp.dot(ub, wx, preferred_element_type=F32) + bx)
    log_a = (-LRU_C) * r * sp
    a = jnp.exp(log_a)
    mult = _sqrt_and_inverse(-jnp.tanh(log_a) * (a * a + 1.0))[0]
    first = (row0 + lax.broadcasted_iota(jnp.int32, u.shape, 0)) == 0
    mult = jnp.where(first, 1.0, mult)
    return u, r, ig, a, mult, first


def _block_scan(a, b, reverse):
    n = a.shape[0]
    a, b = a.reshape(n // 8, 8, LRU_BW), b.reshape(n // 8, 8, LRU_BW)
    pos = lax.broadcasted_iota(jnp.int32, a.shape, 1)
    for s in (1, 2, 4):
        keep = (pos >= 8 - s) if reverse else (pos < s)
        by = 8 - s if reverse else s
        b = jnp.where(keep, b, a * pltpu.roll(b, by, 1) + b)
        a = jnp.where(keep, a, a * pltpu.roll(a, by, 1))
    return a.reshape(n, LRU_BW), b.reshape(n, LRU_BW)


def _carry_scan(a_ref, b_ref, out_ref, out_off, t, reverse):
    groups, per_step = t // 8, 8

    def step(j, h):
        for k in range(per_step):
            g = j * per_step + k
            r0 = pl.multiple_of((groups - 1 - g if reverse else g) * 8, 8)
            edge = r0 if reverse else r0 + 7
            h_out = a_ref[pl.ds(edge, 1), :] * h + b_ref[pl.ds(edge, 1), :]
            out_ref[pl.ds(pl.multiple_of(out_off + r0, 8), 8), :] = a_ref[pl.ds(r0, 8), :] * h + b_ref[pl.ds(r0, 8), :]
            h = h_out
        return h

    lax.fori_loop(0, groups // per_step, step, jnp.zeros((1, LRU_BW), F32))


def _lru_in_specs(t, heads):
    blk = lambda i: (0, i)
    return [
        pl.BlockSpec((2, t, LRU_BW), lambda i: (0, 0, i)),
        pl.BlockSpec((CONV_WIDTH, LRU_BW), blk),
        pl.BlockSpec((1, LRU_BW), blk),
        pl.BlockSpec((None, LRU_BW, LRU_BW), lambda i: (i, 0, 0)),
        pl.BlockSpec((1, LRU_BW), blk),
        pl.BlockSpec((None, LRU_BW, LRU_BW), lambda i: (i, 0, 0)),
        pl.BlockSpec((1, LRU_BW), blk),
        pl.BlockSpec((1, LRU_BW), blk),
    ]


def lru_fwd(name, proj, conv_w, conv_b, wa, ba, wx, bx, lam):
    _, t, c = proj.shape
    heads = c // LRU_BW
    rc = min(256, t)

    def body(proj_ref, cw_ref, cb_ref, wa_ref, ba_ref, wx_ref, bx_ref, lam_ref, out_ref, upad, a_s, b_s):
        upad[0:CONV_HALO, :] = jnp.zeros((CONV_HALO, LRU_BW), F32)
        upad[CONV_HALO:, :] = proj_ref[0]
        sp = _softplus(-lam_ref[...])
        cb, ba, bx, wa, wx = cb_ref[...], ba_ref[...], bx_ref[...], wa_ref[...], wx_ref[...]

        def gates(i, carry):
            r0 = pl.multiple_of(i * rc, rc)
            win = upad[pl.ds(r0, rc + CONV_HALO), :]
            u, r, ig, a, mult, _ = _lru_gates(win, r0, cw_ref, cb, wa, ba, wx, bx, sp)
            rows = pl.ds(r0, rc)
            a_s[rows, :], b_s[rows, :] = _block_scan(a, mult * (ig * u), False)
            return carry

        lax.fori_loop(0, t // rc, gates, 0)
        _carry_scan(a_s, b_s, b_s, 0, t, False)

        def gate_out(i, carry):
            r0 = pl.multiple_of(i * rc, rc)
            gy, _ = _gelu(proj_ref[1, pl.ds(r0, rc), :])
            out_ref[pl.ds(r0, rc), :] = (b_s[pl.ds(r0, rc), :] * gy).astype(BF16)
            return carry

        lax.fori_loop(0, t // rc, gate_out, 0)

    return pl.pallas_call(
        body,
        name=name,
        grid=(heads,),
        in_specs=_lru_in_specs(t, heads),
        out_specs=pl.BlockSpec((t, LRU_BW), lambda i: (0, i)),
        out_shape=jax.ShapeDtypeStruct((t, c), BF16),
        scratch_shapes=[pltpu.VMEM((t + CONV_HALO, LRU_BW), F32)] + [pltpu.VMEM((t, LRU_BW), F32)] * 2,
        compiler_params=_params("parallel"),
    )(proj, conv_w, conv_b, wa, ba, wx, bx, lam)


def lru_bwd(name, proj, dhg, conv_w, conv_b, wa, ba, wx, bx, lam, deps=()):
    _, t, c = proj.shape
    heads = c // LRU_BW
    rc = min(256, t)
    h8 = CONV_HALO

    def body(proj_ref, dhg_ref, cw_ref, cb_ref, wa_ref, ba_ref, wx_ref, bx_ref, lam_ref,
             dproj_ref, dcw_ref, dcb_ref, dba_ref, dbx_ref, dlam_ref, dwa_ref, dwx_ref,
             upad, u_s, r_s, ig_s, apad, hpad, g_s, dupad, sa_s, sb_s):
        zeros8 = jnp.zeros((h8, LRU_BW), F32)
        upad[0:h8, :] = zeros8
        upad[h8:, :] = proj_ref[0]
        hpad[0:h8, :] = zeros8
        apad[t:, :] = zeros8
        dupad[t:, :] = zeros8
        lam = lam_ref[...]
        sp = _softplus(-lam)
        cb, ba, bx, wa, wx = cb_ref[...], ba_ref[...], bx_ref[...], wa_ref[...], wx_ref[...]

        def gates(i, carry):
            r0 = pl.multiple_of(i * rc, rc)
            win = upad[pl.ds(r0, rc + h8), :]
            u, r, ig, a, mult, _ = _lru_gates(win, r0, cw_ref, cb, wa, ba, wx, bx, sp)
            u_s[pl.ds(r0, rc), :] = u
            r_s[pl.ds(r0, rc), :] = r
            ig_s[pl.ds(r0, rc), :] = ig
            rows = pl.ds(r0, rc)
            apad[rows, :] = a
            sa_s[rows, :], sb_s[rows, :] = _block_scan(a, mult * (ig * u), False)
            return carry

        lax.fori_loop(0, t // rc, gates, 0)
        _carry_scan(sa_s, sb_s, hpad, h8, t, False)

        def out_gate(i, carry):
            r0 = pl.multiple_of(i * rc, rc)
            gy, dgy = _gelu(proj_ref[1, pl.ds(r0, rc), :])
            dh = dhg_ref[pl.ds(r0, rc), :]
            hh = hpad[pl.ds(pl.multiple_of(r0 + h8, 8), rc), :]
            dproj_ref[1, pl.ds(r0, rc), :] = (dh * hh * dgy).astype(BF16)
            rows = pl.ds(r0, rc)
            a_next = _up(apad[pl.ds(r0, rc + h8), :], 1)[:rc]
            sa_s[rows, :], sb_s[rows, :] = _block_scan(a_next, dh * gy, True)
            return carry

        lax.fori_loop(0, t // rc, out_gate, 0)
        _carry_scan(sa_s, sb_s, g_s, 0, t, True)

        zrow = jnp.zeros((1, LRU_BW), F32)
        zmat = jnp.zeros((LRU_BW, LRU_BW), F32)

        def grads(i, carry):
            dsp, dba, dbx, dwa, dwx = carry
            r0 = pl.multiple_of(i * rc, rc)
            g = g_s[pl.ds(r0, rc), :]
            u, r, ig, a = u_s[pl.ds(r0, rc), :], r_s[pl.ds(r0, rc), :], ig_s[pl.ds(r0, rc), :], apad[pl.ds(r0, rc), :]
            hprev = _down(hpad[pl.ds(r0, rc + h8), :], 1)[h8:]
            first = (r0 + lax.broadcasted_iota(jnp.int32, u.shape, 0)) == 0
            log_a = (-LRU_C) * r * sp
            mult, inv_mult = _sqrt_and_inverse(-jnp.tanh(log_a) * (a * a + 1.0))
            mult = jnp.where(first, 1.0, mult)
            dmult = jnp.where(first, 0.0, g * (ig * u))
            dlog_a = g * hprev * a - dmult * (a * a) * inv_mult
            dr = dlog_a * ((-LRU_C) * sp)
            dpre_r = dr * r * (1.0 - r)
            dpre_i = (g * mult * u) * ig * (1.0 - ig)
            pr, pi, ub = dpre_r.astype(BF16), dpre_i.astype(BF16), u.astype(BF16)
            nt = (((1,), (1,)), ((), ()))
            tn = (((0,), (0,)), ((), ()))
            du = (g * mult * ig + lax.dot_general(pr, wa, nt, preferred_element_type=F32)
                  + lax.dot_general(pi, wx, nt, preferred_element_type=F32))
            dupad[pl.ds(r0, rc), :] = du
            return (dsp + jnp.sum(dlog_a * ((-LRU_C) * r), axis=0, keepdims=True),
                    dba + jnp.sum(dpre_r, axis=0, keepdims=True),
                    dbx + jnp.sum(dpre_i, axis=0, keepdims=True),
                    dwa + lax.dot_general(ub, pr, tn, preferred_element_type=F32),
                    dwx + lax.dot_general(ub, pi, tn, preferred_element_type=F32))

        dsp, dba, dbx, dwa, dwx = lax.fori_loop(0, t // rc, grads, (zrow, zrow, zrow, zmat, zmat))
        dba_ref[...] = dba
        dbx_ref[...] = dbx
        dwa_ref[...] = dwa
        dwx_ref[...] = dwx
        dlam_ref[...] = -dsp * jax.nn.sigmoid(-lam)

        def conv_back(i, carry):
            dcb, d0, d1, d2, d3 = carry
            r0 = pl.multiple_of(i * rc, rc)
            dwin = dupad[pl.ds(r0, rc + h8), :]
            du = dwin[:rc]
            du0 = (cw_ref[3:4, :] * du + cw_ref[2:3, :] * _up(dwin, 1)[:rc]
                   + cw_ref[1:2, :] * _up(dwin, 2)[:rc] + cw_ref[0:1, :] * _up(dwin, 3)[:rc])
            dproj_ref[0, pl.ds(r0, rc), :] = du0.astype(BF16)
            win = upad[pl.ds(r0, rc + h8), :]
            red = lambda v: jnp.sum(v, axis=0, keepdims=True)
            return (dcb + red(du), d0 + red(du * _down(win, 3)[h8:]), d1 + red(du * _down(win, 2)[h8:]),
                    d2 + red(du * _down(win, 1)[h8:]), d3 + red(du * win[h8:]))

        dcb, d0, d1, d2, d3 = lax.fori_loop(0, t // rc, conv_back, (zrow,) * 5)
        dcb_ref[...] = dcb
        dcw_ref[0:1, :] = d0
        dcw_ref[1:2, :] = d1
        dcw_ref[2:3, :] = d2
        dcw_ref[3:4, :] = d3

    blk = lambda i: (0, i)
    vec = jax.ShapeDtypeStruct((1, c), F32)
    mat = jax.ShapeDtypeStruct((heads, LRU_BW, LRU_BW), F32)
    full = lambda: pltpu.VMEM((t, LRU_BW), F32)
    padded = lambda: pltpu.VMEM((t + h8, LRU_BW), F32)
    return pl.pallas_call(
        lambda *refs: body(*refs[len(deps):]),
        name=name,
        grid=(heads,),
        in_specs=[_ANY] * len(deps) + _lru_in_specs(t, heads)[:1] + [pl.BlockSpec((t, LRU_BW), blk)] + _lru_in_specs(t, heads)[1:],
        out_specs=[pl.BlockSpec((2, t, LRU_BW), lambda i: (0, 0, i)), pl.BlockSpec((CONV_WIDTH, LRU_BW), blk)]
        + [pl.BlockSpec((1, LRU_BW), blk)] * 4 + [pl.BlockSpec((None, LRU_BW, LRU_BW), lambda i: (i, 0, 0))] * 2,
        out_shape=[jax.ShapeDtypeStruct((2, t, c), BF16), jax.ShapeDtypeStruct((CONV_WIDTH, c), F32), vec, vec, vec, vec, mat, mat],
        scratch_shapes=[padded(), full(), full(), full(), padded(), padded(), full(), padded()] + [full()] * 2,
        compiler_params=_params("parallel"),
    )(*deps, proj, dhg, conv_w, conv_b, wa, ba, wx, bx, lam)


def _pick_level(g, levels):
    out = levels[-1]
    for k in range(len(levels) - 2, -1, -1):
        out = jnp.where(g == k, levels[k], out)
    return out


def _pool_z(win, g, row0, rc):
    levels, cur = [], win
    for k in range(len(POOL_WINDOWS)):
        cur = cur + _down(cur, 1 << k)
        levels.append(cur[POOL_HALO:])
    tot = _pick_level(g, levels)
    width = jnp.left_shift(2, g)
    row = row0 + lax.broadcasted_iota(jnp.int32, tot.shape, 0)
    cnt = jnp.minimum(row + 1, width).astype(F32)
    return tot / cnt - win[POOL_HALO:], cnt


def _pool_specs(t, gw):
    blk = lambda g: (0, g)
    return [pl.BlockSpec((t, gw), blk), pl.BlockSpec((None, gw, gw), lambda g: (g, 0, 0)),
            pl.BlockSpec((1, gw), blk), pl.BlockSpec((1, gw), blk)]


def pool_fwd(name, u, w_grp, b_grp, scale):
    t, d = u.shape
    gw = d // len(POOL_WINDOWS)
    rc = min(256, t)

    def body(u_ref, wg_ref, bg_ref, sc_ref, out_ref, upad):
        g = pl.program_id(0)
        upad[0:POOL_HALO, :] = jnp.zeros((POOL_HALO, gw), F32)
        upad[POOL_HALO:, :] = u_ref[...]
        wg, bg, sc = wg_ref[...], bg_ref[...], sc_ref[...]

        def chunk(i, carry):
            r0 = pl.multiple_of(i * rc, rc)
            z, _ = _pool_z(upad[pl.ds(r0, rc + POOL_HALO), :], g, r0, rc)
            z2 = jnp.dot(z.astype(BF16), wg, preferred_element_type=F32) + bg
            out_ref[pl.ds(r0, rc), :] = (z2 * sc).astype(BF16)
            return carry

        lax.fori_loop(0, t // rc, chunk, 0)

    return pl.pallas_call(
        body,
        name=name,
        grid=(len(POOL_WINDOWS),),
        in_specs=_pool_specs(t, gw),
        out_specs=pl.BlockSpec((t, gw), lambda g: (0, g)),
        out_shape=jax.ShapeDtypeStruct((t, d), BF16),
        scratch_shapes=[pltpu.VMEM((t + POOL_HALO, gw), F32)],
        compiler_params=_params("parallel"),
    )(u, w_grp, b_grp, scale)


def pool_bwd(name, u, dzs, w_grp, b_grp, scale, deps=()):
    t, d = u.shape
    gw = d // len(POOL_WINDOWS)
    rc = min(256, t)

    def body(u_ref, dzs_ref, wg_ref, bg_ref, sc_ref, du_ref, dwg_ref, dbg_ref, dsc_ref, upad, qpad, dz_s):
        g = pl.program_id(0)
        upad[0:POOL_HALO, :] = jnp.zeros((POOL_HALO, gw), F32)
        upad[POOL_HALO:, :] = u_ref[...]
        qpad[t:, :] = jnp.zeros((POOL_HALO, gw), F32)
        wg, bg, sc = wg_ref[...], bg_ref[...], sc_ref[...]
        zrow = jnp.zeros((1, gw), F32)

        def chunk(i, carry):
            dsc, dbg, dwg = carry
            r0 = pl.multiple_of(i * rc, rc)
            z, cnt = _pool_z(upad[pl.ds(r0, rc + POOL_HALO), :], g, r0, rc)
            zb = z.astype(BF16)
            z2 = jnp.dot(zb, wg, preferred_element_type=F32) + bg
            dzs = dzs_ref[pl.ds(r0, rc), :]
            dz2 = dzs * sc
            d2b = dz2.astype(BF16)
            dz = lax.dot_general(d2b, wg, (((1,), (1,)), ((), ())), preferred_element_type=F32)
            dz_s[pl.ds(r0, rc), :] = dz
            qpad[pl.ds(r0, rc), :] = dz / cnt
            return (dsc + jnp.sum(dzs * z2, axis=0, keepdims=True), dbg + jnp.sum(dz2, axis=0, keepdims=True),
                    dwg + lax.dot_general(zb, d2b, (((0,), (0,)), ((), ())), preferred_element_type=F32))

        dsc, dbg, dwg = lax.fori_loop(0, t // rc, chunk, (zrow, zrow, jnp.zeros((gw, gw), F32)))
        dsc_ref[...] = dsc
        dbg_ref[...] = dbg
        dwg_ref[...] = dwg

        def spread(i, carry):
            r0 = pl.multiple_of(i * rc, rc)
            levels, cur = [], qpad[pl.ds(r0, rc + POOL_HALO), :]
            for k in range(len(POOL_WINDOWS)):
                cur = cur + _up(cur, 1 << k)
                levels.append(cur[:rc])
            du_ref[pl.ds(r0, rc), :] = (_pick_level(g, levels) - dz_s[pl.ds(r0, rc), :]).astype(BF16)
            return carry

        lax.fori_loop(0, t // rc, spread, 0)

    blk = lambda g: (0, g)
    vec = jax.ShapeDtypeStruct((1, d), F32)
    return pl.pallas_call(
        lambda *refs: body(*refs[len(deps):]),
        name=name,
        grid=(len(POOL_WINDOWS),),
        in_specs=[_ANY] * len(deps) + _pool_specs(t, gw)[:1] + [pl.BlockSpec((t, gw), blk)] + _pool_specs(t, gw)[1:],
        out_specs=[pl.BlockSpec((t, gw), blk), pl.BlockSpec((None, gw, gw), lambda g: (g, 0, 0)),
                   pl.BlockSpec((1, gw), blk), pl.BlockSpec((1, gw), blk)],
        out_shape=[jax.ShapeDtypeStruct((t, d), BF16), jax.ShapeDtypeStruct((len(POOL_WINDOWS), gw, gw), F32), vec, vec],
        scratch_shapes=[pltpu.VMEM((t + POOL_HALO, gw), F32), pltpu.VMEM((t + POOL_HALO, gw), F32), pltpu.VMEM((t, gw), F32)],
        compiler_params=_params("parallel"),
    )(*deps, u, dzs, w_grp, b_grp, scale)


def _place():
    return lax.axis_index("x"), lax.axis_index("y"), lax.axis_index("c")


def _other_chips(x, y):
    return [(1 - x, y), (x, 1 - y), (1 - x, 1 - y)]


def _half(c, rows):
    h = rows // 2
    return pl.ds(pl.multiple_of(c * h, 8), h)


_ANY = pl.BlockSpec(memory_space=pl.ANY)


def into_block(name, shards, layer, r, me, dtype):
    c = shards.shape[1]
    tr = _tile(r, 512, 16)
    per = r // tr

    def body(me_ref, s_ref, o_ref):
        o_ref[...] = s_ref[...].astype(o_ref.dtype)

    return pl.pallas_call(
        body,
        name=name,
        grid_spec=pltpu.PrefetchScalarGridSpec(
            num_scalar_prefetch=1,
            grid=(per,),
            in_specs=[pl.BlockSpec((tr, c), lambda i, me_ref: (layer * per + i, 0))],
            out_specs=pl.BlockSpec((None, tr, c), lambda i, me_ref: (me_ref[0], i, 0)),
        ),
        out_shape=jax.ShapeDtypeStruct((N_CHIPS, r, c), dtype),
        compiler_params=_params("parallel"),
    )(me, shards)


_HBM = pl.BlockSpec(memory_space=pltpu.HBM)
_SEM = pl.BlockSpec(memory_space=pltpu.SEMAPHORE)


def _in_hbm(a):
    return pltpu.with_memory_space_constraint(a, pltpu.HBM)


def split_start(name, plan, n_copies, bufs, dep):
    n = len(bufs)

    def body(*refs):
        for cp in plan(refs[:n], refs[n + 1], refs[n + 2]):
            cp.start()
        refs[-1][...] = jnp.zeros_like(refs[-1])

    res = pl.pallas_call(
        body,
        name=name,
        in_specs=[_HBM] * n + [_ANY],
        out_specs=[_SEM, _SEM] + [_HBM] * n + [pl.BlockSpec(memory_space=pltpu.VMEM)],
        out_shape=[pltpu.SemaphoreType.DMA((n_copies,)), pltpu.SemaphoreType.DMA((n_copies,))]
        + [pltpu.HBM(b.shape, b.dtype) for b in bufs] + [jax.ShapeDtypeStruct((8, 128), F32)],
        input_output_aliases={i: 2 + i for i in range(n)},
        compiler_params=pltpu.CompilerParams(has_side_effects=pltpu.SideEffectType.DATAFLOW_SIDE_EFFECTING),
    )(*[_in_hbm(b) for b in bufs], dep)
    return res[0], res[1], list(res[2:2 + n]), res[-1]


def split_wait(name, plan, send_sems, recv_sems, bufs, after):
    n = len(bufs)

    def body(*refs):
        copies = plan(refs[:n], refs[n], refs[n + 1])
        for cp in copies:
            cp.wait_send()
        for cp in copies:
            cp.wait_recv()

    return pl.pallas_call(
        body,
        name=name,
        in_specs=[_HBM] * n + [_SEM, _SEM, _ANY],
        out_specs=[_HBM] * n,
        out_shape=[pltpu.HBM(b.shape, b.dtype) for b in bufs],
        input_output_aliases={i: i for i in range(n)},
        compiler_params=pltpu.CompilerParams(has_side_effects=pltpu.SideEffectType.DATAFLOW_SIDE_EFFECTING),
    )(*bufs, send_sems, recv_sems, after)


def gather_plan(n):
    def plan(bufs, send_sems, recv_sems):
        x, y, c = _place()
        copies = []
        for i in range(n):
            blk = bufs[i].at[2 * x + y, _half(c, bufs[i].shape[1]), :]
            for j, chip in enumerate(_other_chips(x, y)):
                copies.append(pltpu.make_async_remote_copy(
                    src_ref=blk, dst_ref=blk, send_sem=send_sems.at[3 * i + j], recv_sem=recv_sems.at[3 * i + j],
                    device_id=(*chip, c), device_id_type=MESH))
        return copies

    return plan


def forward_plan(n):
    def plan(bufs, send_sems, recv_sems):
        x, y, c = _place()
        copies = []
        for i in range(n):
            for j, (cx, cy) in enumerate(_other_chips(x, y)):
                blk = bufs[i].at[2 * cx + cy, _half(c, bufs[i].shape[1]), :]
                copies.append(pltpu.make_async_remote_copy(
                    src_ref=blk, dst_ref=blk, send_sem=send_sems.at[3 * i + j], recv_sem=recv_sems.at[3 * i + j],
                    device_id=(x, y, 1 - c), device_id_type=MESH))
        return copies

    return plan


def pair_forward(name, bufs):
    n = len(bufs)

    def body(*refs):
        copies = forward_plan(n)(refs[n:2 * n], refs[2 * n], refs[2 * n + 1])
        for cp in copies:
            cp.start()
        for cp in copies:
            cp.wait()

    return pl.pallas_call(
        body,
        name=name,
        in_specs=[_ANY] * n,
        out_specs=[_ANY] * n,
        out_shape=[jax.ShapeDtypeStruct(b.shape, b.dtype) for b in bufs],
        input_output_aliases={i: i for i in range(n)},
        scratch_shapes=[pltpu.SemaphoreType.DMA((3 * n,)), pltpu.SemaphoreType.DMA((3 * n,))],
    )(*bufs)


def all_gather_chips(name, bufs):
    n = len(bufs)

    def body(*refs):
        outs = refs[n:2 * n]
        send_sems, recv_sems = refs[2 * n:]
        x, y, c = _place()
        me, sibling = 2 * x + y, (x, y, 1 - c)
        chips = _other_chips(x, y)

        def copy(i, slot, block, half, to):
            blk = outs[i].at[block, _half(half, outs[i].shape[1]), :]
            return pltpu.make_async_remote_copy(
                src_ref=blk, dst_ref=blk, send_sem=send_sems.at[i * 6 + slot], recv_sem=recv_sems.at[i * 6 + slot],
                device_id=to, device_id_type=MESH)

        first = [copy(i, j, me, c, (*chip, c)) for i in range(n) for j, chip in enumerate(chips)]
        for cp in first:
            cp.start()
        passed = []
        for i in range(n):
            for j, (cx, cy) in enumerate(chips):
                copy(i, j, 2 * cx + cy, c, (x, y, c)).wait_recv()
                fwd = copy(i, 3 + j, 2 * cx + cy, c, sibling)
                fwd.start()
                passed.append(fwd)
        for i in range(n):
            for j, (cx, cy) in enumerate(chips):
                copy(i, 3 + j, 2 * cx + cy, 1 - c, (x, y, c)).wait_recv()
        for cp in first + passed:
            cp.wait_send()

    return pl.pallas_call(
        body,
        name=name,
        in_specs=[_ANY] * n,
        out_specs=[_ANY] * n,
        out_shape=[jax.ShapeDtypeStruct(b.shape, b.dtype) for b in bufs],
        input_output_aliases={i: i for i in range(n)},
        scratch_shapes=[pltpu.SemaphoreType.DMA((6 * n,)), pltpu.SemaphoreType.DMA((6 * n,))],
    )(*bufs)


def pair_plan(n):
    def plan(bufs, send_sems, recv_sems):
        x, y, c = _place()
        return [pltpu.make_async_remote_copy(
            src_ref=bufs[i].at[:, _half(1 - c, bufs[i].shape[1]), :], dst_ref=bufs[n + i], send_sem=send_sems.at[i],
            recv_sem=recv_sems.at[i], device_id=(x, y, 1 - c), device_id_type=MESH) for i in range(n)]

    return plan


def chip_plan(n):
    def plan(bufs, send_sems, recv_sems):
        x, y, c = _place()
        copies = []
        for i in range(n):
            for j, (cx, cy) in enumerate(_other_chips(x, y)):
                copies.append(pltpu.make_async_remote_copy(
                    src_ref=bufs[i].at[2 * cx + cy], dst_ref=bufs[n + i].at[2 * x + y], send_sem=send_sems.at[3 * i + j],
                    recv_sem=recv_sems.at[3 * i + j], device_id=(cx, cy, c), device_id_type=MESH))
        return copies

    return plan


def pair_lands(grads):
    return [jax.ShapeDtypeStruct((g.shape[0], g.shape[1] // 2, g.shape[2]), g.dtype) for g in grads]


def pair_gather_plan(blocked, layers):
    def plan(bufs, send_sems, recv_sems):
        x, y, c = _place()
        copies = []
        for i in range(len(bufs)):
            buf = bufs[i].at[2 * x + y] if blocked[i] else bufs[i]
            r = buf.shape[0] // layers[i]
            for l in range(layers[i]):
                mine = buf.at[pl.ds(pl.multiple_of(l * r + c * (r // 2), 8), r // 2), :]
                copies.append(pltpu.make_async_remote_copy(
                    src_ref=mine, dst_ref=mine, send_sem=send_sems.at[len(copies)], recv_sem=recv_sems.at[len(copies)],
                    device_id=(x, y, 1 - c), device_id_type=MESH))
        return copies

    return plan


def spread_plan(n):
    def plan(bufs, send_sems, recv_sems):
        x, y, c = _place()
        copies = []
        for i in range(n):
            blk = bufs[i].at[2 * x + y]
            for j, chip in enumerate(_other_chips(x, y)):
                copies.append(pltpu.make_async_remote_copy(
                    src_ref=blk, dst_ref=blk, send_sem=send_sems.at[3 * i + j], recv_sem=recv_sems.at[3 * i + j],
                    device_id=(*chip, c), device_id_type=MESH))
        return copies

    return plan


def pair_gather(name, bufs, blocked, layers):
    n = len(bufs)
    n_copies = sum(layers)

    def body(*refs):
        copies = pair_gather_plan(blocked, layers)(refs[n:2 * n], refs[2 * n], refs[2 * n + 1])
        for cp in copies:
            cp.start()
        for cp in copies:
            cp.wait()

    return pl.pallas_call(
        body,
        name=name,
        in_specs=[_ANY] * n,
        out_specs=[_ANY] * n,
        out_shape=[jax.ShapeDtypeStruct(b.shape, b.dtype) for b in bufs],
        input_output_aliases={i: i for i in range(n)},
        scratch_shapes=[pltpu.SemaphoreType.DMA((n_copies,)), pltpu.SemaphoreType.DMA((n_copies,))],
    )(*bufs)


def pair_sum(name, grad, recv, core, dtype):
    _, r, c = grad.shape
    h = r // 2
    th = _tile(h, 1024, 16)
    per = h // th

    def body(core_ref, g_ref, r_ref, o_ref):
        o_ref[...] = (g_ref[...].astype(F32) + r_ref[...].astype(F32)).astype(o_ref.dtype)

    return pl.pallas_call(
        body,
        name=name,
        grid_spec=pltpu.PrefetchScalarGridSpec(
            num_scalar_prefetch=1,
            grid=(N_CHIPS, per),
            in_specs=[pl.BlockSpec((None, th, c), lambda k, i, core_ref: (k, core_ref[0] * per + i, 0)),
                      pl.BlockSpec((None, th, c), lambda k, i, core_ref: (k, i, 0))],
            out_specs=pl.BlockSpec((None, th, c), lambda k, i, core_ref: (k, i, 0)),
        ),
        out_shape=jax.ShapeDtypeStruct((N_CHIPS, h, c), dtype),
        compiler_params=_params("parallel", "parallel"),
    )(core, grad, recv)


def chip_sum(name, got, parts, place, blocked, into=None, layer=0, n_layers=1):
    _, h, c = parts.shape
    th = _tile(h, 512, 16)
    per = h // th

    def body(place_ref, q0, q1, q2, q3, p_ref, *rest):
        o_ref = rest[-1]
        me = place_ref[0]
        own = p_ref[...].astype(F32)
        v = [jnp.where(me == k, own, q[...].astype(F32)) for k, q in enumerate((q0, q1, q2, q3))]
        o_ref[...] = ((v[0] + v[1]) + v[2]) + v[3]

    def got_spec(k):
        return pl.BlockSpec((None, th, c), lambda i, pr: (jnp.where(pr[0] == k, (k + 1) % N_CHIPS, k), i, 0))

    if blocked:
        out_spec = pl.BlockSpec((None, th, c), lambda i, pr: (pr[0], pr[1] * per + i, 0))
        out_shape = jax.ShapeDtypeStruct((N_CHIPS, 2 * h, c), F32)
    else:
        out_spec = pl.BlockSpec((th, c), lambda i, pr: ((2 * layer + pr[1]) * per + i, 0))
        out_shape = jax.ShapeDtypeStruct((n_layers * 2 * h, c), F32)
    carried = [] if into is None else [into]
    return pl.pallas_call(
        body,
        name=name,
        grid_spec=pltpu.PrefetchScalarGridSpec(
            num_scalar_prefetch=1,
            grid=(per,),
            in_specs=[got_spec(k) for k in range(N_CHIPS)] + [pl.BlockSpec((None, th, c), lambda i, pr: (pr[0], i, 0))]
            + [_ANY] * len(carried),
            out_specs=out_spec,
        ),
        out_shape=out_shape,
        input_output_aliases={6: 0} if carried else {},
        compiler_params=_params("parallel"),
    )(place, got, got, got, got, parts, *carried)


def adamw(name, w, g, m, v):
    r, c = w.shape
    tr = _tile(r, 512, 8)
    c1 = 1.0 - ADAM_B1 ** ADAM_STEP
    c2 = 1.0 - ADAM_B2 ** ADAM_STEP

    def body(w_ref, g_ref, m_ref, v_ref, d_ref, nm_ref, nv_ref, g_out_ref):
        gv = g_ref[...]
        g_out_ref[...] = gv
        nm = ADAM_B1 * m_ref[...] + (1.0 - ADAM_B1) * gv
        nv = ADAM_B2 * v_ref[...] + (1.0 - ADAM_B2) * (gv * gv)
        d_ref[...] = -ADAM_LR * ((nm / c1) / (jnp.sqrt(nv / c2) + ADAM_EPS) + ADAM_WD * w_ref[...])
        nm_ref[...] = nm
        nv_ref[...] = nv

    spec = pl.BlockSpec((tr, c), lambda i: (i, 0))
    return pl.pallas_call(
        body,
        name=name,
        grid=(r // tr,),
        in_specs=[spec] * 4,
        out_specs=[spec] * 4,
        out_shape=[jax.ShapeDtypeStruct((r, c), F32)] * 4,
        compiler_params=_params("parallel"),
    )(w, g, m, v)


def adamw_small(name, ws, gs, ms, vs):
    n = len(ws)
    c1 = 1.0 - ADAM_B1 ** ADAM_STEP
    c2 = 1.0 - ADAM_B2 ** ADAM_STEP

    def body(*refs):
        for i in range(n):
            w_ref, g_ref, m_ref, v_ref = (refs[j * n + i] for j in range(4))
            d_ref, nm_ref, nv_ref = (refs[(4 + j) * n + i] for j in range(3))
            gv = g_ref[...]
            nm = ADAM_B1 * m_ref[...] + (1.0 - ADAM_B1) * gv
            nv = ADAM_B2 * v_ref[...] + (1.0 - ADAM_B2) * (gv * gv)
            d_ref[...] = -ADAM_LR * ((nm / c1) / (jnp.sqrt(nv / c2) + ADAM_EPS) + ADAM_WD * w_ref[...])
            nm_ref[...] = nm
            nv_ref[...] = nv

    whole = pl.BlockSpec(memory_space=pltpu.VMEM)
    res = pl.pallas_call(
        body,
        name=name,
        in_specs=[whole] * (4 * n),
        out_specs=[whole] * (3 * n),
        out_shape=[jax.ShapeDtypeStruct(w.shape, F32) for w in ws] * 3,
        compiler_params=pltpu.CompilerParams(vmem_limit_bytes=VMEM_LIMIT_BYTES),
    )(*ws, *gs, *ms, *vs)
    return res[:n], res[n:2 * n], res[2 * n:]


def _pack(arrays, row_multiple, cols=BLOB_COLS):
    flat = jnp.concatenate([a.reshape(-1).astype(F32) for a in arrays])
    rows = -(-flat.shape[0] // cols)
    rows = -(-rows // row_multiple) * row_multiple
    return jnp.pad(flat, (0, rows * cols - flat.shape[0])).reshape(rows, cols)


def _unpack(blob, shapes):
    flat, out, off = blob.reshape(-1), [], 0
    for s in shapes:
        size = math.prod(s)
        out.append(flat[off:off + size].reshape(s))
        off += size
    return out


def _unpack_rows(blobs, shapes):
    out, off = [], 0
    for s in shapes:
        size = math.prod(s)
        out.append(blobs[:, off:off + size].reshape((blobs.shape[0],) + tuple(s)))
        off += size
    return out


def kernel(x, p, lru_w_in, lru_conv_w, lru_conv_b, lru_wa, lru_ba, lru_wx, lru_bx, lru_lambda, lru_w_out, pool_w_in, pool_w_grp, pool_b_grp, pool_scale, pool_w_out, ln_mix_g, ln_mix_b, mlp_w1, mlp_w2, ln_mlp_g, ln_mlp_b, ple_w, ple_gate_w, ple_gate_b, loss_target, m_lru_w_in, m_lru_conv_w, m_lru_conv_b, m_lru_wa, m_lru_ba, m_lru_wx, m_lru_bx, m_lru_lambda, m_lru_w_out, m_pool_w_in, m_pool_w_grp, m_pool_b_grp, m_pool_scale, m_pool_w_out, m_ln_mix_g, m_ln_mix_b, m_mlp_w1, m_mlp_w2, m_ln_mlp_g, m_ln_mlp_b, m_ple_w, m_ple_gate_w, m_ple_gate_b, v_lru_w_in, v_lru_conv_w, v_lru_conv_b, v_lru_wa, v_lru_ba, v_lru_wx, v_lru_bx, v_lru_lambda, v_lru_w_out, v_pool_w_in, v_pool_w_grp, v_pool_b_grp, v_pool_scale, v_pool_w_out, v_ln_mix_g, v_ln_mix_b, v_mlp_w1, v_mlp_w2, v_ln_mlp_g, v_ln_mlp_b, v_ple_w, v_ple_gate_w, v_ple_gate_b):
    weights = dict(lru_w_in=lru_w_in, lru_conv_w=lru_conv_w, lru_conv_b=lru_conv_b, lru_wa=lru_wa, lru_ba=lru_ba, lru_wx=lru_wx, lru_bx=lru_bx, lru_lambda=lru_lambda, lru_w_out=lru_w_out, pool_w_in=pool_w_in, pool_w_grp=pool_w_grp, pool_b_grp=pool_b_grp, pool_scale=pool_scale, pool_w_out=pool_w_out, ln_mix_g=ln_mix_g, ln_mix_b=ln_mix_b, mlp_w1=mlp_w1, mlp_w2=mlp_w2, ln_mlp_g=ln_mlp_g, ln_mlp_b=ln_mlp_b, ple_w=ple_w, ple_gate_w=ple_gate_w, ple_gate_b=ple_gate_b)
    mom_m = dict(lru_w_in=m_lru_w_in, lru_conv_w=m_lru_conv_w, lru_conv_b=m_lru_conv_b, lru_wa=m_lru_wa, lru_ba=m_lru_ba, lru_wx=m_lru_wx, lru_bx=m_lru_bx, lru_lambda=m_lru_lambda, lru_w_out=m_lru_w_out, pool_w_in=m_pool_w_in, pool_w_grp=m_pool_w_grp, pool_b_grp=m_pool_b_grp, pool_scale=m_pool_scale, pool_w_out=m_pool_w_out, ln_mix_g=m_ln_mix_g, ln_mix_b=m_ln_mix_b, mlp_w1=m_mlp_w1, mlp_w2=m_mlp_w2, ln_mlp_g=m_ln_mlp_g, ln_mlp_b=m_ln_mlp_b, ple_w=m_ple_w, ple_gate_w=m_ple_gate_w, ple_gate_b=m_ple_gate_b)
    mom_v = dict(lru_w_in=v_lru_w_in, lru_conv_w=v_lru_conv_w, lru_conv_b=v_lru_conv_b, lru_wa=v_lru_wa, lru_ba=v_lru_ba, lru_wx=v_lru_wx, lru_bx=v_lru_bx, lru_lambda=v_lru_lambda, lru_w_out=v_lru_w_out, pool_w_in=v_pool_w_in, pool_w_grp=v_pool_w_grp, pool_b_grp=v_pool_b_grp, pool_scale=v_pool_scale, pool_w_out=v_pool_w_out, ln_mix_g=v_ln_mix_g, ln_mix_b=v_ln_mix_b, mlp_w1=v_mlp_w1, mlp_w2=v_mlp_w2, ln_mlp_g=v_ln_mlp_g, ln_mlp_b=v_ln_mlp_b, ple_w=v_ple_w, ple_gate_w=v_ple_gate_w, ple_gate_b=v_ple_gate_b)
    names = list(weights)

    depth, d = ln_mix_g.shape
    t = x.shape[1]
    n_a, n_b = lru_w_in.shape[0], pool_w_in.shape[0]
    d_rnn = lru_w_out.shape[1] * N_CHIPS
    heads = d_rnn // LRU_BW
    d_ff = mlp_w1.shape[2] * N_CHIPS
    ple_dim = ple_w.shape[1]
    n_grp = len(POOL_WINDOWS)
    gw = d // n_grp
    alpha = (2 * depth) ** 0.25
    chip = 2 * lax.axis_index("x") + lax.axis_index("y")
    place = jnp.stack([chip, lax.axis_index("c")]).astype(jnp.int32)

    x2d = x.reshape(t, d)
    target = loss_target.reshape(t, d)
    p3 = p.reshape(depth, t, ple_dim)

    big = ["lru_w_in", "lru_w_out", "pool_w_in", "pool_w_out", "mlp_w1", "mlp_w2", "ple_w", "ple_gate_w", "pool_w_grp"]
    flat2 = lambda a: a.reshape(-1, a.shape[-1])
    small_sharded = ["lru_conv_w", "pool_b_grp", "pool_scale"]
    small_blob = _pack([weights[k] for k in small_sharded], 16, cols=256)
    every_layer = ("mlp_w1", "mlp_w2", "ple_w", "ple_gate_w")

    def layer_keys(i):
        return (["lru_w_in", "lru_w_out"] if i % 2 == 0 else ["pool_w_in", "pool_w_out", "pool_w_grp"]) + list(every_layer)

    def stage(k, i):
        w = weights[k]
        return into_block(f"stage_l{i}_{k}", flat2(w), i if k in every_layer else i // 2, math.prod(w.shape[1:-1]),
                          place[:1], BF16)

    staged = [[stage(k, i) for k in layer_keys(i)] for i in range(depth)]
    first = all_gather_chips("gather_l0", staged[0][:1] + [into_block("stage_small", small_blob, 0, small_blob.shape[0], place[:1], F32)])
    wg = {(layer_keys(0)[0], 0): first[0]}

    tokens = []

    def take_tokens():
        deps = tuple(tokens)
        tokens.clear()
        return deps

    def mm(*args, **kwargs):
        return matmul(*args, deps=take_tokens(), **kwargs)

    def start_gather(tag, bufs, dep):
        plan = gather_plan(len(bufs))
        flight = (plan,) + split_start(f"gather_{tag}_start", plan, 3 * len(bufs), bufs, dep)
        tokens.append(flight[-1])
        return flight

    def land_gather(tag, flight, keys, layer, after, wait_for=True):
        plan, send_sems, recv_sems, bufs, _ = flight
        landed = split_wait(f"gather_{tag}_wait", plan, send_sems, recv_sems, bufs, after)
        if wait_for:
            wg.update(zip([(k, layer) for k in keys], pair_forward(f"gather_{tag}_forward", landed)))
            return None
        plan = forward_plan(len(landed))
        forwarding = (tag, plan) + split_start(f"gather_{tag}_forward_start", plan, 3 * len(landed), landed, after)
        tokens.append(forwarding[-1])
        return forwarding

    def finish_forward(forwarding, keys, layer, after):
        tag, plan, send_sems, recv_sems, bufs, _ = forwarding
        wg.update(zip([(k, layer) for k in keys], split_wait(f"gather_{tag}_forward_wait", plan, send_sems, recv_sems, bufs, after)))

    conv_w_sh, b_grp_sh, scale_sh = _unpack_rows(first[-1].reshape(N_CHIPS, -1), [weights[k].shape for k in small_sharded])
    conv_w_full = jnp.moveaxis(conv_w_sh, 0, 2).reshape(n_a, CONV_WIDTH, d_rnn)
    b_grp_full = jnp.moveaxis(b_grp_sh, 0, 1).reshape(n_b, 1, d)
    scale_full = jnp.moveaxis(scale_sh, 0, 1).reshape(n_b, 1, d)
    rows_grp = gw // N_CHIPS
    w_grp_full = lambda i: jnp.moveaxis(wg["pool_w_grp", i].reshape(N_CHIPS, n_grp, rows_grp, gw), 0, 1).reshape(n_grp, gw, gw)
    wa_bf, wx_bf = lru_wa.astype(BF16), lru_wx.astype(BF16)
    row = lambda a, i: a[i].reshape(1, -1)

    def ln_after(acc, x_in, g, b):
        s = alpha * x_in + acc
        y = _ln_stats(s)[0] * g + b
        return y, y, s

    ln_outs = [plain(shape=(t, d), dtype=F32), plain(shape=(t, d), dtype=BF16), plain(shape=(t, d), dtype=F32)]
    saved = []
    cur, cur_bf = x2d, x2d
    for i in range(depth):
        slot = i // 2
        sv = dict(x_bf=cur_bf)
        if i == 0:
            flight = start_gather("l0_rest", staged[0][1:], first[0])
        elif i + 1 < depth:
            flight = start_gather(f"l{i + 1}", staged[i + 1], cur)
        if i % 2 == 0:
            (proj,) = mm(f"l{i}_lru_in", plain(cur_bf), colsplit(wg["lru_w_in", i], 0, d), "nn",
                         [colsplit(None, 0, t, n=2, full=(2, t, d_rnn), dtype=F32)])
            hg = lru_fwd(f"l{i}_lru", proj, conv_w_full[slot], row(lru_conv_b, slot), wa_bf[slot], row(lru_ba, slot),
                         wx_bf[slot], row(lru_bx, slot), row(lru_lambda, slot))
            if i == 0:
                land_gather("l0_rest", flight, layer_keys(0)[1:], 0, hg)
                flight = start_gather("l1", staged[1], hg)
            x1, x1_bf, s1 = mm(f"l{i}_lru_out", plain(hg), rowsplit_whole(wg["lru_w_out", i]), "nn", ln_outs, pk=2048,
                               epilogue=ln_after, tiles=[plain(cur)], rows=[row(ln_mix_g, i), row(ln_mix_b, i)])
            sv.update(proj=proj, act=hg)
        else:
            (u,) = mm(f"l{i}_pool_in", plain(cur_bf), rowsplit_whole(wg["pool_w_in", i]), "nn",
                          [plain(shape=(t, d), dtype=F32)])
            zs = pool_fwd(f"l{i}_pool", u, w_grp_full(i), b_grp_full[slot], scale_full[slot])
            x1, x1_bf, s1 = mm(f"l{i}_pool_out", plain(zs), rowsplit_whole(wg["pool_w_out", i]), "nn", ln_outs,
                               epilogue=ln_after, tiles=[plain(cur)], rows=[row(ln_mix_g, i), row(ln_mix_b, i)])
            sv.update(u=u, act=zs)

        def relu2(acc):
            hr = jnp.maximum(acc, 0.0)
            return hr, hr * hr

        hr, hh = mm(f"l{i}_mlp_up", plain(x1_bf), colsplit(wg["mlp_w1", i], 0, d), "nn",
                    [plain(shape=(t, d_ff), dtype=BF16), plain(shape=(t, d_ff), dtype=BF16)], epilogue=relu2, pm=2048)
        (mlp,) = mm(f"l{i}_mlp_down", plain(hh), rowsplit_whole(wg["mlp_w2", i]), "nn",
                    [plain(shape=(t, d), dtype=F32)], pk=d_ff)
        x2, x2_bf, s2 = ln_fwd(f"l{i}_ln_mlp", alpha, x1, mlp, row(ln_mlp_g, i), row(ln_mlp_b, i))
        if i + 1 < depth:
            forwarding = land_gather(f"l{i + 1}", flight, layer_keys(i + 1), i + 1, x2_bf, wait_for=False)

        def ple_out(acc, x2_t, gb, p_t, pw):
            e_t = jnp.concatenate([jnp.dot(p_t.astype(BF16), pw[k], preferred_element_type=F32) for k in range(N_CHIPS)], axis=1)
            gate = jax.nn.sigmoid(acc + gb)
            x3 = x2_t + e_t * gate
            return x3, x3, gate, e_t

        cur, cur_bf, gate, e = mm(
            f"l{i}_ple_gate", plain(x2_bf), rowsplit_whole(wg["ple_gate_w", i]), "nn",
            [plain(shape=(t, d), dtype=F32), plain(shape=(t, d), dtype=BF16), plain(shape=(t, d), dtype=F32), plain(shape=(t, d), dtype=F32)],
            epilogue=ple_out, tiles=[plain(x2)], rows=[row(ple_gate_b, i)], pm=512,
            side=[(p3[i], lambda tm, tn: pl.BlockSpec((tm, ple_dim), lambda r, c, kk: (r, 0))),
                  (wg["ple_w", i], lambda tm, tn: pl.BlockSpec(wg["ple_w", i].shape, lambda r, c, kk: (0, 0, 0)))])
        sv.update(s1=s1, x1_bf=x1_bf, hr=hr, hh=hh, s2=s2, x2_bf=x2_bf, gate=gate, e=e)
        saved.append(sv)
        if i + 1 < depth:
            finish_forward(forwarding, layer_keys(i + 1), i + 1, cur)

    dy, loss_part = loss_head("loss", cur, target)
    loss = lax.psum(loss_part.reshape(()), ("x", "y", "c"))

    part = {}
    sums = {}

    def grad_view(key, split):
        w = weights[key]
        return split(None, 0, w.shape[1], full=(N_CHIPS, w.shape[1], w.shape[2]), dtype=BF16)

    def group_start(tag, items, dep):
        srcs = [part[it] for it in items]
        plan = pair_plan(len(srcs))
        lands = [lax.empty(s.shape, s.dtype) for s in pair_lands(srcs)]
        flight = (tag, items, plan) + split_start(f"grads_{tag}_pair_start", plan, len(srcs), srcs + lands, dep)
        tokens.append(flight[-1])
        return flight

    def group_mid(flight, after):
        tag, items, plan, send_sems, recv_sems, bufs, _ = flight
        bufs = split_wait(f"grads_{tag}_pair_wait", plan, send_sems, recv_sems, bufs, after)
        n = len(items)
        parts = [pair_sum(f"grads_{tag}_pair_sum_{j}", bufs[j], bufs[n + j], place[1:], F32 if it[0] == "blob" else BF16)
                 for j, it in enumerate(items)]
        plan = chip_plan(n)
        flight = (tag, items, plan) + split_start(f"grads_{tag}_chip_start", plan, 3 * n,
                                                  parts + [lax.empty(q.shape, q.dtype) for q in parts], after)
        tokens.append(flight[-1])
        return flight

    def group_end(flight, after):
        tag, items, plan, send_sems, recv_sems, bufs, _ = flight
        bufs = split_wait(f"grads_{tag}_chip_wait", plan, send_sems, recv_sems, bufs, after)
        n = len(items)
        for j, (k, layer) in enumerate(items):
            if k == "blob":
                sums[k] = chip_sum(f"grads_{tag}_chip_sum_{j}", bufs[n + j], bufs[j], place, True)
            else:
                sums[k] = chip_sum(f"grads_{tag}_chip_sum_{j}", bufs[n + j], bufs[j], place, False, into=sums.get(k),
                                   layer=layer if k in every_layer else layer // 2, n_layers=weights[k].shape[0])

    big_w = [k for k in big if k != "pool_w_grp"]
    small_keys = [k for k in names if k not in big_w]

    def ln_before(ca):
        def back(acc, upstream, s, g):
            dx = ca * upstream + acc
            xhat, rstd = _ln_stats(s)
            dxh = dx * g
            ds = rstd * (dxh - jnp.mean(dxh, axis=-1, keepdims=True) - xhat * jnp.mean(dxh * xhat, axis=-1, keepdims=True))
            return ds, ds, jnp.sum(dx * xhat, axis=0, keepdims=True), jnp.sum(dx, axis=0, keepdims=True)

        return back

    ds_outs = [plain(shape=(t, d), dtype=F32), plain(shape=(t, d), dtype=BF16)]
    small = {k: [None] * weights[k].shape[0] for k in names if k not in big or k == "pool_w_grp"}
    dcur = dy
    mlp_pair = mlp_chip = mix_pair = mix_chip = None
    for i in reversed(range(depth)):
        slot = i // 2
        sv = saved[i]
        de, dpre, dgb = ple_bwd(f"l{i}_ple_bwd", dcur, sv["gate"], sv["e"])
        small["ple_gate_b"][i] = dgb
        (part["ple_w", i],) = mm(f"l{i}_d_ple_w", plain(p3[i]), plain(de), "tn", [grad_view("ple_w", colsplit)])
        (part["ple_gate_w", i],) = mm(f"l{i}_d_ple_gate_w", plain(sv["x2_bf"]), plain(dpre), "tn",
                                          [grad_view("ple_gate_w", rowsplit)])
        ds2, ds2_bf, dg, db = mm(f"l{i}_dx2", plain(dpre), rowsplit_whole(wg["ple_gate_w", i]), "nt", ds_outs, col_sums=2, pm=512,
                                 epilogue=ln_before(1.0), tiles=[plain(dcur), plain(sv["s2"])], rows=[row(ln_mlp_g, i)])
        small["ln_mlp_g"][i], small["ln_mlp_b"][i] = dg, db
        (part["mlp_w2", i],) = mm(f"l{i}_d_mlp_w2", plain(sv["hh"]), plain(ds2_bf), "tn", [grad_view("mlp_w2", rowsplit)])
        (dhpre,) = mm(f"l{i}_dh", plain(ds2_bf), rowsplit(wg["mlp_w2", i], 0, d_ff // N_CHIPS), "nt",
                      [plain(shape=(t, d_ff), dtype=BF16)], tiles=[plain(sv["hr"])], pm=2048,
                      epilogue=lambda acc, hr_t: (acc * (2.0 * hr_t.astype(F32)),))
        (part["mlp_w1", i],) = mm(f"l{i}_d_mlp_w1", plain(sv["x1_bf"]), plain(dhpre), "tn", [grad_view("mlp_w1", colsplit)])
        if mlp_chip is not None:
            group_end(mlp_chip, dhpre)
        if mix_pair is not None:
            mix_chip = group_mid(mix_pair, dhpre)
        mlp_pair = group_start(f"l{i}_mlp", [(k, i) for k in every_layer], dhpre)
        (dx1b,) = mm(f"l{i}_dx1", plain(dhpre), colsplit(wg["mlp_w1", i], 0, d), "nt", [plain(shape=(t, d), dtype=F32)],
                     pm=2048)
        ds1, ds1_bf, dg, db = ln_bwd(f"l{i}_ln_mix_bwd", alpha, ds2, dx1b, sv["s1"], row(ln_mix_g, i))
        small["ln_mix_g"][i], small["ln_mix_b"][i] = dg, db
        residual = lambda acc, ds_t: (alpha * ds_t + acc,)
        if i % 2 == 0:
            (part["lru_w_out", i],) = mm(f"l{i}_d_lru_out", plain(sv["act"]), plain(ds1_bf), "tn",
                                             [grad_view("lru_w_out", rowsplit)])
            (dhg,) = mm(f"l{i}_dhg", plain(ds1_bf), rowsplit_whole(wg["lru_w_out", i]), "nt",
                            [plain(shape=(t, d_rnn), dtype=F32)], pn=2048)
            mlp_chip = group_mid(mlp_pair, dhg)
            dproj, dcw, dcb, dba, dbx, dlam, dwa, dwx = lru_bwd(
                f"l{i}_lru_bwd", sv["proj"], dhg, conv_w_full[slot], row(lru_conv_b, slot), wa_bf[slot], row(lru_ba, slot),
                wx_bf[slot], row(lru_bx, slot), row(lru_lambda, slot), deps=take_tokens())
            for key, val in (("lru_conv_w", dcw), ("lru_conv_b", dcb), ("lru_ba", dba), ("lru_bx", dbx),
                             ("lru_lambda", dlam), ("lru_wa", dwa), ("lru_wx", dwx)):
                small[key][slot] = val
            dproj_v = colsplit(dproj, 0, t, n=2)
            (part["lru_w_in", i],) = mm(f"l{i}_d_lru_in", plain(sv["x_bf"]), dproj_v, "tn", [grad_view("lru_w_in", colsplit)])
            (dcur,) = mm(f"l{i}_dx", dproj_v, colsplit(wg["lru_w_in", i], 0, d), "nt",
                             [plain(shape=(t, d), dtype=F32)], epilogue=residual, tiles=[plain(ds1)])
        else:
            (part["pool_w_out", i],) = mm(f"l{i}_d_pool_out", plain(sv["act"]), plain(ds1_bf), "tn",
                                              [grad_view("pool_w_out", rowsplit)])
            (dzs,) = mm(f"l{i}_dzs", plain(ds1_bf), rowsplit_whole(wg["pool_w_out", i]), "nt",
                            [plain(shape=(t, d), dtype=F32)])
            mlp_chip = group_mid(mlp_pair, dzs)
            du, dwg, dbg, dsc = pool_bwd(f"l{i}_pool_bwd", sv["u"], dzs, w_grp_full(i), b_grp_full[slot], scale_full[slot],
                                         deps=take_tokens())
            small["pool_w_grp"][slot], small["pool_b_grp"][slot], small["pool_scale"][slot] = dwg, dbg, dsc
            (part["pool_w_in", i],) = mm(f"l{i}_d_pool_in", plain(sv["x_bf"]), plain(du), "tn", [grad_view("pool_w_in", rowsplit)])
            (dcur,) = mm(f"l{i}_dx", plain(du), rowsplit_whole(wg["pool_w_in", i]), "nt",
                             [plain(shape=(t, d), dtype=F32)], epilogue=residual, tiles=[plain(ds1)])
        if mix_chip is not None:
            group_end(mix_chip, dcur)
        mixer = [(k, i) for k in layer_keys(i) if k not in every_layer and k != "pool_w_grp"]
        if i == 0:
            small_full = [jnp.stack(small[k]).reshape((weights[k].shape[0],) + tuple(
                s * (N_CHIPS if ax in _sharded_axis(k) else 1) for ax, s in enumerate(weights[k].shape[1:], 1))) for k in small_keys]
            blob = _pack(small_full, 64)
            part["blob", 0] = blob.reshape(N_CHIPS, blob.shape[0] // N_CHIPS, BLOB_COLS)
            mixer.append(("blob", 0))
        mix_pair = group_start(f"l{i}_mix", mixer, dcur)
    grad_x = dcur.reshape(x.shape)
    mix_chip = group_mid(mix_pair, dcur)
    group_end(mlp_chip, mix_chip[-1])

    grads, delta, new_m, new_v = {}, {}, {}, {}

    def halves_start(tag, keys, dep):
        layers = [1 if k == "blob" else weights[k].shape[0] for k in keys]
        plan = pair_gather_plan([k == "blob" for k in keys], layers)
        return (keys, plan) + split_start(f"grads_pair_gather_{tag}_start", plan, sum(layers), [sums[k] for k in keys], dep)

    def halves_end(tag, flight, after):
        keys, plan, send_sems, recv_sems, bufs, _ = flight
        return dict(zip(keys, split_wait(f"grads_pair_gather_{tag}_wait", plan, send_sems, recv_sems, bufs, after)))

    def update(k, g):
        dl, nm, nv, g = adamw("adamw_" + k, flat2(weights[k]), g, flat2(mom_m[k]), flat2(mom_v[k]))
        delta[k], new_m[k], new_v[k], grads[k] = (a.reshape(weights[k].shape) for a in (dl, nm, nv, g))

    last = [k for k, _ in mix_pair[1]]
    early = [k for k in big_w if k not in last]
    large = [k for k in early if k in ("mlp_w1", "mlp_w2")]
    little = [k for k in early if k not in large]
    little_flight = halves_start("little", little, mix_chip[-1])
    up_flight = halves_start("large0", large[:1], little_flight[-1])
    down_flight = halves_start("large1", large[1:], up_flight[-1])
    whole = halves_end("little", little_flight, down_flight[-1])
    for k in little:
        update(k, whole[k])
    whole.update(halves_end("large0", up_flight, delta[little[-1]]))
    update(large[0], whole[large[0]])
    group_end(mix_chip, delta[large[0]])
    whole.update(zip(last, pair_gather("grads_pair_gather_last", [sums[k] for k in last], [k == "blob" for k in last],
                                       [1 if k == "blob" else weights[k].shape[0] for k in last])))
    plan = spread_plan(1)
    send_sems, recv_sems, spreading, spread_token = split_start("gather_small_grads_start", plan, 3, [whole["blob"]], delta[large[0]])
    whole.update(halves_end("large1", down_flight, spread_token))
    for k in large[1:] + [k for k in last if k != "blob"]:
        update(k, whole[k])
    (blob_all,) = split_wait("gather_small_grads_wait", plan, send_sems, recv_sems, spreading, delta[large[1]])
    small_grads = dict(zip(small_keys, _unpack(blob_all.reshape(blob.shape), [a.shape for a in small_full])))
    for k in small_keys:
        for ax in _sharded_axis(k):
            n = weights[k].shape[ax]
            small_grads[k] = lax.dynamic_slice_in_dim(small_grads[k], chip * n, n, axis=ax)
    grads.update(small_grads)
    dl, nm, nv = adamw_small("adamw_small", *[[flat2(src[k]) for k in small_keys] for src in (weights, grads, mom_m, mom_v)])
    for out, res in ((delta, dl), (new_m, nm), (new_v, nv)):
        out.update({k: a.reshape(weights[k].shape) for k, a in zip(small_keys, res)})

    return (loss, grad_x, *[grads[k] for k in names], *[delta[k] for k in names],
            *[new_m[k] for k in names], *[new_v[k] for k in names])


def _sharded_axis(key):
    return {"lru_conv_w": (2,), "pool_w_grp": (2,), "pool_b_grp": (1,), "pool_scale": (1,)}.get(key, ())
```

```python
import functools
import math

import jax
import jax.numpy as jnp
from jax import lax
from jax.experimental import pallas as pl
from jax.experimental.pallas import tpu as pltpu

F32 = jnp.float32
BF16 = jnp.bfloat16

N_CHIPS = 4
LRU_BW = 128
LRU_C = 8.0
CONV_WIDTH = 4
POOL_WINDOWS = (2, 4, 8, 16)
POOL_HALO = 16
CONV_HALO = 8
SEQ_CHUNK = 512
LN_EPS = 1e-5
ADAM_LR = 0.001
ADAM_B1 = 0.9
ADAM_B2 = 0.999
ADAM_EPS = 1e-08
ADAM_WD = 0.01
ADAM_STEP = 10
GELU_C = math.sqrt(2.0 / math.pi)
GELU_K = 0.044715
VMEM_LIMIT_BYTES = 56 * 1024 * 1024
MATMUL_TILE_BYTES = 44 * 1024 * 1024
MESH = pl.DeviceIdType.MESH
BLOB_COLS = 1024


def _params(*sem):
    return pltpu.CompilerParams(dimension_semantics=tuple(sem), vmem_limit_bytes=VMEM_LIMIT_BYTES)


def _tile(unit, pref, align=128):
    if unit <= pref:
        return unit
    for d in range(2, unit + 1):
        if unit % d == 0 and unit // d <= pref and (unit // d) % align == 0:
            return unit // d
    raise ValueError((unit, pref, align))


class View:
    def __init__(self, arr, shape, row_unit, col_unit, block_fn, full=None, dtype=None):
        self.arr, self.shape, self.row_unit, self.col_unit, self.block_fn = arr, shape, row_unit, col_unit, block_fn
        self.full = full if full is not None else arr.shape
        self.dtype = dtype if dtype is not None else arr.dtype

    def spec(self, tr, tc, f):
        block, idx = self.block_fn(tr, tc)
        return pl.BlockSpec(block, lambda *g: idx(*f(*g)))


def plain(arr=None, shape=None, dtype=None):
    shape = arr.shape if arr is not None else shape
    return View(arr, shape, shape[0], shape[1], lambda tr, tc: ((tr, tc), lambda rt, ct: (rt, ct)), full=shape, dtype=dtype)


def colsplit(arr, layer, rows, n=N_CHIPS, full=None, dtype=None):
    full = arr.shape if arr is not None else full
    c = full[2]

    def block_fn(tr, tc):
        assert rows % tr == 0 and c % tc == 0, (rows, tr, c, tc)
        per, rpl = c // tc, rows // tr
        return (None, tr, tc), lambda rt, ct: (ct // per, layer * rpl + rt, ct % per)

    return View(arr, (rows, n * c), rows, c, block_fn, full=full, dtype=dtype)


def rowsplit(arr, layer, rows, n=N_CHIPS, full=None, dtype=None):
    full = arr.shape if arr is not None else full
    c = full[2]

    def block_fn(tr, tc):
        assert rows % tr == 0 and c % tc == 0, (rows, tr, c, tc)
        per = rows // tr
        return (None, tr, tc), lambda rt, ct: (rt // per, layer * per + rt % per, ct)

    return View(arr, (n * rows, c), rows, c, block_fn, full=full, dtype=dtype)


def rowsplit_whole(arr):
    n, rows, c = arr.shape

    def block_fn(tr, tc):
        assert tr == n * rows and c % tc == 0, (tr, n, rows, c, tc)
        return (n, rows, tc), lambda rt, ct: (0, 0, ct)

    return View(arr, (n * rows, c), n * rows, c, block_fn)


def matmul(name, a, b, mode, outs, epilogue=None, tiles=(), rows=(), side=(), deps=(), col_sums=0, pm=1024, pn=1024, pk=1024):
    if mode == "nn":
        (m, k), (k2, n) = a.shape, b.shape
        um, uk, un = a.row_unit, min(a.col_unit, b.row_unit), b.col_unit
        dims = (((1,), (0,)), ((), ()))
    elif mode == "nt":
        (m, k), (n, k2) = a.shape, b.shape
        um, uk, un = a.row_unit, min(a.col_unit, b.col_unit), b.row_unit
        dims = (((1,), (1,)), ((), ()))
    else:
        (k, m), (k2, n) = a.shape, b.shape
        um, uk, un = a.col_unit, min(a.row_unit, b.row_unit), b.col_unit
        dims = (((0,), (0,)), ((), ()))
    assert k == k2, (name, a.shape, b.shape)
    for o in list(outs) + list(tiles):
        assert o.shape == (m, n), (name, o.shape, m, n)
        um, un = min(um, o.row_unit), min(un, o.col_unit)
    tm, tn, tk = _tile(um, pm), _tile(un, pn), _tile(uk, pk)
    if mode == "tn" and uk == k:
        size = lambda v: jnp.dtype(v.dtype).itemsize
        need = 2 * k * (tm * size(a) + tn * size(b)) + 2 * tm * tn * sum(size(o) for o in outs)
        if need <= MATMUL_TILE_BYTES:
            tk = k
    assert m % tm == 0 and n % tn == 0 and k % tk == 0, (name, m, n, k, tm, tn, tk)
    gm, gn, gk = m // tm, n // tn, k // tk
    assert not col_sums or gn == 1, (name, gn)

    if mode == "nn":
        a_spec = a.spec(tm, tk, lambda i, j, kk: (i, kk))
        b_spec = b.spec(tk, tn, lambda i, j, kk: (kk, j))
    elif mode == "nt":
        a_spec = a.spec(tm, tk, lambda i, j, kk: (i, kk))
        b_spec = b.spec(tn, tk, lambda i, j, kk: (j, kk))
    else:
        a_spec = a.spec(tk, tm, lambda i, j, kk: (kk, i))
        b_spec = b.spec(tk, tn, lambda i, j, kk: (kk, j))
    tile_specs = [t.spec(tm, tn, lambda i, j, kk: (i, j)) for t in tiles]
    row_specs = [pl.BlockSpec((1, tn), lambda i, j, kk: (0, j)) for _ in rows] + [spec(tm, tn) for _, spec in side]
    rows = list(rows) + [arr for arr, _ in side]
    in_place = [o for o in outs if o.arr is not None]
    alias_specs = [pl.BlockSpec(memory_space=pl.ANY) for _ in in_place]
    out_specs = [o.spec(tm, tn, lambda i, j, kk: (i, j)) for o in outs]
    n_in = 2 + len(tiles) + len(rows)
    aliases = {}
    for o_idx, o in enumerate(outs):
        if o.arr is not None:
            aliases[n_in + in_place.index(o)] = o_idx
    n_t, n_r, n_a, n_o = len(tiles), len(rows), len(in_place) + len(deps), len(outs)
    dep_specs = [pl.BlockSpec(memory_space=pl.ANY) for _ in deps]

    def body(*refs):
        a_ref, b_ref = refs[0], refs[1]
        tile_refs = refs[2:2 + n_t]
        row_refs = refs[2 + n_t:2 + n_t + n_r]
        out_refs = refs[2 + n_t + n_r + n_a:2 + n_t + n_r + n_a + n_o]
        sum_refs = refs[2 + n_t + n_r + n_a + n_o:2 + n_t + n_r + n_a + n_o + col_sums]
        acc_ref = refs[-1] if gk > 1 else None

        def finish(acc):
            extra = [t[...] for t in tile_refs] + [r[...] for r in row_refs]
            res = epilogue(acc, *extra) if epilogue is not None else (acc,)
            for o_ref, r in zip(out_refs, res):
                o_ref[...] = r.astype(o_ref.dtype)
            for s_ref, r in zip(sum_refs, res[n_o:]):
                _accumulate(pl.program_id(0), s_ref, r)

        b_tile = b_ref[...]
        b_tile = b_tile.reshape(-1, b_tile.shape[-1])
        prod = lax.dot_general(a_ref[...].astype(BF16), b_tile.astype(BF16), dims, preferred_element_type=F32)
        if gk == 1:
            finish(prod)
        else:
            kk = pl.program_id(2)

            @pl.when(kk == 0)
            def _():
                acc_ref[...] = prod

            @pl.when(kk > 0)
            def _():
                acc_ref[...] += prod

            @pl.when(kk == gk - 1)
            def _():
                finish(acc_ref[...])

    res = pl.pallas_call(
        body,
        name=name,
        grid=(gm, gn, gk),
        in_specs=[a_spec, b_spec] + tile_specs + row_specs + alias_specs + dep_specs,
        out_specs=out_specs + [pl.BlockSpec((1, tn), lambda i, j, kk: (0, 0))] * col_sums,
        out_shape=[jax.ShapeDtypeStruct(o.full, o.dtype) for o in outs] + [jax.ShapeDtypeStruct((1, n), F32)] * col_sums,
        scratch_shapes=[pltpu.VMEM((tm, tn), F32)] if gk > 1 else [],
        input_output_aliases=aliases,
        compiler_params=_params(*(["arbitrary"] * 3 if col_sums else ["parallel", "parallel", "arbitrary"])),
    )(a.arr, b.arr, *[t.arr for t in tiles], *rows, *[o.arr for o in in_place], *deps)
    return res


def rows_call(name, fn, tiled, vecs, tiled_out, acc_out, tr=512):
    t = tiled[0].shape[0]
    tr = min(tr, t)
    assert t % tr == 0
    n1, n2, n3 = len(tiled), len(vecs), len(tiled_out)

    def body(*refs):
        fn(pl.program_id(0), refs[:n1], refs[n1:n1 + n2], refs[n1 + n2:n1 + n2 + n3], refs[n1 + n2 + n3:])

    return pl.pallas_call(
        body,
        name=name,
        grid=(t // tr,),
        in_specs=[pl.BlockSpec((tr, x.shape[1]), lambda i: (i, 0)) for x in tiled]
        + [pl.BlockSpec(v.shape, lambda i: (0, 0)) for v in vecs],
        out_specs=[pl.BlockSpec((tr, c), lambda i: (i, 0)) for c, _ in tiled_out]
        + [pl.BlockSpec(s, lambda i: (0, 0)) for s, _ in acc_out],
        out_shape=[jax.ShapeDtypeStruct((t, c), d) for c, d in tiled_out] + [jax.ShapeDtypeStruct(s, d) for s, d in acc_out],
        compiler_params=_params("arbitrary" if acc_out else "parallel"),
    )(*tiled, *vecs)


def _accumulate(step, ref, val):
    @pl.when(step == 0)
    def _():
        ref[...] = val

    @pl.when(step > 0)
    def _():
        ref[...] += val


def _ln_stats(s):
    mu = jnp.mean(s, axis=-1, keepdims=True)
    d = s - mu
    var = jnp.mean(d * d, axis=-1, keepdims=True)
    rstd = lax.rsqrt(var + LN_EPS)
    return d * rstd, rstd


def ln_fwd(name, alpha, x_in, m, g, b):
    d = x_in.shape[1]

    def fn(step, tiled, vecs, outs, accs):
        s = alpha * tiled[0][...] + tiled[1][...]
        xhat, _ = _ln_stats(s)
        y = xhat * vecs[0][...] + vecs[1][...]
        outs[0][...] = y
        outs[1][...] = y.astype(BF16)
        outs[2][...] = s

    return rows_call(name, fn, [x_in, m], [g, b], [(d, F32), (d, BF16), (d, F32)], [])


def ln_bwd(name, ca, da, db, s, g):
    d = s.shape[1]

    def fn(step, tiled, vecs, outs, accs):
        dx = ca * tiled[0][...] + tiled[1][...]
        xhat, rstd = _ln_stats(tiled[2][...])
        dxh = dx * vecs[0][...]
        ds = rstd * (dxh - jnp.mean(dxh, axis=-1, keepdims=True) - xhat * jnp.mean(dxh * xhat, axis=-1, keepdims=True))
        outs[0][...] = ds
        outs[1][...] = ds.astype(BF16)
        _accumulate(step, accs[0], jnp.sum(dx * xhat, axis=0, keepdims=True))
        _accumulate(step, accs[1], jnp.sum(dx, axis=0, keepdims=True))

    return rows_call(name, fn, [da, db, s], [g], [(d, F32), (d, BF16)], [((1, d), F32), ((1, d), F32)])


def ple_bwd(name, dx3, gate, e):
    d = dx3.shape[1]

    def fn(step, tiled, vecs, outs, accs):
        dx, gt, ev = tiled[0][...], tiled[1][...], tiled[2][...]
        dpre = dx * ev * gt * (1.0 - gt)
        outs[0][...] = (dx * gt).astype(BF16)
        outs[1][...] = dpre.astype(BF16)
        _accumulate(step, accs[0], jnp.sum(dpre, axis=0, keepdims=True))

    return rows_call(name, fn, [dx3, gate, e], [], [(d, BF16), (d, BF16)], [((1, d), F32)])


def loss_head(name, y, target):
    t, d = y.shape

    def fn(step, tiled, vecs, outs, accs):
        err = tiled[0][...] - tiled[1][...]
        outs[0][...] = err * (1.0 / d)
        part = jnp.sum(jnp.sum(err * err, axis=1, keepdims=True), axis=0, keepdims=True) * (0.5 / d)
        _accumulate(step, accs[0], part)

    return rows_call(name, fn, [y, target], [], [(d, F32)], [((1, 1), F32)])


def _softplus(z):
    return jnp.maximum(z, 0.0) + jnp.log1p(jnp.exp(-jnp.abs(z)))


def _sqrt_and_inverse(z):
    inv = lax.rsqrt(jnp.maximum(z, 1e-30))
    return z * inv, inv


def _gelu(y):
    th = jnp.tanh(GELU_C * (y + GELU_K * (y * y * y)))
    cdf = 0.5 * (1.0 + th)
    return y * cdf, cdf + 0.5 * y * (1.0 - th * th) * (GELU_C * (1.0 + 3.0 * GELU_K * y * y))


def _up(win, k):
    return pltpu.roll(win, win.shape[0] - k, 0)


def _down(win, k):
    return pltpu.roll(win, k, 0)


def _lru_gates(win, row0, cw_ref, cb, wa, ba, wx, bx, sp):
    h = CONV_HALO
    u = (cb + cw_ref[3:4, :] * win[h:] + cw_ref[2:3, :] * _down(win, 1)[h:]
         + cw_ref[1:2, :] * _down(win, 2)[h:] + cw_ref[0:1, :] * _down(win, 3)[h:])
    ub = u.astype(BF16)
    r = jax.nn.sigmoid(jnp.dot(ub, wa, preferred_element_type=F32) + ba)
    ig = jax.nn.sigmoid(jnp.dot(ub, wx, preferred_element_type=F32) + bx)
    log_a = (-LRU_C) * r * sp
    a = jnp.exp(log_a)
    mult = _sqrt_and_inverse(-jnp.tanh(log_a) * (a * a + 1.0))[0]
    first = (row0 + lax.broadcasted_iota(jnp.int32, u.shape, 0)) == 0
    mult = jnp.where(first, 1.0, mult)
    return u, r, ig, a, mult, first


def _block_scan(a, b, reverse):
    n = a.shape[0]
    a, b = a.reshape(n // 8, 8, LRU_BW), b.reshape(n // 8, 8, LRU_BW)
    pos = lax.broadcasted_iota(jnp.int32, a.shape, 1)
    for s in (1, 2, 4):
        keep = (pos >= 8 - s) if reverse else (pos < s)
        by = 8 - s if reverse else s
        b = jnp.where(keep, b, a * pltpu.roll(b, by, 1) + b)
        a = jnp.where(keep, a, a * pltpu.roll(a, by, 1))
    return a.reshape(n, LRU_BW), b.reshape(n, LRU_BW)


def _carry_scan(a_ref, b_ref, out_ref, out_off, t, reverse):
    groups, per_step = t // 8, 8

    def step(j, h):
        for k in range(per_step):
            g = j * per_step + k
            r0 = pl.multiple_of((groups - 1 - g if reverse else g) * 8, 8)
            edge = r0 if reverse else r0 + 7
            h_out = a_ref[pl.ds(edge, 1), :] * h + b_ref[pl.ds(edge, 1), :]
            out_ref[pl.ds(pl.multiple_of(out_off + r0, 8), 8), :] = a_ref[pl.ds(r0, 8), :] * h + b_ref[pl.ds(r0, 8), :]
            h = h_out
        return h

    lax.fori_loop(0, groups // per_step, step, jnp.zeros((1, LRU_BW), F32))


def _lru_in_specs(t, heads):
    blk = lambda i: (0, i)
    return [
        pl.BlockSpec((2, t, LRU_BW), lambda i: (0, 0, i)),
        pl.BlockSpec((CONV_WIDTH, LRU_BW), blk),
        pl.BlockSpec((1, LRU_BW), blk),
        pl.BlockSpec((None, LRU_BW, LRU_BW), lambda i: (i, 0, 0)),
        pl.BlockSpec((1, LRU_BW), blk),
        pl.BlockSpec((None, LRU_BW, LRU_BW), lambda i: (i, 0, 0)),
        pl.BlockSpec((1, LRU_BW), blk),
        pl.BlockSpec((1, LRU_BW), blk),
    ]


def lru_fwd(name, proj, conv_w, conv_b, wa, ba, wx, bx, lam):
    _, t, c = proj.shape
    heads = c // LRU_BW
    rc = min(SEQ_CHUNK, t)

    def body(proj_ref, cw_ref, cb_ref, wa_ref, ba_ref, wx_ref, bx_ref, lam_ref, out_ref, upad, a_s, b_s):
        upad[0:CONV_HALO, :] = jnp.zeros((CONV_HALO, LRU_BW), F32)
        upad[CONV_HALO:, :] = proj_ref[0]
        sp = _softplus(-lam_ref[...])
        cb, ba, bx, wa, wx = cb_ref[...], ba_ref[...], bx_ref[...], wa_ref[...], wx_ref[...]

        def gates(i, carry):
            r0 = pl.multiple_of(i * rc, rc)
            win = upad[pl.ds(r0, rc + CONV_HALO), :]
            u, r, ig, a, mult, _ = _lru_gates(win, r0, cw_ref, cb, wa, ba, wx, bx, sp)
            rows = pl.ds(r0, rc)
            a_s[rows, :], b_s[rows, :] = _block_scan(a, mult * (ig * u), False)
            return carry

        lax.fori_loop(0, t // rc, gates, 0)
        _carry_scan(a_s, b_s, b_s, 0, t, False)

        def gate_out(i, carry):
            r0 = pl.multiple_of(i * rc, rc)
            gy, _ = _gelu(proj_ref[1, pl.ds(r0, rc), :])
            out_ref[pl.ds(r0, rc), :] = (b_s[pl.ds(r0, rc), :] * gy).astype(BF16)
            return carry

        lax.fori_loop(0, t // rc, gate_out, 0)

    return pl.pallas_call(
        body,
        name=name,
        grid=(heads,),
        in_specs=_lru_in_specs(t, heads),
        out_specs=pl.BlockSpec((t, LRU_BW), lambda i: (0, i)),
        out_shape=jax.ShapeDtypeStruct((t, c), BF16),
        scratch_shapes=[pltpu.VMEM((t + CONV_HALO, LRU_BW), F32)] + [pltpu.VMEM((t, LRU_BW), F32)] * 2,
        compiler_params=_params("parallel"),
    )(proj, conv_w, conv_b, wa, ba, wx, bx, lam)


def lru_bwd(name, proj, dhg, conv_w, conv_b, wa, ba, wx, bx, lam, deps=()):
    _, t, c = proj.shape
    heads = c // LRU_BW
    rc = min(SEQ_CHUNK, t)
    h8 = CONV_HALO

    def body(proj_ref, dhg_ref, cw_ref, cb_ref, wa_ref, ba_ref, wx_ref, bx_ref, lam_ref,
             dproj_ref, dcw_ref, dcb_ref, dba_ref, dbx_ref, dlam_ref, dwa_ref, dwx_ref,
             upad, u_s, r_s, ig_s, apad, hpad, g_s, dupad, sa_s, sb_s):
        zeros8 = jnp.zeros((h8, LRU_BW), F32)
        upad[0:h8, :] = zeros8
        upad[h8:, :] = proj_ref[0]
        hpad[0:h8, :] = zeros8
        apad[t:, :] = zeros8
        dupad[t:, :] = zeros8
        lam = lam_ref[...]
        sp = _softplus(-lam)
        cb, ba, bx, wa, wx = cb_ref[...], ba_ref[...], bx_ref[...], wa_ref[...], wx_ref[...]

        def gates(i, carry):
            r0 = pl.multiple_of(i * rc, rc)
            win = upad[pl.ds(r0, rc + h8), :]
            u, r, ig, a, mult, _ = _lru_gates(win, r0, cw_ref, cb, wa, ba, wx, bx, sp)
            u_s[pl.ds(r0, rc), :] = u
            r_s[pl.ds(r0, rc), :] = r
            ig_s[pl.ds(r0, rc), :] = ig
            rows = pl.ds(r0, rc)
            apad[rows, :] = a
            sa_s[rows, :], sb_s[rows, :] = _block_scan(a, mult * (ig * u), False)
            return carry

        lax.fori_loop(0, t // rc, gates, 0)
        _carry_scan(sa_s, sb_s, hpad, h8, t, False)

        def out_gate(i, carry):
            r0 = pl.multiple_of(i * rc, rc)
            gy, dgy = _gelu(proj_ref[1, pl.ds(r0, rc), :])
            dh = dhg_ref[pl.ds(r0, rc), :]
            hh = hpad[pl.ds(pl.multiple_of(r0 + h8, 8), rc), :]
            dproj_ref[1, pl.ds(r0, rc), :] = (dh * hh * dgy).astype(BF16)
            rows = pl.ds(r0, rc)
            a_next = _up(apad[pl.ds(r0, rc + h8), :], 1)[:rc]
            sa_s[rows, :], sb_s[rows, :] = _block_scan(a_next, dh * gy, True)
            return carry

        lax.fori_loop(0, t // rc, out_gate, 0)
        _carry_scan(sa_s, sb_s, g_s, 0, t, True)

        zrow = jnp.zeros((1, LRU_BW), F32)
        zmat = jnp.zeros((LRU_BW, LRU_BW), F32)

        def grads(i, carry):
            dsp, dba, dbx, dwa, dwx = carry
            r0 = pl.multiple_of(i * rc, rc)
            g = g_s[pl.ds(r0, rc), :]
            u, r, ig, a = u_s[pl.ds(r0, rc), :], r_s[pl.ds(r0, rc), :], ig_s[pl.ds(r0, rc), :], apad[pl.ds(r0, rc), :]
            hprev = _down(hpad[pl.ds(r0, rc + h8), :], 1)[h8:]
            first = (r0 + lax.broadcasted_iota(jnp.int32, u.shape, 0)) == 0
            log_a = (-LRU_C) * r * sp
            mult, inv_mult = _sqrt_and_inverse(-jnp.tanh(log_a) * (a * a + 1.0))
            mult = jnp.where(first, 1.0, mult)
            dmult = jnp.where(first, 0.0, g * (ig * u))
            dlog_a = g * hprev * a - dmult * (a * a) * inv_mult
            dr = dlog_a * ((-LRU_C) * sp)
            dpre_r = dr * r * (1.0 - r)
            dpre_i = (g * mult * u) * ig * (1.0 - ig)
            pr, pi, ub = dpre_r.astype(BF16), dpre_i.astype(BF16), u.astype(BF16)
            nt = (((1,), (1,)), ((), ()))
            tn = (((0,), (0,)), ((), ()))
            du = (g * mult * ig + lax.dot_general(pr, wa, nt, preferred_element_type=F32)
                  + lax.dot_general(pi, wx, nt, preferred_element_type=F32))
            dupad[pl.ds(r0, rc), :] = du
            return (dsp + jnp.sum(dlog_a * ((-LRU_C) * r), axis=0, keepdims=True),
                    dba + jnp.sum(dpre_r, axis=0, keepdims=True),
                    dbx + jnp.sum(dpre_i, axis=0, keepdims=True),
                    dwa + lax.dot_general(ub, pr, tn, preferred_element_type=F32),
                    dwx + lax.dot_general(ub, pi, tn, preferred_element_type=F32))

        dsp, dba, dbx, dwa, dwx = lax.fori_loop(0, t // rc, grads, (zrow, zrow, zrow, zmat, zmat))
        dba_ref[...] = dba
        dbx_ref[...] = dbx
        dwa_ref[...] = dwa
        dwx_ref[...] = dwx
        dlam_ref[...] = -dsp * jax.nn.sigmoid(-lam)

        def conv_back(i, carry):
            dcb, d0, d1, d2, d3 = carry
            r0 = pl.multiple_of(i * rc, rc)
            dwin = dupad[pl.ds(r0, rc + h8), :]
            du = dwin[:rc]
            du0 = (cw_ref[3:4, :] * du + cw_ref[2:3, :] * _up(dwin, 1)[:rc]
                   + cw_ref[1:2, :] * _up(dwin, 2)[:rc] + cw_ref[0:1, :] * _up(dwin, 3)[:rc])
            dproj_ref[0, pl.ds(r0, rc), :] = du0.astype(BF16)
            win = upad[pl.ds(r0, rc + h8), :]
            red = lambda v: jnp.sum(v, axis=0, keepdims=True)
            return (dcb + red(du), d0 + red(du * _down(win, 3)[h8:]), d1 + red(du * _down(win, 2)[h8:]),
                    d2 + red(du * _down(win, 1)[h8:]), d3 + red(du * win[h8:]))

        dcb, d0, d1, d2, d3 = lax.fori_loop(0, t // rc, conv_back, (zrow,) * 5)
        dcb_ref[...] = dcb
        dcw_ref[0:1, :] = d0
        dcw_ref[1:2, :] = d1
        dcw_ref[2:3, :] = d2
        dcw_ref[3:4, :] = d3

    blk = lambda i: (0, i)
    vec = jax.ShapeDtypeStruct((1, c), F32)
    mat = jax.ShapeDtypeStruct((heads, LRU_BW, LRU_BW), F32)
    full = lambda: pltpu.VMEM((t, LRU_BW), F32)
    padded = lambda: pltpu.VMEM((t + h8, LRU_BW), F32)
    return pl.pallas_call(
        lambda *refs: body(*refs[len(deps):]),
        name=name,
        grid=(heads,),
        in_specs=[_ANY] * len(deps) + _lru_in_specs(t, heads)[:1] + [pl.BlockSpec((t, LRU_BW), blk)] + _lru_in_specs(t, heads)[1:],
        out_specs=[pl.BlockSpec((2, t, LRU_BW), lambda i: (0, 0, i)), pl.BlockSpec((CONV_WIDTH, LRU_BW), blk)]
        + [pl.BlockSpec((1, LRU_BW), blk)] * 4 + [pl.BlockSpec((None, LRU_BW, LRU_BW), lambda i: (i, 0, 0))] * 2,
        out_shape=[jax.ShapeDtypeStruct((2, t, c), BF16), jax.ShapeDtypeStruct((CONV_WIDTH, c), F32), vec, vec, vec, vec, mat, mat],
        scratch_shapes=[padded(), full(), full(), full(), padded(), padded(), full(), padded()] + [full()] * 2,
        compiler_params=_params("parallel"),
    )(*deps, proj, dhg, conv_w, conv_b, wa, ba, wx, bx, lam)


def _pick_level(g, levels):
    out = levels[-1]
    for k in range(len(levels) - 2, -1, -1):
        out = jnp.where(g == k, levels[k], out)
    return out


def _pool_z(win, g, row0, rc):
    levels, cur = [], win
    for k in range(len(POOL_WINDOWS)):
        cur = cur + _down(cur, 1 << k)
        levels.append(cur[POOL_HALO:])
    tot = _pick_level(g, levels)
    width = jnp.left_shift(2, g)
    row = row0 + lax.broadcasted_iota(jnp.int32, tot.shape, 0)
    cnt = jnp.minimum(row + 1, width).astype(F32)
    return tot / cnt - win[POOL_HALO:], cnt


def _pool_specs(t, gw):
    blk = lambda g: (0, g)
    return [pl.BlockSpec((t, gw), blk), pl.BlockSpec((None, gw, gw), lambda g: (g, 0, 0)),
            pl.BlockSpec((1, gw), blk), pl.BlockSpec((1, gw), blk)]


def pool_fwd(name, u, w_grp, b_grp, scale):
    t, d = u.shape
    gw = d // len(POOL_WINDOWS)
    rc = min(SEQ_CHUNK, t)

    def body(u_ref, wg_ref, bg_ref, sc_ref, out_ref, upad):
        g = pl.program_id(0)
        upad[0:POOL_HALO, :] = jnp.zeros((POOL_HALO, gw), F32)
        upad[POOL_HALO:, :] = u_ref[...]
        wg, bg, sc = wg_ref[...], bg_ref[...], sc_ref[...]

        def chunk(i, carry):
            r0 = pl.multiple_of(i * rc, rc)
            z, _ = _pool_z(upad[pl.ds(r0, rc + POOL_HALO), :], g, r0, rc)
            z2 = jnp.dot(z.astype(BF16), wg, preferred_element_type=F32) + bg
            out_ref[pl.ds(r0, rc), :] = (z2 * sc).astype(BF16)
            return carry

        lax.fori_loop(0, t // rc, chunk, 0)

    return pl.pallas_call(
        body,
        name=name,
        grid=(len(POOL_WINDOWS),),
        in_specs=_pool_specs(t, gw),
        out_specs=pl.BlockSpec((t, gw), lambda g: (0, g)),
        out_shape=jax.ShapeDtypeStruct((t, d), BF16),
        scratch_shapes=[pltpu.VMEM((t + POOL_HALO, gw), F32)],
        compiler_params=_params("parallel"),
    )(u, w_grp, b_grp, scale)


def pool_bwd(name, u, dzs, w_grp, b_grp, scale, deps=()):
    t, d = u.shape
    gw = d // len(POOL_WINDOWS)
    rc = min(SEQ_CHUNK, t)

    def body(u_ref, dzs_ref, wg_ref, bg_ref, sc_ref, du_ref, dwg_ref, dbg_ref, dsc_ref, upad, qpad, dz_s):
        g = pl.program_id(0)
        upad[0:POOL_HALO, :] = jnp.zeros((POOL_HALO, gw), F32)
        upad[POOL_HALO:, :] = u_ref[...]
        qpad[t:, :] = jnp.zeros((POOL_HALO, gw), F32)
        wg, bg, sc = wg_ref[...], bg_ref[...], sc_ref[...]
        zrow = jnp.zeros((1, gw), F32)

        def chunk(i, carry):
            dsc, dbg, dwg = carry
            r0 = pl.multiple_of(i * rc, rc)
            z, cnt = _pool_z(upad[pl.ds(r0, rc + POOL_HALO), :], g, r0, rc)
            zb = z.astype(BF16)
            z2 = jnp.dot(zb, wg, preferred_element_type=F32) + bg
            dzs = dzs_ref[pl.ds(r0, rc), :]
            dz2 = dzs * sc
            d2b = dz2.astype(BF16)
            dz = lax.dot_general(d2b, wg, (((1,), (1,)), ((), ())), preferred_element_type=F32)
            dz_s[pl.ds(r0, rc), :] = dz
            qpad[pl.ds(r0, rc), :] = dz / cnt
            return (dsc + jnp.sum(dzs * z2, axis=0, keepdims=True), dbg + jnp.sum(dz2, axis=0, keepdims=True),
                    dwg + lax.dot_general(zb, d2b, (((0,), (0,)), ((), ())), preferred_element_type=F32))

        dsc, dbg, dwg = lax.fori_loop(0, t // rc, chunk, (zrow, zrow, jnp.zeros((gw, gw), F32)))
        dsc_ref[...] = dsc
        dbg_ref[...] = dbg
        dwg_ref[...] = dwg

        def spread(i, carry):
            r0 = pl.multiple_of(i * rc, rc)
            levels, cur = [], qpad[pl.ds(r0, rc + POOL_HALO), :]
            for k in range(len(POOL_WINDOWS)):
                cur = cur + _up(cur, 1 << k)
                levels.append(cur[:rc])
            du_ref[pl.ds(r0, rc), :] = (_pick_level(g, levels) - dz_s[pl.ds(r0, rc), :]).astype(BF16)
            return carry

        lax.fori_loop(0, t // rc, spread, 0)

    blk = lambda g: (0, g)
    vec = jax.ShapeDtypeStruct((1, d), F32)
    return pl.pallas_call(
        lambda *refs: body(*refs[len(deps):]),
        name=name,
        grid=(len(POOL_WINDOWS),),
        in_specs=[_ANY] * len(deps) + _pool_specs(t, gw)[:1] + [pl.BlockSpec((t, gw), blk)] + _pool_specs(t, gw)[1:],
        out_specs=[pl.BlockSpec((t, gw), blk), pl.BlockSpec((None, gw, gw), lambda g: (g, 0, 0)),
                   pl.BlockSpec((1, gw), blk), pl.BlockSpec((1, gw), blk)],
        out_shape=[jax.ShapeDtypeStruct((t, d), BF16), jax.ShapeDtypeStruct((len(POOL_WINDOWS), gw, gw), F32), vec, vec],
        scratch_shapes=[pltpu.VMEM((t + POOL_HALO, gw), F32), pltpu.VMEM((t + POOL_HALO, gw), F32), pltpu.VMEM((t, gw), F32)],
        compiler_params=_params("parallel"),
    )(*deps, u, dzs, w_grp, b_grp, scale)


def _place():
    return lax.axis_index("x"), lax.axis_index("y"), lax.axis_index("c")


def _other_chips(x, y):
    return [(1 - x, y), (x, 1 - y), (1 - x, 1 - y)]


def _half(c, rows):
    h = rows // 2
    return pl.ds(pl.multiple_of(c * h, 8), h)


_ANY = pl.BlockSpec(memory_space=pl.ANY)


def into_block(name, shards, layer, r, me, dtype):
    c = shards.shape[1]
    tr = _tile(r, 512, 16)
    per = r // tr

    def body(me_ref, s_ref, o_ref):
        o_ref[...] = s_ref[...].astype(o_ref.dtype)

    return pl.pallas_call(
        body,
        name=name,
        grid_spec=pltpu.PrefetchScalarGridSpec(
            num_scalar_prefetch=1,
            grid=(per,),
            in_specs=[pl.BlockSpec((tr, c), lambda i, me_ref: (layer * per + i, 0))],
            out_specs=pl.BlockSpec((None, tr, c), lambda i, me_ref: (me_ref[0], i, 0)),
        ),
        out_shape=jax.ShapeDtypeStruct((N_CHIPS, r, c), dtype),
        compiler_params=_params("parallel"),
    )(me, shards)


_HBM = pl.BlockSpec(memory_space=pltpu.HBM)
_SEM = pl.BlockSpec(memory_space=pltpu.SEMAPHORE)


def _in_hbm(a):
    return pltpu.with_memory_space_constraint(a, pltpu.HBM)


def split_start(name, plan, n_copies, bufs, dep):
    n = len(bufs)

    def body(*refs):
        for cp in plan(refs[:n], refs[n + 1], refs[n + 2]):
            cp.start()
        refs[-1][...] = jnp.zeros_like(refs[-1])

    res = pl.pallas_call(
        body,
        name=name,
        in_specs=[_HBM] * n + [_ANY],
        out_specs=[_SEM, _SEM] + [_HBM] * n + [pl.BlockSpec(memory_space=pltpu.VMEM)],
        out_shape=[pltpu.SemaphoreType.DMA((n_copies,)), pltpu.SemaphoreType.DMA((n_copies,))]
        + [pltpu.HBM(b.shape, b.dtype) for b in bufs] + [jax.ShapeDtypeStruct((8, 128), F32)],
        input_output_aliases={i: 2 + i for i in range(n)},
        compiler_params=pltpu.CompilerParams(has_side_effects=pltpu.SideEffectType.DATAFLOW_SIDE_EFFECTING),
    )(*[_in_hbm(b) for b in bufs], dep)
    return res[0], res[1], list(res[2:2 + n]), res[-1]


def split_wait(name, plan, send_sems, recv_sems, bufs, after):
    n = len(bufs)

    def body(*refs):
        copies = plan(refs[:n], refs[n], refs[n + 1])
        for cp in copies:
            cp.wait_send()
        for cp in copies:
            cp.wait_recv()

    return pl.pallas_call(
        body,
        name=name,
        in_specs=[_HBM] * n + [_SEM, _SEM, _ANY],
        out_specs=[_HBM] * n,
        out_shape=[pltpu.HBM(b.shape, b.dtype) for b in bufs],
        input_output_aliases={i: i for i in range(n)},
        compiler_params=pltpu.CompilerParams(has_side_effects=pltpu.SideEffectType.DATAFLOW_SIDE_EFFECTING),
    )(*bufs, send_sems, recv_sems, after)


def gather_plan(n):
    def plan(bufs, send_sems, recv_sems):
        x, y, c = _place()
        copies = []
        for i in range(n):
            blk = bufs[i].at[2 * x + y, _half(c, bufs[i].shape[1]), :]
            for j, chip in enumerate(_other_chips(x, y)):
                copies.append(pltpu.make_async_remote_copy(
                    src_ref=blk, dst_ref=blk, send_sem=send_sems.at[3 * i + j], recv_sem=recv_sems.at[3 * i + j],
                    device_id=(*chip, c), device_id_type=MESH))
        return copies

    return plan


def forward_plan(n):
    def plan(bufs, send_sems, recv_sems):
        x, y, c = _place()
        copies = []
        for i in range(n):
            for j, (cx, cy) in enumerate(_other_chips(x, y)):
                blk = bufs[i].at[2 * cx + cy, _half(c, bufs[i].shape[1]), :]
                copies.append(pltpu.make_async_remote_copy(
                    src_ref=blk, dst_ref=blk, send_sem=send_sems.at[3 * i + j], recv_sem=recv_sems.at[3 * i + j],
                    device_id=(x, y, 1 - c), device_id_type=MESH))
        return copies

    return plan


def pair_forward(name, bufs):
    n = len(bufs)

    def body(*refs):
        copies = forward_plan(n)(refs[n:2 * n], refs[2 * n], refs[2 * n + 1])
        for cp in copies:
            cp.start()
        for cp in copies:
            cp.wait()

    return pl.pallas_call(
        body,
        name=name,
        in_specs=[_ANY] * n,
        out_specs=[_ANY] * n,
        out_shape=[jax.ShapeDtypeStruct(b.shape, b.dtype) for b in bufs],
        input_output_aliases={i: i for i in range(n)},
        scratch_shapes=[pltpu.SemaphoreType.DMA((3 * n,)), pltpu.SemaphoreType.DMA((3 * n,))],
    )(*bufs)


def all_gather_chips(name, bufs):
    n = len(bufs)

    def body(*refs):
        outs = refs[n:2 * n]
        send_sems, recv_sems = refs[2 * n:]
        x, y, c = _place()
        me, sibling = 2 * x + y, (x, y, 1 - c)
        chips = _other_chips(x, y)

        def copy(i, slot, block, half, to):
            blk = outs[i].at[block, _half(half, outs[i].shape[1]), :]
            return pltpu.make_async_remote_copy(
                src_ref=blk, dst_ref=blk, send_sem=send_sems.at[i * 6 + slot], recv_sem=recv_sems.at[i * 6 + slot],
                device_id=to, device_id_type=MESH)

        first = [copy(i, j, me, c, (*chip, c)) for i in range(n) for j, chip in enumerate(chips)]
        for cp in first:
            cp.start()
        passed = []
        for i in range(n):
            for j, (cx, cy) in enumerate(chips):
                copy(i, j, 2 * cx + cy, c, (x, y, c)).wait_recv()
                fwd = copy(i, 3 + j, 2 * cx + cy, c, sibling)
                fwd.start()
                passed.append(fwd)
        for i in range(n):
            for j, (cx, cy) in enumerate(chips):
                copy(i, 3 + j, 2 * cx + cy, 1 - c, (x, y, c)).wait_recv()
        for cp in first + passed:
            cp.wait_send()

    return pl.pallas_call(
        body,
        name=name,
        in_specs=[_ANY] * n,
        out_specs=[_ANY] * n,
        out_shape=[jax.ShapeDtypeStruct(b.shape, b.dtype) for b in bufs],
        input_output_aliases={i: i for i in range(n)},
        scratch_shapes=[pltpu.SemaphoreType.DMA((6 * n,)), pltpu.SemaphoreType.DMA((6 * n,))],
    )(*bufs)


def pair_plan(n):
    def plan(bufs, send_sems, recv_sems):
        x, y, c = _place()
        return [pltpu.make_async_remote_copy(
            src_ref=bufs[i].at[:, _half(1 - c, bufs[i].shape[1]), :], dst_ref=bufs[n + i], send_sem=send_sems.at[i],
            recv_sem=recv_sems.at[i], device_id=(x, y, 1 - c), device_id_type=MESH) for i in range(n)]

    return plan


def chip_plan(n):
    def plan(bufs, send_sems, recv_sems):
        x, y, c = _place()
        copies = []
        for i in range(n):
            for j, (cx, cy) in enumerate(_other_chips(x, y)):
                copies.append(pltpu.make_async_remote_copy(
                    src_ref=bufs[i].at[2 * cx + cy], dst_ref=bufs[n + i].at[2 * x + y], send_sem=send_sems.at[3 * i + j],
                    recv_sem=recv_sems.at[3 * i + j], device_id=(cx, cy, c), device_id_type=MESH))
        return copies

    return plan


def pair_lands(grads):
    return [jax.ShapeDtypeStruct((g.shape[0], g.shape[1] // 2, g.shape[2]), g.dtype) for g in grads]


def pair_gather_plan(blocked, layers):
    def plan(bufs, send_sems, recv_sems):
        x, y, c = _place()
        copies = []
        for i in range(len(bufs)):
            buf = bufs[i].at[2 * x + y] if blocked[i] else bufs[i]
            r = buf.shape[0] // layers[i]
            for l in range(layers[i]):
                mine = buf.at[pl.ds(pl.multiple_of(l * r + c * (r // 2), 8), r // 2), :]
                copies.append(pltpu.make_async_remote_copy(
                    src_ref=mine, dst_ref=mine, send_sem=send_sems.at[len(copies)], recv_sem=recv_sems.at[len(copies)],
                    device_id=(x, y, 1 - c), device_id_type=MESH))
        return copies

    return plan


def spread_plan(n):
    def plan(bufs, send_sems, recv_sems):
        x, y, c = _place()
        copies = []
        for i in range(n):
            blk = bufs[i].at[2 * x + y]
            for j, chip in enumerate(_other_chips(x, y)):
                copies.append(pltpu.make_async_remote_copy(
                    src_ref=blk, dst_ref=blk, send_sem=send_sems.at[3 * i + j], recv_sem=recv_sems.at[3 * i + j],
                    device_id=(*chip, c), device_id_type=MESH))
        return copies

    return plan


def pair_gather(name, bufs, blocked, layers):
    n = len(bufs)
    n_copies = sum(layers)

    def body(*refs):
        copies = pair_gather_plan(blocked, layers)(refs[n:2 * n], refs[2 * n], refs[2 * n + 1])
        for cp in copies:
            cp.start()
        for cp in copies:
            cp.wait()

    return pl.pallas_call(
        body,
        name=name,
        in_specs=[_ANY] * n,
        out_specs=[_ANY] * n,
        out_shape=[jax.ShapeDtypeStruct(b.shape, b.dtype) for b in bufs],
        input_output_aliases={i: i for i in range(n)},
        scratch_shapes=[pltpu.SemaphoreType.DMA((n_copies,)), pltpu.SemaphoreType.DMA((n_copies,))],
    )(*bufs)


def pair_sum(name, grad, recv, core, dtype):
    _, r, c = grad.shape
    h = r // 2
    th = _tile(h, 1024, 16)
    per = h // th

    def body(core_ref, g_ref, r_ref, o_ref):
        o_ref[...] = (g_ref[...].astype(F32) + r_ref[...].astype(F32)).astype(o_ref.dtype)

    return pl.pallas_call(
        body,
        name=name,
        grid_spec=pltpu.PrefetchScalarGridSpec(
            num_scalar_prefetch=1,
            grid=(N_CHIPS, per),
            in_specs=[pl.BlockSpec((None, th, c), lambda k, i, core_ref: (k, core_ref[0] * per + i, 0)),
                      pl.BlockSpec((None, th, c), lambda k, i, core_ref: (k, i, 0))],
            out_specs=pl.BlockSpec((None, th, c), lambda k, i, core_ref: (k, i, 0)),
        ),
        out_shape=jax.ShapeDtypeStruct((N_CHIPS, h, c), dtype),
        compiler_params=_params("parallel", "parallel"),
    )(core, grad, recv)


def chip_sum(name, got, parts, place, blocked, into=None, layer=0, n_layers=1):
    _, h, c = parts.shape
    th = _tile(h, 512, 16)
    per = h // th

    def body(place_ref, q0, q1, q2, q3, p_ref, *rest):
        o_ref = rest[-1]
        me = place_ref[0]
        own = p_ref[...].astype(F32)
        v = [jnp.where(me == k, own, q[...].astype(F32)) for k, q in enumerate((q0, q1, q2, q3))]
        o_ref[...] = ((v[0] + v[1]) + v[2]) + v[3]

    def got_spec(k):
        return pl.BlockSpec((None, th, c), lambda i, pr: (jnp.where(pr[0] == k, (k + 1) % N_CHIPS, k), i, 0))

    if blocked:
        out_spec = pl.BlockSpec((None, th, c), lambda i, pr: (pr[0], pr[1] * per + i, 0))
        out_shape = jax.ShapeDtypeStruct((N_CHIPS, 2 * h, c), F32)
    else:
        out_spec = pl.BlockSpec((th, c), lambda i, pr: ((2 * layer + pr[1]) * per + i, 0))
        out_shape = jax.ShapeDtypeStruct((n_layers * 2 * h, c), F32)
    carried = [] if into is None else [into]
    return pl.pallas_call(
        body,
        name=name,
        grid_spec=pltpu.PrefetchScalarGridSpec(
            num_scalar_prefetch=1,
            grid=(per,),
            in_specs=[got_spec(k) for k in range(N_CHIPS)] + [pl.BlockSpec((None, th, c), lambda i, pr: (pr[0], i, 0))]
            + [_ANY] * len(carried),
            out_specs=out_spec,
        ),
        out_shape=out_shape,
        input_output_aliases={6: 0} if carried else {},
        compiler_params=_params("parallel"),
    )(place, got, got, got, got, parts, *carried)


def adamw(name, w, g, m, v):
    r, c = w.shape
    tr = _tile(r, 512, 8)
    c1 = 1.0 - ADAM_B1 ** ADAM_STEP
    c2 = 1.0 - ADAM_B2 ** ADAM_STEP

    def body(w_ref, g_ref, m_ref, v_ref, d_ref, nm_ref, nv_ref, g_out_ref):
        gv = g_ref[...]
        g_out_ref[...] = gv
        nm = ADAM_B1 * m_ref[...] + (1.0 - ADAM_B1) * gv
        nv = ADAM_B2 * v_ref[...] + (1.0 - ADAM_B2) * (gv * gv)
        d_ref[...] = -ADAM_LR * ((nm / c1) / (jnp.sqrt(nv / c2) + ADAM_EPS) + ADAM_WD * w_ref[...])
        nm_ref[...] = nm
        nv_ref[...] = nv

    spec = pl.BlockSpec((tr, c), lambda i: (i, 0))
    return pl.pallas_call(
        body,
        name=name,
        grid=(r // tr,),
        in_specs=[spec] * 4,
        out_specs=[spec] * 4,
        out_shape=[jax.ShapeDtypeStruct((r, c), F32)] * 4,
        compiler_params=_params("parallel"),
    )(w, g, m, v)


def adamw_small(name, ws, gs, ms, vs):
    n = len(ws)
    c1 = 1.0 - ADAM_B1 ** ADAM_STEP
    c2 = 1.0 - ADAM_B2 ** ADAM_STEP

    def body(*refs):
        for i in range(n):
            w_ref, g_ref, m_ref, v_ref = (refs[j * n + i] for j in range(4))
            d_ref, nm_ref, nv_ref = (refs[(4 + j) * n + i] for j in range(3))
            gv = g_ref[...]
            nm = ADAM_B1 * m_ref[...] + (1.0 - ADAM_B1) * gv
            nv = ADAM_B2 * v_ref[...] + (1.0 - ADAM_B2) * (gv * gv)
            d_ref[...] = -ADAM_LR * ((nm / c1) / (jnp.sqrt(nv / c2) + ADAM_EPS) + ADAM_WD * w_ref[...])
            nm_ref[...] = nm
            nv_ref[...] = nv

    whole = pl.BlockSpec(memory_space=pltpu.VMEM)
    res = pl.pallas_call(
        body,
        name=name,
        in_specs=[whole] * (4 * n),
        out_specs=[whole] * (3 * n),
        out_shape=[jax.ShapeDtypeStruct(w.shape, F32) for w in ws] * 3,
        compiler_params=pltpu.CompilerParams(vmem_limit_bytes=VMEM_LIMIT_BYTES),
    )(*ws, *gs, *ms, *vs)
    return res[:n], res[n:2 * n], res[2 * n:]


def _pack(arrays, row_multiple, cols=BLOB_COLS):
    flat = jnp.concatenate([a.reshape(-1).astype(F32) for a in arrays])
    rows = -(-flat.shape[0] // cols)
    rows = -(-rows // row_multiple) * row_multiple
    return jnp.pad(flat, (0, rows * cols - flat.shape[0])).reshape(rows, cols)


def _unpack(blob, shapes):
    flat, out, off = blob.reshape(-1), [], 0
    for s in shapes:
        size = math.prod(s)
        out.append(flat[off:off + size].reshape(s))
        off += size
    return out


def _unpack_rows(blobs, shapes):
    out, off = [], 0
    for s in shapes:
        size = math.prod(s)
        out.append(blobs[:, off:off + size].reshape((blobs.shape[0],) + tuple(s)))
        off += size
    return out


def kernel(x, p, lru_w_in, lru_conv_w, lru_conv_b, lru_wa, lru_ba, lru_wx, lru_bx, lru_lambda, lru_w_out, pool_w_in, pool_w_grp, pool_b_grp, pool_scale, pool_w_out, ln_mix_g, ln_mix_b, mlp_w1, mlp_w2, ln_mlp_g, ln_mlp_b, ple_w, ple_gate_w, ple_gate_b, loss_target, m_lru_w_in, m_lru_conv_w, m_lru_conv_b, m_lru_wa, m_lru_ba, m_lru_wx, m_lru_bx, m_lru_lambda, m_lru_w_out, m_pool_w_in, m_pool_w_grp, m_pool_b_grp, m_pool_scale, m_pool_w_out, m_ln_mix_g, m_ln_mix_b, m_mlp_w1, m_mlp_w2, m_ln_mlp_g, m_ln_mlp_b, m_ple_w, m_ple_gate_w, m_ple_gate_b, v_lru_w_in, v_lru_conv_w, v_lru_conv_b, v_lru_wa, v_lru_ba, v_lru_wx, v_lru_bx, v_lru_lambda, v_lru_w_out, v_pool_w_in, v_pool_w_grp, v_pool_b_grp, v_pool_scale, v_pool_w_out, v_ln_mix_g, v_ln_mix_b, v_mlp_w1, v_mlp_w2, v_ln_mlp_g, v_ln_mlp_b, v_ple_w, v_ple_gate_w, v_ple_gate_b):
    weights = dict(lru_w_in=lru_w_in, lru_conv_w=lru_conv_w, lru_conv_b=lru_conv_b, lru_wa=lru_wa, lru_ba=lru_ba, lru_wx=lru_wx, lru_bx=lru_bx, lru_lambda=lru_lambda, lru_w_out=lru_w_out, pool_w_in=pool_w_in, pool_w_grp=pool_w_grp, pool_b_grp=pool_b_grp, pool_scale=pool_scale, pool_w_out=pool_w_out, ln_mix_g=ln_mix_g, ln_mix_b=ln_mix_b, mlp_w1=mlp_w1, mlp_w2=mlp_w2, ln_mlp_g=ln_mlp_g, ln_mlp_b=ln_mlp_b, ple_w=ple_w, ple_gate_w=ple_gate_w, ple_gate_b=ple_gate_b)
    mom_m = dict(lru_w_in=m_lru_w_in, lru_conv_w=m_lru_conv_w, lru_conv_b=m_lru_conv_b, lru_wa=m_lru_wa, lru_ba=m_lru_ba, lru_wx=m_lru_wx, lru_bx=m_lru_bx, lru_lambda=m_lru_lambda, lru_w_out=m_lru_w_out, pool_w_in=m_pool_w_in, pool_w_grp=m_pool_w_grp, pool_b_grp=m_pool_b_grp, pool_scale=m_pool_scale, pool_w_out=m_pool_w_out, ln_mix_g=m_ln_mix_g, ln_mix_b=m_ln_mix_b, mlp_w1=m_mlp_w1, mlp_w2=m_mlp_w2, ln_mlp_g=m_ln_mlp_g, ln_mlp_b=m_ln_mlp_b, ple_w=m_ple_w, ple_gate_w=m_ple_gate_w, ple_gate_b=m_ple_gate_b)
    mom_v = dict(lru_w_in=v_lru_w_in, lru_conv_w=v_lru_conv_w, lru_conv_b=v_lru_conv_b, lru_wa=v_lru_wa, lru_ba=v_lru_ba, lru_wx=v_lru_wx, lru_bx=v_lru_bx, lru_lambda=v_lru_lambda, lru_w_out=v_lru_w_out, pool_w_in=v_pool_w_in, pool_w_grp=v_pool_w_grp, pool_b_grp=v_pool_b_grp, pool_scale=v_pool_scale, pool_w_out=v_pool_w_out, ln_mix_g=v_ln_mix_g, ln_mix_b=v_ln_mix_b, mlp_w1=v_mlp_w1, mlp_w2=v_mlp_w2, ln_mlp_g=v_ln_mlp_g, ln_mlp_b=v_ln_mlp_b, ple_w=v_ple_w, ple_gate_w=v_ple_gate_w, ple_gate_b=v_ple_gate_b)
    names = list(weights)

    depth, d = ln_mix_g.shape
    t = x.shape[1]
    n_a, n_b = lru_w_in.shape[0], pool_w_in.shape[0]
    d_rnn = lru_w_out.shape[1] * N_CHIPS
    d_ff = mlp_w1.shape[2] * N_CHIPS
    ple_dim = ple_w.shape[1]
    n_grp = len(POOL_WINDOWS)
    gw = d // n_grp
    alpha = (2 * depth) ** 0.25
    chip = 2 * lax.axis_index("x") + lax.axis_index("y")
    place = jnp.stack([chip, lax.axis_index("c")]).astype(jnp.int32)

    x2d = x.reshape(t, d)
    target = loss_target.reshape(t, d)
    p3 = p.reshape(depth, t, ple_dim)

    big = ["lru_w_in", "lru_w_out", "pool_w_in", "pool_w_out", "mlp_w1", "mlp_w2", "ple_w", "ple_gate_w", "pool_w_grp"]
    flat2 = lambda a: a.reshape(-1, a.shape[-1])
    small_sharded = ["lru_conv_w", "pool_b_grp", "pool_scale"]
    small_blob = _pack([weights[k] for k in small_sharded], 16, cols=256)
    every_layer = ("mlp_w1", "mlp_w2", "ple_w", "ple_gate_w")

    def layer_keys(i):
        return (["lru_w_in", "lru_w_out"] if i % 2 == 0 else ["pool_w_in", "pool_w_out", "pool_w_grp"]) + list(every_layer)

    def stage(k, i):
        w = weights[k]
        return into_block(f"stage_l{i}_{k}", flat2(w), i if k in every_layer else i // 2, math.prod(w.shape[1:-1]),
                          place[:1], BF16)

    staged = [[stage(k, i) for k in layer_keys(i)] for i in range(depth)]
    first = all_gather_chips("gather_l0", staged[0][:1] + [into_block("stage_small", small_blob, 0, small_blob.shape[0], place[:1], F32)])
    wg = {(layer_keys(0)[0], 0): first[0]}

    tokens = []

    def take_tokens():
        deps = tuple(tokens)
        tokens.clear()
        return deps

    def mm(*args, **kwargs):
        return matmul(*args, deps=take_tokens(), **kwargs)

    def start_gather(tag, bufs, dep):
        plan = gather_plan(len(bufs))
        flight = (plan,) + split_start(f"gather_{tag}_start", plan, 3 * len(bufs), bufs, dep)
        tokens.append(flight[-1])
        return flight

    def land_gather(tag, flight, keys, layer, after, wait_for=True):
        plan, send_sems, recv_sems, bufs, _ = flight
        landed = split_wait(f"gather_{tag}_wait", plan, send_sems, recv_sems, bufs, after)
        if wait_for:
            wg.update(zip([(k, layer) for k in keys], pair_forward(f"gather_{tag}_forward", landed)))
            return None
        plan = forward_plan(len(landed))
        forwarding = (tag, plan) + split_start(f"gather_{tag}_forward_start", plan, 3 * len(landed), landed, after)
        tokens.append(forwarding[-1])
        return forwarding

    def finish_forward(forwarding, keys, layer, after):
        tag, plan, send_sems, recv_sems, bufs, _ = forwarding
        wg.update(zip([(k, layer) for k in keys], split_wait(f"gather_{tag}_forward_wait", plan, send_sems, recv_sems, bufs, after)))

    conv_w_sh, b_grp_sh, scale_sh = _unpack_rows(first[-1].reshape(N_CHIPS, -1), [weights[k].shape for k in small_sharded])
    conv_w_full = jnp.moveaxis(conv_w_sh, 0, 2).reshape(n_a, CONV_WIDTH, d_rnn)
    b_grp_full = jnp.moveaxis(b_grp_sh, 0, 1).reshape(n_b, 1, d)
    scale_full = jnp.moveaxis(scale_sh, 0, 1).reshape(n_b, 1, d)
    rows_grp = gw // N_CHIPS
    w_grp_full = lambda i: jnp.moveaxis(wg["pool_w_grp", i].reshape(N_CHIPS, n_grp, rows_grp, gw), 0, 1).reshape(n_grp, gw, gw)
    wa_bf, wx_bf = lru_wa.astype(BF16), lru_wx.astype(BF16)
    row = lambda a, i: a[i].reshape(1, -1)

    def ln_after(acc, x_in, g, b):
        s = alpha * x_in + acc
        y = _ln_stats(s)[0] * g + b
        return y, y, s

    ln_outs = [plain(shape=(t, d), dtype=F32), plain(shape=(t, d), dtype=BF16), plain(shape=(t, d), dtype=F32)]
    saved = []
    cur, cur_bf = x2d, x2d
    for i in range(depth):
        slot = i // 2
        sv = dict(x_bf=cur_bf)
        if i == 0:
            flight = start_gather("l0_rest", staged[0][1:], first[0])
        elif i + 1 < depth:
            flight = start_gather(f"l{i + 1}", staged[i + 1], cur)
        if i % 2 == 0:
            (proj,) = mm(f"l{i}_lru_in", plain(cur_bf), colsplit(wg["lru_w_in", i], 0, d), "nn",
                         [colsplit(None, 0, t, n=2, full=(2, t, d_rnn), dtype=F32)])
            hg = lru_fwd(f"l{i}_lru", proj, conv_w_full[slot], row(lru_conv_b, slot), wa_bf[slot], row(lru_ba, slot),
                         wx_bf[slot], row(lru_bx, slot), row(lru_lambda, slot))
            if i == 0:
                land_gather("l0_rest", flight, layer_keys(0)[1:], 0, hg)
                flight = start_gather("l1", staged[1], hg)
            x1, x1_bf, s1 = mm(f"l{i}_lru_out", plain(hg), rowsplit_whole(wg["lru_w_out", i]), "nn", ln_outs, pk=2048,
                               epilogue=ln_after, tiles=[plain(cur)], rows=[row(ln_mix_g, i), row(ln_mix_b, i)])
            sv.update(proj=proj, act=hg)
        else:
            (u,) = mm(f"l{i}_pool_in", plain(cur_bf), rowsplit_whole(wg["pool_w_in", i]), "nn",
                          [plain(shape=(t, d), dtype=F32)])
            zs = pool_fwd(f"l{i}_pool", u, w_grp_full(i), b_grp_full[slot], scale_full[slot])
            x1, x1_bf, s1 = mm(f"l{i}_pool_out", plain(zs), rowsplit_whole(wg["pool_w_out", i]), "nn", ln_outs,
                               epilogue=ln_after, tiles=[plain(cur)], rows=[row(ln_mix_g, i), row(ln_mix_b, i)])
            sv.update(u=u, act=zs)

        def relu2(acc):
            hr = jnp.maximum(acc, 0.0)
            return hr, hr * hr

        hr, hh = mm(f"l{i}_mlp_up", plain(x1_bf), colsplit(wg["mlp_w1", i], 0, d), "nn",
                    [plain(shape=(t, d_ff), dtype=BF16), plain(shape=(t, d_ff), dtype=BF16)], epilogue=relu2, pm=2048)
        (mlp,) = mm(f"l{i}_mlp_down", plain(hh), rowsplit_whole(wg["mlp_w2", i]), "nn",
                    [plain(shape=(t, d), dtype=F32)], pk=d_ff)
        x2, x2_bf, s2 = ln_fwd(f"l{i}_ln_mlp", alpha, x1, mlp, row(ln_mlp_g, i), row(ln_mlp_b, i))
        if i + 1 < depth:
            forwarding = land_gather(f"l{i + 1}", flight, layer_keys(i + 1), i + 1, x2_bf, wait_for=False)

        def ple_out(acc, x2_t, gb, p_t, pw):
            e_t = jnp.concatenate([jnp.dot(p_t.astype(BF16), pw[k], preferred_element_type=F32) for k in range(N_CHIPS)], axis=1)
            gate = jax.nn.sigmoid(acc + gb)
            x3 = x2_t + e_t * gate
            return x3, x3, gate, e_t

        cur, cur_bf, gate, e = mm(
            f"l{i}_ple_gate", plain(x2_bf), rowsplit_whole(wg["ple_gate_w", i]), "nn",
            [plain(shape=(t, d), dtype=F32), plain(shape=(t, d), dtype=BF16), plain(shape=(t, d), dtype=F32), plain(shape=(t, d), dtype=F32)],
            epilogue=ple_out, tiles=[plain(x2)], rows=[row(ple_gate_b, i)], pm=512,
            side=[(p3[i], lambda tm, tn: pl.BlockSpec((tm, ple_dim), lambda r, c, kk: (r, 0))),
                  (wg["ple_w", i], lambda tm, tn: pl.BlockSpec(wg["ple_w", i].shape, lambda r, c, kk: (0, 0, 0)))])
        sv.update(s1=s1, x1_bf=x1_bf, hr=hr, hh=hh, s2=s2, x2_bf=x2_bf, gate=gate, e=e)
        saved.append(sv)
        if i + 1 < depth:
            finish_forward(forwarding, layer_keys(i + 1), i + 1, cur)

    dy, loss_part = loss_head("loss", cur, target)
    loss = lax.psum(loss_part.reshape(()), ("x", "y", "c"))

    part = {}
    sums = {}

    def grad_view(key, split):
        w = weights[key]
        return split(None, 0, w.shape[1], full=(N_CHIPS, w.shape[1], w.shape[2]), dtype=BF16)

    def group_start(tag, items, dep):
        srcs = [part[it] for it in items]
        plan = pair_plan(len(srcs))
        lands = [lax.empty(s.shape, s.dtype) for s in pair_lands(srcs)]
        flight = (tag, items, plan) + split_start(f"grads_{tag}_pair_start", plan, len(srcs), srcs + lands, dep)
        tokens.append(flight[-1])
        return flight

    def group_mid(flight, after):
        tag, items, plan, send_sems, recv_sems, bufs, _ = flight
        bufs = split_wait(f"grads_{tag}_pair_wait", plan, send_sems, recv_sems, bufs, after)
        n = len(items)
        parts = [pair_sum(f"grads_{tag}_pair_sum_{j}", bufs[j], bufs[n + j], place[1:], F32 if it[0] == "blob" else BF16)
                 for j, it in enumerate(items)]
        plan = chip_plan(n)
        flight = (tag, items, plan) + split_start(f"grads_{tag}_chip_start", plan, 3 * n,
                                                  parts + [lax.empty(q.shape, q.dtype) for q in parts], after)
        tokens.append(flight[-1])
        return flight

    def group_end(flight, after):
        tag, items, plan, send_sems, recv_sems, bufs, _ = flight
        bufs = split_wait(f"grads_{tag}_chip_wait", plan, send_sems, recv_sems, bufs, after)
        n = len(items)
        for j, (k, layer) in enumerate(items):
            if k == "blob":
                sums[k] = chip_sum(f"grads_{tag}_chip_sum_{j}", bufs[n + j], bufs[j], place, True)
            else:
                sums[k] = chip_sum(f"grads_{tag}_chip_sum_{j}", bufs[n + j], bufs[j], place, False, into=sums.get(k),
                                   layer=layer if k in every_layer else layer // 2, n_layers=weights[k].shape[0])

    big_w = [k for k in big if k != "pool_w_grp"]
    small_keys = [k for k in names if k not in big_w]

    def ln_before(ca):
        def back(acc, upstream, s, g):
            dx = ca * upstream + acc
            xhat, rstd = _ln_stats(s)
            dxh = dx * g
            ds = rstd * (dxh - jnp.mean(dxh, axis=-1, keepdims=True) - xhat * jnp.mean(dxh * xhat, axis=-1, keepdims=True))
            return ds, ds, jnp.sum(dx * xhat, axis=0, keepdims=True), jnp.sum(dx, axis=0, keepdims=True)

        return back

    ds_outs = [plain(shape=(t, d), dtype=F32), plain(shape=(t, d), dtype=BF16)]
    small = {k: [None] * weights[k].shape[0] for k in names if k not in big or k == "pool_w_grp"}
    dcur = dy
    mlp_pair = mlp_chip = mix_pair = mix_chip = None
    for i in reversed(range(depth)):
        slot = i // 2
        sv = saved[i]
        de, dpre, dgb = ple_bwd(f"l{i}_ple_bwd", dcur, sv["gate"], sv["e"])
        small["ple_gate_b"][i] = dgb
        (part["ple_w", i],) = mm(f"l{i}_d_ple_w", plain(p3[i]), plain(de), "tn", [grad_view("ple_w", colsplit)])
        (part["ple_gate_w", i],) = mm(f"l{i}_d_ple_gate_w", plain(sv["x2_bf"]), plain(dpre), "tn",
                                          [grad_view("ple_gate_w", rowsplit)])
        ds2, ds2_bf, dg, db = mm(f"l{i}_dx2", plain(dpre), rowsplit_whole(wg["ple_gate_w", i]), "nt", ds_outs, col_sums=2, pm=512,
                                 epilogue=ln_before(1.0), tiles=[plain(dcur), plain(sv["s2"])], rows=[row(ln_mlp_g, i)])
        small["ln_mlp_g"][i], small["ln_mlp_b"][i] = dg, db
        (part["mlp_w2", i],) = mm(f"l{i}_d_mlp_w2", plain(sv["hh"]), plain(ds2_bf), "tn", [grad_view("mlp_w2", rowsplit)])
        (dhpre,) = mm(f"l{i}_dh", plain(ds2_bf), rowsplit(wg["mlp_w2", i], 0, d_ff // N_CHIPS), "nt",
                      [plain(shape=(t, d_ff), dtype=BF16)], tiles=[plain(sv["hr"])], pm=2048,
                      epilogue=lambda acc, hr_t: (acc * (2.0 * hr_t.astype(F32)),))
        (part["mlp_w1", i],) = mm(f"l{i}_d_mlp_w1", plain(sv["x1_bf"]), plain(dhpre), "tn", [grad_view("mlp_w1", colsplit)])
        if mlp_chip is not None:
            group_end(mlp_chip, dhpre)
        if mix_pair is not None:
            mix_chip = group_mid(mix_pair, dhpre)
        mlp_pair = group_start(f"l{i}_mlp", [(k, i) for k in every_layer], dhpre)
        (dx1b,) = mm(f"l{i}_dx1", plain(dhpre), colsplit(wg["mlp_w1", i], 0, d), "nt", [plain(shape=(t, d), dtype=F32)],
                     pm=2048)
        ds1, ds1_bf, dg, db = ln_bwd(f"l{i}_ln_mix_bwd", alpha, ds2, dx1b, sv["s1"], row(ln_mix_g, i))
        small["ln_mix_g"][i], small["ln_mix_b"][i] = dg, db
        residual = lambda acc, ds_t: (alpha * ds_t + acc,)
        if i % 2 == 0:
            (part["lru_w_out", i],) = mm(f"l{i}_d_lru_out", plain(sv["act"]), plain(ds1_bf), "tn",
                                             [grad_view("lru_w_out", rowsplit)])
            (dhg,) = mm(f"l{i}_dhg", plain(ds1_bf), rowsplit_whole(wg["lru_w_out", i]), "nt",
                            [plain(shape=(t, d_rnn), dtype=F32)], pn=2048)
            mlp_chip = group_mid(mlp_pair, dhg)
            dproj, dcw, dcb, dba, dbx, dlam, dwa, dwx = lru_bwd(
                f"l{i}_lru_bwd", sv["proj"], dhg, conv_w_full[slot], row(lru_conv_b, slot), wa_bf[slot], row(lru_ba, slot),
                wx_bf[slot], row(lru_bx, slot), row(lru_lambda, slot), deps=take_tokens())
            for key, val in (("lru_conv_w", dcw), ("lru_conv_b", dcb), ("lru_ba", dba), ("lru_bx", dbx),
                             ("lru_lambda", dlam), ("lru_wa", dwa), ("lru_wx", dwx)):
                small[key][slot] = val
            dproj_v = colsplit(dproj, 0, t, n=2)
            (part["lru_w_in", i],) = mm(f"l{i}_d_lru_in", plain(sv["x_bf"]), dproj_v, "tn", [grad_view("lru_w_in", colsplit)])
            (dcur,) = mm(f"l{i}_dx", dproj_v, colsplit(wg["lru_w_in", i], 0, d), "nt",
                             [plain(shape=(t, d), dtype=F32)], epilogue=residual, tiles=[plain(ds1)])
        else:
            (part["pool_w_out", i],) = mm(f"l{i}_d_pool_out", plain(sv["act"]), plain(ds1_bf), "tn",
                                              [grad_view("pool_w_out", rowsplit)])
            (dzs,) = mm(f"l{i}_dzs", plain(ds1_bf), rowsplit_whole(wg["pool_w_out", i]), "nt",
                            [plain(shape=(t, d), dtype=F32)])
            mlp_chip = group_mid(mlp_pair, dzs)
            du, dwg, dbg, dsc = pool_bwd(f"l{i}_pool_bwd", sv["u"], dzs, w_grp_full(i), b_grp_full[slot], scale_full[slot],
                                         deps=take_tokens())
            small["pool_w_grp"][slot], small["pool_b_grp"][slot], small["pool_scale"][slot] = dwg, dbg, dsc
            (part["pool_w_in", i],) = mm(f"l{i}_d_pool_in", plain(sv["x_bf"]), plain(du), "tn", [grad_view("pool_w_in", rowsplit)])
            (dcur,) = mm(f"l{i}_dx", plain(du), rowsplit_whole(wg["pool_w_in", i]), "nt",
                             [plain(shape=(t, d), dtype=F32)], epilogue=residual, tiles=[plain(ds1)])
        if mix_chip is not None:
            group_end(mix_chip, dcur)
        mixer = [(k, i) for k in layer_keys(i) if k not in every_layer and k != "pool_w_grp"]
        if i == 0:
            small_full = [jnp.stack(small[k]).reshape((weights[k].shape[0],) + tuple(
                s * (N_CHIPS if ax in _sharded_axis(k) else 1) for ax, s in enumerate(weights[k].shape[1:], 1))) for k in small_keys]
            blob = _pack(small_full, 64)
            part["blob", 0] = blob.reshape(N_CHIPS, blob.shape[0] // N_CHIPS, BLOB_COLS)
            mixer.append(("blob", 0))
        mix_pair = group_start(f"l{i}_mix", mixer, dcur)
    grad_x = dcur.reshape(x.shape)
    mix_chip = group_mid(mix_pair, dcur)
    group_end(mlp_chip, mix_chip[-1])

    grads, delta, new_m, new_v = {}, {}, {}, {}

    def halves_start(tag, keys, dep):
        layers = [1 if k == "blob" else weights[k].shape[0] for k in keys]
        plan = pair_gather_plan([k == "blob" for k in keys], layers)
        return (keys, plan) + split_start(f"grads_pair_gather_{tag}_start", plan, sum(layers), [sums[k] for k in keys], dep)

    def halves_end(tag, flight, after):
        keys, plan, send_sems, recv_sems, bufs, _ = flight
        return dict(zip(keys, split_wait(f"grads_pair_gather_{tag}_wait", plan, send_sems, recv_sems, bufs, after)))

    def update(k, g):
        dl, nm, nv, g = adamw("adamw_" + k, flat2(weights[k]), g, flat2(mom_m[k]), flat2(mom_v[k]))
        delta[k], new_m[k], new_v[k], grads[k] = (a.reshape(weights[k].shape) for a in (dl, nm, nv, g))

    last = [k for k, _ in mix_pair[1]]
    early = [k for k in big_w if k not in last]
    large = [k for k in early if k in ("mlp_w1", "mlp_w2")]
    little = [k for k in early if k not in large]
    little_flight = halves_start("little", little, mix_chip[-1])
    up_flight = halves_start("large0", large[:1], little_flight[-1])
    down_flight = halves_start("large1", large[1:], up_flight[-1])
    whole = halves_end("little", little_flight, down_flight[-1])
    for k in little:
        update(k, whole[k])
    whole.update(halves_end("large0", up_flight, delta[little[-1]]))
    update(large[0], whole[large[0]])
    group_end(mix_chip, delta[large[0]])
    whole.update(zip(last, pair_gather("grads_pair_gather_last", [sums[k] for k in last], [k == "blob" for k in last],
                                       [1 if k == "blob" else weights[k].shape[0] for k in last])))
    plan = spread_plan(1)
    send_sems, recv_sems, spreading, spread_token = split_start("gather_small_grads_start", plan, 3, [whole["blob"]], delta[large[0]])
    whole.update(halves_end("large1", down_flight, spread_token))
    for k in large[1:] + [k for k in last if k != "blob"]:
        update(k, whole[k])
    (blob_all,) = split_wait("gather_small_grads_wait", plan, send_sems, recv_sems, spreading, delta[large[1]])
    small_grads = dict(zip(small_keys, _unpack(blob_all.reshape(blob.shape), [a.shape for a in small_full])))
    for k in small_keys:
        for ax in _sharded_axis(k):
            n = weights[k].shape[ax]
            small_grads[k] = lax.dynamic_slice_in_dim(small_grads[k], chip * n, n, axis=ax)
    grads.update(small_grads)
    dl, nm, nv = adamw_small("adamw_small", *[[flat2(src[k]) for k in small_keys] for src in (weights, grads, mom_m, mom_v)])
    for out, res in ((delta, dl), (new_m, nm), (new_v, nv)):
        out.update({k: a.reshape(weights[k].shape) for k, a in zip(small_keys, res)})

    return (loss, grad_x, *[grads[k] for k in names], *[delta[k] for k in names],
            *[new_m[k] for k in names], *[new_v[k] for k in names])


def _sharded_axis(key):
    return {"lru_conv_w": (2,), "pool_w_grp": (2,), "pool_b_grp": (1,), "pool_scale": (1,)}.get(key, ())
```

```python
import functools
import math

import jax
import jax.numpy as jnp
from jax import lax
from jax.experimental import pallas as pl
from jax.experimental.pallas import tpu as pltpu

F32 = jnp.float32
BF16 = jnp.bfloat16

N_CHIPS = 4
LRU_BW = 128
LRU_C = 8.0
CONV_WIDTH = 4
POOL_WINDOWS = (2, 4, 8, 16)
POOL_HALO = 16
CONV_HALO = 8
SEQ_CHUNK = 1024
LN_EPS = 1e-5
ADAM_LR = 0.001
ADAM_B1 = 0.9
ADAM_B2 = 0.999
ADAM_EPS = 1e-08
ADAM_WD = 0.01
ADAM_STEP = 10
GELU_C = math.sqrt(2.0 / math.pi)
GELU_K = 0.044715
VMEM_LIMIT_BYTES = 56 * 1024 * 1024
MATMUL_TILE_BYTES = 44 * 1024 * 1024
MESH = pl.DeviceIdType.MESH
BLOB_COLS = 1024


def _params(*sem):
    return pltpu.CompilerParams(dimension_semantics=tuple(sem), vmem_limit_bytes=VMEM_LIMIT_BYTES)


def _tile(unit, pref, align=128):
    if unit <= pref:
        return unit
    for d in range(2, unit + 1):
        if unit % d == 0 and unit // d <= pref and (unit // d) % align == 0:
            return unit // d
    raise ValueError((unit, pref, align))


class View:
    def __init__(self, arr, shape, row_unit, col_unit, block_fn, full=None, dtype=None):
        self.arr, self.shape, self.row_unit, self.col_unit, self.block_fn = arr, shape, row_unit, col_unit, block_fn
        self.full = full if full is not None else arr.shape
        self.dtype = dtype if dtype is not None else arr.dtype

    def spec(self, tr, tc, f):
        block, idx = self.block_fn(tr, tc)
        return pl.BlockSpec(block, lambda *g: idx(*f(*g)))


def plain(arr=None, shape=None, dtype=None):
    shape = arr.shape if arr is not None else shape
    return View(arr, shape, shape[0], shape[1], lambda tr, tc: ((tr, tc), lambda rt, ct: (rt, ct)), full=shape, dtype=dtype)


def colsplit(arr, layer, rows, n=N_CHIPS, full=None, dtype=None):
    full = arr.shape if arr is not None else full
    c = full[2]

    def block_fn(tr, tc):
        assert rows % tr == 0 and c % tc == 0, (rows, tr, c, tc)
        per, rpl = c // tc, rows // tr
        return (None, tr, tc), lambda rt, ct: (ct // per, layer * rpl + rt, ct % per)

    return View(arr, (rows, n * c), rows, c, block_fn, full=full, dtype=dtype)


def rowsplit(arr, layer, rows, n=N_CHIPS, full=None, dtype=None):
    full = arr.shape if arr is not None else full
    c = full[2]

    def block_fn(tr, tc):
        assert rows % tr == 0 and c % tc == 0, (rows, tr, c, tc)
        per = rows // tr
        return (None, tr, tc), lambda rt, ct: (rt // per, layer * per + rt % per, ct)

    return View(arr, (n * rows, c), rows, c, block_fn, full=full, dtype=dtype)


def rowsplit_whole(arr):
    n, rows, c = arr.shape

    def block_fn(tr, tc):
        assert tr == n * rows and c % tc == 0, (tr, n, rows, c, tc)
        return (n, rows, tc), lambda rt, ct: (0, 0, ct)

    return View(arr, (n * rows, c), n * rows, c, block_fn)


def matmul(name, a, b, mode, outs, epilogue=None, tiles=(), rows=(), side=(), deps=(), col_sums=0, pm=1024, pn=1024, pk=1024):
    if mode == "nn":
        (m, k), (k2, n) = a.shape, b.shape
        um, uk, un = a.row_unit, min(a.col_unit, b.row_unit), b.col_unit
        dims = (((1,), (0,)), ((), ()))
    elif mode == "nt":
        (m, k), (n, k2) = a.shape, b.shape
        um, uk, un = a.row_unit, min(a.col_unit, b.col_unit), b.row_unit
        dims = (((1,), (1,)), ((), ()))
    else:
        (k, m), (k2, n) = a.shape, b.shape
        um, uk, un = a.col_unit, min(a.row_unit, b.row_unit), b.col_unit
        dims = (((0,), (0,)), ((), ()))
    assert k == k2, (name, a.shape, b.shape)
    for o in list(outs) + list(tiles):
        assert o.shape == (m, n), (name, o.shape, m, n)
        um, un = min(um, o.row_unit), min(un, o.col_unit)
    tm, tn, tk = _tile(um, pm), _tile(un, pn), _tile(uk, pk)
    if mode == "tn" and uk == k:
        size = lambda v: jnp.dtype(v.dtype).itemsize
        need = 2 * k * (tm * size(a) + tn * size(b)) + 2 * tm * tn * sum(size(o) for o in outs)
        if need <= MATMUL_TILE_BYTES:
            tk = k
    assert m % tm == 0 and n % tn == 0 and k % tk == 0, (name, m, n, k, tm, tn, tk)
    gm, gn, gk = m // tm, n // tn, k // tk
    assert not col_sums or gn == 1, (name, gn)

    if mode == "nn":
        a_spec = a.spec(tm, tk, lambda i, j, kk: (i, kk))
        b_spec = b.spec(tk, tn, lambda i, j, kk: (kk, j))
    elif mode == "nt":
        a_spec = a.spec(tm, tk, lambda i, j, kk: (i, kk))
        b_spec = b.spec(tn, tk, lambda i, j, kk: (j, kk))
    else:
        a_spec = a.spec(tk, tm, lambda i, j, kk: (kk, i))
        b_spec = b.spec(tk, tn, lambda i, j, kk: (kk, j))
    tile_specs = [t.spec(tm, tn, lambda i, j, kk: (i, j)) for t in tiles]
    row_specs = [pl.BlockSpec((1, tn), lambda i, j, kk: (0, j)) for _ in rows] + [spec(tm, tn) for _, spec in side]
    rows = list(rows) + [arr for arr, _ in side]
    in_place = [o for o in outs if o.arr is not None]
    alias_specs = [pl.BlockSpec(memory_space=pl.ANY) for _ in in_place]
    out_specs = [o.spec(tm, tn, lambda i, j, kk: (i, j)) for o in outs]
    n_in = 2 + len(tiles) + len(rows)
    aliases = {}
    for o_idx, o in enumerate(outs):
        if o.arr is not None:
            aliases[n_in + in_place.index(o)] = o_idx
    n_t, n_r, n_a, n_o = len(tiles), len(rows), len(in_place) + len(deps), len(outs)
    dep_specs = [pl.BlockSpec(memory_space=pl.ANY) for _ in deps]

    def body(*refs):
        a_ref, b_ref = refs[0], refs[1]
        tile_refs = refs[2:2 + n_t]
        row_refs = refs[2 + n_t:2 + n_t + n_r]
        out_refs = refs[2 + n_t + n_r + n_a:2 + n_t + n_r + n_a + n_o]
        sum_refs = refs[2 + n_t + n_r + n_a + n_o:2 + n_t + n_r + n_a + n_o + col_sums]
        acc_ref = refs[-1] if gk > 1 else None

        def finish(acc):
            extra = [t[...] for t in tile_refs] + [r[...] for r in row_refs]
            res = epilogue(acc, *extra) if epilogue is not None else (acc,)
            for o_ref, r in zip(out_refs, res):
                o_ref[...] = r.astype(o_ref.dtype)
            for s_ref, r in zip(sum_refs, res[n_o:]):
                _accumulate(pl.program_id(0), s_ref, r)

        b_tile = b_ref[...]
        b_tile = b_tile.reshape(-1, b_tile.shape[-1])
        prod = lax.dot_general(a_ref[...].astype(BF16), b_tile.astype(BF16), dims, preferred_element_type=F32)
        if gk == 1:
            finish(prod)
        else:
            kk = pl.program_id(2)

            @pl.when(kk == 0)
            def _():
                acc_ref[...] = prod

            @pl.when(kk > 0)
            def _():
                acc_ref[...] += prod

            @pl.when(kk == gk - 1)
            def _():
                finish(acc_ref[...])

    res = pl.pallas_call(
        body,
        name=name,
        grid=(gm, gn, gk),
        in_specs=[a_spec, b_spec] + tile_specs + row_specs + alias_specs + dep_specs,
        out_specs=out_specs + [pl.BlockSpec((1, tn), lambda i, j, kk: (0, 0))] * col_sums,
        out_shape=[jax.ShapeDtypeStruct(o.full, o.dtype) for o in outs] + [jax.ShapeDtypeStruct((1, n), F32)] * col_sums,
        scratch_shapes=[pltpu.VMEM((tm, tn), F32)] if gk > 1 else [],
        input_output_aliases=aliases,
        compiler_params=_params(*(["arbitrary"] * 3 if col_sums else ["parallel", "parallel", "arbitrary"])),
    )(a.arr, b.arr, *[t.arr for t in tiles], *rows, *[o.arr for o in in_place], *deps)
    return res


def rows_call(name, fn, tiled, vecs, tiled_out, acc_out, tr=512):
    t = tiled[0].shape[0]
    tr = min(tr, t)
    assert t % tr == 0
    n1, n2, n3 = len(tiled), len(vecs), len(tiled_out)

    def body(*refs):
        fn(pl.program_id(0), refs[:n1], refs[n1:n1 + n2], refs[n1 + n2:n1 + n2 + n3], refs[n1 + n2 + n3:])

    return pl.pallas_call(
        body,
        name=name,
        grid=(t // tr,),
        in_specs=[pl.BlockSpec((tr, x.shape[1]), lambda i: (i, 0)) for x in tiled]
        + [pl.BlockSpec(v.shape, lambda i: (0, 0)) for v in vecs],
        out_specs=[pl.BlockSpec((tr, c), lambda i: (i, 0)) for c, _ in tiled_out]
        + [pl.BlockSpec(s, lambda i: (0, 0)) for s, _ in acc_out],
        out_shape=[jax.ShapeDtypeStruct((t, c), d) for c, d in tiled_out] + [jax.ShapeDtypeStruct(s, d) for s, d in acc_out],
        compiler_params=_params("arbitrary" if acc_out else "parallel"),
    )(*tiled, *vecs)


def _accumulate(step, ref, val):
    @pl.when(step == 0)
    def _():
        ref[...] = val

    @pl.when(step > 0)
    def _():
        ref[...] += val


def _ln_stats(s):
    mu = jnp.mean(s, axis=-1, keepdims=True)
    d = s - mu
    var = jnp.mean(d * d, axis=-1, keepdims=True)
    rstd = lax.rsqrt(var + LN_EPS)
    return d * rstd, rstd


def ln_fwd(name, alpha, x_in, m, g, b):
    d = x_in.shape[1]

    def fn(step, tiled, vecs, outs, accs):
        s = alpha * tiled[0][...] + tiled[1][...]
        xhat, _ = _ln_stats(s)
        y = xhat * vecs[0][...] + vecs[1][...]
        outs[0][...] = y
        outs[1][...] = y.astype(BF16)
        outs[2][...] = s

    return rows_call(name, fn, [x_in, m], [g, b], [(d, F32), (d, BF16), (d, F32)], [])


def ln_bwd(name, ca, da, db, s, g):
    d = s.shape[1]

    def fn(step, tiled, vecs, outs, accs):
        dx = ca * tiled[0][...] + tiled[1][...]
        xhat, rstd = _ln_stats(tiled[2][...])
        dxh = dx * vecs[0][...]
        ds = rstd * (dxh - jnp.mean(dxh, axis=-1, keepdims=True) - xhat * jnp.mean(dxh * xhat, axis=-1, keepdims=True))
        outs[0][...] = ds
        outs[1][...] = ds.astype(BF16)
        _accumulate(step, accs[0], jnp.sum(dx * xhat, axis=0, keepdims=True))
        _accumulate(step, accs[1], jnp.sum(dx, axis=0, keepdims=True))

    return rows_call(name, fn, [da, db, s], [g], [(d, F32), (d, BF16)], [((1, d), F32), ((1, d), F32)])


def ple_bwd(name, dx3, gate, e):
    d = dx3.shape[1]

    def fn(step, tiled, vecs, outs, accs):
        dx, gt, ev = tiled[0][...], tiled[1][...], tiled[2][...]
        dpre = dx * ev * gt * (1.0 - gt)
        outs[0][...] = (dx * gt).astype(BF16)
        outs[1][...] = dpre.astype(BF16)
        _accumulate(step, accs[0], jnp.sum(dpre, axis=0, keepdims=True))

    return rows_call(name, fn, [dx3, gate, e], [], [(d, BF16), (d, BF16)], [((1, d), F32)])


def loss_head(name, y, target):
    t, d = y.shape

    def fn(step, tiled, vecs, outs, accs):
        err = tiled[0][...] - tiled[1][...]
        outs[0][...] = err * (1.0 / d)
        part = jnp.sum(jnp.sum(err * err, axis=1, keepdims=True), axis=0, keepdims=True) * (0.5 / d)
        _accumulate(step, accs[0], part)

    return rows_call(name, fn, [y, target], [], [(d, F32)], [((1, 1), F32)])


def _softplus(z):
    return jnp.maximum(z, 0.0) + jnp.log1p(jnp.exp(-jnp.abs(z)))


def _sqrt_and_inverse(z):
    inv = lax.rsqrt(jnp.maximum(z, 1e-30))
    return z * inv, inv


def _gelu(y):
    th = jnp.tanh(GELU_C * (y + GELU_K * (y * y * y)))
    cdf = 0.5 * (1.0 + th)
    return y * cdf, cdf + 0.5 * y * (1.0 - th * th) * (GELU_C * (1.0 + 3.0 * GELU_K * y * y))


def _up(win, k):
    return pltpu.roll(win, win.shape[0] - k, 0)


def _down(win, k):
    return pltpu.roll(win, k, 0)


def _lru_gates(win, row0, cw_ref, cb, wa, ba, wx, bx, sp):
    h = CONV_HALO
    u = (cb + cw_ref[3:4, :] * win[h:] + cw_ref[2:3, :] * _down(win, 1)[h:]
         + cw_ref[1:2, :] * _down(win, 2)[h:] + cw_ref[0:1, :] * _down(win, 3)[h:])
    ub = u.astype(BF16)
    r = jax.nn.sigmoid(jnp.dot(ub, wa, preferred_element_type=F32) + ba)
    ig = jax.nn.sigmoid(jnp.dot(ub, wx, preferred_element_type=F32) + bx)
    log_a = (-LRU_C) * r * sp
    a = jnp.exp(log_a)
    mult = _sqrt_and_inverse(-jnp.tanh(log_a) * (a * a + 1.0))[0]
    first = (row0 + lax.broadcasted_iota(jnp.int32, u.shape, 0)) == 0
    mult = jnp.where(first, 1.0, mult)
    return u, r, ig, a, mult, first


def _block_scan(a, b, reverse):
    n = a.shape[0]
    a, b = a.reshape(n // 8, 8, LRU_BW), b.reshape(n // 8, 8, LRU_BW)
    pos = lax.broadcasted_iota(jnp.int32, a.shape, 1)
    for s in (1, 2, 4):
        keep = (pos >= 8 - s) if reverse else (pos < s)
        by = 8 - s if reverse else s
        b = jnp.where(keep, b, a * pltpu.roll(b, by, 1) + b)
        a = jnp.where(keep, a, a * pltpu.roll(a, by, 1))
    return a.reshape(n, LRU_BW), b.reshape(n, LRU_BW)


def _carry_scan(a_ref, b_ref, out_ref, out_off, t, reverse):
    groups, per_step = t // 8, 8

    def step(j, h):
        for k in range(per_step):
            g = j * per_step + k
            r0 = pl.multiple_of((groups - 1 - g if reverse else g) * 8, 8)
            edge = r0 if reverse else r0 + 7
            h_out = a_ref[pl.ds(edge, 1), :] * h + b_ref[pl.ds(edge, 1), :]
            out_ref[pl.ds(pl.multiple_of(out_off + r0, 8), 8), :] = a_ref[pl.ds(r0, 8), :] * h + b_ref[pl.ds(r0, 8), :]
            h = h_out
        return h

    lax.fori_loop(0, groups // per_step, step, jnp.zeros((1, LRU_BW), F32))


def _lru_in_specs(t, heads):
    blk = lambda i: (0, i)
    return [
        pl.BlockSpec((2, t, LRU_BW), lambda i: (0, 0, i)),
        pl.BlockSpec((CONV_WIDTH, LRU_BW), blk),
        pl.BlockSpec((1, LRU_BW), blk),
        pl.BlockSpec((None, LRU_BW, LRU_BW), lambda i: (i, 0, 0)),
        pl.BlockSpec((1, LRU_BW), blk),
        pl.BlockSpec((None, LRU_BW, LRU_BW), lambda i: (i, 0, 0)),
        pl.BlockSpec((1, LRU_BW), blk),
        pl.BlockSpec((1, LRU_BW), blk),
    ]


def lru_fwd(name, proj, conv_w, conv_b, wa, ba, wx, bx, lam):
    _, t, c = proj.shape
    heads = c // LRU_BW
    rc = min(SEQ_CHUNK, t)

    def body(proj_ref, cw_ref, cb_ref, wa_ref, ba_ref, wx_ref, bx_ref, lam_ref, out_ref, upad, a_s, b_s):
        upad[0:CONV_HALO, :] = jnp.zeros((CONV_HALO, LRU_BW), F32)
        upad[CONV_HALO:, :] = proj_ref[0]
        sp = _softplus(-lam_ref[...])
        cb, ba, bx, wa, wx = cb_ref[...], ba_ref[...], bx_ref[...], wa_ref[...], wx_ref[...]

        def gates(i, carry):
            r0 = pl.multiple_of(i * rc, rc)
            win = upad[pl.ds(r0, rc + CONV_HALO), :]
            u, r, ig, a, mult, _ = _lru_gates(win, r0, cw_ref, cb, wa, ba, wx, bx, sp)
            rows = pl.ds(r0, rc)
            a_s[rows, :], b_s[rows, :] = _block_scan(a, mult * (ig * u), False)
            return carry

        lax.fori_loop(0, t // rc, gates, 0)
        _carry_scan(a_s, b_s, b_s, 0, t, False)

        def gate_out(i, carry):
            r0 = pl.multiple_of(i * rc, rc)
            gy, _ = _gelu(proj_ref[1, pl.ds(r0, rc), :])
            out_ref[pl.ds(r0, rc), :] = (b_s[pl.ds(r0, rc), :] * gy).astype(BF16)
            return carry

        lax.fori_loop(0, t // rc, gate_out, 0)

    return pl.pallas_call(
        body,
        name=name,
        grid=(heads,),
        in_specs=_lru_in_specs(t, heads),
        out_specs=pl.BlockSpec((t, LRU_BW), lambda i: (0, i)),
        out_shape=jax.ShapeDtypeStruct((t, c), BF16),
        scratch_shapes=[pltpu.VMEM((t + CONV_HALO, LRU_BW), F32)] + [pltpu.VMEM((t, LRU_BW), F32)] * 2,
        compiler_params=_params("parallel"),
    )(proj, conv_w, conv_b, wa, ba, wx, bx, lam)


def lru_bwd(name, proj, dhg, conv_w, conv_b, wa, ba, wx, bx, lam, deps=()):
    _, t, c = proj.shape
    heads = c // LRU_BW
    rc = min(SEQ_CHUNK, t)
    h8 = CONV_HALO

    def body(proj_ref, dhg_ref, cw_ref, cb_ref, wa_ref, ba_ref, wx_ref, bx_ref, lam_ref,
             dproj_ref, dcw_ref, dcb_ref, dba_ref, dbx_ref, dlam_ref, dwa_ref, dwx_ref,
             upad, u_s, r_s, ig_s, apad, hpad, g_s, dupad, sa_s, sb_s):
        zeros8 = jnp.zeros((h8, LRU_BW), F32)
        upad[0:h8, :] = zeros8
        upad[h8:, :] = proj_ref[0]
        hpad[0:h8, :] = zeros8
        apad[t:, :] = zeros8
        dupad[t:, :] = zeros8
        lam = lam_ref[...]
        sp = _softplus(-lam)
        cb, ba, bx, wa, wx = cb_ref[...], ba_ref[...], bx_ref[...], wa_ref[...], wx_ref[...]

        def gates(i, carry):
            r0 = pl.multiple_of(i * rc, rc)
            win = upad[pl.ds(r0, rc + h8), :]
            u, r, ig, a, mult, _ = _lru_gates(win, r0, cw_ref, cb, wa, ba, wx, bx, sp)
            u_s[pl.ds(r0, rc), :] = u
            r_s[pl.ds(r0, rc), :] = r
            ig_s[pl.ds(r0, rc), :] = ig
            rows = pl.ds(r0, rc)
            apad[rows, :] = a
            sa_s[rows, :], sb_s[rows, :] = _block_scan(a, mult * (ig * u), False)
            return carry

        lax.fori_loop(0, t // rc, gates, 0)
        _carry_scan(sa_s, sb_s, hpad, h8, t, False)

        def out_gate(i, carry):
            r0 = pl.multiple_of(i * rc, rc)
            gy, dgy = _gelu(proj_ref[1, pl.ds(r0, rc), :])
            dh = dhg_ref[pl.ds(r0, rc), :]
            hh = hpad[pl.ds(pl.multiple_of(r0 + h8, 8), rc), :]
            dproj_ref[1, pl.ds(r0, rc), :] = (dh * hh * dgy).astype(BF16)
            rows = pl.ds(r0, rc)
            a_next = _up(apad[pl.ds(r0, rc + h8), :], 1)[:rc]
            sa_s[rows, :], sb_s[rows, :] = _block_scan(a_next, dh * gy, True)
            return carry

        lax.fori_loop(0, t // rc, out_gate, 0)
        _carry_scan(sa_s, sb_s, g_s, 0, t, True)

        zrow = jnp.zeros((1, LRU_BW), F32)
        zmat = jnp.zeros((LRU_BW, LRU_BW), F32)

        def grads(i, carry):
            dsp, dba, dbx, dwa, dwx = carry
            r0 = pl.multiple_of(i * rc, rc)
            g = g_s[pl.ds(r0, rc), :]
            u, r, ig, a = u_s[pl.ds(r0, rc), :], r_s[pl.ds(r0, rc), :], ig_s[pl.ds(r0, rc), :], apad[pl.ds(r0, rc), :]
            hprev = _down(hpad[pl.ds(r0, rc + h8), :], 1)[h8:]
            first = (r0 + lax.broadcasted_iota(jnp.int32, u.shape, 0)) == 0
            log_a = (-LRU_C) * r * sp
            mult, inv_mult = _sqrt_and_inverse(-jnp.tanh(log_a) * (a * a + 1.0))
            mult = jnp.where(first, 1.0, mult)
            dmult = jnp.where(first, 0.0, g * (ig * u))
            dlog_a = g * hprev * a - dmult * (a * a) * inv_mult
            dr = dlog_a * ((-LRU_C) * sp)
            dpre_r = dr * r * (1.0 - r)
            dpre_i = (g * mult * u) * ig * (1.0 - ig)
            pr, pi, ub = dpre_r.astype(BF16), dpre_i.astype(BF16), u.astype(BF16)
            nt = (((1,), (1,)), ((), ()))
            tn = (((0,), (0,)), ((), ()))
            du = (g * mult * ig + lax.dot_general(pr, wa, nt, preferred_element_type=F32)
                  + lax.dot_general(pi, wx, nt, preferred_element_type=F32))
            dupad[pl.ds(r0, rc), :] = du
            return (dsp + jnp.sum(dlog_a * ((-LRU_C) * r), axis=0, keepdims=True),
                    dba + jnp.sum(dpre_r, axis=0, keepdims=True),
                    dbx + jnp.sum(dpre_i, axis=0, keepdims=True),
                    dwa + lax.dot_general(ub, pr, tn, preferred_element_type=F32),
                    dwx + lax.dot_general(ub, pi, tn, preferred_element_type=F32))

        dsp, dba, dbx, dwa, dwx = lax.fori_loop(0, t // rc, grads, (zrow, zrow, zrow, zmat, zmat))
        dba_ref[...] = dba
        dbx_ref[...] = dbx
        dwa_ref[...] = dwa
        dwx_ref[...] = dwx
        dlam_ref[...] = -dsp * jax.nn.sigmoid(-lam)

        def conv_back(i, carry):
            dcb, d0, d1, d2, d3 = carry
            r0 = pl.multiple_of(i * rc, rc)
            dwin = dupad[pl.ds(r0, rc + h8), :]
            du = dwin[:rc]
            du0 = (cw_ref[3:4, :] * du + cw_ref[2:3, :] * _up(dwin, 1)[:rc]
                   + cw_ref[1:2, :] * _up(dwin, 2)[:rc] + cw_ref[0:1, :] * _up(dwin, 3)[:rc])
            dproj_ref[0, pl.ds(r0, rc), :] = du0.astype(BF16)
            win = upad[pl.ds(r0, rc + h8), :]
            red = lambda v: jnp.sum(v, axis=0, keepdims=True)
            return (dcb + red(du), d0 + red(du * _down(win, 3)[h8:]), d1 + red(du * _down(win, 2)[h8:]),
                    d2 + red(du * _down(win, 1)[h8:]), d3 + red(du * win[h8:]))

        dcb, d0, d1, d2, d3 = lax.fori_loop(0, t // rc, conv_back, (zrow,) * 5)
        dcb_ref[...] = dcb
        dcw_ref[0:1, :] = d0
        dcw_ref[1:2, :] = d1
        dcw_ref[2:3, :] = d2
        dcw_ref[3:4, :] = d3

    blk = lambda i: (0, i)
    vec = jax.ShapeDtypeStruct((1, c), F32)
    mat = jax.ShapeDtypeStruct((heads, LRU_BW, LRU_BW), F32)
    full = lambda: pltpu.VMEM((t, LRU_BW), F32)
    padded = lambda: pltpu.VMEM((t + h8, LRU_BW), F32)
    return pl.pallas_call(
        lambda *refs: body(*refs[len(deps):]),
        name=name,
        grid=(heads,),
        in_specs=[_ANY] * len(deps) + _lru_in_specs(t, heads)[:1] + [pl.BlockSpec((t, LRU_BW), blk)] + _lru_in_specs(t, heads)[1:],
        out_specs=[pl.BlockSpec((2, t, LRU_BW), lambda i: (0, 0, i)), pl.BlockSpec((CONV_WIDTH, LRU_BW), blk)]
        + [pl.BlockSpec((1, LRU_BW), blk)] * 4 + [pl.BlockSpec((None, LRU_BW, LRU_BW), lambda i: (i, 0, 0))] * 2,
        out_shape=[jax.ShapeDtypeStruct((2, t, c), BF16), jax.ShapeDtypeStruct((CONV_WIDTH, c), F32), vec, vec, vec, vec, mat, mat],
        scratch_shapes=[padded(), full(), full(), full(), padded(), padded(), full(), padded()] + [full()] * 2,
        compiler_params=_params("parallel"),
    )(*deps, proj, dhg, conv_w, conv_b, wa, ba, wx, bx, lam)


def _pick_level(g, levels):
    out = levels[-1]
    for k in range(len(levels) - 2, -1, -1):
        out = jnp.where(g == k, levels[k], out)
    return out


def _pool_z(win, g, row0, rc):
    levels, cur = [], win
    for k in range(len(POOL_WINDOWS)):
        cur = cur + _down(cur, 1 << k)
        levels.append(cur[POOL_HALO:])
    tot = _pick_level(g, levels)
    width = jnp.left_shift(2, g)
    row = row0 + lax.broadcasted_iota(jnp.int32, tot.shape, 0)
    cnt = jnp.minimum(row + 1, width).astype(F32)
    return tot / cnt - win[POOL_HALO:], cnt


def _pool_specs(t, gw):
    blk = lambda g: (0, g)
    return [pl.BlockSpec((t, gw), blk), pl.BlockSpec((None, gw, gw), lambda g: (g, 0, 0)),
            pl.BlockSpec((1, gw), blk), pl.BlockSpec((1, gw), blk)]


def pool_fwd(name, u, w_grp, b_grp, scale):
    t, d = u.shape
    gw = d // len(POOL_WINDOWS)
    rc = min(SEQ_CHUNK, t)

    def body(u_ref, wg_ref, bg_ref, sc_ref, out_ref, upad):
        g = pl.program_id(0)
        upad[0:POOL_HALO, :] = jnp.zeros((POOL_HALO, gw), F32)
        upad[POOL_HALO:, :] = u_ref[...]
        wg, bg, sc = wg_ref[...], bg_ref[...], sc_ref[...]

        def chunk(i, carry):
            r0 = pl.multiple_of(i * rc, rc)
            z, _ = _pool_z(upad[pl.ds(r0, rc + POOL_HALO), :], g, r0, rc)
            z2 = jnp.dot(z.astype(BF16), wg, preferred_element_type=F32) + bg
            out_ref[pl.ds(r0, rc), :] = (z2 * sc).astype(BF16)
            return carry

        lax.fori_loop(0, t // rc, chunk, 0)

    return pl.pallas_call(
        body,
        name=name,
        grid=(len(POOL_WINDOWS),),
        in_specs=_pool_specs(t, gw),
        out_specs=pl.BlockSpec((t, gw), lambda g: (0, g)),
        out_shape=jax.ShapeDtypeStruct((t, d), BF16),
        scratch_shapes=[pltpu.VMEM((t + POOL_HALO, gw), F32)],
        compiler_params=_params("parallel"),
    )(u, w_grp, b_grp, scale)


def pool_bwd(name, u, dzs, w_grp, b_grp, scale, deps=()):
    t, d = u.shape
    gw = d // len(POOL_WINDOWS)
    rc = min(SEQ_CHUNK, t)

    def body(u_ref, dzs_ref, wg_ref, bg_ref, sc_ref, du_ref, dwg_ref, dbg_ref, dsc_ref, upad, qpad, dz_s):
        g = pl.program_id(0)
        upad[0:POOL_HALO, :] = jnp.zeros((POOL_HALO, gw), F32)
        upad[POOL_HALO:, :] = u_ref[...]
        qpad[t:, :] = jnp.zeros((POOL_HALO, gw), F32)
        wg, bg, sc = wg_ref[...], bg_ref[...], sc_ref[...]
        zrow = jnp.zeros((1, gw), F32)

        def chunk(i, carry):
            dsc, dbg, dwg = carry
            r0 = pl.multiple_of(i * rc, rc)
            z, cnt = _pool_z(upad[pl.ds(r0, rc + POOL_HALO), :], g, r0, rc)
            zb = z.astype(BF16)
            z2 = jnp.dot(zb, wg, preferred_element_type=F32) + bg
            dzs = dzs_ref[pl.ds(r0, rc), :]
            dz2 = dzs * sc
            d2b = dz2.astype(BF16)
            dz = lax.dot_general(d2b, wg, (((1,), (1,)), ((), ())), preferred_element_type=F32)
            dz_s[pl.ds(r0, rc), :] = dz
            qpad[pl.ds(r0, rc), :] = dz / cnt
            return (dsc + jnp.sum(dzs * z2, axis=0, keepdims=True), dbg + jnp.sum(dz2, axis=0, keepdims=True),
                    dwg + lax.dot_general(zb, d2b, (((0,), (0,)), ((), ())), preferred_element_type=F32))

        dsc, dbg, dwg = lax.fori_loop(0, t // rc, chunk, (zrow, zrow, jnp.zeros((gw, gw), F32)))
        dsc_ref[...] = dsc
        dbg_ref[...] = dbg
        dwg_ref[...] = dwg

        def spread(i, carry):
            r0 = pl.multiple_of(i * rc, rc)
            levels, cur = [], qpad[pl.ds(r0, rc + POOL_HALO), :]
            for k in range(len(POOL_WINDOWS)):
                cur = cur + _up(cur, 1 << k)
                levels.append(cur[:rc])
            du_ref[pl.ds(r0, rc), :] = (_pick_level(g, levels) - dz_s[pl.ds(r0, rc), :]).astype(BF16)
            return carry

        lax.fori_loop(0, t // rc, spread, 0)

    blk = lambda g: (0, g)
    vec = jax.ShapeDtypeStruct((1, d), F32)
    return pl.pallas_call(
        lambda *refs: body(*refs[len(deps):]),
        name=name,
        grid=(len(POOL_WINDOWS),),
        in_specs=[_ANY] * len(deps) + _pool_specs(t, gw)[:1] + [pl.BlockSpec((t, gw), blk)] + _pool_specs(t, gw)[1:],
        out_specs=[pl.BlockSpec((t, gw), blk), pl.BlockSpec((None, gw, gw), lambda g: (g, 0, 0)),
                   pl.BlockSpec((1, gw), blk), pl.BlockSpec((1, gw), blk)],
        out_shape=[jax.ShapeDtypeStruct((t, d), BF16), jax.ShapeDtypeStruct((len(POOL_WINDOWS), gw, gw), F32), vec, vec],
        scratch_shapes=[pltpu.VMEM((t + POOL_HALO, gw), F32), pltpu.VMEM((t + POOL_HALO, gw), F32), pltpu.VMEM((t, gw), F32)],
        compiler_params=_params("parallel"),
    )(*deps, u, dzs, w_grp, b_grp, scale)


def _place():
    return lax.axis_index("x"), lax.axis_index("y"), lax.axis_index("c")


def _other_chips(x, y):
    return [(1 - x, y), (x, 1 - y), (1 - x, 1 - y)]


def _half(c, rows):
    h = rows // 2
    return pl.ds(pl.multiple_of(c * h, 8), h)


_ANY = pl.BlockSpec(memory_space=pl.ANY)


def into_block(name, shards, layer, r, me, dtype):
    c = shards.shape[1]
    tr = _tile(r, 512, 16)
    per = r // tr

    def body(me_ref, s_ref, o_ref):
        o_ref[...] = s_ref[...].astype(o_ref.dtype)

    return pl.pallas_call(
        body,
        name=name,
        grid_spec=pltpu.PrefetchScalarGridSpec(
            num_scalar_prefetch=1,
            grid=(per,),
            in_specs=[pl.BlockSpec((tr, c), lambda i, me_ref: (layer * per + i, 0))],
            out_specs=pl.BlockSpec((None, tr, c), lambda i, me_ref: (me_ref[0], i, 0)),
        ),
        out_shape=jax.ShapeDtypeStruct((N_CHIPS, r, c), dtype),
        compiler_params=_params("parallel"),
    )(me, shards)


_HBM = pl.BlockSpec(memory_space=pltpu.HBM)
_SEM = pl.BlockSpec(memory_space=pltpu.SEMAPHORE)


def _in_hbm(a):
    return pltpu.with_memory_space_constraint(a, pltpu.HBM)


def split_start(name, plan, n_copies, bufs, dep):
    n = len(bufs)

    def body(*refs):
        for cp in plan(refs[:n], refs[n + 1], refs[n + 2]):
            cp.start()
        refs[-1][...] = jnp.zeros_like(refs[-1])

    res = pl.pallas_call(
        body,
        name=name,
        in_specs=[_HBM] * n + [_ANY],
        out_specs=[_SEM, _SEM] + [_HBM] * n + [pl.BlockSpec(memory_space=pltpu.VMEM)],
        out_shape=[pltpu.SemaphoreType.DMA((n_copies,)), pltpu.SemaphoreType.DMA((n_copies,))]
        + [pltpu.HBM(b.shape, b.dtype) for b in bufs] + [jax.ShapeDtypeStruct((8, 128), F32)],
        input_output_aliases={i: 2 + i for i in range(n)},
        compiler_params=pltpu.CompilerParams(has_side_effects=pltpu.SideEffectType.DATAFLOW_SIDE_EFFECTING),
    )(*[_in_hbm(b) for b in bufs], dep)
    return res[0], res[1], list(res[2:2 + n]), res[-1]


def split_wait(name, plan, send_sems, recv_sems, bufs, after):
    n = len(bufs)

    def body(*refs):
        copies = plan(refs[:n], refs[n], refs[n + 1])
        for cp in copies:
            cp.wait_send()
        for cp in copies:
            cp.wait_recv()

    return pl.pallas_call(
        body,
        name=name,
        in_specs=[_HBM] * n + [_SEM, _SEM, _ANY],
        out_specs=[_HBM] * n,
        out_shape=[pltpu.HBM(b.shape, b.dtype) for b in bufs],
        input_output_aliases={i: i for i in range(n)},
        compiler_params=pltpu.CompilerParams(has_side_effects=pltpu.SideEffectType.DATAFLOW_SIDE_EFFECTING),
    )(*bufs, send_sems, recv_sems, after)


def gather_plan(n):
    def plan(bufs, send_sems, recv_sems):
        x, y, c = _place()
        copies = []
        for i in range(n):
            blk = bufs[i].at[2 * x + y, _half(c, bufs[i].shape[1]), :]
            for j, chip in enumerate(_other_chips(x, y)):
                copies.append(pltpu.make_async_remote_copy(
                    src_ref=blk, dst_ref=blk, send_sem=send_sems.at[3 * i + j], recv_sem=recv_sems.at[3 * i + j],
                    device_id=(*chip, c), device_id_type=MESH))
        return copies

    return plan


def forward_plan(n):
    def plan(bufs, send_sems, recv_sems):
        x, y, c = _place()
        copies = []
        for i in range(n):
            for j, (cx, cy) in enumerate(_other_chips(x, y)):
                blk = bufs[i].at[2 * cx + cy, _half(c, bufs[i].shape[1]), :]
                copies.append(pltpu.make_async_remote_copy(
                    src_ref=blk, dst_ref=blk, send_sem=send_sems.at[3 * i + j], recv_sem=recv_sems.at[3 * i + j],
                    device_id=(x, y, 1 - c), device_id_type=MESH))
        return copies

    return plan


def pair_forward(name, bufs):
    n = len(bufs)

    def body(*refs):
        copies = forward_plan(n)(refs[n:2 * n], refs[2 * n], refs[2 * n + 1])
        for cp in copies:
            cp.start()
        for cp in copies:
            cp.wait()

    return pl.pallas_call(
        body,
        name=name,
        in_specs=[_ANY] * n,
        out_specs=[_ANY] * n,
        out_shape=[jax.ShapeDtypeStruct(b.shape, b.dtype) for b in bufs],
        input_output_aliases={i: i for i in range(n)},
        scratch_shapes=[pltpu.SemaphoreType.DMA((3 * n,)), pltpu.SemaphoreType.DMA((3 * n,))],
    )(*bufs)


def all_gather_chips(name, bufs):
    n = len(bufs)

    def body(*refs):
        outs = refs[n:2 * n]
        send_sems, recv_sems = refs[2 * n:]
        x, y, c = _place()
        me, sibling = 2 * x + y, (x, y, 1 - c)
        chips = _other_chips(x, y)

        def copy(i, slot, block, half, to):
            blk = outs[i].at[block, _half(half, outs[i].shape[1]), :]
            return pltpu.make_async_remote_copy(
                src_ref=blk, dst_ref=blk, send_sem=send_sems.at[i * 6 + slot], recv_sem=recv_sems.at[i * 6 + slot],
                device_id=to, device_id_type=MESH)

        first = [copy(i, j, me, c, (*chip, c)) for i in range(n) for j, chip in enumerate(chips)]
        for cp in first:
            cp.start()
        passed = []
        for i in range(n):
            for j, (cx, cy) in enumerate(chips):
                copy(i, j, 2 * cx + cy, c, (x, y, c)).wait_recv()
                fwd = copy(i, 3 + j, 2 * cx + cy, c, sibling)
                fwd.start()
                passed.append(fwd)
        for i in range(n):
            for j, (cx, cy) in enumerate(chips):
                copy(i, 3 + j, 2 * cx + cy, 1 - c, (x, y, c)).wait_recv()
        for cp in first + passed:
            cp.wait_send()

    return pl.pallas_call(
        body,
        name=name,
        in_specs=[_ANY] * n,
        out_specs=[_ANY] * n,
        out_shape=[jax.ShapeDtypeStruct(b.shape, b.dtype) for b in bufs],
        input_output_aliases={i: i for i in range(n)},
        scratch_shapes=[pltpu.SemaphoreType.DMA((6 * n,)), pltpu.SemaphoreType.DMA((6 * n,))],
    )(*bufs)


def pair_plan(n):
    def plan(bufs, send_sems, recv_sems):
        x, y, c = _place()
        return [pltpu.make_async_remote_copy(
            src_ref=bufs[i].at[:, _half(1 - c, bufs[i].shape[1]), :], dst_ref=bufs[n + i], send_sem=send_sems.at[i],
            recv_sem=recv_sems.at[i], device_id=(x, y, 1 - c), device_id_type=MESH) for i in range(n)]

    return plan


def chip_plan(n):
    def plan(bufs, send_sems, recv_sems):
        x, y, c = _place()
        copies = []
        for i in range(n):
            for j, (cx, cy) in enumerate(_other_chips(x, y)):
                copies.append(pltpu.make_async_remote_copy(
                    src_ref=bufs[i].at[2 * cx + cy], dst_ref=bufs[n + i].at[2 * x + y], send_sem=send_sems.at[3 * i + j],
                    recv_sem=recv_sems.at[3 * i + j], device_id=(cx, cy, c), device_id_type=MESH))
        return copies

    return plan


def pair_lands(grads):
    return [jax.ShapeDtypeStruct((g.shape[0], g.shape[1] // 2, g.shape[2]), g.dtype) for g in grads]


def pair_gather_plan(blocked, layers):
    def plan(bufs, send_sems, recv_sems):
        x, y, c = _place()
        copies = []
        for i in range(len(bufs)):
            buf = bufs[i].at[2 * x + y] if blocked[i] else bufs[i]
            r = buf.shape[0] // layers[i]
            for l in range(layers[i]):
                mine = buf.at[pl.ds(pl.multiple_of(l * r + c * (r // 2), 8), r // 2), :]
                copies.append(pltpu.make_async_remote_copy(
                    src_ref=mine, dst_ref=mine, send_sem=send_sems.at[len(copies)], recv_sem=recv_sems.at[len(copies)],
                    device_id=(x, y, 1 - c), device_id_type=MESH))
        return copies

    return plan


def spread_plan(n):
    def plan(bufs, send_sems, recv_sems):
        x, y, c = _place()
        copies = []
        for i in range(n):
            blk = bufs[i].at[2 * x + y]
            for j, chip in enumerate(_other_chips(x, y)):
                copies.append(pltpu.make_async_remote_copy(
                    src_ref=blk, dst_ref=blk, send_sem=send_sems.at[3 * i + j], recv_sem=recv_sems.at[3 * i + j],
                    device_id=(*chip, c), device_id_type=MESH))
        return copies

    return plan


def pair_gather(name, bufs, blocked, layers):
    n = len(bufs)
    n_copies = sum(layers)

    def body(*refs):
        copies = pair_gather_plan(blocked, layers)(refs[n:2 * n], refs[2 * n], refs[2 * n + 1])
        for cp in copies:
            cp.start()
        for cp in copies:
            cp.wait()

    return pl.pallas_call(
        body,
        name=name,
        in_specs=[_ANY] * n,
        out_specs=[_ANY] * n,
        out_shape=[jax.ShapeDtypeStruct(b.shape, b.dtype) for b in bufs],
        input_output_aliases={i: i for i in range(n)},
        scratch_shapes=[pltpu.SemaphoreType.DMA((n_copies,)), pltpu.SemaphoreType.DMA((n_copies,))],
    )(*bufs)


def pair_sum(name, grad, recv, core, dtype):
    _, r, c = grad.shape
    h = r // 2
    th = _tile(h, 1024, 16)
    per = h // th

    def body(core_ref, g_ref, r_ref, o_ref):
        o_ref[...] = (g_ref[...].astype(F32) + r_ref[...].astype(F32)).astype(o_ref.dtype)

    return pl.pallas_call(
        body,
        name=name,
        grid_spec=pltpu.PrefetchScalarGridSpec(
            num_scalar_prefetch=1,
            grid=(N_CHIPS, per),
            in_specs=[pl.BlockSpec((None, th, c), lambda k, i, core_ref: (k, core_ref[0] * per + i, 0)),
                      pl.BlockSpec((None, th, c), lambda k, i, core_ref: (k, i, 0))],
            out_specs=pl.BlockSpec((None, th, c), lambda k, i, core_ref: (k, i, 0)),
        ),
        out_shape=jax.ShapeDtypeStruct((N_CHIPS, h, c), dtype),
        compiler_params=_params("parallel", "parallel"),
    )(core, grad, recv)


def chip_sum(name, got, parts, place, blocked, into=None, layer=0, n_layers=1):
    _, h, c = parts.shape
    th = _tile(h, 512, 16)
    per = h // th

    def body(place_ref, q0, q1, q2, q3, p_ref, *rest):
        o_ref = rest[-1]
        me = place_ref[0]
        own = p_ref[...].astype(F32)
        v = [jnp.where(me == k, own, q[...].astype(F32)) for k, q in enumerate((q0, q1, q2, q3))]
        o_ref[...] = ((v[0] + v[1]) + v[2]) + v[3]

    def got_spec(k):
        return pl.BlockSpec((None, th, c), lambda i, pr: (jnp.where(pr[0] == k, (k + 1) % N_CHIPS, k), i, 0))

    if blocked:
        out_spec = pl.BlockSpec((None, th, c), lambda i, pr: (pr[0], pr[1] * per + i, 0))
        out_shape = jax.ShapeDtypeStruct((N_CHIPS, 2 * h, c), F32)
    else:
        out_spec = pl.BlockSpec((th, c), lambda i, pr: ((2 * layer + pr[1]) * per + i, 0))
        out_shape = jax.ShapeDtypeStruct((n_layers * 2 * h, c), F32)
    carried = [] if into is None else [into]
    return pl.pallas_call(
        body,
        name=name,
        grid_spec=pltpu.PrefetchScalarGridSpec(
            num_scalar_prefetch=1,
            grid=(per,),
            in_specs=[got_spec(k) for k in range(N_CHIPS)] + [pl.BlockSpec((None, th, c), lambda i, pr: (pr[0], i, 0))]
            + [_ANY] * len(carried),
            out_specs=out_spec,
        ),
        out_shape=out_shape,
        input_output_aliases={6: 0} if carried else {},
        compiler_params=_params("parallel"),
    )(place, got, got, got, got, parts, *carried)


def adamw(name, w, g, m, v):
    r, c = w.shape
    tr = _tile(r, 512, 8)
    c1 = 1.0 - ADAM_B1 ** ADAM_STEP
    c2 = 1.0 - ADAM_B2 ** ADAM_STEP

    def body(w_ref, g_ref, m_ref, v_ref, d_ref, nm_ref, nv_ref, g_out_ref):
        gv = g_ref[...]
        g_out_ref[...] = gv
        nm = ADAM_B1 * m_ref[...] + (1.0 - ADAM_B1) * gv
        nv = ADAM_B2 * v_ref[...] + (1.0 - ADAM_B2) * (gv * gv)
        d_ref[...] = -ADAM_LR * ((nm / c1) / (jnp.sqrt(nv / c2) + ADAM_EPS) + ADAM_WD * w_ref[...])
        nm_ref[...] = nm
        nv_ref[...] = nv

    spec = pl.BlockSpec((tr, c), lambda i: (i, 0))
    return pl.pallas_call(
        body,
        name=name,
        grid=(r // tr,),
        in_specs=[spec] * 4,
        out_specs=[spec] * 4,
        out_shape=[jax.ShapeDtypeStruct((r, c), F32)] * 4,
        compiler_params=_params("parallel"),
    )(w, g, m, v)


def adamw_small(name, ws, gs, ms, vs):
    n = len(ws)
    c1 = 1.0 - ADAM_B1 ** ADAM_STEP
    c2 = 1.0 - ADAM_B2 ** ADAM_STEP

    def body(*refs):
        for i in range(n):
            w_ref, g_ref, m_ref, v_ref = (refs[j * n + i] for j in range(4))
            d_ref, nm_ref, nv_ref = (refs[(4 + j) * n + i] for j in range(3))
            gv = g_ref[...]
            nm = ADAM_B1 * m_ref[...] + (1.0 - ADAM_B1) * gv
            nv = ADAM_B2 * v_ref[...] + (1.0 - ADAM_B2) * (gv * gv)
            d_ref[...] = -ADAM_LR * ((nm / c1) / (jnp.sqrt(nv / c2) + ADAM_EPS) + ADAM_WD * w_ref[...])
            nm_ref[...] = nm
            nv_ref[...] = nv

    whole = pl.BlockSpec(memory_space=pltpu.VMEM)
    res = pl.pallas_call(
        body,
        name=name,
        in_specs=[whole] * (4 * n),
        out_specs=[whole] * (3 * n),
        out_shape=[jax.ShapeDtypeStruct(w.shape, F32) for w in ws] * 3,
        compiler_params=pltpu.CompilerParams(vmem_limit_bytes=VMEM_LIMIT_BYTES),
    )(*ws, *gs, *ms, *vs)
    return res[:n], res[n:2 * n], res[2 * n:]


def _pack(arrays, row_multiple, cols=BLOB_COLS):
    flat = jnp.concatenate([a.reshape(-1).astype(F32) for a in arrays])
    rows = -(-flat.shape[0] // cols)
    rows = -(-rows // row_multiple) * row_multiple
    return jnp.pad(flat, (0, rows * cols - flat.shape[0])).reshape(rows, cols)


def _unpack(blob, shapes):
    flat, out, off = blob.reshape(-1), [], 0
    for s in shapes:
        size = math.prod(s)
        out.append(flat[off:off + size].reshape(s))
        off += size
    return out


def _unpack_rows(blobs, shapes):
    out, off = [], 0
    for s in shapes:
        size = math.prod(s)
        out.append(blobs[:, off:off + size].reshape((blobs.shape[0],) + tuple(s)))
        off += size
    return out


def kernel(x, p, lru_w_in, lru_conv_w, lru_conv_b, lru_wa, lru_ba, lru_wx, lru_bx, lru_lambda, lru_w_out, pool_w_in, pool_w_grp, pool_b_grp, pool_scale, pool_w_out, ln_mix_g, ln_mix_b, mlp_w1, mlp_w2, ln_mlp_g, ln_mlp_b, ple_w, ple_gate_w, ple_gate_b, loss_target, m_lru_w_in, m_lru_conv_w, m_lru_conv_b, m_lru_wa, m_lru_ba, m_lru_wx, m_lru_bx, m_lru_lambda, m_lru_w_out, m_pool_w_in, m_pool_w_grp, m_pool_b_grp, m_pool_scale, m_pool_w_out, m_ln_mix_g, m_ln_mix_b, m_mlp_w1, m_mlp_w2, m_ln_mlp_g, m_ln_mlp_b, m_ple_w, m_ple_gate_w, m_ple_gate_b, v_lru_w_in, v_lru_conv_w, v_lru_conv_b, v_lru_wa, v_lru_ba, v_lru_wx, v_lru_bx, v_lru_lambda, v_lru_w_out, v_pool_w_in, v_pool_w_grp, v_pool_b_grp, v_pool_scale, v_pool_w_out, v_ln_mix_g, v_ln_mix_b, v_mlp_w1, v_mlp_w2, v_ln_mlp_g, v_ln_mlp_b, v_ple_w, v_ple_gate_w, v_ple_gate_b):
    weights = dict(lru_w_in=lru_w_in, lru_conv_w=lru_conv_w, lru_conv_b=lru_conv_b, lru_wa=lru_wa, lru_ba=lru_ba, lru_wx=lru_wx, lru_bx=lru_bx, lru_lambda=lru_lambda, lru_w_out=lru_w_out, pool_w_in=pool_w_in, pool_w_grp=pool_w_grp, pool_b_grp=pool_b_grp, pool_scale=pool_scale, pool_w_out=pool_w_out, ln_mix_g=ln_mix_g, ln_mix_b=ln_mix_b, mlp_w1=mlp_w1, mlp_w2=mlp_w2, ln_mlp_g=ln_mlp_g, ln_mlp_b=ln_mlp_b, ple_w=ple_w, ple_gate_w=ple_gate_w, ple_gate_b=ple_gate_b)
    mom_m = dict(lru_w_in=m_lru_w_in, lru_conv_w=m_lru_conv_w, lru_conv_b=m_lru_conv_b, lru_wa=m_lru_wa, lru_ba=m_lru_ba, lru_wx=m_lru_wx, lru_bx=m_lru_bx, lru_lambda=m_lru_lambda, lru_w_out=m_lru_w_out, pool_w_in=m_pool_w_in, pool_w_grp=m_pool_w_grp, pool_b_grp=m_pool_b_grp, pool_scale=m_pool_scale, pool_w_out=m_pool_w_out, ln_mix_g=m_ln_mix_g, ln_mix_b=m_ln_mix_b, mlp_w1=m_mlp_w1, mlp_w2=m_mlp_w2, ln_mlp_g=m_ln_mlp_g, ln_mlp_b=m_ln_mlp_b, ple_w=m_ple_w, ple_gate_w=m_ple_gate_w, ple_gate_b=m_ple_gate_b)
    mom_v = dict(lru_w_in=v_lru_w_in, lru_conv_w=v_lru_conv_w, lru_conv_b=v_lru_conv_b, lru_wa=v_lru_wa, lru_ba=v_lru_ba, lru_wx=v_lru_wx, lru_bx=v_lru_bx, lru_lambda=v_lru_lambda, lru_w_out=v_lru_w_out, pool_w_in=v_pool_w_in, pool_w_grp=v_pool_w_grp, pool_b_grp=v_pool_b_grp, pool_scale=v_pool_scale, pool_w_out=v_pool_w_out, ln_mix_g=v_ln_mix_g, ln_mix_b=v_ln_mix_b, mlp_w1=v_mlp_w1, mlp_w2=v_mlp_w2, ln_mlp_g=v_ln_mlp_g, ln_mlp_b=v_ln_mlp_b, ple_w=v_ple_w, ple_gate_w=v_ple_gate_w, ple_gate_b=v_ple_gate_b)
    names = list(weights)

    depth, d = ln_mix_g.shape
    t = x.shape[1]
    n_a, n_b = lru_w_in.shape[0], pool_w_in.shape[0]
    d_rnn = lru_w_out.shape[1] * N_CHIPS
    d_ff = mlp_w1.shape[2] * N_CHIPS
    ple_dim = ple_w.shape[1]
    n_grp = len(POOL_WINDOWS)
    gw = d // n_grp
    alpha = (2 * depth) ** 0.25
    chip = 2 * lax.axis_index("x") + lax.axis_index("y")
    place = jnp.stack([chip, lax.axis_index("c")]).astype(jnp.int32)

    x2d = x.reshape(t, d)
    target = loss_target.reshape(t, d)
    p3 = p.reshape(depth, t, ple_dim)

    big = ["lru_w_in", "lru_w_out", "pool_w_in", "pool_w_out", "mlp_w1", "mlp_w2", "ple_w", "ple_gate_w", "pool_w_grp"]
    flat2 = lambda a: a.reshape(-1, a.shape[-1])
    small_sharded = ["lru_conv_w", "pool_b_grp", "pool_scale"]
    small_blob = _pack([weights[k] for k in small_sharded], 16, cols=256)
    every_layer = ("mlp_w1", "mlp_w2", "ple_w", "ple_gate_w")

    def layer_keys(i):
        return (["lru_w_in", "lru_w_out"] if i % 2 == 0 else ["pool_w_in", "pool_w_out", "pool_w_grp"]) + list(every_layer)

    def stage(k, i):
        w = weights[k]
        return into_block(f"stage_l{i}_{k}", flat2(w), i if k in every_layer else i // 2, math.prod(w.shape[1:-1]),
                          place[:1], BF16)

    staged = [[stage(k, i) for k in layer_keys(i)] for i in range(depth)]
    first = all_gather_chips("gather_l0", staged[0][:1] + [into_block("stage_small", small_blob, 0, small_blob.shape[0], place[:1], F32)])
    wg = {(layer_keys(0)[0], 0): first[0]}

    tokens = []

    def take_tokens():
        deps = tuple(tokens)
        tokens.clear()
        return deps

    def mm(*args, **kwargs):
        return matmul(*args, deps=take_tokens(), **kwargs)

    def start_gather(tag, bufs, dep):
        plan = gather_plan(len(bufs))
        flight = (plan,) + split_start(f"gather_{tag}_start", plan, 3 * len(bufs), bufs, dep)
        tokens.append(flight[-1])
        return flight

    def land_gather(tag, flight, keys, layer, after, wait_for=True):
        plan, send_sems, recv_sems, bufs, _ = flight
        landed = split_wait(f"gather_{tag}_wait", plan, send_sems, recv_sems, bufs, after)
        if wait_for:
            wg.update(zip([(k, layer) for k in keys], pair_forward(f"gather_{tag}_forward", landed)))
            return None
        plan = forward_plan(len(landed))
        forwarding = (tag, plan) + split_start(f"gather_{tag}_forward_start", plan, 3 * len(landed), landed, after)
        tokens.append(forwarding[-1])
        return forwarding

    def finish_forward(forwarding, keys, layer, after):
        tag, plan, send_sems, recv_sems, bufs, _ = forwarding
        wg.update(zip([(k, layer) for k in keys], split_wait(f"gather_{tag}_forward_wait", plan, send_sems, recv_sems, bufs, after)))

    conv_w_sh, b_grp_sh, scale_sh = _unpack_rows(first[-1].reshape(N_CHIPS, -1), [weights[k].shape for k in small_sharded])
    conv_w_full = jnp.moveaxis(conv_w_sh, 0, 2).reshape(n_a, CONV_WIDTH, d_rnn)
    b_grp_full = jnp.moveaxis(b_grp_sh, 0, 1).reshape(n_b, 1, d)
    scale_full = jnp.moveaxis(scale_sh, 0, 1).reshape(n_b, 1, d)
    rows_grp = gw // N_CHIPS
    w_grp_full = lambda i: jnp.moveaxis(wg["pool_w_grp", i].reshape(N_CHIPS, n_grp, rows_grp, gw), 0, 1).reshape(n_grp, gw, gw)
    wa_bf, wx_bf = lru_wa.astype(BF16), lru_wx.astype(BF16)
    row = lambda a, i: a[i].reshape(1, -1)

    def ln_after(acc, x_in, g, b):
        s = alpha * x_in + acc
        y = _ln_stats(s)[0] * g + b
        return y, y, s

    ln_outs = [plain(shape=(t, d), dtype=F32), plain(shape=(t, d), dtype=BF16), plain(shape=(t, d), dtype=F32)]
    saved = []
    cur, cur_bf = x2d, x2d
    for i in range(depth):
        slot = i // 2
        sv = dict(x_bf=cur_bf)
        if i == 0:
            flight = start_gather("l0_rest", staged[0][1:], first[0])
        elif i + 1 < depth:
            flight = start_gather(f"l{i + 1}", staged[i + 1], cur)
        if i % 2 == 0:
            (proj,) = mm(f"l{i}_lru_in", plain(cur_bf), colsplit(wg["lru_w_in", i], 0, d), "nn",
                         [colsplit(None, 0, t, n=2, full=(2, t, d_rnn), dtype=F32)])
            hg = lru_fwd(f"l{i}_lru", proj, conv_w_full[slot], row(lru_conv_b, slot), wa_bf[slot], row(lru_ba, slot),
                         wx_bf[slot], row(lru_bx, slot), row(lru_lambda, slot))
            if i == 0:
                land_gather("l0_rest", flight, layer_keys(0)[1:], 0, hg)
                flight = start_gather("l1", staged[1], hg)
            x1, x1_bf, s1 = mm(f"l{i}_lru_out", plain(hg), rowsplit_whole(wg["lru_w_out", i]), "nn", ln_outs, pk=2048,
                               epilogue=ln_after, tiles=[plain(cur)], rows=[row(ln_mix_g, i), row(ln_mix_b, i)])
            sv.update(proj=proj, act=hg)
        else:
            (u,) = mm(f"l{i}_pool_in", plain(cur_bf), rowsplit_whole(wg["pool_w_in", i]), "nn",
                          [plain(shape=(t, d), dtype=F32)])
            zs = pool_fwd(f"l{i}_pool", u, w_grp_full(i), b_grp_full[slot], scale_full[slot])
            x1, x1_bf, s1 = mm(f"l{i}_pool_out", plain(zs), rowsplit_whole(wg["pool_w_out", i]), "nn", ln_outs,
                               epilogue=ln_after, tiles=[plain(cur)], rows=[row(ln_mix_g, i), row(ln_mix_b, i)])
            sv.update(u=u, act=zs)

        def relu2(acc):
            hr = jnp.maximum(acc, 0.0)
            return hr, hr * hr

        hr, hh = mm(f"l{i}_mlp_up", plain(x1_bf), colsplit(wg["mlp_w1", i], 0, d), "nn",
                    [plain(shape=(t, d_ff), dtype=BF16), plain(shape=(t, d_ff), dtype=BF16)], epilogue=relu2, pm=2048)
        (mlp,) = mm(f"l{i}_mlp_down", plain(hh), rowsplit_whole(wg["mlp_w2", i]), "nn",
                    [plain(shape=(t, d), dtype=F32)], pk=d_ff)
        x2, x2_bf, s2 = ln_fwd(f"l{i}_ln_mlp", alpha, x1, mlp, row(ln_mlp_g, i), row(ln_mlp_b, i))
        if i + 1 < depth:
            forwarding = land_gather(f"l{i + 1}", flight, layer_keys(i + 1), i + 1, x2_bf, wait_for=False)

        def ple_out(acc, x2_t, gb, p_t, pw):
            e_t = jnp.concatenate([jnp.dot(p_t.astype(BF16), pw[k], preferred_element_type=F32) for k in range(N_CHIPS)], axis=1)
            gate = jax.nn.sigmoid(acc + gb)
            x3 = x2_t + e_t * gate
            return x3, x3, gate, e_t

        cur, cur_bf, gate, e = mm(
            f"l{i}_ple_gate", plain(x2_bf), rowsplit_whole(wg["ple_gate_w", i]), "nn",
            [plain(shape=(t, d), dtype=F32), plain(shape=(t, d), dtype=BF16), plain(shape=(t, d), dtype=F32), plain(shape=(t, d), dtype=F32)],
            epilogue=ple_out, tiles=[plain(x2)], rows=[row(ple_gate_b, i)], pm=512,
            side=[(p3[i], lambda tm, tn: pl.BlockSpec((tm, ple_dim), lambda r, c, kk: (r, 0))),
                  (wg["ple_w", i], lambda tm, tn: pl.BlockSpec(wg["ple_w", i].shape, lambda r, c, kk: (0, 0, 0)))])
        sv.update(s1=s1, x1_bf=x1_bf, hr=hr, hh=hh, s2=s2, x2_bf=x2_bf, gate=gate, e=e)
        saved.append(sv)
        if i + 1 < depth:
            finish_forward(forwarding, layer_keys(i + 1), i + 1, cur)

    dy, loss_part = loss_head("loss", cur, target)
    loss = lax.psum(loss_part.reshape(()), ("x", "y", "c"))

    part = {}
    sums = {}

    def grad_view(key, split):
        w = weights[key]
        return split(None, 0, w.shape[1], full=(N_CHIPS, w.shape[1], w.shape[2]), dtype=BF16)

    def group_start(tag, items, dep):
        srcs = [part[it] for it in items]
        plan = pair_plan(len(srcs))
        lands = [lax.empty(s.shape, s.dtype) for s in pair_lands(srcs)]
        flight = (tag, items, plan) + split_start(f"grads_{tag}_pair_start", plan, len(srcs), srcs + lands, dep)
        tokens.append(flight[-1])
        return flight

    def group_mid(flight, after):
        tag, items, plan, send_sems, recv_sems, bufs, _ = flight
        bufs = split_wait(f"grads_{tag}_pair_wait", plan, send_sems, recv_sems, bufs, after)
        n = len(items)
        parts = [pair_sum(f"grads_{tag}_pair_sum_{j}", bufs[j], bufs[n + j], place[1:], F32 if it[0] == "blob" else BF16)
                 for j, it in enumerate(items)]
        plan = chip_plan(n)
        flight = (tag, items, plan) + split_start(f"grads_{tag}_chip_start", plan, 3 * n,
                                                  parts + [lax.empty(q.shape, q.dtype) for q in parts], after)
        tokens.append(flight[-1])
        return flight

    def group_end(flight, after):
        tag, items, plan, send_sems, recv_sems, bufs, _ = flight
        bufs = split_wait(f"grads_{tag}_chip_wait", plan, send_sems, recv_sems, bufs, after)
        n = len(items)
        for j, (k, layer) in enumerate(items):
            if k == "blob":
                sums[k] = chip_sum(f"grads_{tag}_chip_sum_{j}", bufs[n + j], bufs[j], place, True)
            else:
                sums[k] = chip_sum(f"grads_{tag}_chip_sum_{j}", bufs[n + j], bufs[j], place, False, into=sums.get(k),
                                   layer=layer if k in every_layer else layer // 2, n_layers=weights[k].shape[0])

    big_w = [k for k in big if k != "pool_w_grp"]
    small_keys = [k for k in names if k not in big_w]

    def ln_before(ca):
        def back(acc, upstream, s, g):
            dx = ca * upstream + acc
            xhat, rstd = _ln_stats(s)
            dxh = dx * g
            ds = rstd * (dxh - jnp.mean(dxh, axis=-1, keepdims=True) - xhat * jnp.mean(dxh * xhat, axis=-1, keepdims=True))
            return ds, ds, jnp.sum(dx * xhat, axis=0, keepdims=True), jnp.sum(dx, axis=0, keepdims=True)

        return back

    ds_outs = [plain(shape=(t, d), dtype=F32), plain(shape=(t, d), dtype=BF16)]
    small = {k: [None] * weights[k].shape[0] for k in names if k not in big or k == "pool_w_grp"}
    dcur = dy
    mlp_pair = mlp_chip = mix_pair = mix_chip = None
    for i in reversed(range(depth)):
        slot = i // 2
        sv = saved[i]
        de, dpre, dgb = ple_bwd(f"l{i}_ple_bwd", dcur, sv["gate"], sv["e"])
        small["ple_gate_b"][i] = dgb
        (part["ple_w", i],) = mm(f"l{i}_d_ple_w", plain(p3[i]), plain(de), "tn", [grad_view("ple_w", colsplit)])
        (part["ple_gate_w", i],) = mm(f"l{i}_d_ple_gate_w", plain(sv["x2_bf"]), plain(dpre), "tn",
                                          [grad_view("ple_gate_w", rowsplit)])
        ds2, ds2_bf, dg, db = mm(f"l{i}_dx2", plain(dpre), rowsplit_whole(wg["ple_gate_w", i]), "nt", ds_outs, col_sums=2, pm=512,
                                 epilogue=ln_before(1.0), tiles=[plain(dcur), plain(sv["s2"])], rows=[row(ln_mlp_g, i)])
        small["ln_mlp_g"][i], small["ln_mlp_b"][i] = dg, db
        (part["mlp_w2", i],) = mm(f"l{i}_d_mlp_w2", plain(sv["hh"]), plain(ds2_bf), "tn", [grad_view("mlp_w2", rowsplit)])
        (dhpre,) = mm(f"l{i}_dh", plain(ds2_bf), rowsplit(wg["mlp_w2", i], 0, d_ff // N_CHIPS), "nt",
                      [plain(shape=(t, d_ff), dtype=BF16)], tiles=[plain(sv["hr"])], pm=2048,
                      epilogue=lambda acc, hr_t: (acc * (2.0 * hr_t.astype(F32)),))
        (part["mlp_w1", i],) = mm(f"l{i}_d_mlp_w1", plain(sv["x1_bf"]), plain(dhpre), "tn", [grad_view("mlp_w1", colsplit)])
        if mlp_chip is not None:
            group_end(mlp_chip, dhpre)
        if mix_pair is not None:
            mix_chip = group_mid(mix_pair, dhpre)
        mlp_pair = group_start(f"l{i}_mlp", [(k, i) for k in every_layer], dhpre)
        (dx1b,) = mm(f"l{i}_dx1", plain(dhpre), colsplit(wg["mlp_w1", i], 0, d), "nt", [plain(shape=(t, d), dtype=F32)],
                     pm=2048)
        ds1, ds1_bf, dg, db = ln_bwd(f"l{i}_ln_mix_bwd", alpha, ds2, dx1b, sv["s1"], row(ln_mix_g, i))
        small["ln_mix_g"][i], small["ln_mix_b"][i] = dg, db
        residual = lambda acc, ds_t: (alpha * ds_t + acc,)
        if i % 2 == 0:
            (part["lru_w_out", i],) = mm(f"l{i}_d_lru_out", plain(sv["act"]), plain(ds1_bf), "tn",
                                             [grad_view("lru_w_out", rowsplit)])
            (dhg,) = mm(f"l{i}_dhg", plain(ds1_bf), rowsplit_whole(wg["lru_w_out", i]), "nt",
                            [plain(shape=(t, d_rnn), dtype=F32)], pn=2048)
            mlp_chip = group_mid(mlp_pair, dhg)
            dproj, dcw, dcb, dba, dbx, dlam, dwa, dwx = lru_bwd(
                f"l{i}_lru_bwd", sv["proj"], dhg, conv_w_full[slot], row(lru_conv_b, slot), wa_bf[slot], row(lru_ba, slot),
                wx_bf[slot], row(lru_bx, slot), row(lru_lambda, slot), deps=take_tokens())
            for key, val in (("lru_conv_w", dcw), ("lru_conv_b", dcb), ("lru_ba", dba), ("lru_bx", dbx),
                             ("lru_lambda", dlam), ("lru_wa", dwa), ("lru_wx", dwx)):
                small[key][slot] = val
            dproj_v = colsplit(dproj, 0, t, n=2)
            (part["lru_w_in", i],) = mm(f"l{i}_d_lru_in", plain(sv["x_bf"]), dproj_v, "tn", [grad_view("lru_w_in", colsplit)])
            (dcur,) = mm(f"l{i}_dx", dproj_v, colsplit(wg["lru_w_in", i], 0, d), "nt",
                             [plain(shape=(t, d), dtype=F32)], epilogue=residual, tiles=[plain(ds1)])
        else:
            (part["pool_w_out", i],) = mm(f"l{i}_d_pool_out", plain(sv["act"]), plain(ds1_bf), "tn",
                                              [grad_view("pool_w_out", rowsplit)])
            (dzs,) = mm(f"l{i}_dzs", plain(ds1_bf), rowsplit_whole(wg["pool_w_out", i]), "nt",
                            [plain(shape=(t, d), dtype=F32)])
            mlp_chip = group_mid(mlp_pair, dzs)
            du, dwg, dbg, dsc = pool_bwd(f"l{i}_pool_bwd", sv["u"], dzs, w_grp_full(i), b_grp_full[slot], scale_full[slot],
                                         deps=take_tokens())
            small["pool_w_grp"][slot], small["pool_b_grp"][slot], small["pool_scale"][slot] = dwg, dbg, dsc
            (part["pool_w_in", i],) = mm(f"l{i}_d_pool_in", plain(sv["x_bf"]), plain(du), "tn", [grad_view("pool_w_in", rowsplit)])
            (dcur,) = mm(f"l{i}_dx", plain(du), rowsplit_whole(wg["pool_w_in", i]), "nt",
                             [plain(shape=(t, d), dtype=F32)], epilogue=residual, tiles=[plain(ds1)])
        if mix_chip is not None:
            group_end(mix_chip, dcur)
        mixer = [(k, i) for k in layer_keys(i) if k not in every_layer and k != "pool_w_grp"]
        if i == 0:
            small_full = [jnp.stack(small[k]).reshape((weights[k].shape[0],) + tuple(
                s * (N_CHIPS if ax in _sharded_axis(k) else 1) for ax, s in enumerate(weights[k].shape[1:], 1))) for k in small_keys]
            blob = _pack(small_full, 64)
            part["blob", 0] = blob.reshape(N_CHIPS, blob.shape[0] // N_CHIPS, BLOB_COLS)
            mixer.append(("blob", 0))
        mix_pair = group_start(f"l{i}_mix", mixer, dcur)
    grad_x = dcur.reshape(x.shape)
    mix_chip = group_mid(mix_pair, dcur)
    group_end(mlp_chip, mix_chip[-1])

    grads, delta, new_m, new_v = {}, {}, {}, {}

    def halves_start(tag, keys, dep):
        layers = [1 if k == "blob" else weights[k].shape[0] for k in keys]
        plan = pair_gather_plan([k == "blob" for k in keys], layers)
        return (keys, plan) + split_start(f"grads_pair_gather_{tag}_start", plan, sum(layers), [sums[k] for k in keys], dep)

    def halves_end(tag, flight, after):
        keys, plan, send_sems, recv_sems, bufs, _ = flight
        return dict(zip(keys, split_wait(f"grads_pair_gather_{tag}_wait", plan, send_sems, recv_sems, bufs, after)))

    def update(k, g):
        dl, nm, nv, g = adamw("adamw_" + k, flat2(weights[k]), g, flat2(mom_m[k]), flat2(mom_v[k]))
        delta[k], new_m[k], new_v[k], grads[k] = (a.reshape(weights[k].shape) for a in (dl, nm, nv, g))

    last = [k for k, _ in mix_pair[1]]
    early = [k for k in big_w if k not in last]
    large = [k for k in early if k in ("mlp_w1", "mlp_w2")]
    little = [k for k in early if k not in large]
    little_flight = halves_start("little", little, mix_chip[-1])
    up_flight = halves_start("large0", large[:1], little_flight[-1])
    down_flight = halves_start("large1", large[1:], up_flight[-1])
    whole = halves_end("little", little_flight, down_flight[-1])
    for k in little:
        update(k, whole[k])
    whole.update(halves_end("large0", up_flight, delta[little[-1]]))
    update(large[0], whole[large[0]])
    group_end(mix_chip, delta[large[0]])
    whole.update(zip(last, pair_gather("grads_pair_gather_last", [sums[k] for k in last], [k == "blob" for k in last],
                                       [1 if k == "blob" else weights[k].shape[0] for k in last])))
    plan = spread_plan(1)
    send_sems, recv_sems, spreading, spread_token = split_start("gather_small_grads_start", plan, 3, [whole["blob"]], delta[large[0]])
    whole.update(halves_end("large1", down_flight, spread_token))
    for k in large[1:] + [k for k in last if k != "blob"]:
        update(k, whole[k])
    (blob_all,) = split_wait("gather_small_grads_wait", plan, send_sems, recv_sems, spreading, delta[large[1]])
    small_grads = dict(zip(small_keys, _unpack(blob_all.reshape(blob.shape), [a.shape for a in small_full])))
    for k in small_keys:
        for ax in _sharded_axis(k):
            n = weights[k].shape[ax]
            small_grads[k] = lax.dynamic_slice_in_dim(small_grads[k], chip * n, n, axis=ax)
    grads.update(small_grads)
    dl, nm, nv = adamw_small("adamw_small", *[[flat2(src[k]) for k in small_keys] for src in (weights, grads, mom_m, mom_v)])
    for out, res in ((delta, dl), (new_m, nm), (new_v, nv)):
        out.update({k: a.reshape(weights[k].shape) for k, a in zip(small_keys, res)})

    return (loss, grad_x, *[grads[k] for k in names], *[delta[k] for k in names],
            *[new_m[k] for k in names], *[new_v[k] for k in names])


def _sharded_axis(key):
    return {"lru_conv_w": (2,), "pool_w_grp": (2,), "pool_b_grp": (1,), "pool_scale": (1,)}.get(key, ())
```
